```python
import jax, jax.numpy as jnp
from jax import lax
import numpy as np

D_MODEL = 2048
BATCH = 8
SEQ = 2048
DEPTH = 1

HEAD_DIM = 128
N_HEADS_SB = 8
N_HEADS_FOX = 8
D_SB = N_HEADS_SB * HEAD_DIM
D_FOX = N_HEADS_FOX * HEAD_DIM
D_FF = -(-8 * D_MODEL // (3 * 256)) * 256
Q_BLOCK = 128
RMS_EPS = 1e-6
SPLIT_SIZES = (D_SB, D_SB, D_SB, D_FOX, D_FOX, D_FOX, N_HEADS_FOX, D_MODEL, D_MODEL)
SPLIT_POINTS = (D_SB, 2 * D_SB, 3 * D_SB,
                3 * D_SB + D_FOX, 3 * D_SB + 2 * D_FOX, 3 * D_SB + 3 * D_FOX,
                3 * D_SB + 3 * D_FOX + N_HEADS_FOX,
                3 * D_SB + 3 * D_FOX + N_HEADS_FOX + D_MODEL)
D_IN = 3 * D_SB + 3 * D_FOX + N_HEADS_FOX + 2 * D_MODEL

kernel_name = "hybrid_stickbreak_forgetting_gated_block"


def rms_norm(x, g):
    xf = x.astype(jnp.float32)
    y = xf * lax.rsqrt(jnp.mean(xf * xf, axis=-1, keepdims=True) + RMS_EPS)
    return (y * g.astype(jnp.float32)).astype(x.dtype)


def split_heads(t, n_heads):
    b, s, _ = t.shape
    return t.reshape(b, s, n_heads, HEAD_DIM).transpose(0, 2, 1, 3)


def merge_heads(t):
    b, h, s, d = t.shape
    return t.transpose(0, 2, 1, 3).reshape(b, s, h * d)


def stick_breaking_block(q_blk, k, v, q_start):
    tq, tk = q_blk.shape[2], k.shape[2]
    z = jnp.einsum("bhqd,bhkd->bhqk", q_blk, k).astype(jnp.float32) * (HEAD_DIM ** -0.5)
    t_idx = q_start + jnp.arange(tq)[:, None]
    s_idx = jnp.arange(tk)[None, :]
    mask = s_idx < t_idx
    log_keep = jnp.where(mask, jax.nn.log_sigmoid(-z), 0.0)
    between = lax.cumsum(log_keep, axis=3, reverse=True) - log_keep
    w = jnp.where(mask, jnp.exp(jax.nn.log_sigmoid(z) + between), 0.0)
    return jnp.einsum("bhqk,bhkd->bhqd", w.astype(v.dtype), v)


def forgetting_block(q_blk, k, v, cum_q, cum_k, q_start):
    tq, tk = q_blk.shape[2], k.shape[2]
    logits = jnp.einsum("bhqd,bhkd->bhqk", q_blk, k).astype(jnp.float32) * (HEAD_DIM ** -0.5)
    logits = logits + cum_q[..., :, None] - cum_k[..., None, :]
    t_idx = q_start + jnp.arange(tq)[:, None]
    s_idx = jnp.arange(tk)[None, :]
    logits = jnp.where(s_idx <= t_idx, logits, -jnp.inf)
    p = jax.nn.softmax(logits, axis=-1)
    return jnp.einsum("bhqk,bhkd->bhqd", p.astype(v.dtype), v)


def token_mixer(u, w_in, b_forget, w_branch_sb, w_branch_fox, w_out):
    s = u.shape[1]
    proj = u @ w_in
    q_sb, k_sb, v_sb, q_fx, k_fx, v_fx, f_logit, g_sb, g_fx = jnp.split(proj, SPLIT_POINTS, axis=-1)
    q_sb, k_sb, v_sb = (split_heads(t, N_HEADS_SB) for t in (q_sb, k_sb, v_sb))
    q_fx, k_fx, v_fx = (split_heads(t, N_HEADS_FOX) for t in (q_fx, k_fx, v_fx))
    log_f = jax.nn.log_sigmoid((f_logit + b_forget).astype(jnp.float32))
    cum = lax.cumsum(log_f.transpose(0, 2, 1), axis=2)

    outs_sb, outs_fx = [], []
    for i in range(s // Q_BLOCK):
        q0, q1 = i * Q_BLOCK, (i + 1) * Q_BLOCK
        outs_sb.append(stick_breaking_block(q_sb[:, :, q0:q1], k_sb[:, :, :q1], v_sb[:, :, :q1], q0))
        outs_fx.append(forgetting_block(q_fx[:, :, q0:q1], k_fx[:, :, :q1], v_fx[:, :, :q1],
                                        cum[:, :, q0:q1], cum[:, :, :q1], q0))
    o_sb = merge_heads(jnp.concatenate(outs_sb, axis=2))
    o_fx = merge_heads(jnp.concatenate(outs_fx, axis=2))

    merged = jax.nn.sigmoid(g_sb) * (o_sb @ w_branch_sb) + jax.nn.sigmoid(g_fx) * (o_fx @ w_branch_fox)
    return merged @ w_out


def swiglu(u, w_gate, w_up, w_down):
    return (jax.nn.silu(u @ w_gate) * (u @ w_up)) @ w_down


def _fwd_setup_inputs(seed: int = 0) -> dict:
    key = jax.random.key(seed)
    ks = jax.random.split(key, 14)
    f32 = jnp.float32

    def dense(k, fan_in, fan_out):
        return jax.random.normal(k, (DEPTH, fan_in, fan_out), f32) * (fan_in ** -0.5)

    def gain(k):
        return 1.0 + 0.02 * jax.random.normal(k, (DEPTH, D_MODEL), f32)

    return {
        "x": jax.random.normal(ks[0], (BATCH, SEQ, D_MODEL), f32),
        "norm_mix_pre": gain(ks[1]),
        "norm_mix_post": gain(ks[2]),
        "w_in": dense(ks[3], D_MODEL, D_IN),
        "b_forget": 3.0 + 0.1 * jax.random.normal(ks[4], (DEPTH, N_HEADS_FOX), f32),
        "w_branch_sb": dense(ks[5], D_SB, D_MODEL),
        "w_branch_fox": dense(ks[6], D_FOX, D_MODEL),
        "w_out": dense(ks[7], D_MODEL, D_MODEL),
        "norm_ffn_pre": gain(ks[8]),
        "norm_ffn_post": gain(ks[9]),
        "w_ffn_gate": dense(ks[10], D_MODEL, D_FF),
        "w_ffn_up": dense(ks[11], D_MODEL, D_FF),
        "w_ffn_down": dense(ks[12], D_FF, D_MODEL),
    }


def _fwd_reference(x, norm_mix_pre, norm_mix_post, w_in, b_forget, w_branch_sb, w_branch_fox, w_out,
              norm_ffn_pre, norm_ffn_post, w_ffn_gate, w_ffn_up, w_ffn_down):
    h = x
    for l in range(DEPTH):
        mix = token_mixer(rms_norm(h, norm_mix_pre[l]), w_in[l], b_forget[l],
                          w_branch_sb[l], w_branch_fox[l], w_out[l])
        h = h + rms_norm(mix, norm_mix_post[l])
        ff = swiglu(rms_norm(h, norm_ffn_pre[l]), w_ffn_gate[l], w_ffn_up[l], w_ffn_down[l])
        h = h + rms_norm(ff, norm_ffn_post[l])
    return h


import jax as _jax
import jax.numpy as _jnp

TWIN_FORMAT = 'train_step'
FWD_PARAMS = ['x', 'norm_mix_pre', 'norm_mix_post', 'w_in', 'b_forget', 'w_branch_sb', 'w_branch_fox', 'w_out', 'norm_ffn_pre', 'norm_ffn_post', 'w_ffn_gate', 'w_ffn_up', 'w_ffn_down']
TWIN_WEIGHTS = ['norm_mix_pre', 'norm_mix_post', 'w_in', 'b_forget', 'w_branch_sb', 'w_branch_fox', 'w_out', 'norm_ffn_pre', 'norm_ffn_post', 'w_ffn_gate', 'w_ffn_up', 'w_ffn_down']
TWIN_DIFF_INPUT = 'x'
TWIN_INPUTS = ['x', 'norm_mix_pre', 'norm_mix_post', 'w_in', 'b_forget', 'w_branch_sb', 'w_branch_fox', 'w_out', 'norm_ffn_pre', 'norm_ffn_post', 'w_ffn_gate', 'w_ffn_up', 'w_ffn_down', 'loss_target', 'm_norm_mix_pre', 'm_norm_mix_post', 'm_w_in', 'm_b_forget', 'm_w_branch_sb', 'm_w_branch_fox', 'm_w_out', 'm_norm_ffn_pre', 'm_norm_ffn_post', 'm_w_ffn_gate', 'm_w_ffn_up', 'm_w_ffn_down', 'v_norm_mix_pre', 'v_norm_mix_post', 'v_w_in', 'v_b_forget', 'v_w_branch_sb', 'v_w_branch_fox', 'v_w_out', 'v_norm_ffn_pre', 'v_norm_ffn_post', 'v_w_ffn_gate', 'v_w_ffn_up', 'v_w_ffn_down']
TWIN_OUTPUTS = ['loss', 'grad_x', 'grad_norm_mix_pre', 'grad_norm_mix_post', 'grad_w_in', 'grad_b_forget', 'grad_w_branch_sb', 'grad_w_branch_fox', 'grad_w_out', 'grad_norm_ffn_pre', 'grad_norm_ffn_post', 'grad_w_ffn_gate', 'grad_w_ffn_up', 'grad_w_ffn_down', 'delta_norm_mix_pre', 'delta_norm_mix_post', 'delta_w_in', 'delta_b_forget', 'delta_w_branch_sb', 'delta_w_branch_fox', 'delta_w_out', 'delta_norm_ffn_pre', 'delta_norm_ffn_post', 'delta_w_ffn_gate', 'delta_w_ffn_up', 'delta_w_ffn_down', 'new_m_norm_mix_pre', 'new_m_norm_mix_post', 'new_m_w_in', 'new_m_b_forget', 'new_m_w_branch_sb', 'new_m_w_branch_fox', 'new_m_w_out', 'new_m_norm_ffn_pre', 'new_m_norm_ffn_post', 'new_m_w_ffn_gate', 'new_m_w_ffn_up', 'new_m_w_ffn_down', 'new_v_norm_mix_pre', 'new_v_norm_mix_post', 'new_v_w_in', 'new_v_b_forget', 'new_v_w_branch_sb', 'new_v_w_branch_fox', 'new_v_w_out', 'new_v_norm_ffn_pre', 'new_v_norm_ffn_post', 'new_v_w_ffn_gate', 'new_v_w_ffn_up', 'new_v_w_ffn_down']
TWIN_LEAF_KINDS = {'loss': 'loss', 'grad_x': 'grad_x', 'grad_norm_mix_pre': 'grad_w', 'grad_norm_mix_post': 'grad_w', 'grad_w_in': 'grad_w', 'grad_b_forget': 'grad_w', 'grad_w_branch_sb': 'grad_w', 'grad_w_branch_fox': 'grad_w', 'grad_w_out': 'grad_w', 'grad_norm_ffn_pre': 'grad_w', 'grad_norm_ffn_post': 'grad_w', 'grad_w_ffn_gate': 'grad_w', 'grad_w_ffn_up': 'grad_w', 'grad_w_ffn_down': 'grad_w', 'delta_norm_mix_pre': 'delta_w', 'delta_norm_mix_post': 'delta_w', 'delta_w_in': 'delta_w', 'delta_b_forget': 'delta_w', 'delta_w_branch_sb': 'delta_w', 'delta_w_branch_fox': 'delta_w', 'delta_w_out': 'delta_w', 'delta_norm_ffn_pre': 'delta_w', 'delta_norm_ffn_post': 'delta_w', 'delta_w_ffn_gate': 'delta_w', 'delta_w_ffn_up': 'delta_w', 'delta_w_ffn_down': 'delta_w', 'new_m_norm_mix_pre': 'new_m', 'new_m_norm_mix_post': 'new_m', 'new_m_w_in': 'new_m', 'new_m_b_forget': 'new_m', 'new_m_w_branch_sb': 'new_m', 'new_m_w_branch_fox': 'new_m', 'new_m_w_out': 'new_m', 'new_m_norm_ffn_pre': 'new_m', 'new_m_norm_ffn_post': 'new_m', 'new_m_w_ffn_gate': 'new_m', 'new_m_w_ffn_up': 'new_m', 'new_m_w_ffn_down': 'new_m', 'new_v_norm_mix_pre': 'new_v', 'new_v_norm_mix_post': 'new_v', 'new_v_w_in': 'new_v', 'new_v_b_forget': 'new_v', 'new_v_w_branch_sb': 'new_v', 'new_v_w_branch_fox': 'new_v', 'new_v_w_out': 'new_v', 'new_v_norm_ffn_pre': 'new_v', 'new_v_norm_ffn_post': 'new_v', 'new_v_w_ffn_gate': 'new_v', 'new_v_w_ffn_up': 'new_v', 'new_v_w_ffn_down': 'new_v'}


def _forward(args):
    return _fwd_reference(*[args[k] for k in FWD_PARAMS])


def _output_shape():
    out = _jax.eval_shape(lambda: _forward(_fwd_setup_inputs(0)))
    return out.shape, out.dtype

N_MICROBATCH = 1
ADAM_LR = 0.001
ADAM_B1 = 0.9
ADAM_B2 = 0.999
ADAM_EPS = 1e-08
ADAM_WD = 0.01
ADAM_STEP = 10
PER_EXAMPLE_BATCH_AXIS = {'x': 0, 'loss_target': 0}
SHARED_INPUTS = []
_WEIGHT_DTYPES = {'norm_mix_pre': _jnp.float32, 'norm_mix_post': _jnp.float32, 'w_in': _jnp.float32, 'b_forget': _jnp.float32, 'w_branch_sb': _jnp.float32, 'w_branch_fox': _jnp.float32, 'w_out': _jnp.float32, 'norm_ffn_pre': _jnp.float32, 'norm_ffn_post': _jnp.float32, 'w_ffn_gate': _jnp.float32, 'w_ffn_up': _jnp.float32, 'w_ffn_down': _jnp.float32}
MOMENT_SCALE = {'norm_mix_pre': 2.138483e-01, 'norm_mix_post': 8.024681e+00, 'w_in': 9.344925e-02, 'b_forget': 4.924917e-01, 'w_branch_sb': 1.515454e-01, 'w_branch_fox': 6.821900e-02, 'w_out': 1.697844e-01, 'norm_ffn_pre': 1.705905e-01, 'norm_ffn_post': 7.986423e+00, 'w_ffn_gate': 6.805894e-02, 'w_ffn_up': 8.039049e-02, 'w_ffn_down': 1.330867e-01}


def _to_microbatches(a, axis):
    t = _jnp.moveaxis(a, axis, 0)
    t = t.reshape((N_MICROBATCH, t.shape[0] // N_MICROBATCH) + t.shape[1:])
    return _jnp.moveaxis(t, 1, axis + 1)


def setup_inputs(seed: int = 0) -> dict:
    inp = _fwd_setup_inputs(seed)
    key = _jax.random.fold_in(_jax.random.key(seed), 7919)
    shape, _ = _output_shape()
    out = dict(inp)
    out["loss_target"] = _jax.random.normal(_jax.random.fold_in(key, 0), shape, _jnp.float32)
    for i, name in enumerate(TWIN_WEIGHTS):
        w = inp[name].astype(_jnp.float32)
        if MOMENT_SCALE is None:
            s = _jnp.sqrt(_jnp.mean(_jnp.square(w)) + 1e-30)
        else:
            s = MOMENT_SCALE[name]
        km, kv = _jax.random.split(_jax.random.fold_in(key, i + 1))
        out[name] = w
        out["m_" + name] = s * _jax.random.normal(km, w.shape, _jnp.float32)
        out["v_" + name] = (s * s) * _jax.random.uniform(kv, w.shape, _jnp.float32, 0.5, 1.5)
    if N_MICROBATCH > 1:
        for name, axis in PER_EXAMPLE_BATCH_AXIS.items():
            out[name] = _to_microbatches(out[name], axis)
    return {'x': out['x'], 'norm_mix_pre': out['norm_mix_pre'], 'norm_mix_post': out['norm_mix_post'], 'w_in': out['w_in'], 'b_forget': out['b_forget'], 'w_branch_sb': out['w_branch_sb'], 'w_branch_fox': out['w_branch_fox'], 'w_out': out['w_out'], 'norm_ffn_pre': out['norm_ffn_pre'], 'norm_ffn_post': out['norm_ffn_post'], 'w_ffn_gate': out['w_ffn_gate'], 'w_ffn_up': out['w_ffn_up'], 'w_ffn_down': out['w_ffn_down'], 'loss_target': out['loss_target'], 'm_norm_mix_pre': out['m_norm_mix_pre'], 'm_norm_mix_post': out['m_norm_mix_post'], 'm_w_in': out['m_w_in'], 'm_b_forget': out['m_b_forget'], 'm_w_branch_sb': out['m_w_branch_sb'], 'm_w_branch_fox': out['m_w_branch_fox'], 'm_w_out': out['m_w_out'], 'm_norm_ffn_pre': out['m_norm_ffn_pre'], 'm_norm_ffn_post': out['m_norm_ffn_post'], 'm_w_ffn_gate': out['m_w_ffn_gate'], 'm_w_ffn_up': out['m_w_ffn_up'], 'm_w_ffn_down': out['m_w_ffn_down'], 'v_norm_mix_pre': out['v_norm_mix_pre'], 'v_norm_mix_post': out['v_norm_mix_post'], 'v_w_in': out['v_w_in'], 'v_b_forget': out['v_b_forget'], 'v_w_branch_sb': out['v_w_branch_sb'], 'v_w_branch_fox': out['v_w_branch_fox'], 'v_w_out': out['v_w_out'], 'v_norm_ffn_pre': out['v_norm_ffn_pre'], 'v_norm_ffn_post': out['v_norm_ffn_post'], 'v_w_ffn_gate': out['v_w_ffn_gate'], 'v_w_ffn_up': out['v_w_ffn_up'], 'v_w_ffn_down': out['v_w_ffn_down']}


def _loss(weights, diff, rest, loss_target):
    with _jax.named_scope("forward"):
        args = {**rest, TWIN_DIFF_INPUT: diff, **{k: w.astype(_WEIGHT_DTYPES[k]) for k, w in weights.items()}}
        y = _forward(args)
    with _jax.named_scope("loss_head"):
        err = _jnp.square(y.astype(_jnp.float32) - loss_target)
        return 0.5 * _jnp.sum(_jnp.mean(err, axis=-1)) if err.ndim else 0.5 * err


def _adamw(w, g, m, v):
    m = ADAM_B1 * m + (1.0 - ADAM_B1) * g
    v = ADAM_B2 * v + (1.0 - ADAM_B2) * _jnp.square(g)
    m_hat = m / (1.0 - ADAM_B1 ** ADAM_STEP)
    v_hat = v / (1.0 - ADAM_B2 ** ADAM_STEP)
    delta = -ADAM_LR * (m_hat / (_jnp.sqrt(v_hat) + ADAM_EPS) + ADAM_WD * w)
    return delta, m, v


def reference(x, norm_mix_pre, norm_mix_post, w_in, b_forget, w_branch_sb, w_branch_fox, w_out, norm_ffn_pre, norm_ffn_post, w_ffn_gate, w_ffn_up, w_ffn_down, loss_target, m_norm_mix_pre, m_norm_mix_post, m_w_in, m_b_forget, m_w_branch_sb, m_w_branch_fox, m_w_out, m_norm_ffn_pre, m_norm_ffn_post, m_w_ffn_gate, m_w_ffn_up, m_w_ffn_down, v_norm_mix_pre, v_norm_mix_post, v_w_in, v_b_forget, v_w_branch_sb, v_w_branch_fox, v_w_out, v_norm_ffn_pre, v_norm_ffn_post, v_w_ffn_gate, v_w_ffn_up, v_w_ffn_down):
    given = dict(x=x, norm_mix_pre=norm_mix_pre, norm_mix_post=norm_mix_post, w_in=w_in, b_forget=b_forget, w_branch_sb=w_branch_sb, w_branch_fox=w_branch_fox, w_out=w_out, norm_ffn_pre=norm_ffn_pre, norm_ffn_post=norm_ffn_post, w_ffn_gate=w_ffn_gate, w_ffn_up=w_ffn_up, w_ffn_down=w_ffn_down, loss_target=loss_target, m_norm_mix_pre=m_norm_mix_pre, m_norm_mix_post=m_norm_mix_post, m_w_in=m_w_in, m_b_forget=m_b_forget, m_w_branch_sb=m_w_branch_sb, m_w_branch_fox=m_w_branch_fox, m_w_out=m_w_out, m_norm_ffn_pre=m_norm_ffn_pre, m_norm_ffn_post=m_norm_ffn_post, m_w_ffn_gate=m_w_ffn_gate, m_w_ffn_up=m_w_ffn_up, m_w_ffn_down=m_w_ffn_down, v_norm_mix_pre=v_norm_mix_pre, v_norm_mix_post=v_norm_mix_post, v_w_in=v_w_in, v_b_forget=v_b_forget, v_w_branch_sb=v_w_branch_sb, v_w_branch_fox=v_w_branch_fox, v_w_out=v_w_out, v_norm_ffn_pre=v_norm_ffn_pre, v_norm_ffn_post=v_norm_ffn_post, v_w_ffn_gate=v_w_ffn_gate, v_w_ffn_up=v_w_ffn_up, v_w_ffn_down=v_w_ffn_down)
    weights = {n: given[n] for n in TWIN_WEIGHTS}
    shared = {n: given[n] for n in SHARED_INPUTS}
    per_example = {n: given[n] for n in ['x']}
    grad_fn = _jax.value_and_grad(_loss, argnums=(0, 1))

    def one_microbatch(ex, loss_target):
        ex = dict(ex)
        diff = ex.pop(TWIN_DIFF_INPUT)
        return grad_fn(weights, diff, {**shared, **ex}, loss_target)

    if N_MICROBATCH == 1:
        loss, (grad_w, grad_x) = one_microbatch(per_example, given["loss_target"])
    else:
        def body(carry, xs):
            loss_sum, grad_sum = carry
            l_k, (gw_k, gx_k) = one_microbatch(xs[0], xs[1])
            with _jax.named_scope("update"):
                return (loss_sum + l_k, _jax.tree.map(_jnp.add, grad_sum, gw_k)), gx_k

        init = (_jnp.zeros((), _jnp.float32), _jax.tree.map(_jnp.zeros_like, weights))
        (loss, grad_w), grad_x = _jax.lax.scan(body, init, (per_example, given["loss_target"]))
    with _jax.named_scope("update"):
        delta_w, new_m, new_v = {}, {}, {}
        for n in TWIN_WEIGHTS:
            delta_w[n], new_m[n], new_v[n] = _adamw(weights[n], grad_w[n], given["m_" + n], given["v_" + n])
    return (loss, grad_x, *[grad_w[n] for n in TWIN_WEIGHTS], *[delta_w[n] for n in TWIN_WEIGHTS],
            *[new_m[n] for n in TWIN_WEIGHTS], *[new_v[n] for n in TWIN_WEIGHTS])
```

```python
import functools

import jax
import jax.numpy as jnp
from jax import lax
from jax.experimental import pallas as pl
from jax.experimental.pallas import tpu as pltpu

F32 = jnp.float32
BF16 = jnp.bfloat16
MESH = pl.DeviceIdType.MESH

HEAD_DIM = 128
LANES = 128
ATT_TILE = 256
ROW_TILE = 256
FLAT_W = 1024
FLAT_ROW_ALIGN = 512
N_CHIPS = 4
AG_CHUNKS = 4
RMS_EPS = 1e-6
ADAM_LR = 0.001
ADAM_B1 = 0.9
ADAM_B2 = 0.999
ADAM_EPS = 1e-08
ADAM_WD = 0.01
ADAM_STEP = 10
NEG_BIG = -1e30
VMEM_LIMIT = 56 * 1024 * 1024

NN = (((1,), (0,)), ((), ()))
NT = (((1,), (1,)), ((), ()))
TN = (((0,), (0,)), ((), ()))


def _tile(n, pref, align):
    best = None
    t = align
    while t <= min(n, pref):
        if n % t == 0:
            best = t
        t += align
    return n if best is None else best


def _params(*sem):
    return pltpu.CompilerParams(dimension_semantics=sem, vmem_limit_bytes=VMEM_LIMIT)


def mm(a, b, mode, out_dtype, name, b_col0=0, n=None, tm=1024, tn=512, tk=1024):
    if mode == "nn":
        m, k = a.shape
        nb = b.shape[1]
    elif mode == "nt":
        m, k = a.shape
        nb = b.shape[0]
    else:
        k, m = a.shape
        nb = b.shape[1]
    n = nb if n is None else n
    tm = _tile(m, tm, LANES)
    tn = _tile(n, tn, LANES)
    tk = _tile(k, tk, LANES)
    assert b_col0 % tn == 0 and (mode == "nn" or b_col0 == 0)
    j0 = b_col0 // tn
    nk = k // tk
    dims = {"nn": NN, "nt": NT, "tn": TN}[mode]

    def kern(a_ref, b_ref, o_ref, acc_ref):
        kk = pl.program_id(2)

        @pl.when(kk == 0)
        def _():
            acc_ref[...] = jnp.zeros_like(acc_ref)

        acc_ref[...] += lax.dot_general(a_ref[...].astype(BF16), b_ref[...].astype(BF16), dims,
                                        preferred_element_type=F32)

        @pl.when(kk == nk - 1)
        def _():
            o_ref[...] = acc_ref[...].astype(o_ref.dtype)

    if mode == "nn":
        a_spec = pl.BlockSpec((tm, tk), lambda i, j, kk: (i, kk))
        b_spec = pl.BlockSpec((tk, tn), lambda i, j, kk: (kk, j + j0))
    elif mode == "nt":
        a_spec = pl.BlockSpec((tm, tk), lambda i, j, kk: (i, kk))
        b_spec = pl.BlockSpec((tn, tk), lambda i, j, kk: (j, kk))
    else:
        a_spec = pl.BlockSpec((tk, tm), lambda i, j, kk: (kk, i))
        b_spec = pl.BlockSpec((tk, tn), lambda i, j, kk: (kk, j))
    return pl.pallas_call(
        kern, name=name, grid=(m // tm, n // tn, nk),
        in_specs=[a_spec, b_spec],
        out_specs=pl.BlockSpec((tm, tn), lambda i, j, kk: (i, j)),
        out_shape=jax.ShapeDtypeStruct((m, n), out_dtype),
        scratch_shapes=[pltpu.VMEM((tm, tn), F32)],
        compiler_params=_params("parallel", "parallel", "arbitrary"),
    )(a, b)


def _rstd(v):
    return lax.rsqrt(jnp.mean(v * v, axis=-1, keepdims=True) + RMS_EPS)


def _norm_bwd(v, g, dy):
    r = _rstd(v)
    vh = v * r
    dyg = dy * g
    dv = r * (dyg - vh * jnp.mean(dyg * vh, axis=-1, keepdims=True))
    return dv, jnp.sum(dy * vh, axis=0, keepdims=True)


def _row_call(kern, name, ins, outs, s, d):
    tr = _tile(s, ROW_TILE, 16)

    def spec(shape, is_row):
        if is_row:
            return pl.BlockSpec((tr, shape[1]), lambda i: (i, 0))
        return pl.BlockSpec(shape, lambda i: (0, 0))

    return pl.pallas_call(
        kern, name=name, grid=(s // tr,),
        in_specs=[spec(a.shape, r) for a, r in ins],
        out_specs=[spec(sh, r) for sh, _, r in outs],
        out_shape=[jax.ShapeDtypeStruct(sh, dt) for sh, dt, _ in outs],
        compiler_params=_params("arbitrary"),
    )(*[a for a, _ in ins])


def norm_in(x, g):
    s, d = x.shape

    def kern(x_ref, g_ref, u_ref):
        v = x_ref[...]
        u_ref[...] = (v * _rstd(v) * g_ref[...]).astype(BF16)

    return _row_call(kern, "norm_in", [(x, True), (g, False)], [((s, d), BF16, True)], s, d)[0]


def mid_fwd(x, mix, g_post, g_fpre):
    s, d = x.shape

    def kern(x_ref, mix_ref, gp_ref, gf_ref, h1_ref, u2_ref):
        mixv = mix_ref[...]
        h1 = x_ref[...] + mixv * _rstd(mixv) * gp_ref[...]
        h1_ref[...] = h1
        u2_ref[...] = (h1 * _rstd(h1) * gf_ref[...]).astype(BF16)

    return _row_call(kern, "mid_fwd", [(x, True), (mix, True), (g_post, False), (g_fpre, False)],
                     [((s, d), F32, True), ((s, d), BF16, True)], s, d)


def loss_head(h1, ff, g_fpost, target):
    s, d = h1.shape

    def kern(h1_ref, ff_ref, g_ref, t_ref, dy_ref, dff_ref, dg_ref, loss_ref):
        @pl.when(pl.program_id(0) == 0)
        def _():
            dg_ref[...] = jnp.zeros_like(dg_ref)
            loss_ref[...] = jnp.zeros_like(loss_ref)

        ffv = ff_ref[...]
        g = g_ref[...]
        y = h1_ref[...] + ffv * _rstd(ffv) * g
        diff = y - t_ref[...]
        row_loss = jnp.mean(diff * diff, axis=-1, keepdims=True)
        loss_ref[...] += 0.5 * jnp.sum(row_loss, axis=0, keepdims=True)
        dy = diff / d
        dy_ref[...] = dy
        dff, dg = _norm_bwd(ffv, g, dy)
        dff_ref[...] = dff.astype(BF16)
        dg_ref[...] += dg

    return _row_call(kern, "loss_head",
                     [(h1, True), (ff, True), (g_fpost, False), (target, True)],
                     [((s, d), F32, True), ((s, d), BF16, True), ((1, d), F32, False), ((1, 1), F32, False)], s, d)


def mid_bwd(dy, du2, h1, mix, g_fpre, g_post):
    s, d = dy.shape

    def kern(dy_ref, du2_ref, h1_ref, mix_ref, gf_ref, gp_ref, dh1_ref, dmix_ref, dgf_ref, dgp_ref):
        @pl.when(pl.program_id(0) == 0)
        def _():
            dgf_ref[...] = jnp.zeros_like(dgf_ref)
            dgp_ref[...] = jnp.zeros_like(dgp_ref)

        dh, dgf = _norm_bwd(h1_ref[...], gf_ref[...], du2_ref[...])
        dh1 = dy_ref[...] + dh
        dh1_ref[...] = dh1
        dmix, dgp = _norm_bwd(mix_ref[...], gp_ref[...], dh1)
        dmix_ref[...] = dmix.astype(BF16)
        dgf_ref[...] += dgf
        dgp_ref[...] += dgp

    return _row_call(kern, "mid_bwd",
                     [(dy, True), (du2, True), (h1, True), (mix, True), (g_fpre, False), (g_post, False)],
                     [((s, d), F32, True), ((s, d), BF16, True), ((1, d), F32, False), ((1, d), F32, False)], s, d)


def in_bwd(dh1, du, x, g_pre):
    s, d = x.shape

    def kern(dh1_ref, du_ref, x_ref, g_ref, dx_ref, dg_ref):
        @pl.when(pl.program_id(0) == 0)
        def _():
            dg_ref[...] = jnp.zeros_like(dg_ref)

        dxn, dg = _norm_bwd(x_ref[...], g_ref[...], du_ref[...])
        dx_ref[...] = dh1_ref[...] + dxn
        dg_ref[...] += dg

    return _row_call(kern, "in_bwd", [(dh1, True), (du, True), (x, True), (g_pre, False)],
                     [((s, d), F32, True), ((1, d), F32, False)], s, d)


def _sigmoid(v):
    return 1.0 / (1.0 + jnp.exp(-v))


def gate_fwd(bsb, bfx, gf):
    s, d = bsb.shape
    tr, tc = _tile(s, 256, 16), _tile(d, 512, LANES)
    nc = d // tc

    def kern(bsb_ref, bfx_ref, gs_ref, gx_ref, o_ref):
        o_ref[...] = (_sigmoid(gs_ref[...]) * bsb_ref[...] + _sigmoid(gx_ref[...]) * bfx_ref[...]).astype(BF16)

    blk = pl.BlockSpec((tr, tc), lambda i, j: (i, j))
    return pl.pallas_call(
        kern, name="gate_fwd", grid=(s // tr, nc),
        in_specs=[blk, blk, blk, pl.BlockSpec((tr, tc), lambda i, j: (i, j + nc))],
        out_specs=blk, out_shape=jax.ShapeDtypeStruct((s, d), BF16),
        compiler_params=_params("parallel", "parallel"),
    )(bsb, bfx, gf, gf)


def gate_bwd(dmerged, bsb, bfx, gf):
    s, d = bsb.shape
    tr, tc = _tile(s, 256, 16), _tile(d, 512, LANES)
    nc = d // tc

    def kern(dm_ref, bsb_ref, bfx_ref, gs_ref, gx_ref, dbs_ref, dbx_ref, dgs_ref, dgx_ref):
        dm = dm_ref[...]
        ss = _sigmoid(gs_ref[...])
        sx = _sigmoid(gx_ref[...])
        dbs_ref[...] = (dm * ss).astype(BF16)
        dbx_ref[...] = (dm * sx).astype(BF16)
        dgs_ref[...] = (dm * bsb_ref[...] * ss * (1.0 - ss)).astype(BF16)
        dgx_ref[...] = (dm * bfx_ref[...] * sx * (1.0 - sx)).astype(BF16)

    blk = pl.BlockSpec((tr, tc), lambda i, j: (i, j))
    out = jax.ShapeDtypeStruct((s, d), BF16)
    return pl.pallas_call(
        kern, name="gate_bwd", grid=(s // tr, nc),
        in_specs=[blk, blk, blk, blk, pl.BlockSpec((tr, tc), lambda i, j: (i, j + nc))],
        out_specs=[blk, blk, blk, blk], out_shape=[out, out, out, out],
        compiler_params=_params("parallel", "parallel"),
    )(dmerged, bsb, bfx, gf, gf)


def swiglu_fwd(gu):
    s, f2 = gu.shape
    f = f2 // 2
    tr, tc = _tile(s, 256, 16), _tile(f, 512, LANES)
    nc = f // tc

    def kern(g_ref, u_ref, o_ref):
        g = g_ref[...]
        o_ref[...] = (g * _sigmoid(g) * u_ref[...]).astype(BF16)

    return pl.pallas_call(
        kern, name="swiglu_fwd", grid=(s // tr, nc),
        in_specs=[pl.BlockSpec((tr, tc), lambda i, j: (i, j)), pl.BlockSpec((tr, tc), lambda i, j: (i, j + nc))],
        out_specs=pl.BlockSpec((tr, tc), lambda i, j: (i, j)),
        out_shape=jax.ShapeDtypeStruct((s, f), BF16),
        compiler_params=_params("parallel", "parallel"),
    )(gu, gu)


def swiglu_bwd(dact, gu):
    s, f2 = gu.shape
    f = f2 // 2
    tr, tc = _tile(s, 256, 16), _tile(f, 512, LANES)
    nc = f // tc

    def kern(da_ref, g_ref, u_ref, o_ref):
        j = pl.program_id(1)
        da = da_ref[...]
        g = g_ref[...]
        sg = _sigmoid(g)
        d_gate = da * u_ref[...] * (sg * (1.0 + g * (1.0 - sg)))
        d_up = da * (g * sg)
        o_ref[...] = jnp.where(j < nc, d_gate, d_up).astype(BF16)

    return pl.pallas_call(
        kern, name="swiglu_bwd", grid=(s // tr, 2 * nc),
        in_specs=[pl.BlockSpec((tr, tc), lambda i, j: (i, j % nc)),
                  pl.BlockSpec((tr, tc), lambda i, j: (i, j % nc)),
                  pl.BlockSpec((tr, tc), lambda i, j: (i, j % nc + nc))],
        out_specs=pl.BlockSpec((tr, tc), lambda i, j: (i, j)),
        out_shape=jax.ShapeDtypeStruct((s, f2), BF16),
        compiler_params=_params("parallel", "parallel"),
    )(dact, gu, gu)


def _split3(v):
    hi = v.astype(BF16)
    r = v - hi.astype(F32)
    mid = r.astype(BF16)
    lo = (r - mid.astype(F32)).astype(BF16)
    return hi, mid, lo


def _dot3_right(v, ones):
    hi, mid, lo = _split3(v)
    d = lambda p: jnp.dot(p, ones, preferred_element_type=F32)
    return (d(lo) + d(mid)) + d(hi)


def _dot3_left(ones, v):
    hi, mid, lo = _split3(v)
    d = lambda p: jnp.dot(ones, p, preferred_element_type=F32)
    return (d(lo) + d(mid)) + d(hi)


def _log1p_exp_neg_abs(v):
    return jnp.log1p(jnp.exp(-jnp.abs(v)))


def _mask01(cond):
    return jnp.where(cond, 1.0, 0.0).astype(BF16)


def _iota2(t):
    return (lax.broadcasted_iota(jnp.int32, (t, t), 0), lax.broadcasted_iota(jnp.int32, (t, t), 1))


def cum_fwd(gf, b_pad, f_col0):
    s = gf.shape[0]
    t = _tile(s, ATT_TILE, LANES)
    fb = f_col0 // LANES

    def kern(f_ref, b_ref, cum_ref, carry_ref):
        @pl.when(pl.program_id(0) == 0)
        def _():
            carry_ref[...] = jnp.zeros_like(carry_ref)

        v = f_ref[...] + b_ref[...]
        lf = jnp.minimum(v, 0.0) - _log1p_exp_neg_abs(v)
        row, col = _iota2(t)
        cum = _dot3_left(_mask01(col <= row), lf) + carry_ref[...]
        cum_ref[...] = cum
        carry_ref[...] = cum[t - 1:t, :]

    return pl.pallas_call(
        kern, name="cum_fwd", grid=(s // t,),
        in_specs=[pl.BlockSpec((t, LANES), lambda i: (i, fb)), pl.BlockSpec((1, LANES), lambda i: (0, 0))],
        out_specs=pl.BlockSpec((t, LANES), lambda i: (i, 0)),
        out_shape=jax.ShapeDtypeStruct((s, LANES), F32),
        scratch_shapes=[pltpu.VMEM((1, LANES), F32)],
        compiler_params=_params("arbitrary"),
    )(gf, b_pad)


def cum_bwd(dcum, gf, b_pad, f_col0, n_heads):
    s = gf.shape[0]
    t = _tile(s, ATT_TILE, LANES)
    nb = s // t
    fb = f_col0 // LANES

    def kern(dc_ref, f_ref, b_ref, df_ref, db_ref, carry_ref):
        @pl.when(pl.program_id(0) == 0)
        def _():
            carry_ref[...] = jnp.zeros_like(carry_ref)
            db_ref[...] = jnp.zeros_like(db_ref)

        row, col = _iota2(t)
        dlf = _dot3_left(_mask01(col >= row), dc_ref[...]) + carry_ref[...]
        carry_ref[...] = dlf[0:1, :]
        v = f_ref[...] + b_ref[...]
        sig_neg = jnp.exp(-jnp.maximum(v, 0.0) - _log1p_exp_neg_abs(v))
        lane = lax.broadcasted_iota(jnp.int32, (t, LANES), 1)
        df = jnp.where(lane < n_heads, dlf * sig_neg, 0.0)
        df_ref[...] = df.astype(BF16)
        db_ref[...] += jnp.sum(df, axis=0, keepdims=True)

    return pl.pallas_call(
        kern, name="cum_bwd", grid=(nb,),
        in_specs=[pl.BlockSpec((t, LANES), lambda i: (nb - 1 - i, 0)),
                  pl.BlockSpec((t, LANES), lambda i: (nb - 1 - i, fb)),
                  pl.BlockSpec((1, LANES), lambda i: (0, 0))],
        out_specs=[pl.BlockSpec((t, LANES), lambda i: (nb - 1 - i, 0)), pl.BlockSpec((1, LANES), lambda i: (0, 0))],
        out_shape=[jax.ShapeDtypeStruct((s, LANES), BF16), jax.ShapeDtypeStruct((1, LANES), F32)],
        scratch_shapes=[pltpu.VMEM((1, LANES), F32)],
        compiler_params=_params("arbitrary"),
    )(dcum, gf, b_pad)


def _qkv_specs(s, t, n_heads, base):
    return [pl.BlockSpec((t, HEAD_DIM), lambda h, i: (i, base + h)),
            pl.BlockSpec((s, HEAD_DIM), lambda h, i: (0, base + n_heads + h)),
            pl.BlockSpec((s, HEAD_DIM), lambda h, i: (0, base + 2 * n_heads + h))]


def _sb_scores(q, k, j, i, t, strict_lim):
    row, col = _iota2(t)
    z = lax.dot_general(q, k, NT, preferred_element_type=F32) * (HEAD_DIM ** -0.5)
    valid = (col - row) < jnp.where(j < i, t, strict_lim)
    l1p = _log1p_exp_neg_abs(z)
    log_keep = jnp.where(valid, -jnp.maximum(z, 0.0) - l1p, 0.0)
    return z, valid, l1p, log_keep


def sb_fwd(qkv, n_heads, base):
    s = qkv.shape[0]
    t = _tile(s, ATT_TILE, LANES)

    def kern(q_ref, k_ref, v_ref, o_ref):
        i = pl.program_id(1)
        row, col = _iota2(t)
        after = _mask01(row > col)
        q = q_ref[...]

        def body(jj, carry):
            run, acc = carry
            j = i - jj
            off = pl.multiple_of(j * t, t)
            z, valid, l1p, log_keep = _sb_scores(q, k_ref[pl.ds(off, t), :], j, i, t, 0)
            between = _dot3_right(log_keep, after) + run
            w = jnp.where(valid, jnp.exp(jnp.minimum(z, 0.0) - l1p + between), 0.0)
            acc = acc + jnp.dot(w.astype(BF16), v_ref[pl.ds(off, t), :], preferred_element_type=F32)
            return run + jnp.sum(log_keep, axis=1, keepdims=True), acc

        _, acc = lax.fori_loop(0, i + 1, body, (jnp.zeros((t, 1), F32), jnp.zeros((t, HEAD_DIM), F32)))
        o_ref[...] = acc.astype(o_ref.dtype)

    return pl.pallas_call(
        kern, name="sb_fwd", grid=(n_heads, s // t),
        in_specs=_qkv_specs(s, t, n_heads, base),
        out_specs=pl.BlockSpec((t, HEAD_DIM), lambda h, i: (i, h)),
        out_shape=jax.ShapeDtypeStruct((s, n_heads * HEAD_DIM), BF16),
        compiler_params=_params("parallel", "arbitrary"),
    )(qkv, qkv, qkv)


def sb_bwd(qkv, d_o, n_heads, base):
    s = qkv.shape[0]
    t = _tile(s, ATT_TILE, LANES)
    nq = s // t
    scale = HEAD_DIM ** -0.5

    def kern(q_ref, k_ref, v_ref, do_ref, dq_ref, dk_ref, dv_ref, dk_acc, dv_acc, run_ref):
        i = pl.program_id(1)

        @pl.when(i == 0)
        def _():
            dk_acc[...] = jnp.zeros_like(dk_acc)
            dv_acc[...] = jnp.zeros_like(dv_acc)

        row, col = _iota2(t)
        after = _mask01(row > col)
        before = _mask01(row < col)
        q = q_ref[...]
        do = do_ref[...]

        def sweep1(jj, run):
            j = i - jj
            off = pl.multiple_of(j * t, t)
            _, _, _, log_keep = _sb_scores(q, k_ref[pl.ds(off, t), :], j, i, t, 0)
            run_ref[j] = run
            return run + jnp.sum(log_keep, axis=1, keepdims=True)

        lax.fori_loop(0, i + 1, sweep1, jnp.zeros((t, 1), F32))

        def sweep2(j, carry):
            run_e, dq = carry
            off = pl.multiple_of(j * t, t)
            k = k_ref[pl.ds(off, t), :]
            v = v_ref[pl.ds(off, t), :]
            z, valid, l1p, log_keep = _sb_scores(q, k, j, i, t, 0)
            between = _dot3_right(log_keep, after) + run_ref[j]
            w = jnp.where(valid, jnp.exp(jnp.minimum(z, 0.0) - l1p + between), 0.0)
            dw = lax.dot_general(do, v, NT, preferred_element_type=F32)
            e = dw * w
            e_before = _dot3_right(e, before) + run_e
            keep = jnp.exp(log_keep)
            dz = jnp.where(valid, e * keep - e_before * (1.0 - keep), 0.0) * scale
            dzb = dz.astype(BF16)
            dq = dq + jnp.dot(dzb, k, preferred_element_type=F32)
            dk_acc[pl.ds(off, t), :] += lax.dot_general(dzb, q, TN, preferred_element_type=F32)
            dv_acc[pl.ds(off, t), :] += lax.dot_general(w.astype(BF16), do, TN, preferred_element_type=F32)
            return run_e + jnp.sum(e, axis=1, keepdims=True), dq

        _, dq = lax.fori_loop(0, i + 1, sweep2, (jnp.zeros((t, 1), F32), jnp.zeros((t, HEAD_DIM), F32)))
        dq_ref[...] = dq.astype(BF16)

        @pl.when(i == nq - 1)
        def _():
            dk_ref[...] = dk_acc[...].astype(BF16)
            dv_ref[...] = dv_acc[...].astype(BF16)

    out = jax.ShapeDtypeStruct((s, n_heads * HEAD_DIM), BF16)
    head_blk = pl.BlockSpec((s, HEAD_DIM), lambda h, i: (0, h))
    tile_blk = pl.BlockSpec((t, HEAD_DIM), lambda h, i: (i, h))
    return pl.pallas_call(
        kern, name="sb_bwd", grid=(n_heads, nq),
        in_specs=_qkv_specs(s, t, n_heads, base) + [tile_blk],
        out_specs=[tile_blk, head_blk, head_blk],
        out_shape=[out, out, out],
        scratch_shapes=[pltpu.VMEM((s, HEAD_DIM), F32), pltpu.VMEM((s, HEAD_DIM), F32), pltpu.VMEM((nq, t, 1), F32)],
        compiler_params=_params("parallel", "arbitrary"),
    )(qkv, qkv, qkv, d_o)


def _fox_scores(q, k, cq, ck, j, i, t):
    row, col = _iota2(t)
    sc = lax.dot_general(q, k, NT, preferred_element_type=F32) * (HEAD_DIM ** -0.5)
    sc = sc + cq - ck
    valid = (col - row) < jnp.where(j < i, t, 1)
    return jnp.where(valid, sc, NEG_BIG), valid


def fox_fwd(qkv, cum_col, cum_row, n_heads, base):
    s = qkv.shape[0]
    t = _tile(s, ATT_TILE, LANES)

    def kern(q_ref, k_ref, v_ref, cq_ref, ck_ref, o_ref, lse_ref):
        i = pl.program_id(1)
        q = q_ref[...]
        cq = cq_ref[0]

        def body(j, carry):
            m, l, acc = carry
            off = pl.multiple_of(j * t, t)
            sc, _ = _fox_scores(q, k_ref[pl.ds(off, t), :], cq, ck_ref[0, :, pl.ds(off, t)], j, i, t)
            m_new = jnp.maximum(m, jnp.max(sc, axis=1, keepdims=True))
            p = jnp.exp(sc - m_new)
            alpha = jnp.exp(m - m_new)
            l = alpha * l + jnp.sum(p, axis=1, keepdims=True)
            acc = alpha * acc + jnp.dot(p.astype(BF16), v_ref[pl.ds(off, t), :], preferred_element_type=F32)
            return m_new, l, acc

        m, l, acc = lax.fori_loop(0, i + 1, body, (jnp.full((t, 1), NEG_BIG, F32), jnp.zeros((t, 1), F32),
                                                   jnp.zeros((t, HEAD_DIM), F32)))
        o_ref[...] = acc / l
        lse_ref[0] = m + jnp.log(l)

    col_blk = pl.BlockSpec((1, t, 1), lambda h, i: (h, i, 0))
    return pl.pallas_call(
        kern, name="fox_fwd", grid=(n_heads, s // t),
        in_specs=_qkv_specs(s, t, n_heads, base) + [col_blk, pl.BlockSpec((1, 1, s), lambda h, i: (h, 0, 0))],
        out_specs=[pl.BlockSpec((t, HEAD_DIM), lambda h, i: (i, h)), col_blk],
        out_shape=[jax.ShapeDtypeStruct((s, n_heads * HEAD_DIM), F32), jax.ShapeDtypeStruct((n_heads, s, 1), F32)],
        compiler_params=_params("parallel", "arbitrary"),
    )(qkv, qkv, qkv, cum_col, cum_row)


def fox_bwd(qkv, cum_col, cum_row, o, d_o, lse, n_heads, base):
    s = qkv.shape[0]
    t = _tile(s, ATT_TILE, LANES)
    nq = s // t
    scale = HEAD_DIM ** -0.5

    def kern(q_ref, k_ref, v_ref, cq_ref, ck_ref, o_ref, do_ref, lse_ref,
             dq_ref, dk_ref, dv_ref, dcq_ref, dck_ref, dk_acc, dv_acc, dck_acc):
        i = pl.program_id(1)

        @pl.when(i == 0)
        def _():
            dk_acc[...] = jnp.zeros_like(dk_acc)
            dv_acc[...] = jnp.zeros_like(dv_acc)
            dck_acc[...] = jnp.zeros_like(dck_acc)

        q = q_ref[...]
        do = do_ref[...]
        cq = cq_ref[0]
        lse = lse_ref[0]
        delta = jnp.sum(do.astype(F32) * o_ref[...], axis=1, keepdims=True)

        def body(j, carry):
            dq, dcq = carry
            off = pl.multiple_of(j * t, t)
            k = k_ref[pl.ds(off, t), :]
            v = v_ref[pl.ds(off, t), :]
            sc, valid = _fox_scores(q, k, cq, ck_ref[0, :, pl.ds(off, t)], j, i, t)
            p = jnp.where(valid, jnp.exp(sc - lse), 0.0)
            dp = lax.dot_general(do, v, NT, preferred_element_type=F32)
            ds = p * (dp - delta)
            dsb = (ds * scale).astype(BF16)
            dq = dq + jnp.dot(dsb, k, preferred_element_type=F32)
            dk_acc[pl.ds(off, t), :] += lax.dot_general(dsb, q, TN, preferred_element_type=F32)
            dv_acc[pl.ds(off, t), :] += lax.dot_general(p.astype(BF16), do, TN, preferred_element_type=F32)
            dck_acc[:, pl.ds(off, t)] -= jnp.sum(ds, axis=0, keepdims=True)
            return dq, dcq + jnp.sum(ds, axis=1, keepdims=True)

        dq, dcq = lax.fori_loop(0, i + 1, body, (jnp.zeros((t, HEAD_DIM), F32), jnp.zeros((t, 1), F32)))
        dq_ref[...] = dq.astype(BF16)
        dcq_ref[0] = dcq

        @pl.when(i == nq - 1)
        def _():
            dk_ref[...] = dk_acc[...].astype(BF16)
            dv_ref[...] = dv_acc[...].astype(BF16)
            dck_ref[0] = dck_acc[...]

    out = jax.ShapeDtypeStruct((s, n_heads * HEAD_DIM), BF16)
    head_blk = pl.BlockSpec((s, HEAD_DIM), lambda h, i: (0, h))
    tile_blk = pl.BlockSpec((t, HEAD_DIM), lambda h, i: (i, h))
    col_blk = pl.BlockSpec((1, t, 1), lambda h, i: (h, i, 0))
    row_blk = pl.BlockSpec((1, 1, s), lambda h, i: (h, 0, 0))
    return pl.pallas_call(
        kern, name="fox_bwd", grid=(n_heads, nq),
        in_specs=_qkv_specs(s, t, n_heads, base) + [col_blk, row_blk, tile_blk, tile_blk, col_blk],
        out_specs=[tile_blk, head_blk, head_blk, col_blk, row_blk],
        out_shape=[out, out, out, jax.ShapeDtypeStruct((n_heads, s, 1), F32),
                   jax.ShapeDtypeStruct((n_heads, 1, s), F32)],
        scratch_shapes=[pltpu.VMEM((s, HEAD_DIM), F32), pltpu.VMEM((s, HEAD_DIM), F32), pltpu.VMEM((1, s), F32)],
        compiler_params=_params("parallel", "arbitrary"),
    )(qkv, qkv, qkv, cum_col, cum_row, o, d_o, lse)


def _place():
    x, y, c = lax.axis_index("x"), lax.axis_index("y"), lax.axis_index("c")
    other_chips = [(1 - x, y), (x, 1 - y), (1 - x, 1 - y)]
    return x, y, c, other_chips


ANY = pl.BlockSpec(memory_space=pl.ANY)


def all_gather_flat(flat):
    nr, w = flat.shape
    half = nr // 2
    ch = half // AG_CHUNKS
    n_cp = 3 * AG_CHUNKS

    def body(in_ref, out_ref, ici_send, ici_recv, d2d_send, d2d_recv, local_sem):
        x, y, c, chips = _place()
        me = 2 * x + y
        sibling = (x, y, 1 - c)

        def rows(core, q):
            return pl.ds(pl.multiple_of(core * half + q * ch, 16), ch)

        mine = pltpu.make_async_copy(in_ref, out_ref.at[me], local_sem)
        mine.start()

        def ici(j, q):
            return pltpu.make_async_remote_copy(
                src_ref=in_ref.at[rows(c, q)], dst_ref=out_ref.at[me, rows(c, q)],
                send_sem=ici_send.at[j * AG_CHUNKS + q], recv_sem=ici_recv.at[j * AG_CHUNKS + q],
                device_id=(chips[j][0], chips[j][1], c), device_id_type=MESH)

        def landed(j, q, core):
            return out_ref.at[2 * chips[j][0] + chips[j][1], rows(core, q)]

        def d2d(j, q):
            return pltpu.make_async_remote_copy(
                src_ref=landed(j, q, c), dst_ref=landed(j, q, c),
                send_sem=d2d_send.at[j * AG_CHUNKS + q], recv_sem=d2d_recv.at[j * AG_CHUNKS + q],
                device_id=sibling, device_id_type=MESH)

        sends = [ici(j, q) for q in range(AG_CHUNKS) for j in range(3)]
        for cp in sends:
            cp.start()
        for q in range(AG_CHUNKS):
            for j in range(3):
                pltpu.make_async_remote_copy(
                    src_ref=landed(j, q, c), dst_ref=landed(j, q, c),
                    send_sem=ici_send.at[j * AG_CHUNKS + q], recv_sem=ici_recv.at[j * AG_CHUNKS + q],
                    device_id=(chips[j][0], chips[j][1], c), device_id_type=MESH).wait_recv()
                d2d(j, q).start()
        for q in range(AG_CHUNKS):
            for j in range(3):
                pltpu.make_async_remote_copy(
                    src_ref=landed(j, q, 1 - c), dst_ref=landed(j, q, 1 - c),
                    send_sem=d2d_send.at[j * AG_CHUNKS + q], recv_sem=d2d_recv.at[j * AG_CHUNKS + q],
                    device_id=sibling, device_id_type=MESH).wait_recv()
        for cp in sends:
            cp.wait_send()
        for q in range(AG_CHUNKS):
            for j in range(3):
                d2d(j, q).wait_send()
        mine.wait()

    return pl.pallas_call(
        body, name="all_gather_flat", in_specs=[ANY], out_specs=ANY,
        out_shape=jax.ShapeDtypeStruct((N_CHIPS, nr, w), flat.dtype),
        scratch_shapes=[pltpu.SemaphoreType.DMA((n_cp,)), pltpu.SemaphoreType.DMA((n_cp,)),
                        pltpu.SemaphoreType.DMA((n_cp,)), pltpu.SemaphoreType.DMA((n_cp,)),
                        pltpu.SemaphoreType.DMA],
    )(flat)


def swap_halves(pieces):
    _, nr, w = pieces.shape
    half = nr // 2

    def body(in_ref, out_ref, send_sem, recv_sem):
        x, y, c, _ = _place()
        cp = pltpu.make_async_remote_copy(
            src_ref=in_ref.at[:, pl.ds(pl.multiple_of((1 - c) * half, 16), half), :], dst_ref=out_ref,
            send_sem=send_sem, recv_sem=recv_sem, device_id=(x, y, 1 - c), device_id_type=MESH)
        cp.start()
        cp.wait()

    return pl.pallas_call(
        body, name="swap_halves", in_specs=[ANY], out_specs=ANY,
        out_shape=jax.ShapeDtypeStruct((N_CHIPS, half, w), pieces.dtype),
        scratch_shapes=[pltpu.SemaphoreType.DMA, pltpu.SemaphoreType.DMA],
    )(pieces)


def pair_sum(pieces, got, core):
    _, _, half, w = pieces.shape
    tr = _tile(half, 256, 16)

    def kern(core_ref, p_ref, g_ref, o_ref):
        o_ref[...] = (p_ref[...].astype(F32) + g_ref[...].astype(F32)).astype(o_ref.dtype)

    return pl.pallas_call(
        kern, name="pair_sum",
        grid_spec=pltpu.PrefetchScalarGridSpec(
            num_scalar_prefetch=1, grid=(N_CHIPS, half // tr),
            in_specs=[pl.BlockSpec((None, None, tr, w), lambda k, i, core_ref: (k, core_ref[0], i, 0)),
                      pl.BlockSpec((None, tr, w), lambda k, i, core_ref: (k, i, 0))],
            out_specs=pl.BlockSpec((None, tr, w), lambda k, i, core_ref: (k, i, 0))),
        out_shape=jax.ShapeDtypeStruct((N_CHIPS, half, w), pieces.dtype),
        compiler_params=_params("parallel", "parallel"),
    )(core, pieces, got)


def scatter_chips(sums):
    _, half, w = sums.shape

    def body(in_ref, out_ref, send_sems, recv_sems):
        x, y, c, chips = _place()
        cps = [pltpu.make_async_remote_copy(
            src_ref=in_ref.at[2 * chips[j][0] + chips[j][1]], dst_ref=out_ref.at[j],
            send_sem=send_sems.at[j], recv_sem=recv_sems.at[j],
            device_id=(chips[j][0], chips[j][1], c), device_id_type=MESH) for j in range(3)]
        for cp in cps:
            cp.start()
        for cp in cps:
            cp.wait()

    return pl.pallas_call(
        body, name="scatter_chips", in_specs=[ANY], out_specs=ANY,
        out_shape=jax.ShapeDtypeStruct((3, half, w), sums.dtype),
        scratch_shapes=[pltpu.SemaphoreType.DMA((3,)), pltpu.SemaphoreType.DMA((3,))],
    )(sums)


def chip_sum(sums, got, chip):
    _, half, w = sums.shape
    tr = _tile(half, 256, 16)

    def kern(chip_ref, s_ref, g0_ref, g1_ref, g2_ref, o_ref):
        o_ref[...] = ((s_ref[...].astype(F32) + g0_ref[...].astype(F32)) + g1_ref[...].astype(F32)) \
            + g2_ref[...].astype(F32)

    def got_spec(j):
        return pl.BlockSpec((None, tr, w), lambda i, chip_ref: (j, i, 0))

    return pl.pallas_call(
        kern, name="chip_sum",
        grid_spec=pltpu.PrefetchScalarGridSpec(
            num_scalar_prefetch=1, grid=(half // tr,),
            in_specs=[pl.BlockSpec((None, tr, w), lambda i, chip_ref: (chip_ref[0], i, 0)),
                      got_spec(0), got_spec(1), got_spec(2)],
            out_specs=pl.BlockSpec((tr, w), lambda i, chip_ref: (i, 0))),
        out_shape=jax.ShapeDtypeStruct((half, w), F32),
        compiler_params=_params("parallel"),
    )(chip, sums, got, got, got)


def join_halves(mine):
    half, w = mine.shape

    def body(in_ref, out_ref, send_sem, recv_sem, local_sem):
        x, y, c, _ = _place()
        rows = pl.ds(pl.multiple_of(c * half, 8), half)
        local = pltpu.make_async_copy(in_ref, out_ref.at[rows], local_sem)
        local.start()
        cp = pltpu.make_async_remote_copy(
            src_ref=in_ref, dst_ref=out_ref.at[rows], send_sem=send_sem, recv_sem=recv_sem,
            device_id=(x, y, 1 - c), device_id_type=MESH)
        cp.start()
        cp.wait()
        local.wait()

    return pl.pallas_call(
        body, name="join_halves", in_specs=[ANY], out_specs=ANY,
        out_shape=jax.ShapeDtypeStruct((2 * half, w), mine.dtype),
        scratch_shapes=[pltpu.SemaphoreType.DMA, pltpu.SemaphoreType.DMA, pltpu.SemaphoreType.DMA],
    )(mine)


def _adam(w, g, m, v):
    m = ADAM_B1 * m + (1.0 - ADAM_B1) * g
    v = ADAM_B2 * v + (1.0 - ADAM_B2) * (g * g)
    m_hat = m / (1.0 - ADAM_B1 ** ADAM_STEP)
    v_hat = v / (1.0 - ADAM_B2 ** ADAM_STEP)
    delta = -ADAM_LR * (m_hat / (jnp.sqrt(v_hat) + ADAM_EPS) + ADAM_WD * w)
    return delta, m, v


def small_allreduce_adam(g_part, w, m, v):
    n_dev = 8
    r, d = g_part.shape

    def body(g_ref, w_ref, m_ref, v_ref, gs_ref, dl_ref, nm_ref, nv_ref, all_ref, send_sems, recv_sems):
        x, y, c, _ = _place()
        me = 4 * x + 2 * y + c
        all_ref[me] = g_ref[...]
        cps = []
        for rel in range(1, n_dev):
            px = 1 - x if rel & 4 else x
            py = 1 - y if rel & 2 else y
            pc = 1 - c if rel & 1 else c
            cps.append(pltpu.make_async_remote_copy(
                src_ref=g_ref, dst_ref=all_ref.at[me], send_sem=send_sems.at[rel - 1], recv_sem=recv_sems.at[rel - 1],
                device_id=(px, py, pc), device_id_type=MESH))
        for cp in cps:
            cp.start()
        for cp in cps:
            cp.wait()
        total = all_ref[0]
        for dev in range(1, n_dev):
            total = total + all_ref[dev]
        gs_ref[...] = total
        delta, nm, nv = _adam(w_ref[...], total, m_ref[...], v_ref[...])
        dl_ref[...] = delta
        nm_ref[...] = nm
        nv_ref[...] = nv

    vm = pl.BlockSpec(memory_space=pltpu.VMEM)
    out = jax.ShapeDtypeStruct((r, d), F32)
    return pl.pallas_call(
        body, name="small_allreduce_adam", in_specs=[vm, vm, vm, vm], out_specs=[vm, vm, vm, vm],
        out_shape=[out, out, out, out],
        scratch_shapes=[pltpu.VMEM((n_dev, r, d), F32), pltpu.SemaphoreType.DMA((n_dev - 1,)),
                        pltpu.SemaphoreType.DMA((n_dev - 1,))],
    )(g_part, w, m, v)


def adam_update(name, w, g, m, v):
    r, c = w.shape
    tr = _tile(r, 128, 8)

    def kern(w_ref, g_ref, m_ref, v_ref, dl_ref, nm_ref, nv_ref):
        delta, nm, nv = _adam(w_ref[...], g_ref[...], m_ref[...], v_ref[...])
        dl_ref[...] = delta
        nm_ref[...] = nm
        nv_ref[...] = nv

    blk = pl.BlockSpec((tr, c), lambda i: (i, 0))
    out = jax.ShapeDtypeStruct((r, c), F32)
    return pl.pallas_call(
        kern, name=name, grid=(r // tr,), in_specs=[blk, blk, blk, blk], out_specs=[blk, blk, blk],
        out_shape=[out, out, out], compiler_params=_params("parallel"),
    )(w, g, m, v)


def _flat_rows(shapes):
    total = sum(r * c for r, c in shapes)
    rows = -(-total // FLAT_W)
    return -(-rows // FLAT_ROW_ALIGN) * FLAT_ROW_ALIGN


def _to_flat(arrs, lead):
    nr = _flat_rows([a.shape[-2:] for a in arrs])
    flat = jnp.concatenate([a.reshape(lead + (-1,)) for a in arrs], axis=-1)
    pad = nr * FLAT_W - flat.shape[-1]
    flat = jnp.pad(flat, [(0, 0)] * len(lead) + [(0, pad)])
    return flat.reshape(lead + (nr, FLAT_W))


def _from_flat(flat, shapes, lead):
    flat = flat.reshape(lead + (-1,))
    out, off = [], 0
    for r, c in shapes:
        out.append(lax.slice_in_dim(flat, off, off + r * c, axis=len(lead)).reshape(lead + (r, c)))
        off += r * c
    return out


def _unshard_cols(a):
    k, r, c = a.shape
    return a.transpose(1, 0, 2).reshape(r, k * c)


def _shard_cols(a):
    r, n = a.shape
    return a.reshape(r, N_CHIPS, n // N_CHIPS).transpose(1, 0, 2)


def kernel(x, norm_mix_pre, norm_mix_post, w_in, b_forget, w_branch_sb, w_branch_fox, w_out, norm_ffn_pre, norm_ffn_post, w_ffn_gate, w_ffn_up, w_ffn_down, loss_target, m_norm_mix_pre, m_norm_mix_post, m_w_in, m_b_forget, m_w_branch_sb, m_w_branch_fox, m_w_out, m_norm_ffn_pre, m_norm_ffn_post, m_w_ffn_gate, m_w_ffn_up, m_w_ffn_down, v_norm_mix_pre, v_norm_mix_post, v_w_in, v_b_forget, v_w_branch_sb, v_w_branch_fox, v_w_out, v_norm_ffn_pre, v_norm_ffn_post, v_w_ffn_gate, v_w_ffn_up, v_w_ffn_down):
    s, d = x.shape[1], x.shape[2]
    n_heads = b_forget.shape[1]
    d_att = n_heads * HEAD_DIM
    d_ff = w_ffn_down.shape[1] * N_CHIPS
    f_pad = 512
    big = [w_in, w_branch_sb, w_branch_fox, w_out, w_ffn_gate, w_ffn_up, w_ffn_down]
    big_m = [m_w_in, m_w_branch_sb, m_w_branch_fox, m_w_out, m_w_ffn_gate, m_w_ffn_up, m_w_ffn_down]
    big_v = [v_w_in, v_w_branch_sb, v_w_branch_fox, v_w_out, v_w_ffn_gate, v_w_ffn_up, v_w_ffn_down]
    shard_shapes = [w.shape[1:] for w in big]
    core = lax.axis_index("c").astype(jnp.int32).reshape(1)
    chip = (2 * lax.axis_index("x") + lax.axis_index("y")).astype(jnp.int32).reshape(1)

    flat = _to_flat([w[0].astype(BF16) for w in big], ())
    gathered = _from_flat(all_gather_flat(flat), shard_shapes, (N_CHIPS,))
    w_in_full = _unshard_cols(gathered[0])
    w_main = jnp.concatenate(
        [w_in_full[:, :6 * d_att], w_in_full[:, 6 * d_att + n_heads:], w_in_full[:, 6 * d_att:6 * d_att + n_heads],
         jnp.zeros((d, f_pad - n_heads), BF16)], axis=1)
    w_bsb = _unshard_cols(gathered[1])
    w_bfx = _unshard_cols(gathered[2])
    w_o = gathered[3].reshape(d, d)
    w_gu = jnp.concatenate([_unshard_cols(gathered[4]), _unshard_cols(gathered[5])], axis=1)
    w_dn = gathered[6].reshape(d_ff, d)
    n_qkv = 6 * d_att
    n_gf = 2 * d + f_pad

    x2 = x[0]
    tgt = loss_target[0]
    b_pad = jnp.pad(b_forget, ((0, 0), (0, LANES - n_heads)))

    u = norm_in(x2, norm_mix_pre)
    qkv = mm(u, w_main, "nn", BF16, "proj_qkv", 0, n_qkv)
    gf = mm(u, w_main, "nn", F32, "proj_gates", n_qkv, n_gf)
    cum = cum_fwd(gf, b_pad, 2 * d)
    cum_heads = cum[:, :n_heads].T
    cum_col, cum_row = cum_heads[:, :, None], cum_heads[:, None, :]
    o_sb = sb_fwd(qkv, n_heads, 0)
    o_fx, lse = fox_fwd(qkv, cum_col, cum_row, n_heads, 3 * n_heads)
    bsb = mm(o_sb, w_bsb, "nn", F32, "branch_sb")
    bfx = mm(o_fx, w_bfx, "nn", F32, "branch_fox")
    merged = gate_fwd(bsb, bfx, gf)
    mix = mm(merged, w_o, "nn", F32, "out_proj")
    h1, u2 = mid_fwd(x2, mix, norm_mix_post, norm_ffn_pre)
    gu = mm(u2, w_gu, "nn", F32, "ffn_gate_up")
    act = swiglu_fwd(gu)
    ff = mm(act, w_dn, "nn", F32, "ffn_down")
    dy, d_ff_out, dg_fpost, loss_part = loss_head(h1, ff, norm_ffn_post, tgt)

    dw_dn = mm(act, d_ff_out, "tn", F32, "dw_ffn_down")
    d_act = mm(d_ff_out, w_dn, "nt", F32, "d_act")
    d_gu = swiglu_bwd(d_act, gu)
    dw_gu = mm(u2, d_gu, "tn", F32, "dw_ffn_gate_up")
    du2 = mm(d_gu, w_gu, "nt", F32, "d_u2")
    dh1, d_mix, dg_fpre, dg_post = mid_bwd(dy, du2, h1, mix, norm_ffn_pre, norm_mix_post)
    dw_o = mm(merged, d_mix, "tn", F32, "dw_out")
    d_merged = mm(d_mix, w_o, "nt", F32, "d_merged")
    d_bsb, d_bfx, d_gs, d_gx = gate_bwd(d_merged, bsb, bfx, gf)
    dw_bsb = mm(o_sb, d_bsb, "tn", F32, "dw_branch_sb")
    dw_bfx = mm(o_fx, d_bfx, "tn", F32, "dw_branch_fox")
    d_osb = mm(d_bsb, w_bsb, "nt", BF16, "d_o_sb")
    d_ofx = mm(d_bfx, w_bfx, "nt", BF16, "d_o_fox")
    dq_s, dk_s, dv_s = sb_bwd(qkv, d_osb, n_heads, 0)
    dq_f, dk_f, dv_f, dcq, dck = fox_bwd(qkv, cum_col, cum_row, o_fx, d_ofx, lse, n_heads, 3 * n_heads)
    d_cum = jnp.pad((dcq[:, :, 0] + dck[:, 0, :]).T, ((0, 0), (0, LANES - n_heads)))
    d_f, db_pad = cum_bwd(d_cum, gf, b_pad, 2 * d, n_heads)
    d_main = jnp.concatenate(
        [dq_s, dk_s, dv_s, dq_f, dk_f, dv_f, d_gs, d_gx, d_f, jnp.zeros((s, f_pad - LANES), BF16)], axis=1)
    dw_main = mm(u, d_main, "tn", F32, "dw_in")
    du = mm(d_main, w_main, "nt", F32, "d_u")
    dx, dg_pre = in_bwd(dh1, du, x2, norm_mix_pre)

    dw_in_full = jnp.concatenate(
        [dw_main[:, :n_qkv], dw_main[:, n_qkv + 2 * d:n_qkv + 2 * d + n_heads], dw_main[:, n_qkv:n_qkv + 2 * d]], axis=1)
    pieces = _to_flat([_shard_cols(dw_in_full).astype(BF16), _shard_cols(dw_bsb).astype(BF16),
                       _shard_cols(dw_bfx).astype(BF16), dw_o.reshape(N_CHIPS, d // N_CHIPS, d).astype(BF16),
                       _shard_cols(dw_gu[:, :d_ff]).astype(BF16), _shard_cols(dw_gu[:, d_ff:]).astype(BF16),
                       dw_dn.reshape(N_CHIPS, d_ff // N_CHIPS, d).astype(BF16)], (N_CHIPS,))
    nr = pieces.shape[1]
    from_sibling = swap_halves(pieces)
    sums = pair_sum(pieces.reshape(N_CHIPS, 2, nr // 2, FLAT_W), from_sibling, core)
    from_chips = scatter_chips(sums)
    my_half = chip_sum(sums, from_chips, chip)
    g_flat = join_halves(my_half)
    grads = _from_flat(g_flat, shard_shapes, ())

    deltas, new_ms, new_vs = [], [], []
    names = ["adam_w_in", "adam_branch_sb", "adam_branch_fox", "adam_out", "adam_gate", "adam_up", "adam_down"]
    for nm_, w, g, m, v in zip(names, big, grads, big_m, big_v):
        dl, nm, nv = adam_update(nm_, w[0], g, m[0], v[0])
        deltas.append(dl[None])
        new_ms.append(nm[None])
        new_vs.append(nv[None])
    grads = [g[None] for g in grads]

    def pack(rows):
        rows = [jnp.pad(r_, ((0, 0), (0, d - r_.shape[1]))) for r_ in rows]
        return jnp.concatenate(rows + [jnp.zeros((8 - len(rows), d), F32)], axis=0)

    sm_g, sm_d, sm_m, sm_v = small_allreduce_adam(
        pack([dg_pre, dg_post, dg_fpre, dg_fpost, db_pad]),
        pack([norm_mix_pre, norm_mix_post, norm_ffn_pre, norm_ffn_post, b_forget]),
        pack([m_norm_mix_pre, m_norm_mix_post, m_norm_ffn_pre, m_norm_ffn_post, m_b_forget]),
        pack([v_norm_mix_pre, v_norm_mix_post, v_norm_ffn_pre, v_norm_ffn_post, v_b_forget]))

    def small(a):
        return [a[0:1], a[1:2], a[2:3], a[3:4], a[4:5, :n_heads]]

    def ordered(sm, bg):
        return [sm[0], sm[1], bg[0], sm[4], bg[1], bg[2], bg[3], sm[2], sm[3], bg[4], bg[5], bg[6]]

    loss = lax.psum(loss_part[0, 0], ("x", "y", "c"))
    return (loss, dx[None], *ordered(small(sm_g), grads), *ordered(small(sm_d), deltas),
            *ordered(small(sm_m), new_ms), *ordered(small(sm_v), new_vs))
```

```python
import functools

import jax
import jax.numpy as jnp
from jax import lax
from jax.experimental import pallas as pl
from jax.experimental.pallas import tpu as pltpu

F32 = jnp.float32
BF16 = jnp.bfloat16
MESH = pl.DeviceIdType.MESH

HEAD_DIM = 128
LANES = 128
ATT_TILE = 256
ROW_TILE = 256
N_CHIPS = 4
RMS_EPS = 1e-6
ADAM_LR = 0.001
ADAM_B1 = 0.9
ADAM_B2 = 0.999
ADAM_EPS = 1e-08
ADAM_WD = 0.01
ADAM_STEP = 10
NEG_BIG = -1e30
VMEM_LIMIT = 56 * 1024 * 1024

NN = (((1,), (0,)), ((), ()))
NT = (((1,), (1,)), ((), ()))
TN = (((0,), (0,)), ((), ()))


def _tile(n, pref, align):
    best = None
    t = align
    while t <= min(n, pref):
        if n % t == 0:
            best = t
        t += align
    return n if best is None else best


def _params(*sem):
    return pltpu.CompilerParams(dimension_semantics=sem, vmem_limit_bytes=VMEM_LIMIT)


def mm(a, b, mode, out_dtype, name, *, tm=1024, tn=512, tk=1024, b_win=None, chunks=None, out_into=None):
    n_per, blk0 = chunks if chunks else (1, 0)
    if mode == "nn":
        m, k = a.shape
        n = b.shape[0] * n_per * tn if chunks else (b_win[1] if b_win else b.shape[1])
    elif mode == "nt":
        m = a.shape[0]
        k = b.shape[0] * n_per * tk if chunks else a.shape[1]
        n = b.shape[-2]
    else:
        k, m = a.shape
        n = b.shape[1]
    tm = _tile(m, tm, LANES)
    if not (chunks and mode in ("nn", "tn")):
        tn = _tile(n, tn, LANES)
    if not (chunks and mode == "nt"):
        tk = _tile(k, tk, LANES)
    assert m % tm == 0 and n % tn == 0 and k % tk == 0, (name, m, n, k, tm, tn, tk)
    j0 = 0
    if b_win:
        assert b_win[0] % tn == 0
        j0 = b_win[0] // tn
    nk = k // tk
    dims = {"nn": NN, "nt": NT, "tn": TN}[mode]

    def kern(a_ref, b_ref, *rest):
        o_ref, acc_ref = rest[-2], rest[-1]
        kk = pl.program_id(2)

        @pl.when(kk == 0)
        def _():
            acc_ref[...] = jnp.zeros_like(acc_ref)

        acc_ref[...] += lax.dot_general(a_ref[...].astype(BF16), b_ref[...].astype(BF16), dims,
                                        preferred_element_type=F32)

        @pl.when(kk == nk - 1)
        def _():
            o_ref[...] = acc_ref[...].astype(o_ref.dtype)

    out_spec = pl.BlockSpec((tm, tn), lambda i, j, kk: (i, j))
    out_shape = jax.ShapeDtypeStruct((m, n), out_dtype)
    if mode == "nn":
        a_spec = pl.BlockSpec((tm, tk), lambda i, j, kk: (i, kk))
        if chunks:
            b_spec = pl.BlockSpec((None, tk, tn), lambda i, j, kk: (j // n_per, kk, blk0 + j % n_per))
        else:
            b_spec = pl.BlockSpec((tk, tn), lambda i, j, kk: (kk, j + j0))
    elif mode == "nt":
        a_spec = pl.BlockSpec((tm, tk), lambda i, j, kk: (i, kk))
        if chunks:
            b_spec = pl.BlockSpec((None, tn, tk), lambda i, j, kk: (kk // n_per, j, blk0 + kk % n_per))
        else:
            b_spec = pl.BlockSpec((tn, tk), lambda i, j, kk: (j, kk))
    else:
        a_spec = pl.BlockSpec((tk, tm), lambda i, j, kk: (kk, i))
        b_spec = pl.BlockSpec((tk, tn), lambda i, j, kk: (kk, j))
        if chunks:
            out_spec = pl.BlockSpec((None, tm, tn), lambda i, j, kk: (j // n_per, i, blk0 + j % n_per))
    in_specs, operands, aliases = [a_spec, b_spec], [a, b], {}
    if chunks and mode == "tn":
        assert out_into is not None
        out_shape = jax.ShapeDtypeStruct(out_into.shape, out_dtype)
        in_specs.append(pl.BlockSpec(memory_space=pl.ANY))
        operands.append(out_into)
        aliases = {2: 0}
    return pl.pallas_call(
        kern, name=name, grid=(m // tm, n // tn, nk),
        in_specs=in_specs, out_specs=out_spec, out_shape=out_shape,
        scratch_shapes=[pltpu.VMEM((tm, tn), F32)], input_output_aliases=aliases,
        compiler_params=_params("parallel", "parallel", "arbitrary"),
    )(*operands)


def _rstd(v):
    return lax.rsqrt(jnp.mean(v * v, axis=-1, keepdims=True) + RMS_EPS)


def _norm_bwd(v, g, dy):
    r = _rstd(v)
    vh = v * r
    dyg = dy * g
    dv = r * (dyg - vh * jnp.mean(dyg * vh, axis=-1, keepdims=True))
    return dv, jnp.sum(dy * vh, axis=0, keepdims=True)


def _row_call(kern, name, ins, outs, s, d):
    tr = _tile(s, ROW_TILE, 16)

    def spec(shape, is_row):
        if is_row:
            return pl.BlockSpec((tr, shape[1]), lambda i: (i, 0))
        return pl.BlockSpec(shape, lambda i: (0, 0))

    return pl.pallas_call(
        kern, name=name, grid=(s // tr,),
        in_specs=[spec(a.shape, r) for a, r in ins],
        out_specs=[spec(sh, r) for sh, _, r in outs],
        out_shape=[jax.ShapeDtypeStruct(sh, dt) for sh, dt, _ in outs],
        compiler_params=_params("arbitrary"),
    )(*[a for a, _ in ins])


def norm_in(x, g):
    s, d = x.shape

    def kern(x_ref, g_ref, u_ref):
        v = x_ref[...]
        u_ref[...] = (v * _rstd(v) * g_ref[...]).astype(BF16)

    return _row_call(kern, "norm_in", [(x, True), (g, False)], [((s, d), BF16, True)], s, d)[0]


def mid_fwd(x, mix, g_post, g_fpre):
    s, d = x.shape

    def kern(x_ref, mix_ref, gp_ref, gf_ref, h1_ref, u2_ref):
        mixv = mix_ref[...]
        h1 = x_ref[...] + mixv * _rstd(mixv) * gp_ref[...]
        h1_ref[...] = h1
        u2_ref[...] = (h1 * _rstd(h1) * gf_ref[...]).astype(BF16)

    return _row_call(kern, "mid_fwd", [(x, True), (mix, True), (g_post, False), (g_fpre, False)],
                     [((s, d), F32, True), ((s, d), BF16, True)], s, d)


def loss_head(h1, ff, g_fpost, target):
    s, d = h1.shape

    def kern(h1_ref, ff_ref, g_ref, t_ref, dy_ref, dff_ref, dg_ref, loss_ref):
        @pl.when(pl.program_id(0) == 0)
        def _():
            dg_ref[...] = jnp.zeros_like(dg_ref)
            loss_ref[...] = jnp.zeros_like(loss_ref)

        ffv = ff_ref[...]
        g = g_ref[...]
        y = h1_ref[...] + ffv * _rstd(ffv) * g
        diff = y - t_ref[...]
        row_loss = jnp.mean(diff * diff, axis=-1, keepdims=True)
        loss_ref[...] += 0.5 * jnp.sum(row_loss, axis=0, keepdims=True)
        dy = diff / d
        dy_ref[...] = dy
        dff, dg = _norm_bwd(ffv, g, dy)
        dff_ref[...] = dff.astype(BF16)
        dg_ref[...] += dg

    return _row_call(kern, "loss_head",
                     [(h1, True), (ff, True), (g_fpost, False), (target, True)],
                     [((s, d), F32, True), ((s, d), BF16, True), ((1, d), F32, False), ((1, 1), F32, False)], s, d)


def mid_bwd(dy, du2, h1, mix, g_fpre, g_post):
    s, d = dy.shape

    def kern(dy_ref, du2_ref, h1_ref, mix_ref, gf_ref, gp_ref, dh1_ref, dmix_ref, dgf_ref, dgp_ref):
        @pl.when(pl.program_id(0) == 0)
        def _():
            dgf_ref[...] = jnp.zeros_like(dgf_ref)
            dgp_ref[...] = jnp.zeros_like(dgp_ref)

        dh, dgf = _norm_bwd(h1_ref[...], gf_ref[...], du2_ref[...])
        dh1 = dy_ref[...] + dh
        dh1_ref[...] = dh1
        dmix, dgp = _norm_bwd(mix_ref[...], gp_ref[...], dh1)
        dmix_ref[...] = dmix.astype(BF16)
        dgf_ref[...] += dgf
        dgp_ref[...] += dgp

    return _row_call(kern, "mid_bwd",
                     [(dy, True), (du2, True), (h1, True), (mix, True), (g_fpre, False), (g_post, False)],
                     [((s, d), F32, True), ((s, d), BF16, True), ((1, d), F32, False), ((1, d), F32, False)], s, d)


def in_bwd(dh1, du, x, g_pre):
    s, d = x.shape

    def kern(dh1_ref, du_ref, x_ref, g_ref, dx_ref, dg_ref):
        @pl.when(pl.program_id(0) == 0)
        def _():
            dg_ref[...] = jnp.zeros_like(dg_ref)

        dxn, dg = _norm_bwd(x_ref[...], g_ref[...], du_ref[...])
        dx_ref[...] = dh1_ref[...] + dxn
        dg_ref[...] += dg

    return _row_call(kern, "in_bwd", [(dh1, True), (du, True), (x, True), (g_pre, False)],
                     [((s, d), F32, True), ((1, d), F32, False)], s, d)


def _sigmoid(v):
    return 1.0 / (1.0 + jnp.exp(-v))


def gate_fwd(bsb, bfx, gf):
    s, d = bsb.shape
    tr, tc = _tile(s, 256, 16), _tile(d, 512, LANES)
    nc = d // tc

    def kern(bsb_ref, bfx_ref, gs_ref, gx_ref, o_ref):
        o_ref[...] = (_sigmoid(gs_ref[...]) * bsb_ref[...] + _sigmoid(gx_ref[...]) * bfx_ref[...]).astype(BF16)

    blk = pl.BlockSpec((tr, tc), lambda i, j: (i, j))
    return pl.pallas_call(
        kern, name="gate_fwd", grid=(s // tr, nc),
        in_specs=[blk, blk, blk, pl.BlockSpec((tr, tc), lambda i, j: (i, j + nc))],
        out_specs=blk, out_shape=jax.ShapeDtypeStruct((s, d), BF16),
        compiler_params=_params("parallel", "parallel"),
    )(bsb, bfx, gf, gf)


def gate_bwd(dmerged, bsb, bfx, gf):
    s, d = bsb.shape
    tr, tc = _tile(s, 256, 16), _tile(d, 512, LANES)
    nc = d // tc

    def kern(dm_ref, bsb_ref, bfx_ref, gs_ref, gx_ref, dbs_ref, dbx_ref, dgs_ref, dgx_ref):
        dm = dm_ref[...]
        ss = _sigmoid(gs_ref[...])
        sx = _sigmoid(gx_ref[...])
        dbs_ref[...] = (dm * ss).astype(BF16)
        dbx_ref[...] = (dm * sx).astype(BF16)
        dgs_ref[...] = (dm * bsb_ref[...] * ss * (1.0 - ss)).astype(BF16)
        dgx_ref[...] = (dm * bfx_ref[...] * sx * (1.0 - sx)).astype(BF16)

    blk = pl.BlockSpec((tr, tc), lambda i, j: (i, j))
    out = jax.ShapeDtypeStruct((s, d), BF16)
    return pl.pallas_call(
        kern, name="gate_bwd", grid=(s // tr, nc),
        in_specs=[blk, blk, blk, blk, pl.BlockSpec((tr, tc), lambda i, j: (i, j + nc))],
        out_specs=[blk, blk, blk, blk], out_shape=[out, out, out, out],
        compiler_params=_params("parallel", "parallel"),
    )(dmerged, bsb, bfx, gf, gf)


def swiglu_fwd(gu, cw):
    s, f2 = gu.shape
    tr = _tile(s, 256, 16)

    def kern(gu_ref, o_ref):
        g = gu_ref[:, :cw]
        o_ref[...] = (g * _sigmoid(g) * gu_ref[:, cw:]).astype(BF16)

    return pl.pallas_call(
        kern, name="swiglu_fwd", grid=(s // tr, f2 // (2 * cw)),
        in_specs=[pl.BlockSpec((tr, 2 * cw), lambda i, j: (i, j))],
        out_specs=pl.BlockSpec((tr, cw), lambda i, j: (i, j)),
        out_shape=jax.ShapeDtypeStruct((s, f2 // 2), BF16),
        compiler_params=_params("parallel", "parallel"),
    )(gu)


def swiglu_bwd(dact, gu, cw):
    s, f2 = gu.shape
    tr = _tile(s, 256, 16)

    def kern(da_ref, gu_ref, o_ref):
        da = da_ref[...]
        g = gu_ref[:, :cw]
        sg = _sigmoid(g)
        o_ref[:, :cw] = (da * gu_ref[:, cw:] * (sg * (1.0 + g * (1.0 - sg)))).astype(BF16)
        o_ref[:, cw:] = (da * (g * sg)).astype(BF16)

    return pl.pallas_call(
        kern, name="swiglu_bwd", grid=(s // tr, f2 // (2 * cw)),
        in_specs=[pl.BlockSpec((tr, cw), lambda i, j: (i, j)), pl.BlockSpec((tr, 2 * cw), lambda i, j: (i, j))],
        out_specs=pl.BlockSpec((tr, 2 * cw), lambda i, j: (i, j)),
        out_shape=jax.ShapeDtypeStruct((s, f2), BF16),
        compiler_params=_params("parallel", "parallel"),
    )(dact, gu)


def _split3(v):
    hi = v.astype(BF16)
    r = v - hi.astype(F32)
    mid = r.astype(BF16)
    lo = (r - mid.astype(F32)).astype(BF16)
    return hi, mid, lo


def _dot3_right(v, ones):
    hi, mid, lo = _split3(v)
    d = lambda p: jnp.dot(p, ones, preferred_element_type=F32)
    return (d(lo) + d(mid)) + d(hi)


def _dot3_left(ones, v):
    hi, mid, lo = _split3(v)
    d = lambda p: jnp.dot(ones, p, preferred_element_type=F32)
    return (d(lo) + d(mid)) + d(hi)


def _log1p_exp_neg_abs(v):
    return jnp.log1p(jnp.exp(-jnp.abs(v)))


def _mask01(cond):
    return jnp.where(cond, 1.0, 0.0).astype(BF16)


def _iota2(t):
    return (lax.broadcasted_iota(jnp.int32, (t, t), 0), lax.broadcasted_iota(jnp.int32, (t, t), 1))


def cum_fwd(gf, b_pad, f_col0):
    s = gf.shape[0]
    t = _tile(s, ATT_TILE, LANES)
    fb = f_col0 // LANES

    def kern(f_ref, b_ref, cum_ref, carry_ref):
        @pl.when(pl.program_id(0) == 0)
        def _():
            carry_ref[...] = jnp.zeros_like(carry_ref)

        v = f_ref[...] + b_ref[...]
        lf = jnp.minimum(v, 0.0) - _log1p_exp_neg_abs(v)
        row, col = _iota2(t)
        cum = _dot3_left(_mask01(col <= row), lf) + carry_ref[...]
        cum_ref[...] = cum
        carry_ref[...] = cum[t - 1:t, :]

    return pl.pallas_call(
        kern, name="cum_fwd", grid=(s // t,),
        in_specs=[pl.BlockSpec((t, LANES), lambda i: (i, fb)), pl.BlockSpec((1, LANES), lambda i: (0, 0))],
        out_specs=pl.BlockSpec((t, LANES), lambda i: (i, 0)),
        out_shape=jax.ShapeDtypeStruct((s, LANES), F32),
        scratch_shapes=[pltpu.VMEM((1, LANES), F32)],
        compiler_params=_params("arbitrary"),
    )(gf, b_pad)


def cum_bwd(dcum, gf, b_pad, f_col0, n_heads):
    s = gf.shape[0]
    t = _tile(s, ATT_TILE, LANES)
    nb = s // t
    fb = f_col0 // LANES

    def kern(dc_ref, f_ref, b_ref, df_ref, db_ref, carry_ref):
        @pl.when(pl.program_id(0) == 0)
        def _():
            carry_ref[...] = jnp.zeros_like(carry_ref)
            db_ref[...] = jnp.zeros_like(db_ref)

        row, col = _iota2(t)
        dlf = _dot3_left(_mask01(col >= row), dc_ref[...]) + carry_ref[...]
        carry_ref[...] = dlf[0:1, :]
        v = f_ref[...] + b_ref[...]
        sig_neg = jnp.exp(-jnp.maximum(v, 0.0) - _log1p_exp_neg_abs(v))
        lane = lax.broadcasted_iota(jnp.int32, (t, LANES), 1)
        df = jnp.where(lane < n_heads, dlf * sig_neg, 0.0)
        df_ref[...] = df.astype(BF16)
        db_ref[...] += jnp.sum(df, axis=0, keepdims=True)

    return pl.pallas_call(
        kern, name="cum_bwd", grid=(nb,),
        in_specs=[pl.BlockSpec((t, LANES), lambda i: (nb - 1 - i, 0)),
                  pl.BlockSpec((t, LANES), lambda i: (nb - 1 - i, fb)),
                  pl.BlockSpec((1, LANES), lambda i: (0, 0))],
        out_specs=[pl.BlockSpec((t, LANES), lambda i: (nb - 1 - i, 0)), pl.BlockSpec((1, LANES), lambda i: (0, 0))],
        out_shape=[jax.ShapeDtypeStruct((s, LANES), BF16), jax.ShapeDtypeStruct((1, LANES), F32)],
        scratch_shapes=[pltpu.VMEM((1, LANES), F32)],
        compiler_params=_params("arbitrary"),
    )(dcum, gf, b_pad)


def _qkv_specs(s, t, n_heads, base):
    return [pl.BlockSpec((t, HEAD_DIM), lambda h, i: (i, base + h)),
            pl.BlockSpec((s, HEAD_DIM), lambda h, i: (0, base + n_heads + h)),
            pl.BlockSpec((s, HEAD_DIM), lambda h, i: (0, base + 2 * n_heads + h))]


def _sb_scores(q, k, j, i, t, strict_lim):
    row, col = _iota2(t)
    z = lax.dot_general(q, k, NT, preferred_element_type=F32) * (HEAD_DIM ** -0.5)
    valid = (col - row) < jnp.where(j < i, t, strict_lim)
    l1p = _log1p_exp_neg_abs(z)
    log_keep = jnp.where(valid, -jnp.maximum(z, 0.0) - l1p, 0.0)
    return z, valid, l1p, log_keep


def sb_fwd(qkv, n_heads, base):
    s = qkv.shape[0]
    t = _tile(s, ATT_TILE, LANES)

    def kern(q_ref, k_ref, v_ref, o_ref):
        i = pl.program_id(1)
        row, col = _iota2(t)
        after = _mask01(row > col)
        q = q_ref[...]

        def body(jj, carry):
            run, acc = carry
            j = i - jj
            off = pl.multiple_of(j * t, t)
            z, valid, l1p, log_keep = _sb_scores(q, k_ref[pl.ds(off, t), :], j, i, t, 0)
            between = _dot3_right(log_keep, after) + run
            w = jnp.where(valid, jnp.exp(jnp.minimum(z, 0.0) - l1p + between), 0.0)
            acc = acc + jnp.dot(w.astype(BF16), v_ref[pl.ds(off, t), :], preferred_element_type=F32)
            return run + jnp.sum(log_keep, axis=1, keepdims=True), acc

        _, acc = lax.fori_loop(0, i + 1, body, (jnp.zeros((t, 1), F32), jnp.zeros((t, HEAD_DIM), F32)))
        o_ref[...] = acc.astype(o_ref.dtype)

    return pl.pallas_call(
        kern, name="sb_fwd", grid=(n_heads, s // t),
        in_specs=_qkv_specs(s, t, n_heads, base),
        out_specs=pl.BlockSpec((t, HEAD_DIM), lambda h, i: (i, h)),
        out_shape=jax.ShapeDtypeStruct((s, n_heads * HEAD_DIM), BF16),
        compiler_params=_params("parallel", "arbitrary"),
    )(qkv, qkv, qkv)


def sb_bwd(qkv, d_o, n_heads, base):
    s = qkv.shape[0]
    t = _tile(s, ATT_TILE, LANES)
    nq = s // t
    scale = HEAD_DIM ** -0.5

    def kern(q_ref, k_ref, v_ref, do_ref, dq_ref, dk_ref, dv_ref, dk_acc, dv_acc, run_ref):
        i = pl.program_id(1)

        @pl.when(i == 0)
        def _():
            dk_acc[...] = jnp.zeros_like(dk_acc)
            dv_acc[...] = jnp.zeros_like(dv_acc)

        row, col = _iota2(t)
        after = _mask01(row > col)
        before = _mask01(row < col)
        q = q_ref[...]
        do = do_ref[...]

        def sweep1(jj, run):
            j = i - jj
            off = pl.multiple_of(j * t, t)
            _, _, _, log_keep = _sb_scores(q, k_ref[pl.ds(off, t), :], j, i, t, 0)
            run_ref[j] = run
            return run + jnp.sum(log_keep, axis=1, keepdims=True)

        lax.fori_loop(0, i + 1, sweep1, jnp.zeros((t, 1), F32))

        def sweep2(j, carry):
            run_e, dq = carry
            off = pl.multiple_of(j * t, t)
            k = k_ref[pl.ds(off, t), :]
            v = v_ref[pl.ds(off, t), :]
            z, valid, l1p, log_keep = _sb_scores(q, k, j, i, t, 0)
            between = _dot3_right(log_keep, after) + run_ref[j]
            w = jnp.where(valid, jnp.exp(jnp.minimum(z, 0.0) - l1p + between), 0.0)
            dw = lax.dot_general(do, v, NT, preferred_element_type=F32)
            e = dw * w
            e_before = _dot3_right(e, before) + run_e
            keep = jnp.exp(log_keep)
            dz = jnp.where(valid, e * keep - e_before * (1.0 - keep), 0.0) * scale
            dzb = dz.astype(BF16)
            dq = dq + jnp.dot(dzb, k, preferred_element_type=F32)
            dk_acc[pl.ds(off, t), :] += lax.dot_general(dzb, q, TN, preferred_element_type=F32)
            dv_acc[pl.ds(off, t), :] += lax.dot_general(w.astype(BF16), do, TN, preferred_element_type=F32)
            return run_e + jnp.sum(e, axis=1, keepdims=True), dq

        _, dq = lax.fori_loop(0, i + 1, sweep2, (jnp.zeros((t, 1), F32), jnp.zeros((t, HEAD_DIM), F32)))
        dq_ref[...] = dq.astype(BF16)

        @pl.when(i == nq - 1)
        def _():
            dk_ref[...] = dk_acc[...].astype(BF16)
            dv_ref[...] = dv_acc[...].astype(BF16)

    out = jax.ShapeDtypeStruct((s, n_heads * HEAD_DIM), BF16)
    head_blk = pl.BlockSpec((s, HEAD_DIM), lambda h, i: (0, h))
    tile_blk = pl.BlockSpec((t, HEAD_DIM), lambda h, i: (i, h))
    return pl.pallas_call(
        kern, name="sb_bwd", grid=(n_heads, nq),
        in_specs=_qkv_specs(s, t, n_heads, base) + [tile_blk],
        out_specs=[tile_blk, head_blk, head_blk],
        out_shape=[out, out, out],
        scratch_shapes=[pltpu.VMEM((s, HEAD_DIM), F32), pltpu.VMEM((s, HEAD_DIM), F32), pltpu.VMEM((nq, t, 1), F32)],
        compiler_params=_params("parallel", "arbitrary"),
    )(qkv, qkv, qkv, d_o)


def _fox_scores(q, k, cq, ck, j, i, t):
    row, col = _iota2(t)
    sc = lax.dot_general(q, k, NT, preferred_element_type=F32) * (HEAD_DIM ** -0.5)
    sc = sc + cq - ck
    valid = (col - row) < jnp.where(j < i, t, 1)
    return jnp.where(valid, sc, NEG_BIG), valid


def fox_fwd(qkv, cum_col, cum_row, n_heads, base):
    s = qkv.shape[0]
    t = _tile(s, ATT_TILE, LANES)

    def kern(q_ref, k_ref, v_ref, cq_ref, ck_ref, o_ref, lse_ref):
        i = pl.program_id(1)
        q = q_ref[...]
        cq = cq_ref[0]

        def body(j, carry):
            m, l, acc = carry
            off = pl.multiple_of(j * t, t)
            sc, _ = _fox_scores(q, k_ref[pl.ds(off, t), :], cq, ck_ref[0, :, pl.ds(off, t)], j, i, t)
            m_new = jnp.maximum(m, jnp.max(sc, axis=1, keepdims=True))
            p = jnp.exp(sc - m_new)
            alpha = jnp.exp(m - m_new)
            l = alpha * l + jnp.sum(p, axis=1, keepdims=True)
            acc = alpha * acc + jnp.dot(p.astype(BF16), v_ref[pl.ds(off, t), :], preferred_element_type=F32)
            return m_new, l, acc

        m, l, acc = lax.fori_loop(0, i + 1, body, (jnp.full((t, 1), NEG_BIG, F32), jnp.zeros((t, 1), F32),
                                                   jnp.zeros((t, HEAD_DIM), F32)))
        o_ref[...] = acc / l
        lse_ref[0] = m + jnp.log(l)

    col_blk = pl.BlockSpec((1, t, 1), lambda h, i: (h, i, 0))
    return pl.pallas_call(
        kern, name="fox_fwd", grid=(n_heads, s // t),
        in_specs=_qkv_specs(s, t, n_heads, base) + [col_blk, pl.BlockSpec((1, 1, s), lambda h, i: (h, 0, 0))],
        out_specs=[pl.BlockSpec((t, HEAD_DIM), lambda h, i: (i, h)), col_blk],
        out_shape=[jax.ShapeDtypeStruct((s, n_heads * HEAD_DIM), F32), jax.ShapeDtypeStruct((n_heads, s, 1), F32)],
        compiler_params=_params("parallel", "arbitrary"),
    )(qkv, qkv, qkv, cum_col, cum_row)


def fox_bwd(qkv, cum_col, cum_row, o, d_o, lse, n_heads, base):
    s = qkv.shape[0]
    t = _tile(s, ATT_TILE, LANES)
    nq = s // t
    scale = HEAD_DIM ** -0.5

    def kern(q_ref, k_ref, v_ref, cq_ref, ck_ref, o_ref, do_ref, lse_ref,
             dq_ref, dk_ref, dv_ref, dcq_ref, dck_ref, dk_acc, dv_acc, dck_acc):
        i = pl.program_id(1)

        @pl.when(i == 0)
        def _():
            dk_acc[...] = jnp.zeros_like(dk_acc)
            dv_acc[...] = jnp.zeros_like(dv_acc)
            dck_acc[...] = jnp.zeros_like(dck_acc)

        q = q_ref[...]
        do = do_ref[...]
        cq = cq_ref[0]
        lse = lse_ref[0]
        delta = jnp.sum(do.astype(F32) * o_ref[...], axis=1, keepdims=True)

        def body(j, carry):
            dq, dcq = carry
            off = pl.multiple_of(j * t, t)
            k = k_ref[pl.ds(off, t), :]
            v = v_ref[pl.ds(off, t), :]
            sc, valid = _fox_scores(q, k, cq, ck_ref[0, :, pl.ds(off, t)], j, i, t)
            p = jnp.where(valid, jnp.exp(sc - lse), 0.0)
            dp = lax.dot_general(do, v, NT, preferred_element_type=F32)
            ds = p * (dp - delta)
            dsb = (ds * scale).astype(BF16)
            dq = dq + jnp.dot(dsb, k, preferred_element_type=F32)
            dk_acc[pl.ds(off, t), :] += lax.dot_general(dsb, q, TN, preferred_element_type=F32)
            dv_acc[pl.ds(off, t), :] += lax.dot_general(p.astype(BF16), do, TN, preferred_element_type=F32)
            dck_acc[:, pl.ds(off, t)] -= jnp.sum(ds, axis=0, keepdims=True)
            return dq, dcq + jnp.sum(ds, axis=1, keepdims=True)

        dq, dcq = lax.fori_loop(0, i + 1, body, (jnp.zeros((t, HEAD_DIM), F32), jnp.zeros((t, 1), F32)))
        dq_ref[...] = dq.astype(BF16)
        dcq_ref[0] = dcq

        @pl.when(i == nq - 1)
        def _():
            dk_ref[...] = dk_acc[...].astype(BF16)
            dv_ref[...] = dv_acc[...].astype(BF16)
            dck_ref[0] = dck_acc[...]

    out = jax.ShapeDtypeStruct((s, n_heads * HEAD_DIM), BF16)
    head_blk = pl.BlockSpec((s, HEAD_DIM), lambda h, i: (0, h))
    tile_blk = pl.BlockSpec((t, HEAD_DIM), lambda h, i: (i, h))
    col_blk = pl.BlockSpec((1, t, 1), lambda h, i: (h, i, 0))
    row_blk = pl.BlockSpec((1, 1, s), lambda h, i: (h, 0, 0))
    return pl.pallas_call(
        kern, name="fox_bwd", grid=(n_heads, nq),
        in_specs=_qkv_specs(s, t, n_heads, base) + [col_blk, row_blk, tile_blk, tile_blk, col_blk],
        out_specs=[tile_blk, head_blk, head_blk, col_blk, row_blk],
        out_shape=[out, out, out, jax.ShapeDtypeStruct((n_heads, s, 1), F32),
                   jax.ShapeDtypeStruct((n_heads, 1, s), F32)],
        scratch_shapes=[pltpu.VMEM((s, HEAD_DIM), F32), pltpu.VMEM((s, HEAD_DIM), F32), pltpu.VMEM((1, s), F32)],
        compiler_params=_params("parallel", "arbitrary"),
    )(qkv, qkv, qkv, cum_col, cum_row, o, d_o, lse)


def _place():
    x, y, c = lax.axis_index("x"), lax.axis_index("y"), lax.axis_index("c")
    other_chips = [(1 - x, y), (x, 1 - y), (1 - x, 1 - y)]
    return x, y, c, other_chips


ANY = pl.BlockSpec(memory_space=pl.ANY)


def _remote(src, dst, send_sem, recv_sem, dev):
    return pltpu.make_async_remote_copy(src_ref=src, dst_ref=dst, send_sem=send_sem, recv_sem=recv_sem,
                                        device_id=dev, device_id_type=MESH)


def cast_place(name, ws, chip):
    r = ws[0].shape[0]
    cs = [w.shape[1] for w in ws]
    tr = _tile(r, 256, 16)

    def kern(chip_ref, *refs):
        o_ref = refs[-1]
        off = 0
        for w_ref, c in zip(refs[:-1], cs):
            o_ref[:, off:off + c] = w_ref[...].astype(BF16)
            off += c

    return pl.pallas_call(
        kern, name=name,
        grid_spec=pltpu.PrefetchScalarGridSpec(
            num_scalar_prefetch=1, grid=(r // tr,),
            in_specs=[pl.BlockSpec((tr, c), lambda i, chip_ref: (i, 0)) for c in cs],
            out_specs=pl.BlockSpec((None, tr, sum(cs)), lambda i, chip_ref: (chip_ref[0], i, 0))),
        out_shape=jax.ShapeDtypeStruct((N_CHIPS, r, sum(cs)), BF16),
        compiler_params=_params("parallel"),
    )(chip, *ws)


def all_gather(bufs):
    n = len(bufs)
    halves = [b.shape[1] // 2 for b in bufs]

    def body(*refs):
        outs = refs[n:2 * n]
        ici_send, ici_recv, d2d_send, d2d_recv = refs[2 * n:]
        x, y, c, chips = _place()
        me = 2 * x + y
        sibling = (x, y, 1 - c)

        def rows(a, k, core):
            return outs[a].at[k, pl.ds(pl.multiple_of(core * halves[a], 16), halves[a])]

        def chip_of(j):
            return 2 * chips[j][0] + chips[j][1]

        def ici(a, j, k):
            return _remote(rows(a, k, c), rows(a, k, c), ici_send.at[3 * a + j], ici_recv.at[3 * a + j],
                           (chips[j][0], chips[j][1], c))

        def d2d(a, j, core):
            return _remote(rows(a, chip_of(j), core), rows(a, chip_of(j), core), d2d_send.at[3 * a + j],
                           d2d_recv.at[3 * a + j], sibling)

        pairs = [(a, j) for a in range(n) for j in range(3)]
        for a, j in pairs:
            ici(a, j, me).start()
        for a, j in pairs:
            ici(a, j, chip_of(j)).wait_recv()
            d2d(a, j, c).start()
        for a, j in pairs:
            d2d(a, j, 1 - c).wait_recv()
        for a, j in pairs:
            ici(a, j, me).wait_send()
            d2d(a, j, c).wait_send()

    return pl.pallas_call(
        body, name="all_gather", in_specs=[ANY] * n, out_specs=[ANY] * n,
        out_shape=[jax.ShapeDtypeStruct(b.shape, b.dtype) for b in bufs],
        input_output_aliases={a: a for a in range(n)},
        scratch_shapes=[pltpu.SemaphoreType.DMA((3 * n,)) for _ in range(4)],
    )(*bufs)


def swap_halves(pieces):
    n = len(pieces)
    halves = [p.shape[1] // 2 for p in pieces]

    def body(*refs):
        ins, outs = refs[:n], refs[n:2 * n]
        send_sems, recv_sems = refs[2 * n:]
        x, y, c, _ = _place()
        cps = [_remote(ins[a].at[:, pl.ds(pl.multiple_of((1 - c) * halves[a], 16), halves[a]), :], outs[a],
                       send_sems.at[a], recv_sems.at[a], (x, y, 1 - c)) for a in range(n)]
        for cp in cps:
            cp.start()
        for cp in cps:
            cp.wait()

    return pl.pallas_call(
        body, name="swap_halves", in_specs=[ANY] * n, out_specs=[ANY] * n,
        out_shape=[jax.ShapeDtypeStruct((N_CHIPS, h, p.shape[2]), p.dtype) for p, h in zip(pieces, halves)],
        scratch_shapes=[pltpu.SemaphoreType.DMA((n,)), pltpu.SemaphoreType.DMA((n,))],
    )(*pieces)


def pair_sum(name, pieces, got, core):
    _, r, w = pieces.shape
    half = r // 2
    tr = _tile(half, 256, 16)

    def kern(core_ref, p_ref, g_ref, o_ref):
        o_ref[...] = (p_ref[...].astype(F32) + g_ref[...].astype(F32)).astype(o_ref.dtype)

    return pl.pallas_call(
        kern, name=name,
        grid_spec=pltpu.PrefetchScalarGridSpec(
            num_scalar_prefetch=1, grid=(N_CHIPS, half // tr),
            in_specs=[pl.BlockSpec((None, None, tr, w), lambda k, i, core_ref: (k, core_ref[0], i, 0)),
                      pl.BlockSpec((None, tr, w), lambda k, i, core_ref: (k, i, 0))],
            out_specs=pl.BlockSpec((None, tr, w), lambda k, i, core_ref: (k, i, 0))),
        out_shape=jax.ShapeDtypeStruct((N_CHIPS, half, w), pieces.dtype),
        compiler_params=_params("parallel", "parallel"),
    )(core, pieces.reshape(N_CHIPS, 2, half, w), got)


def scatter_chips(sums):
    n = len(sums)

    def body(*refs):
        ins, outs = refs[:n], refs[n:2 * n]
        send_sems, recv_sems = refs[2 * n:]
        x, y, c, chips = _place()
        cps = [_remote(ins[a].at[2 * chips[j][0] + chips[j][1]], outs[a].at[j], send_sems.at[3 * a + j],
                       recv_sems.at[3 * a + j], (chips[j][0], chips[j][1], c))
               for a in range(n) for j in range(3)]
        for cp in cps:
            cp.start()
        for cp in cps:
            cp.wait()

    return pl.pallas_call(
        body, name="scatter_chips", in_specs=[ANY] * n, out_specs=[ANY] * n,
        out_shape=[jax.ShapeDtypeStruct((3,) + t.shape[1:], t.dtype) for t in sums],
        scratch_shapes=[pltpu.SemaphoreType.DMA((3 * n,)), pltpu.SemaphoreType.DMA((3 * n,))],
    )(*sums)


def chip_sum(name, sums, got, chip, core):
    _, half, w = sums.shape
    tr = _tile(half, 256, 16)
    nb = half // tr

    def kern(ids_ref, s_ref, g0_ref, g1_ref, g2_ref, o_ref):
        o_ref[...] = ((s_ref[...].astype(F32) + g0_ref[...].astype(F32)) + g1_ref[...].astype(F32)) \
            + g2_ref[...].astype(F32)

    def got_spec(j):
        return pl.BlockSpec((None, tr, w), lambda i, ids_ref: (j, i, 0))

    return pl.pallas_call(
        kern, name=name,
        grid_spec=pltpu.PrefetchScalarGridSpec(
            num_scalar_prefetch=1, grid=(nb,),
            in_specs=[pl.BlockSpec((None, tr, w), lambda i, ids_ref: (ids_ref[0], i, 0)),
                      got_spec(0), got_spec(1), got_spec(2)],
            out_specs=pl.BlockSpec((tr, w), lambda i, ids_ref: (ids_ref[1] * nb + i, 0))),
        out_shape=jax.ShapeDtypeStruct((2 * half, w), F32),
        compiler_params=_params("parallel"),
    )(jnp.concatenate([chip, core]), sums, got, got, got)


def join_halves(shards):
    n = len(shards)
    halves = [g.shape[0] // 2 for g in shards]

    def body(*refs):
        outs = refs[n:2 * n]
        send_sems, recv_sems = refs[2 * n:]
        x, y, c, _ = _place()
        cps = []
        for a in range(n):
            rows = outs[a].at[pl.ds(pl.multiple_of(c * halves[a], 8), halves[a])]
            cps.append(_remote(rows, rows, send_sems.at[a], recv_sems.at[a], (x, y, 1 - c)))
        for cp in cps:
            cp.start()
        for cp in cps:
            cp.wait()

    return pl.pallas_call(
        body, name="join_halves", in_specs=[ANY] * n, out_specs=[ANY] * n,
        out_shape=[jax.ShapeDtypeStruct(g.shape, g.dtype) for g in shards],
        input_output_aliases={a: a for a in range(n)},
        scratch_shapes=[pltpu.SemaphoreType.DMA((n,)), pltpu.SemaphoreType.DMA((n,))],
    )(*shards)


def _adam(w, g, m, v):
    m = ADAM_B1 * m + (1.0 - ADAM_B1) * g
    v = ADAM_B2 * v + (1.0 - ADAM_B2) * (g * g)
    m_hat = m / (1.0 - ADAM_B1 ** ADAM_STEP)
    v_hat = v / (1.0 - ADAM_B2 ** ADAM_STEP)
    delta = -ADAM_LR * (m_hat / (jnp.sqrt(v_hat) + ADAM_EPS) + ADAM_WD * w)
    return delta, m, v


def small_allreduce_adam(g_part, w, m, v):
    n_dev = 8
    r, d = g_part.shape

    def body(g_ref, w_ref, m_ref, v_ref, gs_ref, dl_ref, nm_ref, nv_ref, all_ref, send_sems, recv_sems):
        x, y, c, _ = _place()
        me = 4 * x + 2 * y + c
        all_ref[me] = g_ref[...]
        cps = []
        for rel in range(1, n_dev):
            px = 1 - x if rel & 4 else x
            py = 1 - y if rel & 2 else y
            pc = 1 - c if rel & 1 else c
            cps.append(_remote(g_ref, all_ref.at[me], send_sems.at[rel - 1], recv_sems.at[rel - 1], (px, py, pc)))
        for cp in cps:
            cp.start()
        for cp in cps:
            cp.wait()
        total = all_ref[0]
        for dev in range(1, n_dev):
            total = total + all_ref[dev]
        gs_ref[...] = total
        delta, nm, nv = _adam(w_ref[...], total, m_ref[...], v_ref[...])
        dl_ref[...] = delta
        nm_ref[...] = nm
        nv_ref[...] = nv

    vm = pl.BlockSpec(memory_space=pltpu.VMEM)
    out = jax.ShapeDtypeStruct((r, d), F32)
    return pl.pallas_call(
        body, name="small_allreduce_adam", in_specs=[vm, vm, vm, vm], out_specs=[vm, vm, vm, vm],
        out_shape=[out, out, out, out],
        scratch_shapes=[pltpu.VMEM((n_dev, r, d), F32), pltpu.SemaphoreType.DMA((n_dev - 1,)),
                        pltpu.SemaphoreType.DMA((n_dev - 1,))],
    )(g_part, w, m, v)


def adam_update(name, w, m, v, g_buf, col_blk, copy_g):
    r, c = w.shape
    tr = _tile(r, 128, 8)

    def kern(w_ref, m_ref, v_ref, g_ref, *outs):
        g = g_ref[...]
        delta, nm, nv = _adam(w_ref[...], g, m_ref[...], v_ref[...])
        if copy_g:
            outs[0][...] = g
        outs[-3][...] = delta
        outs[-2][...] = nm
        outs[-1][...] = nv

    blk = pl.BlockSpec((tr, c), lambda i: (i, 0))
    out = jax.ShapeDtypeStruct((r, c), F32)
    n_out = 4 if copy_g else 3
    return pl.pallas_call(
        kern, name=name, grid=(r // tr,),
        in_specs=[blk, blk, blk, pl.BlockSpec((tr, c), lambda i: (i, col_blk))],
        out_specs=[blk] * n_out, out_shape=[out] * n_out, compiler_params=_params("parallel"),
    )(w, m, v, g_buf)


def _col_pieces(slab_of, lo, hi, cw):
    out = []
    while lo < hi:
        k, a = divmod(lo, cw)
        b = min(cw, a + hi - lo)
        out.append(slab_of(k)[:, a:b])
        lo += b - a
    return out


def kernel(x, norm_mix_pre, norm_mix_post, w_in, b_forget, w_branch_sb, w_branch_fox, w_out, norm_ffn_pre, norm_ffn_post, w_ffn_gate, w_ffn_up, w_ffn_down, loss_target, m_norm_mix_pre, m_norm_mix_post, m_w_in, m_b_forget, m_w_branch_sb, m_w_branch_fox, m_w_out, m_norm_ffn_pre, m_norm_ffn_post, m_w_ffn_gate, m_w_ffn_up, m_w_ffn_down, v_norm_mix_pre, v_norm_mix_post, v_w_in, v_b_forget, v_w_branch_sb, v_w_branch_fox, v_w_out, v_norm_ffn_pre, v_norm_ffn_post, v_w_ffn_gate, v_w_ffn_up, v_w_ffn_down):
    s, d = x.shape[1], x.shape[2]
    n_heads = b_forget.shape[1]
    d_att = n_heads * HEAD_DIM
    c_in = w_in.shape[2]
    c_br = w_branch_sb.shape[2]
    c_gu = w_ffn_gate.shape[2]
    d_ff = c_gu * N_CHIPS
    d_in = c_in * N_CHIPS
    f_pad = 512
    n_qkv = 6 * d_att
    n_gf = 2 * d + f_pad
    core = lax.axis_index("c").astype(jnp.int32).reshape(1)
    chip = (2 * lax.axis_index("x") + lax.axis_index("y")).astype(jnp.int32).reshape(1)

    g_in, g_br, g_out, g_gu, g_dn = all_gather([
        cast_place("place_w_in", [w_in[0]], chip),
        cast_place("place_branch", [w_branch_sb[0], w_branch_fox[0]], chip),
        cast_place("place_out", [w_out[0]], chip),
        cast_place("place_gate_up", [w_ffn_gate[0], w_ffn_up[0]], chip),
        cast_place("place_down", [w_ffn_down[0]], chip)])
    slab = lambda k: g_in[k]
    w_main = jnp.concatenate(
        _col_pieces(slab, 0, n_qkv, c_in) + _col_pieces(slab, n_qkv + n_heads, d_in, c_in)
        + _col_pieces(slab, n_qkv, n_qkv + n_heads, c_in) + [jnp.zeros((d, f_pad - n_heads), BF16)], axis=1)
    w_o = g_out.reshape(d, d)
    w_dn = g_dn.reshape(d_ff, d)

    x2 = x[0]
    tgt = loss_target[0]
    b_pad = jnp.pad(b_forget, ((0, 0), (0, LANES - n_heads)))

    u = norm_in(x2, norm_mix_pre)
    qkv = mm(u, w_main, "nn", BF16, "proj_qkv", b_win=(0, n_qkv))
    gf = mm(u, w_main, "nn", F32, "proj_gates", b_win=(n_qkv, n_gf))
    cum = cum_fwd(gf, b_pad, 2 * d)
    cum_heads = cum[:, :n_heads].T
    cum_col, cum_row = cum_heads[:, :, None], cum_heads[:, None, :]
    o_sb = sb_fwd(qkv, n_heads, 0)
    o_fx, lse = fox_fwd(qkv, cum_col, cum_row, n_heads, 3 * n_heads)
    bsb = mm(o_sb, g_br, "nn", F32, "branch_sb", tn=c_br, chunks=(1, 0))
    bfx = mm(o_fx, g_br, "nn", F32, "branch_fox", tn=c_br, chunks=(1, 1))
    merged = gate_fwd(bsb, bfx, gf)
    mix = mm(merged, w_o, "nn", F32, "out_proj")
    h1, u2 = mid_fwd(x2, mix, norm_mix_post, norm_ffn_pre)
    gu = mm(u2, g_gu, "nn", F32, "ffn_gate_up", tn=c_gu, chunks=(2, 0))
    act = swiglu_fwd(gu, c_gu)
    ff = mm(act, w_dn, "nn", F32, "ffn_down")
    dy, d_ff_out, dg_fpost, loss_part = loss_head(h1, ff, norm_ffn_post, tgt)

    p_dn = mm(act, d_ff_out, "tn", BF16, "dw_ffn_down").reshape(N_CHIPS, d_ff // N_CHIPS, d)
    d_act = mm(d_ff_out, w_dn, "nt", F32, "d_act")
    d_gu = swiglu_bwd(d_act, gu, c_gu)
    p_gu = mm(u2, d_gu, "tn", BF16, "dw_ffn_gate_up", tn=c_gu, chunks=(2, 0),
              out_into=lax.empty((N_CHIPS, d, 2 * c_gu), BF16))
    du2 = mm(d_gu, g_gu, "nt", F32, "d_u2", tk=c_gu, chunks=(2, 0))
    dh1, d_mix, dg_fpre, dg_post = mid_bwd(dy, du2, h1, mix, norm_ffn_pre, norm_mix_post)
    p_out = mm(merged, d_mix, "tn", BF16, "dw_out").reshape(N_CHIPS, d // N_CHIPS, d)
    d_merged = mm(d_mix, w_o, "nt", F32, "d_merged")
    d_bsb, d_bfx, d_gs, d_gx = gate_bwd(d_merged, bsb, bfx, gf)
    p_br = mm(o_sb, d_bsb, "tn", BF16, "dw_branch_sb", tn=c_br, chunks=(1, 0),
              out_into=lax.empty((N_CHIPS, d_att, 2 * c_br), BF16))
    p_br = mm(o_fx, d_bfx, "tn", BF16, "dw_branch_fox", tn=c_br, chunks=(1, 1), out_into=p_br)
    d_osb = mm(d_bsb, g_br, "nt", BF16, "d_o_sb", tk=c_br, chunks=(1, 0))
    d_ofx = mm(d_bfx, g_br, "nt", BF16, "d_o_fox", tk=c_br, chunks=(1, 1))
    dq_s, dk_s, dv_s = sb_bwd(qkv, d_osb, n_heads, 0)
    dq_f, dk_f, dv_f, dcq, dck = fox_bwd(qkv, cum_col, cum_row, o_fx, d_ofx, lse, n_heads, 3 * n_heads)
    d_cum = jnp.pad((dcq[:, :, 0] + dck[:, 0, :]).T, ((0, 0), (0, LANES - n_heads)))
    d_f, db_pad = cum_bwd(d_cum, gf, b_pad, 2 * d, n_heads)
    d_main = jnp.concatenate(
        [dq_s, dk_s, dv_s, dq_f, dk_f, dv_f, d_gs, d_gx, d_f, jnp.zeros((s, f_pad - LANES), BF16)], axis=1)
    dw_main = mm(u, d_main, "tn", BF16, "dw_in")
    du = mm(d_main, w_main, "nt", F32, "d_u")
    dx, dg_pre = in_bwd(dh1, du, x2, norm_mix_pre)

    def main_cols(lo, hi):
        out = []
        for s0, s1, m0 in [(0, n_qkv, 0), (n_qkv, n_qkv + n_heads, n_qkv + 2 * d), (n_qkv + n_heads, d_in, n_qkv)]:
            a, b = max(lo, s0), min(hi, s1)
            if a < b:
                out.append(dw_main[:, m0 + a - s0:m0 + b - s0])
        return out

    p_in = jnp.stack([jnp.concatenate(main_cols(k * c_in, (k + 1) * c_in), axis=1) for k in range(N_CHIPS)])

    pieces = [p_in, p_br, p_out, p_gu, p_dn]
    tags = ["in", "branch", "out", "gate_up", "down"]
    from_sibling = swap_halves(pieces)
    sums = [pair_sum("pair_sum_" + t, p, q, core) for t, p, q in zip(tags, pieces, from_sibling)]
    from_chips = scatter_chips(sums)
    gr_in, gr_br, gr_out, gr_gu, gr_dn = join_halves(
        [chip_sum("chip_sum_" + t, sm, got, chip, core) for t, sm, got in zip(tags, sums, from_chips)])

    d_in_, m_in_, v_in_ = adam_update("adam_w_in", w_in[0], m_w_in[0], v_w_in[0], gr_in, 0, False)
    g_bs, d_bs, m_bs, v_bs = adam_update("adam_branch_sb", w_branch_sb[0], m_w_branch_sb[0], v_w_branch_sb[0], gr_br, 0, True)
    g_bf, d_bf, m_bf, v_bf = adam_update("adam_branch_fox", w_branch_fox[0], m_w_branch_fox[0], v_w_branch_fox[0], gr_br, 1, True)
    d_o_, m_o_, v_o_ = adam_update("adam_out", w_out[0], m_w_out[0], v_w_out[0], gr_out, 0, False)
    g_ga, d_ga, m_ga, v_ga = adam_update("adam_gate", w_ffn_gate[0], m_w_ffn_gate[0], v_w_ffn_gate[0], gr_gu, 0, True)
    g_up, d_up, m_up, v_up = adam_update("adam_up", w_ffn_up[0], m_w_ffn_up[0], v_w_ffn_up[0], gr_gu, 1, True)
    d_dn, m_dn, v_dn = adam_update("adam_down", w_ffn_down[0], m_w_ffn_down[0], v_w_ffn_down[0], gr_dn, 0, False)
    lead = lambda arrs: [a[None] for a in arrs]
    grads = lead([gr_in, g_bs, g_bf, gr_out, g_ga, g_up, gr_dn])
    deltas = lead([d_in_, d_bs, d_bf, d_o_, d_ga, d_up, d_dn])
    new_ms = lead([m_in_, m_bs, m_bf, m_o_, m_ga, m_up, m_dn])
    new_vs = lead([v_in_, v_bs, v_bf, v_o_, v_ga, v_up, v_dn])

    def pack(rows):
        rows = [jnp.pad(r_, ((0, 0), (0, d - r_.shape[1]))) for r_ in rows]
        return jnp.concatenate(rows + [jnp.zeros((8 - len(rows), d), F32)], axis=0)

    sm_g, sm_d, sm_m, sm_v = small_allreduce_adam(
        pack([dg_pre, dg_post, dg_fpre, dg_fpost, db_pad]),
        pack([norm_mix_pre, norm_mix_post, norm_ffn_pre, norm_ffn_post, b_forget]),
        pack([m_norm_mix_pre, m_norm_mix_post, m_norm_ffn_pre, m_norm_ffn_post, m_b_forget]),
        pack([v_norm_mix_pre, v_norm_mix_post, v_norm_ffn_pre, v_norm_ffn_post, v_b_forget]))

    def small(a):
        return [a[0:1], a[1:2], a[2:3], a[3:4], a[4:5, :n_heads]]

    def ordered(sm, bg):
        return [sm[0], sm[1], bg[0], sm[4], bg[1], bg[2], bg[3], sm[2], sm[3], bg[4], bg[5], bg[6]]

    loss = lax.psum(loss_part[0, 0], ("x", "y", "c"))
    return (loss, dx[None], *ordered(small(sm_g), grads), *ordered(small(sm_d), deltas),
            *ordered(small(sm_m), new_ms), *ordered(small(sm_v), new_vs))
```

```python
import functools

import jax
import jax.numpy as jnp
from jax import lax
from jax.experimental import pallas as pl
from jax.experimental.pallas import tpu as pltpu

F32 = jnp.float32
BF16 = jnp.bfloat16
MESH = pl.DeviceIdType.MESH

HEAD_DIM = 128
LANES = 128
ATT_TILE = 256
ROW_TILE = 256
N_CHIPS = 4
RMS_EPS = 1e-6
ADAM_LR = 0.001
ADAM_B1 = 0.9
ADAM_B2 = 0.999
ADAM_EPS = 1e-08
ADAM_WD = 0.01
ADAM_STEP = 10
NEG_BIG = -1e30
VMEM_LIMIT = 56 * 1024 * 1024
MM_VMEM_BUDGET = 40 * 1024 * 1024
ATT_STRIP = 128

NN = (((1,), (0,)), ((), ()))
NT = (((1,), (1,)), ((), ()))
TN = (((0,), (0,)), ((), ()))


def _tile(n, pref, align):
    best = None
    t = align
    while t <= min(n, pref):
        if n % t == 0:
            best = t
        t += align
    return n if best is None else best


def _params(*sem):
    return pltpu.CompilerParams(dimension_semantics=sem, vmem_limit_bytes=VMEM_LIMIT)


def _mm_tiles(m, n, k, a_bytes, b_bytes, out_bytes, tn, tk):
    tm = _tile(m, 2048, LANES)
    tk = tk or _tile(k, 512, LANES)

    def vmem(t):
        acc = 0 if out_bytes == 4 else tm * t * 4
        return acc + 2 * tm * t * out_bytes + 2 * (tm * tk * a_bytes + tk * t * b_bytes)

    if tn is None:
        fits = [t for t in range(LANES, min(n, 2048) + 1, LANES) if n % t == 0 and vmem(t) <= MM_VMEM_BUDGET]
        tn = max(fits) if fits else _tile(n, LANES, LANES)
    return tm, tn, tk


def mm(a, b, mode, out_dtype, name, *, tn=None, tk=None, b_win=None, chunks=None, out_into=None):
    n_per, blk0 = chunks if chunks else (1, 0)
    if mode == "nn":
        m, k = a.shape
        n = b.shape[0] * n_per * tn if chunks else (b_win[1] if b_win else b.shape[1])
    elif mode == "nt":
        m = a.shape[0]
        k = b.shape[0] * n_per * tk if chunks else a.shape[1]
        n = b.shape[-2]
    else:
        k, m = a.shape
        n = b.shape[1]
    in_place = jnp.dtype(out_dtype) == jnp.dtype(F32)
    tm, tn, tk = _mm_tiles(m, n, k, a.dtype.itemsize, b.dtype.itemsize, jnp.dtype(out_dtype).itemsize, tn, tk)
    assert m % tm == 0 and n % tn == 0 and k % tk == 0, (name, m, n, k, tm, tn, tk)
    j0 = 0
    if b_win:
        assert b_win[0] % tn == 0
        j0 = b_win[0] // tn
    nk = k // tk
    dims = {"nn": NN, "nt": NT, "tn": TN}[mode]

    def kern(a_ref, b_ref, *rest):
        o_ref, acc_ref = (rest[-1], rest[-1]) if in_place else (rest[-2], rest[-1])
        kk = pl.program_id(2)

        @pl.when(kk == 0)
        def _():
            acc_ref[...] = jnp.zeros_like(acc_ref)

        acc_ref[...] += lax.dot_general(a_ref[...].astype(BF16), b_ref[...].astype(BF16), dims,
                                        preferred_element_type=F32)

        if not in_place:
            @pl.when(kk == nk - 1)
            def _():
                o_ref[...] = acc_ref[...].astype(o_ref.dtype)

    out_spec = pl.BlockSpec((tm, tn), lambda i, j, kk: (i, j))
    out_shape = jax.ShapeDtypeStruct((m, n), out_dtype)
    if mode == "nn":
        a_spec = pl.BlockSpec((tm, tk), lambda i, j, kk: (i, kk))
        if chunks:
            b_spec = pl.BlockSpec((None, tk, tn), lambda i, j, kk: (j // n_per, kk, blk0 + j % n_per))
        else:
            b_spec = pl.BlockSpec((tk, tn), lambda i, j, kk: (kk, j + j0))
    elif mode == "nt":
        a_spec = pl.BlockSpec((tm, tk), lambda i, j, kk: (i, kk))
        if chunks:
            b_spec = pl.BlockSpec((None, tn, tk), lambda i, j, kk: (kk // n_per, j, blk0 + kk % n_per))
        else:
            b_spec = pl.BlockSpec((tn, tk), lambda i, j, kk: (j, kk))
    else:
        a_spec = pl.BlockSpec((tk, tm), lambda i, j, kk: (kk, i))
        b_spec = pl.BlockSpec((tk, tn), lambda i, j, kk: (kk, j))
        if chunks:
            out_spec = pl.BlockSpec((None, tm, tn), lambda i, j, kk: (j // n_per, i, blk0 + j % n_per))
    in_specs, operands, aliases = [a_spec, b_spec], [a, b], {}
    if chunks and mode == "tn":
        assert out_into is not None
        out_shape = jax.ShapeDtypeStruct(out_into.shape, out_dtype)
        in_specs.append(pl.BlockSpec(memory_space=pl.ANY))
        operands.append(out_into)
        aliases = {2: 0}
    return pl.pallas_call(
        kern, name=name, grid=(m // tm, n // tn, nk),
        in_specs=in_specs, out_specs=out_spec, out_shape=out_shape,
        scratch_shapes=[] if in_place else [pltpu.VMEM((tm, tn), F32)], input_output_aliases=aliases,
        compiler_params=_params("parallel", "parallel", "arbitrary"),
    )(*operands)


def _rstd(v):
    return lax.rsqrt(jnp.mean(v * v, axis=-1, keepdims=True) + RMS_EPS)


def _norm_bwd(v, g, dy):
    r = _rstd(v)
    vh = v * r
    dyg = dy * g
    dv = r * (dyg - vh * jnp.mean(dyg * vh, axis=-1, keepdims=True))
    return dv, jnp.sum(dy * vh, axis=0, keepdims=True)


def _row_call(kern, name, ins, outs, s, d):
    tr = _tile(s, ROW_TILE, 16)

    def spec(shape, is_row):
        if is_row:
            return pl.BlockSpec((tr, shape[1]), lambda i: (i, 0))
        return pl.BlockSpec(shape, lambda i: (0, 0))

    return pl.pallas_call(
        kern, name=name, grid=(s // tr,),
        in_specs=[spec(a.shape, r) for a, r in ins],
        out_specs=[spec(sh, r) for sh, _, r in outs],
        out_shape=[jax.ShapeDtypeStruct(sh, dt) for sh, dt, _ in outs],
        compiler_params=_params("arbitrary"),
    )(*[a for a, _ in ins])


def norm_in(x, g):
    s, d = x.shape

    def kern(x_ref, g_ref, u_ref):
        v = x_ref[...]
        u_ref[...] = (v * _rstd(v) * g_ref[...]).astype(BF16)

    return _row_call(kern, "norm_in", [(x, True), (g, False)], [((s, d), BF16, True)], s, d)[0]


def mid_fwd(x, mix, g_post, g_fpre):
    s, d = x.shape

    def kern(x_ref, mix_ref, gp_ref, gf_ref, h1_ref, u2_ref):
        mixv = mix_ref[...]
        h1 = x_ref[...] + mixv * _rstd(mixv) * gp_ref[...]
        h1_ref[...] = h1
        u2_ref[...] = (h1 * _rstd(h1) * gf_ref[...]).astype(BF16)

    return _row_call(kern, "mid_fwd", [(x, True), (mix, True), (g_post, False), (g_fpre, False)],
                     [((s, d), F32, True), ((s, d), BF16, True)], s, d)


def loss_head(h1, ff, g_fpost, target):
    s, d = h1.shape

    def kern(h1_ref, ff_ref, g_ref, t_ref, dy_ref, dff_ref, dg_ref, loss_ref):
        @pl.when(pl.program_id(0) == 0)
        def _():
            dg_ref[...] = jnp.zeros_like(dg_ref)
            loss_ref[...] = jnp.zeros_like(loss_ref)

        ffv = ff_ref[...]
        g = g_ref[...]
        y = h1_ref[...] + ffv * _rstd(ffv) * g
        diff = y - t_ref[...]
        row_loss = jnp.mean(diff * diff, axis=-1, keepdims=True)
        loss_ref[...] += 0.5 * jnp.sum(row_loss, axis=0, keepdims=True)
        dy = diff / d
        dy_ref[...] = dy
        dff, dg = _norm_bwd(ffv, g, dy)
        dff_ref[...] = dff.astype(BF16)
        dg_ref[...] += dg

    return _row_call(kern, "loss_head",
                     [(h1, True), (ff, True), (g_fpost, False), (target, True)],
                     [((s, d), F32, True), ((s, d), BF16, True), ((1, d), F32, False), ((1, 1), F32, False)], s, d)


def mid_bwd(dy, du2, h1, mix, g_fpre, g_post):
    s, d = dy.shape

    def kern(dy_ref, du2_ref, h1_ref, mix_ref, gf_ref, gp_ref, dh1_ref, dmix_ref, dgf_ref, dgp_ref):
        @pl.when(pl.program_id(0) == 0)
        def _():
            dgf_ref[...] = jnp.zeros_like(dgf_ref)
            dgp_ref[...] = jnp.zeros_like(dgp_ref)

        dh, dgf = _norm_bwd(h1_ref[...], gf_ref[...], du2_ref[...])
        dh1 = dy_ref[...] + dh
        dh1_ref[...] = dh1
        dmix, dgp = _norm_bwd(mix_ref[...], gp_ref[...], dh1)
        dmix_ref[...] = dmix.astype(BF16)
        dgf_ref[...] += dgf
        dgp_ref[...] += dgp

    return _row_call(kern, "mid_bwd",
                     [(dy, True), (du2, True), (h1, True), (mix, True), (g_fpre, False), (g_post, False)],
                     [((s, d), F32, True), ((s, d), BF16, True), ((1, d), F32, False), ((1, d), F32, False)], s, d)


def in_bwd(dh1, du, x, g_pre):
    s, d = x.shape

    def kern(dh1_ref, du_ref, x_ref, g_ref, dx_ref, dg_ref):
        @pl.when(pl.program_id(0) == 0)
        def _():
            dg_ref[...] = jnp.zeros_like(dg_ref)

        dxn, dg = _norm_bwd(x_ref[...], g_ref[...], du_ref[...])
        dx_ref[...] = dh1_ref[...] + dxn
        dg_ref[...] += dg

    return _row_call(kern, "in_bwd", [(dh1, True), (du, True), (x, True), (g_pre, False)],
                     [((s, d), F32, True), ((1, d), F32, False)], s, d)


def _sigmoid(v):
    return 1.0 / (1.0 + jnp.exp(-v))


def gate_fwd(bsb, bfx, gf):
    s, d = bsb.shape
    tr, tc = _tile(s, 256, 16), _tile(d, 512, LANES)
    nc = d // tc

    def kern(bsb_ref, bfx_ref, gs_ref, gx_ref, o_ref):
        o_ref[...] = (_sigmoid(gs_ref[...]) * bsb_ref[...] + _sigmoid(gx_ref[...]) * bfx_ref[...]).astype(BF16)

    blk = pl.BlockSpec((tr, tc), lambda i, j: (i, j))
    return pl.pallas_call(
        kern, name="gate_fwd", grid=(s // tr, nc),
        in_specs=[blk, blk, blk, pl.BlockSpec((tr, tc), lambda i, j: (i, j + nc))],
        out_specs=blk, out_shape=jax.ShapeDtypeStruct((s, d), BF16),
        compiler_params=_params("parallel", "parallel"),
    )(bsb, bfx, gf, gf)


def gate_bwd(dmerged, bsb, bfx, gf):
    s, d = bsb.shape
    tr, tc = _tile(s, 256, 16), _tile(d, 512, LANES)
    nc = d // tc

    def kern(dm_ref, bsb_ref, bfx_ref, gs_ref, gx_ref, dbs_ref, dbx_ref, dgs_ref, dgx_ref):
        dm = dm_ref[...]
        ss = _sigmoid(gs_ref[...])
        sx = _sigmoid(gx_ref[...])
        dbs_ref[...] = (dm * ss).astype(BF16)
        dbx_ref[...] = (dm * sx).astype(BF16)
        dgs_ref[...] = (dm * bsb_ref[...] * ss * (1.0 - ss)).astype(BF16)
        dgx_ref[...] = (dm * bfx_ref[...] * sx * (1.0 - sx)).astype(BF16)

    blk = pl.BlockSpec((tr, tc), lambda i, j: (i, j))
    out = jax.ShapeDtypeStruct((s, d), BF16)
    return pl.pallas_call(
        kern, name="gate_bwd", grid=(s // tr, nc),
        in_specs=[blk, blk, blk, blk, pl.BlockSpec((tr, tc), lambda i, j: (i, j + nc))],
        out_specs=[blk, blk, blk, blk], out_shape=[out, out, out, out],
        compiler_params=_params("parallel", "parallel"),
    )(dmerged, bsb, bfx, gf, gf)


def swiglu_fwd(gu, cw):
    s, f2 = gu.shape
    tr = _tile(s, 256, 16)

    def kern(gu_ref, o_ref):
        g = gu_ref[:, :cw]
        o_ref[...] = (g * _sigmoid(g) * gu_ref[:, cw:]).astype(BF16)

    return pl.pallas_call(
        kern, name="swiglu_fwd", grid=(s // tr, f2 // (2 * cw)),
        in_specs=[pl.BlockSpec((tr, 2 * cw), lambda i, j: (i, j))],
        out_specs=pl.BlockSpec((tr, cw), lambda i, j: (i, j)),
        out_shape=jax.ShapeDtypeStruct((s, f2 // 2), BF16),
        compiler_params=_params("parallel", "parallel"),
    )(gu)


def swiglu_bwd(dact, gu, cw):
    s, f2 = gu.shape
    tr = _tile(s, 256, 16)

    def kern(da_ref, gu_ref, o_ref):
        da = da_ref[...]
        g = gu_ref[:, :cw]
        sg = _sigmoid(g)
        o_ref[:, :cw] = (da * gu_ref[:, cw:] * (sg * (1.0 + g * (1.0 - sg)))).astype(BF16)
        o_ref[:, cw:] = (da * (g * sg)).astype(BF16)

    return pl.pallas_call(
        kern, name="swiglu_bwd", grid=(s // tr, f2 // (2 * cw)),
        in_specs=[pl.BlockSpec((tr, cw), lambda i, j: (i, j)), pl.BlockSpec((tr, 2 * cw), lambda i, j: (i, j))],
        out_specs=pl.BlockSpec((tr, 2 * cw), lambda i, j: (i, j)),
        out_shape=jax.ShapeDtypeStruct((s, f2), BF16),
        compiler_params=_params("parallel", "parallel"),
    )(dact, gu)


def _split3(v):
    hi = v.astype(BF16)
    r = v - hi.astype(F32)
    mid = r.astype(BF16)
    lo = (r - mid.astype(F32)).astype(BF16)
    return hi, mid, lo


def _dot3_right(v, ones):
    hi, mid, lo = _split3(v)
    d = lambda p: jnp.dot(p, ones, preferred_element_type=F32)
    return (d(lo) + d(mid)) + d(hi)


def _dot3_left(ones, v):
    hi, mid, lo = _split3(v)
    d = lambda p: jnp.dot(ones, p, preferred_element_type=F32)
    return (d(lo) + d(mid)) + d(hi)


def _split2(v):
    hi = v.astype(BF16)
    return hi, (v - hi.astype(F32)).astype(BF16)


def _dot2_right(v, ones):
    hi, lo = _split2(v)
    return jnp.dot(lo, ones, preferred_element_type=F32) + jnp.dot(hi, ones, preferred_element_type=F32)


def _log1p_exp_neg_abs(v):
    return jnp.log(1.0 + jnp.exp(-jnp.abs(v)))


def _mask01(cond):
    return jnp.where(cond, 1.0, 0.0).astype(BF16)


def _iota2(t):
    return (lax.broadcasted_iota(jnp.int32, (t, t), 0), lax.broadcasted_iota(jnp.int32, (t, t), 1))


def cum_fwd(gf, b_pad, f_col0):
    s = gf.shape[0]
    t = _tile(s, ATT_TILE, LANES)
    fb = f_col0 // LANES

    def kern(f_ref, b_ref, cum_ref, carry_ref):
        @pl.when(pl.program_id(0) == 0)
        def _():
            carry_ref[...] = jnp.zeros_like(carry_ref)

        v = f_ref[...] + b_ref[...]
        lf = jnp.minimum(v, 0.0) - _log1p_exp_neg_abs(v)
        row, col = _iota2(t)
        cum = _dot3_left(_mask01(col <= row), lf) + carry_ref[...]
        cum_ref[...] = cum
        carry_ref[...] = cum[t - 1:t, :]

    return pl.pallas_call(
        kern, name="cum_fwd", grid=(s // t,),
        in_specs=[pl.BlockSpec((t, LANES), lambda i: (i, fb)), pl.BlockSpec((1, LANES), lambda i: (0, 0))],
        out_specs=pl.BlockSpec((t, LANES), lambda i: (i, 0)),
        out_shape=jax.ShapeDtypeStruct((s, LANES), F32),
        scratch_shapes=[pltpu.VMEM((1, LANES), F32)],
        compiler_params=_params("arbitrary"),
    )(gf, b_pad)


def cum_bwd(dcum, gf, b_pad, f_col0, n_heads):
    s = gf.shape[0]
    t = _tile(s, ATT_TILE, LANES)
    nb = s // t
    fb = f_col0 // LANES

    def kern(dc_ref, f_ref, b_ref, df_ref, db_ref, carry_ref):
        @pl.when(pl.program_id(0) == 0)
        def _():
            carry_ref[...] = jnp.zeros_like(carry_ref)
            db_ref[...] = jnp.zeros_like(db_ref)

        row, col = _iota2(t)
        dlf = _dot3_left(_mask01(col >= row), dc_ref[...]) + carry_ref[...]
        carry_ref[...] = dlf[0:1, :]
        v = f_ref[...] + b_ref[...]
        sig_neg = jnp.exp(-jnp.maximum(v, 0.0) - _log1p_exp_neg_abs(v))
        lane = lax.broadcasted_iota(jnp.int32, (t, LANES), 1)
        df = jnp.where(lane < n_heads, dlf * sig_neg, 0.0)
        df_ref[...] = df.astype(BF16)
        db_ref[...] += jnp.sum(df, axis=0, keepdims=True)

    return pl.pallas_call(
        kern, name="cum_bwd", grid=(nb,),
        in_specs=[pl.BlockSpec((t, LANES), lambda i: (nb - 1 - i, 0)),
                  pl.BlockSpec((t, LANES), lambda i: (nb - 1 - i, fb)),
                  pl.BlockSpec((1, LANES), lambda i: (0, 0))],
        out_specs=[pl.BlockSpec((t, LANES), lambda i: (nb - 1 - i, 0)), pl.BlockSpec((1, LANES), lambda i: (0, 0))],
        out_shape=[jax.ShapeDtypeStruct((s, LANES), BF16), jax.ShapeDtypeStruct((1, LANES), F32)],
        scratch_shapes=[pltpu.VMEM((1, LANES), F32)],
        compiler_params=_params("arbitrary"),
    )(dcum, gf, b_pad)


def _qkv_specs(s, t, n_heads, base):
    return [pl.BlockSpec((t, HEAD_DIM), lambda h, i: (i, base + h)),
            pl.BlockSpec((s, HEAD_DIM), lambda h, i: (0, base + n_heads + h)),
            pl.BlockSpec((s, HEAD_DIM), lambda h, i: (0, base + 2 * n_heads + h))]


def _strips(t):
    sr = _tile(t, ATT_STRIP, 8)
    return sr, t // sr, [slice(si * sr, (si + 1) * sr) for si in range(t // sr)]


def _key_minus_row(sr, t):
    return lax.broadcasted_iota(jnp.int32, (sr, t), 1) - lax.broadcasted_iota(jnp.int32, (sr, t), 0)


def _sb_scores(q, k, diff, lim):
    z = lax.dot_general(q, k, NT, preferred_element_type=F32) * (HEAD_DIM ** -0.5)
    valid = diff < lim
    l1p = _log1p_exp_neg_abs(z)
    log_keep = jnp.where(valid, -jnp.maximum(z, 0.0) - l1p, 0.0)
    return z, valid, l1p, log_keep


def sb_fwd(qkv, n_heads, base):
    s = qkv.shape[0]
    t = _tile(s, ATT_TILE, LANES)
    sr, ns, strips = _strips(t)

    def kern(q_ref, k_ref, v_ref, o_ref):
        i = pl.program_id(1)
        row, col = _iota2(t)
        after = _mask01(row > col)
        diff = _key_minus_row(sr, t)
        qs = [q_ref[sl, :] for sl in strips]

        def body(jj, carry):
            runs, accs = carry
            j = i - jj
            off = pl.multiple_of(j * t, t)
            k = k_ref[pl.ds(off, t), :]
            v = v_ref[pl.ds(off, t), :]
            lim = jnp.where(j < i, t, 0)
            new_runs, new_accs = [], []
            for si in range(ns):
                z, valid, l1p, log_keep = _sb_scores(qs[si], k, diff, lim + si * sr)
                between = _dot2_right(log_keep, after) + runs[si]
                w = jnp.where(valid, jnp.exp(jnp.minimum(z, 0.0) - l1p + between), 0.0)
                new_accs.append(accs[si] + jnp.dot(w.astype(BF16), v, preferred_element_type=F32))
                new_runs.append(runs[si] + jnp.sum(log_keep, axis=1, keepdims=True))
            return tuple(new_runs), tuple(new_accs)

        init = (tuple(jnp.zeros((sr, 1), F32) for _ in strips), tuple(jnp.zeros((sr, HEAD_DIM), F32) for _ in strips))
        _, accs = lax.fori_loop(0, i + 1, body, init)
        for sl, acc in zip(strips, accs):
            o_ref[sl, :] = acc.astype(o_ref.dtype)

    return pl.pallas_call(
        kern, name="sb_fwd", grid=(n_heads, s // t),
        in_specs=_qkv_specs(s, t, n_heads, base),
        out_specs=pl.BlockSpec((t, HEAD_DIM), lambda h, i: (i, h)),
        out_shape=jax.ShapeDtypeStruct((s, n_heads * HEAD_DIM), BF16),
        compiler_params=_params("parallel", "arbitrary"),
    )(qkv, qkv, qkv)


def sb_bwd(qkv, d_o, n_heads, base):
    s = qkv.shape[0]
    t = _tile(s, ATT_TILE, LANES)
    nq = s // t
    sr, ns, strips = _strips(t)
    scale = HEAD_DIM ** -0.5

    def kern(q_ref, k_ref, v_ref, do_ref, dq_ref, dk_ref, dv_ref, dk_acc, dv_acc, run_ref):
        i = pl.program_id(1)

        @pl.when(i == 0)
        def _():
            dk_acc[...] = jnp.zeros_like(dk_acc)
            dv_acc[...] = jnp.zeros_like(dv_acc)

        row, col = _iota2(t)
        after = _mask01(row > col)
        before = _mask01(row < col)
        diff = _key_minus_row(sr, t)
        qs = [q_ref[sl, :] for sl in strips]
        dos = [do_ref[sl, :] for sl in strips]

        def sweep1(jj, runs):
            j = i - jj
            k = k_ref[pl.ds(pl.multiple_of(j * t, t), t), :]
            lim = jnp.where(j < i, t, 0)
            new_runs = []
            for si, sl in enumerate(strips):
                _, _, _, log_keep = _sb_scores(qs[si], k, diff, lim + si * sr)
                run_ref[j, sl, :] = runs[si]
                new_runs.append(runs[si] + jnp.sum(log_keep, axis=1, keepdims=True))
            return tuple(new_runs)

        lax.fori_loop(0, i + 1, sweep1, tuple(jnp.zeros((sr, 1), F32) for _ in strips))

        def sweep2(j, carry):
            run_es, dqs = carry
            off = pl.multiple_of(j * t, t)
            k = k_ref[pl.ds(off, t), :]
            v = v_ref[pl.ds(off, t), :]
            lim = jnp.where(j < i, t, 0)
            new_es, new_dqs = [], []
            dk_t = jnp.zeros((t, HEAD_DIM), F32)
            dv_t = jnp.zeros((t, HEAD_DIM), F32)
            for si, sl in enumerate(strips):
                z, valid, l1p, log_keep = _sb_scores(qs[si], k, diff, lim + si * sr)
                between = _dot2_right(log_keep, after) + run_ref[j, sl, :]
                w = jnp.where(valid, jnp.exp(jnp.minimum(z, 0.0) - l1p + between), 0.0)
                dw = lax.dot_general(dos[si], v, NT, preferred_element_type=F32)
                e = dw * w
                e_before = _dot2_right(e, before) + run_es[si]
                keep = jnp.exp(log_keep)
                dz = jnp.where(valid, e * keep - e_before * (1.0 - keep), 0.0) * scale
                dzb = dz.astype(BF16)
                new_dqs.append(dqs[si] + jnp.dot(dzb, k, preferred_element_type=F32))
                dk_t = dk_t + lax.dot_general(dzb, qs[si], TN, preferred_element_type=F32)
                dv_t = dv_t + lax.dot_general(w.astype(BF16), dos[si], TN, preferred_element_type=F32)
                new_es.append(run_es[si] + jnp.sum(e, axis=1, keepdims=True))
            dk_acc[pl.ds(off, t), :] += dk_t
            dv_acc[pl.ds(off, t), :] += dv_t
            return tuple(new_es), tuple(new_dqs)

        init = (tuple(jnp.zeros((sr, 1), F32) for _ in strips), tuple(jnp.zeros((sr, HEAD_DIM), F32) for _ in strips))
        _, dqs = lax.fori_loop(0, i + 1, sweep2, init)
        for sl, dq in zip(strips, dqs):
            dq_ref[sl, :] = dq.astype(BF16)

        @pl.when(i == nq - 1)
        def _():
            dk_ref[...] = dk_acc[...].astype(BF16)
            dv_ref[...] = dv_acc[...].astype(BF16)

    out = jax.ShapeDtypeStruct((s, n_heads * HEAD_DIM), BF16)
    head_blk = pl.BlockSpec((s, HEAD_DIM), lambda h, i: (0, h))
    tile_blk = pl.BlockSpec((t, HEAD_DIM), lambda h, i: (i, h))
    return pl.pallas_call(
        kern, name="sb_bwd", grid=(n_heads, nq),
        in_specs=_qkv_specs(s, t, n_heads, base) + [tile_blk],
        out_specs=[tile_blk, head_blk, head_blk],
        out_shape=[out, out, out],
        scratch_shapes=[pltpu.VMEM((s, HEAD_DIM), F32), pltpu.VMEM((s, HEAD_DIM), F32), pltpu.VMEM((nq, t, 1), F32)],
        compiler_params=_params("parallel", "arbitrary"),
    )(qkv, qkv, qkv, d_o)


def _fox_scores(q, k, cq, ck, diff, lim):
    sc = lax.dot_general(q, k, NT, preferred_element_type=F32) * (HEAD_DIM ** -0.5)
    sc = sc + cq - ck
    valid = diff < lim
    return jnp.where(valid, sc, NEG_BIG), valid


def fox_fwd(qkv, cum_col, cum_row, n_heads, base):
    s = qkv.shape[0]
    t = _tile(s, ATT_TILE, LANES)
    sr, ns, strips = _strips(t)

    def kern(q_ref, k_ref, v_ref, cq_ref, ck_ref, o_ref, lse_ref):
        i = pl.program_id(1)
        diff = _key_minus_row(sr, t)
        qs = [q_ref[sl, :] for sl in strips]
        cqs = [cq_ref[0, sl, :] for sl in strips]

        def body(j, carry):
            off = pl.multiple_of(j * t, t)
            k = k_ref[pl.ds(off, t), :]
            v = v_ref[pl.ds(off, t), :]
            ck = ck_ref[0, :, pl.ds(off, t)]
            lim = jnp.where(j < i, t, 1)
            out = []
            for si in range(ns):
                m, l, acc = carry[si]
                sc, _ = _fox_scores(qs[si], k, cqs[si], ck, diff, lim + si * sr)
                m_new = jnp.maximum(m, jnp.max(sc, axis=1, keepdims=True))
                p = jnp.exp(sc - m_new)
                alpha = jnp.exp(m - m_new)
                l = alpha * l + jnp.sum(p, axis=1, keepdims=True)
                acc = alpha * acc + jnp.dot(p.astype(BF16), v, preferred_element_type=F32)
                out.append((m_new, l, acc))
            return tuple(out)

        init = tuple((jnp.full((sr, 1), NEG_BIG, F32), jnp.zeros((sr, 1), F32), jnp.zeros((sr, HEAD_DIM), F32))
                     for _ in strips)
        res = lax.fori_loop(0, i + 1, body, init)
        for sl, (m, l, acc) in zip(strips, res):
            o_ref[sl, :] = acc / l
            lse_ref[0, sl, :] = m + jnp.log(l)

    col_blk = pl.BlockSpec((1, t, 1), lambda h, i: (h, i, 0))
    return pl.pallas_call(
        kern, name="fox_fwd", grid=(n_heads, s // t),
        in_specs=_qkv_specs(s, t, n_heads, base) + [col_blk, pl.BlockSpec((1, 1, s), lambda h, i: (h, 0, 0))],
        out_specs=[pl.BlockSpec((t, HEAD_DIM), lambda h, i: (i, h)), col_blk],
        out_shape=[jax.ShapeDtypeStruct((s, n_heads * HEAD_DIM), F32), jax.ShapeDtypeStruct((n_heads, s, 1), F32)],
        compiler_params=_params("parallel", "arbitrary"),
    )(qkv, qkv, qkv, cum_col, cum_row)


def fox_bwd(qkv, cum_col, cum_row, o, d_o, lse, n_heads, base):
    s = qkv.shape[0]
    t = _tile(s, ATT_TILE, LANES)
    nq = s // t
    sr, ns, strips = _strips(t)
    scale = HEAD_DIM ** -0.5

    def kern(q_ref, k_ref, v_ref, cq_ref, ck_ref, o_ref, do_ref, lse_ref,
             dq_ref, dk_ref, dv_ref, dcq_ref, dck_ref, dk_acc, dv_acc, dck_acc):
        i = pl.program_id(1)

        @pl.when(i == 0)
        def _():
            dk_acc[...] = jnp.zeros_like(dk_acc)
            dv_acc[...] = jnp.zeros_like(dv_acc)
            dck_acc[...] = jnp.zeros_like(dck_acc)

        diff = _key_minus_row(sr, t)
        qs = [q_ref[sl, :] for sl in strips]
        dos = [do_ref[sl, :] for sl in strips]
        cqs = [cq_ref[0, sl, :] for sl in strips]
        lses = [lse_ref[0, sl, :] for sl in strips]
        deltas = [jnp.sum(dos[si].astype(F32) * o_ref[sl, :], axis=1, keepdims=True) for si, sl in enumerate(strips)]

        def body(j, carry):
            off = pl.multiple_of(j * t, t)
            k = k_ref[pl.ds(off, t), :]
            v = v_ref[pl.ds(off, t), :]
            ck = ck_ref[0, :, pl.ds(off, t)]
            lim = jnp.where(j < i, t, 1)
            out = []
            dk_t = jnp.zeros((t, HEAD_DIM), F32)
            dv_t = jnp.zeros((t, HEAD_DIM), F32)
            dck_t = jnp.zeros((1, t), F32)
            for si in range(ns):
                dq, dcq = carry[si]
                sc, valid = _fox_scores(qs[si], k, cqs[si], ck, diff, lim + si * sr)
                p = jnp.where(valid, jnp.exp(sc - lses[si]), 0.0)
                dp = lax.dot_general(dos[si], v, NT, preferred_element_type=F32)
                ds = p * (dp - deltas[si])
                dsb = (ds * scale).astype(BF16)
                dq = dq + jnp.dot(dsb, k, preferred_element_type=F32)
                dk_t = dk_t + lax.dot_general(dsb, qs[si], TN, preferred_element_type=F32)
                dv_t = dv_t + lax.dot_general(p.astype(BF16), dos[si], TN, preferred_element_type=F32)
                dck_t = dck_t + jnp.sum(ds, axis=0, keepdims=True)
                out.append((dq, dcq + jnp.sum(ds, axis=1, keepdims=True)))
            dk_acc[pl.ds(off, t), :] += dk_t
            dv_acc[pl.ds(off, t), :] += dv_t
            dck_acc[:, pl.ds(off, t)] -= dck_t
            return tuple(out)

        init = tuple((jnp.zeros((sr, HEAD_DIM), F32), jnp.zeros((sr, 1), F32)) for _ in strips)
        res = lax.fori_loop(0, i + 1, body, init)
        for sl, (dq, dcq) in zip(strips, res):
            dq_ref[sl, :] = dq.astype(BF16)
            dcq_ref[0, sl, :] = dcq

        @pl.when(i == nq - 1)
        def _():
            dk_ref[...] = dk_acc[...].astype(BF16)
            dv_ref[...] = dv_acc[...].astype(BF16)
            dck_ref[0] = dck_acc[...]

    out = jax.ShapeDtypeStruct((s, n_heads * HEAD_DIM), BF16)
    head_blk = pl.BlockSpec((s, HEAD_DIM), lambda h, i: (0, h))
    tile_blk = pl.BlockSpec((t, HEAD_DIM), lambda h, i: (i, h))
    col_blk = pl.BlockSpec((1, t, 1), lambda h, i: (h, i, 0))
    row_blk = pl.BlockSpec((1, 1, s), lambda h, i: (h, 0, 0))
    return pl.pallas_call(
        kern, name="fox_bwd", grid=(n_heads, nq),
        in_specs=_qkv_specs(s, t, n_heads, base) + [col_blk, row_blk, tile_blk, tile_blk, col_blk],
        out_specs=[tile_blk, head_blk, head_blk, col_blk, row_blk],
        out_shape=[out, out, out, jax.ShapeDtypeStruct((n_heads, s, 1), F32),
                   jax.ShapeDtypeStruct((n_heads, 1, s), F32)],
        scratch_shapes=[pltpu.VMEM((s, HEAD_DIM), F32), pltpu.VMEM((s, HEAD_DIM), F32), pltpu.VMEM((1, s), F32)],
        compiler_params=_params("parallel", "arbitrary"),
    )(qkv, qkv, qkv, cum_col, cum_row, o, d_o, lse)


def _place():
    x, y, c = lax.axis_index("x"), lax.axis_index("y"), lax.axis_index("c")
    other_chips = [(1 - x, y), (x, 1 - y), (1 - x, 1 - y)]
    return x, y, c, other_chips


ANY = pl.BlockSpec(memory_space=pl.ANY)


def _remote(src, dst, send_sem, recv_sem, dev):
    return pltpu.make_async_remote_copy(src_ref=src, dst_ref=dst, send_sem=send_sem, recv_sem=recv_sem,
                                        device_id=dev, device_id_type=MESH)


def cast_place(name, ws, chip):
    r = ws[0].shape[0]
    cs = [w.shape[1] for w in ws]
    tr = _tile(r, 256, 16)

    def kern(chip_ref, *refs):
        o_ref = refs[-1]
        off = 0
        for w_ref, c in zip(refs[:-1], cs):
            o_ref[:, off:off + c] = w_ref[...].astype(BF16)
            off += c

    return pl.pallas_call(
        kern, name=name,
        grid_spec=pltpu.PrefetchScalarGridSpec(
            num_scalar_prefetch=1, grid=(r // tr,),
            in_specs=[pl.BlockSpec((tr, c), lambda i, chip_ref: (i, 0)) for c in cs],
            out_specs=pl.BlockSpec((None, tr, sum(cs)), lambda i, chip_ref: (chip_ref[0], i, 0))),
        out_shape=jax.ShapeDtypeStruct((N_CHIPS, r, sum(cs)), BF16),
        compiler_params=_params("parallel"),
    )(chip, *ws)


def all_gather(bufs):
    n = len(bufs)
    halves = [b.shape[1] // 2 for b in bufs]

    def body(*refs):
        outs = refs[n:2 * n]
        ici_send, ici_recv, d2d_send, d2d_recv = refs[2 * n:]
        x, y, c, chips = _place()
        me = 2 * x + y
        sibling = (x, y, 1 - c)

        def rows(a, k, core):
            return outs[a].at[k, pl.ds(pl.multiple_of(core * halves[a], 16), halves[a])]

        def chip_of(j):
            return 2 * chips[j][0] + chips[j][1]

        def ici(a, j, k):
            return _remote(rows(a, k, c), rows(a, k, c), ici_send.at[3 * a + j], ici_recv.at[3 * a + j],
                           (chips[j][0], chips[j][1], c))

        def d2d(a, j, core):
            return _remote(rows(a, chip_of(j), core), rows(a, chip_of(j), core), d2d_send.at[3 * a + j],
                           d2d_recv.at[3 * a + j], sibling)

        pairs = [(a, j) for a in range(n) for j in range(3)]
        for a, j in pairs:
            ici(a, j, me).start()
        for a, j in pairs:
            ici(a, j, chip_of(j)).wait_recv()
            d2d(a, j, c).start()
        for a, j in pairs:
            d2d(a, j, 1 - c).wait_recv()
        for a, j in pairs:
            ici(a, j, me).wait_send()
            d2d(a, j, c).wait_send()

    return pl.pallas_call(
        body, name="all_gather", in_specs=[ANY] * n, out_specs=[ANY] * n,
        out_shape=[jax.ShapeDtypeStruct(b.shape, b.dtype) for b in bufs],
        input_output_aliases={a: a for a in range(n)},
        scratch_shapes=[pltpu.SemaphoreType.DMA((3 * n,)) for _ in range(4)],
    )(*bufs)


def swap_halves(pieces):
    n = len(pieces)
    halves = [p.shape[1] // 2 for p in pieces]

    def body(*refs):
        ins, outs = refs[:n], refs[n:2 * n]
        send_sems, recv_sems = refs[2 * n:]
        x, y, c, _ = _place()
        cps = [_remote(ins[a].at[:, pl.ds(pl.multiple_of((1 - c) * halves[a], 16), halves[a]), :], outs[a],
                       send_sems.at[a], recv_sems.at[a], (x, y, 1 - c)) for a in range(n)]
        for cp in cps:
            cp.start()
        for cp in cps:
            cp.wait()

    return pl.pallas_call(
        body, name="swap_halves", in_specs=[ANY] * n, out_specs=[ANY] * n,
        out_shape=[jax.ShapeDtypeStruct((N_CHIPS, h, p.shape[2]), p.dtype) for p, h in zip(pieces, halves)],
        scratch_shapes=[pltpu.SemaphoreType.DMA((n,)), pltpu.SemaphoreType.DMA((n,))],
    )(*pieces)


def pair_sum(name, pieces, got, core):
    _, r, w = pieces.shape
    half = r // 2
    tr = _tile(half, 256, 16)

    def kern(core_ref, p_ref, g_ref, o_ref):
        o_ref[...] = (p_ref[...].astype(F32) + g_ref[...].astype(F32)).astype(o_ref.dtype)

    return pl.pallas_call(
        kern, name=name,
        grid_spec=pltpu.PrefetchScalarGridSpec(
            num_scalar_prefetch=1, grid=(N_CHIPS, half // tr),
            in_specs=[pl.BlockSpec((None, None, tr, w), lambda k, i, core_ref: (k, core_ref[0], i, 0)),
                      pl.BlockSpec((None, tr, w), lambda k, i, core_ref: (k, i, 0))],
            out_specs=pl.BlockSpec((None, tr, w), lambda k, i, core_ref: (k, i, 0))),
        out_shape=jax.ShapeDtypeStruct((N_CHIPS, half, w), pieces.dtype),
        compiler_params=_params("parallel", "parallel"),
    )(core, pieces.reshape(N_CHIPS, 2, half, w), got)


def scatter_chips(sums):
    n = len(sums)

    def body(*refs):
        ins, outs = refs[:n], refs[n:2 * n]
        send_sems, recv_sems = refs[2 * n:]
        x, y, c, chips = _place()
        cps = [_remote(ins[a].at[2 * chips[j][0] + chips[j][1]], outs[a].at[j], send_sems.at[3 * a + j],
                       recv_sems.at[3 * a + j], (chips[j][0], chips[j][1], c))
               for a in range(n) for j in range(3)]
        for cp in cps:
            cp.start()
        for cp in cps:
            cp.wait()

    return pl.pallas_call(
        body, name="scatter_chips", in_specs=[ANY] * n, out_specs=[ANY] * n,
        out_shape=[jax.ShapeDtypeStruct((3,) + t.shape[1:], t.dtype) for t in sums],
        scratch_shapes=[pltpu.SemaphoreType.DMA((3 * n,)), pltpu.SemaphoreType.DMA((3 * n,))],
    )(*sums)


def chip_sum(name, sums, got, chip, core):
    _, half, w = sums.shape
    tr = _tile(half, 256, 16)
    nb = half // tr

    def kern(ids_ref, s_ref, g0_ref, g1_ref, g2_ref, o_ref):
        o_ref[...] = ((s_ref[...].astype(F32) + g0_ref[...].astype(F32)) + g1_ref[...].astype(F32)) \
            + g2_ref[...].astype(F32)

    def got_spec(j):
        return pl.BlockSpec((None, tr, w), lambda i, ids_ref: (j, i, 0))

    return pl.pallas_call(
        kern, name=name,
        grid_spec=pltpu.PrefetchScalarGridSpec(
            num_scalar_prefetch=1, grid=(nb,),
            in_specs=[pl.BlockSpec((None, tr, w), lambda i, ids_ref: (ids_ref[0], i, 0)),
                      got_spec(0), got_spec(1), got_spec(2)],
            out_specs=pl.BlockSpec((tr, w), lambda i, ids_ref: (ids_ref[1] * nb + i, 0))),
        out_shape=jax.ShapeDtypeStruct((2 * half, w), F32),
        compiler_params=_params("parallel"),
    )(jnp.concatenate([chip, core]), sums, got, got, got)


def join_halves(shards):
    n = len(shards)
    halves = [g.shape[0] // 2 for g in shards]

    def body(*refs):
        outs = refs[n:2 * n]
        send_sems, recv_sems = refs[2 * n:]
        x, y, c, _ = _place()
        cps = []
        for a in range(n):
            rows = outs[a].at[pl.ds(pl.multiple_of(c * halves[a], 8), halves[a])]
            cps.append(_remote(rows, rows, send_sems.at[a], recv_sems.at[a], (x, y, 1 - c)))
        for cp in cps:
            cp.start()
        for cp in cps:
            cp.wait()

    return pl.pallas_call(
        body, name="join_halves", in_specs=[ANY] * n, out_specs=[ANY] * n,
        out_shape=[jax.ShapeDtypeStruct(g.shape, g.dtype) for g in shards],
        input_output_aliases={a: a for a in range(n)},
        scratch_shapes=[pltpu.SemaphoreType.DMA((n,)), pltpu.SemaphoreType.DMA((n,))],
    )(*shards)


def _adam(w, g, m, v):
    m = ADAM_B1 * m + (1.0 - ADAM_B1) * g
    v = ADAM_B2 * v + (1.0 - ADAM_B2) * (g * g)
    m_hat = m / (1.0 - ADAM_B1 ** ADAM_STEP)
    v_hat = v / (1.0 - ADAM_B2 ** ADAM_STEP)
    delta = -ADAM_LR * (m_hat / (jnp.sqrt(v_hat) + ADAM_EPS) + ADAM_WD * w)
    return delta, m, v


def small_allreduce_adam(g_part, w, m, v):
    n_dev = 8
    r, d = g_part.shape

    def body(g_ref, w_ref, m_ref, v_ref, gs_ref, dl_ref, nm_ref, nv_ref, all_ref, send_sems, recv_sems):
        x, y, c, _ = _place()
        me = 4 * x + 2 * y + c
        all_ref[me] = g_ref[...]
        cps = []
        for rel in range(1, n_dev):
            px = 1 - x if rel & 4 else x
            py = 1 - y if rel & 2 else y
            pc = 1 - c if rel & 1 else c
            cps.append(_remote(g_ref, all_ref.at[me], send_sems.at[rel - 1], recv_sems.at[rel - 1], (px, py, pc)))
        for cp in cps:
            cp.start()
        for cp in cps:
            cp.wait()
        total = all_ref[0]
        for dev in range(1, n_dev):
            total = total + all_ref[dev]
        gs_ref[...] = total
        delta, nm, nv = _adam(w_ref[...], total, m_ref[...], v_ref[...])
        dl_ref[...] = delta
        nm_ref[...] = nm
        nv_ref[...] = nv

    vm = pl.BlockSpec(memory_space=pltpu.VMEM)
    out = jax.ShapeDtypeStruct((r, d), F32)
    return pl.pallas_call(
        body, name="small_allreduce_adam", in_specs=[vm, vm, vm, vm], out_specs=[vm, vm, vm, vm],
        out_shape=[out, out, out, out],
        scratch_shapes=[pltpu.VMEM((n_dev, r, d), F32), pltpu.SemaphoreType.DMA((n_dev - 1,)),
                        pltpu.SemaphoreType.DMA((n_dev - 1,))],
    )(g_part, w, m, v)


def adam_update(name, w, m, v, g_buf, col_blk, copy_g):
    r, c = w.shape
    tr = _tile(r, 128, 8)

    def kern(w_ref, m_ref, v_ref, g_ref, *outs):
        g = g_ref[...]
        delta, nm, nv = _adam(w_ref[...], g, m_ref[...], v_ref[...])
        if copy_g:
            outs[0][...] = g
        outs[-3][...] = delta
        outs[-2][...] = nm
        outs[-1][...] = nv

    blk = pl.BlockSpec((tr, c), lambda i: (i, 0))
    out = jax.ShapeDtypeStruct((r, c), F32)
    n_out = 4 if copy_g else 3
    return pl.pallas_call(
        kern, name=name, grid=(r // tr,),
        in_specs=[blk, blk, blk, pl.BlockSpec((tr, c), lambda i: (i, col_blk))],
        out_specs=[blk] * n_out, out_shape=[out] * n_out, compiler_params=_params("parallel"),
    )(w, m, v, g_buf)


def _col_pieces(slab_of, lo, hi, cw):
    out = []
    while lo < hi:
        k, a = divmod(lo, cw)
        b = min(cw, a + hi - lo)
        out.append(slab_of(k)[:, a:b])
        lo += b - a
    return out


def kernel(x, norm_mix_pre, norm_mix_post, w_in, b_forget, w_branch_sb, w_branch_fox, w_out, norm_ffn_pre, norm_ffn_post, w_ffn_gate, w_ffn_up, w_ffn_down, loss_target, m_norm_mix_pre, m_norm_mix_post, m_w_in, m_b_forget, m_w_branch_sb, m_w_branch_fox, m_w_out, m_norm_ffn_pre, m_norm_ffn_post, m_w_ffn_gate, m_w_ffn_up, m_w_ffn_down, v_norm_mix_pre, v_norm_mix_post, v_w_in, v_b_forget, v_w_branch_sb, v_w_branch_fox, v_w_out, v_norm_ffn_pre, v_norm_ffn_post, v_w_ffn_gate, v_w_ffn_up, v_w_ffn_down):
    s, d = x.shape[1], x.shape[2]
    n_heads = b_forget.shape[1]
    d_att = n_heads * HEAD_DIM
    c_in = w_in.shape[2]
    c_br = w_branch_sb.shape[2]
    c_gu = w_ffn_gate.shape[2]
    d_ff = c_gu * N_CHIPS
    d_in = c_in * N_CHIPS
    f_pad = 512
    n_qkv = 6 * d_att
    n_gf = 2 * d + f_pad
    core = lax.axis_index("c").astype(jnp.int32).reshape(1)
    chip = (2 * lax.axis_index("x") + lax.axis_index("y")).astype(jnp.int32).reshape(1)

    g_in, g_br, g_out, g_gu, g_dn = all_gather([
        cast_place("place_w_in", [w_in[0]], chip),
        cast_place("place_branch", [w_branch_sb[0], w_branch_fox[0]], chip),
        cast_place("place_out", [w_out[0]], chip),
        cast_place("place_gate_up", [w_ffn_gate[0], w_ffn_up[0]], chip),
        cast_place("place_down", [w_ffn_down[0]], chip)])
    slab = lambda k: g_in[k]
    w_main = jnp.concatenate(
        _col_pieces(slab, 0, n_qkv, c_in) + _col_pieces(slab, n_qkv + n_heads, d_in, c_in)
        + _col_pieces(slab, n_qkv, n_qkv + n_heads, c_in) + [jnp.zeros((d, f_pad - n_heads), BF16)], axis=1)
    w_o = g_out.reshape(d, d)
    w_dn = g_dn.reshape(d_ff, d)

    x2 = x[0]
    tgt = loss_target[0]
    b_pad = jnp.pad(b_forget, ((0, 0), (0, LANES - n_heads)))

    u = norm_in(x2, norm_mix_pre)
    qkv = mm(u, w_main, "nn", BF16, "proj_qkv", b_win=(0, n_qkv))
    gf = mm(u, w_main, "nn", F32, "proj_gates", b_win=(n_qkv, n_gf))
    cum = cum_fwd(gf, b_pad, 2 * d)
    cum_heads = cum[:, :n_heads].T
    cum_col, cum_row = cum_heads[:, :, None], cum_heads[:, None, :]
    o_sb = sb_fwd(qkv, n_heads, 0)
    o_fx, lse = fox_fwd(qkv, cum_col, cum_row, n_heads, 3 * n_heads)
    bsb = mm(o_sb, g_br, "nn", F32, "branch_sb", tn=c_br, chunks=(1, 0))
    bfx = mm(o_fx, g_br, "nn", F32, "branch_fox", tn=c_br, chunks=(1, 1))
    merged = gate_fwd(bsb, bfx, gf)
    mix = mm(merged, w_o, "nn", F32, "out_proj")
    h1, u2 = mid_fwd(x2, mix, norm_mix_post, norm_ffn_pre)
    gu = mm(u2, g_gu, "nn", F32, "ffn_gate_up", tn=c_gu, chunks=(2, 0))
    act = swiglu_fwd(gu, c_gu)
    ff = mm(act, w_dn, "nn", F32, "ffn_down")
    dy, d_ff_out, dg_fpost, loss_part = loss_head(h1, ff, norm_ffn_post, tgt)

    p_dn = mm(act, d_ff_out, "tn", BF16, "dw_ffn_down").reshape(N_CHIPS, d_ff // N_CHIPS, d)
    d_act = mm(d_ff_out, w_dn, "nt", F32, "d_act")
    d_gu = swiglu_bwd(d_act, gu, c_gu)
    p_gu = mm(u2, d_gu, "tn", BF16, "dw_ffn_gate_up", tn=c_gu, chunks=(2, 0),
              out_into=lax.empty((N_CHIPS, d, 2 * c_gu), BF16))
    du2 = mm(d_gu, g_gu, "nt", F32, "d_u2", tk=c_gu, chunks=(2, 0))
    dh1, d_mix, dg_fpre, dg_post = mid_bwd(dy, du2, h1, mix, norm_ffn_pre, norm_mix_post)
    p_out = mm(merged, d_mix, "tn", BF16, "dw_out").reshape(N_CHIPS, d // N_CHIPS, d)
    d_merged = mm(d_mix, w_o, "nt", F32, "d_merged")
    d_bsb, d_bfx, d_gs, d_gx = gate_bwd(d_merged, bsb, bfx, gf)
    p_br = mm(o_sb, d_bsb, "tn", BF16, "dw_branch_sb", tn=c_br, chunks=(1, 0),
              out_into=lax.empty((N_CHIPS, d_att, 2 * c_br), BF16))
    p_br = mm(o_fx, d_bfx, "tn", BF16, "dw_branch_fox", tn=c_br, chunks=(1, 1), out_into=p_br)
    d_osb = mm(d_bsb, g_br, "nt", BF16, "d_o_sb", tk=c_br, chunks=(1, 0))
    d_ofx = mm(d_bfx, g_br, "nt", BF16, "d_o_fox", tk=c_br, chunks=(1, 1))
    dq_s, dk_s, dv_s = sb_bwd(qkv, d_osb, n_heads, 0)
    dq_f, dk_f, dv_f, dcq, dck = fox_bwd(qkv, cum_col, cum_row, o_fx, d_ofx, lse, n_heads, 3 * n_heads)
    d_cum = jnp.pad((dcq[:, :, 0] + dck[:, 0, :]).T, ((0, 0), (0, LANES - n_heads)))
    d_f, db_pad = cum_bwd(d_cum, gf, b_pad, 2 * d, n_heads)
    d_main = jnp.concatenate(
        [dq_s, dk_s, dv_s, dq_f, dk_f, dv_f, d_gs, d_gx, d_f, jnp.zeros((s, f_pad - LANES), BF16)], axis=1)
    dw_main = mm(u, d_main, "tn", BF16, "dw_in")
    du = mm(d_main, w_main, "nt", F32, "d_u")
    dx, dg_pre = in_bwd(dh1, du, x2, norm_mix_pre)

    def main_cols(lo, hi):
        out = []
        for s0, s1, m0 in [(0, n_qkv, 0), (n_qkv, n_qkv + n_heads, n_qkv + 2 * d), (n_qkv + n_heads, d_in, n_qkv)]:
            a, b = max(lo, s0), min(hi, s1)
            if a < b:
                out.append(dw_main[:, m0 + a - s0:m0 + b - s0])
        return out

    p_in = jnp.stack([jnp.concatenate(main_cols(k * c_in, (k + 1) * c_in), axis=1) for k in range(N_CHIPS)])

    pieces = [p_in, p_br, p_out, p_gu, p_dn]
    tags = ["in", "branch", "out", "gate_up", "down"]
    from_sibling = swap_halves(pieces)
    sums = [pair_sum("pair_sum_" + t, p, q, core) for t, p, q in zip(tags, pieces, from_sibling)]
    from_chips = scatter_chips(sums)
    gr_in, gr_br, gr_out, gr_gu, gr_dn = join_halves(
        [chip_sum("chip_sum_" + t, sm, got, chip, core) for t, sm, got in zip(tags, sums, from_chips)])

    g_in_, d_in_, m_in_, v_in_ = adam_update("adam_w_in", w_in[0], m_w_in[0], v_w_in[0], gr_in, 0, True)
    g_bs, d_bs, m_bs, v_bs = adam_update("adam_branch_sb", w_branch_sb[0], m_w_branch_sb[0], v_w_branch_sb[0], gr_br, 0, True)
    g_bf, d_bf, m_bf, v_bf = adam_update("adam_branch_fox", w_branch_fox[0], m_w_branch_fox[0], v_w_branch_fox[0], gr_br, 1, True)
    g_o_, d_o_, m_o_, v_o_ = adam_update("adam_out", w_out[0], m_w_out[0], v_w_out[0], gr_out, 0, True)
    g_ga, d_ga, m_ga, v_ga = adam_update("adam_gate", w_ffn_gate[0], m_w_ffn_gate[0], v_w_ffn_gate[0], gr_gu, 0, True)
    g_up, d_up, m_up, v_up = adam_update("adam_up", w_ffn_up[0], m_w_ffn_up[0], v_w_ffn_up[0], gr_gu, 1, True)
    g_dn_, d_dn, m_dn, v_dn = adam_update("adam_down", w_ffn_down[0], m_w_ffn_down[0], v_w_ffn_down[0], gr_dn, 0, True)
    lead = lambda arrs: [a[None] for a in arrs]
    grads = lead([g_in_, g_bs, g_bf, g_o_, g_ga, g_up, g_dn_])
    deltas = lead([d_in_, d_bs, d_bf, d_o_, d_ga, d_up, d_dn])
    new_ms = lead([m_in_, m_bs, m_bf, m_o_, m_ga, m_up, m_dn])
    new_vs = lead([v_in_, v_bs, v_bf, v_o_, v_ga, v_up, v_dn])

    def pack(rows):
        rows = [jnp.pad(r_, ((0, 0), (0, d - r_.shape[1]))) for r_ in rows]
        return jnp.concatenate(rows + [jnp.zeros((8 - len(rows), d), F32)], axis=0)

    sm_g, sm_d, sm_m, sm_v = small_allreduce_adam(
        pack([dg_pre, dg_post, dg_fpre, dg_fpost, db_pad]),
        pack([norm_mix_pre, norm_mix_post, norm_ffn_pre, norm_ffn_post, b_forget]),
        pack([m_norm_mix_pre, m_norm_mix_post, m_norm_ffn_pre, m_norm_ffn_post, m_b_forget]),
        pack([v_norm_mix_pre, v_norm_mix_post, v_norm_ffn_pre, v_norm_ffn_post, v_b_forget]))

    def small(a):
        return [a[0:1], a[1:2], a[2:3], a[3:4], a[4:5, :n_heads]]

    def ordered(sm, bg):
        return [sm[0], sm[1], bg[0], sm[4], bg[1], bg[2], bg[3], sm[2], sm[3], bg[4], bg[5], bg[6]]

    loss = lax.psum(loss_part[0, 0], ("x", "y", "c"))
    return (loss, dx[None], *ordered(small(sm_g), grads), *ordered(small(sm_d), deltas),
            *ordered(small(sm_m), new_ms), *ordered(small(sm_v), new_vs))
```

```python
import functools

import jax
import jax.numpy as jnp
from jax import lax
from jax.experimental import pallas as pl
from jax.experimental.pallas import tpu as pltpu

F32 = jnp.float32
BF16 = jnp.bfloat16
MESH = pl.DeviceIdType.MESH

HEAD_DIM = 128
LANES = 128
ATT_TILE = 512
ROW_TILE = 256
N_CHIPS = 4
RMS_EPS = 1e-6
ADAM_LR = 0.001
ADAM_B1 = 0.9
ADAM_B2 = 0.999
ADAM_EPS = 1e-08
ADAM_WD = 0.01
ADAM_STEP = 10
NEG_BIG = -1e30
VMEM_LIMIT = 56 * 1024 * 1024
MM_VMEM_BUDGET = 40 * 1024 * 1024
ATT_STRIP = 512

NN = (((1,), (0,)), ((), ()))
NT = (((1,), (1,)), ((), ()))
TN = (((0,), (0,)), ((), ()))


def _tile(n, pref, align):
    best = None
    t = align
    while t <= min(n, pref):
        if n % t == 0:
            best = t
        t += align
    return n if best is None else best


def _params(*sem):
    return pltpu.CompilerParams(dimension_semantics=sem, vmem_limit_bytes=VMEM_LIMIT)


def _mm_tiles(m, n, k, a_bytes, b_bytes, out_bytes, tn, tk):
    tm = _tile(m, 2048, LANES)
    tk = tk or _tile(k, 512, LANES)

    def vmem(t):
        acc = 0 if out_bytes == 4 else tm * t * 4
        return acc + 2 * tm * t * out_bytes + 2 * (tm * tk * a_bytes + tk * t * b_bytes)

    if tn is None:
        fits = [t for t in range(LANES, min(n, 2048) + 1, LANES) if n % t == 0 and vmem(t) <= MM_VMEM_BUDGET]
        tn = max(fits) if fits else _tile(n, LANES, LANES)
    return tm, tn, tk


def mm(a, b, mode, out_dtype, name, *, tn=None, tk=None, b_win=None, chunks=None, out_into=None):
    n_per, blk0 = chunks if chunks else (1, 0)
    if mode == "nn":
        m, k = a.shape
        n = b.shape[0] * n_per * tn if chunks else (b_win[1] if b_win else b.shape[1])
    elif mode == "nt":
        m = a.shape[0]
        k = b.shape[0] * n_per * tk if chunks else a.shape[1]
        n = b.shape[-2]
    else:
        k, m = a.shape
        n = b.shape[1]
    in_place = jnp.dtype(out_dtype) == jnp.dtype(F32)
    tm, tn, tk = _mm_tiles(m, n, k, a.dtype.itemsize, b.dtype.itemsize, jnp.dtype(out_dtype).itemsize, tn, tk)
    assert m % tm == 0 and n % tn == 0 and k % tk == 0, (name, m, n, k, tm, tn, tk)
    j0 = 0
    if b_win:
        assert b_win[0] % tn == 0
        j0 = b_win[0] // tn
    nk = k // tk
    dims = {"nn": NN, "nt": NT, "tn": TN}[mode]

    def kern(a_ref, b_ref, *rest):
        o_ref, acc_ref = (rest[-1], rest[-1]) if in_place else (rest[-2], rest[-1])
        kk = pl.program_id(2)

        @pl.when(kk == 0)
        def _():
            acc_ref[...] = jnp.zeros_like(acc_ref)

        acc_ref[...] += lax.dot_general(a_ref[...].astype(BF16), b_ref[...].astype(BF16), dims,
                                        preferred_element_type=F32)

        if not in_place:
            @pl.when(kk == nk - 1)
            def _():
                o_ref[...] = acc_ref[...].astype(o_ref.dtype)

    out_spec = pl.BlockSpec((tm, tn), lambda i, j, kk: (i, j))
    out_shape = jax.ShapeDtypeStruct((m, n), out_dtype)
    if mode == "nn":
        a_spec = pl.BlockSpec((tm, tk), lambda i, j, kk: (i, kk))
        if chunks:
            b_spec = pl.BlockSpec((None, tk, tn), lambda i, j, kk: (j // n_per, kk, blk0 + j % n_per))
        else:
            b_spec = pl.BlockSpec((tk, tn), lambda i, j, kk: (kk, j + j0))
    elif mode == "nt":
        a_spec = pl.BlockSpec((tm, tk), lambda i, j, kk: (i, kk))
        if chunks:
            b_spec = pl.BlockSpec((None, tn, tk), lambda i, j, kk: (kk // n_per, j, blk0 + kk % n_per))
        else:
            b_spec = pl.BlockSpec((tn, tk), lambda i, j, kk: (j, kk))
    else:
        a_spec = pl.BlockSpec((tk, tm), lambda i, j, kk: (kk, i))
        b_spec = pl.BlockSpec((tk, tn), lambda i, j, kk: (kk, j))
        if chunks:
            out_spec = pl.BlockSpec((None, tm, tn), lambda i, j, kk: (j // n_per, i, blk0 + j % n_per))
    in_specs, operands, aliases = [a_spec, b_spec], [a, b], {}
    if chunks and mode == "tn":
        assert out_into is not None
        out_shape = jax.ShapeDtypeStruct(out_into.shape, out_dtype)
        in_specs.append(pl.BlockSpec(memory_space=pl.ANY))
        operands.append(out_into)
        aliases = {2: 0}
    return pl.pallas_call(
        kern, name=name, grid=(m // tm, n // tn, nk),
        in_specs=in_specs, out_specs=out_spec, out_shape=out_shape,
        scratch_shapes=[] if in_place else [pltpu.VMEM((tm, tn), F32)], input_output_aliases=aliases,
        compiler_params=_params("parallel", "parallel", "arbitrary"),
    )(*operands)


def _rstd(v):
    return lax.rsqrt(jnp.mean(v * v, axis=-1, keepdims=True) + RMS_EPS)


def _norm_bwd(v, g, dy):
    r = _rstd(v)
    vh = v * r
    dyg = dy * g
    dv = r * (dyg - vh * jnp.mean(dyg * vh, axis=-1, keepdims=True))
    return dv, jnp.sum(dy * vh, axis=0, keepdims=True)


def _row_call(kern, name, ins, outs, s, d):
    tr = _tile(s, ROW_TILE, 16)

    def spec(shape, is_row):
        if is_row:
            return pl.BlockSpec((tr, shape[1]), lambda i: (i, 0))
        return pl.BlockSpec(shape, lambda i: (0, 0))

    return pl.pallas_call(
        kern, name=name, grid=(s // tr,),
        in_specs=[spec(a.shape, r) for a, r in ins],
        out_specs=[spec(sh, r) for sh, _, r in outs],
        out_shape=[jax.ShapeDtypeStruct(sh, dt) for sh, dt, _ in outs],
        compiler_params=_params("arbitrary"),
    )(*[a for a, _ in ins])


def norm_in(x, g):
    s, d = x.shape

    def kern(x_ref, g_ref, u_ref):
        v = x_ref[...]
        u_ref[...] = (v * _rstd(v) * g_ref[...]).astype(BF16)

    return _row_call(kern, "norm_in", [(x, True), (g, False)], [((s, d), BF16, True)], s, d)[0]


def mid_fwd(x, mix, g_post, g_fpre):
    s, d = x.shape

    def kern(x_ref, mix_ref, gp_ref, gf_ref, h1_ref, u2_ref):
        mixv = mix_ref[...]
        h1 = x_ref[...] + mixv * _rstd(mixv) * gp_ref[...]
        h1_ref[...] = h1
        u2_ref[...] = (h1 * _rstd(h1) * gf_ref[...]).astype(BF16)

    return _row_call(kern, "mid_fwd", [(x, True), (mix, True), (g_post, False), (g_fpre, False)],
                     [((s, d), F32, True), ((s, d), BF16, True)], s, d)


def loss_head(h1, ff, g_fpost, target):
    s, d = h1.shape

    def kern(h1_ref, ff_ref, g_ref, t_ref, dy_ref, dff_ref, dg_ref, loss_ref):
        @pl.when(pl.program_id(0) == 0)
        def _():
            dg_ref[...] = jnp.zeros_like(dg_ref)
            loss_ref[...] = jnp.zeros_like(loss_ref)

        ffv = ff_ref[...]
        g = g_ref[...]
        y = h1_ref[...] + ffv * _rstd(ffv) * g
        diff = y - t_ref[...]
        row_loss = jnp.mean(diff * diff, axis=-1, keepdims=True)
        loss_ref[...] += 0.5 * jnp.sum(row_loss, axis=0, keepdims=True)
        dy = diff / d
        dy_ref[...] = dy
        dff, dg = _norm_bwd(ffv, g, dy)
        dff_ref[...] = dff.astype(BF16)
        dg_ref[...] += dg

    return _row_call(kern, "loss_head",
                     [(h1, True), (ff, True), (g_fpost, False), (target, True)],
                     [((s, d), F32, True), ((s, d), BF16, True), ((1, d), F32, False), ((1, 1), F32, False)], s, d)


def mid_bwd(dy, du2, h1, mix, g_fpre, g_post):
    s, d = dy.shape

    def kern(dy_ref, du2_ref, h1_ref, mix_ref, gf_ref, gp_ref, dh1_ref, dmix_ref, dgf_ref, dgp_ref):
        @pl.when(pl.program_id(0) == 0)
        def _():
            dgf_ref[...] = jnp.zeros_like(dgf_ref)
            dgp_ref[...] = jnp.zeros_like(dgp_ref)

        dh, dgf = _norm_bwd(h1_ref[...], gf_ref[...], du2_ref[...])
        dh1 = dy_ref[...] + dh
        dh1_ref[...] = dh1
        dmix, dgp = _norm_bwd(mix_ref[...], gp_ref[...], dh1)
        dmix_ref[...] = dmix.astype(BF16)
        dgf_ref[...] += dgf
        dgp_ref[...] += dgp

    return _row_call(kern, "mid_bwd",
                     [(dy, True), (du2, True), (h1, True), (mix, True), (g_fpre, False), (g_post, False)],
                     [((s, d), F32, True), ((s, d), BF16, True), ((1, d), F32, False), ((1, d), F32, False)], s, d)


def in_bwd(dh1, du, x, g_pre):
    s, d = x.shape

    def kern(dh1_ref, du_ref, x_ref, g_ref, dx_ref, dg_ref):
        @pl.when(pl.program_id(0) == 0)
        def _():
            dg_ref[...] = jnp.zeros_like(dg_ref)

        dxn, dg = _norm_bwd(x_ref[...], g_ref[...], du_ref[...])
        dx_ref[...] = dh1_ref[...] + dxn
        dg_ref[...] += dg

    return _row_call(kern, "in_bwd", [(dh1, True), (du, True), (x, True), (g_pre, False)],
                     [((s, d), F32, True), ((1, d), F32, False)], s, d)


def _sigmoid(v):
    return 1.0 / (1.0 + jnp.exp(-v))


def gate_fwd(bsb, bfx, gf):
    s, d = bsb.shape
    tr, tc = _tile(s, 256, 16), _tile(d, 512, LANES)
    nc = d // tc

    def kern(bsb_ref, bfx_ref, gs_ref, gx_ref, o_ref):
        o_ref[...] = (_sigmoid(gs_ref[...]) * bsb_ref[...] + _sigmoid(gx_ref[...]) * bfx_ref[...]).astype(BF16)

    blk = pl.BlockSpec((tr, tc), lambda i, j: (i, j))
    return pl.pallas_call(
        kern, name="gate_fwd", grid=(s // tr, nc),
        in_specs=[blk, blk, blk, pl.BlockSpec((tr, tc), lambda i, j: (i, j + nc))],
        out_specs=blk, out_shape=jax.ShapeDtypeStruct((s, d), BF16),
        compiler_params=_params("parallel", "parallel"),
    )(bsb, bfx, gf, gf)


def gate_bwd(dmerged, bsb, bfx, gf):
    s, d = bsb.shape
    tr, tc = _tile(s, 256, 16), _tile(d, 512, LANES)
    nc = d // tc

    def kern(dm_ref, bsb_ref, bfx_ref, gs_ref, gx_ref, dbs_ref, dbx_ref, dgs_ref, dgx_ref):
        dm = dm_ref[...]
        ss = _sigmoid(gs_ref[...])
        sx = _sigmoid(gx_ref[...])
        dbs_ref[...] = (dm * ss).astype(BF16)
        dbx_ref[...] = (dm * sx).astype(BF16)
        dgs_ref[...] = (dm * bsb_ref[...] * ss * (1.0 - ss)).astype(BF16)
        dgx_ref[...] = (dm * bfx_ref[...] * sx * (1.0 - sx)).astype(BF16)

    blk = pl.BlockSpec((tr, tc), lambda i, j: (i, j))
    out = jax.ShapeDtypeStruct((s, d), BF16)
    return pl.pallas_call(
        kern, name="gate_bwd", grid=(s // tr, nc),
        in_specs=[blk, blk, blk, blk, pl.BlockSpec((tr, tc), lambda i, j: (i, j + nc))],
        out_specs=[blk, blk, blk, blk], out_shape=[out, out, out, out],
        compiler_params=_params("parallel", "parallel"),
    )(dmerged, bsb, bfx, gf, gf)


def swiglu_fwd(gu, cw):
    s, f2 = gu.shape
    tr = _tile(s, 256, 16)

    def kern(gu_ref, o_ref):
        g = gu_ref[:, :cw]
        o_ref[...] = (g * _sigmoid(g) * gu_ref[:, cw:]).astype(BF16)

    return pl.pallas_call(
        kern, name="swiglu_fwd", grid=(s // tr, f2 // (2 * cw)),
        in_specs=[pl.BlockSpec((tr, 2 * cw), lambda i, j: (i, j))],
        out_specs=pl.BlockSpec((tr, cw), lambda i, j: (i, j)),
        out_shape=jax.ShapeDtypeStruct((s, f2 // 2), BF16),
        compiler_params=_params("parallel", "parallel"),
    )(gu)


def swiglu_bwd(dact, gu, cw):
    s, f2 = gu.shape
    tr = _tile(s, 256, 16)

    def kern(da_ref, gu_ref, o_ref):
        da = da_ref[...]
        g = gu_ref[:, :cw]
        sg = _sigmoid(g)
        o_ref[:, :cw] = (da * gu_ref[:, cw:] * (sg * (1.0 + g * (1.0 - sg)))).astype(BF16)
        o_ref[:, cw:] = (da * (g * sg)).astype(BF16)

    return pl.pallas_call(
        kern, name="swiglu_bwd", grid=(s // tr, f2 // (2 * cw)),
        in_specs=[pl.BlockSpec((tr, cw), lambda i, j: (i, j)), pl.BlockSpec((tr, 2 * cw), lambda i, j: (i, j))],
        out_specs=pl.BlockSpec((tr, 2 * cw), lambda i, j: (i, j)),
        out_shape=jax.ShapeDtypeStruct((s, f2), BF16),
        compiler_params=_params("parallel", "parallel"),
    )(dact, gu)


def _split3(v):
    hi = v.astype(BF16)
    r = v - hi.astype(F32)
    mid = r.astype(BF16)
    lo = (r - mid.astype(F32)).astype(BF16)
    return hi, mid, lo


def _dot3_right(v, ones):
    hi, mid, lo = _split3(v)
    d = lambda p: jnp.dot(p, ones, preferred_element_type=F32)
    return (d(lo) + d(mid)) + d(hi)


def _dot3_left(ones, v):
    hi, mid, lo = _split3(v)
    d = lambda p: jnp.dot(ones, p, preferred_element_type=F32)
    return (d(lo) + d(mid)) + d(hi)


def _split2(v):
    hi = v.astype(BF16)
    return hi, (v - hi.astype(F32)).astype(BF16)


def _dot2_right(v, ones):
    hi, lo = _split2(v)
    return jnp.dot(lo, ones, preferred_element_type=F32) + jnp.dot(hi, ones, preferred_element_type=F32)


def _log1p_exp_neg_abs(v):
    return jnp.log(1.0 + jnp.exp(-jnp.abs(v)))


def _mask01(cond):
    return jnp.where(cond, 1.0, 0.0).astype(BF16)


def _iota2(t):
    return (lax.broadcasted_iota(jnp.int32, (t, t), 0), lax.broadcasted_iota(jnp.int32, (t, t), 1))


def cum_fwd(gf, b_pad, f_col0):
    s = gf.shape[0]
    t = _tile(s, ATT_TILE, LANES)
    fb = f_col0 // LANES

    def kern(f_ref, b_ref, cum_ref, carry_ref):
        @pl.when(pl.program_id(0) == 0)
        def _():
            carry_ref[...] = jnp.zeros_like(carry_ref)

        v = f_ref[...] + b_ref[...]
        lf = jnp.minimum(v, 0.0) - _log1p_exp_neg_abs(v)
        row, col = _iota2(t)
        cum = _dot3_left(_mask01(col <= row), lf) + carry_ref[...]
        cum_ref[...] = cum
        carry_ref[...] = cum[t - 1:t, :]

    return pl.pallas_call(
        kern, name="cum_fwd", grid=(s // t,),
        in_specs=[pl.BlockSpec((t, LANES), lambda i: (i, fb)), pl.BlockSpec((1, LANES), lambda i: (0, 0))],
        out_specs=pl.BlockSpec((t, LANES), lambda i: (i, 0)),
        out_shape=jax.ShapeDtypeStruct((s, LANES), F32),
        scratch_shapes=[pltpu.VMEM((1, LANES), F32)],
        compiler_params=_params("arbitrary"),
    )(gf, b_pad)


def cum_bwd(dcum, gf, b_pad, f_col0, n_heads):
    s = gf.shape[0]
    t = _tile(s, ATT_TILE, LANES)
    nb = s // t
    fb = f_col0 // LANES

    def kern(dc_ref, f_ref, b_ref, df_ref, db_ref, carry_ref):
        @pl.when(pl.program_id(0) == 0)
        def _():
            carry_ref[...] = jnp.zeros_like(carry_ref)
            db_ref[...] = jnp.zeros_like(db_ref)

        row, col = _iota2(t)
        dlf = _dot3_left(_mask01(col >= row), dc_ref[...]) + carry_ref[...]
        carry_ref[...] = dlf[0:1, :]
        v = f_ref[...] + b_ref[...]
        sig_neg = jnp.exp(-jnp.maximum(v, 0.0) - _log1p_exp_neg_abs(v))
        lane = lax.broadcasted_iota(jnp.int32, (t, LANES), 1)
        df = jnp.where(lane < n_heads, dlf * sig_neg, 0.0)
        df_ref[...] = df.astype(BF16)
        db_ref[...] += jnp.sum(df, axis=0, keepdims=True)

    return pl.pallas_call(
        kern, name="cum_bwd", grid=(nb,),
        in_specs=[pl.BlockSpec((t, LANES), lambda i: (nb - 1 - i, 0)),
                  pl.BlockSpec((t, LANES), lambda i: (nb - 1 - i, fb)),
                  pl.BlockSpec((1, LANES), lambda i: (0, 0))],
        out_specs=[pl.BlockSpec((t, LANES), lambda i: (nb - 1 - i, 0)), pl.BlockSpec((1, LANES), lambda i: (0, 0))],
        out_shape=[jax.ShapeDtypeStruct((s, LANES), BF16), jax.ShapeDtypeStruct((1, LANES), F32)],
        scratch_shapes=[pltpu.VMEM((1, LANES), F32)],
        compiler_params=_params("arbitrary"),
    )(dcum, gf, b_pad)


def _qkv_specs(s, t, n_heads, base):
    return [pl.BlockSpec((t, HEAD_DIM), lambda h, i: (i, base + h)),
            pl.BlockSpec((s, HEAD_DIM), lambda h, i: (0, base + n_heads + h)),
            pl.BlockSpec((s, HEAD_DIM), lambda h, i: (0, base + 2 * n_heads + h))]


def _strips(t):
    sr = _tile(t, ATT_STRIP, 8)
    return sr, t // sr, [slice(si * sr, (si + 1) * sr) for si in range(t // sr)]


def _key_minus_row(sr, t):
    return lax.broadcasted_iota(jnp.int32, (sr, t), 1) - lax.broadcasted_iota(jnp.int32, (sr, t), 0)


def _sb_scores(q, k, diff, lim):
    z = lax.dot_general(q, k, NT, preferred_element_type=F32) * (HEAD_DIM ** -0.5)
    valid = diff < lim
    l1p = _log1p_exp_neg_abs(z)
    log_keep = jnp.where(valid, -jnp.maximum(z, 0.0) - l1p, 0.0)
    return z, valid, l1p, log_keep


def sb_fwd(qkv, n_heads, base):
    s = qkv.shape[0]
    t = _tile(s, ATT_TILE, LANES)
    sr, ns, strips = _strips(t)

    def kern(q_ref, k_ref, v_ref, o_ref):
        i = pl.program_id(1)
        row, col = _iota2(t)
        after = _mask01(row > col)
        diff = _key_minus_row(sr, t)
        qs = [q_ref[sl, :] for sl in strips]

        def body(jj, carry):
            runs, accs = carry
            j = i - jj
            off = pl.multiple_of(j * t, t)
            k = k_ref[pl.ds(off, t), :]
            v = v_ref[pl.ds(off, t), :]
            lim = jnp.where(j < i, t, 0)
            new_runs, new_accs = [], []
            for si in range(ns):
                z, valid, l1p, log_keep = _sb_scores(qs[si], k, diff, lim + si * sr)
                between = _dot2_right(log_keep, after) + runs[si]
                w = jnp.where(valid, jnp.exp(jnp.minimum(z, 0.0) - l1p + between), 0.0)
                new_accs.append(accs[si] + jnp.dot(w.astype(BF16), v, preferred_element_type=F32))
                new_runs.append(runs[si] + jnp.sum(log_keep, axis=1, keepdims=True))
            return tuple(new_runs), tuple(new_accs)

        init = (tuple(jnp.zeros((sr, 1), F32) for _ in strips), tuple(jnp.zeros((sr, HEAD_DIM), F32) for _ in strips))
        _, accs = lax.fori_loop(0, i + 1, body, init)
        for sl, acc in zip(strips, accs):
            o_ref[sl, :] = acc.astype(o_ref.dtype)

    return pl.pallas_call(
        kern, name="sb_fwd", grid=(n_heads, s // t),
        in_specs=_qkv_specs(s, t, n_heads, base),
        out_specs=pl.BlockSpec((t, HEAD_DIM), lambda h, i: (i, h)),
        out_shape=jax.ShapeDtypeStruct((s, n_heads * HEAD_DIM), BF16),
        compiler_params=_params("parallel", "arbitrary"),
    )(qkv, qkv, qkv)


def sb_bwd(qkv, d_o, n_heads, base):
    s = qkv.shape[0]
    t = _tile(s, ATT_TILE, LANES)
    nq = s // t
    sr, ns, strips = _strips(t)
    scale = HEAD_DIM ** -0.5

    def kern(q_ref, k_ref, v_ref, do_ref, dq_ref, dk_ref, dv_ref, dk_acc, dv_acc, run_ref):
        i = pl.program_id(1)

        @pl.when(i == 0)
        def _():
            dk_acc[...] = jnp.zeros_like(dk_acc)
            dv_acc[...] = jnp.zeros_like(dv_acc)

        row, col = _iota2(t)
        after = _mask01(row > col)
        before = _mask01(row < col)
        diff = _key_minus_row(sr, t)
        qs = [q_ref[sl, :] for sl in strips]
        dos = [do_ref[sl, :] for sl in strips]

        def sweep1(jj, runs):
            j = i - jj
            k = k_ref[pl.ds(pl.multiple_of(j * t, t), t), :]
            lim = jnp.where(j < i, t, 0)
            new_runs = []
            for si, sl in enumerate(strips):
                _, _, _, log_keep = _sb_scores(qs[si], k, diff, lim + si * sr)
                run_ref[j, sl, :] = runs[si]
                new_runs.append(runs[si] + jnp.sum(log_keep, axis=1, keepdims=True))
            return tuple(new_runs)

        lax.fori_loop(0, i + 1, sweep1, tuple(jnp.zeros((sr, 1), F32) for _ in strips))

        def sweep2(j, carry):
            run_es, dqs = carry
            off = pl.multiple_of(j * t, t)
            k = k_ref[pl.ds(off, t), :]
            v = v_ref[pl.ds(off, t), :]
            lim = jnp.where(j < i, t, 0)
            new_es, new_dqs = [], []
            dk_t = jnp.zeros((t, HEAD_DIM), F32)
            dv_t = jnp.zeros((t, HEAD_DIM), F32)
            for si, sl in enumerate(strips):
                z, valid, l1p, log_keep = _sb_scores(qs[si], k, diff, lim + si * sr)
                between = _dot2_right(log_keep, after) + run_ref[j, sl, :]
                w = jnp.where(valid, jnp.exp(jnp.minimum(z, 0.0) - l1p + between), 0.0)
                dw = lax.dot_general(dos[si], v, NT, preferred_element_type=F32)
                e = dw * w
                e_before = _dot2_right(e, before) + run_es[si]
                keep = jnp.exp(log_keep)
                dz = jnp.where(valid, e * keep - e_before * (1.0 - keep), 0.0) * scale
                dzb = dz.astype(BF16)
                new_dqs.append(dqs[si] + jnp.dot(dzb, k, preferred_element_type=F32))
                dk_t = dk_t + lax.dot_general(dzb, qs[si], TN, preferred_element_type=F32)
                dv_t = dv_t + lax.dot_general(w.astype(BF16), dos[si], TN, preferred_element_type=F32)
                new_es.append(run_es[si] + jnp.sum(e, axis=1, keepdims=True))
            dk_acc[pl.ds(off, t), :] += dk_t
            dv_acc[pl.ds(off, t), :] += dv_t
            return tuple(new_es), tuple(new_dqs)

        init = (tuple(jnp.zeros((sr, 1), F32) for _ in strips), tuple(jnp.zeros((sr, HEAD_DIM), F32) for _ in strips))
        _, dqs = lax.fori_loop(0, i + 1, sweep2, init)
        for sl, dq in zip(strips, dqs):
            dq_ref[sl, :] = dq.astype(BF16)

        @pl.when(i == nq - 1)
        def _():
            dk_ref[...] = dk_acc[...].astype(BF16)
            dv_ref[...] = dv_acc[...].astype(BF16)

    out = jax.ShapeDtypeStruct((s, n_heads * HEAD_DIM), BF16)
    head_blk = pl.BlockSpec((s, HEAD_DIM), lambda h, i: (0, h))
    tile_blk = pl.BlockSpec((t, HEAD_DIM), lambda h, i: (i, h))
    return pl.pallas_call(
        kern, name="sb_bwd", grid=(n_heads, nq),
        in_specs=_qkv_specs(s, t, n_heads, base) + [tile_blk],
        out_specs=[tile_blk, head_blk, head_blk],
        out_shape=[out, out, out],
        scratch_shapes=[pltpu.VMEM((s, HEAD_DIM), F32), pltpu.VMEM((s, HEAD_DIM), F32), pltpu.VMEM((nq, t, 1), F32)],
        compiler_params=_params("parallel", "arbitrary"),
    )(qkv, qkv, qkv, d_o)


def _fox_scores(q, k, cq, ck, diff, lim):
    sc = lax.dot_general(q, k, NT, preferred_element_type=F32) * (HEAD_DIM ** -0.5)
    sc = sc + cq - ck
    valid = diff < lim
    return jnp.where(valid, sc, NEG_BIG), valid


def fox_fwd(qkv, cum_col, cum_row, n_heads, base):
    s = qkv.shape[0]
    t = _tile(s, ATT_TILE, LANES)
    sr, ns, strips = _strips(t)

    def kern(q_ref, k_ref, v_ref, cq_ref, ck_ref, o_ref, lse_ref):
        i = pl.program_id(1)
        diff = _key_minus_row(sr, t)
        qs = [q_ref[sl, :] for sl in strips]
        cqs = [cq_ref[0, sl, :] for sl in strips]

        def body(j, carry):
            off = pl.multiple_of(j * t, t)
            k = k_ref[pl.ds(off, t), :]
            v = v_ref[pl.ds(off, t), :]
            ck = ck_ref[0, :, pl.ds(off, t)]
            lim = jnp.where(j < i, t, 1)
            out = []
            for si in range(ns):
                m, l, acc = carry[si]
                sc, _ = _fox_scores(qs[si], k, cqs[si], ck, diff, lim + si * sr)
                m_new = jnp.maximum(m, jnp.max(sc, axis=1, keepdims=True))
                p = jnp.exp(sc - m_new)
                alpha = jnp.exp(m - m_new)
                l = alpha * l + jnp.sum(p, axis=1, keepdims=True)
                acc = alpha * acc + jnp.dot(p.astype(BF16), v, preferred_element_type=F32)
                out.append((m_new, l, acc))
            return tuple(out)

        init = tuple((jnp.full((sr, 1), NEG_BIG, F32), jnp.zeros((sr, 1), F32), jnp.zeros((sr, HEAD_DIM), F32))
                     for _ in strips)
        res = lax.fori_loop(0, i + 1, body, init)
        for sl, (m, l, acc) in zip(strips, res):
            o_ref[sl, :] = acc / l
            lse_ref[0, sl, :] = m + jnp.log(l)

    col_blk = pl.BlockSpec((1, t, 1), lambda h, i: (h, i, 0))
    return pl.pallas_call(
        kern, name="fox_fwd", grid=(n_heads, s // t),
        in_specs=_qkv_specs(s, t, n_heads, base) + [col_blk, pl.BlockSpec((1, 1, s), lambda h, i: (h, 0, 0))],
        out_specs=[pl.BlockSpec((t, HEAD_DIM), lambda h, i: (i, h)), col_blk],
        out_shape=[jax.ShapeDtypeStruct((s, n_heads * HEAD_DIM), F32), jax.ShapeDtypeStruct((n_heads, s, 1), F32)],
        compiler_params=_params("parallel", "arbitrary"),
    )(qkv, qkv, qkv, cum_col, cum_row)


def fox_bwd(qkv, cum_col, cum_row, o, d_o, lse, n_heads, base):
    s = qkv.shape[0]
    t = _tile(s, ATT_TILE, LANES)
    nq = s // t
    sr, ns, strips = _strips(t)
    scale = HEAD_DIM ** -0.5

    def kern(q_ref, k_ref, v_ref, cq_ref, ck_ref, o_ref, do_ref, lse_ref,
             dq_ref, dk_ref, dv_ref, dcq_ref, dck_ref, dk_acc, dv_acc, dck_acc):
        i = pl.program_id(1)

        @pl.when(i == 0)
        def _():
            dk_acc[...] = jnp.zeros_like(dk_acc)
            dv_acc[...] = jnp.zeros_like(dv_acc)
            dck_acc[...] = jnp.zeros_like(dck_acc)

        diff = _key_minus_row(sr, t)
        qs = [q_ref[sl, :] for sl in strips]
        dos = [do_ref[sl, :] for sl in strips]
        cqs = [cq_ref[0, sl, :] for sl in strips]
        lses = [lse_ref[0, sl, :] for sl in strips]
        deltas = [jnp.sum(dos[si].astype(F32) * o_ref[sl, :], axis=1, keepdims=True) for si, sl in enumerate(strips)]

        def body(j, carry):
            off = pl.multiple_of(j * t, t)
            k = k_ref[pl.ds(off, t), :]
            v = v_ref[pl.ds(off, t), :]
            ck = ck_ref[0, :, pl.ds(off, t)]
            lim = jnp.where(j < i, t, 1)
            out = []
            dk_t = jnp.zeros((t, HEAD_DIM), F32)
            dv_t = jnp.zeros((t, HEAD_DIM), F32)
            dck_t = jnp.zeros((1, t), F32)
            for si in range(ns):
                dq, dcq = carry[si]
                sc, valid = _fox_scores(qs[si], k, cqs[si], ck, diff, lim + si * sr)
                p = jnp.where(valid, jnp.exp(sc - lses[si]), 0.0)
                dp = lax.dot_general(dos[si], v, NT, preferred_element_type=F32)
                ds = p * (dp - deltas[si])
                dsb = (ds * scale).astype(BF16)
                dq = dq + jnp.dot(dsb, k, preferred_element_type=F32)
                dk_t = dk_t + lax.dot_general(dsb, qs[si], TN, preferred_element_type=F32)
                dv_t = dv_t + lax.dot_general(p.astype(BF16), dos[si], TN, preferred_element_type=F32)
                dck_t = dck_t + jnp.sum(ds, axis=0, keepdims=True)
                out.append((dq, dcq + jnp.sum(ds, axis=1, keepdims=True)))
            dk_acc[pl.ds(off, t), :] += dk_t
            dv_acc[pl.ds(off, t), :] += dv_t
            dck_acc[:, pl.ds(off, t)] -= dck_t
            return tuple(out)

        init = tuple((jnp.zeros((sr, HEAD_DIM), F32), jnp.zeros((sr, 1), F32)) for _ in strips)
        res = lax.fori_loop(0, i + 1, body, init)
        for sl, (dq, dcq) in zip(strips, res):
            dq_ref[sl, :] = dq.astype(BF16)
            dcq_ref[0, sl, :] = dcq

        @pl.when(i == nq - 1)
        def _():
            dk_ref[...] = dk_acc[...].astype(BF16)
            dv_ref[...] = dv_acc[...].astype(BF16)
            dck_ref[0] = dck_acc[...]

    out = jax.ShapeDtypeStruct((s, n_heads * HEAD_DIM), BF16)
    head_blk = pl.BlockSpec((s, HEAD_DIM), lambda h, i: (0, h))
    tile_blk = pl.BlockSpec((t, HEAD_DIM), lambda h, i: (i, h))
    col_blk = pl.BlockSpec((1, t, 1), lambda h, i: (h, i, 0))
    row_blk = pl.BlockSpec((1, 1, s), lambda h, i: (h, 0, 0))
    return pl.pallas_call(
        kern, name="fox_bwd", grid=(n_heads, nq),
        in_specs=_qkv_specs(s, t, n_heads, base) + [col_blk, row_blk, tile_blk, tile_blk, col_blk],
        out_specs=[tile_blk, head_blk, head_blk, col_blk, row_blk],
        out_shape=[out, out, out, jax.ShapeDtypeStruct((n_heads, s, 1), F32),
                   jax.ShapeDtypeStruct((n_heads, 1, s), F32)],
        scratch_shapes=[pltpu.VMEM((s, HEAD_DIM), F32), pltpu.VMEM((s, HEAD_DIM), F32), pltpu.VMEM((1, s), F32)],
        compiler_params=_params("parallel", "arbitrary"),
    )(qkv, qkv, qkv, cum_col, cum_row, o, d_o, lse)


def _place():
    x, y, c = lax.axis_index("x"), lax.axis_index("y"), lax.axis_index("c")
    other_chips = [(1 - x, y), (x, 1 - y), (1 - x, 1 - y)]
    return x, y, c, other_chips


ANY = pl.BlockSpec(memory_space=pl.ANY)


def _remote(src, dst, send_sem, recv_sem, dev):
    return pltpu.make_async_remote_copy(src_ref=src, dst_ref=dst, send_sem=send_sem, recv_sem=recv_sem,
                                        device_id=dev, device_id_type=MESH)


def cast_place(name, ws, chip):
    r = ws[0].shape[0]
    cs = [w.shape[1] for w in ws]
    tr = _tile(r, 256, 16)

    def kern(chip_ref, *refs):
        o_ref = refs[-1]
        off = 0
        for w_ref, c in zip(refs[:-1], cs):
            o_ref[:, off:off + c] = w_ref[...].astype(BF16)
            off += c

    return pl.pallas_call(
        kern, name=name,
        grid_spec=pltpu.PrefetchScalarGridSpec(
            num_scalar_prefetch=1, grid=(r // tr,),
            in_specs=[pl.BlockSpec((tr, c), lambda i, chip_ref: (i, 0)) for c in cs],
            out_specs=pl.BlockSpec((None, tr, sum(cs)), lambda i, chip_ref: (chip_ref[0], i, 0))),
        out_shape=jax.ShapeDtypeStruct((N_CHIPS, r, sum(cs)), BF16),
        compiler_params=_params("parallel"),
    )(chip, *ws)


def all_gather(bufs):
    n = len(bufs)
    halves = [b.shape[1] // 2 for b in bufs]

    def body(*refs):
        outs = refs[n:2 * n]
        ici_send, ici_recv, d2d_send, d2d_recv = refs[2 * n:]
        x, y, c, chips = _place()
        me = 2 * x + y
        sibling = (x, y, 1 - c)

        def rows(a, k, core):
            return outs[a].at[k, pl.ds(pl.multiple_of(core * halves[a], 16), halves[a])]

        def chip_of(j):
            return 2 * chips[j][0] + chips[j][1]

        def ici(a, j, k):
            return _remote(rows(a, k, c), rows(a, k, c), ici_send.at[3 * a + j], ici_recv.at[3 * a + j],
                           (chips[j][0], chips[j][1], c))

        def d2d(a, j, core):
            return _remote(rows(a, chip_of(j), core), rows(a, chip_of(j), core), d2d_send.at[3 * a + j],
                           d2d_recv.at[3 * a + j], sibling)

        pairs = [(a, j) for a in range(n) for j in range(3)]
        for a, j in pairs:
            ici(a, j, me).start()
        for a, j in pairs:
            ici(a, j, chip_of(j)).wait_recv()
            d2d(a, j, c).start()
        for a, j in pairs:
            d2d(a, j, 1 - c).wait_recv()
        for a, j in pairs:
            ici(a, j, me).wait_send()
            d2d(a, j, c).wait_send()

    return pl.pallas_call(
        body, name="all_gather", in_specs=[ANY] * n, out_specs=[ANY] * n,
        out_shape=[jax.ShapeDtypeStruct(b.shape, b.dtype) for b in bufs],
        input_output_aliases={a: a for a in range(n)},
        scratch_shapes=[pltpu.SemaphoreType.DMA((3 * n,)) for _ in range(4)],
    )(*bufs)


def swap_halves(pieces):
    n = len(pieces)
    halves = [p.shape[1] // 2 for p in pieces]

    def body(*refs):
        ins, outs = refs[:n], refs[n:2 * n]
        send_sems, recv_sems = refs[2 * n:]
        x, y, c, _ = _place()
        cps = [_remote(ins[a].at[:, pl.ds(pl.multiple_of((1 - c) * halves[a], 16), halves[a]), :], outs[a],
                       send_sems.at[a], recv_sems.at[a], (x, y, 1 - c)) for a in range(n)]
        for cp in cps:
            cp.start()
        for cp in cps:
            cp.wait()

    return pl.pallas_call(
        body, name="swap_halves", in_specs=[ANY] * n, out_specs=[ANY] * n,
        out_shape=[jax.ShapeDtypeStruct((N_CHIPS, h, p.shape[2]), p.dtype) for p, h in zip(pieces, halves)],
        scratch_shapes=[pltpu.SemaphoreType.DMA((n,)), pltpu.SemaphoreType.DMA((n,))],
    )(*pieces)


def pair_sum(name, pieces, got, core):
    _, r, w = pieces.shape
    half = r // 2
    tr = _tile(half, 256, 16)

    def kern(core_ref, p_ref, g_ref, o_ref):
        o_ref[...] = (p_ref[...].astype(F32) + g_ref[...].astype(F32)).astype(o_ref.dtype)

    return pl.pallas_call(
        kern, name=name,
        grid_spec=pltpu.PrefetchScalarGridSpec(
            num_scalar_prefetch=1, grid=(N_CHIPS, half // tr),
            in_specs=[pl.BlockSpec((None, None, tr, w), lambda k, i, core_ref: (k, core_ref[0], i, 0)),
                      pl.BlockSpec((None, tr, w), lambda k, i, core_ref: (k, i, 0))],
            out_specs=pl.BlockSpec((None, tr, w), lambda k, i, core_ref: (k, i, 0))),
        out_shape=jax.ShapeDtypeStruct((N_CHIPS, half, w), pieces.dtype),
        compiler_params=_params("parallel", "parallel"),
    )(core, pieces.reshape(N_CHIPS, 2, half, w), got)


def scatter_chips(sums):
    n = len(sums)

    def body(*refs):
        ins, outs = refs[:n], refs[n:2 * n]
        send_sems, recv_sems = refs[2 * n:]
        x, y, c, chips = _place()
        cps = [_remote(ins[a].at[2 * chips[j][0] + chips[j][1]], outs[a].at[j], send_sems.at[3 * a + j],
                       recv_sems.at[3 * a + j], (chips[j][0], chips[j][1], c))
               for a in range(n) for j in range(3)]
        for cp in cps:
            cp.start()
        for cp in cps:
            cp.wait()

    return pl.pallas_call(
        body, name="scatter_chips", in_specs=[ANY] * n, out_specs=[ANY] * n,
        out_shape=[jax.ShapeDtypeStruct((3,) + t.shape[1:], t.dtype) for t in sums],
        scratch_shapes=[pltpu.SemaphoreType.DMA((3 * n,)), pltpu.SemaphoreType.DMA((3 * n,))],
    )(*sums)


def chip_sum(name, sums, got, chip, core):
    _, half, w = sums.shape
    tr = _tile(half, 256, 16)
    nb = half // tr

    def kern(ids_ref, s_ref, g0_ref, g1_ref, g2_ref, o_ref):
        o_ref[...] = ((s_ref[...].astype(F32) + g0_ref[...].astype(F32)) + g1_ref[...].astype(F32)) \
            + g2_ref[...].astype(F32)

    def got_spec(j):
        return pl.BlockSpec((None, tr, w), lambda i, ids_ref: (j, i, 0))

    return pl.pallas_call(
        kern, name=name,
        grid_spec=pltpu.PrefetchScalarGridSpec(
            num_scalar_prefetch=1, grid=(nb,),
            in_specs=[pl.BlockSpec((None, tr, w), lambda i, ids_ref: (ids_ref[0], i, 0)),
                      got_spec(0), got_spec(1), got_spec(2)],
            out_specs=pl.BlockSpec((tr, w), lambda i, ids_ref: (ids_ref[1] * nb + i, 0))),
        out_shape=jax.ShapeDtypeStruct((2 * half, w), F32),
        compiler_params=_params("parallel"),
    )(jnp.concatenate([chip, core]), sums, got, got, got)


def join_halves(shards):
    n = len(shards)
    halves = [g.shape[0] // 2 for g in shards]

    def body(*refs):
        outs = refs[n:2 * n]
        send_sems, recv_sems = refs[2 * n:]
        x, y, c, _ = _place()
        cps = []
        for a in range(n):
            rows = outs[a].at[pl.ds(pl.multiple_of(c * halves[a], 8), halves[a])]
            cps.append(_remote(rows, rows, send_sems.at[a], recv_sems.at[a], (x, y, 1 - c)))
        for cp in cps:
            cp.start()
        for cp in cps:
            cp.wait()

    return pl.pallas_call(
        body, name="join_halves", in_specs=[ANY] * n, out_specs=[ANY] * n,
        out_shape=[jax.ShapeDtypeStruct(g.shape, g.dtype) for g in shards],
        input_output_aliases={a: a for a in range(n)},
        scratch_shapes=[pltpu.SemaphoreType.DMA((n,)), pltpu.SemaphoreType.DMA((n,))],
    )(*shards)


def _adam(w, g, m, v):
    m = ADAM_B1 * m + (1.0 - ADAM_B1) * g
    v = ADAM_B2 * v + (1.0 - ADAM_B2) * (g * g)
    m_hat = m / (1.0 - ADAM_B1 ** ADAM_STEP)
    v_hat = v / (1.0 - ADAM_B2 ** ADAM_STEP)
    delta = -ADAM_LR * (m_hat / (jnp.sqrt(v_hat) + ADAM_EPS) + ADAM_WD * w)
    return delta, m, v


def small_allreduce_adam(g_part, w, m, v):
    n_dev = 8
    r, d = g_part.shape

    def body(g_ref, w_ref, m_ref, v_ref, gs_ref, dl_ref, nm_ref, nv_ref, all_ref, send_sems, recv_sems):
        x, y, c, _ = _place()
        me = 4 * x + 2 * y + c
        all_ref[me] = g_ref[...]
        cps = []
        for rel in range(1, n_dev):
            px = 1 - x if rel & 4 else x
            py = 1 - y if rel & 2 else y
            pc = 1 - c if rel & 1 else c
            cps.append(_remote(g_ref, all_ref.at[me], send_sems.at[rel - 1], recv_sems.at[rel - 1], (px, py, pc)))
        for cp in cps:
            cp.start()
        for cp in cps:
            cp.wait()
        total = all_ref[0]
        for dev in range(1, n_dev):
            total = total + all_ref[dev]
        gs_ref[...] = total
        delta, nm, nv = _adam(w_ref[...], total, m_ref[...], v_ref[...])
        dl_ref[...] = delta
        nm_ref[...] = nm
        nv_ref[...] = nv

    vm = pl.BlockSpec(memory_space=pltpu.VMEM)
    out = jax.ShapeDtypeStruct((r, d), F32)
    return pl.pallas_call(
        body, name="small_allreduce_adam", in_specs=[vm, vm, vm, vm], out_specs=[vm, vm, vm, vm],
        out_shape=[out, out, out, out],
        scratch_shapes=[pltpu.VMEM((n_dev, r, d), F32), pltpu.SemaphoreType.DMA((n_dev - 1,)),
                        pltpu.SemaphoreType.DMA((n_dev - 1,))],
    )(g_part, w, m, v)


def adam_update(name, w, m, v, g_buf, col_blk, copy_g):
    r, c = w.shape
    tr = _tile(r, 128, 8)

    def kern(w_ref, m_ref, v_ref, g_ref, *outs):
        g = g_ref[...]
        delta, nm, nv = _adam(w_ref[...], g, m_ref[...], v_ref[...])
        if copy_g:
            outs[0][...] = g
        outs[-3][...] = delta
        outs[-2][...] = nm
        outs[-1][...] = nv

    blk = pl.BlockSpec((tr, c), lambda i: (i, 0))
    out = jax.ShapeDtypeStruct((r, c), F32)
    n_out = 4 if copy_g else 3
    return pl.pallas_call(
        kern, name=name, grid=(r // tr,),
        in_specs=[blk, blk, blk, pl.BlockSpec((tr, c), lambda i: (i, col_blk))],
        out_specs=[blk] * n_out, out_shape=[out] * n_out, compiler_params=_params("parallel"),
    )(w, m, v, g_buf)


def _col_pieces(slab_of, lo, hi, cw):
    out = []
    while lo < hi:
        k, a = divmod(lo, cw)
        b = min(cw, a + hi - lo)
        out.append(slab_of(k)[:, a:b])
        lo += b - a
    return out


def kernel(x, norm_mix_pre, norm_mix_post, w_in, b_forget, w_branch_sb, w_branch_fox, w_out, norm_ffn_pre, norm_ffn_post, w_ffn_gate, w_ffn_up, w_ffn_down, loss_target, m_norm_mix_pre, m_norm_mix_post, m_w_in, m_b_forget, m_w_branch_sb, m_w_branch_fox, m_w_out, m_norm_ffn_pre, m_norm_ffn_post, m_w_ffn_gate, m_w_ffn_up, m_w_ffn_down, v_norm_mix_pre, v_norm_mix_post, v_w_in, v_b_forget, v_w_branch_sb, v_w_branch_fox, v_w_out, v_norm_ffn_pre, v_norm_ffn_post, v_w_ffn_gate, v_w_ffn_up, v_w_ffn_down):
    s, d = x.shape[1], x.shape[2]
    n_heads = b_forget.shape[1]
    d_att = n_heads * HEAD_DIM
    c_in = w_in.shape[2]
    c_br = w_branch_sb.shape[2]
    c_gu = w_ffn_gate.shape[2]
    d_ff = c_gu * N_CHIPS
    d_in = c_in * N_CHIPS
    f_pad = 512
    n_qkv = 6 * d_att
    n_gf = 2 * d + f_pad
    core = lax.axis_index("c").astype(jnp.int32).reshape(1)
    chip = (2 * lax.axis_index("x") + lax.axis_index("y")).astype(jnp.int32).reshape(1)

    g_in, g_br, g_out, g_gu, g_dn = all_gather([
        cast_place("place_w_in", [w_in[0]], chip),
        cast_place("place_branch", [w_branch_sb[0], w_branch_fox[0]], chip),
        cast_place("place_out", [w_out[0]], chip),
        cast_place("place_gate_up", [w_ffn_gate[0], w_ffn_up[0]], chip),
        cast_place("place_down", [w_ffn_down[0]], chip)])
    slab = lambda k: g_in[k]
    w_main = jnp.concatenate(
        _col_pieces(slab, 0, n_qkv, c_in) + _col_pieces(slab, n_qkv + n_heads, d_in, c_in)
        + _col_pieces(slab, n_qkv, n_qkv + n_heads, c_in) + [jnp.zeros((d, f_pad - n_heads), BF16)], axis=1)
    w_o = g_out.reshape(d, d)
    w_dn = g_dn.reshape(d_ff, d)

    x2 = x[0]
    tgt = loss_target[0]
    b_pad = jnp.pad(b_forget, ((0, 0), (0, LANES - n_heads)))

    u = norm_in(x2, norm_mix_pre)
    qkv = mm(u, w_main, "nn", BF16, "proj_qkv", b_win=(0, n_qkv))
    gf = mm(u, w_main, "nn", F32, "proj_gates", b_win=(n_qkv, n_gf))
    cum = cum_fwd(gf, b_pad, 2 * d)
    cum_heads = cum[:, :n_heads].T
    cum_col, cum_row = cum_heads[:, :, None], cum_heads[:, None, :]
    o_sb = sb_fwd(qkv, n_heads, 0)
    o_fx, lse = fox_fwd(qkv, cum_col, cum_row, n_heads, 3 * n_heads)
    bsb = mm(o_sb, g_br, "nn", F32, "branch_sb", tn=c_br, chunks=(1, 0))
    bfx = mm(o_fx, g_br, "nn", F32, "branch_fox", tn=c_br, chunks=(1, 1))
    merged = gate_fwd(bsb, bfx, gf)
    mix = mm(merged, w_o, "nn", F32, "out_proj")
    h1, u2 = mid_fwd(x2, mix, norm_mix_post, norm_ffn_pre)
    gu = mm(u2, g_gu, "nn", F32, "ffn_gate_up", tn=c_gu, chunks=(2, 0))
    act = swiglu_fwd(gu, c_gu)
    ff = mm(act, w_dn, "nn", F32, "ffn_down")
    dy, d_ff_out, dg_fpost, loss_part = loss_head(h1, ff, norm_ffn_post, tgt)

    p_dn = mm(act, d_ff_out, "tn", BF16, "dw_ffn_down").reshape(N_CHIPS, d_ff // N_CHIPS, d)
    d_act = mm(d_ff_out, w_dn, "nt", F32, "d_act")
    d_gu = swiglu_bwd(d_act, gu, c_gu)
    p_gu = mm(u2, d_gu, "tn", BF16, "dw_ffn_gate_up", tn=c_gu, chunks=(2, 0),
              out_into=lax.empty((N_CHIPS, d, 2 * c_gu), BF16))
    du2 = mm(d_gu, g_gu, "nt", F32, "d_u2", tk=c_gu, chunks=(2, 0))
    dh1, d_mix, dg_fpre, dg_post = mid_bwd(dy, du2, h1, mix, norm_ffn_pre, norm_mix_post)
    p_out = mm(merged, d_mix, "tn", BF16, "dw_out").reshape(N_CHIPS, d // N_CHIPS, d)
    d_merged = mm(d_mix, w_o, "nt", F32, "d_merged")
    d_bsb, d_bfx, d_gs, d_gx = gate_bwd(d_merged, bsb, bfx, gf)
    p_br = mm(o_sb, d_bsb, "tn", BF16, "dw_branch_sb", tn=c_br, chunks=(1, 0),
              out_into=lax.empty((N_CHIPS, d_att, 2 * c_br), BF16))
    p_br = mm(o_fx, d_bfx, "tn", BF16, "dw_branch_fox", tn=c_br, chunks=(1, 1), out_into=p_br)
    d_osb = mm(d_bsb, g_br, "nt", BF16, "d_o_sb", tk=c_br, chunks=(1, 0))
    d_ofx = mm(d_bfx, g_br, "nt", BF16, "d_o_fox", tk=c_br, chunks=(1, 1))
    dq_s, dk_s, dv_s = sb_bwd(qkv, d_osb, n_heads, 0)
    dq_f, dk_f, dv_f, dcq, dck = fox_bwd(qkv, cum_col, cum_row, o_fx, d_ofx, lse, n_heads, 3 * n_heads)
    d_cum = jnp.pad((dcq[:, :, 0] + dck[:, 0, :]).T, ((0, 0), (0, LANES - n_heads)))
    d_f, db_pad = cum_bwd(d_cum, gf, b_pad, 2 * d, n_heads)
    d_main = jnp.concatenate(
        [dq_s, dk_s, dv_s, dq_f, dk_f, dv_f, d_gs, d_gx, d_f, jnp.zeros((s, f_pad - LANES), BF16)], axis=1)
    dw_main = mm(u, d_main, "tn", BF16, "dw_in")
    du = mm(d_main, w_main, "nt", F32, "d_u")
    dx, dg_pre = in_bwd(dh1, du, x2, norm_mix_pre)

    def main_cols(lo, hi):
        out = []
        for s0, s1, m0 in [(0, n_qkv, 0), (n_qkv, n_qkv + n_heads, n_qkv + 2 * d), (n_qkv + n_heads, d_in, n_qkv)]:
            a, b = max(lo, s0), min(hi, s1)
            if a < b:
                out.append(dw_main[:, m0 + a - s0:m0 + b - s0])
        return out

    p_in = jnp.stack([jnp.concatenate(main_cols(k * c_in, (k + 1) * c_in), axis=1) for k in range(N_CHIPS)])

    pieces = [p_in, p_br, p_out, p_gu, p_dn]
    tags = ["in", "branch", "out", "gate_up", "down"]
    from_sibling = swap_halves(pieces)
    sums = [pair_sum("pair_sum_" + t, p, q, core) for t, p, q in zip(tags, pieces, from_sibling)]
    from_chips = scatter_chips(sums)
    gr_in, gr_br, gr_out, gr_gu, gr_dn = join_halves(
        [chip_sum("chip_sum_" + t, sm, got, chip, core) for t, sm, got in zip(tags, sums, from_chips)])

    g_in_, d_in_, m_in_, v_in_ = adam_update("adam_w_in", w_in[0], m_w_in[0], v_w_in[0], gr_in, 0, True)
    g_bs, d_bs, m_bs, v_bs = adam_update("adam_branch_sb", w_branch_sb[0], m_w_branch_sb[0], v_w_branch_sb[0], gr_br, 0, True)
    g_bf, d_bf, m_bf, v_bf = adam_update("adam_branch_fox", w_branch_fox[0], m_w_branch_fox[0], v_w_branch_fox[0], gr_br, 1, True)
    g_o_, d_o_, m_o_, v_o_ = adam_update("adam_out", w_out[0], m_w_out[0], v_w_out[0], gr_out, 0, True)
    g_ga, d_ga, m_ga, v_ga = adam_update("adam_gate", w_ffn_gate[0], m_w_ffn_gate[0], v_w_ffn_gate[0], gr_gu, 0, True)
    g_up, d_up, m_up, v_up = adam_update("adam_up", w_ffn_up[0], m_w_ffn_up[0], v_w_ffn_up[0], gr_gu, 1, True)
    g_dn_, d_dn, m_dn, v_dn = adam_update("adam_down", w_ffn_down[0], m_w_ffn_down[0], v_w_ffn_down[0], gr_dn, 0, True)
    lead = lambda arrs: [a[None] for a in arrs]
    grads = lead([g_in_, g_bs, g_bf, g_o_, g_ga, g_up, g_dn_])
    deltas = lead([d_in_, d_bs, d_bf, d_o_, d_ga, d_up, d_dn])
    new_ms = lead([m_in_, m_bs, m_bf, m_o_, m_ga, m_up, m_dn])
    new_vs = lead([v_in_, v_bs, v_bf, v_o_, v_ga, v_up, v_dn])

    def pack(rows):
        rows = [jnp.pad(r_, ((0, 0), (0, d - r_.shape[1]))) for r_ in rows]
        return jnp.concatenate(rows + [jnp.zeros((8 - len(rows), d), F32)], axis=0)

    sm_g, sm_d, sm_m, sm_v = small_allreduce_adam(
        pack([dg_pre, dg_post, dg_fpre, dg_fpost, db_pad]),
        pack([norm_mix_pre, norm_mix_post, norm_ffn_pre, norm_ffn_post, b_forget]),
        pack([m_norm_mix_pre, m_norm_mix_post, m_norm_ffn_pre, m_norm_ffn_post, m_b_forget]),
        pack([v_norm_mix_pre, v_norm_mix_post, v_norm_ffn_pre, v_norm_ffn_post, v_b_forget]))

    def small(a):
        return [a[0:1], a[1:2], a[2:3], a[3:4], a[4:5, :n_heads]]

    def ordered(sm, bg):
        return [sm[0], sm[1], bg[0], sm[4], bg[1], bg[2], bg[3], sm[2], sm[3], bg[4], bg[5], bg[6]]

    loss = lax.psum(loss_part[0, 0], ("x", "y", "c"))
    return (loss, dx[None], *ordered(small(sm_g), grads), *ordered(small(sm_d), deltas),
            *ordered(small(sm_m), new_ms), *ordered(small(sm_v), new_vs))
```

```python
import functools

import jax
import jax.numpy as jnp
from jax import lax
from jax.experimental import pallas as pl
from jax.experimental.pallas import tpu as pltpu

F32 = jnp.float32
BF16 = jnp.bfloat16
MESH = pl.DeviceIdType.MESH

HEAD_DIM = 128
LANES = 128
ATT_TILE = 512
ROW_TILE = 256
N_CHIPS = 4
RMS_EPS = 1e-6
ADAM_LR = 0.001
ADAM_B1 = 0.9
ADAM_B2 = 0.999
ADAM_EPS = 1e-08
ADAM_WD = 0.01
ADAM_STEP = 10
NEG_BIG = -1e30
VMEM_LIMIT = 56 * 1024 * 1024
MM_VMEM_BUDGET = 40 * 1024 * 1024
ATT_STRIP = 512

NN = (((1,), (0,)), ((), ()))
NT = (((1,), (1,)), ((), ()))
TN = (((0,), (0,)), ((), ()))


def _tile(n, pref, align):
    best = None
    t = align
    while t <= min(n, pref):
        if n % t == 0:
            best = t
        t += align
    return n if best is None else best


def _params(*sem):
    return pltpu.CompilerParams(dimension_semantics=sem, vmem_limit_bytes=VMEM_LIMIT)


def _mm_tiles(m, n, k, a_bytes, b_bytes, out_bytes, tn, tk):
    tm = _tile(m, 2048, LANES)
    tk = tk or _tile(k, 512, LANES)

    def vmem(t):
        acc = 0 if out_bytes == 4 else tm * t * 4
        return acc + 2 * tm * t * out_bytes + 2 * (tm * tk * a_bytes + tk * t * b_bytes)

    if tn is None:
        fits = [t for t in range(LANES, min(n, 2048) + 1, LANES) if n % t == 0 and vmem(t) <= MM_VMEM_BUDGET]
        tn = max(fits) if fits else _tile(n, LANES, LANES)
    return tm, tn, tk


def mm(a, b, mode, out_dtype, name, *, tn=None, tk=None, b_win=None, chunks=None, out_into=None):
    n_per, blk0 = chunks if chunks else (1, 0)
    if mode == "nn":
        m, k = a.shape
        n = b.shape[0] * n_per * tn if chunks else (b_win[1] if b_win else b.shape[1])
    elif mode == "nt":
        m = a.shape[0]
        k = b.shape[0] * n_per * tk if chunks else a.shape[1]
        n = b.shape[-2]
    else:
        k, m = a.shape
        n = b.shape[1]
    in_place = jnp.dtype(out_dtype) == jnp.dtype(F32)
    tm, tn, tk = _mm_tiles(m, n, k, a.dtype.itemsize, b.dtype.itemsize, jnp.dtype(out_dtype).itemsize, tn, tk)
    assert m % tm == 0 and n % tn == 0 and k % tk == 0, (name, m, n, k, tm, tn, tk)
    j0 = 0
    if b_win:
        assert b_win[0] % tn == 0
        j0 = b_win[0] // tn
    nk = k // tk
    dims = {"nn": NN, "nt": NT, "tn": TN}[mode]

    def kern(a_ref, b_ref, *rest):
        o_ref, acc_ref = (rest[-1], rest[-1]) if in_place else (rest[-2], rest[-1])
        kk = pl.program_id(2)

        @pl.when(kk == 0)
        def _():
            acc_ref[...] = jnp.zeros_like(acc_ref)

        acc_ref[...] += lax.dot_general(a_ref[...].astype(BF16), b_ref[...].astype(BF16), dims,
                                        preferred_element_type=F32)

        if not in_place:
            @pl.when(kk == nk - 1)
            def _():
                o_ref[...] = acc_ref[...].astype(o_ref.dtype)

    out_spec = pl.BlockSpec((tm, tn), lambda i, j, kk: (i, j))
    out_shape = jax.ShapeDtypeStruct((m, n), out_dtype)
    if mode == "nn":
        a_spec = pl.BlockSpec((tm, tk), lambda i, j, kk: (i, kk))
        if chunks:
            b_spec = pl.BlockSpec((None, tk, tn), lambda i, j, kk: (j // n_per, kk, blk0 + j % n_per))
        else:
            b_spec = pl.BlockSpec((tk, tn), lambda i, j, kk: (kk, j + j0))
    elif mode == "nt":
        a_spec = pl.BlockSpec((tm, tk), lambda i, j, kk: (i, kk))
        if chunks:
            b_spec = pl.BlockSpec((None, tn, tk), lambda i, j, kk: (kk // n_per, j, blk0 + kk % n_per))
        else:
            b_spec = pl.BlockSpec((tn, tk), lambda i, j, kk: (j, kk))
    else:
        a_spec = pl.BlockSpec((tk, tm), lambda i, j, kk: (kk, i))
        b_spec = pl.BlockSpec((tk, tn), lambda i, j, kk: (kk, j))
        if chunks:
            out_spec = pl.BlockSpec((None, tm, tn), lambda i, j, kk: (j // n_per, i, blk0 + j % n_per))
    in_specs, operands, aliases = [a_spec, b_spec], [a, b], {}
    if chunks and mode == "tn":
        assert out_into is not None
        out_shape = jax.ShapeDtypeStruct(out_into.shape, out_dtype)
        in_specs.append(pl.BlockSpec(memory_space=pl.ANY))
        operands.append(out_into)
        aliases = {2: 0}
    return pl.pallas_call(
        kern, name=name, grid=(m // tm, n // tn, nk),
        in_specs=in_specs, out_specs=out_spec, out_shape=out_shape,
        scratch_shapes=[] if in_place else [pltpu.VMEM((tm, tn), F32)], input_output_aliases=aliases,
        compiler_params=_params("parallel", "parallel", "arbitrary"),
    )(*operands)


def _rstd(v):
    return lax.rsqrt(jnp.mean(v * v, axis=-1, keepdims=True) + RMS_EPS)


def _norm_bwd(v, g, dy):
    r = _rstd(v)
    vh = v * r
    dyg = dy * g
    dv = r * (dyg - vh * jnp.mean(dyg * vh, axis=-1, keepdims=True))
    return dv, jnp.sum(dy * vh, axis=0, keepdims=True)


def _row_call(kern, name, ins, outs, s, d):
    tr = _tile(s, ROW_TILE, 16)

    def spec(shape, is_row):
        if is_row:
            return pl.BlockSpec((tr, shape[1]), lambda i: (i, 0))
        return pl.BlockSpec(shape, lambda i: (0, 0))

    return pl.pallas_call(
        kern, name=name, grid=(s // tr,),
        in_specs=[spec(a.shape, r) for a, r in ins],
        out_specs=[spec(sh, r) for sh, _, r in outs],
        out_shape=[jax.ShapeDtypeStruct(sh, dt) for sh, dt, _ in outs],
        compiler_params=_params("arbitrary"),
    )(*[a for a, _ in ins])


def norm_in(x, g):
    s, d = x.shape

    def kern(x_ref, g_ref, u_ref):
        v = x_ref[...]
        u_ref[...] = (v * _rstd(v) * g_ref[...]).astype(BF16)

    return _row_call(kern, "norm_in", [(x, True), (g, False)], [((s, d), BF16, True)], s, d)[0]


def mid_fwd(x, mix, g_post, g_fpre):
    s, d = x.shape

    def kern(x_ref, mix_ref, gp_ref, gf_ref, h1_ref, u2_ref):
        mixv = mix_ref[...]
        h1 = x_ref[...] + mixv * _rstd(mixv) * gp_ref[...]
        h1_ref[...] = h1
        u2_ref[...] = (h1 * _rstd(h1) * gf_ref[...]).astype(BF16)

    return _row_call(kern, "mid_fwd", [(x, True), (mix, True), (g_post, False), (g_fpre, False)],
                     [((s, d), F32, True), ((s, d), BF16, True)], s, d)


def loss_head(h1, ff, g_fpost, target):
    s, d = h1.shape

    def kern(h1_ref, ff_ref, g_ref, t_ref, dy_ref, dff_ref, dg_ref, loss_ref):
        @pl.when(pl.program_id(0) == 0)
        def _():
            dg_ref[...] = jnp.zeros_like(dg_ref)
            loss_ref[...] = jnp.zeros_like(loss_ref)

        ffv = ff_ref[...]
        g = g_ref[...]
        y = h1_ref[...] + ffv * _rstd(ffv) * g
        diff = y - t_ref[...]
        row_loss = jnp.mean(diff * diff, axis=-1, keepdims=True)
        loss_ref[...] += 0.5 * jnp.sum(row_loss, axis=0, keepdims=True)
        dy = diff / d
        dy_ref[...] = dy
        dff, dg = _norm_bwd(ffv, g, dy)
        dff_ref[...] = dff.astype(BF16)
        dg_ref[...] += dg

    return _row_call(kern, "loss_head",
                     [(h1, True), (ff, True), (g_fpost, False), (target, True)],
                     [((s, d), F32, True), ((s, d), BF16, True), ((1, d), F32, False), ((1, 1), F32, False)], s, d)


def mid_bwd(dy, du2, h1, mix, g_fpre, g_post):
    s, d = dy.shape

    def kern(dy_ref, du2_ref, h1_ref, mix_ref, gf_ref, gp_ref, dh1_ref, dmix_ref, dgf_ref, dgp_ref):
        @pl.when(pl.program_id(0) == 0)
        def _():
            dgf_ref[...] = jnp.zeros_like(dgf_ref)
            dgp_ref[...] = jnp.zeros_like(dgp_ref)

        dh, dgf = _norm_bwd(h1_ref[...], gf_ref[...], du2_ref[...])
        dh1 = dy_ref[...] + dh
        dh1_ref[...] = dh1
        dmix, dgp = _norm_bwd(mix_ref[...], gp_ref[...], dh1)
        dmix_ref[...] = dmix.astype(BF16)
        dgf_ref[...] += dgf
        dgp_ref[...] += dgp

    return _row_call(kern, "mid_bwd",
                     [(dy, True), (du2, True), (h1, True), (mix, True), (g_fpre, False), (g_post, False)],
                     [((s, d), F32, True), ((s, d), BF16, True), ((1, d), F32, False), ((1, d), F32, False)], s, d)


def in_bwd(dh1, du, x, g_pre):
    s, d = x.shape

    def kern(dh1_ref, du_ref, x_ref, g_ref, dx_ref, dg_ref):
        @pl.when(pl.program_id(0) == 0)
        def _():
            dg_ref[...] = jnp.zeros_like(dg_ref)

        dxn, dg = _norm_bwd(x_ref[...], g_ref[...], du_ref[...])
        dx_ref[...] = dh1_ref[...] + dxn
        dg_ref[...] += dg

    return _row_call(kern, "in_bwd", [(dh1, True), (du, True), (x, True), (g_pre, False)],
                     [((s, d), F32, True), ((1, d), F32, False)], s, d)


def _sigmoid(v):
    return 1.0 / (1.0 + jnp.exp(-v))


def gate_fwd(bsb, bfx, gf):
    s, d = bsb.shape
    tr, tc = _tile(s, 256, 16), _tile(d, 512, LANES)
    nc = d // tc

    def kern(bsb_ref, bfx_ref, gs_ref, gx_ref, o_ref):
        o_ref[...] = (_sigmoid(gs_ref[...]) * bsb_ref[...] + _sigmoid(gx_ref[...]) * bfx_ref[...]).astype(BF16)

    blk = pl.BlockSpec((tr, tc), lambda i, j: (i, j))
    return pl.pallas_call(
        kern, name="gate_fwd", grid=(s // tr, nc),
        in_specs=[blk, blk, blk, pl.BlockSpec((tr, tc), lambda i, j: (i, j + nc))],
        out_specs=blk, out_shape=jax.ShapeDtypeStruct((s, d), BF16),
        compiler_params=_params("parallel", "parallel"),
    )(bsb, bfx, gf, gf)


def gate_bwd(dmerged, bsb, bfx, gf):
    s, d = bsb.shape
    tr, tc = _tile(s, 256, 16), _tile(d, 512, LANES)
    nc = d // tc

    def kern(dm_ref, bsb_ref, bfx_ref, gs_ref, gx_ref, dbs_ref, dbx_ref, dgs_ref, dgx_ref):
        dm = dm_ref[...]
        ss = _sigmoid(gs_ref[...])
        sx = _sigmoid(gx_ref[...])
        dbs_ref[...] = (dm * ss).astype(BF16)
        dbx_ref[...] = (dm * sx).astype(BF16)
        dgs_ref[...] = (dm * bsb_ref[...] * ss * (1.0 - ss)).astype(BF16)
        dgx_ref[...] = (dm * bfx_ref[...] * sx * (1.0 - sx)).astype(BF16)

    blk = pl.BlockSpec((tr, tc), lambda i, j: (i, j))
    out = jax.ShapeDtypeStruct((s, d), BF16)
    return pl.pallas_call(
        kern, name="gate_bwd", grid=(s // tr, nc),
        in_specs=[blk, blk, blk, blk, pl.BlockSpec((tr, tc), lambda i, j: (i, j + nc))],
        out_specs=[blk, blk, blk, blk], out_shape=[out, out, out, out],
        compiler_params=_params("parallel", "parallel"),
    )(dmerged, bsb, bfx, gf, gf)


def swiglu_fwd(gu, cw):
    s, f2 = gu.shape
    tr = _tile(s, 256, 16)

    def kern(gu_ref, o_ref):
        g = gu_ref[:, :cw]
        o_ref[...] = (g * _sigmoid(g) * gu_ref[:, cw:]).astype(BF16)

    return pl.pallas_call(
        kern, name="swiglu_fwd", grid=(s // tr, f2 // (2 * cw)),
        in_specs=[pl.BlockSpec((tr, 2 * cw), lambda i, j: (i, j))],
        out_specs=pl.BlockSpec((tr, cw), lambda i, j: (i, j)),
        out_shape=jax.ShapeDtypeStruct((s, f2 // 2), BF16),
        compiler_params=_params("parallel", "parallel"),
    )(gu)


def swiglu_bwd(dact, gu, cw):
    s, f2 = gu.shape
    tr = _tile(s, 256, 16)

    def kern(da_ref, gu_ref, o_ref):
        da = da_ref[...]
        g = gu_ref[:, :cw]
        sg = _sigmoid(g)
        o_ref[:, :cw] = (da * gu_ref[:, cw:] * (sg * (1.0 + g * (1.0 - sg)))).astype(BF16)
        o_ref[:, cw:] = (da * (g * sg)).astype(BF16)

    return pl.pallas_call(
        kern, name="swiglu_bwd", grid=(s // tr, f2 // (2 * cw)),
        in_specs=[pl.BlockSpec((tr, cw), lambda i, j: (i, j)), pl.BlockSpec((tr, 2 * cw), lambda i, j: (i, j))],
        out_specs=pl.BlockSpec((tr, 2 * cw), lambda i, j: (i, j)),
        out_shape=jax.ShapeDtypeStruct((s, f2), BF16),
        compiler_params=_params("parallel", "parallel"),
    )(dact, gu)


def _split3(v):
    hi = v.astype(BF16)
    r = v - hi.astype(F32)
    mid = r.astype(BF16)
    lo = (r - mid.astype(F32)).astype(BF16)
    return hi, mid, lo


def _dot3_right(v, ones):
    hi, mid, lo = _split3(v)
    d = lambda p: jnp.dot(p, ones, preferred_element_type=F32)
    return (d(lo) + d(mid)) + d(hi)


def _dot3_left(ones, v):
    hi, mid, lo = _split3(v)
    d = lambda p: jnp.dot(ones, p, preferred_element_type=F32)
    return (d(lo) + d(mid)) + d(hi)


def _split2(v):
    hi = v.astype(BF16)
    return hi, (v - hi.astype(F32)).astype(BF16)


def _dot2_right(v, ones):
    hi, lo = _split2(v)
    return jnp.dot(lo, ones, preferred_element_type=F32) + jnp.dot(hi, ones, preferred_element_type=F32)


def _log1p_exp_neg_abs(v):
    return jnp.log(1.0 + jnp.exp(-jnp.abs(v)))


def _mask01(cond):
    return jnp.where(cond, 1.0, 0.0).astype(BF16)


def _iota2(t):
    return (lax.broadcasted_iota(jnp.int32, (t, t), 0), lax.broadcasted_iota(jnp.int32, (t, t), 1))


def cum_fwd(gf, b_pad, f_col0):
    s = gf.shape[0]
    t = _tile(s, ATT_TILE, LANES)
    fb = f_col0 // LANES

    def kern(f_ref, b_ref, cum_ref, carry_ref):
        @pl.when(pl.program_id(0) == 0)
        def _():
            carry_ref[...] = jnp.zeros_like(carry_ref)

        v = f_ref[...] + b_ref[...]
        lf = jnp.minimum(v, 0.0) - _log1p_exp_neg_abs(v)
        row, col = _iota2(t)
        cum = _dot3_left(_mask01(col <= row), lf) + carry_ref[...]
        cum_ref[...] = cum
        carry_ref[...] = cum[t - 1:t, :]

    return pl.pallas_call(
        kern, name="cum_fwd", grid=(s // t,),
        in_specs=[pl.BlockSpec((t, LANES), lambda i: (i, fb)), pl.BlockSpec((1, LANES), lambda i: (0, 0))],
        out_specs=pl.BlockSpec((t, LANES), lambda i: (i, 0)),
        out_shape=jax.ShapeDtypeStruct((s, LANES), F32),
        scratch_shapes=[pltpu.VMEM((1, LANES), F32)],
        compiler_params=_params("arbitrary"),
    )(gf, b_pad)


def cum_bwd(dcum, gf, b_pad, f_col0, n_heads):
    s = gf.shape[0]
    t = _tile(s, ATT_TILE, LANES)
    nb = s // t
    fb = f_col0 // LANES

    def kern(dc_ref, f_ref, b_ref, df_ref, db_ref, carry_ref):
        @pl.when(pl.program_id(0) == 0)
        def _():
            carry_ref[...] = jnp.zeros_like(carry_ref)
            db_ref[...] = jnp.zeros_like(db_ref)

        row, col = _iota2(t)
        dlf = _dot3_left(_mask01(col >= row), dc_ref[...]) + carry_ref[...]
        carry_ref[...] = dlf[0:1, :]
        v = f_ref[...] + b_ref[...]
        sig_neg = jnp.exp(-jnp.maximum(v, 0.0) - _log1p_exp_neg_abs(v))
        lane = lax.broadcasted_iota(jnp.int32, (t, LANES), 1)
        df = jnp.where(lane < n_heads, dlf * sig_neg, 0.0)
        df_ref[...] = df.astype(BF16)
        db_ref[...] += jnp.sum(df, axis=0, keepdims=True)

    return pl.pallas_call(
        kern, name="cum_bwd", grid=(nb,),
        in_specs=[pl.BlockSpec((t, LANES), lambda i: (nb - 1 - i, 0)),
                  pl.BlockSpec((t, LANES), lambda i: (nb - 1 - i, fb)),
                  pl.BlockSpec((1, LANES), lambda i: (0, 0))],
        out_specs=[pl.BlockSpec((t, LANES), lambda i: (nb - 1 - i, 0)), pl.BlockSpec((1, LANES), lambda i: (0, 0))],
        out_shape=[jax.ShapeDtypeStruct((s, LANES), BF16), jax.ShapeDtypeStruct((1, LANES), F32)],
        scratch_shapes=[pltpu.VMEM((1, LANES), F32)],
        compiler_params=_params("arbitrary"),
    )(dcum, gf, b_pad)


def _qkv_specs(s, t, n_heads, base):
    return [pl.BlockSpec((t, HEAD_DIM), lambda h, i: (i, base + h)),
            pl.BlockSpec((s, HEAD_DIM), lambda h, i: (0, base + n_heads + h)),
            pl.BlockSpec((s, HEAD_DIM), lambda h, i: (0, base + 2 * n_heads + h))]


def _strips(t):
    sr = _tile(t, ATT_STRIP, 8)
    return sr, t // sr, [slice(si * sr, (si + 1) * sr) for si in range(t // sr)]


def _key_minus_row(sr, t):
    return lax.broadcasted_iota(jnp.int32, (sr, t), 1) - lax.broadcasted_iota(jnp.int32, (sr, t), 0)


def _sb_scores(q, k, diff, lim):
    z = lax.dot_general(q, k, NT, preferred_element_type=F32) * (HEAD_DIM ** -0.5)
    valid = diff < lim
    l1p = _log1p_exp_neg_abs(z)
    log_keep = jnp.where(valid, -jnp.maximum(z, 0.0) - l1p, 0.0)
    return z, valid, l1p, log_keep


def sb_fwd(qkv, n_heads, base):
    s = qkv.shape[0]
    t = _tile(s, ATT_TILE, LANES)
    sr, ns, strips = _strips(t)

    def kern(q_ref, k_ref, v_ref, o_ref):
        i = pl.program_id(1)
        row, col = _iota2(t)
        after = _mask01(row > col)
        diff = _key_minus_row(sr, t)
        qs = [q_ref[sl, :] for sl in strips]

        def body(jj, carry):
            runs, accs = carry
            j = i - jj
            off = pl.multiple_of(j * t, t)
            k = k_ref[pl.ds(off, t), :]
            v = v_ref[pl.ds(off, t), :]
            lim = jnp.where(j < i, t, 0)
            new_runs, new_accs = [], []
            for si in range(ns):
                z, valid, l1p, log_keep = _sb_scores(qs[si], k, diff, lim + si * sr)
                between = _dot2_right(log_keep, after) + runs[si]
                w = jnp.where(valid, jnp.exp(jnp.minimum(z, 0.0) - l1p + between), 0.0)
                new_accs.append(accs[si] + jnp.dot(w.astype(BF16), v, preferred_element_type=F32))
                new_runs.append(runs[si] + jnp.sum(log_keep, axis=1, keepdims=True))
            return tuple(new_runs), tuple(new_accs)

        init = (tuple(jnp.zeros((sr, 1), F32) for _ in strips), tuple(jnp.zeros((sr, HEAD_DIM), F32) for _ in strips))
        _, accs = lax.fori_loop(0, i + 1, body, init)
        for sl, acc in zip(strips, accs):
            o_ref[sl, :] = acc.astype(o_ref.dtype)

    return pl.pallas_call(
        kern, name="sb_fwd", grid=(n_heads, s // t),
        in_specs=_qkv_specs(s, t, n_heads, base),
        out_specs=pl.BlockSpec((t, HEAD_DIM), lambda h, i: (i, h)),
        out_shape=jax.ShapeDtypeStruct((s, n_heads * HEAD_DIM), BF16),
        compiler_params=_params("parallel", "arbitrary"),
    )(qkv, qkv, qkv)


def sb_bwd(qkv, d_o, n_heads, base):
    s = qkv.shape[0]
    t = _tile(s, ATT_TILE, LANES)
    nq = s // t
    sr, ns, strips = _strips(t)
    scale = HEAD_DIM ** -0.5

    def kern(q_ref, k_ref, v_ref, do_ref, dq_ref, dk_ref, dv_ref, dk_acc, dv_acc, run_ref):
        i = pl.program_id(1)

        @pl.when(i == 0)
        def _():
            dk_acc[...] = jnp.zeros_like(dk_acc)
            dv_acc[...] = jnp.zeros_like(dv_acc)

        row, col = _iota2(t)
        after = _mask01(row > col)
        before = _mask01(row < col)
        diff = _key_minus_row(sr, t)
        qs = [q_ref[sl, :] for sl in strips]
        dos = [do_ref[sl, :] for sl in strips]

        def sweep1(jj, runs):
            j = i - jj
            k = k_ref[pl.ds(pl.multiple_of(j * t, t), t), :]
            lim = jnp.where(j < i, t, 0)
            new_runs = []
            for si, sl in enumerate(strips):
                _, _, _, log_keep = _sb_scores(qs[si], k, diff, lim + si * sr)
                run_ref[j, sl, :] = runs[si]
                new_runs.append(runs[si] + jnp.sum(log_keep, axis=1, keepdims=True))
            return tuple(new_runs)

        lax.fori_loop(0, i + 1, sweep1, tuple(jnp.zeros((sr, 1), F32) for _ in strips))

        def sweep2(j, carry):
            run_es, dqs = carry
            off = pl.multiple_of(j * t, t)
            k = k_ref[pl.ds(off, t), :]
            v = v_ref[pl.ds(off, t), :]
            lim = jnp.where(j < i, t, 0)
            new_es, new_dqs = [], []
            dk_t = jnp.zeros((t, HEAD_DIM), F32)
            dv_t = jnp.zeros((t, HEAD_DIM), F32)
            for si, sl in enumerate(strips):
                z, valid, l1p, log_keep = _sb_scores(qs[si], k, diff, lim + si * sr)
                between = _dot2_right(log_keep, after) + run_ref[j, sl, :]
                w = jnp.where(valid, jnp.exp(jnp.minimum(z, 0.0) - l1p + between), 0.0)
                dw = lax.dot_general(dos[si], v, NT, preferred_element_type=F32)
                e = dw * w
                e_before = _dot2_right(e, before) + run_es[si]
                keep = jnp.exp(log_keep)
                dz = jnp.where(valid, e * keep - e_before * (1.0 - keep), 0.0) * scale
                dzb = dz.astype(BF16)
                new_dqs.append(dqs[si] + jnp.dot(dzb, k, preferred_element_type=F32))
                dk_t = dk_t + lax.dot_general(dzb, qs[si], TN, preferred_element_type=F32)
                dv_t = dv_t + lax.dot_general(w.astype(BF16), dos[si], TN, preferred_element_type=F32)
                new_es.append(run_es[si] + jnp.sum(e, axis=1, keepdims=True))
            dk_acc[pl.ds(off, t), :] += dk_t
            dv_acc[pl.ds(off, t), :] += dv_t
            return tuple(new_es), tuple(new_dqs)

        init = (tuple(jnp.zeros((sr, 1), F32) for _ in strips), tuple(jnp.zeros((sr, HEAD_DIM), F32) for _ in strips))
        _, dqs = lax.fori_loop(0, i + 1, sweep2, init)
        for sl, dq in zip(strips, dqs):
            dq_ref[sl, :] = dq.astype(BF16)

        @pl.when(i == nq - 1)
        def _():
            dk_ref[...] = dk_acc[...].astype(BF16)
            dv_ref[...] = dv_acc[...].astype(BF16)

    out = jax.ShapeDtypeStruct((s, n_heads * HEAD_DIM), BF16)
    head_blk = pl.BlockSpec((s, HEAD_DIM), lambda h, i: (0, h))
    tile_blk = pl.BlockSpec((t, HEAD_DIM), lambda h, i: (i, h))
    return pl.pallas_call(
        kern, name="sb_bwd", grid=(n_heads, nq),
        in_specs=_qkv_specs(s, t, n_heads, base) + [tile_blk],
        out_specs=[tile_blk, head_blk, head_blk],
        out_shape=[out, out, out],
        scratch_shapes=[pltpu.VMEM((s, HEAD_DIM), F32), pltpu.VMEM((s, HEAD_DIM), F32), pltpu.VMEM((nq, t, 1), F32)],
        compiler_params=_params("parallel", "arbitrary"),
    )(qkv, qkv, qkv, d_o)


def _fox_scores(q, k, cq, ck, diff, lim):
    sc = lax.dot_general(q, k, NT, preferred_element_type=F32) * (HEAD_DIM ** -0.5)
    sc = sc + cq - ck
    valid = diff < lim
    return jnp.where(valid, sc, NEG_BIG), valid


def fox_fwd(qkv, cum_col, cum_row, n_heads, base):
    s = qkv.shape[0]
    t = _tile(s, ATT_TILE, LANES)
    sr, ns, strips = _strips(t)

    def kern(q_ref, k_ref, v_ref, cq_ref, ck_ref, o_ref, lse_ref):
        i = pl.program_id(1)
        diff = _key_minus_row(sr, t)
        qs = [q_ref[sl, :] for sl in strips]
        cqs = [cq_ref[0, sl, :] for sl in strips]

        def body(j, carry):
            off = pl.multiple_of(j * t, t)
            k = k_ref[pl.ds(off, t), :]
            v = v_ref[pl.ds(off, t), :]
            ck = ck_ref[0, :, pl.ds(off, t)]
            lim = jnp.where(j < i, t, 1)
            out = []
            for si in range(ns):
                m, l, acc = carry[si]
                sc, _ = _fox_scores(qs[si], k, cqs[si], ck, diff, lim + si * sr)
                m_new = jnp.maximum(m, jnp.max(sc, axis=1, keepdims=True))
                p = jnp.exp(sc - m_new)
                alpha = jnp.exp(m - m_new)
                l = alpha * l + jnp.sum(p, axis=1, keepdims=True)
                acc = alpha * acc + jnp.dot(p.astype(BF16), v, preferred_element_type=F32)
                out.append((m_new, l, acc))
            return tuple(out)

        init = tuple((jnp.full((sr, 1), NEG_BIG, F32), jnp.zeros((sr, 1), F32), jnp.zeros((sr, HEAD_DIM), F32))
                     for _ in strips)
        res = lax.fori_loop(0, i + 1, body, init)
        for sl, (m, l, acc) in zip(strips, res):
            o_ref[sl, :] = acc / l
            lse_ref[0, sl, :] = m + jnp.log(l)

    col_blk = pl.BlockSpec((1, t, 1), lambda h, i: (h, i, 0))
    return pl.pallas_call(
        kern, name="fox_fwd", grid=(n_heads, s // t),
        in_specs=_qkv_specs(s, t, n_heads, base) + [col_blk, pl.BlockSpec((1, 1, s), lambda h, i: (h, 0, 0))],
        out_specs=[pl.BlockSpec((t, HEAD_DIM), lambda h, i: (i, h)), col_blk],
        out_shape=[jax.ShapeDtypeStruct((s, n_heads * HEAD_DIM), F32), jax.ShapeDtypeStruct((n_heads, s, 1), F32)],
        compiler_params=_params("parallel", "arbitrary"),
    )(qkv, qkv, qkv, cum_col, cum_row)


def fox_bwd(qkv, cum_col, cum_row, o, d_o, lse, n_heads, base):
    s = qkv.shape[0]
    t = _tile(s, ATT_TILE, LANES)
    nq = s // t
    sr, ns, strips = _strips(t)
    scale = HEAD_DIM ** -0.5

    def kern(q_ref, k_ref, v_ref, cq_ref, ck_ref, o_ref, do_ref, lse_ref,
             dq_ref, dk_ref, dv_ref, dcq_ref, dck_ref, dk_acc, dv_acc, dck_acc):
        i = pl.program_id(1)

        @pl.when(i == 0)
        def _():
            dk_acc[...] = jnp.zeros_like(dk_acc)
            dv_acc[...] = jnp.zeros_like(dv_acc)
            dck_acc[...] = jnp.zeros_like(dck_acc)

        diff = _key_minus_row(sr, t)
        qs = [q_ref[sl, :] for sl in strips]
        dos = [do_ref[sl, :] for sl in strips]
        cqs = [cq_ref[0, sl, :] for sl in strips]
        lses = [lse_ref[0, sl, :] for sl in strips]
        deltas = [jnp.sum(dos[si].astype(F32) * o_ref[sl, :], axis=1, keepdims=True) for si, sl in enumerate(strips)]

        def body(j, carry):
            off = pl.multiple_of(j * t, t)
            k = k_ref[pl.ds(off, t), :]
            v = v_ref[pl.ds(off, t), :]
            ck = ck_ref[0, :, pl.ds(off, t)]
            lim = jnp.where(j < i, t, 1)
            out = []
            dk_t = jnp.zeros((t, HEAD_DIM), F32)
            dv_t = jnp.zeros((t, HEAD_DIM), F32)
            dck_t = jnp.zeros((1, t), F32)
            for si in range(ns):
                dq, dcq = carry[si]
                sc, valid = _fox_scores(qs[si], k, cqs[si], ck, diff, lim + si * sr)
                p = jnp.where(valid, jnp.exp(sc - lses[si]), 0.0)
                dp = lax.dot_general(dos[si], v, NT, preferred_element_type=F32)
                ds = p * (dp - deltas[si])
                dsb = (ds * scale).astype(BF16)
                dq = dq + jnp.dot(dsb, k, preferred_element_type=F32)
                dk_t = dk_t + lax.dot_general(dsb, qs[si], TN, preferred_element_type=F32)
                dv_t = dv_t + lax.dot_general(p.astype(BF16), dos[si], TN, preferred_element_type=F32)
                dck_t = dck_t + jnp.sum(ds, axis=0, keepdims=True)
                out.append((dq, dcq + jnp.sum(ds, axis=1, keepdims=True)))
            dk_acc[pl.ds(off, t), :] += dk_t
            dv_acc[pl.ds(off, t), :] += dv_t
            dck_acc[:, pl.ds(off, t)] -= dck_t
            return tuple(out)

        init = tuple((jnp.zeros((sr, HEAD_DIM), F32), jnp.zeros((sr, 1), F32)) for _ in strips)
        res = lax.fori_loop(0, i + 1, body, init)
        for sl, (dq, dcq) in zip(strips, res):
            dq_ref[sl, :] = dq.astype(BF16)
            dcq_ref[0, sl, :] = dcq

        @pl.when(i == nq - 1)
        def _():
            dk_ref[...] = dk_acc[...].astype(BF16)
            dv_ref[...] = dv_acc[...].astype(BF16)
            dck_ref[0] = dck_acc[...]

    out = jax.ShapeDtypeStruct((s, n_heads * HEAD_DIM), BF16)
    head_blk = pl.BlockSpec((s, HEAD_DIM), lambda h, i: (0, h))
    tile_blk = pl.BlockSpec((t, HEAD_DIM), lambda h, i: (i, h))
    col_blk = pl.BlockSpec((1, t, 1), lambda h, i: (h, i, 0))
    row_blk = pl.BlockSpec((1, 1, s), lambda h, i: (h, 0, 0))
    return pl.pallas_call(
        kern, name="fox_bwd", grid=(n_heads, nq),
        in_specs=_qkv_specs(s, t, n_heads, base) + [col_blk, row_blk, tile_blk, tile_blk, col_blk],
        out_specs=[tile_blk, head_blk, head_blk, col_blk, row_blk],
        out_shape=[out, out, out, jax.ShapeDtypeStruct((n_heads, s, 1), F32),
                   jax.ShapeDtypeStruct((n_heads, 1, s), F32)],
        scratch_shapes=[pltpu.VMEM((s, HEAD_DIM), F32), pltpu.VMEM((s, HEAD_DIM), F32), pltpu.VMEM((1, s), F32)],
        compiler_params=_params("parallel", "arbitrary"),
    )(qkv, qkv, qkv, cum_col, cum_row, o, d_o, lse)


def _place():
    x, y, c = lax.axis_index("x"), lax.axis_index("y"), lax.axis_index("c")
    other_chips = [(1 - x, y), (x, 1 - y), (1 - x, 1 - y)]
    return x, y, c, other_chips


ANY = pl.BlockSpec(memory_space=pl.ANY)


def _remote(src, dst, send_sem, recv_sem, dev):
    return pltpu.make_async_remote_copy(src_ref=src, dst_ref=dst, send_sem=send_sem, recv_sem=recv_sem,
                                        device_id=dev, device_id_type=MESH)


def cast_place(name, ws, chip):
    r = ws[0].shape[0]
    cs = [w.shape[1] for w in ws]
    tr = _tile(r, 256, 16)

    def kern(chip_ref, *refs):
        o_ref = refs[-1]
        off = 0
        for w_ref, c in zip(refs[:-1], cs):
            o_ref[:, off:off + c] = w_ref[...].astype(BF16)
            off += c

    return pl.pallas_call(
        kern, name=name,
        grid_spec=pltpu.PrefetchScalarGridSpec(
            num_scalar_prefetch=1, grid=(r // tr,),
            in_specs=[pl.BlockSpec((tr, c), lambda i, chip_ref: (i, 0)) for c in cs],
            out_specs=pl.BlockSpec((None, tr, sum(cs)), lambda i, chip_ref: (chip_ref[0], i, 0))),
        out_shape=jax.ShapeDtypeStruct((N_CHIPS, r, sum(cs)), BF16),
        compiler_params=_params("parallel"),
    )(chip, *ws)


HBM = pl.BlockSpec(memory_space=pltpu.HBM)
SEM = pl.BlockSpec(memory_space=pltpu.SEMAPHORE)
SPLIT = pltpu.CompilerParams(has_side_effects=pltpu.SideEffectType.DATAFLOW_SIDE_EFFECTING)


def _in_hbm(a):
    return pltpu.with_memory_space_constraint(a, pltpu.HBM)


def _slab_rows(ref, k, core):
    half = ref.shape[1] // 2
    return ref.at[k, pl.ds(pl.multiple_of(core * half, 16), half)]


def gather_start(bufs, groups):
    n = len(bufs)
    ng = len(groups)

    def body(*refs):
        ins, sems = refs[:n], refs[n:n + 2 * ng]
        x, y, c, chips = _place()
        me = 2 * x + y
        for gi, grp in enumerate(groups):
            for ai, a in enumerate(grp):
                for j in range(3):
                    rows = _slab_rows(ins[a], me, c)
                    _remote(rows, rows, sems[2 * gi].at[3 * ai + j], sems[2 * gi + 1].at[3 * ai + j],
                            (chips[j][0], chips[j][1], c)).start()

    sem_shapes = []
    for grp in groups:
        sem_shapes += [pltpu.SemaphoreType.DMA((3 * len(grp),))] * 2
    res = pl.pallas_call(
        body, name="gather_start", in_specs=[HBM] * n, out_specs=[SEM] * (2 * ng) + [HBM] * n,
        out_shape=sem_shapes + [pltpu.HBM(b.shape, b.dtype) for b in bufs],
        input_output_aliases={a: 2 * ng + a for a in range(n)}, compiler_params=SPLIT,
    )(*[_in_hbm(b) for b in bufs])
    return res[:2 * ng], res[2 * ng:]


def gather_wait(name, bufs, send_sems, recv_sems, after):
    n = len(bufs)

    def body(*refs):
        ins, send, recv = refs[:n], refs[n], refs[n + 1]
        x, y, c, chips = _place()
        me = 2 * x + y
        for a in range(n):
            for j in range(3):
                dev = (chips[j][0], chips[j][1], c)
                mine = _slab_rows(ins[a], me, c)
                _remote(mine, mine, send.at[3 * a + j], recv.at[3 * a + j], dev).wait_send()
                land = _slab_rows(ins[a], 2 * chips[j][0] + chips[j][1], c)
                _remote(land, land, send.at[3 * a + j], recv.at[3 * a + j], dev).wait_recv()

    return pl.pallas_call(
        body, name=name, in_specs=[HBM] * n + [SEM, SEM, ANY], out_specs=[HBM] * n,
        out_shape=[pltpu.HBM(b.shape, b.dtype) for b in bufs],
        input_output_aliases={a: a for a in range(n)}, compiler_params=SPLIT,
    )(*bufs, send_sems, recv_sems, after)


def gather_forward(name, bufs):
    n = len(bufs)

    def body(*refs):
        outs = refs[n:2 * n]
        send_sems, recv_sems = refs[2 * n:]
        x, y, c, chips = _place()
        sibling = (x, y, 1 - c)

        def d2d(a, j, core):
            rows = _slab_rows(outs[a], 2 * chips[j][0] + chips[j][1], core)
            return _remote(rows, rows, send_sems.at[3 * a + j], recv_sems.at[3 * a + j], sibling)

        pairs = [(a, j) for a in range(n) for j in range(3)]
        for a, j in pairs:
            d2d(a, j, c).start()
        for a, j in pairs:
            d2d(a, j, 1 - c).wait_recv()
        for a, j in pairs:
            d2d(a, j, c).wait_send()

    return pl.pallas_call(
        body, name=name, in_specs=[ANY] * n, out_specs=[ANY] * n,
        out_shape=[jax.ShapeDtypeStruct(b.shape, b.dtype) for b in bufs],
        input_output_aliases={a: a for a in range(n)},
        scratch_shapes=[pltpu.SemaphoreType.DMA((3 * n,)), pltpu.SemaphoreType.DMA((3 * n,))],
    )(*bufs)


def swap_halves(name, pieces):
    n = len(pieces)
    halves = [p.shape[1] // 2 for p in pieces]

    def body(*refs):
        ins, outs = refs[:n], refs[n:2 * n]
        send_sems, recv_sems = refs[2 * n:]
        x, y, c, _ = _place()
        cps = [_remote(ins[a].at[:, pl.ds(pl.multiple_of((1 - c) * halves[a], 16), halves[a]), :], outs[a],
                       send_sems.at[a], recv_sems.at[a], (x, y, 1 - c)) for a in range(n)]
        for cp in cps:
            cp.start()
        for cp in cps:
            cp.wait()

    return pl.pallas_call(
        body, name=name, in_specs=[ANY] * n, out_specs=[ANY] * n,
        out_shape=[jax.ShapeDtypeStruct((N_CHIPS, h, p.shape[2]), p.dtype) for p, h in zip(pieces, halves)],
        scratch_shapes=[pltpu.SemaphoreType.DMA((n,)), pltpu.SemaphoreType.DMA((n,))],
    )(*pieces)


def pair_sum(name, pieces, got, core):
    _, r, w = pieces.shape
    half = r // 2
    tr = _tile(half, 256, 16)

    def kern(core_ref, p_ref, g_ref, o_ref):
        o_ref[...] = (p_ref[...].astype(F32) + g_ref[...].astype(F32)).astype(o_ref.dtype)

    return pl.pallas_call(
        kern, name=name,
        grid_spec=pltpu.PrefetchScalarGridSpec(
            num_scalar_prefetch=1, grid=(N_CHIPS, half // tr),
            in_specs=[pl.BlockSpec((None, None, tr, w), lambda k, i, core_ref: (k, core_ref[0], i, 0)),
                      pl.BlockSpec((None, tr, w), lambda k, i, core_ref: (k, i, 0))],
            out_specs=pl.BlockSpec((None, tr, w), lambda k, i, core_ref: (k, i, 0))),
        out_shape=jax.ShapeDtypeStruct((N_CHIPS, half, w), pieces.dtype),
        compiler_params=_params("parallel", "parallel"),
    )(core, pieces.reshape(N_CHIPS, 2, half, w), got)


def _scatter_copies(sums, lands, send, recv):
    x, y, c, chips = _place()
    return [_remote(sums[a].at[2 * chips[j][0] + chips[j][1]], lands[a].at[j], send.at[3 * a + j], recv.at[3 * a + j],
                    (chips[j][0], chips[j][1], c)) for a in range(len(sums)) for j in range(3)]


def scatter_start(name, sums):
    n = len(sums)
    lands = [lax.empty((3,) + t.shape[1:], t.dtype) for t in sums]

    def body(*refs):
        ins, land_in, send, recv, token = refs[:n], refs[n:2 * n], refs[2 * n], refs[2 * n + 1], refs[-1]
        for cp in _scatter_copies(ins, land_in, send, recv):
            cp.start()
        token[...] = jnp.zeros_like(token)

    sem = pltpu.SemaphoreType.DMA((3 * n,))
    res = pl.pallas_call(
        body, name=name, in_specs=[HBM] * (2 * n),
        out_specs=[SEM, SEM] + [HBM] * (2 * n) + [pl.BlockSpec(memory_space=pltpu.VMEM)],
        out_shape=[sem, sem] + [pltpu.HBM(t.shape, t.dtype) for t in sums + lands] + [jax.ShapeDtypeStruct((8, LANES), F32)],
        input_output_aliases={a: 2 + a for a in range(2 * n)}, compiler_params=SPLIT,
    )(*[_in_hbm(t) for t in sums + lands])
    return res[0], res[1], res[2:2 + n], res[2 + n:2 + 2 * n], res[-1]


def scatter_wait(name, sums, lands, send_sems, recv_sems, after):
    n = len(sums)

    def body(*refs):
        ins, land_in, send, recv = refs[:n], refs[n:2 * n], refs[2 * n], refs[2 * n + 1]
        for cp in _scatter_copies(ins, land_in, send, recv):
            cp.wait_send()
            cp.wait_recv()

    res = pl.pallas_call(
        body, name=name, in_specs=[HBM] * (2 * n) + [SEM, SEM, ANY], out_specs=[HBM] * (2 * n),
        out_shape=[pltpu.HBM(t.shape, t.dtype) for t in sums + lands],
        input_output_aliases={a: a for a in range(2 * n)}, compiler_params=SPLIT,
    )(*sums, *lands, send_sems, recv_sems, after)
    return res[:n], res[n:]


def chip_sum(name, sums, got, chip, core):
    _, half, w = sums.shape
    tr = _tile(half, 256, 16)
    nb = half // tr

    def kern(ids_ref, s_ref, g0_ref, g1_ref, g2_ref, o_ref):
        o_ref[...] = ((s_ref[...].astype(F32) + g0_ref[...].astype(F32)) + g1_ref[...].astype(F32)) \
            + g2_ref[...].astype(F32)

    def got_spec(j):
        return pl.BlockSpec((None, tr, w), lambda i, ids_ref: (j, i, 0))

    return pl.pallas_call(
        kern, name=name,
        grid_spec=pltpu.PrefetchScalarGridSpec(
            num_scalar_prefetch=1, grid=(nb,),
            in_specs=[pl.BlockSpec((None, tr, w), lambda i, ids_ref: (ids_ref[0], i, 0)),
                      got_spec(0), got_spec(1), got_spec(2)],
            out_specs=pl.BlockSpec((tr, w), lambda i, ids_ref: (ids_ref[1] * nb + i, 0))),
        out_shape=jax.ShapeDtypeStruct((2 * half, w), F32),
        compiler_params=_params("parallel"),
    )(jnp.concatenate([chip, core]), sums, got, got, got)


def join_halves(name, shards):
    n = len(shards)
    halves = [g.shape[0] // 2 for g in shards]

    def body(*refs):
        outs = refs[n:2 * n]
        send_sems, recv_sems = refs[2 * n:]
        x, y, c, _ = _place()
        cps = []
        for a in range(n):
            rows = outs[a].at[pl.ds(pl.multiple_of(c * halves[a], 8), halves[a])]
            cps.append(_remote(rows, rows, send_sems.at[a], recv_sems.at[a], (x, y, 1 - c)))
        for cp in cps:
            cp.start()
        for cp in cps:
            cp.wait()

    return pl.pallas_call(
        body, name=name, in_specs=[ANY] * n, out_specs=[ANY] * n,
        out_shape=[jax.ShapeDtypeStruct(g.shape, g.dtype) for g in shards],
        input_output_aliases={a: a for a in range(n)},
        scratch_shapes=[pltpu.SemaphoreType.DMA((n,)), pltpu.SemaphoreType.DMA((n,))],
    )(*shards)


def _adam(w, g, m, v):
    m = ADAM_B1 * m + (1.0 - ADAM_B1) * g
    v = ADAM_B2 * v + (1.0 - ADAM_B2) * (g * g)
    m_hat = m / (1.0 - ADAM_B1 ** ADAM_STEP)
    v_hat = v / (1.0 - ADAM_B2 ** ADAM_STEP)
    delta = -ADAM_LR * (m_hat / (jnp.sqrt(v_hat) + ADAM_EPS) + ADAM_WD * w)
    return delta, m, v


def small_allreduce_adam(g_part, w, m, v):
    n_dev = 8
    r, d = g_part.shape

    def body(g_ref, w_ref, m_ref, v_ref, gs_ref, dl_ref, nm_ref, nv_ref, all_ref, send_sems, recv_sems):
        x, y, c, _ = _place()
        me = 4 * x + 2 * y + c
        all_ref[me] = g_ref[...]
        cps = []
        for rel in range(1, n_dev):
            px = 1 - x if rel & 4 else x
            py = 1 - y if rel & 2 else y
            pc = 1 - c if rel & 1 else c
            cps.append(_remote(g_ref, all_ref.at[me], send_sems.at[rel - 1], recv_sems.at[rel - 1], (px, py, pc)))
        for cp in cps:
            cp.start()
        for cp in cps:
            cp.wait()
        total = all_ref[0]
        for dev in range(1, n_dev):
            total = total + all_ref[dev]
        gs_ref[...] = total
        delta, nm, nv = _adam(w_ref[...], total, m_ref[...], v_ref[...])
        dl_ref[...] = delta
        nm_ref[...] = nm
        nv_ref[...] = nv

    vm = pl.BlockSpec(memory_space=pltpu.VMEM)
    out = jax.ShapeDtypeStruct((r, d), F32)
    return pl.pallas_call(
        body, name="small_allreduce_adam", in_specs=[vm, vm, vm, vm], out_specs=[vm, vm, vm, vm],
        out_shape=[out, out, out, out],
        scratch_shapes=[pltpu.VMEM((n_dev, r, d), F32), pltpu.SemaphoreType.DMA((n_dev - 1,)),
                        pltpu.SemaphoreType.DMA((n_dev - 1,))],
    )(g_part, w, m, v)


def adam_update(name, w, m, v, g_buf, col_blk, copy_g):
    r, c = w.shape
    tr = _tile(r, 128, 8)

    def kern(w_ref, m_ref, v_ref, g_ref, *outs):
        g = g_ref[...]
        delta, nm, nv = _adam(w_ref[...], g, m_ref[...], v_ref[...])
        if copy_g:
            outs[0][...] = g
        outs[-3][...] = delta
        outs[-2][...] = nm
        outs[-1][...] = nv

    blk = pl.BlockSpec((tr, c), lambda i: (i, 0))
    out = jax.ShapeDtypeStruct((r, c), F32)
    n_out = 4 if copy_g else 3
    return pl.pallas_call(
        kern, name=name, grid=(r // tr,),
        in_specs=[blk, blk, blk, pl.BlockSpec((tr, c), lambda i: (i, col_blk))],
        out_specs=[blk] * n_out, out_shape=[out] * n_out, compiler_params=_params("parallel"),
    )(w, m, v, g_buf)


def _col_pieces(slab_of, lo, hi, cw):
    out = []
    while lo < hi:
        k, a = divmod(lo, cw)
        b = min(cw, a + hi - lo)
        out.append(slab_of(k)[:, a:b])
        lo += b - a
    return out


def kernel(x, norm_mix_pre, norm_mix_post, w_in, b_forget, w_branch_sb, w_branch_fox, w_out, norm_ffn_pre, norm_ffn_post, w_ffn_gate, w_ffn_up, w_ffn_down, loss_target, m_norm_mix_pre, m_norm_mix_post, m_w_in, m_b_forget, m_w_branch_sb, m_w_branch_fox, m_w_out, m_norm_ffn_pre, m_norm_ffn_post, m_w_ffn_gate, m_w_ffn_up, m_w_ffn_down, v_norm_mix_pre, v_norm_mix_post, v_w_in, v_b_forget, v_w_branch_sb, v_w_branch_fox, v_w_out, v_norm_ffn_pre, v_norm_ffn_post, v_w_ffn_gate, v_w_ffn_up, v_w_ffn_down):
    s, d = x.shape[1], x.shape[2]
    n_heads = b_forget.shape[1]
    d_att = n_heads * HEAD_DIM
    c_in = w_in.shape[2]
    c_br = w_branch_sb.shape[2]
    c_gu = w_ffn_gate.shape[2]
    d_ff = c_gu * N_CHIPS
    d_in = c_in * N_CHIPS
    f_pad = 512
    n_qkv = 6 * d_att
    n_gf = 2 * d + f_pad
    core = lax.axis_index("c").astype(jnp.int32).reshape(1)
    chip = (2 * lax.axis_index("x") + lax.axis_index("y")).astype(jnp.int32).reshape(1)

    ag_sems, ag_bufs = gather_start([
        cast_place("place_w_in", [w_in[0]], chip),
        cast_place("place_branch", [w_branch_sb[0], w_branch_fox[0]], chip),
        cast_place("place_out", [w_out[0]], chip),
        cast_place("place_gate_up", [w_ffn_gate[0], w_ffn_up[0]], chip),
        cast_place("place_down", [w_ffn_down[0]], chip)], [[0], [1, 2, 3, 4]])
    g_in, = gather_forward("forward_w_in", gather_wait("gather_wait_w_in", ag_bufs[:1], ag_sems[0], ag_sems[1],
                                                       norm_mix_pre))
    slab = lambda k: g_in[k]
    w_main = jnp.concatenate(
        _col_pieces(slab, 0, n_qkv, c_in) + _col_pieces(slab, n_qkv + n_heads, d_in, c_in)
        + _col_pieces(slab, n_qkv, n_qkv + n_heads, c_in) + [jnp.zeros((d, f_pad - n_heads), BF16)], axis=1)
    x2 = x[0]
    tgt = loss_target[0]
    b_pad = jnp.pad(b_forget, ((0, 0), (0, LANES - n_heads)))

    u = norm_in(x2, norm_mix_pre)
    qkv = mm(u, w_main, "nn", BF16, "proj_qkv", b_win=(0, n_qkv))
    gf = mm(u, w_main, "nn", F32, "proj_gates", b_win=(n_qkv, n_gf))
    cum = cum_fwd(gf, b_pad, 2 * d)
    cum_heads = cum[:, :n_heads].T
    cum_col, cum_row = cum_heads[:, :, None], cum_heads[:, None, :]
    o_sb = sb_fwd(qkv, n_heads, 0)
    o_fx, lse = fox_fwd(qkv, cum_col, cum_row, n_heads, 3 * n_heads)
    g_br, g_out, g_gu, g_dn = gather_forward(
        "forward_rest", gather_wait("gather_wait_rest", ag_bufs[1:], ag_sems[2], ag_sems[3], o_fx))
    w_o = g_out.reshape(d, d)
    w_dn = g_dn.reshape(d_ff, d)
    bsb = mm(o_sb, g_br, "nn", F32, "branch_sb", tn=c_br, chunks=(1, 0))
    bfx = mm(o_fx, g_br, "nn", F32, "branch_fox", tn=c_br, chunks=(1, 1))
    merged = gate_fwd(bsb, bfx, gf)
    mix = mm(merged, w_o, "nn", F32, "out_proj")
    h1, u2 = mid_fwd(x2, mix, norm_mix_post, norm_ffn_pre)
    gu = mm(u2, g_gu, "nn", F32, "ffn_gate_up", tn=c_gu, chunks=(2, 0))
    act = swiglu_fwd(gu, c_gu)
    ff = mm(act, w_dn, "nn", F32, "ffn_down")
    dy, d_ff_out, dg_fpost, loss_part = loss_head(h1, ff, norm_ffn_post, tgt)

    p_dn = mm(act, d_ff_out, "tn", BF16, "dw_ffn_down").reshape(N_CHIPS, d_ff // N_CHIPS, d)
    d_act = mm(d_ff_out, w_dn, "nt", F32, "d_act")
    d_gu = swiglu_bwd(d_act, gu, c_gu)
    p_gu = mm(u2, d_gu, "tn", BF16, "dw_ffn_gate_up", tn=c_gu, chunks=(2, 0),
              out_into=lax.empty((N_CHIPS, d, 2 * c_gu), BF16))
    du2 = mm(d_gu, g_gu, "nt", F32, "d_u2", tk=c_gu, chunks=(2, 0))
    dh1, d_mix, dg_fpre, dg_post = mid_bwd(dy, du2, h1, mix, norm_ffn_pre, norm_mix_post)
    p_out = mm(merged, d_mix, "tn", BF16, "dw_out").reshape(N_CHIPS, d // N_CHIPS, d)
    d_merged = mm(d_mix, w_o, "nt", F32, "d_merged")
    d_bsb, d_bfx, d_gs, d_gx = gate_bwd(d_merged, bsb, bfx, gf)
    p_br = mm(o_sb, d_bsb, "tn", BF16, "dw_branch_sb", tn=c_br, chunks=(1, 0),
              out_into=lax.empty((N_CHIPS, d_att, 2 * c_br), BF16))
    p_br = mm(o_fx, d_bfx, "tn", BF16, "dw_branch_fox", tn=c_br, chunks=(1, 1), out_into=p_br)
    d_osb = mm(d_bsb, g_br, "nt", BF16, "d_o_sb", tk=c_br, chunks=(1, 0))
    d_ofx = mm(d_bfx, g_br, "nt", BF16, "d_o_fox", tk=c_br, chunks=(1, 1))

    def reduce_start(tag, pieces, names):
        from_sibling = swap_halves("swap_halves_" + tag, pieces)
        sums = [pair_sum("pair_sum_" + t, p, q, core) for t, p, q in zip(names, pieces, from_sibling)]
        return scatter_start("scatter_start_" + tag, sums)

    def reduce_end(tag, started, names, after):
        send, recv, sums, lands, _ = started
        sums, lands = scatter_wait("scatter_wait_" + tag, sums, lands, send, recv, after)
        return join_halves("join_halves_" + tag, [chip_sum("chip_sum_" + t, sm, got, chip, core)
                                                  for t, sm, got in zip(names, sums, lands)])

    rest_names = ["branch", "out", "gate_up", "down"]
    rest_started = reduce_start("rest", [p_br, p_out, p_gu, p_dn], rest_names)
    d_osb = d_osb + rest_started[4][0, 0].astype(BF16)
    dq_s, dk_s, dv_s = sb_bwd(qkv, d_osb, n_heads, 0)
    dq_f, dk_f, dv_f, dcq, dck = fox_bwd(qkv, cum_col, cum_row, o_fx, d_ofx, lse, n_heads, 3 * n_heads)
    d_cum = jnp.pad((dcq[:, :, 0] + dck[:, 0, :]).T, ((0, 0), (0, LANES - n_heads)))
    d_f, db_pad = cum_bwd(d_cum, gf, b_pad, 2 * d, n_heads)
    d_main = jnp.concatenate(
        [dq_s, dk_s, dv_s, dq_f, dk_f, dv_f, d_gs, d_gx, d_f, jnp.zeros((s, f_pad - LANES), BF16)], axis=1)
    dw_main = mm(u, d_main, "tn", BF16, "dw_in")

    def main_cols(lo, hi):
        out = []
        for s0, s1, m0 in [(0, n_qkv, 0), (n_qkv, n_qkv + n_heads, n_qkv + 2 * d), (n_qkv + n_heads, d_in, n_qkv)]:
            a, b = max(lo, s0), min(hi, s1)
            if a < b:
                out.append(dw_main[:, m0 + a - s0:m0 + b - s0])
        return out

    p_in = jnp.stack([jnp.concatenate(main_cols(k * c_in, (k + 1) * c_in), axis=1) for k in range(N_CHIPS)])

    in_started = reduce_start("w_in", [p_in], ["in"])
    du = mm(d_main, w_main, "nt", F32, "d_u")
    dx, dg_pre = in_bwd(dh1, du, x2, norm_mix_pre + in_started[4][0:1, 0:1])
    gr_br, gr_out, gr_gu, gr_dn = reduce_end("rest", rest_started, rest_names, dx)

    g_bs, d_bs, m_bs, v_bs = adam_update("adam_branch_sb", w_branch_sb[0], m_w_branch_sb[0], v_w_branch_sb[0], gr_br, 0, True)
    g_bf, d_bf, m_bf, v_bf = adam_update("adam_branch_fox", w_branch_fox[0], m_w_branch_fox[0], v_w_branch_fox[0], gr_br, 1, True)
    g_o_, d_o_, m_o_, v_o_ = adam_update("adam_out", w_out[0], m_w_out[0], v_w_out[0], gr_out, 0, True)
    g_ga, d_ga, m_ga, v_ga = adam_update("adam_gate", w_ffn_gate[0], m_w_ffn_gate[0], v_w_ffn_gate[0], gr_gu, 0, True)
    g_up, d_up, m_up, v_up = adam_update("adam_up", w_ffn_up[0], m_w_ffn_up[0], v_w_ffn_up[0], gr_gu, 1, True)
    g_dn_, d_dn, m_dn, v_dn = adam_update("adam_down", w_ffn_down[0], m_w_ffn_down[0], v_w_ffn_down[0], gr_dn, 0, True)
    gr_in, = reduce_end("w_in", in_started, ["in"], d_dn)
    g_in_, d_in_, m_in_, v_in_ = adam_update("adam_w_in", w_in[0], m_w_in[0], v_w_in[0], gr_in, 0, True)
    lead = lambda arrs: [a[None] for a in arrs]
    grads = lead([g_in_, g_bs, g_bf, g_o_, g_ga, g_up, g_dn_])
    deltas = lead([d_in_, d_bs, d_bf, d_o_, d_ga, d_up, d_dn])
    new_ms = lead([m_in_, m_bs, m_bf, m_o_, m_ga, m_up, m_dn])
    new_vs = lead([v_in_, v_bs, v_bf, v_o_, v_ga, v_up, v_dn])

    def pack(rows):
        rows = [jnp.pad(r_, ((0, 0), (0, d - r_.shape[1]))) for r_ in rows]
        return jnp.concatenate(rows + [jnp.zeros((8 - len(rows), d), F32)], axis=0)

    sm_g, sm_d, sm_m, sm_v = small_allreduce_adam(
        pack([dg_pre, dg_post, dg_fpre, dg_fpost, db_pad]),
        pack([norm_mix_pre, norm_mix_post, norm_ffn_pre, norm_ffn_post, b_forget]),
        pack([m_norm_mix_pre, m_norm_mix_post, m_norm_ffn_pre, m_norm_ffn_post, m_b_forget]),
        pack([v_norm_mix_pre, v_norm_mix_post, v_norm_ffn_pre, v_norm_ffn_post, v_b_forget]))

    def small(a):
        return [a[0:1], a[1:2], a[2:3], a[3:4], a[4:5, :n_heads]]

    def ordered(sm, bg):
        return [sm[0], sm[1], bg[0], sm[4], bg[1], bg[2], bg[3], sm[2], sm[3], bg[4], bg[5], bg[6]]

    loss = lax.psum(loss_part[0, 0], ("x", "y", "c"))
    return (loss, dx[None], *ordered(small(sm_g), grads), *ordered(small(sm_d), deltas),
            *ordered(small(sm_m), new_ms), *ordered(small(sm_v), new_vs))
```

```python
import functools

import jax
import jax.numpy as jnp
from jax import lax
from jax.experimental import pallas as pl
from jax.experimental.pallas import tpu as pltpu

F32 = jnp.float32
BF16 = jnp.bfloat16
MESH = pl.DeviceIdType.MESH

HEAD_DIM = 128
LANES = 128
ATT_TILE = 512
ROW_TILE = 256
N_CHIPS = 4
RMS_EPS = 1e-6
ADAM_LR = 0.001
ADAM_B1 = 0.9
ADAM_B2 = 0.999
ADAM_EPS = 1e-08
ADAM_WD = 0.01
ADAM_STEP = 10
NEG_BIG = -1e30
VMEM_LIMIT = 56 * 1024 * 1024
MM_VMEM_BUDGET = 40 * 1024 * 1024
ATT_STRIP = 512

NN = (((1,), (0,)), ((), ()))
NT = (((1,), (1,)), ((), ()))
TN = (((0,), (0,)), ((), ()))


def _tile(n, pref, align):
    best = None
    t = align
    while t <= min(n, pref):
        if n % t == 0:
            best = t
        t += align
    return n if best is None else best


def _params(*sem):
    return pltpu.CompilerParams(dimension_semantics=sem, vmem_limit_bytes=VMEM_LIMIT)


def _mm_tiles(m, n, k, a_bytes, b_bytes, out_bytes, tn, tk):
    tm = _tile(m, 2048, LANES)
    tk = tk or _tile(k, 512, LANES)

    def vmem(t):
        acc = 0 if out_bytes == 4 else tm * t * 4
        return acc + 2 * tm * t * out_bytes + 2 * (tm * tk * a_bytes + tk * t * b_bytes)

    if tn is None:
        fits = [t for t in range(LANES, min(n, 2048) + 1, LANES) if n % t == 0 and vmem(t) <= MM_VMEM_BUDGET]
        tn = max(fits) if fits else _tile(n, LANES, LANES)
    return tm, tn, tk


def mm(a, b, mode, out_dtype, name, *, tn=None, tk=None, b_win=None, chunks=None, out_into=None):
    n_per, blk0 = chunks if chunks else (1, 0)
    if mode == "nn":
        m, k = a.shape
        n = b.shape[0] * n_per * tn if chunks else (b_win[1] if b_win else b.shape[1])
    elif mode == "nt":
        m = a.shape[0]
        k = b.shape[0] * n_per * tk if chunks else a.shape[1]
        n = b.shape[-2]
    else:
        k, m = a.shape
        n = b.shape[1]
    in_place = jnp.dtype(out_dtype) == jnp.dtype(F32)
    tm, tn, tk = _mm_tiles(m, n, k, a.dtype.itemsize, b.dtype.itemsize, jnp.dtype(out_dtype).itemsize, tn, tk)
    assert m % tm == 0 and n % tn == 0 and k % tk == 0, (name, m, n, k, tm, tn, tk)
    j0 = 0
    if b_win:
        assert b_win[0] % tn == 0
        j0 = b_win[0] // tn
    nk = k // tk
    dims = {"nn": NN, "nt": NT, "tn": TN}[mode]

    def kern(a_ref, b_ref, *rest):
        o_ref, acc_ref = (rest[-1], rest[-1]) if in_place else (rest[-2], rest[-1])
        kk = pl.program_id(2)

        @pl.when(kk == 0)
        def _():
            acc_ref[...] = jnp.zeros_like(acc_ref)

        acc_ref[...] += lax.dot_general(a_ref[...].astype(BF16), b_ref[...].astype(BF16), dims,
                                        preferred_element_type=F32)

        if not in_place:
            @pl.when(kk == nk - 1)
            def _():
                o_ref[...] = acc_ref[...].astype(o_ref.dtype)

    out_spec = pl.BlockSpec((tm, tn), lambda i, j, kk: (i, j))
    out_shape = jax.ShapeDtypeStruct((m, n), out_dtype)
    if mode == "nn":
        a_spec = pl.BlockSpec((tm, tk), lambda i, j, kk: (i, kk))
        if chunks:
            b_spec = pl.BlockSpec((None, tk, tn), lambda i, j, kk: (j // n_per, kk, blk0 + j % n_per))
        else:
            b_spec = pl.BlockSpec((tk, tn), lambda i, j, kk: (kk, j + j0))
    elif mode == "nt":
        a_spec = pl.BlockSpec((tm, tk), lambda i, j, kk: (i, kk))
        if chunks:
            b_spec = pl.BlockSpec((None, tn, tk), lambda i, j, kk: (kk // n_per, j, blk0 + kk % n_per))
        else:
            b_spec = pl.BlockSpec((tn, tk), lambda i, j, kk: (j, kk))
    else:
        a_spec = pl.BlockSpec((tk, tm), lambda i, j, kk: (kk, i))
        b_spec = pl.BlockSpec((tk, tn), lambda i, j, kk: (kk, j))
        if chunks:
            out_spec = pl.BlockSpec((None, tm, tn), lambda i, j, kk: (j // n_per, i, blk0 + j % n_per))
    in_specs, operands, aliases = [a_spec, b_spec], [a, b], {}
    if chunks and mode == "tn":
        assert out_into is not None
        out_shape = jax.ShapeDtypeStruct(out_into.shape, out_dtype)
        in_specs.append(pl.BlockSpec(memory_space=pl.ANY))
        operands.append(out_into)
        aliases = {2: 0}
    return pl.pallas_call(
        kern, name=name, grid=(m // tm, n // tn, nk),
        in_specs=in_specs, out_specs=out_spec, out_shape=out_shape,
        scratch_shapes=[] if in_place else [pltpu.VMEM((tm, tn), F32)], input_output_aliases=aliases,
        compiler_params=_params("parallel", "parallel", "arbitrary"),
    )(*operands)


def _rstd(v):
    return lax.rsqrt(jnp.mean(v * v, axis=-1, keepdims=True) + RMS_EPS)


def _norm_bwd(v, g, dy):
    r = _rstd(v)
    vh = v * r
    dyg = dy * g
    dv = r * (dyg - vh * jnp.mean(dyg * vh, axis=-1, keepdims=True))
    return dv, jnp.sum(dy * vh, axis=0, keepdims=True)


def _row_call(kern, name, ins, outs, s, d):
    tr = _tile(s, ROW_TILE, 16)

    def spec(shape, is_row):
        if is_row:
            return pl.BlockSpec((tr, shape[1]), lambda i: (i, 0))
        return pl.BlockSpec(shape, lambda i: (0, 0))

    return pl.pallas_call(
        kern, name=name, grid=(s // tr,),
        in_specs=[spec(a.shape, r) for a, r in ins],
        out_specs=[spec(sh, r) for sh, _, r in outs],
        out_shape=[jax.ShapeDtypeStruct(sh, dt) for sh, dt, _ in outs],
        compiler_params=_params("arbitrary"),
    )(*[a for a, _ in ins])


def norm_in(x, g):
    s, d = x.shape

    def kern(x_ref, g_ref, u_ref):
        v = x_ref[...]
        u_ref[...] = (v * _rstd(v) * g_ref[...]).astype(BF16)

    return _row_call(kern, "norm_in", [(x, True), (g, False)], [((s, d), BF16, True)], s, d)[0]


def mid_fwd(x, mix, g_post, g_fpre):
    s, d = x.shape

    def kern(x_ref, mix_ref, gp_ref, gf_ref, h1_ref, u2_ref):
        mixv = mix_ref[...]
        h1 = x_ref[...] + mixv * _rstd(mixv) * gp_ref[...]
        h1_ref[...] = h1
        u2_ref[...] = (h1 * _rstd(h1) * gf_ref[...]).astype(BF16)

    return _row_call(kern, "mid_fwd", [(x, True), (mix, True), (g_post, False), (g_fpre, False)],
                     [((s, d), F32, True), ((s, d), BF16, True)], s, d)


def loss_head(h1, ff, g_fpost, target):
    s, d = h1.shape

    def kern(h1_ref, ff_ref, g_ref, t_ref, dy_ref, dff_ref, dg_ref, loss_ref):
        @pl.when(pl.program_id(0) == 0)
        def _():
            dg_ref[...] = jnp.zeros_like(dg_ref)
            loss_ref[...] = jnp.zeros_like(loss_ref)

        ffv = ff_ref[...]
        g = g_ref[...]
        y = h1_ref[...] + ffv * _rstd(ffv) * g
        diff = y - t_ref[...]
        row_loss = jnp.mean(diff * diff, axis=-1, keepdims=True)
        loss_ref[...] += 0.5 * jnp.sum(row_loss, axis=0, keepdims=True)
        dy = diff / d
        dy_ref[...] = dy
        dff, dg = _norm_bwd(ffv, g, dy)
        dff_ref[...] = dff.astype(BF16)
        dg_ref[...] += dg

    return _row_call(kern, "loss_head",
                     [(h1, True), (ff, True), (g_fpost, False), (target, True)],
                     [((s, d), F32, True), ((s, d), BF16, True), ((1, d), F32, False), ((1, 1), F32, False)], s, d)


def mid_bwd(dy, du2, h1, mix, g_fpre, g_post):
    s, d = dy.shape

    def kern(dy_ref, du2_ref, h1_ref, mix_ref, gf_ref, gp_ref, dh1_ref, dmix_ref, dgf_ref, dgp_ref):
        @pl.when(pl.program_id(0) == 0)
        def _():
            dgf_ref[...] = jnp.zeros_like(dgf_ref)
            dgp_ref[...] = jnp.zeros_like(dgp_ref)

        dh, dgf = _norm_bwd(h1_ref[...], gf_ref[...], du2_ref[...])
        dh1 = dy_ref[...] + dh
        dh1_ref[...] = dh1
        dmix, dgp = _norm_bwd(mix_ref[...], gp_ref[...], dh1)
        dmix_ref[...] = dmix.astype(BF16)
        dgf_ref[...] += dgf
        dgp_ref[...] += dgp

    return _row_call(kern, "mid_bwd",
                     [(dy, True), (du2, True), (h1, True), (mix, True), (g_fpre, False), (g_post, False)],
                     [((s, d), F32, True), ((s, d), BF16, True), ((1, d), F32, False), ((1, d), F32, False)], s, d)


def in_bwd(dh1, du, x, g_pre):
    s, d = x.shape

    def kern(dh1_ref, du_ref, x_ref, g_ref, dx_ref, dg_ref):
        @pl.when(pl.program_id(0) == 0)
        def _():
            dg_ref[...] = jnp.zeros_like(dg_ref)

        dxn, dg = _norm_bwd(x_ref[...], g_ref[...], du_ref[...])
        dx_ref[...] = dh1_ref[...] + dxn
        dg_ref[...] += dg

    return _row_call(kern, "in_bwd", [(dh1, True), (du, True), (x, True), (g_pre, False)],
                     [((s, d), F32, True), ((1, d), F32, False)], s, d)


def _sigmoid(v):
    return 1.0 / (1.0 + jnp.exp(-v))


def gate_fwd(bsb, bfx, gf):
    s, d = bsb.shape
    tr, tc = _tile(s, 256, 16), _tile(d, 512, LANES)
    nc = d // tc

    def kern(bsb_ref, bfx_ref, gs_ref, gx_ref, o_ref):
        o_ref[...] = (_sigmoid(gs_ref[...]) * bsb_ref[...] + _sigmoid(gx_ref[...]) * bfx_ref[...]).astype(BF16)

    blk = pl.BlockSpec((tr, tc), lambda i, j: (i, j))
    return pl.pallas_call(
        kern, name="gate_fwd", grid=(s // tr, nc),
        in_specs=[blk, blk, blk, pl.BlockSpec((tr, tc), lambda i, j: (i, j + nc))],
        out_specs=blk, out_shape=jax.ShapeDtypeStruct((s, d), BF16),
        compiler_params=_params("parallel", "parallel"),
    )(bsb, bfx, gf, gf)


def gate_bwd(dmerged, bsb, bfx, gf):
    s, d = bsb.shape
    tr, tc = _tile(s, 256, 16), _tile(d, 512, LANES)
    nc = d // tc

    def kern(dm_ref, bsb_ref, bfx_ref, gs_ref, gx_ref, dbs_ref, dbx_ref, dgs_ref, dgx_ref):
        dm = dm_ref[...]
        ss = _sigmoid(gs_ref[...])
        sx = _sigmoid(gx_ref[...])
        dbs_ref[...] = (dm * ss).astype(BF16)
        dbx_ref[...] = (dm * sx).astype(BF16)
        dgs_ref[...] = (dm * bsb_ref[...] * ss * (1.0 - ss)).astype(BF16)
        dgx_ref[...] = (dm * bfx_ref[...] * sx * (1.0 - sx)).astype(BF16)

    blk = pl.BlockSpec((tr, tc), lambda i, j: (i, j))
    out = jax.ShapeDtypeStruct((s, d), BF16)
    return pl.pallas_call(
        kern, name="gate_bwd", grid=(s // tr, nc),
        in_specs=[blk, blk, blk, blk, pl.BlockSpec((tr, tc), lambda i, j: (i, j + nc))],
        out_specs=[blk, blk, blk, blk], out_shape=[out, out, out, out],
        compiler_params=_params("parallel", "parallel"),
    )(dmerged, bsb, bfx, gf, gf)


def swiglu_fwd(gu, cw):
    s, f2 = gu.shape
    tr = _tile(s, 256, 16)

    def kern(gu_ref, o_ref):
        g = gu_ref[:, :cw]
        o_ref[...] = (g * _sigmoid(g) * gu_ref[:, cw:]).astype(BF16)

    return pl.pallas_call(
        kern, name="swiglu_fwd", grid=(s // tr, f2 // (2 * cw)),
        in_specs=[pl.BlockSpec((tr, 2 * cw), lambda i, j: (i, j))],
        out_specs=pl.BlockSpec((tr, cw), lambda i, j: (i, j)),
        out_shape=jax.ShapeDtypeStruct((s, f2 // 2), BF16),
        compiler_params=_params("parallel", "parallel"),
    )(gu)


def swiglu_bwd(dact, gu, cw):
    s, f2 = gu.shape
    tr = _tile(s, 256, 16)

    def kern(da_ref, gu_ref, o_ref):
        da = da_ref[...]
        g = gu_ref[:, :cw]
        sg = _sigmoid(g)
        o_ref[:, :cw] = (da * gu_ref[:, cw:] * (sg * (1.0 + g * (1.0 - sg)))).astype(BF16)
        o_ref[:, cw:] = (da * (g * sg)).astype(BF16)

    return pl.pallas_call(
        kern, name="swiglu_bwd", grid=(s // tr, f2 // (2 * cw)),
        in_specs=[pl.BlockSpec((tr, cw), lambda i, j: (i, j)), pl.BlockSpec((tr, 2 * cw), lambda i, j: (i, j))],
        out_specs=pl.BlockSpec((tr, 2 * cw), lambda i, j: (i, j)),
        out_shape=jax.ShapeDtypeStruct((s, f2), BF16),
        compiler_params=_params("parallel", "parallel"),
    )(dact, gu)


def _split3(v):
    hi = v.astype(BF16)
    r = v - hi.astype(F32)
    mid = r.astype(BF16)
    lo = (r - mid.astype(F32)).astype(BF16)
    return hi, mid, lo


def _dot3_right(v, ones):
    hi, mid, lo = _split3(v)
    d = lambda p: jnp.dot(p, ones, preferred_element_type=F32)
    return (d(lo) + d(mid)) + d(hi)


def _dot3_left(ones, v):
    hi, mid, lo = _split3(v)
    d = lambda p: jnp.dot(ones, p, preferred_element_type=F32)
    return (d(lo) + d(mid)) + d(hi)


def _split2(v):
    hi = v.astype(BF16)
    return hi, (v - hi.astype(F32)).astype(BF16)


def _dot2_right(v, ones):
    hi, lo = _split2(v)
    return jnp.dot(lo, ones, preferred_element_type=F32) + jnp.dot(hi, ones, preferred_element_type=F32)


def _log1p_exp_neg_abs(v):
    return jnp.log(1.0 + jnp.exp(-jnp.abs(v)))


def _mask01(cond):
    return jnp.where(cond, 1.0, 0.0).astype(BF16)


def _iota2(t):
    return (lax.broadcasted_iota(jnp.int32, (t, t), 0), lax.broadcasted_iota(jnp.int32, (t, t), 1))


def cum_fwd(gf, b_pad, f_col0):
    s = gf.shape[0]
    t = _tile(s, ATT_TILE, LANES)
    fb = f_col0 // LANES

    def kern(f_ref, b_ref, cum_ref, carry_ref):
        @pl.when(pl.program_id(0) == 0)
        def _():
            carry_ref[...] = jnp.zeros_like(carry_ref)

        v = f_ref[...] + b_ref[...]
        lf = jnp.minimum(v, 0.0) - _log1p_exp_neg_abs(v)
        row, col = _iota2(t)
        cum = _dot3_left(_mask01(col <= row), lf) + carry_ref[...]
        cum_ref[...] = cum
        carry_ref[...] = cum[t - 1:t, :]

    return pl.pallas_call(
        kern, name="cum_fwd", grid=(s // t,),
        in_specs=[pl.BlockSpec((t, LANES), lambda i: (i, fb)), pl.BlockSpec((1, LANES), lambda i: (0, 0))],
        out_specs=pl.BlockSpec((t, LANES), lambda i: (i, 0)),
        out_shape=jax.ShapeDtypeStruct((s, LANES), F32),
        scratch_shapes=[pltpu.VMEM((1, LANES), F32)],
        compiler_params=_params("arbitrary"),
    )(gf, b_pad)


def cum_bwd(dcum, gf, b_pad, f_col0, n_heads):
    s = gf.shape[0]
    t = _tile(s, ATT_TILE, LANES)
    nb = s // t
    fb = f_col0 // LANES

    def kern(dc_ref, f_ref, b_ref, df_ref, db_ref, carry_ref):
        @pl.when(pl.program_id(0) == 0)
        def _():
            carry_ref[...] = jnp.zeros_like(carry_ref)
            db_ref[...] = jnp.zeros_like(db_ref)

        row, col = _iota2(t)
        dlf = _dot3_left(_mask01(col >= row), dc_ref[...]) + carry_ref[...]
        carry_ref[...] = dlf[0:1, :]
        v = f_ref[...] + b_ref[...]
        sig_neg = jnp.exp(-jnp.maximum(v, 0.0) - _log1p_exp_neg_abs(v))
        lane = lax.broadcasted_iota(jnp.int32, (t, LANES), 1)
        df = jnp.where(lane < n_heads, dlf * sig_neg, 0.0)
        df_ref[...] = df.astype(BF16)
        db_ref[...] += jnp.sum(df, axis=0, keepdims=True)

    return pl.pallas_call(
        kern, name="cum_bwd", grid=(nb,),
        in_specs=[pl.BlockSpec((t, LANES), lambda i: (nb - 1 - i, 0)),
                  pl.BlockSpec((t, LANES), lambda i: (nb - 1 - i, fb)),
                  pl.BlockSpec((1, LANES), lambda i: (0, 0))],
        out_specs=[pl.BlockSpec((t, LANES), lambda i: (nb - 1 - i, 0)), pl.BlockSpec((1, LANES), lambda i: (0, 0))],
        out_shape=[jax.ShapeDtypeStruct((s, LANES), BF16), jax.ShapeDtypeStruct((1, LANES), F32)],
        scratch_shapes=[pltpu.VMEM((1, LANES), F32)],
        compiler_params=_params("arbitrary"),
    )(dcum, gf, b_pad)


def _qkv_specs(s, t, n_heads, base):
    return [pl.BlockSpec((t, HEAD_DIM), lambda h, i: (i, base + h)),
            pl.BlockSpec((s, HEAD_DIM), lambda h, i: (0, base + n_heads + h)),
            pl.BlockSpec((s, HEAD_DIM), lambda h, i: (0, base + 2 * n_heads + h))]


def _strips(t):
    sr = _tile(t, ATT_STRIP, 8)
    return sr, t // sr, [slice(si * sr, (si + 1) * sr) for si in range(t // sr)]


def _key_minus_row(sr, t):
    return lax.broadcasted_iota(jnp.int32, (sr, t), 1) - lax.broadcasted_iota(jnp.int32, (sr, t), 0)


def _sb_scores(q, k, diff, lim):
    z = lax.dot_general(q, k, NT, preferred_element_type=F32) * (HEAD_DIM ** -0.5)
    valid = diff < lim
    l1p = _log1p_exp_neg_abs(z)
    log_keep = jnp.where(valid, -jnp.maximum(z, 0.0) - l1p, 0.0)
    return z, valid, l1p, log_keep


def sb_fwd(qkv, n_heads, base):
    s = qkv.shape[0]
    t = _tile(s, ATT_TILE, LANES)
    sr, ns, strips = _strips(t)

    def kern(q_ref, k_ref, v_ref, o_ref):
        i = pl.program_id(1)
        row, col = _iota2(t)
        after = _mask01(row > col)
        diff = _key_minus_row(sr, t)
        qs = [q_ref[sl, :] for sl in strips]

        def body(jj, carry):
            runs, accs = carry
            j = i - jj
            off = pl.multiple_of(j * t, t)
            k = k_ref[pl.ds(off, t), :]
            v = v_ref[pl.ds(off, t), :]
            lim = jnp.where(j < i, t, 0)
            new_runs, new_accs = [], []
            for si in range(ns):
                z, valid, l1p, log_keep = _sb_scores(qs[si], k, diff, lim + si * sr)
                between = _dot2_right(log_keep, after) + runs[si]
                w = jnp.where(valid, jnp.exp(jnp.minimum(z, 0.0) - l1p + between), 0.0)
                new_accs.append(accs[si] + jnp.dot(w.astype(BF16), v, preferred_element_type=F32))
                new_runs.append(runs[si] + jnp.sum(log_keep, axis=1, keepdims=True))
            return tuple(new_runs), tuple(new_accs)

        init = (tuple(jnp.zeros((sr, 1), F32) for _ in strips), tuple(jnp.zeros((sr, HEAD_DIM), F32) for _ in strips))
        _, accs = lax.fori_loop(0, i + 1, body, init)
        for sl, acc in zip(strips, accs):
            o_ref[sl, :] = acc.astype(o_ref.dtype)

    return pl.pallas_call(
        kern, name="sb_fwd", grid=(n_heads, s // t),
        in_specs=_qkv_specs(s, t, n_heads, base),
        out_specs=pl.BlockSpec((t, HEAD_DIM), lambda h, i: (i, h)),
        out_shape=jax.ShapeDtypeStruct((s, n_heads * HEAD_DIM), BF16),
        compiler_params=_params("parallel", "arbitrary"),
    )(qkv, qkv, qkv)


def sb_bwd(qkv, d_o, n_heads, base):
    s = qkv.shape[0]
    t = _tile(s, ATT_TILE, LANES)
    nq = s // t
    sr, ns, strips = _strips(t)
    scale = HEAD_DIM ** -0.5

    def kern(q_ref, k_ref, v_ref, do_ref, dq_ref, dk_ref, dv_ref, dk_acc, dv_acc, run_ref):
        i = pl.program_id(1)

        @pl.when(i == 0)
        def _():
            dk_acc[...] = jnp.zeros_like(dk_acc)
            dv_acc[...] = jnp.zeros_like(dv_acc)

        row, col = _iota2(t)
        after = _mask01(row > col)
        before = _mask01(row < col)
        diff = _key_minus_row(sr, t)
        qs = [q_ref[sl, :] for sl in strips]
        dos = [do_ref[sl, :] for sl in strips]

        def sweep1(jj, runs):
            j = i - jj
            k = k_ref[pl.ds(pl.multiple_of(j * t, t), t), :]
            lim = jnp.where(j < i, t, 0)
            new_runs = []
            for si, sl in enumerate(strips):
                _, _, _, log_keep = _sb_scores(qs[si], k, diff, lim + si * sr)
                run_ref[j, sl, :] = runs[si]
                new_runs.append(runs[si] + jnp.sum(log_keep, axis=1, keepdims=True))
            return tuple(new_runs)

        lax.fori_loop(0, i + 1, sweep1, tuple(jnp.zeros((sr, 1), F32) for _ in strips))

        def sweep2(j, carry):
            run_es, dqs = carry
            off = pl.multiple_of(j * t, t)
            k = k_ref[pl.ds(off, t), :]
            v = v_ref[pl.ds(off, t), :]
            lim = jnp.where(j < i, t, 0)
            new_es, new_dqs = [], []
            dk_t = jnp.zeros((t, HEAD_DIM), F32)
            dv_t = jnp.zeros((t, HEAD_DIM), F32)
            for si, sl in enumerate(strips):
                z, valid, l1p, log_keep = _sb_scores(qs[si], k, diff, lim + si * sr)
                between = _dot2_right(log_keep, after) + run_ref[j, sl, :]
                w = jnp.where(valid, jnp.exp(jnp.minimum(z, 0.0) - l1p + between), 0.0)
                dw = lax.dot_general(dos[si], v, NT, preferred_element_type=F32)
                e = dw * w
                e_before = _dot2_right(e, before) + run_es[si]
                keep = jnp.exp(log_keep)
                dz = jnp.where(valid, e * keep - e_before * (1.0 - keep), 0.0) * scale
                dzb = dz.astype(BF16)
                new_dqs.append(dqs[si] + jnp.dot(dzb, k, preferred_element_type=F32))
                dk_t = dk_t + lax.dot_general(dzb, qs[si], TN, preferred_element_type=F32)
                dv_t = dv_t + lax.dot_general(w.astype(BF16), dos[si], TN, preferred_element_type=F32)
                new_es.append(run_es[si] + jnp.sum(e, axis=1, keepdims=True))
            dk_acc[pl.ds(off, t), :] += dk_t
            dv_acc[pl.ds(off, t), :] += dv_t
            return tuple(new_es), tuple(new_dqs)

        init = (tuple(jnp.zeros((sr, 1), F32) for _ in strips), tuple(jnp.zeros((sr, HEAD_DIM), F32) for _ in strips))
        _, dqs = lax.fori_loop(0, i + 1, sweep2, init)
        for sl, dq in zip(strips, dqs):
            dq_ref[sl, :] = dq.astype(BF16)

        @pl.when(i == nq - 1)
        def _():
            dk_ref[...] = dk_acc[...].astype(BF16)
            dv_ref[...] = dv_acc[...].astype(BF16)

    out = jax.ShapeDtypeStruct((s, n_heads * HEAD_DIM), BF16)
    head_blk = pl.BlockSpec((s, HEAD_DIM), lambda h, i: (0, h))
    tile_blk = pl.BlockSpec((t, HEAD_DIM), lambda h, i: (i, h))
    return pl.pallas_call(
        kern, name="sb_bwd", grid=(n_heads, nq),
        in_specs=_qkv_specs(s, t, n_heads, base) + [tile_blk],
        out_specs=[tile_blk, head_blk, head_blk],
        out_shape=[out, out, out],
        scratch_shapes=[pltpu.VMEM((s, HEAD_DIM), F32), pltpu.VMEM((s, HEAD_DIM), F32), pltpu.VMEM((nq, t, 1), F32)],
        compiler_params=_params("parallel", "arbitrary"),
    )(qkv, qkv, qkv, d_o)


def _fox_scores(q, k, cq, ck, diff, lim):
    sc = lax.dot_general(q, k, NT, preferred_element_type=F32) * (HEAD_DIM ** -0.5)
    sc = sc + cq - ck
    valid = diff < lim
    return jnp.where(valid, sc, NEG_BIG), valid


def fox_fwd(qkv, cum_col, cum_row, n_heads, base):
    s = qkv.shape[0]
    t = _tile(s, ATT_TILE, LANES)
    sr, ns, strips = _strips(t)

    def kern(q_ref, k_ref, v_ref, cq_ref, ck_ref, o_ref, lse_ref):
        i = pl.program_id(1)
        diff = _key_minus_row(sr, t)
        qs = [q_ref[sl, :] for sl in strips]
        cqs = [cq_ref[0, sl, :] for sl in strips]

        def body(j, carry):
            off = pl.multiple_of(j * t, t)
            k = k_ref[pl.ds(off, t), :]
            v = v_ref[pl.ds(off, t), :]
            ck = ck_ref[0, :, pl.ds(off, t)]
            lim = jnp.where(j < i, t, 1)
            out = []
            for si in range(ns):
                m, l, acc = carry[si]
                sc, _ = _fox_scores(qs[si], k, cqs[si], ck, diff, lim + si * sr)
                m_new = jnp.maximum(m, jnp.max(sc, axis=1, keepdims=True))
                p = jnp.exp(sc - m_new)
                alpha = jnp.exp(m - m_new)
                l = alpha * l + jnp.sum(p, axis=1, keepdims=True)
                acc = alpha * acc + jnp.dot(p.astype(BF16), v, preferred_element_type=F32)
                out.append((m_new, l, acc))
            return tuple(out)

        init = tuple((jnp.full((sr, 1), NEG_BIG, F32), jnp.zeros((sr, 1), F32), jnp.zeros((sr, HEAD_DIM), F32))
                     for _ in strips)
        res = lax.fori_loop(0, i + 1, body, init)
        for sl, (m, l, acc) in zip(strips, res):
            o_ref[sl, :] = acc / l
            lse_ref[0, sl, :] = m + jnp.log(l)

    col_blk = pl.BlockSpec((1, t, 1), lambda h, i: (h, i, 0))
    return pl.pallas_call(
        kern, name="fox_fwd", grid=(n_heads, s // t),
        in_specs=_qkv_specs(s, t, n_heads, base) + [col_blk, pl.BlockSpec((1, 1, s), lambda h, i: (h, 0, 0))],
        out_specs=[pl.BlockSpec((t, HEAD_DIM), lambda h, i: (i, h)), col_blk],
        out_shape=[jax.ShapeDtypeStruct((s, n_heads * HEAD_DIM), F32), jax.ShapeDtypeStruct((n_heads, s, 1), F32)],
        compiler_params=_params("parallel", "arbitrary"),
    )(qkv, qkv, qkv, cum_col, cum_row)


def fox_bwd(qkv, cum_col, cum_row, o, d_o, lse, n_heads, base):
    s = qkv.shape[0]
    t = _tile(s, ATT_TILE, LANES)
    nq = s // t
    sr, ns, strips = _strips(t)
    scale = HEAD_DIM ** -0.5

    def kern(q_ref, k_ref, v_ref, cq_ref, ck_ref, o_ref, do_ref, lse_ref,
             dq_ref, dk_ref, dv_ref, dcq_ref, dck_ref, dk_acc, dv_acc, dck_acc):
        i = pl.program_id(1)

        @pl.when(i == 0)
        def _():
            dk_acc[...] = jnp.zeros_like(dk_acc)
            dv_acc[...] = jnp.zeros_like(dv_acc)
            dck_acc[...] = jnp.zeros_like(dck_acc)

        diff = _key_minus_row(sr, t)
        qs = [q_ref[sl, :] for sl in strips]
        dos = [do_ref[sl, :] for sl in strips]
        cqs = [cq_ref[0, sl, :] for sl in strips]
        lses = [lse_ref[0, sl, :] for sl in strips]
        deltas = [jnp.sum(dos[si].astype(F32) * o_ref[sl, :], axis=1, keepdims=True) for si, sl in enumerate(strips)]

        def body(j, carry):
            off = pl.multiple_of(j * t, t)
            k = k_ref[pl.ds(off, t), :]
            v = v_ref[pl.ds(off, t), :]
            ck = ck_ref[0, :, pl.ds(off, t)]
            lim = jnp.where(j < i, t, 1)
            out = []
            dk_t = jnp.zeros((t, HEAD_DIM), F32)
            dv_t = jnp.zeros((t, HEAD_DIM), F32)
            dck_t = jnp.zeros((1, t), F32)
            for si in range(ns):
                dq, dcq = carry[si]
                sc, valid = _fox_scores(qs[si], k, cqs[si], ck, diff, lim + si * sr)
                p = jnp.where(valid, jnp.exp(sc - lses[si]), 0.0)
                dp = lax.dot_general(dos[si], v, NT, preferred_element_type=F32)
                ds = p * (dp - deltas[si])
                dsb = (ds * scale).astype(BF16)
                dq = dq + jnp.dot(dsb, k, preferred_element_type=F32)
                dk_t = dk_t + lax.dot_general(dsb, qs[si], TN, preferred_element_type=F32)
                dv_t = dv_t + lax.dot_general(p.astype(BF16), dos[si], TN, preferred_element_type=F32)
                dck_t = dck_t + jnp.sum(ds, axis=0, keepdims=True)
                out.append((dq, dcq + jnp.sum(ds, axis=1, keepdims=True)))
            dk_acc[pl.ds(off, t), :] += dk_t
            dv_acc[pl.ds(off, t), :] += dv_t
            dck_acc[:, pl.ds(off, t)] -= dck_t
            return tuple(out)

        init = tuple((jnp.zeros((sr, HEAD_DIM), F32), jnp.zeros((sr, 1), F32)) for _ in strips)
        res = lax.fori_loop(0, i + 1, body, init)
        for sl, (dq, dcq) in zip(strips, res):
            dq_ref[sl, :] = dq.astype(BF16)
            dcq_ref[0, sl, :] = dcq

        @pl.when(i == nq - 1)
        def _():
            dk_ref[...] = dk_acc[...].astype(BF16)
            dv_ref[...] = dv_acc[...].astype(BF16)
            dck_ref[0] = dck_acc[...]

    out = jax.ShapeDtypeStruct((s, n_heads * HEAD_DIM), BF16)
    head_blk = pl.BlockSpec((s, HEAD_DIM), lambda h, i: (0, h))
    tile_blk = pl.BlockSpec((t, HEAD_DIM), lambda h, i: (i, h))
    col_blk = pl.BlockSpec((1, t, 1), lambda h, i: (h, i, 0))
    row_blk = pl.BlockSpec((1, 1, s), lambda h, i: (h, 0, 0))
    return pl.pallas_call(
        kern, name="fox_bwd", grid=(n_heads, nq),
        in_specs=_qkv_specs(s, t, n_heads, base) + [col_blk, row_blk, tile_blk, tile_blk, col_blk],
        out_specs=[tile_blk, head_blk, head_blk, col_blk, row_blk],
        out_shape=[out, out, out, jax.ShapeDtypeStruct((n_heads, s, 1), F32),
                   jax.ShapeDtypeStruct((n_heads, 1, s), F32)],
        scratch_shapes=[pltpu.VMEM((s, HEAD_DIM), F32), pltpu.VMEM((s, HEAD_DIM), F32), pltpu.VMEM((1, s), F32)],
        compiler_params=_params("parallel", "arbitrary"),
    )(qkv, qkv, qkv, cum_col, cum_row, o, d_o, lse)


def _place():
    x, y, c = lax.axis_index("x"), lax.axis_index("y"), lax.axis_index("c")
    other_chips = [(1 - x, y), (x, 1 - y), (1 - x, 1 - y)]
    return x, y, c, other_chips


ANY = pl.BlockSpec(memory_space=pl.ANY)


def _remote(src, dst, send_sem, recv_sem, dev):
    return pltpu.make_async_remote_copy(src_ref=src, dst_ref=dst, send_sem=send_sem, recv_sem=recv_sem,
                                        device_id=dev, device_id_type=MESH)


def cast_place(name, ws, chip):
    r = ws[0].shape[1]
    cs = [w.shape[2] for w in ws]
    tr = _tile(r, 256, 16)

    def kern(chip_ref, *refs):
        o_ref = refs[-1]
        off = 0
        for w_ref, c in zip(refs[:-1], cs):
            o_ref[:, off:off + c] = w_ref[...].astype(BF16)
            off += c

    return pl.pallas_call(
        kern, name=name,
        grid_spec=pltpu.PrefetchScalarGridSpec(
            num_scalar_prefetch=1, grid=(r // tr,),
            in_specs=[pl.BlockSpec((None, tr, c), lambda i, chip_ref: (0, i, 0)) for c in cs],
            out_specs=pl.BlockSpec((None, tr, sum(cs)), lambda i, chip_ref: (chip_ref[0], i, 0))),
        out_shape=jax.ShapeDtypeStruct((N_CHIPS, r, sum(cs)), BF16),
        compiler_params=_params("parallel"),
    )(chip, *ws)


HBM = pl.BlockSpec(memory_space=pltpu.HBM)
SEM = pl.BlockSpec(memory_space=pltpu.SEMAPHORE)
SPLIT = pltpu.CompilerParams(has_side_effects=pltpu.SideEffectType.DATAFLOW_SIDE_EFFECTING)


def _in_hbm(a):
    return pltpu.with_memory_space_constraint(a, pltpu.HBM)


def _slab_rows(ref, k, core):
    half = ref.shape[1] // 2
    return ref.at[k, pl.ds(pl.multiple_of(core * half, 16), half)]


def gather_start(name, bufs):
    n = len(bufs)

    def body(*refs):
        ins, send, recv, token = refs[:n], refs[n], refs[n + 1], refs[-1]
        x, y, c, chips = _place()
        me = 2 * x + y
        for a in range(n):
            for j in range(3):
                rows = _slab_rows(ins[a], me, c)
                _remote(rows, rows, send.at[3 * a + j], recv.at[3 * a + j], (chips[j][0], chips[j][1], c)).start()
        token[...] = jnp.zeros_like(token)

    sem = pltpu.SemaphoreType.DMA((3 * n,))
    res = pl.pallas_call(
        body, name=name, in_specs=[HBM] * n, out_specs=[SEM, SEM] + [HBM] * n + [pl.BlockSpec(memory_space=pltpu.VMEM)],
        out_shape=[sem, sem] + [pltpu.HBM(b.shape, b.dtype) for b in bufs] + [jax.ShapeDtypeStruct((8, LANES), F32)],
        input_output_aliases={a: 2 + a for a in range(n)}, compiler_params=SPLIT,
    )(*[_in_hbm(b) for b in bufs])
    return res[0], res[1], res[2:2 + n], res[-1]


def gather_wait(name, bufs, send_sems, recv_sems, after):
    n = len(bufs)

    def body(*refs):
        ins, send, recv = refs[:n], refs[n], refs[n + 1]
        x, y, c, chips = _place()
        me = 2 * x + y
        for a in range(n):
            for j in range(3):
                dev = (chips[j][0], chips[j][1], c)
                mine = _slab_rows(ins[a], me, c)
                _remote(mine, mine, send.at[3 * a + j], recv.at[3 * a + j], dev).wait_send()
                land = _slab_rows(ins[a], 2 * chips[j][0] + chips[j][1], c)
                _remote(land, land, send.at[3 * a + j], recv.at[3 * a + j], dev).wait_recv()

    return pl.pallas_call(
        body, name=name, in_specs=[HBM] * n + [SEM, SEM, ANY], out_specs=[HBM] * n,
        out_shape=[pltpu.HBM(b.shape, b.dtype) for b in bufs],
        input_output_aliases={a: a for a in range(n)}, compiler_params=SPLIT,
    )(*bufs, send_sems, recv_sems, after)


def gather_forward(name, bufs):
    n = len(bufs)

    def body(*refs):
        outs = refs[n:2 * n]
        send_sems, recv_sems = refs[2 * n:]
        x, y, c, chips = _place()
        sibling = (x, y, 1 - c)

        def d2d(a, j, core):
            rows = _slab_rows(outs[a], 2 * chips[j][0] + chips[j][1], core)
            return _remote(rows, rows, send_sems.at[3 * a + j], recv_sems.at[3 * a + j], sibling)

        pairs = [(a, j) for a in range(n) for j in range(3)]
        for a, j in pairs:
            d2d(a, j, c).start()
        for a, j in pairs:
            d2d(a, j, 1 - c).wait_recv()
        for a, j in pairs:
            d2d(a, j, c).wait_send()

    return pl.pallas_call(
        body, name=name, in_specs=[ANY] * n, out_specs=[ANY] * n,
        out_shape=[jax.ShapeDtypeStruct(b.shape, b.dtype) for b in bufs],
        input_output_aliases={a: a for a in range(n)},
        scratch_shapes=[pltpu.SemaphoreType.DMA((3 * n,)), pltpu.SemaphoreType.DMA((3 * n,))],
    )(*bufs)


def swap_halves(name, pieces):
    n = len(pieces)
    halves = [p.shape[1] // 2 for p in pieces]

    def body(*refs):
        ins, outs = refs[:n], refs[n:2 * n]
        send_sems, recv_sems = refs[2 * n:]
        x, y, c, _ = _place()
        cps = [_remote(ins[a].at[:, pl.ds(pl.multiple_of((1 - c) * halves[a], 16), halves[a]), :], outs[a],
                       send_sems.at[a], recv_sems.at[a], (x, y, 1 - c)) for a in range(n)]
        for cp in cps:
            cp.start()
        for cp in cps:
            cp.wait()

    return pl.pallas_call(
        body, name=name, in_specs=[ANY] * n, out_specs=[ANY] * n,
        out_shape=[jax.ShapeDtypeStruct((N_CHIPS, h, p.shape[2]), p.dtype) for p, h in zip(pieces, halves)],
        scratch_shapes=[pltpu.SemaphoreType.DMA((n,)), pltpu.SemaphoreType.DMA((n,))],
    )(*pieces)


def pair_sum(name, pieces, got, core):
    _, r, w = pieces.shape
    half = r // 2
    tr = _tile(half, 256, 16)

    def kern(core_ref, p_ref, g_ref, o_ref):
        o_ref[...] = (p_ref[...].astype(F32) + g_ref[...].astype(F32)).astype(o_ref.dtype)

    return pl.pallas_call(
        kern, name=name,
        grid_spec=pltpu.PrefetchScalarGridSpec(
            num_scalar_prefetch=1, grid=(N_CHIPS, half // tr),
            in_specs=[pl.BlockSpec((None, None, tr, w), lambda k, i, core_ref: (k, core_ref[0], i, 0)),
                      pl.BlockSpec((None, tr, w), lambda k, i, core_ref: (k, i, 0))],
            out_specs=pl.BlockSpec((None, tr, w), lambda k, i, core_ref: (k, i, 0))),
        out_shape=jax.ShapeDtypeStruct((N_CHIPS, half, w), pieces.dtype),
        compiler_params=_params("parallel", "parallel"),
    )(core, pieces.reshape(N_CHIPS, 2, half, w), got)


def _scatter_copies(sums, lands, send, recv):
    x, y, c, chips = _place()
    return [_remote(sums[a].at[2 * chips[j][0] + chips[j][1]], lands[a].at[j], send.at[3 * a + j], recv.at[3 * a + j],
                    (chips[j][0], chips[j][1], c)) for a in range(len(sums)) for j in range(3)]


def scatter_start(name, sums):
    n = len(sums)
    lands = [lax.empty((3,) + t.shape[1:], t.dtype) for t in sums]

    def body(*refs):
        ins, land_in, send, recv, token = refs[:n], refs[n:2 * n], refs[2 * n], refs[2 * n + 1], refs[-1]
        for cp in _scatter_copies(ins, land_in, send, recv):
            cp.start()
        token[...] = jnp.zeros_like(token)

    sem = pltpu.SemaphoreType.DMA((3 * n,))
    res = pl.pallas_call(
        body, name=name, in_specs=[HBM] * (2 * n),
        out_specs=[SEM, SEM] + [HBM] * (2 * n) + [pl.BlockSpec(memory_space=pltpu.VMEM)],
        out_shape=[sem, sem] + [pltpu.HBM(t.shape, t.dtype) for t in sums + lands] + [jax.ShapeDtypeStruct((8, LANES), F32)],
        input_output_aliases={a: 2 + a for a in range(2 * n)}, compiler_params=SPLIT,
    )(*[_in_hbm(t) for t in sums + lands])
    return res[0], res[1], res[2:2 + n], res[2 + n:2 + 2 * n], res[-1]


def scatter_wait(name, sums, lands, send_sems, recv_sems, after):
    n = len(sums)

    def body(*refs):
        ins, land_in, send, recv = refs[:n], refs[n:2 * n], refs[2 * n], refs[2 * n + 1]
        for cp in _scatter_copies(ins, land_in, send, recv):
            cp.wait_send()
            cp.wait_recv()

    res = pl.pallas_call(
        body, name=name, in_specs=[HBM] * (2 * n) + [SEM, SEM, ANY], out_specs=[HBM] * (2 * n),
        out_shape=[pltpu.HBM(t.shape, t.dtype) for t in sums + lands],
        input_output_aliases={a: a for a in range(2 * n)}, compiler_params=SPLIT,
    )(*sums, *lands, send_sems, recv_sems, after)
    return res[:n], res[n:]


def chip_sum(name, sums, got, chip, core):
    _, half, w = sums.shape
    tr = _tile(half, 256, 16)
    nb = half // tr

    def kern(ids_ref, s_ref, g0_ref, g1_ref, g2_ref, o_ref):
        o_ref[...] = ((s_ref[...].astype(F32) + g0_ref[...].astype(F32)) + g1_ref[...].astype(F32)) \
            + g2_ref[...].astype(F32)

    def got_spec(j):
        return pl.BlockSpec((None, tr, w), lambda i, ids_ref: (j, i, 0))

    return pl.pallas_call(
        kern, name=name,
        grid_spec=pltpu.PrefetchScalarGridSpec(
            num_scalar_prefetch=1, grid=(nb,),
            in_specs=[pl.BlockSpec((None, tr, w), lambda i, ids_ref: (ids_ref[0], i, 0)),
                      got_spec(0), got_spec(1), got_spec(2)],
            out_specs=pl.BlockSpec((tr, w), lambda i, ids_ref: (ids_ref[1] * nb + i, 0))),
        out_shape=jax.ShapeDtypeStruct((2 * half, w), F32),
        compiler_params=_params("parallel"),
    )(jnp.concatenate([chip, core]), sums, got, got, got)


def join_halves(name, shards):
    n = len(shards)
    halves = [g.shape[0] // 2 for g in shards]

    def body(*refs):
        outs = refs[n:2 * n]
        send_sems, recv_sems = refs[2 * n:]
        x, y, c, _ = _place()
        cps = []
        for a in range(n):
            rows = outs[a].at[pl.ds(pl.multiple_of(c * halves[a], 8), halves[a])]
            cps.append(_remote(rows, rows, send_sems.at[a], recv_sems.at[a], (x, y, 1 - c)))
        for cp in cps:
            cp.start()
        for cp in cps:
            cp.wait()

    return pl.pallas_call(
        body, name=name, in_specs=[ANY] * n, out_specs=[ANY] * n,
        out_shape=[jax.ShapeDtypeStruct(g.shape, g.dtype) for g in shards],
        input_output_aliases={a: a for a in range(n)},
        scratch_shapes=[pltpu.SemaphoreType.DMA((n,)), pltpu.SemaphoreType.DMA((n,))],
    )(*shards)


def _adam(w, g, m, v):
    m = ADAM_B1 * m + (1.0 - ADAM_B1) * g
    v = ADAM_B2 * v + (1.0 - ADAM_B2) * (g * g)
    m_hat = m / (1.0 - ADAM_B1 ** ADAM_STEP)
    v_hat = v / (1.0 - ADAM_B2 ** ADAM_STEP)
    delta = -ADAM_LR * (m_hat / (jnp.sqrt(v_hat) + ADAM_EPS) + ADAM_WD * w)
    return delta, m, v


def small_allreduce_adam(g_part, w, m, v):
    n_dev = 8
    r, d = g_part.shape

    def body(g_ref, w_ref, m_ref, v_ref, gs_ref, dl_ref, nm_ref, nv_ref, all_ref, send_sems, recv_sems):
        x, y, c, _ = _place()
        me = 4 * x + 2 * y + c
        all_ref[me] = g_ref[...]
        cps = []
        for rel in range(1, n_dev):
            px = 1 - x if rel & 4 else x
            py = 1 - y if rel & 2 else y
            pc = 1 - c if rel & 1 else c
            cps.append(_remote(g_ref, all_ref.at[me], send_sems.at[rel - 1], recv_sems.at[rel - 1], (px, py, pc)))
        for cp in cps:
            cp.start()
        for cp in cps:
            cp.wait()
        total = all_ref[0]
        for dev in range(1, n_dev):
            total = total + all_ref[dev]
        gs_ref[...] = total
        delta, nm, nv = _adam(w_ref[...], total, m_ref[...], v_ref[...])
        dl_ref[...] = delta
        nm_ref[...] = nm
        nv_ref[...] = nv

    vm = pl.BlockSpec(memory_space=pltpu.VMEM)
    out = jax.ShapeDtypeStruct((r, d), F32)
    return pl.pallas_call(
        body, name="small_allreduce_adam", in_specs=[vm, vm, vm, vm], out_specs=[vm, vm, vm, vm],
        out_shape=[out, out, out, out],
        scratch_shapes=[pltpu.VMEM((n_dev, r, d), F32), pltpu.SemaphoreType.DMA((n_dev - 1,)),
                        pltpu.SemaphoreType.DMA((n_dev - 1,))],
    )(g_part, w, m, v)


def adam_update(name, w, m, v, g_buf, col_blk):
    _, r, c = w.shape
    tr = _tile(r, 128, 8)

    def kern(w_ref, m_ref, v_ref, g_ref, go_ref, dl_ref, nm_ref, nv_ref):
        g = g_ref[...]
        delta, nm, nv = _adam(w_ref[...], g, m_ref[...], v_ref[...])
        go_ref[...] = g
        dl_ref[...] = delta
        nm_ref[...] = nm
        nv_ref[...] = nv

    blk = pl.BlockSpec((None, tr, c), lambda i: (0, i, 0))
    out = jax.ShapeDtypeStruct((1, r, c), F32)
    return pl.pallas_call(
        kern, name=name, grid=(r // tr,),
        in_specs=[blk, blk, blk, pl.BlockSpec((tr, c), lambda i: (i, col_blk))],
        out_specs=[blk] * 4, out_shape=[out] * 4, compiler_params=_params("parallel"),
    )(w, m, v, g_buf)


def _col_pieces(slab_of, lo, hi, cw):
    out = []
    while lo < hi:
        k, a = divmod(lo, cw)
        b = min(cw, a + hi - lo)
        out.append(slab_of(k)[:, a:b])
        lo += b - a
    return out


def kernel(x, norm_mix_pre, norm_mix_post, w_in, b_forget, w_branch_sb, w_branch_fox, w_out, norm_ffn_pre, norm_ffn_post, w_ffn_gate, w_ffn_up, w_ffn_down, loss_target, m_norm_mix_pre, m_norm_mix_post, m_w_in, m_b_forget, m_w_branch_sb, m_w_branch_fox, m_w_out, m_norm_ffn_pre, m_norm_ffn_post, m_w_ffn_gate, m_w_ffn_up, m_w_ffn_down, v_norm_mix_pre, v_norm_mix_post, v_w_in, v_b_forget, v_w_branch_sb, v_w_branch_fox, v_w_out, v_norm_ffn_pre, v_norm_ffn_post, v_w_ffn_gate, v_w_ffn_up, v_w_ffn_down):
    s, d = x.shape[1], x.shape[2]
    n_heads = b_forget.shape[1]
    d_att = n_heads * HEAD_DIM
    c_in = w_in.shape[2]
    c_br = w_branch_sb.shape[2]
    c_gu = w_ffn_gate.shape[2]
    d_ff = c_gu * N_CHIPS
    d_in = c_in * N_CHIPS
    f_pad = 512
    n_qkv = 6 * d_att
    n_gf = 2 * d + f_pad
    core = lax.axis_index("c").astype(jnp.int32).reshape(1)
    chip = (2 * lax.axis_index("x") + lax.axis_index("y")).astype(jnp.int32).reshape(1)

    in_send, in_recv, in_bufs, _ = gather_start("gather_start_w_in", [cast_place("place_w_in", [w_in], chip)])
    ag_send, ag_recv, ag_bufs, ag_token = gather_start("gather_start_rest", [
        cast_place("place_branch", [w_branch_sb, w_branch_fox], chip),
        cast_place("place_out", [w_out], chip),
        cast_place("place_gate_up", [w_ffn_gate, w_ffn_up], chip),
        cast_place("place_down", [w_ffn_down], chip)])
    g_in, = gather_forward("forward_w_in", gather_wait("gather_wait_w_in", in_bufs, in_send, in_recv, ag_token))
    slab = lambda k: g_in[k]
    w_main = jnp.concatenate(
        _col_pieces(slab, 0, n_qkv, c_in) + _col_pieces(slab, n_qkv + n_heads, d_in, c_in)
        + _col_pieces(slab, n_qkv, n_qkv + n_heads, c_in) + [jnp.zeros((d, f_pad - n_heads), BF16)], axis=1)
    x2 = x[0]
    tgt = loss_target[0]
    b_pad = jnp.pad(b_forget, ((0, 0), (0, LANES - n_heads)))

    u = norm_in(x2, norm_mix_pre)
    qkv = mm(u, w_main, "nn", BF16, "proj_qkv", b_win=(0, n_qkv))
    gf = mm(u, w_main, "nn", F32, "proj_gates", b_win=(n_qkv, n_gf))
    cum = cum_fwd(gf, b_pad, 2 * d)
    cum_heads = cum[:, :n_heads].T
    cum_col, cum_row = cum_heads[:, :, None], cum_heads[:, None, :]
    o_sb = sb_fwd(qkv, n_heads, 0)
    o_fx, lse = fox_fwd(qkv, cum_col, cum_row, n_heads, 3 * n_heads)
    g_br, g_out, g_gu, g_dn = gather_forward(
        "forward_rest", gather_wait("gather_wait_rest", ag_bufs, ag_send, ag_recv, o_fx))
    w_o = g_out.reshape(d, d)
    w_dn = g_dn.reshape(d_ff, d)
    bsb = mm(o_sb, g_br, "nn", F32, "branch_sb", tn=c_br, chunks=(1, 0))
    bfx = mm(o_fx, g_br, "nn", F32, "branch_fox", tn=c_br, chunks=(1, 1))
    merged = gate_fwd(bsb, bfx, gf)
    mix = mm(merged, w_o, "nn", F32, "out_proj")
    h1, u2 = mid_fwd(x2, mix, norm_mix_post, norm_ffn_pre)
    gu = mm(u2, g_gu, "nn", F32, "ffn_gate_up", tn=c_gu, chunks=(2, 0))
    act = swiglu_fwd(gu, c_gu)
    ff = mm(act, w_dn, "nn", F32, "ffn_down")
    dy, d_ff_out, dg_fpost, loss_part = loss_head(h1, ff, norm_ffn_post, tgt)

    p_dn = mm(act, d_ff_out, "tn", BF16, "dw_ffn_down").reshape(N_CHIPS, d_ff // N_CHIPS, d)
    d_act = mm(d_ff_out, w_dn, "nt", F32, "d_act")
    d_gu = swiglu_bwd(d_act, gu, c_gu)
    p_gu = mm(u2, d_gu, "tn", BF16, "dw_ffn_gate_up", tn=c_gu, chunks=(2, 0),
              out_into=lax.empty((N_CHIPS, d, 2 * c_gu), BF16))
    du2 = mm(d_gu, g_gu, "nt", F32, "d_u2", tk=c_gu, chunks=(2, 0))
    dh1, d_mix, dg_fpre, dg_post = mid_bwd(dy, du2, h1, mix, norm_ffn_pre, norm_mix_post)
    p_out = mm(merged, d_mix, "tn", BF16, "dw_out").reshape(N_CHIPS, d // N_CHIPS, d)
    d_merged = mm(d_mix, w_o, "nt", F32, "d_merged")
    d_bsb, d_bfx, d_gs, d_gx = gate_bwd(d_merged, bsb, bfx, gf)
    p_br = mm(o_sb, d_bsb, "tn", BF16, "dw_branch_sb", tn=c_br, chunks=(1, 0),
              out_into=lax.empty((N_CHIPS, d_att, 2 * c_br), BF16))
    p_br = mm(o_fx, d_bfx, "tn", BF16, "dw_branch_fox", tn=c_br, chunks=(1, 1), out_into=p_br)
    d_osb = mm(d_bsb, g_br, "nt", BF16, "d_o_sb", tk=c_br, chunks=(1, 0))
    d_ofx = mm(d_bfx, g_br, "nt", BF16, "d_o_fox", tk=c_br, chunks=(1, 1))

    def reduce_start(tag, pieces, names):
        from_sibling = swap_halves("swap_halves_" + tag, pieces)
        sums = [pair_sum("pair_sum_" + t, p, q, core) for t, p, q in zip(names, pieces, from_sibling)]
        return scatter_start("scatter_start_" + tag, sums)

    def reduce_end(tag, started, names, after):
        send, recv, sums, lands, _ = started
        sums, lands = scatter_wait("scatter_wait_" + tag, sums, lands, send, recv, after)
        return join_halves("join_halves_" + tag, [chip_sum("chip_sum_" + t, sm, got, chip, core)
                                                  for t, sm, got in zip(names, sums, lands)])

    rest_names = ["branch", "out", "gate_up", "down"]
    rest_started = reduce_start("rest", [p_br, p_out, p_gu, p_dn], rest_names)
    d_osb = d_osb + rest_started[4][0, 0].astype(BF16)
    dq_s, dk_s, dv_s = sb_bwd(qkv, d_osb, n_heads, 0)
    dq_f, dk_f, dv_f, dcq, dck = fox_bwd(qkv, cum_col, cum_row, o_fx, d_ofx, lse, n_heads, 3 * n_heads)
    d_cum = jnp.pad((dcq[:, :, 0] + dck[:, 0, :]).T, ((0, 0), (0, LANES - n_heads)))
    d_f, db_pad = cum_bwd(d_cum, gf, b_pad, 2 * d, n_heads)
    d_main = jnp.concatenate(
        [dq_s, dk_s, dv_s, dq_f, dk_f, dv_f, d_gs, d_gx, d_f, jnp.zeros((s, f_pad - LANES), BF16)], axis=1)
    dw_main = mm(u, d_main, "tn", BF16, "dw_in")

    def main_cols(lo, hi):
        out = []
        for s0, s1, m0 in [(0, n_qkv, 0), (n_qkv, n_qkv + n_heads, n_qkv + 2 * d), (n_qkv + n_heads, d_in, n_qkv)]:
            a, b = max(lo, s0), min(hi, s1)
            if a < b:
                out.append(dw_main[:, m0 + a - s0:m0 + b - s0])
        return out

    p_in = jnp.stack([jnp.concatenate(main_cols(k * c_in, (k + 1) * c_in), axis=1) for k in range(N_CHIPS)])

    in_started = reduce_start("w_in", [p_in], ["in"])
    du = mm(d_main, w_main, "nt", F32, "d_u")
    dx, dg_pre = in_bwd(dh1, du, x2, norm_mix_pre + in_started[4][0:1, 0:1])
    gr_br, gr_out, gr_gu, gr_dn = reduce_end("rest", rest_started, rest_names, dx)

    upd_bs = adam_update("adam_branch_sb", w_branch_sb, m_w_branch_sb, v_w_branch_sb, gr_br, 0)
    upd_bf = adam_update("adam_branch_fox", w_branch_fox, m_w_branch_fox, v_w_branch_fox, gr_br, 1)
    upd_o = adam_update("adam_out", w_out, m_w_out, v_w_out, gr_out, 0)
    upd_ga = adam_update("adam_gate", w_ffn_gate, m_w_ffn_gate, v_w_ffn_gate, gr_gu, 0)
    upd_up = adam_update("adam_up", w_ffn_up, m_w_ffn_up, v_w_ffn_up, gr_gu, 1)
    upd_dn = adam_update("adam_down", w_ffn_down, m_w_ffn_down, v_w_ffn_down, gr_dn, 0)

    def pack(rows):
        rows = [jnp.pad(r_, ((0, 0), (0, d - r_.shape[1]))) for r_ in rows]
        return jnp.concatenate(rows + [jnp.zeros((8 - len(rows), d), F32)], axis=0)

    sm_g, sm_d, sm_m, sm_v = small_allreduce_adam(
        pack([dg_pre, dg_post, dg_fpre, dg_fpost, db_pad]),
        pack([norm_mix_pre, norm_mix_post, norm_ffn_pre, norm_ffn_post, b_forget]),
        pack([m_norm_mix_pre, m_norm_mix_post, m_norm_ffn_pre, m_norm_ffn_post, m_b_forget]),
        pack([v_norm_mix_pre, v_norm_mix_post, v_norm_ffn_pre, v_norm_ffn_post, v_b_forget]))

    done = sm_d[0:1, 0:1] + sum(u_[1][0, 0:1, 0:1] for u_ in (upd_bs, upd_bf, upd_o, upd_ga, upd_up, upd_dn))
    gr_in, = reduce_end("w_in", in_started, ["in"], done)
    upd_in = adam_update("adam_w_in", w_in, m_w_in, v_w_in, gr_in, 0)
    grads, deltas, new_ms, new_vs = zip(upd_in, upd_bs, upd_bf, upd_o, upd_ga, upd_up, upd_dn)

    def small(a):
        return [a[0:1], a[1:2], a[2:3], a[3:4], a[4:5, :n_heads]]

    def ordered(sm, bg):
        return [sm[0], sm[1], bg[0], sm[4], bg[1], bg[2], bg[3], sm[2], sm[3], bg[4], bg[5], bg[6]]

    loss = lax.psum(loss_part[0, 0], ("x", "y", "c"))
    return (loss, dx[None], *ordered(small(sm_g), grads), *ordered(small(sm_d), deltas),
            *ordered(small(sm_m), new_ms), *ordered(small(sm_v), new_vs))
```

```python
import functools

import jax
import jax.numpy as jnp
from jax import lax
from jax.experimental import pallas as pl
from jax.experimental.pallas import tpu as pltpu

F32 = jnp.float32
BF16 = jnp.bfloat16
MESH = pl.DeviceIdType.MESH

HEAD_DIM = 128
LANES = 128
ATT_TILE = 512
ROW_TILE = 256
N_CHIPS = 4
RMS_EPS = 1e-6
ADAM_LR = 0.001
ADAM_B1 = 0.9
ADAM_B2 = 0.999
ADAM_EPS = 1e-08
ADAM_WD = 0.01
ADAM_STEP = 10
NEG_BIG = -1e30
VMEM_LIMIT = 56 * 1024 * 1024
MM_VMEM_BUDGET = 40 * 1024 * 1024
ATT_STRIP = 512

NN = (((1,), (0,)), ((), ()))
NT = (((1,), (1,)), ((), ()))
TN = (((0,), (0,)), ((), ()))


def _tile(n, pref, align):
    best = None
    t = align
    while t <= min(n, pref):
        if n % t == 0:
            best = t
        t += align
    return n if best is None else best


def _params(*sem):
    return pltpu.CompilerParams(dimension_semantics=sem, vmem_limit_bytes=VMEM_LIMIT)


def _mm_tiles(m, n, k, a_bytes, b_bytes, out_bytes, tn, tk):
    tm = _tile(m, 2048, LANES)
    tk = tk or _tile(k, 512, LANES)

    def vmem(t):
        acc = 0 if out_bytes == 4 else tm * t * 4
        return acc + 2 * tm * t * out_bytes + 2 * (tm * tk * a_bytes + tk * t * b_bytes)

    if tn is None:
        fits = [t for t in range(LANES, min(n, 2048) + 1, LANES) if n % t == 0 and vmem(t) <= MM_VMEM_BUDGET]
        tn = max(fits) if fits else _tile(n, LANES, LANES)
    return tm, tn, tk


def mm(a, b, mode, out_dtype, name, *, tn=None, tk=None, b_win=None, chunks=None, out_into=None):
    n_per, blk0 = chunks if chunks else (1, 0)
    if mode == "nn":
        m, k = a.shape
        n = b.shape[0] * n_per * tn if chunks else (b_win[1] if b_win else b.shape[1])
    elif mode == "nt":
        m = a.shape[0]
        k = b.shape[0] * n_per * tk if chunks else a.shape[1]
        n = b.shape[-2]
    else:
        k, m = a.shape
        n = b.shape[1]
    in_place = jnp.dtype(out_dtype) == jnp.dtype(F32)
    tm, tn, tk = _mm_tiles(m, n, k, a.dtype.itemsize, b.dtype.itemsize, jnp.dtype(out_dtype).itemsize, tn, tk)
    assert m % tm == 0 and n % tn == 0 and k % tk == 0, (name, m, n, k, tm, tn, tk)
    j0 = 0
    if b_win:
        assert b_win[0] % tn == 0
        j0 = b_win[0] // tn
    nk = k // tk
    dims = {"nn": NN, "nt": NT, "tn": TN}[mode]

    def kern(a_ref, b_ref, *rest):
        o_ref, acc_ref = (rest[-1], rest[-1]) if in_place else (rest[-2], rest[-1])
        kk = pl.program_id(2)

        @pl.when(kk == 0)
        def _():
            acc_ref[...] = jnp.zeros_like(acc_ref)

        acc_ref[...] += lax.dot_general(a_ref[...].astype(BF16), b_ref[...].astype(BF16), dims,
                                        preferred_element_type=F32)

        if not in_place:
            @pl.when(kk == nk - 1)
            def _():
                o_ref[...] = acc_ref[...].astype(o_ref.dtype)

    out_spec = pl.BlockSpec((tm, tn), lambda i, j, kk: (i, j))
    out_shape = jax.ShapeDtypeStruct((m, n), out_dtype)
    if mode == "nn":
        a_spec = pl.BlockSpec((tm, tk), lambda i, j, kk: (i, kk))
        if chunks:
            b_spec = pl.BlockSpec((None, tk, tn), lambda i, j, kk: (j // n_per, kk, blk0 + j % n_per))
        else:
            b_spec = pl.BlockSpec((tk, tn), lambda i, j, kk: (kk, j + j0))
    elif mode == "nt":
        a_spec = pl.BlockSpec((tm, tk), lambda i, j, kk: (i, kk))
        if chunks:
            b_spec = pl.BlockSpec((None, tn, tk), lambda i, j, kk: (kk // n_per, j, blk0 + kk % n_per))
        else:
            b_spec = pl.BlockSpec((tn, tk), lambda i, j, kk: (j, kk))
    else:
        a_spec = pl.BlockSpec((tk, tm), lambda i, j, kk: (kk, i))
        b_spec = pl.BlockSpec((tk, tn), lambda i, j, kk: (kk, j))
        if chunks:
            out_spec = pl.BlockSpec((None, tm, tn), lambda i, j, kk: (j // n_per, i, blk0 + j % n_per))
    in_specs, operands, aliases = [a_spec, b_spec], [a, b], {}
    if chunks and mode == "tn":
        assert out_into is not None
        out_shape = jax.ShapeDtypeStruct(out_into.shape, out_dtype)
        in_specs.append(pl.BlockSpec(memory_space=pl.ANY))
        operands.append(out_into)
        aliases = {2: 0}
    return pl.pallas_call(
        kern, name=name, grid=(m // tm, n // tn, nk),
        in_specs=in_specs, out_specs=out_spec, out_shape=out_shape,
        scratch_shapes=[] if in_place else [pltpu.VMEM((tm, tn), F32)], input_output_aliases=aliases,
        compiler_params=_params("parallel", "parallel", "arbitrary"),
    )(*operands)


def _rstd(v):
    return lax.rsqrt(jnp.mean(v * v, axis=-1, keepdims=True) + RMS_EPS)


def _norm_bwd(v, g, dy):
    r = _rstd(v)
    vh = v * r
    dyg = dy * g
    dv = r * (dyg - vh * jnp.mean(dyg * vh, axis=-1, keepdims=True))
    return dv, jnp.sum(dy * vh, axis=0, keepdims=True)


def _row_call(kern, name, ins, outs, s, d):
    tr = _tile(s, ROW_TILE, 16)

    def spec(shape, is_row):
        if is_row:
            return pl.BlockSpec((tr, shape[1]), lambda i: (i, 0))
        return pl.BlockSpec(shape, lambda i: (0, 0))

    return pl.pallas_call(
        kern, name=name, grid=(s // tr,),
        in_specs=[spec(a.shape, r) for a, r in ins],
        out_specs=[spec(sh, r) for sh, _, r in outs],
        out_shape=[jax.ShapeDtypeStruct(sh, dt) for sh, dt, _ in outs],
        compiler_params=_params("arbitrary"),
    )(*[a for a, _ in ins])


def norm_in(x, g):
    s, d = x.shape

    def kern(x_ref, g_ref, u_ref):
        v = x_ref[...]
        u_ref[...] = (v * _rstd(v) * g_ref[...]).astype(BF16)

    return _row_call(kern, "norm_in", [(x, True), (g, False)], [((s, d), BF16, True)], s, d)[0]


def mid_fwd(x, mix, g_post, g_fpre):
    s, d = x.shape

    def kern(x_ref, mix_ref, gp_ref, gf_ref, h1_ref, u2_ref):
        mixv = mix_ref[...]
        h1 = x_ref[...] + mixv * _rstd(mixv) * gp_ref[...]
        h1_ref[...] = h1
        u2_ref[...] = (h1 * _rstd(h1) * gf_ref[...]).astype(BF16)

    return _row_call(kern, "mid_fwd", [(x, True), (mix, True), (g_post, False), (g_fpre, False)],
                     [((s, d), F32, True), ((s, d), BF16, True)], s, d)


def loss_head(h1, ff, g_fpost, target):
    s, d = h1.shape

    def kern(h1_ref, ff_ref, g_ref, t_ref, dy_ref, dff_ref, dg_ref, loss_ref):
        @pl.when(pl.program_id(0) == 0)
        def _():
            dg_ref[...] = jnp.zeros_like(dg_ref)
            loss_ref[...] = jnp.zeros_like(loss_ref)

        ffv = ff_ref[...]
        g = g_ref[...]
        y = h1_ref[...] + ffv * _rstd(ffv) * g
        diff = y - t_ref[...]
        row_loss = jnp.mean(diff * diff, axis=-1, keepdims=True)
        loss_ref[...] += 0.5 * jnp.sum(row_loss, axis=0, keepdims=True)
        dy = diff / d
        dy_ref[...] = dy
        dff, dg = _norm_bwd(ffv, g, dy)
        dff_ref[...] = dff.astype(BF16)
        dg_ref[...] += dg

    return _row_call(kern, "loss_head",
                     [(h1, True), (ff, True), (g_fpost, False), (target, True)],
                     [((s, d), F32, True), ((s, d), BF16, True), ((1, d), F32, False), ((1, 1), F32, False)], s, d)


def mid_bwd(dy, du2, h1, mix, g_fpre, g_post):
    s, d = dy.shape

    def kern(dy_ref, du2_ref, h1_ref, mix_ref, gf_ref, gp_ref, dh1_ref, dmix_ref, dgf_ref, dgp_ref):
        @pl.when(pl.program_id(0) == 0)
        def _():
            dgf_ref[...] = jnp.zeros_like(dgf_ref)
            dgp_ref[...] = jnp.zeros_like(dgp_ref)

        dh, dgf = _norm_bwd(h1_ref[...], gf_ref[...], du2_ref[...])
        dh1 = dy_ref[...] + dh
        dh1_ref[...] = dh1
        dmix, dgp = _norm_bwd(mix_ref[...], gp_ref[...], dh1)
        dmix_ref[...] = dmix.astype(BF16)
        dgf_ref[...] += dgf
        dgp_ref[...] += dgp

    return _row_call(kern, "mid_bwd",
                     [(dy, True), (du2, True), (h1, True), (mix, True), (g_fpre, False), (g_post, False)],
                     [((s, d), F32, True), ((s, d), BF16, True), ((1, d), F32, False), ((1, d), F32, False)], s, d)


def in_bwd(dh1, du, x, g_pre):
    s, d = x.shape

    def kern(dh1_ref, du_ref, x_ref, g_ref, dx_ref, dg_ref):
        @pl.when(pl.program_id(0) == 0)
        def _():
            dg_ref[...] = jnp.zeros_like(dg_ref)

        dxn, dg = _norm_bwd(x_ref[...], g_ref[...], du_ref[...])
        dx_ref[...] = dh1_ref[...] + dxn
        dg_ref[...] += dg

    return _row_call(kern, "in_bwd", [(dh1, True), (du, True), (x, True), (g_pre, False)],
                     [((s, d), F32, True), ((1, d), F32, False)], s, d)


def _sigmoid(v):
    return 1.0 / (1.0 + jnp.exp(-v))


def gate_fwd(bsb, bfx, gf):
    s, d = bsb.shape
    tr, tc = _tile(s, 256, 16), _tile(d, 512, LANES)
    nc = d // tc

    def kern(bsb_ref, bfx_ref, gs_ref, gx_ref, o_ref):
        o_ref[...] = (_sigmoid(gs_ref[...]) * bsb_ref[...] + _sigmoid(gx_ref[...]) * bfx_ref[...]).astype(BF16)

    blk = pl.BlockSpec((tr, tc), lambda i, j: (i, j))
    return pl.pallas_call(
        kern, name="gate_fwd", grid=(s // tr, nc),
        in_specs=[blk, blk, blk, pl.BlockSpec((tr, tc), lambda i, j: (i, j + nc))],
        out_specs=blk, out_shape=jax.ShapeDtypeStruct((s, d), BF16),
        compiler_params=_params("parallel", "parallel"),
    )(bsb, bfx, gf, gf)


def gate_bwd(dmerged, bsb, bfx, gf):
    s, d = bsb.shape
    tr, tc = _tile(s, 256, 16), _tile(d, 512, LANES)
    nc = d // tc

    def kern(dm_ref, bsb_ref, bfx_ref, gs_ref, gx_ref, dbs_ref, dbx_ref, dgs_ref, dgx_ref):
        dm = dm_ref[...]
        ss = _sigmoid(gs_ref[...])
        sx = _sigmoid(gx_ref[...])
        dbs_ref[...] = (dm * ss).astype(BF16)
        dbx_ref[...] = (dm * sx).astype(BF16)
        dgs_ref[...] = (dm * bsb_ref[...] * ss * (1.0 - ss)).astype(BF16)
        dgx_ref[...] = (dm * bfx_ref[...] * sx * (1.0 - sx)).astype(BF16)

    blk = pl.BlockSpec((tr, tc), lambda i, j: (i, j))
    out = jax.ShapeDtypeStruct((s, d), BF16)
    return pl.pallas_call(
        kern, name="gate_bwd", grid=(s // tr, nc),
        in_specs=[blk, blk, blk, blk, pl.BlockSpec((tr, tc), lambda i, j: (i, j + nc))],
        out_specs=[blk, blk, blk, blk], out_shape=[out, out, out, out],
        compiler_params=_params("parallel", "parallel"),
    )(dmerged, bsb, bfx, gf, gf)


def swiglu_fwd(gu, cw):
    s, f2 = gu.shape
    tr = _tile(s, 256, 16)

    def kern(gu_ref, o_ref):
        g = gu_ref[:, :cw]
        o_ref[...] = (g * _sigmoid(g) * gu_ref[:, cw:]).astype(BF16)

    return pl.pallas_call(
        kern, name="swiglu_fwd", grid=(s // tr, f2 // (2 * cw)),
        in_specs=[pl.BlockSpec((tr, 2 * cw), lambda i, j: (i, j))],
        out_specs=pl.BlockSpec((tr, cw), lambda i, j: (i, j)),
        out_shape=jax.ShapeDtypeStruct((s, f2 // 2), BF16),
        compiler_params=_params("parallel", "parallel"),
    )(gu)


def swiglu_bwd(dact, gu, cw):
    s, f2 = gu.shape
    tr = _tile(s, 256, 16)

    def kern(da_ref, gu_ref, o_ref):
        da = da_ref[...]
        g = gu_ref[:, :cw]
        sg = _sigmoid(g)
        o_ref[:, :cw] = (da * gu_ref[:, cw:] * (sg * (1.0 + g * (1.0 - sg)))).astype(BF16)
        o_ref[:, cw:] = (da * (g * sg)).astype(BF16)

    return pl.pallas_call(
        kern, name="swiglu_bwd", grid=(s // tr, f2 // (2 * cw)),
        in_specs=[pl.BlockSpec((tr, cw), lambda i, j: (i, j)), pl.BlockSpec((tr, 2 * cw), lambda i, j: (i, j))],
        out_specs=pl.BlockSpec((tr, 2 * cw), lambda i, j: (i, j)),
        out_shape=jax.ShapeDtypeStruct((s, f2), BF16),
        compiler_params=_params("parallel", "parallel"),
    )(dact, gu)


def _split3(v):
    hi = v.astype(BF16)
    r = v - hi.astype(F32)
    mid = r.astype(BF16)
    lo = (r - mid.astype(F32)).astype(BF16)
    return hi, mid, lo


def _dot3_right(v, ones):
    hi, mid, lo = _split3(v)
    d = lambda p: jnp.dot(p, ones, preferred_element_type=F32)
    return (d(lo) + d(mid)) + d(hi)


def _dot3_left(ones, v):
    hi, mid, lo = _split3(v)
    d = lambda p: jnp.dot(ones, p, preferred_element_type=F32)
    return (d(lo) + d(mid)) + d(hi)


def _split2(v):
    hi = v.astype(BF16)
    return hi, (v - hi.astype(F32)).astype(BF16)


def _dot2_right(v, ones):
    hi, lo = _split2(v)
    return jnp.dot(lo, ones, preferred_element_type=F32) + jnp.dot(hi, ones, preferred_element_type=F32)


def _log1p_exp_neg_abs(v):
    return jnp.log(1.0 + jnp.exp(-jnp.abs(v)))


def _mask01(cond):
    return jnp.where(cond, 1.0, 0.0).astype(BF16)


def _iota2(t):
    return (lax.broadcasted_iota(jnp.int32, (t, t), 0), lax.broadcasted_iota(jnp.int32, (t, t), 1))


def cum_fwd(gf, b_pad, f_col0):
    s = gf.shape[0]
    t = _tile(s, ATT_TILE, LANES)
    fb = f_col0 // LANES

    def kern(f_ref, b_ref, cum_ref, carry_ref):
        @pl.when(pl.program_id(0) == 0)
        def _():
            carry_ref[...] = jnp.zeros_like(carry_ref)

        v = f_ref[...] + b_ref[...]
        lf = jnp.minimum(v, 0.0) - _log1p_exp_neg_abs(v)
        row, col = _iota2(t)
        cum = _dot3_left(_mask01(col <= row), lf) + carry_ref[...]
        cum_ref[...] = cum
        carry_ref[...] = cum[t - 1:t, :]

    return pl.pallas_call(
        kern, name="cum_fwd", grid=(s // t,),
        in_specs=[pl.BlockSpec((t, LANES), lambda i: (i, fb)), pl.BlockSpec((1, LANES), lambda i: (0, 0))],
        out_specs=pl.BlockSpec((t, LANES), lambda i: (i, 0)),
        out_shape=jax.ShapeDtypeStruct((s, LANES), F32),
        scratch_shapes=[pltpu.VMEM((1, LANES), F32)],
        compiler_params=_params("arbitrary"),
    )(gf, b_pad)


def cum_bwd(dcum, gf, b_pad, f_col0, n_heads):
    s = gf.shape[0]
    t = _tile(s, ATT_TILE, LANES)
    nb = s // t
    fb = f_col0 // LANES

    def kern(dc_ref, f_ref, b_ref, df_ref, db_ref, carry_ref):
        @pl.when(pl.program_id(0) == 0)
        def _():
            carry_ref[...] = jnp.zeros_like(carry_ref)
            db_ref[...] = jnp.zeros_like(db_ref)

        row, col = _iota2(t)
        dlf = _dot3_left(_mask01(col >= row), dc_ref[...]) + carry_ref[...]
        carry_ref[...] = dlf[0:1, :]
        v = f_ref[...] + b_ref[...]
        sig_neg = jnp.exp(-jnp.maximum(v, 0.0) - _log1p_exp_neg_abs(v))
        lane = lax.broadcasted_iota(jnp.int32, (t, LANES), 1)
        df = jnp.where(lane < n_heads, dlf * sig_neg, 0.0)
        df_ref[...] = df.astype(BF16)
        db_ref[...] += jnp.sum(df, axis=0, keepdims=True)

    return pl.pallas_call(
        kern, name="cum_bwd", grid=(nb,),
        in_specs=[pl.BlockSpec((t, LANES), lambda i: (nb - 1 - i, 0)),
                  pl.BlockSpec((t, LANES), lambda i: (nb - 1 - i, fb)),
                  pl.BlockSpec((1, LANES), lambda i: (0, 0))],
        out_specs=[pl.BlockSpec((t, LANES), lambda i: (nb - 1 - i, 0)), pl.BlockSpec((1, LANES), lambda i: (0, 0))],
        out_shape=[jax.ShapeDtypeStruct((s, LANES), BF16), jax.ShapeDtypeStruct((1, LANES), F32)],
        scratch_shapes=[pltpu.VMEM((1, LANES), F32)],
        compiler_params=_params("arbitrary"),
    )(dcum, gf, b_pad)


def _qkv_specs(s, t, n_heads, base):
    return [pl.BlockSpec((t, HEAD_DIM), lambda h, i: (i, base + h)),
            pl.BlockSpec((s, HEAD_DIM), lambda h, i: (0, base + n_heads + h)),
            pl.BlockSpec((s, HEAD_DIM), lambda h, i: (0, base + 2 * n_heads + h))]


def _strips(t):
    sr = _tile(t, ATT_STRIP, 8)
    return sr, t // sr, [slice(si * sr, (si + 1) * sr) for si in range(t // sr)]


def _key_minus_row(sr, t):
    return lax.broadcasted_iota(jnp.int32, (sr, t), 1) - lax.broadcasted_iota(jnp.int32, (sr, t), 0)


def _sb_scores(q, k, diff, lim):
    z = lax.dot_general(q, k, NT, preferred_element_type=F32) * (HEAD_DIM ** -0.5)
    valid = diff < lim
    l1p = _log1p_exp_neg_abs(z)
    log_keep = jnp.where(valid, -jnp.maximum(z, 0.0) - l1p, 0.0)
    return z, valid, l1p, log_keep


def sb_fwd(qkv, n_heads, base):
    s = qkv.shape[0]
    t = _tile(s, ATT_TILE, LANES)
    sr, ns, strips = _strips(t)

    def kern(q_ref, k_ref, v_ref, o_ref):
        i = pl.program_id(1)
        row, col = _iota2(t)
        after = _mask01(row > col)
        diff = _key_minus_row(sr, t)
        qs = [q_ref[sl, :] for sl in strips]

        def body(jj, carry):
            runs, accs = carry
            j = i - jj
            off = pl.multiple_of(j * t, t)
            k = k_ref[pl.ds(off, t), :]
            v = v_ref[pl.ds(off, t), :]
            lim = jnp.where(j < i, t, 0)
            new_runs, new_accs = [], []
            for si in range(ns):
                z, valid, l1p, log_keep = _sb_scores(qs[si], k, diff, lim + si * sr)
                between = _dot2_right(log_keep, after) + runs[si]
                w = jnp.where(valid, jnp.exp(jnp.minimum(z, 0.0) - l1p + between), 0.0)
                new_accs.append(accs[si] + jnp.dot(w.astype(BF16), v, preferred_element_type=F32))
                new_runs.append(runs[si] + jnp.sum(log_keep, axis=1, keepdims=True))
            return tuple(new_runs), tuple(new_accs)

        init = (tuple(jnp.zeros((sr, 1), F32) for _ in strips), tuple(jnp.zeros((sr, HEAD_DIM), F32) for _ in strips))
        _, accs = lax.fori_loop(0, i + 1, body, init)
        for sl, acc in zip(strips, accs):
            o_ref[sl, :] = acc.astype(o_ref.dtype)

    return pl.pallas_call(
        kern, name="sb_fwd", grid=(n_heads, s // t),
        in_specs=_qkv_specs(s, t, n_heads, base),
        out_specs=pl.BlockSpec((t, HEAD_DIM), lambda h, i: (i, h)),
        out_shape=jax.ShapeDtypeStruct((s, n_heads * HEAD_DIM), BF16),
        compiler_params=_params("parallel", "arbitrary"),
    )(qkv, qkv, qkv)


def sb_bwd(qkv, d_o, n_heads, base):
    s = qkv.shape[0]
    t = _tile(s, ATT_TILE, LANES)
    nq = s // t
    sr, ns, strips = _strips(t)
    scale = HEAD_DIM ** -0.5

    def kern(q_ref, k_ref, v_ref, do_ref, dq_ref, dk_ref, dv_ref, dk_acc, dv_acc, run_ref):
        i = pl.program_id(1)

        @pl.when(i == 0)
        def _():
            dk_acc[...] = jnp.zeros_like(dk_acc)
            dv_acc[...] = jnp.zeros_like(dv_acc)

        row, col = _iota2(t)
        after = _mask01(row > col)
        before = _mask01(row < col)
        diff = _key_minus_row(sr, t)
        qs = [q_ref[sl, :] for sl in strips]
        dos = [do_ref[sl, :] for sl in strips]

        def sweep1(jj, runs):
            j = i - jj
            k = k_ref[pl.ds(pl.multiple_of(j * t, t), t), :]
            lim = jnp.where(j < i, t, 0)
            new_runs = []
            for si, sl in enumerate(strips):
                _, _, _, log_keep = _sb_scores(qs[si], k, diff, lim + si * sr)
                run_ref[j, sl, :] = runs[si]
                new_runs.append(runs[si] + jnp.sum(log_keep, axis=1, keepdims=True))
            return tuple(new_runs)

        lax.fori_loop(0, i + 1, sweep1, tuple(jnp.zeros((sr, 1), F32) for _ in strips))

        def sweep2(j, carry):
            run_es, dqs = carry
            off = pl.multiple_of(j * t, t)
            k = k_ref[pl.ds(off, t), :]
            v = v_ref[pl.ds(off, t), :]
            lim = jnp.where(j < i, t, 0)
            new_es, new_dqs = [], []
            dk_t = jnp.zeros((t, HEAD_DIM), F32)
            dv_t = jnp.zeros((t, HEAD_DIM), F32)
            for si, sl in enumerate(strips):
                z, valid, l1p, log_keep = _sb_scores(qs[si], k, diff, lim + si * sr)
                between = _dot2_right(log_keep, after) + run_ref[j, sl, :]
                w = jnp.where(valid, jnp.exp(jnp.minimum(z, 0.0) - l1p + between), 0.0)
                dw = lax.dot_general(dos[si], v, NT, preferred_element_type=F32)
                e = dw * w
                e_before = _dot2_right(e, before) + run_es[si]
                keep = jnp.exp(log_keep)
                dz = jnp.where(valid, e * keep - e_before * (1.0 - keep), 0.0) * scale
                dzb = dz.astype(BF16)
                new_dqs.append(dqs[si] + jnp.dot(dzb, k, preferred_element_type=F32))
                dk_t = dk_t + lax.dot_general(dzb, qs[si], TN, preferred_element_type=F32)
                dv_t = dv_t + lax.dot_general(w.astype(BF16), dos[si], TN, preferred_element_type=F32)
                new_es.append(run_es[si] + jnp.sum(e, axis=1, keepdims=True))
            dk_acc[pl.ds(off, t), :] += dk_t
            dv_acc[pl.ds(off, t), :] += dv_t
            return tuple(new_es), tuple(new_dqs)

        init = (tuple(jnp.zeros((sr, 1), F32) for _ in strips), tuple(jnp.zeros((sr, HEAD_DIM), F32) for _ in strips))
        _, dqs = lax.fori_loop(0, i + 1, sweep2, init)
        for sl, dq in zip(strips, dqs):
            dq_ref[sl, :] = dq.astype(BF16)

        @pl.when(i == nq - 1)
        def _():
            dk_ref[...] = dk_acc[...].astype(BF16)
            dv_ref[...] = dv_acc[...].astype(BF16)

    out = jax.ShapeDtypeStruct((s, n_heads * HEAD_DIM), BF16)
    head_blk = pl.BlockSpec((s, HEAD_DIM), lambda h, i: (0, h))
    tile_blk = pl.BlockSpec((t, HEAD_DIM), lambda h, i: (i, h))
    return pl.pallas_call(
        kern, name="sb_bwd", grid=(n_heads, nq),
        in_specs=_qkv_specs(s, t, n_heads, base) + [tile_blk],
        out_specs=[tile_blk, head_blk, head_blk],
        out_shape=[out, out, out],
        scratch_shapes=[pltpu.VMEM((s, HEAD_DIM), F32), pltpu.VMEM((s, HEAD_DIM), F32), pltpu.VMEM((nq, t, 1), F32)],
        compiler_params=_params("parallel", "arbitrary"),
    )(qkv, qkv, qkv, d_o)


def _fox_scores(q, k, cq, ck, diff, lim):
    sc = lax.dot_general(q, k, NT, preferred_element_type=F32) * (HEAD_DIM ** -0.5)
    sc = sc + cq - ck
    valid = diff < lim
    return jnp.where(valid, sc, NEG_BIG), valid


def fox_fwd(qkv, cum_col, cum_row, n_heads, base):
    s = qkv.shape[0]
    t = _tile(s, ATT_TILE, LANES)
    sr, ns, strips = _strips(t)

    def kern(q_ref, k_ref, v_ref, cq_ref, ck_ref, o_ref, lse_ref):
        i = pl.program_id(1)
        diff = _key_minus_row(sr, t)
        qs = [q_ref[sl, :] for sl in strips]
        cqs = [cq_ref[0, sl, :] for sl in strips]

        def body(j, carry):
            off = pl.multiple_of(j * t, t)
            k = k_ref[pl.ds(off, t), :]
            v = v_ref[pl.ds(off, t), :]
            ck = ck_ref[0, :, pl.ds(off, t)]
            lim = jnp.where(j < i, t, 1)
            out = []
            for si in range(ns):
                m, l, acc = carry[si]
                sc, _ = _fox_scores(qs[si], k, cqs[si], ck, diff, lim + si * sr)
                m_new = jnp.maximum(m, jnp.max(sc, axis=1, keepdims=True))
                p = jnp.exp(sc - m_new)
                alpha = jnp.exp(m - m_new)
                l = alpha * l + jnp.sum(p, axis=1, keepdims=True)
                acc = alpha * acc + jnp.dot(p.astype(BF16), v, preferred_element_type=F32)
                out.append((m_new, l, acc))
            return tuple(out)

        init = tuple((jnp.full((sr, 1), NEG_BIG, F32), jnp.zeros((sr, 1), F32), jnp.zeros((sr, HEAD_DIM), F32))
                     for _ in strips)
        res = lax.fori_loop(0, i + 1, body, init)
        for sl, (m, l, acc) in zip(strips, res):
            o_ref[sl, :] = acc / l
            lse_ref[0, sl, :] = m + jnp.log(l)

    col_blk = pl.BlockSpec((1, t, 1), lambda h, i: (h, i, 0))
    return pl.pallas_call(
        kern, name="fox_fwd", grid=(n_heads, s // t),
        in_specs=_qkv_specs(s, t, n_heads, base) + [col_blk, pl.BlockSpec((1, 1, s), lambda h, i: (h, 0, 0))],
        out_specs=[pl.BlockSpec((t, HEAD_DIM), lambda h, i: (i, h)), col_blk],
        out_shape=[jax.ShapeDtypeStruct((s, n_heads * HEAD_DIM), F32), jax.ShapeDtypeStruct((n_heads, s, 1), F32)],
        compiler_params=_params("parallel", "arbitrary"),
    )(qkv, qkv, qkv, cum_col, cum_row)


def fox_bwd(qkv, cum_col, cum_row, o, d_o, lse, n_heads, base):
    s = qkv.shape[0]
    t = _tile(s, ATT_TILE, LANES)
    nq = s // t
    sr, ns, strips = _strips(t)
    scale = HEAD_DIM ** -0.5

    def kern(q_ref, k_ref, v_ref, cq_ref, ck_ref, o_ref, do_ref, lse_ref,
             dq_ref, dk_ref, dv_ref, dcq_ref, dck_ref, dk_acc, dv_acc, dck_acc):
        i = pl.program_id(1)

        @pl.when(i == 0)
        def _():
            dk_acc[...] = jnp.zeros_like(dk_acc)
            dv_acc[...] = jnp.zeros_like(dv_acc)
            dck_acc[...] = jnp.zeros_like(dck_acc)

        diff = _key_minus_row(sr, t)
        qs = [q_ref[sl, :] for sl in strips]
        dos = [do_ref[sl, :] for sl in strips]
        cqs = [cq_ref[0, sl, :] for sl in strips]
        lses = [lse_ref[0, sl, :] for sl in strips]
        deltas = [jnp.sum(dos[si].astype(F32) * o_ref[sl, :], axis=1, keepdims=True) for si, sl in enumerate(strips)]

        def body(j, carry):
            off = pl.multiple_of(j * t, t)
            k = k_ref[pl.ds(off, t), :]
            v = v_ref[pl.ds(off, t), :]
            ck = ck_ref[0, :, pl.ds(off, t)]
            lim = jnp.where(j < i, t, 1)
            out = []
            dk_t = jnp.zeros((t, HEAD_DIM), F32)
            dv_t = jnp.zeros((t, HEAD_DIM), F32)
            dck_t = jnp.zeros((1, t), F32)
            for si in range(ns):
                dq, dcq = carry[si]
                sc, valid = _fox_scores(qs[si], k, cqs[si], ck, diff, lim + si * sr)
                p = jnp.where(valid, jnp.exp(sc - lses[si]), 0.0)
                dp = lax.dot_general(dos[si], v, NT, preferred_element_type=F32)
                ds = p * (dp - deltas[si])
                dsb = (ds * scale).astype(BF16)
                dq = dq + jnp.dot(dsb, k, preferred_element_type=F32)
                dk_t = dk_t + lax.dot_general(dsb, qs[si], TN, preferred_element_type=F32)
                dv_t = dv_t + lax.dot_general(p.astype(BF16), dos[si], TN, preferred_element_type=F32)
                dck_t = dck_t + jnp.sum(ds, axis=0, keepdims=True)
                out.append((dq, dcq + jnp.sum(ds, axis=1, keepdims=True)))
            dk_acc[pl.ds(off, t), :] += dk_t
            dv_acc[pl.ds(off, t), :] += dv_t
            dck_acc[:, pl.ds(off, t)] -= dck_t
            return tuple(out)

        init = tuple((jnp.zeros((sr, HEAD_DIM), F32), jnp.zeros((sr, 1), F32)) for _ in strips)
        res = lax.fori_loop(0, i + 1, body, init)
        for sl, (dq, dcq) in zip(strips, res):
            dq_ref[sl, :] = dq.astype(BF16)
            dcq_ref[0, sl, :] = dcq

        @pl.when(i == nq - 1)
        def _():
            dk_ref[...] = dk_acc[...].astype(BF16)
            dv_ref[...] = dv_acc[...].astype(BF16)
            dck_ref[0] = dck_acc[...]

    out = jax.ShapeDtypeStruct((s, n_heads * HEAD_DIM), BF16)
    head_blk = pl.BlockSpec((s, HEAD_DIM), lambda h, i: (0, h))
    tile_blk = pl.BlockSpec((t, HEAD_DIM), lambda h, i: (i, h))
    col_blk = pl.BlockSpec((1, t, 1), lambda h, i: (h, i, 0))
    row_blk = pl.BlockSpec((1, 1, s), lambda h, i: (h, 0, 0))
    return pl.pallas_call(
        kern, name="fox_bwd", grid=(n_heads, nq),
        in_specs=_qkv_specs(s, t, n_heads, base) + [col_blk, row_blk, tile_blk, tile_blk, col_blk],
        out_specs=[tile_blk, head_blk, head_blk, col_blk, row_blk],
        out_shape=[out, out, out, jax.ShapeDtypeStruct((n_heads, s, 1), F32),
                   jax.ShapeDtypeStruct((n_heads, 1, s), F32)],
        scratch_shapes=[pltpu.VMEM((s, HEAD_DIM), F32), pltpu.VMEM((s, HEAD_DIM), F32), pltpu.VMEM((1, s), F32)],
        compiler_params=_params("parallel", "arbitrary"),
    )(qkv, qkv, qkv, cum_col, cum_row, o, d_o, lse)


def _place():
    x, y, c = lax.axis_index("x"), lax.axis_index("y"), lax.axis_index("c")
    other_chips = [(1 - x, y), (x, 1 - y), (1 - x, 1 - y)]
    return x, y, c, other_chips


ANY = pl.BlockSpec(memory_space=pl.ANY)


def _remote(src, dst, send_sem, recv_sem, dev):
    return pltpu.make_async_remote_copy(src_ref=src, dst_ref=dst, send_sem=send_sem, recv_sem=recv_sem,
                                        device_id=dev, device_id_type=MESH)


def cast_place(name, ws, chip):
    r = ws[0].shape[1]
    cs = [w.shape[2] for w in ws]
    tr = _tile(r, 256, 16)

    def kern(chip_ref, *refs):
        o_ref = refs[-1]
        off = 0
        for w_ref, c in zip(refs[:-1], cs):
            o_ref[:, off:off + c] = w_ref[...].astype(BF16)
            off += c

    return pl.pallas_call(
        kern, name=name,
        grid_spec=pltpu.PrefetchScalarGridSpec(
            num_scalar_prefetch=1, grid=(r // tr,),
            in_specs=[pl.BlockSpec((None, tr, c), lambda i, chip_ref: (0, i, 0)) for c in cs],
            out_specs=pl.BlockSpec((None, tr, sum(cs)), lambda i, chip_ref: (chip_ref[0], i, 0))),
        out_shape=jax.ShapeDtypeStruct((N_CHIPS, r, sum(cs)), BF16),
        compiler_params=_params("parallel"),
    )(chip, *ws)


HBM = pl.BlockSpec(memory_space=pltpu.HBM)
SEM = pl.BlockSpec(memory_space=pltpu.SEMAPHORE)
SPLIT = pltpu.CompilerParams(has_side_effects=pltpu.SideEffectType.DATAFLOW_SIDE_EFFECTING)


def _in_hbm(a):
    return pltpu.with_memory_space_constraint(a, pltpu.HBM)


def _slab_rows(ref, k, core):
    half = ref.shape[1] // 2
    return ref.at[k, pl.ds(pl.multiple_of(core * half, 16), half)]


def gather_start(name, bufs):
    n = len(bufs)

    def body(*refs):
        ins, send, recv, token = refs[:n], refs[n], refs[n + 1], refs[-1]
        x, y, c, chips = _place()
        me = 2 * x + y
        for a in range(n):
            for j in range(3):
                rows = _slab_rows(ins[a], me, c)
                _remote(rows, rows, send.at[3 * a + j], recv.at[3 * a + j], (chips[j][0], chips[j][1], c)).start()
        token[...] = jnp.zeros_like(token)

    sem = pltpu.SemaphoreType.DMA((3 * n,))
    res = pl.pallas_call(
        body, name=name, in_specs=[HBM] * n, out_specs=[SEM, SEM] + [HBM] * n + [pl.BlockSpec(memory_space=pltpu.VMEM)],
        out_shape=[sem, sem] + [pltpu.HBM(b.shape, b.dtype) for b in bufs] + [jax.ShapeDtypeStruct((8, LANES), F32)],
        input_output_aliases={a: 2 + a for a in range(n)}, compiler_params=SPLIT,
    )(*[_in_hbm(b) for b in bufs])
    return res[0], res[1], res[2:2 + n], res[-1]


def gather_wait(name, bufs, send_sems, recv_sems, after):
    n = len(bufs)

    def body(*refs):
        ins, send, recv = refs[:n], refs[n], refs[n + 1]
        x, y, c, chips = _place()
        me = 2 * x + y
        for a in range(n):
            for j in range(3):
                dev = (chips[j][0], chips[j][1], c)
                mine = _slab_rows(ins[a], me, c)
                _remote(mine, mine, send.at[3 * a + j], recv.at[3 * a + j], dev).wait_send()
                land = _slab_rows(ins[a], 2 * chips[j][0] + chips[j][1], c)
                _remote(land, land, send.at[3 * a + j], recv.at[3 * a + j], dev).wait_recv()

    return pl.pallas_call(
        body, name=name, in_specs=[HBM] * n + [SEM, SEM] + [ANY] * len(after), out_specs=[HBM] * n,
        out_shape=[pltpu.HBM(b.shape, b.dtype) for b in bufs],
        input_output_aliases={a: a for a in range(n)}, compiler_params=SPLIT,
    )(*bufs, send_sems, recv_sems, *after)


def gather_forward(name, bufs):
    n = len(bufs)

    def body(*refs):
        outs = refs[n:2 * n]
        send_sems, recv_sems = refs[2 * n:]
        x, y, c, chips = _place()
        sibling = (x, y, 1 - c)

        def d2d(a, j, core):
            rows = _slab_rows(outs[a], 2 * chips[j][0] + chips[j][1], core)
            return _remote(rows, rows, send_sems.at[3 * a + j], recv_sems.at[3 * a + j], sibling)

        pairs = [(a, j) for a in range(n) for j in range(3)]
        for a, j in pairs:
            d2d(a, j, c).start()
        for a, j in pairs:
            d2d(a, j, 1 - c).wait_recv()
        for a, j in pairs:
            d2d(a, j, c).wait_send()

    return pl.pallas_call(
        body, name=name, in_specs=[ANY] * n, out_specs=[ANY] * n,
        out_shape=[jax.ShapeDtypeStruct(b.shape, b.dtype) for b in bufs],
        input_output_aliases={a: a for a in range(n)},
        scratch_shapes=[pltpu.SemaphoreType.DMA((3 * n,)), pltpu.SemaphoreType.DMA((3 * n,))],
    )(*bufs)


def swap_halves(name, pieces):
    n = len(pieces)
    halves = [p.shape[1] // 2 for p in pieces]

    def body(*refs):
        ins, outs = refs[:n], refs[n:2 * n]
        send_sems, recv_sems = refs[2 * n:]
        x, y, c, _ = _place()
        cps = [_remote(ins[a].at[:, pl.ds(pl.multiple_of((1 - c) * halves[a], 16), halves[a]), :], outs[a],
                       send_sems.at[a], recv_sems.at[a], (x, y, 1 - c)) for a in range(n)]
        for cp in cps:
            cp.start()
        for cp in cps:
            cp.wait()

    return pl.pallas_call(
        body, name=name, in_specs=[ANY] * n, out_specs=[ANY] * n,
        out_shape=[jax.ShapeDtypeStruct((N_CHIPS, h, p.shape[2]), p.dtype) for p, h in zip(pieces, halves)],
        scratch_shapes=[pltpu.SemaphoreType.DMA((n,)), pltpu.SemaphoreType.DMA((n,))],
    )(*pieces)


def pair_sum(name, pieces, got, core):
    _, r, w = pieces.shape
    half = r // 2
    tr = _tile(half, 256, 16)

    def kern(core_ref, p_ref, g_ref, o_ref):
        o_ref[...] = (p_ref[...].astype(F32) + g_ref[...].astype(F32)).astype(o_ref.dtype)

    return pl.pallas_call(
        kern, name=name,
        grid_spec=pltpu.PrefetchScalarGridSpec(
            num_scalar_prefetch=1, grid=(N_CHIPS, half // tr),
            in_specs=[pl.BlockSpec((None, None, tr, w), lambda k, i, core_ref: (k, core_ref[0], i, 0)),
                      pl.BlockSpec((None, tr, w), lambda k, i, core_ref: (k, i, 0))],
            out_specs=pl.BlockSpec((None, tr, w), lambda k, i, core_ref: (k, i, 0))),
        out_shape=jax.ShapeDtypeStruct((N_CHIPS, half, w), pieces.dtype),
        compiler_params=_params("parallel", "parallel"),
    )(core, pieces.reshape(N_CHIPS, 2, half, w), got)


def _scatter_copies(sums, lands, send, recv):
    x, y, c, chips = _place()
    return [_remote(sums[a].at[2 * chips[j][0] + chips[j][1]], lands[a].at[j], send.at[3 * a + j], recv.at[3 * a + j],
                    (chips[j][0], chips[j][1], c)) for a in range(len(sums)) for j in range(3)]


def scatter_start(name, sums):
    n = len(sums)
    lands = [lax.empty((3,) + t.shape[1:], t.dtype) for t in sums]

    def body(*refs):
        ins, land_in, send, recv, token = refs[:n], refs[n:2 * n], refs[2 * n], refs[2 * n + 1], refs[-1]
        for cp in _scatter_copies(ins, land_in, send, recv):
            cp.start()
        token[...] = jnp.zeros_like(token)

    sem = pltpu.SemaphoreType.DMA((3 * n,))
    res = pl.pallas_call(
        body, name=name, in_specs=[HBM] * (2 * n),
        out_specs=[SEM, SEM] + [HBM] * (2 * n) + [pl.BlockSpec(memory_space=pltpu.VMEM)],
        out_shape=[sem, sem] + [pltpu.HBM(t.shape, t.dtype) for t in sums + lands] + [jax.ShapeDtypeStruct((8, LANES), F32)],
        input_output_aliases={a: 2 + a for a in range(2 * n)}, compiler_params=SPLIT,
    )(*[_in_hbm(t) for t in sums + lands])
    return res[0], res[1], res[2:2 + n], res[2 + n:2 + 2 * n], res[-1]


def scatter_wait(name, sums, lands, send_sems, recv_sems, after):
    n = len(sums)

    def body(*refs):
        ins, land_in, send, recv = refs[:n], refs[n:2 * n], refs[2 * n], refs[2 * n + 1]
        for cp in _scatter_copies(ins, land_in, send, recv):
            cp.wait_send()
            cp.wait_recv()

    res = pl.pallas_call(
        body, name=name, in_specs=[HBM] * (2 * n) + [SEM, SEM, ANY], out_specs=[HBM] * (2 * n),
        out_shape=[pltpu.HBM(t.shape, t.dtype) for t in sums + lands],
        input_output_aliases={a: a for a in range(2 * n)}, compiler_params=SPLIT,
    )(*sums, *lands, send_sems, recv_sems, after)
    return res[:n], res[n:]


def chip_sum(name, sums, got, chip, core):
    _, half, w = sums.shape
    tr = _tile(half, 256, 16)
    nb = half // tr

    def kern(ids_ref, s_ref, g0_ref, g1_ref, g2_ref, o_ref):
        o_ref[...] = ((s_ref[...].astype(F32) + g0_ref[...].astype(F32)) + g1_ref[...].astype(F32)) \
            + g2_ref[...].astype(F32)

    def got_spec(j):
        return pl.BlockSpec((None, tr, w), lambda i, ids_ref: (j, i, 0))

    return pl.pallas_call(
        kern, name=name,
        grid_spec=pltpu.PrefetchScalarGridSpec(
            num_scalar_prefetch=1, grid=(nb,),
            in_specs=[pl.BlockSpec((None, tr, w), lambda i, ids_ref: (ids_ref[0], i, 0)),
                      got_spec(0), got_spec(1), got_spec(2)],
            out_specs=pl.BlockSpec((tr, w), lambda i, ids_ref: (ids_ref[1] * nb + i, 0))),
        out_shape=jax.ShapeDtypeStruct((2 * half, w), F32),
        compiler_params=_params("parallel"),
    )(jnp.concatenate([chip, core]), sums, got, got, got)


def join_halves(name, shards):
    n = len(shards)
    halves = [g.shape[0] // 2 for g in shards]

    def body(*refs):
        outs = refs[n:2 * n]
        send_sems, recv_sems = refs[2 * n:]
        x, y, c, _ = _place()
        cps = []
        for a in range(n):
            rows = outs[a].at[pl.ds(pl.multiple_of(c * halves[a], 8), halves[a])]
            cps.append(_remote(rows, rows, send_sems.at[a], recv_sems.at[a], (x, y, 1 - c)))
        for cp in cps:
            cp.start()
        for cp in cps:
            cp.wait()

    return pl.pallas_call(
        body, name=name, in_specs=[ANY] * n, out_specs=[ANY] * n,
        out_shape=[jax.ShapeDtypeStruct(g.shape, g.dtype) for g in shards],
        input_output_aliases={a: a for a in range(n)},
        scratch_shapes=[pltpu.SemaphoreType.DMA((n,)), pltpu.SemaphoreType.DMA((n,))],
    )(*shards)


def _adam(w, g, m, v):
    m = ADAM_B1 * m + (1.0 - ADAM_B1) * g
    v = ADAM_B2 * v + (1.0 - ADAM_B2) * (g * g)
    m_hat = m / (1.0 - ADAM_B1 ** ADAM_STEP)
    v_hat = v / (1.0 - ADAM_B2 ** ADAM_STEP)
    delta = -ADAM_LR * (m_hat / (jnp.sqrt(v_hat) + ADAM_EPS) + ADAM_WD * w)
    return delta, m, v


def small_allreduce_adam(g_part, w, m, v):
    n_dev = 8
    r, d = g_part.shape

    def body(g_ref, w_ref, m_ref, v_ref, gs_ref, dl_ref, nm_ref, nv_ref, all_ref, send_sems, recv_sems):
        x, y, c, _ = _place()
        me = 4 * x + 2 * y + c
        all_ref[me] = g_ref[...]
        cps = []
        for rel in range(1, n_dev):
            px = 1 - x if rel & 4 else x
            py = 1 - y if rel & 2 else y
            pc = 1 - c if rel & 1 else c
            cps.append(_remote(g_ref, all_ref.at[me], send_sems.at[rel - 1], recv_sems.at[rel - 1], (px, py, pc)))
        for cp in cps:
            cp.start()
        for cp in cps:
            cp.wait()
        total = all_ref[0]
        for dev in range(1, n_dev):
            total = total + all_ref[dev]
        gs_ref[...] = total
        delta, nm, nv = _adam(w_ref[...], total, m_ref[...], v_ref[...])
        dl_ref[...] = delta
        nm_ref[...] = nm
        nv_ref[...] = nv

    vm = pl.BlockSpec(memory_space=pltpu.VMEM)
    out = jax.ShapeDtypeStruct((r, d), F32)
    return pl.pallas_call(
        body, name="small_allreduce_adam", in_specs=[vm, vm, vm, vm], out_specs=[vm, vm, vm, vm],
        out_shape=[out, out, out, out],
        scratch_shapes=[pltpu.VMEM((n_dev, r, d), F32), pltpu.SemaphoreType.DMA((n_dev - 1,)),
                        pltpu.SemaphoreType.DMA((n_dev - 1,))],
    )(g_part, w, m, v)


def adam_update(name, w, m, v, g_buf, col_blk):
    w, m, v = w[0], m[0], v[0]
    r, c = w.shape
    tr = _tile(r, 128, 8)

    def kern(w_ref, m_ref, v_ref, g_ref, go_ref, dl_ref, nm_ref, nv_ref):
        g = g_ref[...]
        delta, nm, nv = _adam(w_ref[...], g, m_ref[...], v_ref[...])
        go_ref[...] = g
        dl_ref[...] = delta
        nm_ref[...] = nm
        nv_ref[...] = nv

    blk = pl.BlockSpec((tr, c), lambda i: (i, 0))
    out = jax.ShapeDtypeStruct((r, c), F32)
    res = pl.pallas_call(
        kern, name=name, grid=(r // tr,),
        in_specs=[blk, blk, blk, pl.BlockSpec((tr, c), lambda i: (i, col_blk))],
        out_specs=[blk] * 4, out_shape=[out] * 4, compiler_params=_params("parallel"),
    )(w, m, v, g_buf)
    return [a[None] for a in res]


def _col_pieces(slab_of, lo, hi, cw):
    out = []
    while lo < hi:
        k, a = divmod(lo, cw)
        b = min(cw, a + hi - lo)
        out.append(slab_of(k)[:, a:b])
        lo += b - a
    return out


def kernel(x, norm_mix_pre, norm_mix_post, w_in, b_forget, w_branch_sb, w_branch_fox, w_out, norm_ffn_pre, norm_ffn_post, w_ffn_gate, w_ffn_up, w_ffn_down, loss_target, m_norm_mix_pre, m_norm_mix_post, m_w_in, m_b_forget, m_w_branch_sb, m_w_branch_fox, m_w_out, m_norm_ffn_pre, m_norm_ffn_post, m_w_ffn_gate, m_w_ffn_up, m_w_ffn_down, v_norm_mix_pre, v_norm_mix_post, v_w_in, v_b_forget, v_w_branch_sb, v_w_branch_fox, v_w_out, v_norm_ffn_pre, v_norm_ffn_post, v_w_ffn_gate, v_w_ffn_up, v_w_ffn_down):
    s, d = x.shape[1], x.shape[2]
    n_heads = b_forget.shape[1]
    d_att = n_heads * HEAD_DIM
    c_in = w_in.shape[2]
    c_br = w_branch_sb.shape[2]
    c_gu = w_ffn_gate.shape[2]
    d_ff = c_gu * N_CHIPS
    d_in = c_in * N_CHIPS
    f_pad = 512
    n_qkv = 6 * d_att
    n_gf = 2 * d + f_pad
    core = lax.axis_index("c").astype(jnp.int32).reshape(1)
    chip = (2 * lax.axis_index("x") + lax.axis_index("y")).astype(jnp.int32).reshape(1)

    in_send, in_recv, in_bufs, in_token = gather_start("gather_start_w_in", [cast_place("place_w_in", [w_in], chip)])
    ag_send, ag_recv, ag_bufs, ag_token = gather_start("gather_start_rest", [
        cast_place("place_branch", [w_branch_sb, w_branch_fox], chip),
        cast_place("place_out", [w_out + in_token[0, 0]], chip),
        cast_place("place_gate_up", [w_ffn_gate, w_ffn_up], chip),
        cast_place("place_down", [w_ffn_down], chip)])
    g_in, = gather_forward("forward_w_in", gather_wait("gather_wait_w_in", in_bufs, in_send, in_recv,
                                                       [ag_token, m_w_in[0], v_w_in[0]]))
    slab = lambda k: g_in[k]
    w_main = jnp.concatenate(
        _col_pieces(slab, 0, n_qkv, c_in) + _col_pieces(slab, n_qkv + n_heads, d_in, c_in)
        + _col_pieces(slab, n_qkv, n_qkv + n_heads, c_in) + [jnp.zeros((d, f_pad - n_heads), BF16)], axis=1)
    x2 = x[0]
    tgt = loss_target[0]
    b_pad = jnp.pad(b_forget, ((0, 0), (0, LANES - n_heads)))

    u = norm_in(x2, norm_mix_pre)
    qkv = mm(u, w_main, "nn", BF16, "proj_qkv", b_win=(0, n_qkv))
    gf = mm(u, w_main, "nn", F32, "proj_gates", b_win=(n_qkv, n_gf))
    cum = cum_fwd(gf, b_pad, 2 * d)
    cum_heads = cum[:, :n_heads].T
    cum_col, cum_row = cum_heads[:, :, None], cum_heads[:, None, :]
    o_sb = sb_fwd(qkv, n_heads, 0)
    o_fx, lse = fox_fwd(qkv, cum_col, cum_row, n_heads, 3 * n_heads)
    g_br, g_out, g_gu, g_dn = gather_forward(
        "forward_rest", gather_wait("gather_wait_rest", ag_bufs, ag_send, ag_recv, [o_sb, o_fx]))
    w_o = g_out.reshape(d, d)
    w_dn = g_dn.reshape(d_ff, d)
    bsb = mm(o_sb, g_br, "nn", F32, "branch_sb", tn=c_br, chunks=(1, 0))
    bfx = mm(o_fx, g_br, "nn", F32, "branch_fox", tn=c_br, chunks=(1, 1))
    merged = gate_fwd(bsb, bfx, gf)
    mix = mm(merged, w_o, "nn", F32, "out_proj")
    h1, u2 = mid_fwd(x2, mix, norm_mix_post, norm_ffn_pre)
    gu = mm(u2, g_gu, "nn", F32, "ffn_gate_up", tn=c_gu, chunks=(2, 0))
    act = swiglu_fwd(gu, c_gu)
    ff = mm(act, w_dn, "nn", F32, "ffn_down")
    dy, d_ff_out, dg_fpost, loss_part = loss_head(h1, ff, norm_ffn_post, tgt)

    p_dn = mm(act, d_ff_out, "tn", BF16, "dw_ffn_down").reshape(N_CHIPS, d_ff // N_CHIPS, d)
    d_act = mm(d_ff_out, w_dn, "nt", F32, "d_act")
    d_gu = swiglu_bwd(d_act, gu, c_gu)
    p_gu = mm(u2, d_gu, "tn", BF16, "dw_ffn_gate_up", tn=c_gu, chunks=(2, 0),
              out_into=lax.empty((N_CHIPS, d, 2 * c_gu), BF16))
    du2 = mm(d_gu, g_gu, "nt", F32, "d_u2", tk=c_gu, chunks=(2, 0))
    dh1, d_mix, dg_fpre, dg_post = mid_bwd(dy, du2, h1, mix, norm_ffn_pre, norm_mix_post)
    p_out = mm(merged, d_mix, "tn", BF16, "dw_out").reshape(N_CHIPS, d // N_CHIPS, d)
    d_merged = mm(d_mix, w_o, "nt", F32, "d_merged")
    d_bsb, d_bfx, d_gs, d_gx = gate_bwd(d_merged, bsb, bfx, gf)
    p_br = mm(o_sb, d_bsb, "tn", BF16, "dw_branch_sb", tn=c_br, chunks=(1, 0),
              out_into=lax.empty((N_CHIPS, d_att, 2 * c_br), BF16))
    p_br = mm(o_fx, d_bfx, "tn", BF16, "dw_branch_fox", tn=c_br, chunks=(1, 1), out_into=p_br)
    d_osb = mm(d_bsb, g_br, "nt", BF16, "d_o_sb", tk=c_br, chunks=(1, 0))
    d_ofx = mm(d_bfx, g_br, "nt", BF16, "d_o_fox", tk=c_br, chunks=(1, 1))

    def reduce_start(tag, pieces, names):
        from_sibling = swap_halves("swap_halves_" + tag, pieces)
        sums = [pair_sum("pair_sum_" + t, p, q, core) for t, p, q in zip(names, pieces, from_sibling)]
        return scatter_start("scatter_start_" + tag, sums)

    def reduce_end(tag, started, names, after):
        send, recv, sums, lands, _ = started
        sums, lands = scatter_wait("scatter_wait_" + tag, sums, lands, send, recv, after)
        return join_halves("join_halves_" + tag, [chip_sum("chip_sum_" + t, sm, got, chip, core)
                                                  for t, sm, got in zip(names, sums, lands)])

    rest_names = ["branch", "out", "gate_up", "down"]
    rest_started = reduce_start("rest", [p_br, p_out, p_gu, p_dn], rest_names)
    d_osb = d_osb + rest_started[4][0, 0].astype(BF16)
    dq_s, dk_s, dv_s = sb_bwd(qkv, d_osb, n_heads, 0)
    dq_f, dk_f, dv_f, dcq, dck = fox_bwd(qkv, cum_col, cum_row, o_fx, d_ofx, lse, n_heads, 3 * n_heads)
    d_cum = jnp.pad((dcq[:, :, 0] + dck[:, 0, :]).T, ((0, 0), (0, LANES - n_heads)))
    d_f, db_pad = cum_bwd(d_cum, gf, b_pad, 2 * d, n_heads)
    d_main = jnp.concatenate(
        [dq_s, dk_s, dv_s, dq_f, dk_f, dv_f, d_gs, d_gx, d_f, jnp.zeros((s, f_pad - LANES), BF16)], axis=1)
    dw_main = mm(u, d_main, "tn", BF16, "dw_in")

    def main_cols(lo, hi):
        out = []
        for s0, s1, m0 in [(0, n_qkv, 0), (n_qkv, n_qkv + n_heads, n_qkv + 2 * d), (n_qkv + n_heads, d_in, n_qkv)]:
            a, b = max(lo, s0), min(hi, s1)
            if a < b:
                out.append(dw_main[:, m0 + a - s0:m0 + b - s0])
        return out

    p_in = jnp.stack([jnp.concatenate(main_cols(k * c_in, (k + 1) * c_in), axis=1) for k in range(N_CHIPS)])

    in_started = reduce_start("w_in", [p_in], ["in"])
    du = mm(d_main, w_main, "nt", F32, "d_u")
    dx, dg_pre = in_bwd(dh1, du, x2, norm_mix_pre + in_started[4][0:1, 0:1])
    gr_br, gr_out, gr_gu, gr_dn = reduce_end("rest", rest_started, rest_names, dx)

    upd_bs = adam_update("adam_branch_sb", w_branch_sb, m_w_branch_sb, v_w_branch_sb, gr_br, 0)
    upd_bf = adam_update("adam_branch_fox", w_branch_fox, m_w_branch_fox, v_w_branch_fox, gr_br, 1)
    upd_o = adam_update("adam_out", w_out, m_w_out, v_w_out, gr_out, 0)
    upd_ga = adam_update("adam_gate", w_ffn_gate, m_w_ffn_gate, v_w_ffn_gate, gr_gu, 0)
    upd_up = adam_update("adam_up", w_ffn_up, m_w_ffn_up, v_w_ffn_up, gr_gu, 1)
    upd_dn = adam_update("adam_down", w_ffn_down, m_w_ffn_down, v_w_ffn_down, gr_dn, 0)

    def pack(rows):
        rows = [jnp.pad(r_, ((0, 0), (0, d - r_.shape[1]))) for r_ in rows]
        return jnp.concatenate(rows + [jnp.zeros((8 - len(rows), d), F32)], axis=0)

    sm_g, sm_d, sm_m, sm_v = small_allreduce_adam(
        pack([dg_pre, dg_post, dg_fpre, dg_fpost, db_pad]),
        pack([norm_mix_pre, norm_mix_post, norm_ffn_pre, norm_ffn_post, b_forget]),
        pack([m_norm_mix_pre, m_norm_mix_post, m_norm_ffn_pre, m_norm_ffn_post, m_b_forget]),
        pack([v_norm_mix_pre, v_norm_mix_post, v_norm_ffn_pre, v_norm_ffn_post, v_b_forget]))

    done = sm_d[0:1, 0:1] + sum(u_[1][0, 0:1, 0:1] for u_ in (upd_bs, upd_bf, upd_o, upd_ga, upd_up, upd_dn))
    gr_in, = reduce_end("w_in", in_started, ["in"], done)
    upd_in = adam_update("adam_w_in", w_in, m_w_in, v_w_in, gr_in, 0)
    grads, deltas, new_ms, new_vs = zip(upd_in, upd_bs, upd_bf, upd_o, upd_ga, upd_up, upd_dn)

    def small(a):
        return [a[0:1], a[1:2], a[2:3], a[3:4], a[4:5, :n_heads]]

    def ordered(sm, bg):
        return [sm[0], sm[1], bg[0], sm[4], bg[1], bg[2], bg[3], sm[2], sm[3], bg[4], bg[5], bg[6]]

    loss = lax.psum(loss_part[0, 0], ("x", "y", "c"))
    return (loss, dx[None], *ordered(small(sm_g), grads), *ordered(small(sm_d), deltas),
            *ordered(small(sm_m), new_ms), *ordered(small(sm_v), new_vs))
```

```python
import functools

import jax
import jax.numpy as jnp
from jax import lax
from jax.experimental import pallas as pl
from jax.experimental.pallas import tpu as pltpu

F32 = jnp.float32
BF16 = jnp.bfloat16
MESH = pl.DeviceIdType.MESH

HEAD_DIM = 128
LANES = 128
ATT_TILE = 512
ROW_TILE = 256
N_CHIPS = 4
RMS_EPS = 1e-6
ADAM_LR = 0.001
ADAM_B1 = 0.9
ADAM_B2 = 0.999
ADAM_EPS = 1e-08
ADAM_WD = 0.01
ADAM_STEP = 10
NEG_BIG = -1e30
VMEM_LIMIT = 56 * 1024 * 1024
MM_VMEM_BUDGET = 40 * 1024 * 1024
ATT_STRIP = 512

NN = (((1,), (0,)), ((), ()))
NT = (((1,), (1,)), ((), ()))
TN = (((0,), (0,)), ((), ()))


def _tile(n, pref, align):
    best = None
    t = align
    while t <= min(n, pref):
        if n % t == 0:
            best = t
        t += align
    return n if best is None else best


def _params(*sem):
    return pltpu.CompilerParams(dimension_semantics=sem, vmem_limit_bytes=VMEM_LIMIT)


def _mm_tiles(m, n, k, a_bytes, b_bytes, out_bytes, tn, tk):
    tm = _tile(m, 2048, LANES)
    tk = tk or _tile(k, 512, LANES)

    def vmem(t):
        acc = 0 if out_bytes == 4 else tm * t * 4
        return acc + 2 * tm * t * out_bytes + 2 * (tm * tk * a_bytes + tk * t * b_bytes)

    if tn is None:
        fits = [t for t in range(LANES, min(n, 2048) + 1, LANES) if n % t == 0 and vmem(t) <= MM_VMEM_BUDGET]
        tn = max(fits) if fits else _tile(n, LANES, LANES)
    return tm, tn, tk


def mm(a, b, mode, out_dtype, name, *, tn=None, tk=None, b_win=None, chunks=None, out_into=None, after=None):
    n_per, blk0 = chunks if chunks else (1, 0)
    if mode == "nn":
        m, k = a.shape
        n = b.shape[0] * n_per * tn if chunks else (b_win[1] if b_win else b.shape[1])
    elif mode == "nt":
        m = a.shape[0]
        k = b.shape[0] * n_per * tk if chunks else a.shape[1]
        n = b.shape[-2]
    else:
        k, m = a.shape
        n = b.shape[1]
    in_place = jnp.dtype(out_dtype) == jnp.dtype(F32)
    tm, tn, tk = _mm_tiles(m, n, k, a.dtype.itemsize, b.dtype.itemsize, jnp.dtype(out_dtype).itemsize, tn, tk)
    assert m % tm == 0 and n % tn == 0 and k % tk == 0, (name, m, n, k, tm, tn, tk)
    j0 = 0
    if b_win:
        assert b_win[0] % tn == 0
        j0 = b_win[0] // tn
    nk = k // tk
    dims = {"nn": NN, "nt": NT, "tn": TN}[mode]

    def kern(a_ref, b_ref, *rest):
        o_ref, acc_ref = (rest[-1], rest[-1]) if in_place else (rest[-2], rest[-1])
        kk = pl.program_id(2)

        @pl.when(kk == 0)
        def _():
            acc_ref[...] = jnp.zeros_like(acc_ref)

        acc_ref[...] += lax.dot_general(a_ref[...].astype(BF16), b_ref[...].astype(BF16), dims,
                                        preferred_element_type=F32)

        if not in_place:
            @pl.when(kk == nk - 1)
            def _():
                o_ref[...] = acc_ref[...].astype(o_ref.dtype)

    out_spec = pl.BlockSpec((tm, tn), lambda i, j, kk: (i, j))
    out_shape = jax.ShapeDtypeStruct((m, n), out_dtype)
    if mode == "nn":
        a_spec = pl.BlockSpec((tm, tk), lambda i, j, kk: (i, kk))
        if chunks:
            b_spec = pl.BlockSpec((None, tk, tn), lambda i, j, kk: (j // n_per, kk, blk0 + j % n_per))
        else:
            b_spec = pl.BlockSpec((tk, tn), lambda i, j, kk: (kk, j + j0))
    elif mode == "nt":
        a_spec = pl.BlockSpec((tm, tk), lambda i, j, kk: (i, kk))
        if chunks:
            b_spec = pl.BlockSpec((None, tn, tk), lambda i, j, kk: (kk // n_per, j, blk0 + kk % n_per))
        else:
            b_spec = pl.BlockSpec((tn, tk), lambda i, j, kk: (j, kk))
    else:
        a_spec = pl.BlockSpec((tk, tm), lambda i, j, kk: (kk, i))
        b_spec = pl.BlockSpec((tk, tn), lambda i, j, kk: (kk, j))
        if chunks:
            out_spec = pl.BlockSpec((None, tm, tn), lambda i, j, kk: (j // n_per, i, blk0 + j % n_per))
    in_specs, operands, aliases = [a_spec, b_spec], [a, b], {}
    if chunks and mode == "tn":
        assert out_into is not None
        out_shape = jax.ShapeDtypeStruct(out_into.shape, out_dtype)
        in_specs.append(pl.BlockSpec(memory_space=pl.ANY))
        operands.append(out_into)
        aliases = {2: 0}
    if after is not None:
        in_specs.append(pl.BlockSpec(memory_space=pl.ANY))
        operands.append(after)
    return pl.pallas_call(
        kern, name=name, grid=(m // tm, n // tn, nk),
        in_specs=in_specs, out_specs=out_spec, out_shape=out_shape,
        scratch_shapes=[] if in_place else [pltpu.VMEM((tm, tn), F32)], input_output_aliases=aliases,
        compiler_params=_params("parallel", "parallel", "arbitrary"),
    )(*operands)


def _rstd(v):
    return lax.rsqrt(jnp.mean(v * v, axis=-1, keepdims=True) + RMS_EPS)


def _norm_bwd(v, g, dy):
    r = _rstd(v)
    vh = v * r
    dyg = dy * g
    dv = r * (dyg - vh * jnp.mean(dyg * vh, axis=-1, keepdims=True))
    return dv, jnp.sum(dy * vh, axis=0, keepdims=True)


def _row_call(kern, name, ins, outs, s, d):
    tr = _tile(s, ROW_TILE, 16)

    def spec(shape, is_row):
        if is_row:
            return pl.BlockSpec((tr, shape[1]), lambda i: (i, 0))
        return pl.BlockSpec(shape, lambda i: (0, 0))

    return pl.pallas_call(
        kern, name=name, grid=(s // tr,),
        in_specs=[spec(a.shape, r) for a, r in ins],
        out_specs=[spec(sh, r) for sh, _, r in outs],
        out_shape=[jax.ShapeDtypeStruct(sh, dt) for sh, dt, _ in outs],
        compiler_params=_params("arbitrary"),
    )(*[a for a, _ in ins])


def norm_in(x, g):
    s, d = x.shape

    def kern(x_ref, g_ref, u_ref):
        v = x_ref[...]
        u_ref[...] = (v * _rstd(v) * g_ref[...]).astype(BF16)

    return _row_call(kern, "norm_in", [(x, True), (g, False)], [((s, d), BF16, True)], s, d)[0]


def mid_fwd(x, mix, g_post, g_fpre):
    s, d = x.shape

    def kern(x_ref, mix_ref, gp_ref, gf_ref, h1_ref, u2_ref):
        mixv = mix_ref[...]
        h1 = x_ref[...] + mixv * _rstd(mixv) * gp_ref[...]
        h1_ref[...] = h1
        u2_ref[...] = (h1 * _rstd(h1) * gf_ref[...]).astype(BF16)

    return _row_call(kern, "mid_fwd", [(x, True), (mix, True), (g_post, False), (g_fpre, False)],
                     [((s, d), F32, True), ((s, d), BF16, True)], s, d)


def loss_head(h1, ff, g_fpost, target):
    s, d = h1.shape

    def kern(h1_ref, ff_ref, g_ref, t_ref, dy_ref, dff_ref, dg_ref, loss_ref):
        @pl.when(pl.program_id(0) == 0)
        def _():
            dg_ref[...] = jnp.zeros_like(dg_ref)
            loss_ref[...] = jnp.zeros_like(loss_ref)

        ffv = ff_ref[...]
        g = g_ref[...]
        y = h1_ref[...] + ffv * _rstd(ffv) * g
        diff = y - t_ref[...]
        row_loss = jnp.mean(diff * diff, axis=-1, keepdims=True)
        loss_ref[...] += 0.5 * jnp.sum(row_loss, axis=0, keepdims=True)
        dy = diff / d
        dy_ref[...] = dy
        dff, dg = _norm_bwd(ffv, g, dy)
        dff_ref[...] = dff.astype(BF16)
        dg_ref[...] += dg

    return _row_call(kern, "loss_head",
                     [(h1, True), (ff, True), (g_fpost, False), (target, True)],
                     [((s, d), F32, True), ((s, d), BF16, True), ((1, d), F32, False), ((1, 1), F32, False)], s, d)


def mid_bwd(dy, du2, h1, mix, g_fpre, g_post):
    s, d = dy.shape

    def kern(dy_ref, du2_ref, h1_ref, mix_ref, gf_ref, gp_ref, dh1_ref, dmix_ref, dgf_ref, dgp_ref):
        @pl.when(pl.program_id(0) == 0)
        def _():
            dgf_ref[...] = jnp.zeros_like(dgf_ref)
            dgp_ref[...] = jnp.zeros_like(dgp_ref)

        dh, dgf = _norm_bwd(h1_ref[...], gf_ref[...], du2_ref[...])
        dh1 = dy_ref[...] + dh
        dh1_ref[...] = dh1
        dmix, dgp = _norm_bwd(mix_ref[...], gp_ref[...], dh1)
        dmix_ref[...] = dmix.astype(BF16)
        dgf_ref[...] += dgf
        dgp_ref[...] += dgp

    return _row_call(kern, "mid_bwd",
                     [(dy, True), (du2, True), (h1, True), (mix, True), (g_fpre, False), (g_post, False)],
                     [((s, d), F32, True), ((s, d), BF16, True), ((1, d), F32, False), ((1, d), F32, False)], s, d)


def in_bwd(dh1, du, x, g_pre):
    s, d = x.shape

    def kern(dh1_ref, du_ref, x_ref, g_ref, dx_ref, dg_ref):
        @pl.when(pl.program_id(0) == 0)
        def _():
            dg_ref[...] = jnp.zeros_like(dg_ref)

        dxn, dg = _norm_bwd(x_ref[...], g_ref[...], du_ref[...])
        dx_ref[...] = dh1_ref[...] + dxn
        dg_ref[...] += dg

    return _row_call(kern, "in_bwd", [(dh1, True), (du, True), (x, True), (g_pre, False)],
                     [((s, d), F32, True), ((1, d), F32, False)], s, d)


def _sigmoid(v):
    return 1.0 / (1.0 + jnp.exp(-v))


def gate_fwd(bsb, bfx, gf):
    s, d = bsb.shape
    tr, tc = _tile(s, 256, 16), _tile(d, 512, LANES)
    nc = d // tc

    def kern(bsb_ref, bfx_ref, gs_ref, gx_ref, o_ref):
        o_ref[...] = (_sigmoid(gs_ref[...]) * bsb_ref[...] + _sigmoid(gx_ref[...]) * bfx_ref[...]).astype(BF16)

    blk = pl.BlockSpec((tr, tc), lambda i, j: (i, j))
    return pl.pallas_call(
        kern, name="gate_fwd", grid=(s // tr, nc),
        in_specs=[blk, blk, blk, pl.BlockSpec((tr, tc), lambda i, j: (i, j + nc))],
        out_specs=blk, out_shape=jax.ShapeDtypeStruct((s, d), BF16),
        compiler_params=_params("parallel", "parallel"),
    )(bsb, bfx, gf, gf)


def gate_bwd(dmerged, bsb, bfx, gf):
    s, d = bsb.shape
    tr, tc = _tile(s, 256, 16), _tile(d, 512, LANES)
    nc = d // tc

    def kern(dm_ref, bsb_ref, bfx_ref, gs_ref, gx_ref, dbs_ref, dbx_ref, dgs_ref, dgx_ref):
        dm = dm_ref[...]
        ss = _sigmoid(gs_ref[...])
        sx = _sigmoid(gx_ref[...])
        dbs_ref[...] = (dm * ss).astype(BF16)
        dbx_ref[...] = (dm * sx).astype(BF16)
        dgs_ref[...] = (dm * bsb_ref[...] * ss * (1.0 - ss)).astype(BF16)
        dgx_ref[...] = (dm * bfx_ref[...] * sx * (1.0 - sx)).astype(BF16)

    blk = pl.BlockSpec((tr, tc), lambda i, j: (i, j))
    out = jax.ShapeDtypeStruct((s, d), BF16)
    return pl.pallas_call(
        kern, name="gate_bwd", grid=(s // tr, nc),
        in_specs=[blk, blk, blk, blk, pl.BlockSpec((tr, tc), lambda i, j: (i, j + nc))],
        out_specs=[blk, blk, blk, blk], out_shape=[out, out, out, out],
        compiler_params=_params("parallel", "parallel"),
    )(dmerged, bsb, bfx, gf, gf)


def swiglu_fwd(gu, cw):
    s, f2 = gu.shape
    tr = _tile(s, 256, 16)

    def kern(gu_ref, o_ref):
        g = gu_ref[:, :cw]
        o_ref[...] = (g * _sigmoid(g) * gu_ref[:, cw:]).astype(BF16)

    return pl.pallas_call(
        kern, name="swiglu_fwd", grid=(s // tr, f2 // (2 * cw)),
        in_specs=[pl.BlockSpec((tr, 2 * cw), lambda i, j: (i, j))],
        out_specs=pl.BlockSpec((tr, cw), lambda i, j: (i, j)),
        out_shape=jax.ShapeDtypeStruct((s, f2 // 2), BF16),
        compiler_params=_params("parallel", "parallel"),
    )(gu)


def swiglu_bwd(dact, gu, cw):
    s, f2 = gu.shape
    tr = _tile(s, 256, 16)

    def kern(da_ref, gu_ref, o_ref):
        da = da_ref[...]
        g = gu_ref[:, :cw]
        sg = _sigmoid(g)
        o_ref[:, :cw] = (da * gu_ref[:, cw:] * (sg * (1.0 + g * (1.0 - sg)))).astype(BF16)
        o_ref[:, cw:] = (da * (g * sg)).astype(BF16)

    return pl.pallas_call(
        kern, name="swiglu_bwd", grid=(s // tr, f2 // (2 * cw)),
        in_specs=[pl.BlockSpec((tr, cw), lambda i, j: (i, j)), pl.BlockSpec((tr, 2 * cw), lambda i, j: (i, j))],
        out_specs=pl.BlockSpec((tr, 2 * cw), lambda i, j: (i, j)),
        out_shape=jax.ShapeDtypeStruct((s, f2), BF16),
        compiler_params=_params("parallel", "parallel"),
    )(dact, gu)


def _split3(v):
    hi = v.astype(BF16)
    r = v - hi.astype(F32)
    mid = r.astype(BF16)
    lo = (r - mid.astype(F32)).astype(BF16)
    return hi, mid, lo


def _dot3_right(v, ones):
    hi, mid, lo = _split3(v)
    d = lambda p: jnp.dot(p, ones, preferred_element_type=F32)
    return (d(lo) + d(mid)) + d(hi)


def _dot3_left(ones, v):
    hi, mid, lo = _split3(v)
    d = lambda p: jnp.dot(ones, p, preferred_element_type=F32)
    return (d(lo) + d(mid)) + d(hi)


def _split2(v):
    hi = v.astype(BF16)
    return hi, (v - hi.astype(F32)).astype(BF16)


def _dot2_right(v, ones):
    hi, lo = _split2(v)
    return jnp.dot(lo, ones, preferred_element_type=F32) + jnp.dot(hi, ones, preferred_element_type=F32)


def _log1p_exp_neg_abs(v):
    return jnp.log(1.0 + jnp.exp(-jnp.abs(v)))


def _mask01(cond):
    return jnp.where(cond, 1.0, 0.0).astype(BF16)


def _iota2(t):
    return (lax.broadcasted_iota(jnp.int32, (t, t), 0), lax.broadcasted_iota(jnp.int32, (t, t), 1))


def cum_fwd(gf, b_pad, f_col0):
    s = gf.shape[0]
    t = _tile(s, ATT_TILE, LANES)
    fb = f_col0 // LANES

    def kern(f_ref, b_ref, cum_ref, carry_ref):
        @pl.when(pl.program_id(0) == 0)
        def _():
            carry_ref[...] = jnp.zeros_like(carry_ref)

        v = f_ref[...] + b_ref[...]
        lf = jnp.minimum(v, 0.0) - _log1p_exp_neg_abs(v)
        row, col = _iota2(t)
        cum = _dot3_left(_mask01(col <= row), lf) + carry_ref[...]
        cum_ref[...] = cum
        carry_ref[...] = cum[t - 1:t, :]

    return pl.pallas_call(
        kern, name="cum_fwd", grid=(s // t,),
        in_specs=[pl.BlockSpec((t, LANES), lambda i: (i, fb)), pl.BlockSpec((1, LANES), lambda i: (0, 0))],
        out_specs=pl.BlockSpec((t, LANES), lambda i: (i, 0)),
        out_shape=jax.ShapeDtypeStruct((s, LANES), F32),
        scratch_shapes=[pltpu.VMEM((1, LANES), F32)],
        compiler_params=_params("arbitrary"),
    )(gf, b_pad)


def cum_bwd(dcum, gf, b_pad, f_col0, n_heads):
    s = gf.shape[0]
    t = _tile(s, ATT_TILE, LANES)
    nb = s // t
    fb = f_col0 // LANES

    def kern(dc_ref, f_ref, b_ref, df_ref, db_ref, carry_ref):
        @pl.when(pl.program_id(0) == 0)
        def _():
            carry_ref[...] = jnp.zeros_like(carry_ref)
            db_ref[...] = jnp.zeros_like(db_ref)

        row, col = _iota2(t)
        dlf = _dot3_left(_mask01(col >= row), dc_ref[...]) + carry_ref[...]
        carry_ref[...] = dlf[0:1, :]
        v = f_ref[...] + b_ref[...]
        sig_neg = jnp.exp(-jnp.maximum(v, 0.0) - _log1p_exp_neg_abs(v))
        lane = lax.broadcasted_iota(jnp.int32, (t, LANES), 1)
        df = jnp.where(lane < n_heads, dlf * sig_neg, 0.0)
        df_ref[...] = df.astype(BF16)
        db_ref[...] += jnp.sum(df, axis=0, keepdims=True)

    return pl.pallas_call(
        kern, name="cum_bwd", grid=(nb,),
        in_specs=[pl.BlockSpec((t, LANES), lambda i: (nb - 1 - i, 0)),
                  pl.BlockSpec((t, LANES), lambda i: (nb - 1 - i, fb)),
                  pl.BlockSpec((1, LANES), lambda i: (0, 0))],
        out_specs=[pl.BlockSpec((t, LANES), lambda i: (nb - 1 - i, 0)), pl.BlockSpec((1, LANES), lambda i: (0, 0))],
        out_shape=[jax.ShapeDtypeStruct((s, LANES), BF16), jax.ShapeDtypeStruct((1, LANES), F32)],
        scratch_shapes=[pltpu.VMEM((1, LANES), F32)],
        compiler_params=_params("arbitrary"),
    )(dcum, gf, b_pad)


def _qkv_specs(s, t, n_heads, base):
    return [pl.BlockSpec((t, HEAD_DIM), lambda h, i: (i, base + h)),
            pl.BlockSpec((s, HEAD_DIM), lambda h, i: (0, base + n_heads + h)),
            pl.BlockSpec((s, HEAD_DIM), lambda h, i: (0, base + 2 * n_heads + h))]


def _strips(t):
    sr = _tile(t, ATT_STRIP, 8)
    return sr, t // sr, [slice(si * sr, (si + 1) * sr) for si in range(t // sr)]


def _key_minus_row(sr, t):
    return lax.broadcasted_iota(jnp.int32, (sr, t), 1) - lax.broadcasted_iota(jnp.int32, (sr, t), 0)


def _sb_scores(q, k, diff, lim):
    z = lax.dot_general(q, k, NT, preferred_element_type=F32) * (HEAD_DIM ** -0.5)
    valid = diff < lim
    l1p = _log1p_exp_neg_abs(z)
    log_keep = jnp.where(valid, -jnp.maximum(z, 0.0) - l1p, 0.0)
    return z, valid, l1p, log_keep


def sb_fwd(qkv, n_heads, base):
    s = qkv.shape[0]
    t = _tile(s, ATT_TILE, LANES)
    sr, ns, strips = _strips(t)

    def kern(q_ref, k_ref, v_ref, o_ref):
        i = pl.program_id(1)
        row, col = _iota2(t)
        after = _mask01(row > col)
        diff = _key_minus_row(sr, t)
        qs = [q_ref[sl, :] for sl in strips]

        def body(jj, carry):
            runs, accs = carry
            j = i - jj
            off = pl.multiple_of(j * t, t)
            k = k_ref[pl.ds(off, t), :]
            v = v_ref[pl.ds(off, t), :]
            lim = jnp.where(j < i, t, 0)
            new_runs, new_accs = [], []
            for si in range(ns):
                z, valid, l1p, log_keep = _sb_scores(qs[si], k, diff, lim + si * sr)
                between = _dot2_right(log_keep, after) + runs[si]
                w = jnp.where(valid, jnp.exp(jnp.minimum(z, 0.0) - l1p + between), 0.0)
                new_accs.append(accs[si] + jnp.dot(w.astype(BF16), v, preferred_element_type=F32))
                new_runs.append(runs[si] + jnp.sum(log_keep, axis=1, keepdims=True))
            return tuple(new_runs), tuple(new_accs)

        init = (tuple(jnp.zeros((sr, 1), F32) for _ in strips), tuple(jnp.zeros((sr, HEAD_DIM), F32) for _ in strips))
        _, accs = lax.fori_loop(0, i + 1, body, init)
        for sl, acc in zip(strips, accs):
            o_ref[sl, :] = acc.astype(o_ref.dtype)

    return pl.pallas_call(
        kern, name="sb_fwd", grid=(n_heads, s // t),
        in_specs=_qkv_specs(s, t, n_heads, base),
        out_specs=pl.BlockSpec((t, HEAD_DIM), lambda h, i: (i, h)),
        out_shape=jax.ShapeDtypeStruct((s, n_heads * HEAD_DIM), BF16),
        compiler_params=_params("parallel", "arbitrary"),
    )(qkv, qkv, qkv)


def sb_bwd(qkv, d_o, n_heads, base):
    s = qkv.shape[0]
    t = _tile(s, ATT_TILE, LANES)
    nq = s // t
    sr, ns, strips = _strips(t)
    scale = HEAD_DIM ** -0.5

    def kern(q_ref, k_ref, v_ref, do_ref, dq_ref, dk_ref, dv_ref, dk_acc, dv_acc, run_ref):
        i = pl.program_id(1)

        @pl.when(i == 0)
        def _():
            dk_acc[...] = jnp.zeros_like(dk_acc)
            dv_acc[...] = jnp.zeros_like(dv_acc)

        row, col = _iota2(t)
        after = _mask01(row > col)
        before = _mask01(row < col)
        diff = _key_minus_row(sr, t)
        qs = [q_ref[sl, :] for sl in strips]
        dos = [do_ref[sl, :] for sl in strips]

        def sweep1(jj, runs):
            j = i - jj
            k = k_ref[pl.ds(pl.multiple_of(j * t, t), t), :]
            lim = jnp.where(j < i, t, 0)
            new_runs = []
            for si, sl in enumerate(strips):
                _, _, _, log_keep = _sb_scores(qs[si], k, diff, lim + si * sr)
                run_ref[j, sl, :] = runs[si]
                new_runs.append(runs[si] + jnp.sum(log_keep, axis=1, keepdims=True))
            return tuple(new_runs)

        lax.fori_loop(0, i + 1, sweep1, tuple(jnp.zeros((sr, 1), F32) for _ in strips))

        def sweep2(j, carry):
            run_es, dqs = carry
            off = pl.multiple_of(j * t, t)
            k = k_ref[pl.ds(off, t), :]
            v = v_ref[pl.ds(off, t), :]
            lim = jnp.where(j < i, t, 0)
            new_es, new_dqs = [], []
            dk_t = jnp.zeros((t, HEAD_DIM), F32)
            dv_t = jnp.zeros((t, HEAD_DIM), F32)
            for si, sl in enumerate(strips):
                z, valid, l1p, log_keep = _sb_scores(qs[si], k, diff, lim + si * sr)
                between = _dot2_right(log_keep, after) + run_ref[j, sl, :]
                w = jnp.where(valid, jnp.exp(jnp.minimum(z, 0.0) - l1p + between), 0.0)
                dw = lax.dot_general(dos[si], v, NT, preferred_element_type=F32)
                e = dw * w
                e_before = _dot2_right(e, before) + run_es[si]
                keep = jnp.exp(log_keep)
                dz = jnp.where(valid, e * keep - e_before * (1.0 - keep), 0.0) * scale
                dzb = dz.astype(BF16)
                new_dqs.append(dqs[si] + jnp.dot(dzb, k, preferred_element_type=F32))
                dk_t = dk_t + lax.dot_general(dzb, qs[si], TN, preferred_element_type=F32)
                dv_t = dv_t + lax.dot_general(w.astype(BF16), dos[si], TN, preferred_element_type=F32)
                new_es.append(run_es[si] + jnp.sum(e, axis=1, keepdims=True))
            dk_acc[pl.ds(off, t), :] += dk_t
            dv_acc[pl.ds(off, t), :] += dv_t
            return tuple(new_es), tuple(new_dqs)

        init = (tuple(jnp.zeros((sr, 1), F32) for _ in strips), tuple(jnp.zeros((sr, HEAD_DIM), F32) for _ in strips))
        _, dqs = lax.fori_loop(0, i + 1, sweep2, init)
        for sl, dq in zip(strips, dqs):
            dq_ref[sl, :] = dq.astype(BF16)

        @pl.when(i == nq - 1)
        def _():
            dk_ref[...] = dk_acc[...].astype(BF16)
            dv_ref[...] = dv_acc[...].astype(BF16)

    out = jax.ShapeDtypeStruct((s, n_heads * HEAD_DIM), BF16)
    head_blk = pl.BlockSpec((s, HEAD_DIM), lambda h, i: (0, h))
    tile_blk = pl.BlockSpec((t, HEAD_DIM), lambda h, i: (i, h))
    return pl.pallas_call(
        kern, name="sb_bwd", grid=(n_heads, nq),
        in_specs=_qkv_specs(s, t, n_heads, base) + [tile_blk],
        out_specs=[tile_blk, head_blk, head_blk],
        out_shape=[out, out, out],
        scratch_shapes=[pltpu.VMEM((s, HEAD_DIM), F32), pltpu.VMEM((s, HEAD_DIM), F32), pltpu.VMEM((nq, t, 1), F32)],
        compiler_params=_params("parallel", "arbitrary"),
    )(qkv, qkv, qkv, d_o)


def _fox_scores(q, k, cq, ck, diff, lim):
    sc = lax.dot_general(q, k, NT, preferred_element_type=F32) * (HEAD_DIM ** -0.5)
    sc = sc + cq - ck
    valid = diff < lim
    return jnp.where(valid, sc, NEG_BIG), valid


def fox_fwd(qkv, cum_col, cum_row, n_heads, base):
    s = qkv.shape[0]
    t = _tile(s, ATT_TILE, LANES)
    sr, ns, strips = _strips(t)

    def kern(q_ref, k_ref, v_ref, cq_ref, ck_ref, o_ref, lse_ref):
        i = pl.program_id(1)
        diff = _key_minus_row(sr, t)
        qs = [q_ref[sl, :] for sl in strips]
        cqs = [cq_ref[0, sl, :] for sl in strips]

        def body(j, carry):
            off = pl.multiple_of(j * t, t)
            k = k_ref[pl.ds(off, t), :]
            v = v_ref[pl.ds(off, t), :]
            ck = ck_ref[0, :, pl.ds(off, t)]
            lim = jnp.where(j < i, t, 1)
            out = []
            for si in range(ns):
                m, l, acc = carry[si]
                sc, _ = _fox_scores(qs[si], k, cqs[si], ck, diff, lim + si * sr)
                m_new = jnp.maximum(m, jnp.max(sc, axis=1, keepdims=True))
                p = jnp.exp(sc - m_new)
                alpha = jnp.exp(m - m_new)
                l = alpha * l + jnp.sum(p, axis=1, keepdims=True)
                acc = alpha * acc + jnp.dot(p.astype(BF16), v, preferred_element_type=F32)
                out.append((m_new, l, acc))
            return tuple(out)

        init = tuple((jnp.full((sr, 1), NEG_BIG, F32), jnp.zeros((sr, 1), F32), jnp.zeros((sr, HEAD_DIM), F32))
                     for _ in strips)
        res = lax.fori_loop(0, i + 1, body, init)
        for sl, (m, l, acc) in zip(strips, res):
            o_ref[sl, :] = acc / l
            lse_ref[0, sl, :] = m + jnp.log(l)

    col_blk = pl.BlockSpec((1, t, 1), lambda h, i: (h, i, 0))
    return pl.pallas_call(
        kern, name="fox_fwd", grid=(n_heads, s // t),
        in_specs=_qkv_specs(s, t, n_heads, base) + [col_blk, pl.BlockSpec((1, 1, s), lambda h, i: (h, 0, 0))],
        out_specs=[pl.BlockSpec((t, HEAD_DIM), lambda h, i: (i, h)), col_blk],
        out_shape=[jax.ShapeDtypeStruct((s, n_heads * HEAD_DIM), F32), jax.ShapeDtypeStruct((n_heads, s, 1), F32)],
        compiler_params=_params("parallel", "arbitrary"),
    )(qkv, qkv, qkv, cum_col, cum_row)


def fox_bwd(qkv, cum_col, cum_row, o, d_o, lse, n_heads, base):
    s = qkv.shape[0]
    t = _tile(s, ATT_TILE, LANES)
    nq = s // t
    sr, ns, strips = _strips(t)
    scale = HEAD_DIM ** -0.5

    def kern(q_ref, k_ref, v_ref, cq_ref, ck_ref, o_ref, do_ref, lse_ref,
             dq_ref, dk_ref, dv_ref, dcq_ref, dck_ref, dk_acc, dv_acc, dck_acc):
        i = pl.program_id(1)

        @pl.when(i == 0)
        def _():
            dk_acc[...] = jnp.zeros_like(dk_acc)
            dv_acc[...] = jnp.zeros_like(dv_acc)
            dck_acc[...] = jnp.zeros_like(dck_acc)

        diff = _key_minus_row(sr, t)
        qs = [q_ref[sl, :] for sl in strips]
        dos = [do_ref[sl, :] for sl in strips]
        cqs = [cq_ref[0, sl, :] for sl in strips]
        lses = [lse_ref[0, sl, :] for sl in strips]
        deltas = [jnp.sum(dos[si].astype(F32) * o_ref[sl, :], axis=1, keepdims=True) for si, sl in enumerate(strips)]

        def body(j, carry):
            off = pl.multiple_of(j * t, t)
            k = k_ref[pl.ds(off, t), :]
            v = v_ref[pl.ds(off, t), :]
            ck = ck_ref[0, :, pl.ds(off, t)]
            lim = jnp.where(j < i, t, 1)
            out = []
            dk_t = jnp.zeros((t, HEAD_DIM), F32)
            dv_t = jnp.zeros((t, HEAD_DIM), F32)
            dck_t = jnp.zeros((1, t), F32)
            for si in range(ns):
                dq, dcq = carry[si]
                sc, valid = _fox_scores(qs[si], k, cqs[si], ck, diff, lim + si * sr)
                p = jnp.where(valid, jnp.exp(sc - lses[si]), 0.0)
                dp = lax.dot_general(dos[si], v, NT, preferred_element_type=F32)
                ds = p * (dp - deltas[si])
                dsb = (ds * scale).astype(BF16)
                dq = dq + jnp.dot(dsb, k, preferred_element_type=F32)
                dk_t = dk_t + lax.dot_general(dsb, qs[si], TN, preferred_element_type=F32)
                dv_t = dv_t + lax.dot_general(p.astype(BF16), dos[si], TN, preferred_element_type=F32)
                dck_t = dck_t + jnp.sum(ds, axis=0, keepdims=True)
                out.append((dq, dcq + jnp.sum(ds, axis=1, keepdims=True)))
            dk_acc[pl.ds(off, t), :] += dk_t
            dv_acc[pl.ds(off, t), :] += dv_t
            dck_acc[:, pl.ds(off, t)] -= dck_t
            return tuple(out)

        init = tuple((jnp.zeros((sr, HEAD_DIM), F32), jnp.zeros((sr, 1), F32)) for _ in strips)
        res = lax.fori_loop(0, i + 1, body, init)
        for sl, (dq, dcq) in zip(strips, res):
            dq_ref[sl, :] = dq.astype(BF16)
            dcq_ref[0, sl, :] = dcq

        @pl.when(i == nq - 1)
        def _():
            dk_ref[...] = dk_acc[...].astype(BF16)
            dv_ref[...] = dv_acc[...].astype(BF16)
            dck_ref[0] = dck_acc[...]

    out = jax.ShapeDtypeStruct((s, n_heads * HEAD_DIM), BF16)
    head_blk = pl.BlockSpec((s, HEAD_DIM), lambda h, i: (0, h))
    tile_blk = pl.BlockSpec((t, HEAD_DIM), lambda h, i: (i, h))
    col_blk = pl.BlockSpec((1, t, 1), lambda h, i: (h, i, 0))
    row_blk = pl.BlockSpec((1, 1, s), lambda h, i: (h, 0, 0))
    return pl.pallas_call(
        kern, name="fox_bwd", grid=(n_heads, nq),
        in_specs=_qkv_specs(s, t, n_heads, base) + [col_blk, row_blk, tile_blk, tile_blk, col_blk],
        out_specs=[tile_blk, head_blk, head_blk, col_blk, row_blk],
        out_shape=[out, out, out, jax.ShapeDtypeStruct((n_heads, s, 1), F32),
                   jax.ShapeDtypeStruct((n_heads, 1, s), F32)],
        scratch_shapes=[pltpu.VMEM((s, HEAD_DIM), F32), pltpu.VMEM((s, HEAD_DIM), F32), pltpu.VMEM((1, s), F32)],
        compiler_params=_params("parallel", "arbitrary"),
    )(qkv, qkv, qkv, cum_col, cum_row, o, d_o, lse)


def _place():
    x, y, c = lax.axis_index("x"), lax.axis_index("y"), lax.axis_index("c")
    other_chips = [(1 - x, y), (x, 1 - y), (1 - x, 1 - y)]
    return x, y, c, other_chips


ANY = pl.BlockSpec(memory_space=pl.ANY)


def _remote(src, dst, send_sem, recv_sem, dev):
    return pltpu.make_async_remote_copy(src_ref=src, dst_ref=dst, send_sem=send_sem, recv_sem=recv_sem,
                                        device_id=dev, device_id_type=MESH)


def cast_place(name, ws, chip):
    r = ws[0].shape[1]
    cs = [w.shape[2] for w in ws]
    tr = _tile(r, 256, 16)

    def kern(chip_ref, *refs):
        o_ref = refs[-1]
        off = 0
        for w_ref, c in zip(refs[:-1], cs):
            o_ref[:, off:off + c] = w_ref[...].astype(BF16)
            off += c

    return pl.pallas_call(
        kern, name=name,
        grid_spec=pltpu.PrefetchScalarGridSpec(
            num_scalar_prefetch=1, grid=(r // tr,),
            in_specs=[pl.BlockSpec((None, tr, c), lambda i, chip_ref: (0, i, 0)) for c in cs],
            out_specs=pl.BlockSpec((None, tr, sum(cs)), lambda i, chip_ref: (chip_ref[0], i, 0))),
        out_shape=jax.ShapeDtypeStruct((N_CHIPS, r, sum(cs)), BF16),
        compiler_params=_params("parallel"),
    )(chip, *ws)


HBM = pl.BlockSpec(memory_space=pltpu.HBM)
SEM = pl.BlockSpec(memory_space=pltpu.SEMAPHORE)
SPLIT = pltpu.CompilerParams(has_side_effects=pltpu.SideEffectType.DATAFLOW_SIDE_EFFECTING)


def _in_hbm(a):
    return pltpu.with_memory_space_constraint(a, pltpu.HBM)


def _slab_rows(ref, k, core):
    half = ref.shape[1] // 2
    return ref.at[k, pl.ds(pl.multiple_of(core * half, 16), half)]


def gather_start(name, bufs):
    n = len(bufs)

    def body(*refs):
        ins, send, recv, token = refs[:n], refs[n], refs[n + 1], refs[-1]
        x, y, c, chips = _place()
        me = 2 * x + y
        for a in range(n):
            for j in range(3):
                rows = _slab_rows(ins[a], me, c)
                _remote(rows, rows, send.at[3 * a + j], recv.at[3 * a + j], (chips[j][0], chips[j][1], c)).start()
        token[...] = jnp.zeros_like(token)

    sem = pltpu.SemaphoreType.DMA((3 * n,))
    res = pl.pallas_call(
        body, name=name, in_specs=[HBM] * n, out_specs=[SEM, SEM] + [HBM] * n + [pl.BlockSpec(memory_space=pltpu.VMEM)],
        out_shape=[sem, sem] + [pltpu.HBM(b.shape, b.dtype) for b in bufs] + [jax.ShapeDtypeStruct((8, LANES), F32)],
        input_output_aliases={a: 2 + a for a in range(n)}, compiler_params=SPLIT,
    )(*[_in_hbm(b) for b in bufs])
    return res[0], res[1], res[2:2 + n], res[-1]


def gather_wait(name, bufs, send_sems, recv_sems, after):
    n = len(bufs)

    def body(*refs):
        ins, send, recv = refs[:n], refs[n], refs[n + 1]
        x, y, c, chips = _place()
        me = 2 * x + y
        for a in range(n):
            for j in range(3):
                dev = (chips[j][0], chips[j][1], c)
                mine = _slab_rows(ins[a], me, c)
                _remote(mine, mine, send.at[3 * a + j], recv.at[3 * a + j], dev).wait_send()
                land = _slab_rows(ins[a], 2 * chips[j][0] + chips[j][1], c)
                _remote(land, land, send.at[3 * a + j], recv.at[3 * a + j], dev).wait_recv()

    return pl.pallas_call(
        body, name=name, in_specs=[HBM] * n + [SEM, SEM] + [ANY] * len(after), out_specs=[HBM] * n,
        out_shape=[pltpu.HBM(b.shape, b.dtype) for b in bufs],
        input_output_aliases={a: a for a in range(n)}, compiler_params=SPLIT,
    )(*bufs, send_sems, recv_sems, *after)


def gather_forward(name, bufs):
    n = len(bufs)

    def body(*refs):
        outs = refs[n:2 * n]
        send_sems, recv_sems = refs[2 * n:]
        x, y, c, chips = _place()
        sibling = (x, y, 1 - c)

        def d2d(a, j, core):
            rows = _slab_rows(outs[a], 2 * chips[j][0] + chips[j][1], core)
            return _remote(rows, rows, send_sems.at[3 * a + j], recv_sems.at[3 * a + j], sibling)

        pairs = [(a, j) for a in range(n) for j in range(3)]
        for a, j in pairs:
            d2d(a, j, c).start()
        for a, j in pairs:
            d2d(a, j, 1 - c).wait_recv()
        for a, j in pairs:
            d2d(a, j, c).wait_send()

    return pl.pallas_call(
        body, name=name, in_specs=[ANY] * n, out_specs=[ANY] * n,
        out_shape=[jax.ShapeDtypeStruct(b.shape, b.dtype) for b in bufs],
        input_output_aliases={a: a for a in range(n)},
        scratch_shapes=[pltpu.SemaphoreType.DMA((3 * n,)), pltpu.SemaphoreType.DMA((3 * n,))],
    )(*bufs)


def swap_halves(name, pieces):
    n = len(pieces)
    halves = [p.shape[1] // 2 for p in pieces]

    def body(*refs):
        ins, outs = refs[:n], refs[n:2 * n]
        send_sems, recv_sems = refs[2 * n:]
        x, y, c, _ = _place()
        cps = [_remote(ins[a].at[:, pl.ds(pl.multiple_of((1 - c) * halves[a], 16), halves[a]), :], outs[a],
                       send_sems.at[a], recv_sems.at[a], (x, y, 1 - c)) for a in range(n)]
        for cp in cps:
            cp.start()
        for cp in cps:
            cp.wait()

    return pl.pallas_call(
        body, name=name, in_specs=[ANY] * n, out_specs=[ANY] * n,
        out_shape=[jax.ShapeDtypeStruct((N_CHIPS, h, p.shape[2]), p.dtype) for p, h in zip(pieces, halves)],
        scratch_shapes=[pltpu.SemaphoreType.DMA((n,)), pltpu.SemaphoreType.DMA((n,))],
    )(*pieces)


def pair_sum(name, pieces, got, core):
    _, r, w = pieces.shape
    half = r // 2
    tr = _tile(half, 256, 16)

    def kern(core_ref, p_ref, g_ref, o_ref):
        o_ref[...] = (p_ref[...].astype(F32) + g_ref[...].astype(F32)).astype(o_ref.dtype)

    return pl.pallas_call(
        kern, name=name,
        grid_spec=pltpu.PrefetchScalarGridSpec(
            num_scalar_prefetch=1, grid=(N_CHIPS, half // tr),
            in_specs=[pl.BlockSpec((None, None, tr, w), lambda k, i, core_ref: (k, core_ref[0], i, 0)),
                      pl.BlockSpec((None, tr, w), lambda k, i, core_ref: (k, i, 0))],
            out_specs=pl.BlockSpec((None, tr, w), lambda k, i, core_ref: (k, i, 0))),
        out_shape=jax.ShapeDtypeStruct((N_CHIPS, half, w), pieces.dtype),
        compiler_params=_params("parallel", "parallel"),
    )(core, pieces.reshape(N_CHIPS, 2, half, w), got)


def _scatter_copies(sums, lands, send, recv):
    x, y, c, chips = _place()
    return [_remote(sums[a].at[2 * chips[j][0] + chips[j][1]], lands[a].at[j], send.at[3 * a + j], recv.at[3 * a + j],
                    (chips[j][0], chips[j][1], c)) for a in range(len(sums)) for j in range(3)]


def scatter_start(name, sums):
    n = len(sums)
    lands = [lax.empty((3,) + t.shape[1:], t.dtype) for t in sums]

    def body(*refs):
        ins, land_in, send, recv, token = refs[:n], refs[n:2 * n], refs[2 * n], refs[2 * n + 1], refs[-1]
        for cp in _scatter_copies(ins, land_in, send, recv):
            cp.start()
        token[...] = jnp.zeros_like(token)

    sem = pltpu.SemaphoreType.DMA((3 * n,))
    res = pl.pallas_call(
        body, name=name, in_specs=[HBM] * (2 * n),
        out_specs=[SEM, SEM] + [HBM] * (2 * n) + [pl.BlockSpec(memory_space=pltpu.VMEM)],
        out_shape=[sem, sem] + [pltpu.HBM(t.shape, t.dtype) for t in sums + lands] + [jax.ShapeDtypeStruct((8, LANES), F32)],
        input_output_aliases={a: 2 + a for a in range(2 * n)}, compiler_params=SPLIT,
    )(*[_in_hbm(t) for t in sums + lands])
    return res[0], res[1], res[2:2 + n], res[2 + n:2 + 2 * n], res[-1]


def scatter_wait(name, sums, lands, send_sems, recv_sems, after):
    n = len(sums)

    def body(*refs):
        ins, land_in, send, recv = refs[:n], refs[n:2 * n], refs[2 * n], refs[2 * n + 1]
        for cp in _scatter_copies(ins, land_in, send, recv):
            cp.wait_send()
            cp.wait_recv()

    res = pl.pallas_call(
        body, name=name, in_specs=[HBM] * (2 * n) + [SEM, SEM, ANY], out_specs=[HBM] * (2 * n),
        out_shape=[pltpu.HBM(t.shape, t.dtype) for t in sums + lands],
        input_output_aliases={a: a for a in range(2 * n)}, compiler_params=SPLIT,
    )(*sums, *lands, send_sems, recv_sems, after)
    return res[:n], res[n:]


def chip_sum(name, sums, got, chip, core):
    _, half, w = sums.shape
    tr = _tile(half, 256, 16)
    nb = half // tr

    def kern(ids_ref, s_ref, g0_ref, g1_ref, g2_ref, o_ref):
        o_ref[...] = ((s_ref[...].astype(F32) + g0_ref[...].astype(F32)) + g1_ref[...].astype(F32)) \
            + g2_ref[...].astype(F32)

    def got_spec(j):
        return pl.BlockSpec((None, tr, w), lambda i, ids_ref: (j, i, 0))

    return pl.pallas_call(
        kern, name=name,
        grid_spec=pltpu.PrefetchScalarGridSpec(
            num_scalar_prefetch=1, grid=(nb,),
            in_specs=[pl.BlockSpec((None, tr, w), lambda i, ids_ref: (ids_ref[0], i, 0)),
                      got_spec(0), got_spec(1), got_spec(2)],
            out_specs=pl.BlockSpec((tr, w), lambda i, ids_ref: (ids_ref[1] * nb + i, 0))),
        out_shape=jax.ShapeDtypeStruct((2 * half, w), F32),
        compiler_params=_params("parallel"),
    )(jnp.concatenate([chip, core]), sums, got, got, got)


def join_halves(name, shards):
    n = len(shards)
    halves = [g.shape[0] // 2 for g in shards]

    def body(*refs):
        outs = refs[n:2 * n]
        send_sems, recv_sems = refs[2 * n:]
        x, y, c, _ = _place()
        cps = []
        for a in range(n):
            rows = outs[a].at[pl.ds(pl.multiple_of(c * halves[a], 8), halves[a])]
            cps.append(_remote(rows, rows, send_sems.at[a], recv_sems.at[a], (x, y, 1 - c)))
        for cp in cps:
            cp.start()
        for cp in cps:
            cp.wait()

    return pl.pallas_call(
        body, name=name, in_specs=[ANY] * n, out_specs=[ANY] * n,
        out_shape=[jax.ShapeDtypeStruct(g.shape, g.dtype) for g in shards],
        input_output_aliases={a: a for a in range(n)},
        scratch_shapes=[pltpu.SemaphoreType.DMA((n,)), pltpu.SemaphoreType.DMA((n,))],
    )(*shards)


def _adam(w, g, m, v):
    m = ADAM_B1 * m + (1.0 - ADAM_B1) * g
    v = ADAM_B2 * v + (1.0 - ADAM_B2) * (g * g)
    m_hat = m / (1.0 - ADAM_B1 ** ADAM_STEP)
    v_hat = v / (1.0 - ADAM_B2 ** ADAM_STEP)
    delta = -ADAM_LR * (m_hat / (jnp.sqrt(v_hat) + ADAM_EPS) + ADAM_WD * w)
    return delta, m, v


def small_allreduce_adam(g_part, w, m, v):
    n_dev = 8
    r, d = g_part.shape

    def body(g_ref, w_ref, m_ref, v_ref, gs_ref, dl_ref, nm_ref, nv_ref, all_ref, send_sems, recv_sems):
        x, y, c, _ = _place()
        me = 4 * x + 2 * y + c
        all_ref[me] = g_ref[...]
        cps = []
        for rel in range(1, n_dev):
            px = 1 - x if rel & 4 else x
            py = 1 - y if rel & 2 else y
            pc = 1 - c if rel & 1 else c
            cps.append(_remote(g_ref, all_ref.at[me], send_sems.at[rel - 1], recv_sems.at[rel - 1], (px, py, pc)))
        for cp in cps:
            cp.start()
        for cp in cps:
            cp.wait()
        total = all_ref[0]
        for dev in range(1, n_dev):
            total = total + all_ref[dev]
        gs_ref[...] = total
        delta, nm, nv = _adam(w_ref[...], total, m_ref[...], v_ref[...])
        dl_ref[...] = delta
        nm_ref[...] = nm
        nv_ref[...] = nv

    vm = pl.BlockSpec(memory_space=pltpu.VMEM)
    out = jax.ShapeDtypeStruct((r, d), F32)
    return pl.pallas_call(
        body, name="small_allreduce_adam", in_specs=[vm, vm, vm, vm], out_specs=[vm, vm, vm, vm],
        out_shape=[out, out, out, out],
        scratch_shapes=[pltpu.VMEM((n_dev, r, d), F32), pltpu.SemaphoreType.DMA((n_dev - 1,)),
                        pltpu.SemaphoreType.DMA((n_dev - 1,))],
    )(g_part, w, m, v)


def adam_update(name, w, m, v, g_buf, col_blk, after=None):
    w, m, v = w[0], m[0], v[0]
    r, c = w.shape
    tr = _tile(r, 128, 8)
    extra = [] if after is None else [after]

    def kern(w_ref, m_ref, v_ref, g_ref, *rest):
        go_ref, dl_ref, nm_ref, nv_ref = rest[len(extra):]
        g = g_ref[...]
        delta, nm, nv = _adam(w_ref[...], g, m_ref[...], v_ref[...])
        go_ref[...] = g
        dl_ref[...] = delta
        nm_ref[...] = nm
        nv_ref[...] = nv

    blk = pl.BlockSpec((tr, c), lambda i: (i, 0))
    out = jax.ShapeDtypeStruct((r, c), F32)
    res = pl.pallas_call(
        kern, name=name, grid=(r // tr,),
        in_specs=[blk, blk, blk, pl.BlockSpec((tr, c), lambda i: (i, col_blk))] + [ANY] * len(extra),
        out_specs=[blk] * 4, out_shape=[out] * 4, compiler_params=_params("parallel"),
    )(w, m, v, g_buf, *extra)
    return [a[None] for a in res]


def _w_in_segments(cw, n_qkv, n_heads, d):
    out = []

    def add(lo, hi, main):
        while lo < hi:
            k, a = divmod(lo, cw)
            w = min(cw - a, hi - lo)
            out.append((k, a, main, w))
            lo, main = lo + w, main + w

    add(0, n_qkv, 0)
    add(n_qkv + n_heads, N_CHIPS * cw, n_qkv)
    add(n_qkv, n_qkv + n_heads, n_qkv + 2 * d)
    return out


def regroup_w_in(g_in, segments, n_main):
    _, d, cw = g_in.shape
    tr = _tile(d, 128, 16)
    n_real = max(m + w for _, _, m, w in segments)

    def kern(s_ref, o_ref):
        for k, a, m, w in segments:
            o_ref[:, m:m + w] = s_ref[k, :, a:a + w]
        o_ref[:, n_real:] = jnp.zeros((tr, n_main - n_real), o_ref.dtype)

    return pl.pallas_call(
        kern, name="regroup_w_in", grid=(d // tr,),
        in_specs=[pl.BlockSpec((N_CHIPS, tr, cw), lambda i: (0, i, 0))],
        out_specs=pl.BlockSpec((tr, n_main), lambda i: (i, 0)),
        out_shape=jax.ShapeDtypeStruct((d, n_main), g_in.dtype), compiler_params=_params("parallel"),
    )(g_in)


def regroup_dw_in(dw_main, segments, cw):
    d, n_main = dw_main.shape
    tr = _tile(d, 128, 16)

    def kern(s_ref, o_ref):
        for k, a, m, w in segments:
            o_ref[k, :, a:a + w] = s_ref[:, m:m + w]

    return pl.pallas_call(
        kern, name="regroup_dw_in", grid=(d // tr,),
        in_specs=[pl.BlockSpec((tr, n_main), lambda i: (i, 0))],
        out_specs=pl.BlockSpec((N_CHIPS, tr, cw), lambda i: (0, i, 0)),
        out_shape=jax.ShapeDtypeStruct((N_CHIPS, d, cw), dw_main.dtype), compiler_params=_params("parallel"),
    )(dw_main)


def kernel(x, norm_mix_pre, norm_mix_post, w_in, b_forget, w_branch_sb, w_branch_fox, w_out, norm_ffn_pre, norm_ffn_post, w_ffn_gate, w_ffn_up, w_ffn_down, loss_target, m_norm_mix_pre, m_norm_mix_post, m_w_in, m_b_forget, m_w_branch_sb, m_w_branch_fox, m_w_out, m_norm_ffn_pre, m_norm_ffn_post, m_w_ffn_gate, m_w_ffn_up, m_w_ffn_down, v_norm_mix_pre, v_norm_mix_post, v_w_in, v_b_forget, v_w_branch_sb, v_w_branch_fox, v_w_out, v_norm_ffn_pre, v_norm_ffn_post, v_w_ffn_gate, v_w_ffn_up, v_w_ffn_down):
    s, d = x.shape[1], x.shape[2]
    n_heads = b_forget.shape[1]
    d_att = n_heads * HEAD_DIM
    c_in = w_in.shape[2]
    c_br = w_branch_sb.shape[2]
    c_gu = w_ffn_gate.shape[2]
    d_ff = c_gu * N_CHIPS
    d_in = c_in * N_CHIPS
    f_pad = 512
    n_qkv = 6 * d_att
    n_gf = 2 * d + f_pad
    core = lax.axis_index("c").astype(jnp.int32).reshape(1)
    chip = (2 * lax.axis_index("x") + lax.axis_index("y")).astype(jnp.int32).reshape(1)

    in_send, in_recv, in_bufs, in_token = gather_start("gather_start_w_in", [cast_place("place_w_in", [w_in], chip)])
    ag_send, ag_recv, ag_bufs, ag_token = gather_start("gather_start_rest", [
        cast_place("place_branch", [w_branch_sb, w_branch_fox], chip),
        cast_place("place_out", [w_out + in_token[0, 0]], chip),
        cast_place("place_gate_up", [w_ffn_gate, w_ffn_up], chip),
        cast_place("place_down", [w_ffn_down], chip)])
    g_in, = gather_forward("forward_w_in", gather_wait("gather_wait_w_in", in_bufs, in_send, in_recv,
                                                       [ag_token, m_w_in[0], v_w_in[0]]))
    segments = _w_in_segments(c_in, n_qkv, n_heads, d)
    w_main = regroup_w_in(g_in, segments, n_qkv + n_gf)
    x2 = x[0]
    tgt = loss_target[0]
    b_pad = jnp.pad(b_forget, ((0, 0), (0, LANES - n_heads)))

    u = norm_in(x2, norm_mix_pre)
    qkv = mm(u, w_main, "nn", BF16, "proj_qkv", b_win=(0, n_qkv))
    gf = mm(u, w_main, "nn", F32, "proj_gates", b_win=(n_qkv, n_gf))
    cum = cum_fwd(gf, b_pad, 2 * d)
    cum_heads = cum[:, :n_heads].T
    cum_col, cum_row = cum_heads[:, :, None], cum_heads[:, None, :]
    o_sb = sb_fwd(qkv, n_heads, 0)
    o_fx, lse = fox_fwd(qkv, cum_col, cum_row, n_heads, 3 * n_heads)
    g_br, g_out, g_gu, g_dn = gather_forward(
        "forward_rest", gather_wait("gather_wait_rest", ag_bufs, ag_send, ag_recv, [o_sb, o_fx]))
    w_o = g_out.reshape(d, d)
    w_dn = g_dn.reshape(d_ff, d)
    bsb = mm(o_sb, g_br, "nn", F32, "branch_sb", tn=c_br, chunks=(1, 0))
    bfx = mm(o_fx, g_br, "nn", F32, "branch_fox", tn=c_br, chunks=(1, 1))
    merged = gate_fwd(bsb, bfx, gf)
    mix = mm(merged, w_o, "nn", F32, "out_proj")
    h1, u2 = mid_fwd(x2, mix, norm_mix_post, norm_ffn_pre)
    gu = mm(u2, g_gu, "nn", F32, "ffn_gate_up", tn=c_gu, chunks=(2, 0))
    act = swiglu_fwd(gu, c_gu)
    ff = mm(act, w_dn, "nn", F32, "ffn_down")
    dy, d_ff_out, dg_fpost, loss_part = loss_head(h1, ff, norm_ffn_post, tgt)

    p_dn = mm(act, d_ff_out, "tn", BF16, "dw_ffn_down").reshape(N_CHIPS, d_ff // N_CHIPS, d)
    d_act = mm(d_ff_out, w_dn, "nt", F32, "d_act")
    d_gu = swiglu_bwd(d_act, gu, c_gu)
    p_gu = mm(u2, d_gu, "tn", BF16, "dw_ffn_gate_up", tn=c_gu, chunks=(2, 0),
              out_into=lax.empty((N_CHIPS, d, 2 * c_gu), BF16))
    du2 = mm(d_gu, g_gu, "nt", F32, "d_u2", tk=c_gu, chunks=(2, 0))
    dh1, d_mix, dg_fpre, dg_post = mid_bwd(dy, du2, h1, mix, norm_ffn_pre, norm_mix_post)
    p_out = mm(merged, d_mix, "tn", BF16, "dw_out").reshape(N_CHIPS, d // N_CHIPS, d)
    d_merged = mm(d_mix, w_o, "nt", F32, "d_merged")
    d_bsb, d_bfx, d_gs, d_gx = gate_bwd(d_merged, bsb, bfx, gf)
    p_br = mm(o_sb, d_bsb, "tn", BF16, "dw_branch_sb", tn=c_br, chunks=(1, 0),
              out_into=lax.empty((N_CHIPS, d_att, 2 * c_br), BF16))
    p_br = mm(o_fx, d_bfx, "tn", BF16, "dw_branch_fox", tn=c_br, chunks=(1, 1), out_into=p_br)
    d_osb = mm(d_bsb, g_br, "nt", BF16, "d_o_sb", tk=c_br, chunks=(1, 0))
    d_ofx = mm(d_bfx, g_br, "nt", BF16, "d_o_fox", tk=c_br, chunks=(1, 1))

    def reduce_start(tag, pieces, names):
        from_sibling = swap_halves("swap_halves_" + tag, pieces)
        sums = [pair_sum("pair_sum_" + t, p, q, core) for t, p, q in zip(names, pieces, from_sibling)]
        return scatter_start("scatter_start_" + tag, sums)

    def reduce_end(tag, started, names, after):
        send, recv, sums, lands, _ = started
        sums, lands = scatter_wait("scatter_wait_" + tag, sums, lands, send, recv, after)
        return join_halves("join_halves_" + tag, [chip_sum("chip_sum_" + t, sm, got, chip, core)
                                                  for t, sm, got in zip(names, sums, lands)])

    rest_names = ["branch", "out", "gate_up", "down"]
    rest_started = reduce_start("rest", [p_br, p_out, p_gu, p_dn], rest_names)
    d_osb = d_osb + rest_started[4][0, 0].astype(BF16)
    dq_s, dk_s, dv_s = sb_bwd(qkv, d_osb, n_heads, 0)
    dq_f, dk_f, dv_f, dcq, dck = fox_bwd(qkv, cum_col, cum_row, o_fx, d_ofx, lse, n_heads, 3 * n_heads)
    d_cum = jnp.pad((dcq[:, :, 0] + dck[:, 0, :]).T, ((0, 0), (0, LANES - n_heads)))
    d_f, db_pad = cum_bwd(d_cum, gf, b_pad, 2 * d, n_heads)
    d_main = jnp.concatenate(
        [dq_s, dk_s, dv_s, dq_f, dk_f, dv_f, d_gs, d_gx, d_f, jnp.zeros((s, f_pad - LANES), BF16)], axis=1)
    p_in = regroup_dw_in(mm(u, d_main, "tn", BF16, "dw_in"), segments, c_in)

    in_started = reduce_start("w_in", [p_in], ["in"])
    du = mm(d_main, w_main, "nt", F32, "d_u", after=in_started[4])
    dx, dg_pre = in_bwd(dh1, du, x2, norm_mix_pre + in_started[4][0:1, 0:1])
    gr_br, gr_out, gr_gu, gr_dn = reduce_end("rest", rest_started, rest_names, dx)

    upd_bs = adam_update("adam_branch_sb", w_branch_sb, m_w_branch_sb, v_w_branch_sb, gr_br, 0)
    upd_bf = adam_update("adam_branch_fox", w_branch_fox, m_w_branch_fox, v_w_branch_fox, gr_br, 1)
    upd_o = adam_update("adam_out", w_out, m_w_out, v_w_out, gr_out, 0)

    def pack(rows):
        rows = [jnp.pad(r_, ((0, 0), (0, d - r_.shape[1]))) for r_ in rows]
        return jnp.concatenate(rows + [jnp.zeros((8 - len(rows), d), F32)], axis=0)

    sm_g, sm_d, sm_m, sm_v = small_allreduce_adam(
        pack([dg_pre, dg_post, dg_fpre, dg_fpost, db_pad]),
        pack([norm_mix_pre, norm_mix_post, norm_ffn_pre, norm_ffn_post, b_forget]),
        pack([m_norm_mix_pre, m_norm_mix_post, m_norm_ffn_pre, m_norm_ffn_post, m_b_forget]),
        pack([v_norm_mix_pre, v_norm_mix_post, v_norm_ffn_pre, v_norm_ffn_post, v_b_forget]))

    done = sm_d[0:1, 0:1] + sum(u_[1][0, 0:1, 0:1] for u_ in (upd_bs, upd_bf, upd_o))
    gr_in, = reduce_end("w_in", in_started, ["in"], done)
    upd_in = adam_update("adam_w_in", w_in, m_w_in, v_w_in, gr_in, 0)
    upd_ga = adam_update("adam_gate", w_ffn_gate, m_w_ffn_gate, v_w_ffn_gate, gr_gu, 0, after=upd_in[1])
    upd_up = adam_update("adam_up", w_ffn_up, m_w_ffn_up, v_w_ffn_up, gr_gu, 1, after=upd_in[1])
    upd_dn = adam_update("adam_down", w_ffn_down, m_w_ffn_down, v_w_ffn_down, gr_dn, 0, after=upd_in[1])
    grads, deltas, new_ms, new_vs = zip(upd_in, upd_bs, upd_bf, upd_o, upd_ga, upd_up, upd_dn)

    def small(a):
        return [a[0:1], a[1:2], a[2:3], a[3:4], a[4:5, :n_heads]]

    def ordered(sm, bg):
        return [sm[0], sm[1], bg[0], sm[4], bg[1], bg[2], bg[3], sm[2], sm[3], bg[4], bg[5], bg[6]]

    loss = lax.psum(loss_part[0, 0], ("x", "y", "c"))
    return (loss, dx[None], *ordered(small(sm_g), grads), *ordered(small(sm_d), deltas),
            *ordered(small(sm_m), new_ms), *ordered(small(sm_v), new_vs))
```

```python
import functools

import jax
import jax.numpy as jnp
from jax import lax
from jax.experimental import pallas as pl
from jax.experimental.pallas import tpu as pltpu

F32 = jnp.float32
BF16 = jnp.bfloat16
MESH = pl.DeviceIdType.MESH

HEAD_DIM = 128
LANES = 128
ATT_TILE = 512
ROW_TILE = 256
N_CHIPS = 4
RMS_EPS = 1e-6
ADAM_LR = 0.001
ADAM_B1 = 0.9
ADAM_B2 = 0.999
ADAM_EPS = 1e-08
ADAM_WD = 0.01
ADAM_STEP = 10
NEG_BIG = -1e30
VMEM_LIMIT = 56 * 1024 * 1024
MM_VMEM_BUDGET = 40 * 1024 * 1024
ATT_STRIP = 512

NN = (((1,), (0,)), ((), ()))
NT = (((1,), (1,)), ((), ()))
TN = (((0,), (0,)), ((), ()))


def _tile(n, pref, align):
    best = None
    t = align
    while t <= min(n, pref):
        if n % t == 0:
            best = t
        t += align
    return n if best is None else best


def _params(*sem):
    return pltpu.CompilerParams(dimension_semantics=sem, vmem_limit_bytes=VMEM_LIMIT)


def _mm_tiles(m, n, k, a_bytes, b_bytes, out_bytes, tn, tk):
    tm = _tile(m, 2048, LANES)
    tk = tk or _tile(k, 512, LANES)

    def vmem(t):
        acc = 0 if out_bytes == 4 else tm * t * 4
        return acc + 2 * tm * t * out_bytes + 2 * (tm * tk * a_bytes + tk * t * b_bytes)

    if tn is None:
        fits = [t for t in range(LANES, min(n, 2048) + 1, LANES) if n % t == 0 and vmem(t) <= MM_VMEM_BUDGET]
        tn = max(fits) if fits else _tile(n, LANES, LANES)
    return tm, tn, tk


def mm(a, b, mode, out_dtype, name, *, tn=None, tk=None, b_win=None, chunks=None, out_into=None, after=None):
    n_per, blk0 = chunks if chunks else (1, 0)
    if mode == "nn":
        m, k = a.shape
        n = b.shape[0] * n_per * tn if chunks else (b_win[1] if b_win else b.shape[1])
    elif mode == "nt":
        m = a.shape[0]
        k = b.shape[0] * n_per * tk if chunks else a.shape[1]
        n = b.shape[-2]
    else:
        k, m = a.shape
        n = b.shape[1]
    in_place = jnp.dtype(out_dtype) == jnp.dtype(F32)
    tm, tn, tk = _mm_tiles(m, n, k, a.dtype.itemsize, b.dtype.itemsize, jnp.dtype(out_dtype).itemsize, tn, tk)
    assert m % tm == 0 and n % tn == 0 and k % tk == 0, (name, m, n, k, tm, tn, tk)
    j0 = 0
    if b_win:
        assert b_win[0] % tn == 0
        j0 = b_win[0] // tn
    nk = k // tk
    dims = {"nn": NN, "nt": NT, "tn": TN}[mode]

    def kern(a_ref, b_ref, *rest):
        o_ref, acc_ref = (rest[-1], rest[-1]) if in_place else (rest[-2], rest[-1])
        kk = pl.program_id(2)

        @pl.when(kk == 0)
        def _():
            acc_ref[...] = jnp.zeros_like(acc_ref)

        acc_ref[...] += lax.dot_general(a_ref[...].astype(BF16), b_ref[...].astype(BF16), dims,
                                        preferred_element_type=F32)

        if not in_place:
            @pl.when(kk == nk - 1)
            def _():
                o_ref[...] = acc_ref[...].astype(o_ref.dtype)

    out_spec = pl.BlockSpec((tm, tn), lambda i, j, kk: (i, j))
    out_shape = jax.ShapeDtypeStruct((m, n), out_dtype)
    if mode == "nn":
        a_spec = pl.BlockSpec((tm, tk), lambda i, j, kk: (i, kk))
        if chunks:
            b_spec = pl.BlockSpec((None, tk, tn), lambda i, j, kk: (j // n_per, kk, blk0 + j % n_per))
        else:
            b_spec = pl.BlockSpec((tk, tn), lambda i, j, kk: (kk, j + j0))
    elif mode == "nt":
        a_spec = pl.BlockSpec((tm, tk), lambda i, j, kk: (i, kk))
        if chunks:
            b_spec = pl.BlockSpec((None, tn, tk), lambda i, j, kk: (kk // n_per, j, blk0 + kk % n_per))
        else:
            b_spec = pl.BlockSpec((tn, tk), lambda i, j, kk: (j, kk))
    else:
        a_spec = pl.BlockSpec((tk, tm), lambda i, j, kk: (kk, i))
        b_spec = pl.BlockSpec((tk, tn), lambda i, j, kk: (kk, j))
        if chunks:
            out_spec = pl.BlockSpec((None, tm, tn), lambda i, j, kk: (j // n_per, i, blk0 + j % n_per))
    in_specs, operands, aliases = [a_spec, b_spec], [a, b], {}
    if chunks and mode == "tn":
        assert out_into is not None
        out_shape = jax.ShapeDtypeStruct(out_into.shape, out_dtype)
        in_specs.append(pl.BlockSpec(memory_space=pl.ANY))
        operands.append(out_into)
        aliases = {2: 0}
    if after is not None:
        in_specs.append(pl.BlockSpec(memory_space=pl.ANY))
        operands.append(after)
    return pl.pallas_call(
        kern, name=name, grid=(m // tm, n // tn, nk),
        in_specs=in_specs, out_specs=out_spec, out_shape=out_shape,
        scratch_shapes=[] if in_place else [pltpu.VMEM((tm, tn), F32)], input_output_aliases=aliases,
        compiler_params=_params("parallel", "parallel", "arbitrary"),
    )(*operands)


def _rstd(v):
    return lax.rsqrt(jnp.mean(v * v, axis=-1, keepdims=True) + RMS_EPS)


def _norm_bwd(v, g, dy):
    r = _rstd(v)
    vh = v * r
    dyg = dy * g
    dv = r * (dyg - vh * jnp.mean(dyg * vh, axis=-1, keepdims=True))
    return dv, jnp.sum(dy * vh, axis=0, keepdims=True)


def _row_call(kern, name, ins, outs, s, d):
    tr = _tile(s, ROW_TILE, 16)

    def spec(shape, is_row):
        if is_row:
            return pl.BlockSpec((tr, shape[1]), lambda i: (i, 0))
        return pl.BlockSpec(shape, lambda i: (0, 0))

    return pl.pallas_call(
        kern, name=name, grid=(s // tr,),
        in_specs=[spec(a.shape, r) for a, r in ins],
        out_specs=[spec(sh, r) for sh, _, r in outs],
        out_shape=[jax.ShapeDtypeStruct(sh, dt) for sh, dt, _ in outs],
        compiler_params=_params("arbitrary"),
    )(*[a for a, _ in ins])


def norm_in(x, g):
    s, d = x.shape

    def kern(x_ref, g_ref, u_ref):
        v = x_ref[...]
        u_ref[...] = (v * _rstd(v) * g_ref[...]).astype(BF16)

    return _row_call(kern, "norm_in", [(x, True), (g, False)], [((s, d), BF16, True)], s, d)[0]


def mid_fwd(x, mix, g_post, g_fpre):
    s, d = x.shape

    def kern(x_ref, mix_ref, gp_ref, gf_ref, h1_ref, u2_ref):
        mixv = mix_ref[...]
        h1 = x_ref[...] + mixv * _rstd(mixv) * gp_ref[...]
        h1_ref[...] = h1
        u2_ref[...] = (h1 * _rstd(h1) * gf_ref[...]).astype(BF16)

    return _row_call(kern, "mid_fwd", [(x, True), (mix, True), (g_post, False), (g_fpre, False)],
                     [((s, d), F32, True), ((s, d), BF16, True)], s, d)


def loss_head(h1, ff, g_fpost, target):
    s, d = h1.shape

    def kern(h1_ref, ff_ref, g_ref, t_ref, dy_ref, dff_ref, dg_ref, loss_ref):
        @pl.when(pl.program_id(0) == 0)
        def _():
            dg_ref[...] = jnp.zeros_like(dg_ref)
            loss_ref[...] = jnp.zeros_like(loss_ref)

        ffv = ff_ref[...]
        g = g_ref[...]
        y = h1_ref[...] + ffv * _rstd(ffv) * g
        diff = y - t_ref[...]
        row_loss = jnp.mean(diff * diff, axis=-1, keepdims=True)
        loss_ref[...] += 0.5 * jnp.sum(row_loss, axis=0, keepdims=True)
        dy = diff / d
        dy_ref[...] = dy
        dff, dg = _norm_bwd(ffv, g, dy)
        dff_ref[...] = dff.astype(BF16)
        dg_ref[...] += dg

    return _row_call(kern, "loss_head",
                     [(h1, True), (ff, True), (g_fpost, False), (target, True)],
                     [((s, d), F32, True), ((s, d), BF16, True), ((1, d), F32, False), ((1, 1), F32, False)], s, d)


def mid_bwd(dy, du2, h1, mix, g_fpre, g_post):
    s, d = dy.shape

    def kern(dy_ref, du2_ref, h1_ref, mix_ref, gf_ref, gp_ref, dh1_ref, dmix_ref, dgf_ref, dgp_ref):
        @pl.when(pl.program_id(0) == 0)
        def _():
            dgf_ref[...] = jnp.zeros_like(dgf_ref)
            dgp_ref[...] = jnp.zeros_like(dgp_ref)

        dh, dgf = _norm_bwd(h1_ref[...], gf_ref[...], du2_ref[...])
        dh1 = dy_ref[...] + dh
        dh1_ref[...] = dh1
        dmix, dgp = _norm_bwd(mix_ref[...], gp_ref[...], dh1)
        dmix_ref[...] = dmix.astype(BF16)
        dgf_ref[...] += dgf
        dgp_ref[...] += dgp

    return _row_call(kern, "mid_bwd",
                     [(dy, True), (du2, True), (h1, True), (mix, True), (g_fpre, False), (g_post, False)],
                     [((s, d), F32, True), ((s, d), BF16, True), ((1, d), F32, False), ((1, d), F32, False)], s, d)


def in_bwd(dh1, du, x, g_pre):
    s, d = x.shape

    def kern(dh1_ref, du_ref, x_ref, g_ref, dx_ref, dg_ref):
        @pl.when(pl.program_id(0) == 0)
        def _():
            dg_ref[...] = jnp.zeros_like(dg_ref)

        dxn, dg = _norm_bwd(x_ref[...], g_ref[...], du_ref[...])
        dx_ref[...] = dh1_ref[...] + dxn
        dg_ref[...] += dg

    return _row_call(kern, "in_bwd", [(dh1, True), (du, True), (x, True), (g_pre, False)],
                     [((s, d), F32, True), ((1, d), F32, False)], s, d)


def _sigmoid(v):
    return 1.0 / (1.0 + jnp.exp(-v))


def gate_fwd(bsb, bfx, gf):
    s, d = bsb.shape
    tr, tc = _tile(s, 256, 16), _tile(d, 512, LANES)
    nc = d // tc

    def kern(bsb_ref, bfx_ref, gs_ref, gx_ref, o_ref):
        o_ref[...] = (_sigmoid(gs_ref[...]) * bsb_ref[...] + _sigmoid(gx_ref[...]) * bfx_ref[...]).astype(BF16)

    blk = pl.BlockSpec((tr, tc), lambda i, j: (i, j))
    return pl.pallas_call(
        kern, name="gate_fwd", grid=(s // tr, nc),
        in_specs=[blk, blk, blk, pl.BlockSpec((tr, tc), lambda i, j: (i, j + nc))],
        out_specs=blk, out_shape=jax.ShapeDtypeStruct((s, d), BF16),
        compiler_params=_params("parallel", "parallel"),
    )(bsb, bfx, gf, gf)


def gate_bwd(dmerged, bsb, bfx, gf):
    s, d = bsb.shape
    tr, tc = _tile(s, 256, 16), _tile(d, 512, LANES)
    nc = d // tc

    def kern(dm_ref, bsb_ref, bfx_ref, gs_ref, gx_ref, dbs_ref, dbx_ref, dgs_ref, dgx_ref):
        dm = dm_ref[...]
        ss = _sigmoid(gs_ref[...])
        sx = _sigmoid(gx_ref[...])
        dbs_ref[...] = (dm * ss).astype(BF16)
        dbx_ref[...] = (dm * sx).astype(BF16)
        dgs_ref[...] = (dm * bsb_ref[...] * ss * (1.0 - ss)).astype(BF16)
        dgx_ref[...] = (dm * bfx_ref[...] * sx * (1.0 - sx)).astype(BF16)

    blk = pl.BlockSpec((tr, tc), lambda i, j: (i, j))
    out = jax.ShapeDtypeStruct((s, d), BF16)
    return pl.pallas_call(
        kern, name="gate_bwd", grid=(s // tr, nc),
        in_specs=[blk, blk, blk, blk, pl.BlockSpec((tr, tc), lambda i, j: (i, j + nc))],
        out_specs=[blk, blk, blk, blk], out_shape=[out, out, out, out],
        compiler_params=_params("parallel", "parallel"),
    )(dmerged, bsb, bfx, gf, gf)


def swiglu_fwd(gu, cw):
    s, f2 = gu.shape
    tr = _tile(s, 256, 16)

    def kern(gu_ref, o_ref):
        g = gu_ref[:, :cw]
        o_ref[...] = (g * _sigmoid(g) * gu_ref[:, cw:]).astype(BF16)

    return pl.pallas_call(
        kern, name="swiglu_fwd", grid=(s // tr, f2 // (2 * cw)),
        in_specs=[pl.BlockSpec((tr, 2 * cw), lambda i, j: (i, j))],
        out_specs=pl.BlockSpec((tr, cw), lambda i, j: (i, j)),
        out_shape=jax.ShapeDtypeStruct((s, f2 // 2), BF16),
        compiler_params=_params("parallel", "parallel"),
    )(gu)


def swiglu_bwd(dact, gu, cw):
    s, f2 = gu.shape
    tr = _tile(s, 256, 16)

    def kern(da_ref, gu_ref, o_ref):
        da = da_ref[...]
        g = gu_ref[:, :cw]
        sg = _sigmoid(g)
        o_ref[:, :cw] = (da * gu_ref[:, cw:] * (sg * (1.0 + g * (1.0 - sg)))).astype(BF16)
        o_ref[:, cw:] = (da * (g * sg)).astype(BF16)

    return pl.pallas_call(
        kern, name="swiglu_bwd", grid=(s // tr, f2 // (2 * cw)),
        in_specs=[pl.BlockSpec((tr, cw), lambda i, j: (i, j)), pl.BlockSpec((tr, 2 * cw), lambda i, j: (i, j))],
        out_specs=pl.BlockSpec((tr, 2 * cw), lambda i, j: (i, j)),
        out_shape=jax.ShapeDtypeStruct((s, f2), BF16),
        compiler_params=_params("parallel", "parallel"),
    )(dact, gu)


def _split3(v):
    hi = v.astype(BF16)
    r = v - hi.astype(F32)
    mid = r.astype(BF16)
    lo = (r - mid.astype(F32)).astype(BF16)
    return hi, mid, lo


def _dot3_right(v, ones):
    hi, mid, lo = _split3(v)
    d = lambda p: jnp.dot(p, ones, preferred_element_type=F32)
    return (d(lo) + d(mid)) + d(hi)


def _dot3_left(ones, v):
    hi, mid, lo = _split3(v)
    d = lambda p: jnp.dot(ones, p, preferred_element_type=F32)
    return (d(lo) + d(mid)) + d(hi)


def _split2(v):
    hi = v.astype(BF16)
    return hi, (v - hi.astype(F32)).astype(BF16)


def _dot2_right(v, ones):
    hi, lo = _split2(v)
    return jnp.dot(lo, ones, preferred_element_type=F32) + jnp.dot(hi, ones, preferred_element_type=F32)


def _log1p_exp_neg_abs(v):
    return jnp.log(1.0 + jnp.exp(-jnp.abs(v)))


def _mask01(cond):
    return jnp.where(cond, 1.0, 0.0).astype(BF16)


def _iota2(t):
    return (lax.broadcasted_iota(jnp.int32, (t, t), 0), lax.broadcasted_iota(jnp.int32, (t, t), 1))


def cum_fwd(gf, b_pad, f_col0):
    s = gf.shape[0]
    t = _tile(s, ATT_TILE, LANES)
    fb = f_col0 // LANES

    def kern(f_ref, b_ref, cum_ref, carry_ref):
        @pl.when(pl.program_id(0) == 0)
        def _():
            carry_ref[...] = jnp.zeros_like(carry_ref)

        v = f_ref[...] + b_ref[...]
        lf = jnp.minimum(v, 0.0) - _log1p_exp_neg_abs(v)
        row, col = _iota2(t)
        cum = _dot3_left(_mask01(col <= row), lf) + carry_ref[...]
        cum_ref[...] = cum
        carry_ref[...] = cum[t - 1:t, :]

    return pl.pallas_call(
        kern, name="cum_fwd", grid=(s // t,),
        in_specs=[pl.BlockSpec((t, LANES), lambda i: (i, fb)), pl.BlockSpec((1, LANES), lambda i: (0, 0))],
        out_specs=pl.BlockSpec((t, LANES), lambda i: (i, 0)),
        out_shape=jax.ShapeDtypeStruct((s, LANES), F32),
        scratch_shapes=[pltpu.VMEM((1, LANES), F32)],
        compiler_params=_params("arbitrary"),
    )(gf, b_pad)


def cum_bwd(dcum, gf, b_pad, f_col0, n_heads):
    s = gf.shape[0]
    t = _tile(s, ATT_TILE, LANES)
    nb = s // t
    fb = f_col0 // LANES

    def kern(dc_ref, f_ref, b_ref, df_ref, db_ref, carry_ref):
        @pl.when(pl.program_id(0) == 0)
        def _():
            carry_ref[...] = jnp.zeros_like(carry_ref)
            db_ref[...] = jnp.zeros_like(db_ref)

        row, col = _iota2(t)
        dlf = _dot3_left(_mask01(col >= row), dc_ref[...]) + carry_ref[...]
        carry_ref[...] = dlf[0:1, :]
        v = f_ref[...] + b_ref[...]
        sig_neg = jnp.exp(-jnp.maximum(v, 0.0) - _log1p_exp_neg_abs(v))
        lane = lax.broadcasted_iota(jnp.int32, (t, LANES), 1)
        df = jnp.where(lane < n_heads, dlf * sig_neg, 0.0)
        df_ref[...] = df.astype(BF16)
        db_ref[...] += jnp.sum(df, axis=0, keepdims=True)

    return pl.pallas_call(
        kern, name="cum_bwd", grid=(nb,),
        in_specs=[pl.BlockSpec((t, LANES), lambda i: (nb - 1 - i, 0)),
                  pl.BlockSpec((t, LANES), lambda i: (nb - 1 - i, fb)),
                  pl.BlockSpec((1, LANES), lambda i: (0, 0))],
        out_specs=[pl.BlockSpec((t, LANES), lambda i: (nb - 1 - i, 0)), pl.BlockSpec((1, LANES), lambda i: (0, 0))],
        out_shape=[jax.ShapeDtypeStruct((s, LANES), BF16), jax.ShapeDtypeStruct((1, LANES), F32)],
        scratch_shapes=[pltpu.VMEM((1, LANES), F32)],
        compiler_params=_params("arbitrary"),
    )(dcum, gf, b_pad)


def _qkv_specs(s, t, n_heads, base):
    return [pl.BlockSpec((t, HEAD_DIM), lambda h, i: (i, base + h)),
            pl.BlockSpec((s, HEAD_DIM), lambda h, i: (0, base + n_heads + h)),
            pl.BlockSpec((s, HEAD_DIM), lambda h, i: (0, base + 2 * n_heads + h))]


def _strips(t):
    sr = _tile(t, ATT_STRIP, 8)
    return sr, t // sr, [slice(si * sr, (si + 1) * sr) for si in range(t // sr)]


def _key_minus_row(sr, t):
    return lax.broadcasted_iota(jnp.int32, (sr, t), 1) - lax.broadcasted_iota(jnp.int32, (sr, t), 0)


def _keep(valid, v):
    return v if valid is None else jnp.where(valid, v, 0.0)


def _sb_scores(q, k, diff, lim):
    z = lax.dot_general(q, k, NT, preferred_element_type=F32) * (HEAD_DIM ** -0.5)
    valid = None if lim is None else diff < lim
    l1p = _log1p_exp_neg_abs(z)
    return z, valid, l1p, _keep(valid, -jnp.maximum(z, 0.0) - l1p)


def sb_fwd(qkv, n_heads, base):
    s = qkv.shape[0]
    t = _tile(s, ATT_TILE, LANES)
    sr, ns, strips = _strips(t)

    def kern(q_ref, k_ref, v_ref, o_ref):
        i = pl.program_id(1)
        row, col = _iota2(t)
        after = _mask01(row > col)
        diff = _key_minus_row(sr, t)
        qs = [q_ref[sl, :] for sl in strips]

        def tile(j, carry, diagonal):
            runs, accs = carry
            off = pl.multiple_of(j * t, t)
            k = k_ref[pl.ds(off, t), :]
            v = v_ref[pl.ds(off, t), :]
            new_runs, new_accs = [], []
            for si in range(ns):
                z, valid, l1p, log_keep = _sb_scores(qs[si], k, diff, si * sr if diagonal else None)
                between = _dot2_right(log_keep, after) + runs[si]
                w = _keep(valid, jnp.exp(jnp.minimum(z, 0.0) - l1p + between))
                new_accs.append(accs[si] + jnp.dot(w.astype(BF16), v, preferred_element_type=F32))
                new_runs.append(runs[si] + jnp.sum(log_keep, axis=1, keepdims=True))
            return tuple(new_runs), tuple(new_accs)

        init = (tuple(jnp.zeros((sr, 1), F32) for _ in strips), tuple(jnp.zeros((sr, HEAD_DIM), F32) for _ in strips))
        _, accs = lax.fori_loop(0, i, lambda jj, c: tile(i - 1 - jj, c, False), tile(i, init, True))
        for sl, acc in zip(strips, accs):
            o_ref[sl, :] = acc.astype(o_ref.dtype)

    return pl.pallas_call(
        kern, name="sb_fwd", grid=(n_heads, s // t),
        in_specs=_qkv_specs(s, t, n_heads, base),
        out_specs=pl.BlockSpec((t, HEAD_DIM), lambda h, i: (i, h)),
        out_shape=jax.ShapeDtypeStruct((s, n_heads * HEAD_DIM), BF16),
        compiler_params=_params("parallel", "arbitrary"),
    )(qkv, qkv, qkv)


def sb_bwd(qkv, d_o, n_heads, base):
    s = qkv.shape[0]
    t = _tile(s, ATT_TILE, LANES)
    nq = s // t
    sr, ns, strips = _strips(t)
    scale = HEAD_DIM ** -0.5

    def kern(q_ref, k_ref, v_ref, do_ref, dq_ref, dk_ref, dv_ref, dk_acc, dv_acc, run_ref):
        i = pl.program_id(1)

        @pl.when(i == 0)
        def _():
            dk_acc[...] = jnp.zeros_like(dk_acc)
            dv_acc[...] = jnp.zeros_like(dv_acc)

        row, col = _iota2(t)
        after = _mask01(row > col)
        before = _mask01(row < col)
        diff = _key_minus_row(sr, t)
        qs = [q_ref[sl, :] for sl in strips]
        dos = [do_ref[sl, :] for sl in strips]

        def sweep1(j, runs, diagonal):
            k = k_ref[pl.ds(pl.multiple_of(j * t, t), t), :]
            new_runs = []
            for si, sl in enumerate(strips):
                _, _, _, log_keep = _sb_scores(qs[si], k, diff, si * sr if diagonal else None)
                run_ref[j, sl, :] = runs[si]
                new_runs.append(runs[si] + jnp.sum(log_keep, axis=1, keepdims=True))
            return tuple(new_runs)

        lax.fori_loop(0, i, lambda jj, c: sweep1(i - 1 - jj, c, False),
                      sweep1(i, tuple(jnp.zeros((sr, 1), F32) for _ in strips), True))

        def sweep2(j, carry, diagonal):
            run_es, dqs = carry
            off = pl.multiple_of(j * t, t)
            k = k_ref[pl.ds(off, t), :]
            v = v_ref[pl.ds(off, t), :]
            new_es, new_dqs = [], []
            dk_t = jnp.zeros((t, HEAD_DIM), F32)
            dv_t = jnp.zeros((t, HEAD_DIM), F32)
            for si, sl in enumerate(strips):
                z, valid, l1p, log_keep = _sb_scores(qs[si], k, diff, si * sr if diagonal else None)
                between = _dot2_right(log_keep, after) + run_ref[j, sl, :]
                w = _keep(valid, jnp.exp(jnp.minimum(z, 0.0) - l1p + between))
                dw = lax.dot_general(dos[si], v, NT, preferred_element_type=F32)
                e = dw * w
                e_before = _dot2_right(e, before) + run_es[si]
                keep = jnp.exp(log_keep)
                dz = _keep(valid, e * keep - e_before * (1.0 - keep)) * scale
                dzb = dz.astype(BF16)
                new_dqs.append(dqs[si] + jnp.dot(dzb, k, preferred_element_type=F32))
                dk_t = dk_t + lax.dot_general(dzb, qs[si], TN, preferred_element_type=F32)
                dv_t = dv_t + lax.dot_general(w.astype(BF16), dos[si], TN, preferred_element_type=F32)
                new_es.append(run_es[si] + jnp.sum(e, axis=1, keepdims=True))
            dk_acc[pl.ds(off, t), :] += dk_t
            dv_acc[pl.ds(off, t), :] += dv_t
            return tuple(new_es), tuple(new_dqs)

        init = (tuple(jnp.zeros((sr, 1), F32) for _ in strips), tuple(jnp.zeros((sr, HEAD_DIM), F32) for _ in strips))
        _, dqs = sweep2(i, lax.fori_loop(0, i, lambda j, c: sweep2(j, c, False), init), True)
        for sl, dq in zip(strips, dqs):
            dq_ref[sl, :] = dq.astype(BF16)

        @pl.when(i == nq - 1)
        def _():
            dk_ref[...] = dk_acc[...].astype(BF16)
            dv_ref[...] = dv_acc[...].astype(BF16)

    out = jax.ShapeDtypeStruct((s, n_heads * HEAD_DIM), BF16)
    head_blk = pl.BlockSpec((s, HEAD_DIM), lambda h, i: (0, h))
    tile_blk = pl.BlockSpec((t, HEAD_DIM), lambda h, i: (i, h))
    return pl.pallas_call(
        kern, name="sb_bwd", grid=(n_heads, nq),
        in_specs=_qkv_specs(s, t, n_heads, base) + [tile_blk],
        out_specs=[tile_blk, head_blk, head_blk],
        out_shape=[out, out, out],
        scratch_shapes=[pltpu.VMEM((s, HEAD_DIM), F32), pltpu.VMEM((s, HEAD_DIM), F32), pltpu.VMEM((nq, t, 1), F32)],
        compiler_params=_params("parallel", "arbitrary"),
    )(qkv, qkv, qkv, d_o)


def _fox_scores(q, k, cq, ck, diff, lim):
    sc = lax.dot_general(q, k, NT, preferred_element_type=F32) * (HEAD_DIM ** -0.5)
    sc = sc + cq - ck
    if lim is None:
        return sc, None
    valid = diff < lim
    return jnp.where(valid, sc, NEG_BIG), valid


def fox_fwd(qkv, cum_col, cum_row, n_heads, base):
    s = qkv.shape[0]
    t = _tile(s, ATT_TILE, LANES)
    sr, ns, strips = _strips(t)

    def kern(q_ref, k_ref, v_ref, cq_ref, ck_ref, o_ref, lse_ref):
        i = pl.program_id(1)
        diff = _key_minus_row(sr, t)
        qs = [q_ref[sl, :] for sl in strips]
        cqs = [cq_ref[0, sl, :] for sl in strips]

        def tile(j, carry, diagonal):
            off = pl.multiple_of(j * t, t)
            k = k_ref[pl.ds(off, t), :]
            v = v_ref[pl.ds(off, t), :]
            ck = ck_ref[0, :, pl.ds(off, t)]
            out = []
            for si in range(ns):
                m, l, acc = carry[si]
                sc, _ = _fox_scores(qs[si], k, cqs[si], ck, diff, si * sr + 1 if diagonal else None)
                m_new = jnp.maximum(m, jnp.max(sc, axis=1, keepdims=True))
                p = jnp.exp(sc - m_new)
                alpha = jnp.exp(m - m_new)
                l = alpha * l + jnp.sum(p, axis=1, keepdims=True)
                acc = alpha * acc + jnp.dot(p.astype(BF16), v, preferred_element_type=F32)
                out.append((m_new, l, acc))
            return tuple(out)

        init = tuple((jnp.full((sr, 1), NEG_BIG, F32), jnp.zeros((sr, 1), F32), jnp.zeros((sr, HEAD_DIM), F32))
                     for _ in strips)
        res = tile(i, lax.fori_loop(0, i, lambda j, c: tile(j, c, False), init), True)
        for sl, (m, l, acc) in zip(strips, res):
            o_ref[sl, :] = acc / l
            lse_ref[0, sl, :] = m + jnp.log(l)

    col_blk = pl.BlockSpec((1, t, 1), lambda h, i: (h, i, 0))
    return pl.pallas_call(
        kern, name="fox_fwd", grid=(n_heads, s // t),
        in_specs=_qkv_specs(s, t, n_heads, base) + [col_blk, pl.BlockSpec((1, 1, s), lambda h, i: (h, 0, 0))],
        out_specs=[pl.BlockSpec((t, HEAD_DIM), lambda h, i: (i, h)), col_blk],
        out_shape=[jax.ShapeDtypeStruct((s, n_heads * HEAD_DIM), F32), jax.ShapeDtypeStruct((n_heads, s, 1), F32)],
        compiler_params=_params("parallel", "arbitrary"),
    )(qkv, qkv, qkv, cum_col, cum_row)


def fox_bwd(qkv, cum_col, cum_row, o, d_o, lse, n_heads, base):
    s = qkv.shape[0]
    t = _tile(s, ATT_TILE, LANES)
    nq = s // t
    sr, ns, strips = _strips(t)
    scale = HEAD_DIM ** -0.5

    def kern(q_ref, k_ref, v_ref, cq_ref, ck_ref, o_ref, do_ref, lse_ref,
             dq_ref, dk_ref, dv_ref, dcq_ref, dck_ref, dk_acc, dv_acc, dck_acc):
        i = pl.program_id(1)

        @pl.when(i == 0)
        def _():
            dk_acc[...] = jnp.zeros_like(dk_acc)
            dv_acc[...] = jnp.zeros_like(dv_acc)
            dck_acc[...] = jnp.zeros_like(dck_acc)

        diff = _key_minus_row(sr, t)
        qs = [q_ref[sl, :] for sl in strips]
        dos = [do_ref[sl, :] for sl in strips]
        cqs = [cq_ref[0, sl, :] for sl in strips]
        lses = [lse_ref[0, sl, :] for sl in strips]
        deltas = [jnp.sum(dos[si].astype(F32) * o_ref[sl, :], axis=1, keepdims=True) for si, sl in enumerate(strips)]

        def tile(j, carry, diagonal):
            off = pl.multiple_of(j * t, t)
            k = k_ref[pl.ds(off, t), :]
            v = v_ref[pl.ds(off, t), :]
            ck = ck_ref[0, :, pl.ds(off, t)]
            out = []
            dk_t = jnp.zeros((t, HEAD_DIM), F32)
            dv_t = jnp.zeros((t, HEAD_DIM), F32)
            dck_t = jnp.zeros((1, t), F32)
            for si in range(ns):
                dq, dcq = carry[si]
                sc, valid = _fox_scores(qs[si], k, cqs[si], ck, diff, si * sr + 1 if diagonal else None)
                p = _keep(valid, jnp.exp(sc - lses[si]))
                dp = lax.dot_general(dos[si], v, NT, preferred_element_type=F32)
                ds = p * (dp - deltas[si])
                dsb = (ds * scale).astype(BF16)
                dq = dq + jnp.dot(dsb, k, preferred_element_type=F32)
                dk_t = dk_t + lax.dot_general(dsb, qs[si], TN, preferred_element_type=F32)
                dv_t = dv_t + lax.dot_general(p.astype(BF16), dos[si], TN, preferred_element_type=F32)
                dck_t = dck_t + jnp.sum(ds, axis=0, keepdims=True)
                out.append((dq, dcq + jnp.sum(ds, axis=1, keepdims=True)))
            dk_acc[pl.ds(off, t), :] += dk_t
            dv_acc[pl.ds(off, t), :] += dv_t
            dck_acc[:, pl.ds(off, t)] -= dck_t
            return tuple(out)

        init = tuple((jnp.zeros((sr, HEAD_DIM), F32), jnp.zeros((sr, 1), F32)) for _ in strips)
        res = tile(i, lax.fori_loop(0, i, lambda j, c: tile(j, c, False), init), True)
        for sl, (dq, dcq) in zip(strips, res):
            dq_ref[sl, :] = dq.astype(BF16)
            dcq_ref[0, sl, :] = dcq

        @pl.when(i == nq - 1)
        def _():
            dk_ref[...] = dk_acc[...].astype(BF16)
            dv_ref[...] = dv_acc[...].astype(BF16)
            dck_ref[0] = dck_acc[...]

    out = jax.ShapeDtypeStruct((s, n_heads * HEAD_DIM), BF16)
    head_blk = pl.BlockSpec((s, HEAD_DIM), lambda h, i: (0, h))
    tile_blk = pl.BlockSpec((t, HEAD_DIM), lambda h, i: (i, h))
    col_blk = pl.BlockSpec((1, t, 1), lambda h, i: (h, i, 0))
    row_blk = pl.BlockSpec((1, 1, s), lambda h, i: (h, 0, 0))
    return pl.pallas_call(
        kern, name="fox_bwd", grid=(n_heads, nq),
        in_specs=_qkv_specs(s, t, n_heads, base) + [col_blk, row_blk, tile_blk, tile_blk, col_blk],
        out_specs=[tile_blk, head_blk, head_blk, col_blk, row_blk],
        out_shape=[out, out, out, jax.ShapeDtypeStruct((n_heads, s, 1), F32),
                   jax.ShapeDtypeStruct((n_heads, 1, s), F32)],
        scratch_shapes=[pltpu.VMEM((s, HEAD_DIM), F32), pltpu.VMEM((s, HEAD_DIM), F32), pltpu.VMEM((1, s), F32)],
        compiler_params=_params("parallel", "arbitrary"),
    )(qkv, qkv, qkv, cum_col, cum_row, o, d_o, lse)


def _place():
    x, y, c = lax.axis_index("x"), lax.axis_index("y"), lax.axis_index("c")
    other_chips = [(1 - x, y), (x, 1 - y), (1 - x, 1 - y)]
    return x, y, c, other_chips


ANY = pl.BlockSpec(memory_space=pl.ANY)


def _remote(src, dst, send_sem, recv_sem, dev):
    return pltpu.make_async_remote_copy(src_ref=src, dst_ref=dst, send_sem=send_sem, recv_sem=recv_sem,
                                        device_id=dev, device_id_type=MESH)


def cast_place(name, ws, chip):
    r = ws[0].shape[1]
    cs = [w.shape[2] for w in ws]
    tr = _tile(r, 256, 16)

    def kern(chip_ref, *refs):
        o_ref = refs[-1]
        off = 0
        for w_ref, c in zip(refs[:-1], cs):
            o_ref[:, off:off + c] = w_ref[...].astype(BF16)
            off += c

    return pl.pallas_call(
        kern, name=name,
        grid_spec=pltpu.PrefetchScalarGridSpec(
            num_scalar_prefetch=1, grid=(r // tr,),
            in_specs=[pl.BlockSpec((None, tr, c), lambda i, chip_ref: (0, i, 0)) for c in cs],
            out_specs=pl.BlockSpec((None, tr, sum(cs)), lambda i, chip_ref: (chip_ref[0], i, 0))),
        out_shape=jax.ShapeDtypeStruct((N_CHIPS, r, sum(cs)), BF16),
        compiler_params=_params("parallel"),
    )(chip, *ws)


HBM = pl.BlockSpec(memory_space=pltpu.HBM)
SEM = pl.BlockSpec(memory_space=pltpu.SEMAPHORE)
SPLIT = pltpu.CompilerParams(has_side_effects=pltpu.SideEffectType.DATAFLOW_SIDE_EFFECTING)


def _in_hbm(a):
    return pltpu.with_memory_space_constraint(a, pltpu.HBM)


def _slab_rows(ref, k, core):
    half = ref.shape[1] // 2
    return ref.at[k, pl.ds(pl.multiple_of(core * half, 16), half)]


def gather_start(name, bufs):
    n = len(bufs)

    def body(*refs):
        ins, send, recv, token = refs[:n], refs[n], refs[n + 1], refs[-1]
        x, y, c, chips = _place()
        me = 2 * x + y
        for a in range(n):
            for j in range(3):
                rows = _slab_rows(ins[a], me, c)
                _remote(rows, rows, send.at[3 * a + j], recv.at[3 * a + j], (chips[j][0], chips[j][1], c)).start()
        token[...] = jnp.zeros_like(token)

    sem = pltpu.SemaphoreType.DMA((3 * n,))
    res = pl.pallas_call(
        body, name=name, in_specs=[HBM] * n, out_specs=[SEM, SEM] + [HBM] * n + [pl.BlockSpec(memory_space=pltpu.VMEM)],
        out_shape=[sem, sem] + [pltpu.HBM(b.shape, b.dtype) for b in bufs] + [jax.ShapeDtypeStruct((8, LANES), F32)],
        input_output_aliases={a: 2 + a for a in range(n)}, compiler_params=SPLIT,
    )(*[_in_hbm(b) for b in bufs])
    return res[0], res[1], res[2:2 + n], res[-1]


def gather_wait(name, bufs, send_sems, recv_sems, after):
    n = len(bufs)

    def body(*refs):
        ins, send, recv = refs[:n], refs[n], refs[n + 1]
        x, y, c, chips = _place()
        me = 2 * x + y
        for a in range(n):
            for j in range(3):
                dev = (chips[j][0], chips[j][1], c)
                mine = _slab_rows(ins[a], me, c)
                _remote(mine, mine, send.at[3 * a + j], recv.at[3 * a + j], dev).wait_send()
                land = _slab_rows(ins[a], 2 * chips[j][0] + chips[j][1], c)
                _remote(land, land, send.at[3 * a + j], recv.at[3 * a + j], dev).wait_recv()

    return pl.pallas_call(
        body, name=name, in_specs=[HBM] * n + [SEM, SEM] + [ANY] * len(after), out_specs=[HBM] * n,
        out_shape=[pltpu.HBM(b.shape, b.dtype) for b in bufs],
        input_output_aliases={a: a for a in range(n)}, compiler_params=SPLIT,
    )(*bufs, send_sems, recv_sems, *after)


def gather_forward(name, bufs):
    n = len(bufs)

    def body(*refs):
        outs = refs[n:2 * n]
        send_sems, recv_sems = refs[2 * n:]
        x, y, c, chips = _place()
        sibling = (x, y, 1 - c)

        def d2d(a, j, core):
            rows = _slab_rows(outs[a], 2 * chips[j][0] + chips[j][1], core)
            return _remote(rows, rows, send_sems.at[3 * a + j], recv_sems.at[3 * a + j], sibling)

        pairs = [(a, j) for a in range(n) for j in range(3)]
        for a, j in pairs:
            d2d(a, j, c).start()
        for a, j in pairs:
            d2d(a, j, 1 - c).wait_recv()
        for a, j in pairs:
            d2d(a, j, c).wait_send()

    return pl.pallas_call(
        body, name=name, in_specs=[ANY] * n, out_specs=[ANY] * n,
        out_shape=[jax.ShapeDtypeStruct(b.shape, b.dtype) for b in bufs],
        input_output_aliases={a: a for a in range(n)},
        scratch_shapes=[pltpu.SemaphoreType.DMA((3 * n,)), pltpu.SemaphoreType.DMA((3 * n,))],
    )(*bufs)


def swap_halves(name, pieces):
    n = len(pieces)
    halves = [p.shape[1] // 2 for p in pieces]

    def body(*refs):
        ins, outs = refs[:n], refs[n:2 * n]
        send_sems, recv_sems = refs[2 * n:]
        x, y, c, _ = _place()
        cps = [_remote(ins[a].at[:, pl.ds(pl.multiple_of((1 - c) * halves[a], 16), halves[a]), :], outs[a],
                       send_sems.at[a], recv_sems.at[a], (x, y, 1 - c)) for a in range(n)]
        for cp in cps:
            cp.start()
        for cp in cps:
            cp.wait()

    return pl.pallas_call(
        body, name=name, in_specs=[ANY] * n, out_specs=[ANY] * n,
        out_shape=[jax.ShapeDtypeStruct((N_CHIPS, h, p.shape[2]), p.dtype) for p, h in zip(pieces, halves)],
        scratch_shapes=[pltpu.SemaphoreType.DMA((n,)), pltpu.SemaphoreType.DMA((n,))],
    )(*pieces)


def pair_sum(name, pieces, got, core):
    _, r, w = pieces.shape
    half = r // 2
    tr = _tile(half, 256, 16)

    def kern(core_ref, p_ref, g_ref, o_ref):
        o_ref[...] = (p_ref[...].astype(F32) + g_ref[...].astype(F32)).astype(o_ref.dtype)

    return pl.pallas_call(
        kern, name=name,
        grid_spec=pltpu.PrefetchScalarGridSpec(
            num_scalar_prefetch=1, grid=(N_CHIPS, half // tr),
            in_specs=[pl.BlockSpec((None, None, tr, w), lambda k, i, core_ref: (k, core_ref[0], i, 0)),
                      pl.BlockSpec((None, tr, w), lambda k, i, core_ref: (k, i, 0))],
            out_specs=pl.BlockSpec((None, tr, w), lambda k, i, core_ref: (k, i, 0))),
        out_shape=jax.ShapeDtypeStruct((N_CHIPS, half, w), pieces.dtype),
        compiler_params=_params("parallel", "parallel"),
    )(core, pieces.reshape(N_CHIPS, 2, half, w), got)


def _scatter_copies(sums, lands, send, recv):
    x, y, c, chips = _place()
    return [_remote(sums[a].at[2 * chips[j][0] + chips[j][1]], lands[a].at[j], send.at[3 * a + j], recv.at[3 * a + j],
                    (chips[j][0], chips[j][1], c)) for a in range(len(sums)) for j in range(3)]


def scatter_start(name, sums):
    n = len(sums)
    lands = [lax.empty((3,) + t.shape[1:], t.dtype) for t in sums]

    def body(*refs):
        ins, land_in, send, recv, token = refs[:n], refs[n:2 * n], refs[2 * n], refs[2 * n + 1], refs[-1]
        for cp in _scatter_copies(ins, land_in, send, recv):
            cp.start()
        token[...] = jnp.zeros_like(token)

    sem = pltpu.SemaphoreType.DMA((3 * n,))
    res = pl.pallas_call(
        body, name=name, in_specs=[HBM] * (2 * n),
        out_specs=[SEM, SEM] + [HBM] * (2 * n) + [pl.BlockSpec(memory_space=pltpu.VMEM)],
        out_shape=[sem, sem] + [pltpu.HBM(t.shape, t.dtype) for t in sums + lands] + [jax.ShapeDtypeStruct((8, LANES), F32)],
        input_output_aliases={a: 2 + a for a in range(2 * n)}, compiler_params=SPLIT,
    )(*[_in_hbm(t) for t in sums + lands])
    return res[0], res[1], res[2:2 + n], res[2 + n:2 + 2 * n], res[-1]


def scatter_wait(name, sums, lands, send_sems, recv_sems, after):
    n = len(sums)

    def body(*refs):
        ins, land_in, send, recv = refs[:n], refs[n:2 * n], refs[2 * n], refs[2 * n + 1]
        for cp in _scatter_copies(ins, land_in, send, recv):
            cp.wait_send()
            cp.wait_recv()

    res = pl.pallas_call(
        body, name=name, in_specs=[HBM] * (2 * n) + [SEM, SEM, ANY], out_specs=[HBM] * (2 * n),
        out_shape=[pltpu.HBM(t.shape, t.dtype) for t in sums + lands],
        input_output_aliases={a: a for a in range(2 * n)}, compiler_params=SPLIT,
    )(*sums, *lands, send_sems, recv_sems, after)
    return res[:n], res[n:]


def chip_sum(name, sums, got, chip, core):
    _, half, w = sums.shape
    tr = _tile(half, 256, 16)
    nb = half // tr

    def kern(ids_ref, s_ref, g0_ref, g1_ref, g2_ref, o_ref):
        o_ref[...] = ((s_ref[...].astype(F32) + g0_ref[...].astype(F32)) + g1_ref[...].astype(F32)) \
            + g2_ref[...].astype(F32)

    def got_spec(j):
        return pl.BlockSpec((None, tr, w), lambda i, ids_ref: (j, i, 0))

    return pl.pallas_call(
        kern, name=name,
        grid_spec=pltpu.PrefetchScalarGridSpec(
            num_scalar_prefetch=1, grid=(nb,),
            in_specs=[pl.BlockSpec((None, tr, w), lambda i, ids_ref: (ids_ref[0], i, 0)),
                      got_spec(0), got_spec(1), got_spec(2)],
            out_specs=pl.BlockSpec((tr, w), lambda i, ids_ref: (ids_ref[1] * nb + i, 0))),
        out_shape=jax.ShapeDtypeStruct((2 * half, w), F32),
        compiler_params=_params("parallel"),
    )(jnp.concatenate([chip, core]), sums, got, got, got)


def join_halves(name, shards):
    n = len(shards)
    halves = [g.shape[0] // 2 for g in shards]

    def body(*refs):
        outs = refs[n:2 * n]
        send_sems, recv_sems = refs[2 * n:]
        x, y, c, _ = _place()
        cps = []
        for a in range(n):
            rows = outs[a].at[pl.ds(pl.multiple_of(c * halves[a], 8), halves[a])]
            cps.append(_remote(rows, rows, send_sems.at[a], recv_sems.at[a], (x, y, 1 - c)))
        for cp in cps:
            cp.start()
        for cp in cps:
            cp.wait()

    return pl.pallas_call(
        body, name=name, in_specs=[ANY] * n, out_specs=[ANY] * n,
        out_shape=[jax.ShapeDtypeStruct(g.shape, g.dtype) for g in shards],
        input_output_aliases={a: a for a in range(n)},
        scratch_shapes=[pltpu.SemaphoreType.DMA((n,)), pltpu.SemaphoreType.DMA((n,))],
    )(*shards)


def _adam(w, g, m, v):
    m = ADAM_B1 * m + (1.0 - ADAM_B1) * g
    v = ADAM_B2 * v + (1.0 - ADAM_B2) * (g * g)
    m_hat = m / (1.0 - ADAM_B1 ** ADAM_STEP)
    v_hat = v / (1.0 - ADAM_B2 ** ADAM_STEP)
    delta = -ADAM_LR * (m_hat / (jnp.sqrt(v_hat) + ADAM_EPS) + ADAM_WD * w)
    return delta, m, v


def small_allreduce_adam(g_part, w, m, v):
    n_dev = 8
    r, d = g_part.shape

    def body(g_ref, w_ref, m_ref, v_ref, gs_ref, dl_ref, nm_ref, nv_ref, all_ref, send_sems, recv_sems):
        x, y, c, _ = _place()
        me = 4 * x + 2 * y + c
        all_ref[me] = g_ref[...]
        cps = []
        for rel in range(1, n_dev):
            px = 1 - x if rel & 4 else x
            py = 1 - y if rel & 2 else y
            pc = 1 - c if rel & 1 else c
            cps.append(_remote(g_ref, all_ref.at[me], send_sems.at[rel - 1], recv_sems.at[rel - 1], (px, py, pc)))
        for cp in cps:
            cp.start()
        for cp in cps:
            cp.wait()
        total = all_ref[0]
        for dev in range(1, n_dev):
            total = total + all_ref[dev]
        gs_ref[...] = total
        delta, nm, nv = _adam(w_ref[...], total, m_ref[...], v_ref[...])
        dl_ref[...] = delta
        nm_ref[...] = nm
        nv_ref[...] = nv

    vm = pl.BlockSpec(memory_space=pltpu.VMEM)
    out = jax.ShapeDtypeStruct((r, d), F32)
    return pl.pallas_call(
        body, name="small_allreduce_adam", in_specs=[vm, vm, vm, vm], out_specs=[vm, vm, vm, vm],
        out_shape=[out, out, out, out],
        scratch_shapes=[pltpu.VMEM((n_dev, r, d), F32), pltpu.SemaphoreType.DMA((n_dev - 1,)),
                        pltpu.SemaphoreType.DMA((n_dev - 1,))],
    )(g_part, w, m, v)


def adam_update(name, w, m, v, g_buf, col_blk, after=None):
    w, m, v = w[0], m[0], v[0]
    r, c = w.shape
    tr = _tile(r, 128, 8)
    extra = [] if after is None else [after]

    def kern(w_ref, m_ref, v_ref, g_ref, *rest):
        go_ref, dl_ref, nm_ref, nv_ref = rest[len(extra):]
        g = g_ref[...]
        delta, nm, nv = _adam(w_ref[...], g, m_ref[...], v_ref[...])
        go_ref[...] = g
        dl_ref[...] = delta
        nm_ref[...] = nm
        nv_ref[...] = nv

    blk = pl.BlockSpec((tr, c), lambda i: (i, 0))
    out = jax.ShapeDtypeStruct((r, c), F32)
    res = pl.pallas_call(
        kern, name=name, grid=(r // tr,),
        in_specs=[blk, blk, blk, pl.BlockSpec((tr, c), lambda i: (i, col_blk))] + [ANY] * len(extra),
        out_specs=[blk] * 4, out_shape=[out] * 4, compiler_params=_params("parallel"),
    )(w, m, v, g_buf, *extra)
    return [a[None] for a in res]


def _w_in_segments(cw, n_qkv, n_heads, d):
    out = []

    def add(lo, hi, main):
        while lo < hi:
            k, a = divmod(lo, cw)
            w = min(cw - a, hi - lo)
            out.append((k, a, main, w))
            lo, main = lo + w, main + w

    add(0, n_qkv, 0)
    add(n_qkv + n_heads, N_CHIPS * cw, n_qkv)
    add(n_qkv, n_qkv + n_heads, n_qkv + 2 * d)
    return out


def regroup_w_in(g_in, segments, n_main):
    _, d, cw = g_in.shape
    tr = _tile(d, 128, 16)
    n_real = max(m + w for _, _, m, w in segments)

    def kern(s_ref, o_ref):
        for k, a, m, w in segments:
            o_ref[:, m:m + w] = s_ref[k, :, a:a + w]
        o_ref[:, n_real:] = jnp.zeros((tr, n_main - n_real), o_ref.dtype)

    return pl.pallas_call(
        kern, name="regroup_w_in", grid=(d // tr,),
        in_specs=[pl.BlockSpec((N_CHIPS, tr, cw), lambda i: (0, i, 0))],
        out_specs=pl.BlockSpec((tr, n_main), lambda i: (i, 0)),
        out_shape=jax.ShapeDtypeStruct((d, n_main), g_in.dtype), compiler_params=_params("parallel"),
    )(g_in)


def regroup_dw_in(dw_main, segments, cw):
    d, n_main = dw_main.shape
    tr = _tile(d, 128, 16)

    def kern(s_ref, o_ref):
        for k, a, m, w in segments:
            o_ref[k, :, a:a + w] = s_ref[:, m:m + w]

    return pl.pallas_call(
        kern, name="regroup_dw_in", grid=(d // tr,),
        in_specs=[pl.BlockSpec((tr, n_main), lambda i: (i, 0))],
        out_specs=pl.BlockSpec((N_CHIPS, tr, cw), lambda i: (0, i, 0)),
        out_shape=jax.ShapeDtypeStruct((N_CHIPS, d, cw), dw_main.dtype), compiler_params=_params("parallel"),
    )(dw_main)


def kernel(x, norm_mix_pre, norm_mix_post, w_in, b_forget, w_branch_sb, w_branch_fox, w_out, norm_ffn_pre, norm_ffn_post, w_ffn_gate, w_ffn_up, w_ffn_down, loss_target, m_norm_mix_pre, m_norm_mix_post, m_w_in, m_b_forget, m_w_branch_sb, m_w_branch_fox, m_w_out, m_norm_ffn_pre, m_norm_ffn_post, m_w_ffn_gate, m_w_ffn_up, m_w_ffn_down, v_norm_mix_pre, v_norm_mix_post, v_w_in, v_b_forget, v_w_branch_sb, v_w_branch_fox, v_w_out, v_norm_ffn_pre, v_norm_ffn_post, v_w_ffn_gate, v_w_ffn_up, v_w_ffn_down):
    s, d = x.shape[1], x.shape[2]
    n_heads = b_forget.shape[1]
    d_att = n_heads * HEAD_DIM
    c_in = w_in.shape[2]
    c_br = w_branch_sb.shape[2]
    c_gu = w_ffn_gate.shape[2]
    d_ff = c_gu * N_CHIPS
    d_in = c_in * N_CHIPS
    f_pad = 512
    n_qkv = 6 * d_att
    n_gf = 2 * d + f_pad
    core = lax.axis_index("c").astype(jnp.int32).reshape(1)
    chip = (2 * lax.axis_index("x") + lax.axis_index("y")).astype(jnp.int32).reshape(1)

    in_send, in_recv, in_bufs, in_token = gather_start("gather_start_w_in", [cast_place("place_w_in", [w_in], chip)])
    ag_send, ag_recv, ag_bufs, ag_token = gather_start("gather_start_rest", [
        cast_place("place_branch", [w_branch_sb, w_branch_fox], chip),
        cast_place("place_out", [w_out + in_token[0, 0]], chip),
        cast_place("place_gate_up", [w_ffn_gate, w_ffn_up], chip),
        cast_place("place_down", [w_ffn_down], chip)])
    g_in, = gather_forward("forward_w_in", gather_wait("gather_wait_w_in", in_bufs, in_send, in_recv,
                                                       [ag_token, m_w_in[0], v_w_in[0]]))
    segments = _w_in_segments(c_in, n_qkv, n_heads, d)
    w_main = regroup_w_in(g_in, segments, n_qkv + n_gf)
    x2 = x[0]
    tgt = loss_target[0]
    b_pad = jnp.pad(b_forget, ((0, 0), (0, LANES - n_heads)))

    u = norm_in(x2, norm_mix_pre)
    qkv = mm(u, w_main, "nn", BF16, "proj_qkv", b_win=(0, n_qkv))
    gf = mm(u, w_main, "nn", F32, "proj_gates", b_win=(n_qkv, n_gf))
    cum = cum_fwd(gf, b_pad, 2 * d)
    cum_heads = cum[:, :n_heads].T
    cum_col, cum_row = cum_heads[:, :, None], cum_heads[:, None, :]
    o_sb = sb_fwd(qkv, n_heads, 0)
    o_fx, lse = fox_fwd(qkv, cum_col, cum_row, n_heads, 3 * n_heads)
    g_br, g_out, g_gu, g_dn = gather_forward(
        "forward_rest", gather_wait("gather_wait_rest", ag_bufs, ag_send, ag_recv, [o_sb, o_fx]))
    w_o = g_out.reshape(d, d)
    w_dn = g_dn.reshape(d_ff, d)
    bsb = mm(o_sb, g_br, "nn", F32, "branch_sb", tn=c_br, chunks=(1, 0))
    bfx = mm(o_fx, g_br, "nn", F32, "branch_fox", tn=c_br, chunks=(1, 1))
    merged = gate_fwd(bsb, bfx, gf)
    mix = mm(merged, w_o, "nn", F32, "out_proj")
    h1, u2 = mid_fwd(x2, mix, norm_mix_post, norm_ffn_pre)
    gu = mm(u2, g_gu, "nn", F32, "ffn_gate_up", tn=c_gu, chunks=(2, 0))
    act = swiglu_fwd(gu, c_gu)
    ff = mm(act, w_dn, "nn", F32, "ffn_down")
    dy, d_ff_out, dg_fpost, loss_part = loss_head(h1, ff, norm_ffn_post, tgt)

    p_dn = mm(act, d_ff_out, "tn", BF16, "dw_ffn_down").reshape(N_CHIPS, d_ff // N_CHIPS, d)
    d_act = mm(d_ff_out, w_dn, "nt", F32, "d_act")
    d_gu = swiglu_bwd(d_act, gu, c_gu)
    p_gu = mm(u2, d_gu, "tn", BF16, "dw_ffn_gate_up", tn=c_gu, chunks=(2, 0),
              out_into=lax.empty((N_CHIPS, d, 2 * c_gu), BF16))
    du2 = mm(d_gu, g_gu, "nt", F32, "d_u2", tk=c_gu, chunks=(2, 0))
    dh1, d_mix, dg_fpre, dg_post = mid_bwd(dy, du2, h1, mix, norm_ffn_pre, norm_mix_post)
    p_out = mm(merged, d_mix, "tn", BF16, "dw_out").reshape(N_CHIPS, d // N_CHIPS, d)
    d_merged = mm(d_mix, w_o, "nt", F32, "d_merged")
    d_bsb, d_bfx, d_gs, d_gx = gate_bwd(d_merged, bsb, bfx, gf)
    p_br = mm(o_sb, d_bsb, "tn", BF16, "dw_branch_sb", tn=c_br, chunks=(1, 0),
              out_into=lax.empty((N_CHIPS, d_att, 2 * c_br), BF16))
    p_br = mm(o_fx, d_bfx, "tn", BF16, "dw_branch_fox", tn=c_br, chunks=(1, 1), out_into=p_br)
    d_osb = mm(d_bsb, g_br, "nt", BF16, "d_o_sb", tk=c_br, chunks=(1, 0))
    d_ofx = mm(d_bfx, g_br, "nt", BF16, "d_o_fox", tk=c_br, chunks=(1, 1))

    def reduce_start(tag, pieces, names):
        from_sibling = swap_halves("swap_halves_" + tag, pieces)
        sums = [pair_sum("pair_sum_" + t, p, q, core) for t, p, q in zip(names, pieces, from_sibling)]
        return scatter_start("scatter_start_" + tag, sums)

    def reduce_end(tag, started, names, after):
        send, recv, sums, lands, _ = started
        sums, lands = scatter_wait("scatter_wait_" + tag, sums, lands, send, recv, after)
        return join_halves("join_halves_" + tag, [chip_sum("chip_sum_" + t, sm, got, chip, core)
                                                  for t, sm, got in zip(names, sums, lands)])

    rest_names = ["branch", "out", "gate_up", "down"]
    rest_started = reduce_start("rest", [p_br, p_out, p_gu, p_dn], rest_names)
    d_osb = d_osb + rest_started[4][0, 0].astype(BF16)
    dq_s, dk_s, dv_s = sb_bwd(qkv, d_osb, n_heads, 0)
    dq_f, dk_f, dv_f, dcq, dck = fox_bwd(qkv, cum_col, cum_row, o_fx, d_ofx, lse, n_heads, 3 * n_heads)
    d_cum = jnp.pad((dcq[:, :, 0] + dck[:, 0, :]).T, ((0, 0), (0, LANES - n_heads)))
    d_f, db_pad = cum_bwd(d_cum, gf, b_pad, 2 * d, n_heads)
    d_main = jnp.concatenate(
        [dq_s, dk_s, dv_s, dq_f, dk_f, dv_f, d_gs, d_gx, d_f, jnp.zeros((s, f_pad - LANES), BF16)], axis=1)
    p_in = regroup_dw_in(mm(u, d_main, "tn", BF16, "dw_in"), segments, c_in)

    in_started = reduce_start("w_in", [p_in], ["in"])
    du = mm(d_main, w_main, "nt", F32, "d_u", after=in_started[4])
    dx, dg_pre = in_bwd(dh1, du, x2, norm_mix_pre + in_started[4][0:1, 0:1])
    gr_br, gr_out, gr_gu, gr_dn = reduce_end("rest", rest_started, rest_names, dx)

    upd_bs = adam_update("adam_branch_sb", w_branch_sb, m_w_branch_sb, v_w_branch_sb, gr_br, 0)
    upd_bf = adam_update("adam_branch_fox", w_branch_fox, m_w_branch_fox, v_w_branch_fox, gr_br, 1)
    upd_o = adam_update("adam_out", w_out, m_w_out, v_w_out, gr_out, 0)

    def pack(rows):
        rows = [jnp.pad(r_, ((0, 0), (0, d - r_.shape[1]))) for r_ in rows]
        return jnp.concatenate(rows + [jnp.zeros((8 - len(rows), d), F32)], axis=0)

    sm_g, sm_d, sm_m, sm_v = small_allreduce_adam(
        pack([dg_pre, dg_post, dg_fpre, dg_fpost, db_pad]),
        pack([norm_mix_pre, norm_mix_post, norm_ffn_pre, norm_ffn_post, b_forget]),
        pack([m_norm_mix_pre, m_norm_mix_post, m_norm_ffn_pre, m_norm_ffn_post, m_b_forget]),
        pack([v_norm_mix_pre, v_norm_mix_post, v_norm_ffn_pre, v_norm_ffn_post, v_b_forget]))

    done = sm_d[0:1, 0:1] + sum(u_[1][0, 0:1, 0:1] for u_ in (upd_bs, upd_bf, upd_o))
    gr_in, = reduce_end("w_in", in_started, ["in"], done)
    upd_in = adam_update("adam_w_in", w_in, m_w_in, v_w_in, gr_in, 0)
    upd_ga = adam_update("adam_gate", w_ffn_gate, m_w_ffn_gate, v_w_ffn_gate, gr_gu, 0, after=upd_in[1])
    upd_up = adam_update("adam_up", w_ffn_up, m_w_ffn_up, v_w_ffn_up, gr_gu, 1, after=upd_in[1])
    upd_dn = adam_update("adam_down", w_ffn_down, m_w_ffn_down, v_w_ffn_down, gr_dn, 0, after=upd_in[1])
    grads, deltas, new_ms, new_vs = zip(upd_in, upd_bs, upd_bf, upd_o, upd_ga, upd_up, upd_dn)

    def small(a):
        return [a[0:1], a[1:2], a[2:3], a[3:4], a[4:5, :n_heads]]

    def ordered(sm, bg):
        return [sm[0], sm[1], bg[0], sm[4], bg[1], bg[2], bg[3], sm[2], sm[3], bg[4], bg[5], bg[6]]

    loss = lax.psum(loss_part[0, 0], ("x", "y", "c"))
    return (loss, dx[None], *ordered(small(sm_g), grads), *ordered(small(sm_d), deltas),
            *ordered(small(sm_m), new_ms), *ordered(small(sm_v), new_vs))
```

```python
import functools

import jax
import jax.numpy as jnp
from jax import lax
from jax.experimental import pallas as pl
from jax.experimental.pallas import tpu as pltpu

F32 = jnp.float32
BF16 = jnp.bfloat16
MESH = pl.DeviceIdType.MESH

HEAD_DIM = 128
LANES = 128
ATT_TILE = 512
ROW_TILE = 256
N_CHIPS = 4
RMS_EPS = 1e-6
ADAM_LR = 0.001
ADAM_B1 = 0.9
ADAM_B2 = 0.999
ADAM_EPS = 1e-08
ADAM_WD = 0.01
ADAM_STEP = 10
NEG_BIG = -1e30
VMEM_LIMIT = 56 * 1024 * 1024
MM_VMEM_BUDGET = 40 * 1024 * 1024
ATT_STRIP = 512

NN = (((1,), (0,)), ((), ()))
NT = (((1,), (1,)), ((), ()))
TN = (((0,), (0,)), ((), ()))


def _tile(n, pref, align):
    best = None
    t = align
    while t <= min(n, pref):
        if n % t == 0:
            best = t
        t += align
    return n if best is None else best


def _params(*sem):
    return pltpu.CompilerParams(dimension_semantics=sem, vmem_limit_bytes=VMEM_LIMIT)


def _mm_tiles(m, n, k, a_bytes, b_bytes, out_bytes, tn, tk):
    tm = _tile(m, 2048, LANES)
    tk = tk or _tile(k, 512, LANES)

    def vmem(t):
        acc = 0 if out_bytes == 4 else tm * t * 4
        return acc + 2 * tm * t * out_bytes + 2 * (tm * tk * a_bytes + tk * t * b_bytes)

    if tn is None:
        fits = [t for t in range(LANES, min(n, 2048) + 1, LANES) if n % t == 0 and vmem(t) <= MM_VMEM_BUDGET]
        tn = max(fits) if fits else _tile(n, LANES, LANES)
    return tm, tn, tk


def mm(a, b, mode, out_dtype, name, *, tn=None, tk=None, b_win=None, chunks=None, out_into=None, after=None):
    n_per, blk0 = chunks if chunks else (1, 0)
    if mode == "nn":
        m, k = a.shape
        n = b.shape[0] * n_per * tn if chunks else (b_win[1] if b_win else b.shape[1])
    elif mode == "nt":
        m = a.shape[0]
        k = b.shape[0] * n_per * tk if chunks else a.shape[1]
        n = b.shape[-2]
    else:
        k, m = a.shape
        n = b.shape[1]
    in_place = jnp.dtype(out_dtype) == jnp.dtype(F32)
    tm, tn, tk = _mm_tiles(m, n, k, a.dtype.itemsize, b.dtype.itemsize, jnp.dtype(out_dtype).itemsize, tn, tk)
    assert m % tm == 0 and n % tn == 0 and k % tk == 0, (name, m, n, k, tm, tn, tk)
    j0 = 0
    if b_win:
        assert b_win[0] % tn == 0
        j0 = b_win[0] // tn
    nk = k // tk
    dims = {"nn": NN, "nt": NT, "tn": TN}[mode]

    def kern(a_ref, b_ref, *rest):
        o_ref, acc_ref = (rest[-1], rest[-1]) if in_place else (rest[-2], rest[-1])
        kk = pl.program_id(2)

        @pl.when(kk == 0)
        def _():
            acc_ref[...] = jnp.zeros_like(acc_ref)

        acc_ref[...] += lax.dot_general(a_ref[...].astype(BF16), b_ref[...].astype(BF16), dims,
                                        preferred_element_type=F32)

        if not in_place:
            @pl.when(kk == nk - 1)
            def _():
                o_ref[...] = acc_ref[...].astype(o_ref.dtype)

    out_spec = pl.BlockSpec((tm, tn), lambda i, j, kk: (i, j))
    out_shape = jax.ShapeDtypeStruct((m, n), out_dtype)
    if mode == "nn":
        a_spec = pl.BlockSpec((tm, tk), lambda i, j, kk: (i, kk))
        if chunks:
            b_spec = pl.BlockSpec((None, tk, tn), lambda i, j, kk: (j // n_per, kk, blk0 + j % n_per))
        else:
            b_spec = pl.BlockSpec((tk, tn), lambda i, j, kk: (kk, j + j0))
    elif mode == "nt":
        a_spec = pl.BlockSpec((tm, tk), lambda i, j, kk: (i, kk))
        if chunks:
            b_spec = pl.BlockSpec((None, tn, tk), lambda i, j, kk: (kk // n_per, j, blk0 + kk % n_per))
        else:
            b_spec = pl.BlockSpec((tn, tk), lambda i, j, kk: (j, kk))
    else:
        a_spec = pl.BlockSpec((tk, tm), lambda i, j, kk: (kk, i))
        b_spec = pl.BlockSpec((tk, tn), lambda i, j, kk: (kk, j))
        if chunks:
            out_spec = pl.BlockSpec((None, tm, tn), lambda i, j, kk: (j // n_per, i, blk0 + j % n_per))
    in_specs, operands, aliases = [a_spec, b_spec], [a, b], {}
    if chunks and mode == "tn":
        assert out_into is not None
        out_shape = jax.ShapeDtypeStruct(out_into.shape, out_dtype)
        in_specs.append(pl.BlockSpec(memory_space=pl.ANY))
        operands.append(out_into)
        aliases = {2: 0}
    if after is not None:
        in_specs.append(pl.BlockSpec(memory_space=pl.ANY))
        operands.append(after)
    return pl.pallas_call(
        kern, name=name, grid=(m // tm, n // tn, nk),
        in_specs=in_specs, out_specs=out_spec, out_shape=out_shape,
        scratch_shapes=[] if in_place else [pltpu.VMEM((tm, tn), F32)], input_output_aliases=aliases,
        compiler_params=_params("parallel", "parallel", "arbitrary"),
    )(*operands)


def _rstd(v):
    return lax.rsqrt(jnp.mean(v * v, axis=-1, keepdims=True) + RMS_EPS)


def _norm_bwd(v, g, dy):
    r = _rstd(v)
    vh = v * r
    dyg = dy * g
    dv = r * (dyg - vh * jnp.mean(dyg * vh, axis=-1, keepdims=True))
    return dv, jnp.sum(dy * vh, axis=0, keepdims=True)


def _row_call(kern, name, ins, outs, s, d):
    tr = _tile(s, ROW_TILE, 16)

    def spec(shape, is_row):
        if is_row:
            return pl.BlockSpec((tr, shape[1]), lambda i: (i, 0))
        return pl.BlockSpec(shape, lambda i: (0, 0))

    return pl.pallas_call(
        kern, name=name, grid=(s // tr,),
        in_specs=[spec(a.shape, r) for a, r in ins],
        out_specs=[spec(sh, r) for sh, _, r in outs],
        out_shape=[jax.ShapeDtypeStruct(sh, dt) for sh, dt, _ in outs],
        compiler_params=_params("arbitrary"),
    )(*[a for a, _ in ins])


def norm_in(x, g):
    s, d = x.shape

    def kern(x_ref, g_ref, u_ref):
        v = x_ref[...]
        u_ref[...] = (v * _rstd(v) * g_ref[...]).astype(BF16)

    return _row_call(kern, "norm_in", [(x, True), (g, False)], [((s, d), BF16, True)], s, d)[0]


def mid_fwd(x, mix, g_post, g_fpre):
    s, d = x.shape

    def kern(x_ref, mix_ref, gp_ref, gf_ref, h1_ref, u2_ref):
        mixv = mix_ref[...]
        h1 = x_ref[...] + mixv * _rstd(mixv) * gp_ref[...]
        h1_ref[...] = h1
        u2_ref[...] = (h1 * _rstd(h1) * gf_ref[...]).astype(BF16)

    return _row_call(kern, "mid_fwd", [(x, True), (mix, True), (g_post, False), (g_fpre, False)],
                     [((s, d), F32, True), ((s, d), BF16, True)], s, d)


def loss_head(h1, ff, g_fpost, target):
    s, d = h1.shape

    def kern(h1_ref, ff_ref, g_ref, t_ref, dy_ref, dff_ref, dg_ref, loss_ref):
        @pl.when(pl.program_id(0) == 0)
        def _():
            dg_ref[...] = jnp.zeros_like(dg_ref)
            loss_ref[...] = jnp.zeros_like(loss_ref)

        ffv = ff_ref[...]
        g = g_ref[...]
        y = h1_ref[...] + ffv * _rstd(ffv) * g
        diff = y - t_ref[...]
        row_loss = jnp.mean(diff * diff, axis=-1, keepdims=True)
        loss_ref[...] += 0.5 * jnp.sum(row_loss, axis=0, keepdims=True)
        dy = diff / d
        dy_ref[...] = dy
        dff, dg = _norm_bwd(ffv, g, dy)
        dff_ref[...] = dff.astype(BF16)
        dg_ref[...] += dg

    return _row_call(kern, "loss_head",
                     [(h1, True), (ff, True), (g_fpost, False), (target, True)],
                     [((s, d), F32, True), ((s, d), BF16, True), ((1, d), F32, False), ((1, 1), F32, False)], s, d)


def mid_bwd(dy, du2, h1, mix, g_fpre, g_post):
    s, d = dy.shape

    def kern(dy_ref, du2_ref, h1_ref, mix_ref, gf_ref, gp_ref, dh1_ref, dmix_ref, dgf_ref, dgp_ref):
        @pl.when(pl.program_id(0) == 0)
        def _():
            dgf_ref[...] = jnp.zeros_like(dgf_ref)
            dgp_ref[...] = jnp.zeros_like(dgp_ref)

        dh, dgf = _norm_bwd(h1_ref[...], gf_ref[...], du2_ref[...])
        dh1 = dy_ref[...] + dh
        dh1_ref[...] = dh1
        dmix, dgp = _norm_bwd(mix_ref[...], gp_ref[...], dh1)
        dmix_ref[...] = dmix.astype(BF16)
        dgf_ref[...] += dgf
        dgp_ref[...] += dgp

    return _row_call(kern, "mid_bwd",
                     [(dy, True), (du2, True), (h1, True), (mix, True), (g_fpre, False), (g_post, False)],
                     [((s, d), F32, True), ((s, d), BF16, True), ((1, d), F32, False), ((1, d), F32, False)], s, d)


def in_bwd(dh1, du, x, g_pre):
    s, d = x.shape

    def kern(dh1_ref, du_ref, x_ref, g_ref, dx_ref, dg_ref):
        @pl.when(pl.program_id(0) == 0)
        def _():
            dg_ref[...] = jnp.zeros_like(dg_ref)

        dxn, dg = _norm_bwd(x_ref[...], g_ref[...], du_ref[...])
        dx_ref[...] = dh1_ref[...] + dxn
        dg_ref[...] += dg

    return _row_call(kern, "in_bwd", [(dh1, True), (du, True), (x, True), (g_pre, False)],
                     [((s, d), F32, True), ((1, d), F32, False)], s, d)


def _sigmoid(v):
    return 1.0 / (1.0 + jnp.exp(-v))


def gate_fwd(bsb, bfx, gf):
    s, d = bsb.shape
    tr, tc = _tile(s, 256, 16), _tile(d, 512, LANES)
    nc = d // tc

    def kern(bsb_ref, bfx_ref, gs_ref, gx_ref, o_ref):
        o_ref[...] = (_sigmoid(gs_ref[...]) * bsb_ref[...] + _sigmoid(gx_ref[...]) * bfx_ref[...]).astype(BF16)

    blk = pl.BlockSpec((tr, tc), lambda i, j: (i, j))
    return pl.pallas_call(
        kern, name="gate_fwd", grid=(s // tr, nc),
        in_specs=[blk, blk, blk, pl.BlockSpec((tr, tc), lambda i, j: (i, j + nc))],
        out_specs=blk, out_shape=jax.ShapeDtypeStruct((s, d), BF16),
        compiler_params=_params("parallel", "parallel"),
    )(bsb, bfx, gf, gf)


def gate_bwd(dmerged, bsb, bfx, gf):
    s, d = bsb.shape
    tr, tc = _tile(s, 256, 16), _tile(d, 512, LANES)
    nc = d // tc

    def kern(dm_ref, bsb_ref, bfx_ref, gs_ref, gx_ref, dbs_ref, dbx_ref, dgs_ref, dgx_ref):
        dm = dm_ref[...]
        ss = _sigmoid(gs_ref[...])
        sx = _sigmoid(gx_ref[...])
        dbs_ref[...] = (dm * ss).astype(BF16)
        dbx_ref[...] = (dm * sx).astype(BF16)
        dgs_ref[...] = (dm * bsb_ref[...] * ss * (1.0 - ss)).astype(BF16)
        dgx_ref[...] = (dm * bfx_ref[...] * sx * (1.0 - sx)).astype(BF16)

    blk = pl.BlockSpec((tr, tc), lambda i, j: (i, j))
    out = jax.ShapeDtypeStruct((s, d), BF16)
    return pl.pallas_call(
        kern, name="gate_bwd", grid=(s // tr, nc),
        in_specs=[blk, blk, blk, blk, pl.BlockSpec((tr, tc), lambda i, j: (i, j + nc))],
        out_specs=[blk, blk, blk, blk], out_shape=[out, out, out, out],
        compiler_params=_params("parallel", "parallel"),
    )(dmerged, bsb, bfx, gf, gf)


def swiglu_fwd(gu, cw):
    s, f2 = gu.shape
    tr = _tile(s, 256, 16)

    def kern(gu_ref, o_ref):
        g = gu_ref[:, :cw]
        o_ref[...] = (g * _sigmoid(g) * gu_ref[:, cw:]).astype(BF16)

    return pl.pallas_call(
        kern, name="swiglu_fwd", grid=(s // tr, f2 // (2 * cw)),
        in_specs=[pl.BlockSpec((tr, 2 * cw), lambda i, j: (i, j))],
        out_specs=pl.BlockSpec((tr, cw), lambda i, j: (i, j)),
        out_shape=jax.ShapeDtypeStruct((s, f2 // 2), BF16),
        compiler_params=_params("parallel", "parallel"),
    )(gu)


def swiglu_bwd(dact, gu, cw):
    s, f2 = gu.shape
    tr = _tile(s, 256, 16)

    def kern(da_ref, gu_ref, o_ref):
        da = da_ref[...]
        g = gu_ref[:, :cw]
        sg = _sigmoid(g)
        o_ref[:, :cw] = (da * gu_ref[:, cw:] * (sg * (1.0 + g * (1.0 - sg)))).astype(BF16)
        o_ref[:, cw:] = (da * (g * sg)).astype(BF16)

    return pl.pallas_call(
        kern, name="swiglu_bwd", grid=(s // tr, f2 // (2 * cw)),
        in_specs=[pl.BlockSpec((tr, cw), lambda i, j: (i, j)), pl.BlockSpec((tr, 2 * cw), lambda i, j: (i, j))],
        out_specs=pl.BlockSpec((tr, 2 * cw), lambda i, j: (i, j)),
        out_shape=jax.ShapeDtypeStruct((s, f2), BF16),
        compiler_params=_params("parallel", "parallel"),
    )(dact, gu)


def _split3(v):
    hi = v.astype(BF16)
    r = v - hi.astype(F32)
    mid = r.astype(BF16)
    lo = (r - mid.astype(F32)).astype(BF16)
    return hi, mid, lo


def _dot3_right(v, ones):
    hi, mid, lo = _split3(v)
    d = lambda p: jnp.dot(p, ones, preferred_element_type=F32)
    return (d(lo) + d(mid)) + d(hi)


def _dot3_left(ones, v):
    hi, mid, lo = _split3(v)
    d = lambda p: jnp.dot(ones, p, preferred_element_type=F32)
    return (d(lo) + d(mid)) + d(hi)


def _split2(v):
    hi = v.astype(BF16)
    return hi, (v - hi.astype(F32)).astype(BF16)


def _dot2_right(v, ones):
    hi, lo = _split2(v)
    return jnp.dot(lo, ones, preferred_element_type=F32) + jnp.dot(hi, ones, preferred_element_type=F32)


def _log1p_exp_neg_abs(v):
    return jnp.log(1.0 + jnp.exp(-jnp.abs(v)))


def _mask01(cond):
    return jnp.where(cond, 1.0, 0.0).astype(BF16)


def _iota2(t):
    return (lax.broadcasted_iota(jnp.int32, (t, t), 0), lax.broadcasted_iota(jnp.int32, (t, t), 1))


def cum_fwd(gf, b_pad, f_col0):
    s = gf.shape[0]
    t = _tile(s, ATT_TILE, LANES)
    fb = f_col0 // LANES

    def kern(f_ref, b_ref, cum_ref, carry_ref):
        @pl.when(pl.program_id(0) == 0)
        def _():
            carry_ref[...] = jnp.zeros_like(carry_ref)

        v = f_ref[...] + b_ref[...]
        lf = jnp.minimum(v, 0.0) - _log1p_exp_neg_abs(v)
        row, col = _iota2(t)
        cum = _dot3_left(_mask01(col <= row), lf) + carry_ref[...]
        cum_ref[...] = cum
        carry_ref[...] = cum[t - 1:t, :]

    return pl.pallas_call(
        kern, name="cum_fwd", grid=(s // t,),
        in_specs=[pl.BlockSpec((t, LANES), lambda i: (i, fb)), pl.BlockSpec((1, LANES), lambda i: (0, 0))],
        out_specs=pl.BlockSpec((t, LANES), lambda i: (i, 0)),
        out_shape=jax.ShapeDtypeStruct((s, LANES), F32),
        scratch_shapes=[pltpu.VMEM((1, LANES), F32)],
        compiler_params=_params("arbitrary"),
    )(gf, b_pad)


def cum_bwd(dcum, gf, b_pad, f_col0, n_heads):
    s = gf.shape[0]
    t = _tile(s, ATT_TILE, LANES)
    nb = s // t
    fb = f_col0 // LANES

    def kern(dc_ref, f_ref, b_ref, df_ref, db_ref, carry_ref):
        @pl.when(pl.program_id(0) == 0)
        def _():
            carry_ref[...] = jnp.zeros_like(carry_ref)
            db_ref[...] = jnp.zeros_like(db_ref)

        row, col = _iota2(t)
        dlf = _dot3_left(_mask01(col >= row), dc_ref[...]) + carry_ref[...]
        carry_ref[...] = dlf[0:1, :]
        v = f_ref[...] + b_ref[...]
        sig_neg = jnp.exp(-jnp.maximum(v, 0.0) - _log1p_exp_neg_abs(v))
        lane = lax.broadcasted_iota(jnp.int32, (t, LANES), 1)
        df = jnp.where(lane < n_heads, dlf * sig_neg, 0.0)
        df_ref[...] = df.astype(BF16)
        db_ref[...] += jnp.sum(df, axis=0, keepdims=True)

    return pl.pallas_call(
        kern, name="cum_bwd", grid=(nb,),
        in_specs=[pl.BlockSpec((t, LANES), lambda i: (nb - 1 - i, 0)),
                  pl.BlockSpec((t, LANES), lambda i: (nb - 1 - i, fb)),
                  pl.BlockSpec((1, LANES), lambda i: (0, 0))],
        out_specs=[pl.BlockSpec((t, LANES), lambda i: (nb - 1 - i, 0)), pl.BlockSpec((1, LANES), lambda i: (0, 0))],
        out_shape=[jax.ShapeDtypeStruct((s, LANES), BF16), jax.ShapeDtypeStruct((1, LANES), F32)],
        scratch_shapes=[pltpu.VMEM((1, LANES), F32)],
        compiler_params=_params("arbitrary"),
    )(dcum, gf, b_pad)


def _qkv_specs(s, t, n_heads, base):
    return [pl.BlockSpec((t, HEAD_DIM), lambda h, i: (i, base + h)),
            pl.BlockSpec((s, HEAD_DIM), lambda h, i: (0, base + n_heads + h)),
            pl.BlockSpec((s, HEAD_DIM), lambda h, i: (0, base + 2 * n_heads + h))]


def _strips(t):
    sr = _tile(t, ATT_STRIP, 8)
    return sr, t // sr, [slice(si * sr, (si + 1) * sr) for si in range(t // sr)]


def _key_minus_row(sr, t):
    return lax.broadcasted_iota(jnp.int32, (sr, t), 1) - lax.broadcasted_iota(jnp.int32, (sr, t), 0)


def _keep(valid, v):
    return v if valid is None else jnp.where(valid, v, 0.0)


def _sb_scores(q, k, diff, lim):
    z = lax.dot_general(q, k, NT, preferred_element_type=F32) * (HEAD_DIM ** -0.5)
    valid = None if lim is None else diff < lim
    l1p = _log1p_exp_neg_abs(z)
    return z, valid, l1p, _keep(valid, -jnp.maximum(z, 0.0) - l1p)


def sb_fwd(qkv, n_heads, base):
    s = qkv.shape[0]
    t = _tile(s, ATT_TILE, LANES)
    sr, ns, strips = _strips(t)

    def kern(q_ref, k_ref, v_ref, o_ref):
        i = pl.program_id(1)
        row, col = _iota2(t)
        after = _mask01(row > col)
        diff = _key_minus_row(sr, t)
        qs = [q_ref[sl, :] for sl in strips]

        def tile(j, carry, diagonal):
            runs, accs = carry
            off = pl.multiple_of(j * t, t)
            k = k_ref[pl.ds(off, t), :]
            v = v_ref[pl.ds(off, t), :]
            new_runs, new_accs = [], []
            for si in range(ns):
                z, valid, l1p, log_keep = _sb_scores(qs[si], k, diff, si * sr if diagonal else None)
                between = _dot2_right(log_keep, after) + runs[si]
                w = _keep(valid, jnp.exp(jnp.minimum(z, 0.0) - l1p + between))
                new_accs.append(accs[si] + jnp.dot(w.astype(BF16), v, preferred_element_type=F32))
                new_runs.append(runs[si] + jnp.sum(log_keep, axis=1, keepdims=True))
            return tuple(new_runs), tuple(new_accs)

        init = (tuple(jnp.zeros((sr, 1), F32) for _ in strips), tuple(jnp.zeros((sr, HEAD_DIM), F32) for _ in strips))
        _, accs = lax.fori_loop(0, i, lambda jj, c: tile(i - 1 - jj, c, False), tile(i, init, True))
        for sl, acc in zip(strips, accs):
            o_ref[sl, :] = acc.astype(o_ref.dtype)

    return pl.pallas_call(
        kern, name="sb_fwd", grid=(n_heads, s // t),
        in_specs=_qkv_specs(s, t, n_heads, base),
        out_specs=pl.BlockSpec((t, HEAD_DIM), lambda h, i: (i, h)),
        out_shape=jax.ShapeDtypeStruct((s, n_heads * HEAD_DIM), BF16),
        compiler_params=_params("parallel", "arbitrary"),
    )(qkv, qkv, qkv)


def sb_bwd(qkv, d_o, n_heads, base):
    s = qkv.shape[0]
    t = _tile(s, ATT_TILE, LANES)
    nq = s // t
    sr, ns, strips = _strips(t)
    scale = HEAD_DIM ** -0.5

    def kern(q_ref, k_ref, v_ref, do_ref, dq_ref, dk_ref, dv_ref, dk_acc, dv_acc, run_ref):
        i = pl.program_id(1)

        @pl.when(i == 0)
        def _():
            dk_acc[...] = jnp.zeros_like(dk_acc)
            dv_acc[...] = jnp.zeros_like(dv_acc)

        row, col = _iota2(t)
        after = _mask01(row > col)
        before = _mask01(row < col)
        diff = _key_minus_row(sr, t)
        qs = [q_ref[sl, :] for sl in strips]
        dos = [do_ref[sl, :] for sl in strips]

        def sweep1(j, runs, diagonal):
            k = k_ref[pl.ds(pl.multiple_of(j * t, t), t), :]
            new_runs = []
            for si, sl in enumerate(strips):
                _, _, _, log_keep = _sb_scores(qs[si], k, diff, si * sr if diagonal else None)
                run_ref[j, sl, :] = runs[si]
                new_runs.append(runs[si] + jnp.sum(log_keep, axis=1, keepdims=True))
            return tuple(new_runs)

        lax.fori_loop(0, i, lambda jj, c: sweep1(i - 1 - jj, c, False),
                      sweep1(i, tuple(jnp.zeros((sr, 1), F32) for _ in strips), True))

        def sweep2(j, carry, diagonal):
            run_es, dqs = carry
            off = pl.multiple_of(j * t, t)
            k = k_ref[pl.ds(off, t), :]
            v = v_ref[pl.ds(off, t), :]
            new_es, new_dqs = [], []
            dk_t = jnp.zeros((t, HEAD_DIM), F32)
            dv_t = jnp.zeros((t, HEAD_DIM), F32)
            for si, sl in enumerate(strips):
                z, valid, l1p, log_keep = _sb_scores(qs[si], k, diff, si * sr if diagonal else None)
                between = _dot2_right(log_keep, after) + run_ref[j, sl, :]
                w = _keep(valid, jnp.exp(jnp.minimum(z, 0.0) - l1p + between))
                dw = lax.dot_general(dos[si], v, NT, preferred_element_type=F32)
                e = dw * w
                e_before = _dot2_right(e, before) + run_es[si]
                keep = jnp.exp(log_keep)
                dz = _keep(valid, e * keep - e_before * (1.0 - keep)) * scale
                dzb = dz.astype(BF16)
                new_dqs.append(dqs[si] + jnp.dot(dzb, k, preferred_element_type=F32))
                dk_t = dk_t + lax.dot_general(dzb, qs[si], TN, preferred_element_type=F32)
                dv_t = dv_t + lax.dot_general(w.astype(BF16), dos[si], TN, preferred_element_type=F32)
                new_es.append(run_es[si] + jnp.sum(e, axis=1, keepdims=True))
            dk_acc[pl.ds(off, t), :] += dk_t
            dv_acc[pl.ds(off, t), :] += dv_t
            return tuple(new_es), tuple(new_dqs)

        init = (tuple(jnp.zeros((sr, 1), F32) for _ in strips), tuple(jnp.zeros((sr, HEAD_DIM), F32) for _ in strips))
        _, dqs = sweep2(i, lax.fori_loop(0, i, lambda j, c: sweep2(j, c, False), init), True)
        for sl, dq in zip(strips, dqs):
            dq_ref[sl, :] = dq.astype(BF16)

        @pl.when(i == nq - 1)
        def _():
            dk_ref[...] = dk_acc[...].astype(BF16)
            dv_ref[...] = dv_acc[...].astype(BF16)

    out = jax.ShapeDtypeStruct((s, n_heads * HEAD_DIM), BF16)
    head_blk = pl.BlockSpec((s, HEAD_DIM), lambda h, i: (0, h))
    tile_blk = pl.BlockSpec((t, HEAD_DIM), lambda h, i: (i, h))
    return pl.pallas_call(
        kern, name="sb_bwd", grid=(n_heads, nq),
        in_specs=_qkv_specs(s, t, n_heads, base) + [tile_blk],
        out_specs=[tile_blk, head_blk, head_blk],
        out_shape=[out, out, out],
        scratch_shapes=[pltpu.VMEM((s, HEAD_DIM), F32), pltpu.VMEM((s, HEAD_DIM), F32), pltpu.VMEM((nq, t, 1), F32)],
        compiler_params=_params("parallel", "arbitrary"),
    )(qkv, qkv, qkv, d_o)


def _fox_scores(q, k, cq, ck, diff, lim):
    sc = lax.dot_general(q, k, NT, preferred_element_type=F32) * (HEAD_DIM ** -0.5)
    sc = sc + cq - ck
    if lim is None:
        return sc, None
    valid = diff < lim
    return jnp.where(valid, sc, NEG_BIG), valid


def fox_fwd(qkv, cum_col, cum_row, n_heads, base):
    s = qkv.shape[0]
    t = _tile(s, ATT_TILE, LANES)
    sr, ns, strips = _strips(t)

    def kern(q_ref, k_ref, v_ref, cq_ref, ck_ref, o_ref, lse_ref):
        i = pl.program_id(1)
        diff = _key_minus_row(sr, t)
        qs = [q_ref[sl, :] for sl in strips]
        cqs = [cq_ref[0, sl, :] for sl in strips]

        def tile(j, carry, diagonal):
            off = pl.multiple_of(j * t, t)
            k = k_ref[pl.ds(off, t), :]
            v = v_ref[pl.ds(off, t), :]
            ck = ck_ref[0, :, pl.ds(off, t)]
            out = []
            for si in range(ns):
                m, l, acc = carry[si]
                sc, _ = _fox_scores(qs[si], k, cqs[si], ck, diff, si * sr + 1 if diagonal else None)
                m_new = jnp.maximum(m, jnp.max(sc, axis=1, keepdims=True))
                p = jnp.exp(sc - m_new)
                alpha = jnp.exp(m - m_new)
                l = alpha * l + jnp.sum(p, axis=1, keepdims=True)
                acc = alpha * acc + jnp.dot(p.astype(BF16), v, preferred_element_type=F32)
                out.append((m_new, l, acc))
            return tuple(out)

        init = tuple((jnp.full((sr, 1), NEG_BIG, F32), jnp.zeros((sr, 1), F32), jnp.zeros((sr, HEAD_DIM), F32))
                     for _ in strips)
        res = tile(i, lax.fori_loop(0, i, lambda j, c: tile(j, c, False), init), True)
        for sl, (m, l, acc) in zip(strips, res):
            o_ref[sl, :] = acc / l
            lse_ref[0, sl, :] = m + jnp.log(l)

    col_blk = pl.BlockSpec((1, t, 1), lambda h, i: (h, i, 0))
    return pl.pallas_call(
        kern, name="fox_fwd", grid=(n_heads, s // t),
        in_specs=_qkv_specs(s, t, n_heads, base) + [col_blk, pl.BlockSpec((1, 1, s), lambda h, i: (h, 0, 0))],
        out_specs=[pl.BlockSpec((t, HEAD_DIM), lambda h, i: (i, h)), col_blk],
        out_shape=[jax.ShapeDtypeStruct((s, n_heads * HEAD_DIM), F32), jax.ShapeDtypeStruct((n_heads, s, 1), F32)],
        compiler_params=_params("parallel", "arbitrary"),
    )(qkv, qkv, qkv, cum_col, cum_row)


def fox_bwd(qkv, cum_col, cum_row, o, d_o, lse, n_heads, base):
    s = qkv.shape[0]
    t = _tile(s, ATT_TILE, LANES)
    nq = s // t
    sr, ns, strips = _strips(t)
    scale = HEAD_DIM ** -0.5

    def kern(q_ref, k_ref, v_ref, cq_ref, ck_ref, o_ref, do_ref, lse_ref,
             dq_ref, dk_ref, dv_ref, dcq_ref, dck_ref, dk_acc, dv_acc, dck_acc):
        i = pl.program_id(1)

        @pl.when(i == 0)
        def _():
            dk_acc[...] = jnp.zeros_like(dk_acc)
            dv_acc[...] = jnp.zeros_like(dv_acc)
            dck_acc[...] = jnp.zeros_like(dck_acc)

        diff = _key_minus_row(sr, t)
        qs = [q_ref[sl, :] for sl in strips]
        dos = [do_ref[sl, :] for sl in strips]
        cqs = [cq_ref[0, sl, :] for sl in strips]
        lses = [lse_ref[0, sl, :] for sl in strips]
        deltas = [jnp.sum(dos[si].astype(F32) * o_ref[sl, :], axis=1, keepdims=True) for si, sl in enumerate(strips)]

        def tile(j, carry, diagonal):
            off = pl.multiple_of(j * t, t)
            k = k_ref[pl.ds(off, t), :]
            v = v_ref[pl.ds(off, t), :]
            ck = ck_ref[0, :, pl.ds(off, t)]
            out = []
            dk_t = jnp.zeros((t, HEAD_DIM), F32)
            dv_t = jnp.zeros((t, HEAD_DIM), F32)
            dck_t = jnp.zeros((1, t), F32)
            for si in range(ns):
                dq, dcq = carry[si]
                sc, valid = _fox_scores(qs[si], k, cqs[si], ck, diff, si * sr + 1 if diagonal else None)
                p = _keep(valid, jnp.exp(sc - lses[si]))
                dp = lax.dot_general(dos[si], v, NT, preferred_element_type=F32)
                ds = p * (dp - deltas[si])
                dsb = (ds * scale).astype(BF16)
                dq = dq + jnp.dot(dsb, k, preferred_element_type=F32)
                dk_t = dk_t + lax.dot_general(dsb, qs[si], TN, preferred_element_type=F32)
                dv_t = dv_t + lax.dot_general(p.astype(BF16), dos[si], TN, preferred_element_type=F32)
                dck_t = dck_t + jnp.sum(ds, axis=0, keepdims=True)
                out.append((dq, dcq + jnp.sum(ds, axis=1, keepdims=True)))
            dk_acc[pl.ds(off, t), :] += dk_t
            dv_acc[pl.ds(off, t), :] += dv_t
            dck_acc[:, pl.ds(off, t)] -= dck_t
            return tuple(out)

        init = tuple((jnp.zeros((sr, HEAD_DIM), F32), jnp.zeros((sr, 1), F32)) for _ in strips)
        res = tile(i, lax.fori_loop(0, i, lambda j, c: tile(j, c, False), init), True)
        for sl, (dq, dcq) in zip(strips, res):
            dq_ref[sl, :] = dq.astype(BF16)
            dcq_ref[0, sl, :] = dcq

        @pl.when(i == nq - 1)
        def _():
            dk_ref[...] = dk_acc[...].astype(BF16)
            dv_ref[...] = dv_acc[...].astype(BF16)
            dck_ref[0] = dck_acc[...]

    out = jax.ShapeDtypeStruct((s, n_heads * HEAD_DIM), BF16)
    head_blk = pl.BlockSpec((s, HEAD_DIM), lambda h, i: (0, h))
    tile_blk = pl.BlockSpec((t, HEAD_DIM), lambda h, i: (i, h))
    col_blk = pl.BlockSpec((1, t, 1), lambda h, i: (h, i, 0))
    row_blk = pl.BlockSpec((1, 1, s), lambda h, i: (h, 0, 0))
    return pl.pallas_call(
        kern, name="fox_bwd", grid=(n_heads, nq),
        in_specs=_qkv_specs(s, t, n_heads, base) + [col_blk, row_blk, tile_blk, tile_blk, col_blk],
        out_specs=[tile_blk, head_blk, head_blk, col_blk, row_blk],
        out_shape=[out, out, out, jax.ShapeDtypeStruct((n_heads, s, 1), F32),
                   jax.ShapeDtypeStruct((n_heads, 1, s), F32)],
        scratch_shapes=[pltpu.VMEM((s, HEAD_DIM), F32), pltpu.VMEM((s, HEAD_DIM), F32), pltpu.VMEM((1, s), F32)],
        compiler_params=_params("parallel", "arbitrary"),
    )(qkv, qkv, qkv, cum_col, cum_row, o, d_o, lse)


def _place():
    x, y, c = lax.axis_index("x"), lax.axis_index("y"), lax.axis_index("c")
    other_chips = [(1 - x, y), (x, 1 - y), (1 - x, 1 - y)]
    return x, y, c, other_chips


ANY = pl.BlockSpec(memory_space=pl.ANY)


def _remote(src, dst, send_sem, recv_sem, dev):
    return pltpu.make_async_remote_copy(src_ref=src, dst_ref=dst, send_sem=send_sem, recv_sem=recv_sem,
                                        device_id=dev, device_id_type=MESH)


def cast_place(name, ws, chip):
    r = ws[0].shape[1]
    cs = [w.shape[2] for w in ws]
    tr = _tile(r, 256, 16)

    def kern(chip_ref, *refs):
        o_ref = refs[-1]
        off = 0
        for w_ref, c in zip(refs[:-1], cs):
            o_ref[:, off:off + c] = w_ref[...].astype(BF16)
            off += c

    return pl.pallas_call(
        kern, name=name,
        grid_spec=pltpu.PrefetchScalarGridSpec(
            num_scalar_prefetch=1, grid=(r // tr,),
            in_specs=[pl.BlockSpec((None, tr, c), lambda i, chip_ref: (0, i, 0)) for c in cs],
            out_specs=pl.BlockSpec((None, tr, sum(cs)), lambda i, chip_ref: (chip_ref[0], i, 0))),
        out_shape=jax.ShapeDtypeStruct((N_CHIPS, r, sum(cs)), BF16),
        compiler_params=_params("parallel"),
    )(chip, *ws)


HBM = pl.BlockSpec(memory_space=pltpu.HBM)
SEM = pl.BlockSpec(memory_space=pltpu.SEMAPHORE)
SPLIT = pltpu.CompilerParams(has_side_effects=pltpu.SideEffectType.DATAFLOW_SIDE_EFFECTING)


def _in_hbm(a):
    return pltpu.with_memory_space_constraint(a, pltpu.HBM)


def _slab_rows(ref, k, core):
    half = ref.shape[1] // 2
    return ref.at[k, pl.ds(pl.multiple_of(core * half, 16), half)]


def gather_start(name, bufs):
    n = len(bufs)

    def body(*refs):
        ins, send, recv, token = refs[:n], refs[n], refs[n + 1], refs[-1]
        x, y, c, chips = _place()
        me = 2 * x + y
        for a in range(n):
            for j in range(3):
                rows = _slab_rows(ins[a], me, c)
                _remote(rows, rows, send.at[3 * a + j], recv.at[3 * a + j], (chips[j][0], chips[j][1], c)).start()
        token[...] = jnp.zeros_like(token)

    sem = pltpu.SemaphoreType.DMA((3 * n,))
    res = pl.pallas_call(
        body, name=name, in_specs=[HBM] * n, out_specs=[SEM, SEM] + [HBM] * n + [pl.BlockSpec(memory_space=pltpu.VMEM)],
        out_shape=[sem, sem] + [pltpu.HBM(b.shape, b.dtype) for b in bufs] + [jax.ShapeDtypeStruct((8, LANES), F32)],
        input_output_aliases={a: 2 + a for a in range(n)}, compiler_params=SPLIT,
    )(*[_in_hbm(b) for b in bufs])
    return res[0], res[1], res[2:2 + n], res[-1]


def gather_wait(name, bufs, send_sems, recv_sems, after):
    n = len(bufs)

    def body(*refs):
        ins, send, recv = refs[:n], refs[n], refs[n + 1]
        x, y, c, chips = _place()
        me = 2 * x + y
        for a in range(n):
            for j in range(3):
                dev = (chips[j][0], chips[j][1], c)
                mine = _slab_rows(ins[a], me, c)
                _remote(mine, mine, send.at[3 * a + j], recv.at[3 * a + j], dev).wait_send()
                land = _slab_rows(ins[a], 2 * chips[j][0] + chips[j][1], c)
                _remote(land, land, send.at[3 * a + j], recv.at[3 * a + j], dev).wait_recv()

    return pl.pallas_call(
        body, name=name, in_specs=[HBM] * n + [SEM, SEM] + [ANY] * len(after), out_specs=[HBM] * n,
        out_shape=[pltpu.HBM(b.shape, b.dtype) for b in bufs],
        input_output_aliases={a: a for a in range(n)}, compiler_params=SPLIT,
    )(*bufs, send_sems, recv_sems, *after)


def gather_forward(name, bufs):
    n = len(bufs)

    def body(*refs):
        outs = refs[n:2 * n]
        send_sems, recv_sems = refs[2 * n:]
        x, y, c, chips = _place()
        sibling = (x, y, 1 - c)

        def d2d(a, j, core):
            rows = _slab_rows(outs[a], 2 * chips[j][0] + chips[j][1], core)
            return _remote(rows, rows, send_sems.at[3 * a + j], recv_sems.at[3 * a + j], sibling)

        pairs = [(a, j) for a in range(n) for j in range(3)]
        for a, j in pairs:
            d2d(a, j, c).start()
        for a, j in pairs:
            d2d(a, j, 1 - c).wait_recv()
        for a, j in pairs:
            d2d(a, j, c).wait_send()

    return pl.pallas_call(
        body, name=name, in_specs=[ANY] * n, out_specs=[ANY] * n,
        out_shape=[jax.ShapeDtypeStruct(b.shape, b.dtype) for b in bufs],
        input_output_aliases={a: a for a in range(n)},
        scratch_shapes=[pltpu.SemaphoreType.DMA((3 * n,)), pltpu.SemaphoreType.DMA((3 * n,))],
    )(*bufs)


def _forward_plan(refs):
    x, y, c, chips = _place()
    out = []
    for ref in refs:
        for j in range(3):
            k = 2 * chips[j][0] + chips[j][1]
            out.append((_slab_rows(ref, k, c), _slab_rows(ref, k, c), _slab_rows(ref, k, 1 - c)))
    return out


def _swap_plan(refs):
    x, y, c, _ = _place()
    n = len(refs) // 2
    out = []
    for a in range(n):
        half = refs[a].shape[1] // 2
        src = refs[a].at[:, pl.ds(pl.multiple_of((1 - c) * half, 16), half), :]
        out.append((src, refs[n + a], refs[n + a]))
    return out


def sibling_start(name, arrays, plan, n_copies):
    n = len(arrays)

    def body(*refs):
        send, recv, token = refs[n], refs[n + 1], refs[-1]
        x, y, c, _ = _place()
        for idx, (src, dst, _) in enumerate(plan(refs[:n])):
            _remote(src, dst, send.at[idx], recv.at[idx], (x, y, 1 - c)).start()
        token[...] = jnp.zeros_like(token)

    sem = pltpu.SemaphoreType.DMA((n_copies,))
    res = pl.pallas_call(
        body, name=name, in_specs=[HBM] * n, out_specs=[SEM, SEM] + [HBM] * n + [pl.BlockSpec(memory_space=pltpu.VMEM)],
        out_shape=[sem, sem] + [pltpu.HBM(b.shape, b.dtype) for b in arrays] + [jax.ShapeDtypeStruct((8, LANES), F32)],
        input_output_aliases={a: 2 + a for a in range(n)}, compiler_params=SPLIT,
    )(*[_in_hbm(b) for b in arrays])
    return res[0], res[1], res[2:2 + n], res[-1]


def sibling_wait(name, arrays, send_sems, recv_sems, plan, after):
    n = len(arrays)

    def body(*refs):
        send, recv = refs[n], refs[n + 1]
        x, y, c, _ = _place()
        for idx, (src, dst, filled) in enumerate(plan(refs[:n])):
            _remote(src, dst, send.at[idx], recv.at[idx], (x, y, 1 - c)).wait_send()
            _remote(filled, filled, send.at[idx], recv.at[idx], (x, y, 1 - c)).wait_recv()

    return pl.pallas_call(
        body, name=name, in_specs=[HBM] * n + [SEM, SEM] + [ANY] * len(after), out_specs=[HBM] * n,
        out_shape=[pltpu.HBM(b.shape, b.dtype) for b in arrays],
        input_output_aliases={a: a for a in range(n)}, compiler_params=SPLIT,
    )(*arrays, send_sems, recv_sems, *after)


def swap_halves(name, pieces):
    n = len(pieces)
    halves = [p.shape[1] // 2 for p in pieces]

    def body(*refs):
        ins, outs = refs[:n], refs[n:2 * n]
        send_sems, recv_sems = refs[2 * n:]
        x, y, c, _ = _place()
        cps = [_remote(ins[a].at[:, pl.ds(pl.multiple_of((1 - c) * halves[a], 16), halves[a]), :], outs[a],
                       send_sems.at[a], recv_sems.at[a], (x, y, 1 - c)) for a in range(n)]
        for cp in cps:
            cp.start()
        for cp in cps:
            cp.wait()

    return pl.pallas_call(
        body, name=name, in_specs=[ANY] * n, out_specs=[ANY] * n,
        out_shape=[jax.ShapeDtypeStruct((N_CHIPS, h, p.shape[2]), p.dtype) for p, h in zip(pieces, halves)],
        scratch_shapes=[pltpu.SemaphoreType.DMA((n,)), pltpu.SemaphoreType.DMA((n,))],
    )(*pieces)


def pair_sum(name, pieces, got, core):
    _, r, w = pieces.shape
    half = r // 2
    tr = _tile(half, 256, 16)

    def kern(core_ref, p_ref, g_ref, o_ref):
        o_ref[...] = (p_ref[...].astype(F32) + g_ref[...].astype(F32)).astype(o_ref.dtype)

    return pl.pallas_call(
        kern, name=name,
        grid_spec=pltpu.PrefetchScalarGridSpec(
            num_scalar_prefetch=1, grid=(N_CHIPS, half // tr),
            in_specs=[pl.BlockSpec((None, None, tr, w), lambda k, i, core_ref: (k, core_ref[0], i, 0)),
                      pl.BlockSpec((None, tr, w), lambda k, i, core_ref: (k, i, 0))],
            out_specs=pl.BlockSpec((None, tr, w), lambda k, i, core_ref: (k, i, 0))),
        out_shape=jax.ShapeDtypeStruct((N_CHIPS, half, w), pieces.dtype),
        compiler_params=_params("parallel", "parallel"),
    )(core, pieces.reshape(N_CHIPS, 2, half, w), got)


def _scatter_copies(sums, lands, send, recv):
    x, y, c, chips = _place()
    return [_remote(sums[a].at[2 * chips[j][0] + chips[j][1]], lands[a].at[j], send.at[3 * a + j], recv.at[3 * a + j],
                    (chips[j][0], chips[j][1], c)) for a in range(len(sums)) for j in range(3)]


def scatter_start(name, sums):
    n = len(sums)
    lands = [lax.empty((3,) + t.shape[1:], t.dtype) for t in sums]

    def body(*refs):
        ins, land_in, send, recv, token = refs[:n], refs[n:2 * n], refs[2 * n], refs[2 * n + 1], refs[-1]
        for cp in _scatter_copies(ins, land_in, send, recv):
            cp.start()
        token[...] = jnp.zeros_like(token)

    sem = pltpu.SemaphoreType.DMA((3 * n,))
    res = pl.pallas_call(
        body, name=name, in_specs=[HBM] * (2 * n),
        out_specs=[SEM, SEM] + [HBM] * (2 * n) + [pl.BlockSpec(memory_space=pltpu.VMEM)],
        out_shape=[sem, sem] + [pltpu.HBM(t.shape, t.dtype) for t in sums + lands] + [jax.ShapeDtypeStruct((8, LANES), F32)],
        input_output_aliases={a: 2 + a for a in range(2 * n)}, compiler_params=SPLIT,
    )(*[_in_hbm(t) for t in sums + lands])
    return res[0], res[1], res[2:2 + n], res[2 + n:2 + 2 * n], res[-1]


def scatter_wait(name, sums, lands, send_sems, recv_sems, after):
    n = len(sums)

    def body(*refs):
        ins, land_in, send, recv = refs[:n], refs[n:2 * n], refs[2 * n], refs[2 * n + 1]
        for cp in _scatter_copies(ins, land_in, send, recv):
            cp.wait_send()
            cp.wait_recv()

    res = pl.pallas_call(
        body, name=name, in_specs=[HBM] * (2 * n) + [SEM, SEM, ANY], out_specs=[HBM] * (2 * n),
        out_shape=[pltpu.HBM(t.shape, t.dtype) for t in sums + lands],
        input_output_aliases={a: a for a in range(2 * n)}, compiler_params=SPLIT,
    )(*sums, *lands, send_sems, recv_sems, after)
    return res[:n], res[n:]


def chip_sum(name, sums, got, chip, core):
    _, half, w = sums.shape
    tr = _tile(half, 256, 16)
    nb = half // tr

    def kern(ids_ref, s_ref, g0_ref, g1_ref, g2_ref, o_ref):
        o_ref[...] = ((s_ref[...].astype(F32) + g0_ref[...].astype(F32)) + g1_ref[...].astype(F32)) \
            + g2_ref[...].astype(F32)

    def got_spec(j):
        return pl.BlockSpec((None, tr, w), lambda i, ids_ref: (j, i, 0))

    return pl.pallas_call(
        kern, name=name,
        grid_spec=pltpu.PrefetchScalarGridSpec(
            num_scalar_prefetch=1, grid=(nb,),
            in_specs=[pl.BlockSpec((None, tr, w), lambda i, ids_ref: (ids_ref[0], i, 0)),
                      got_spec(0), got_spec(1), got_spec(2)],
            out_specs=pl.BlockSpec((tr, w), lambda i, ids_ref: (ids_ref[1] * nb + i, 0))),
        out_shape=jax.ShapeDtypeStruct((2 * half, w), F32),
        compiler_params=_params("parallel"),
    )(jnp.concatenate([chip, core]), sums, got, got, got)


def join_halves(name, shards):
    n = len(shards)
    halves = [g.shape[0] // 2 for g in shards]

    def body(*refs):
        outs = refs[n:2 * n]
        send_sems, recv_sems = refs[2 * n:]
        x, y, c, _ = _place()
        cps = []
        for a in range(n):
            rows = outs[a].at[pl.ds(pl.multiple_of(c * halves[a], 8), halves[a])]
            cps.append(_remote(rows, rows, send_sems.at[a], recv_sems.at[a], (x, y, 1 - c)))
        for cp in cps:
            cp.start()
        for cp in cps:
            cp.wait()

    return pl.pallas_call(
        body, name=name, in_specs=[ANY] * n, out_specs=[ANY] * n,
        out_shape=[jax.ShapeDtypeStruct(g.shape, g.dtype) for g in shards],
        input_output_aliases={a: a for a in range(n)},
        scratch_shapes=[pltpu.SemaphoreType.DMA((n,)), pltpu.SemaphoreType.DMA((n,))],
    )(*shards)


def _adam(w, g, m, v):
    m = ADAM_B1 * m + (1.0 - ADAM_B1) * g
    v = ADAM_B2 * v + (1.0 - ADAM_B2) * (g * g)
    m_hat = m / (1.0 - ADAM_B1 ** ADAM_STEP)
    v_hat = v / (1.0 - ADAM_B2 ** ADAM_STEP)
    delta = -ADAM_LR * (m_hat / (jnp.sqrt(v_hat) + ADAM_EPS) + ADAM_WD * w)
    return delta, m, v


def small_allreduce_adam(g_part, w, m, v):
    n_dev = 8
    r, d = g_part.shape

    def body(g_ref, w_ref, m_ref, v_ref, gs_ref, dl_ref, nm_ref, nv_ref, all_ref, send_sems, recv_sems):
        x, y, c, _ = _place()
        me = 4 * x + 2 * y + c
        all_ref[me] = g_ref[...]
        cps = []
        for rel in range(1, n_dev):
            px = 1 - x if rel & 4 else x
            py = 1 - y if rel & 2 else y
            pc = 1 - c if rel & 1 else c
            cps.append(_remote(g_ref, all_ref.at[me], send_sems.at[rel - 1], recv_sems.at[rel - 1], (px, py, pc)))
        for cp in cps:
            cp.start()
        for cp in cps:
            cp.wait()
        total = all_ref[0]
        for dev in range(1, n_dev):
            total = total + all_ref[dev]
        gs_ref[...] = total
        delta, nm, nv = _adam(w_ref[...], total, m_ref[...], v_ref[...])
        dl_ref[...] = delta
        nm_ref[...] = nm
        nv_ref[...] = nv

    vm = pl.BlockSpec(memory_space=pltpu.VMEM)
    out = jax.ShapeDtypeStruct((r, d), F32)
    return pl.pallas_call(
        body, name="small_allreduce_adam", in_specs=[vm, vm, vm, vm], out_specs=[vm, vm, vm, vm],
        out_shape=[out, out, out, out],
        scratch_shapes=[pltpu.VMEM((n_dev, r, d), F32), pltpu.SemaphoreType.DMA((n_dev - 1,)),
                        pltpu.SemaphoreType.DMA((n_dev - 1,))],
    )(g_part, w, m, v)


def adam_update(name, w, m, v, g_buf, col_blk, after=None):
    w, m, v = w[0], m[0], v[0]
    r, c = w.shape
    tr = _tile(r, 128, 8)
    extra = [] if after is None else [after]

    def kern(w_ref, m_ref, v_ref, g_ref, *rest):
        go_ref, dl_ref, nm_ref, nv_ref = rest[len(extra):]
        g = g_ref[...]
        delta, nm, nv = _adam(w_ref[...], g, m_ref[...], v_ref[...])
        go_ref[...] = g
        dl_ref[...] = delta
        nm_ref[...] = nm
        nv_ref[...] = nv

    blk = pl.BlockSpec((tr, c), lambda i: (i, 0))
    out = jax.ShapeDtypeStruct((r, c), F32)
    res = pl.pallas_call(
        kern, name=name, grid=(r // tr,),
        in_specs=[blk, blk, blk, pl.BlockSpec((tr, c), lambda i: (i, col_blk))] + [ANY] * len(extra),
        out_specs=[blk] * 4, out_shape=[out] * 4, compiler_params=_params("parallel"),
    )(w, m, v, g_buf, *extra)
    return [a[None] for a in res]


def _w_in_segments(cw, n_qkv, n_heads, d):
    out = []

    def add(lo, hi, main):
        while lo < hi:
            k, a = divmod(lo, cw)
            w = min(cw - a, hi - lo)
            out.append((k, a, main, w))
            lo, main = lo + w, main + w

    add(0, n_qkv, 0)
    add(n_qkv + n_heads, N_CHIPS * cw, n_qkv)
    add(n_qkv, n_qkv + n_heads, n_qkv + 2 * d)
    return out


def regroup_w_in(g_in, segments, n_main):
    _, d, cw = g_in.shape
    tr = _tile(d, 128, 16)
    n_real = max(m + w for _, _, m, w in segments)

    def kern(s_ref, o_ref):
        for k, a, m, w in segments:
            o_ref[:, m:m + w] = s_ref[k, :, a:a + w]
        o_ref[:, n_real:] = jnp.zeros((tr, n_main - n_real), o_ref.dtype)

    return pl.pallas_call(
        kern, name="regroup_w_in", grid=(d // tr,),
        in_specs=[pl.BlockSpec((N_CHIPS, tr, cw), lambda i: (0, i, 0))],
        out_specs=pl.BlockSpec((tr, n_main), lambda i: (i, 0)),
        out_shape=jax.ShapeDtypeStruct((d, n_main), g_in.dtype), compiler_params=_params("parallel"),
    )(g_in)


def regroup_dw_in(dw_main, segments, cw):
    d, n_main = dw_main.shape
    tr = _tile(d, 128, 16)

    def kern(s_ref, o_ref):
        for k, a, m, w in segments:
            o_ref[k, :, a:a + w] = s_ref[:, m:m + w]

    return pl.pallas_call(
        kern, name="regroup_dw_in", grid=(d // tr,),
        in_specs=[pl.BlockSpec((tr, n_main), lambda i: (i, 0))],
        out_specs=pl.BlockSpec((N_CHIPS, tr, cw), lambda i: (0, i, 0)),
        out_shape=jax.ShapeDtypeStruct((N_CHIPS, d, cw), dw_main.dtype), compiler_params=_params("parallel"),
    )(dw_main)


def kernel(x, norm_mix_pre, norm_mix_post, w_in, b_forget, w_branch_sb, w_branch_fox, w_out, norm_ffn_pre, norm_ffn_post, w_ffn_gate, w_ffn_up, w_ffn_down, loss_target, m_norm_mix_pre, m_norm_mix_post, m_w_in, m_b_forget, m_w_branch_sb, m_w_branch_fox, m_w_out, m_norm_ffn_pre, m_norm_ffn_post, m_w_ffn_gate, m_w_ffn_up, m_w_ffn_down, v_norm_mix_pre, v_norm_mix_post, v_w_in, v_b_forget, v_w_branch_sb, v_w_branch_fox, v_w_out, v_norm_ffn_pre, v_norm_ffn_post, v_w_ffn_gate, v_w_ffn_up, v_w_ffn_down):
    s, d = x.shape[1], x.shape[2]
    n_heads = b_forget.shape[1]
    d_att = n_heads * HEAD_DIM
    c_in = w_in.shape[2]
    c_br = w_branch_sb.shape[2]
    c_gu = w_ffn_gate.shape[2]
    d_ff = c_gu * N_CHIPS
    d_in = c_in * N_CHIPS
    f_pad = 512
    n_qkv = 6 * d_att
    n_gf = 2 * d + f_pad
    core = lax.axis_index("c").astype(jnp.int32).reshape(1)
    chip = (2 * lax.axis_index("x") + lax.axis_index("y")).astype(jnp.int32).reshape(1)

    in_send, in_recv, in_bufs, in_token = gather_start("gather_start_w_in", [cast_place("place_w_in", [w_in], chip)])
    ag_send, ag_recv, ag_bufs, ag_token = gather_start("gather_start_rest", [
        cast_place("place_branch", [w_branch_sb, w_branch_fox], chip),
        cast_place("place_out", [w_out + in_token[0, 0]], chip),
        cast_place("place_gate_up", [w_ffn_gate, w_ffn_up], chip),
        cast_place("place_down", [w_ffn_down], chip)])
    g_in, = gather_forward("forward_w_in", gather_wait("gather_wait_w_in", in_bufs, in_send, in_recv,
                                                       [ag_token, m_w_in[0], v_w_in[0]]))
    segments = _w_in_segments(c_in, n_qkv, n_heads, d)
    w_main = regroup_w_in(g_in, segments, n_qkv + n_gf)
    x2 = x[0]
    tgt = loss_target[0]
    b_pad = jnp.pad(b_forget, ((0, 0), (0, LANES - n_heads)))

    u = norm_in(x2, norm_mix_pre)
    qkv = mm(u, w_main, "nn", BF16, "proj_qkv", b_win=(0, n_qkv))
    gf = mm(u, w_main, "nn", F32, "proj_gates", b_win=(n_qkv, n_gf))
    cum = cum_fwd(gf, b_pad, 2 * d)
    cum_heads = cum[:, :n_heads].T
    cum_col, cum_row = cum_heads[:, :, None], cum_heads[:, None, :]
    o_sb = sb_fwd(qkv, n_heads, 0)
    o_fx, lse = fox_fwd(qkv, cum_col, cum_row, n_heads, 3 * n_heads)
    rest = gather_wait("gather_wait_rest", ag_bufs, ag_send, ag_recv, [o_sb, o_fx])
    g_br, g_out = gather_forward("forward_small", rest[:2])
    fb_send, fb_recv, fb_bufs, fb_token = sibling_start("forward_big_start", rest[2:], _forward_plan, 6)
    w_o = g_out.reshape(d, d)
    bsb = mm(o_sb, g_br, "nn", F32, "branch_sb", tn=c_br, chunks=(1, 0), after=fb_token)
    bfx = mm(o_fx, g_br, "nn", F32, "branch_fox", tn=c_br, chunks=(1, 1), after=fb_token)
    merged = gate_fwd(bsb, bfx, gf)
    mix = mm(merged, w_o, "nn", F32, "out_proj")
    h1, u2 = mid_fwd(x2, mix, norm_mix_post, norm_ffn_pre)
    g_gu, g_dn = sibling_wait("forward_big_wait", fb_bufs, fb_send, fb_recv, _forward_plan, [u2])
    w_dn = g_dn.reshape(d_ff, d)
    gu = mm(u2, g_gu, "nn", F32, "ffn_gate_up", tn=c_gu, chunks=(2, 0))
    act = swiglu_fwd(gu, c_gu)
    ff = mm(act, w_dn, "nn", F32, "ffn_down")
    dy, d_ff_out, dg_fpost, loss_part = loss_head(h1, ff, norm_ffn_post, tgt)

    p_dn = mm(act, d_ff_out, "tn", BF16, "dw_ffn_down").reshape(N_CHIPS, d_ff // N_CHIPS, d)
    d_act = mm(d_ff_out, w_dn, "nt", F32, "d_act")
    d_gu = swiglu_bwd(d_act, gu, c_gu)
    p_gu = mm(u2, d_gu, "tn", BF16, "dw_ffn_gate_up", tn=c_gu, chunks=(2, 0),
              out_into=lax.empty((N_CHIPS, d, 2 * c_gu), BF16))
    sw_send, sw_recv, sw_arrs, sw_token = sibling_start(
        "swap_big_start", [p_gu, p_dn, lax.empty((N_CHIPS, d // 2, 2 * c_gu), BF16),
                           lax.empty((N_CHIPS, d_ff // N_CHIPS // 2, d), BF16)], _swap_plan, 2)
    du2 = mm(d_gu, g_gu, "nt", F32, "d_u2", tk=c_gu, chunks=(2, 0), after=sw_token)
    dh1, d_mix, dg_fpre, dg_post = mid_bwd(dy, du2, h1, mix, norm_ffn_pre, norm_mix_post)
    p_out = mm(merged, d_mix, "tn", BF16, "dw_out").reshape(N_CHIPS, d // N_CHIPS, d)
    d_merged = mm(d_mix, w_o, "nt", F32, "d_merged")
    d_bsb, d_bfx, d_gs, d_gx = gate_bwd(d_merged, bsb, bfx, gf)
    p_br = mm(o_sb, d_bsb, "tn", BF16, "dw_branch_sb", tn=c_br, chunks=(1, 0),
              out_into=lax.empty((N_CHIPS, d_att, 2 * c_br), BF16))
    p_br = mm(o_fx, d_bfx, "tn", BF16, "dw_branch_fox", tn=c_br, chunks=(1, 1), out_into=p_br)
    d_osb = mm(d_bsb, g_br, "nt", BF16, "d_o_sb", tk=c_br, chunks=(1, 0))
    d_ofx = mm(d_bfx, g_br, "nt", BF16, "d_o_fox", tk=c_br, chunks=(1, 1))

    def reduce_start(tag, pieces, names):
        from_sibling = swap_halves("swap_halves_" + tag, pieces)
        sums = [pair_sum("pair_sum_" + t, p, q, core) for t, p, q in zip(names, pieces, from_sibling)]
        return scatter_start("scatter_start_" + tag, sums)

    def reduce_end(tag, started, names, after):
        send, recv, sums, lands, _ = started
        sums, lands = scatter_wait("scatter_wait_" + tag, sums, lands, send, recv, after)
        return join_halves("join_halves_" + tag, [chip_sum("chip_sum_" + t, sm, got, chip, core)
                                                  for t, sm, got in zip(names, sums, lands)])

    rest_names = ["branch", "out", "gate_up", "down"]
    p_gu, p_dn, q_gu, q_dn = sibling_wait("swap_big_wait", sw_arrs, sw_send, sw_recv, _swap_plan, [p_br])
    q_br, q_out = swap_halves("swap_halves_small", [p_br, p_out])
    rest_started = scatter_start("scatter_start_rest", [
        pair_sum("pair_sum_" + t, p, q, core)
        for t, p, q in zip(rest_names, [p_br, p_out, p_gu, p_dn], [q_br, q_out, q_gu, q_dn])])
    d_osb = d_osb + rest_started[4][0, 0].astype(BF16)
    dq_s, dk_s, dv_s = sb_bwd(qkv, d_osb, n_heads, 0)
    dq_f, dk_f, dv_f, dcq, dck = fox_bwd(qkv, cum_col, cum_row, o_fx, d_ofx, lse, n_heads, 3 * n_heads)
    d_cum = jnp.pad((dcq[:, :, 0] + dck[:, 0, :]).T, ((0, 0), (0, LANES - n_heads)))
    d_f, db_pad = cum_bwd(d_cum, gf, b_pad, 2 * d, n_heads)
    d_main = jnp.concatenate(
        [dq_s, dk_s, dv_s, dq_f, dk_f, dv_f, d_gs, d_gx, d_f, jnp.zeros((s, f_pad - LANES), BF16)], axis=1)
    p_in = regroup_dw_in(mm(u, d_main, "tn", BF16, "dw_in"), segments, c_in)

    in_started = reduce_start("w_in", [p_in], ["in"])
    du = mm(d_main, w_main, "nt", F32, "d_u", after=in_started[4])
    dx, dg_pre = in_bwd(dh1, du, x2, norm_mix_pre + in_started[4][0:1, 0:1])
    gr_br, gr_out, gr_gu, gr_dn = reduce_end("rest", rest_started, rest_names, dx)

    upd_bs = adam_update("adam_branch_sb", w_branch_sb, m_w_branch_sb, v_w_branch_sb, gr_br, 0)
    upd_bf = adam_update("adam_branch_fox", w_branch_fox, m_w_branch_fox, v_w_branch_fox, gr_br, 1)
    upd_o = adam_update("adam_out", w_out, m_w_out, v_w_out, gr_out, 0)

    def pack(rows):
        rows = [jnp.pad(r_, ((0, 0), (0, d - r_.shape[1]))) for r_ in rows]
        return jnp.concatenate(rows + [jnp.zeros((8 - len(rows), d), F32)], axis=0)

    sm_g, sm_d, sm_m, sm_v = small_allreduce_adam(
        pack([dg_pre, dg_post, dg_fpre, dg_fpost, db_pad]),
        pack([norm_mix_pre, norm_mix_post, norm_ffn_pre, norm_ffn_post, b_forget]),
        pack([m_norm_mix_pre, m_norm_mix_post, m_norm_ffn_pre, m_norm_ffn_post, m_b_forget]),
        pack([v_norm_mix_pre, v_norm_mix_post, v_norm_ffn_pre, v_norm_ffn_post, v_b_forget]))

    done = sm_d[0:1, 0:1] + sum(u_[1][0, 0:1, 0:1] for u_ in (upd_bs, upd_bf, upd_o))
    gr_in, = reduce_end("w_in", in_started, ["in"], done)
    upd_in = adam_update("adam_w_in", w_in, m_w_in, v_w_in, gr_in, 0)
    upd_ga = adam_update("adam_gate", w_ffn_gate, m_w_ffn_gate, v_w_ffn_gate, gr_gu, 0, after=upd_in[1])
    upd_up = adam_update("adam_up", w_ffn_up, m_w_ffn_up, v_w_ffn_up, gr_gu, 1, after=upd_in[1])
    upd_dn = adam_update("adam_down", w_ffn_down, m_w_ffn_down, v_w_ffn_down, gr_dn, 0, after=upd_in[1])
    grads, deltas, new_ms, new_vs = zip(upd_in, upd_bs, upd_bf, upd_o, upd_ga, upd_up, upd_dn)

    def small(a):
        return [a[0:1], a[1:2], a[2:3], a[3:4], a[4:5, :n_heads]]

    def ordered(sm, bg):
        return [sm[0], sm[1], bg[0], sm[4], bg[1], bg[2], bg[3], sm[2], sm[3], bg[4], bg[5], bg[6]]

    loss = lax.psum(loss_part[0, 0], ("x", "y", "c"))
    return (loss, dx[None], *ordered(small(sm_g), grads), *ordered(small(sm_d), deltas),
            *ordered(small(sm_m), new_ms), *ordered(small(sm_v), new_vs))
```

```python
import functools

import jax
import jax.numpy as jnp
from jax import lax
from jax.experimental import pallas as pl
from jax.experimental.pallas import tpu as pltpu

F32 = jnp.float32
BF16 = jnp.bfloat16
MESH = pl.DeviceIdType.MESH

HEAD_DIM = 128
LANES = 128
ATT_TILE = 512
ROW_TILE = 256
N_CHIPS = 4
RMS_EPS = 1e-6
ADAM_LR = 0.001
ADAM_B1 = 0.9
ADAM_B2 = 0.999
ADAM_EPS = 1e-08
ADAM_WD = 0.01
ADAM_STEP = 10
NEG_BIG = -1e30
VMEM_LIMIT = 56 * 1024 * 1024
MM_VMEM_BUDGET = 40 * 1024 * 1024
ATT_STRIP = 512

NN = (((1,), (0,)), ((), ()))
NT = (((1,), (1,)), ((), ()))
TN = (((0,), (0,)), ((), ()))


def _tile(n, pref, align):
    best = None
    t = align
    while t <= min(n, pref):
        if n % t == 0:
            best = t
        t += align
    return n if best is None else best


def _params(*sem):
    return pltpu.CompilerParams(dimension_semantics=sem, vmem_limit_bytes=VMEM_LIMIT)


def _mm_tiles(m, n, k, a_bytes, b_bytes, out_bytes, tn, tk):
    tm = _tile(m, 2048, LANES)
    tk = tk or _tile(k, 512, LANES)

    def vmem(t):
        acc = 0 if out_bytes == 4 else tm * t * 4
        return acc + 2 * tm * t * out_bytes + 2 * (tm * tk * a_bytes + tk * t * b_bytes)

    if tn is None:
        fits = [t for t in range(LANES, min(n, 2048) + 1, LANES) if n % t == 0 and vmem(t) <= MM_VMEM_BUDGET]
        tn = max(fits) if fits else _tile(n, LANES, LANES)
    return tm, tn, tk


def mm(a, b, mode, out_dtype, name, *, tn=None, tk=None, b_win=None, chunks=None, out_into=None, after=None):
    n_per, blk0 = chunks if chunks else (1, 0)
    if mode == "nn":
        m, k = a.shape
        n = b.shape[0] * n_per * tn if chunks else (b_win[1] if b_win else b.shape[1])
    elif mode == "nt":
        m = a.shape[0]
        k = b.shape[0] * n_per * tk if chunks else a.shape[1]
        n = b.shape[-2]
    else:
        k, m = a.shape
        n = b.shape[1]
    in_place = jnp.dtype(out_dtype) == jnp.dtype(F32)
    tm, tn, tk = _mm_tiles(m, n, k, a.dtype.itemsize, b.dtype.itemsize, jnp.dtype(out_dtype).itemsize, tn, tk)
    assert m % tm == 0 and n % tn == 0 and k % tk == 0, (name, m, n, k, tm, tn, tk)
    j0 = 0
    if b_win:
        assert b_win[0] % tn == 0
        j0 = b_win[0] // tn
    nk = k // tk
    dims = {"nn": NN, "nt": NT, "tn": TN}[mode]

    def kern(a_ref, b_ref, *rest):
        o_ref, acc_ref = (rest[-1], rest[-1]) if in_place else (rest[-2], rest[-1])
        kk = pl.program_id(2)

        @pl.when(kk == 0)
        def _():
            acc_ref[...] = jnp.zeros_like(acc_ref)

        acc_ref[...] += lax.dot_general(a_ref[...].astype(BF16), b_ref[...].astype(BF16), dims,
                                        preferred_element_type=F32)

        if not in_place:
            @pl.when(kk == nk - 1)
            def _():
                o_ref[...] = acc_ref[...].astype(o_ref.dtype)

    out_spec = pl.BlockSpec((tm, tn), lambda i, j, kk: (i, j))
    out_shape = jax.ShapeDtypeStruct((m, n), out_dtype)
    if mode == "nn":
        a_spec = pl.BlockSpec((tm, tk), lambda i, j, kk: (i, kk))
        if chunks:
            b_spec = pl.BlockSpec((None, tk, tn), lambda i, j, kk: (j // n_per, kk, blk0 + j % n_per))
        else:
            b_spec = pl.BlockSpec((tk, tn), lambda i, j, kk: (kk, j + j0))
    elif mode == "nt":
        a_spec = pl.BlockSpec((tm, tk), lambda i, j, kk: (i, kk))
        if chunks:
            b_spec = pl.BlockSpec((None, tn, tk), lambda i, j, kk: (kk // n_per, j, blk0 + kk % n_per))
        else:
            b_spec = pl.BlockSpec((tn, tk), lambda i, j, kk: (j, kk))
    else:
        a_spec = pl.BlockSpec((tk, tm), lambda i, j, kk: (kk, i))
        b_spec = pl.BlockSpec((tk, tn), lambda i, j, kk: (kk, j))
        if chunks:
            out_spec = pl.BlockSpec((None, tm, tn), lambda i, j, kk: (j // n_per, i, blk0 + j % n_per))
    in_specs, operands, aliases = [a_spec, b_spec], [a, b], {}
    if chunks and mode == "tn":
        assert out_into is not None
        out_shape = jax.ShapeDtypeStruct(out_into.shape, out_dtype)
        in_specs.append(pl.BlockSpec(memory_space=pl.ANY))
        operands.append(out_into)
        aliases = {2: 0}
    if after is not None:
        in_specs.append(pl.BlockSpec(memory_space=pl.ANY))
        operands.append(after)
    return pl.pallas_call(
        kern, name=name, grid=(m // tm, n // tn, nk),
        in_specs=in_specs, out_specs=out_spec, out_shape=out_shape,
        scratch_shapes=[] if in_place else [pltpu.VMEM((tm, tn), F32)], input_output_aliases=aliases,
        compiler_params=_params("parallel", "parallel", "arbitrary"),
    )(*operands)


def _rstd(v):
    return lax.rsqrt(jnp.mean(v * v, axis=-1, keepdims=True) + RMS_EPS)


def _norm_bwd(v, g, dy):
    r = _rstd(v)
    vh = v * r
    dyg = dy * g
    dv = r * (dyg - vh * jnp.mean(dyg * vh, axis=-1, keepdims=True))
    return dv, jnp.sum(dy * vh, axis=0, keepdims=True)


def _row_call(kern, name, ins, outs, s, d):
    tr = _tile(s, ROW_TILE, 16)

    def spec(shape, is_row):
        if is_row:
            return pl.BlockSpec((tr, shape[1]), lambda i: (i, 0))
        return pl.BlockSpec(shape, lambda i: (0, 0))

    return pl.pallas_call(
        kern, name=name, grid=(s // tr,),
        in_specs=[spec(a.shape, r) for a, r in ins],
        out_specs=[spec(sh, r) for sh, _, r in outs],
        out_shape=[jax.ShapeDtypeStruct(sh, dt) for sh, dt, _ in outs],
        compiler_params=_params("arbitrary"),
    )(*[a for a, _ in ins])


def norm_in(x, g):
    s, d = x.shape

    def kern(x_ref, g_ref, u_ref):
        v = x_ref[...]
        u_ref[...] = (v * _rstd(v) * g_ref[...]).astype(BF16)

    return _row_call(kern, "norm_in", [(x, True), (g, False)], [((s, d), BF16, True)], s, d)[0]


def mid_fwd(x, mix, g_post, g_fpre):
    s, d = x.shape

    def kern(x_ref, mix_ref, gp_ref, gf_ref, h1_ref, u2_ref):
        mixv = mix_ref[...]
        h1 = x_ref[...] + mixv * _rstd(mixv) * gp_ref[...]
        h1_ref[...] = h1
        u2_ref[...] = (h1 * _rstd(h1) * gf_ref[...]).astype(BF16)

    return _row_call(kern, "mid_fwd", [(x, True), (mix, True), (g_post, False), (g_fpre, False)],
                     [((s, d), F32, True), ((s, d), BF16, True)], s, d)


def loss_head(h1, ff, g_fpost, target):
    s, d = h1.shape

    def kern(h1_ref, ff_ref, g_ref, t_ref, dy_ref, dff_ref, dg_ref, loss_ref):
        @pl.when(pl.program_id(0) == 0)
        def _():
            dg_ref[...] = jnp.zeros_like(dg_ref)
            loss_ref[...] = jnp.zeros_like(loss_ref)

        ffv = ff_ref[...]
        g = g_ref[...]
        y = h1_ref[...] + ffv * _rstd(ffv) * g
        diff = y - t_ref[...]
        row_loss = jnp.mean(diff * diff, axis=-1, keepdims=True)
        loss_ref[...] += 0.5 * jnp.sum(row_loss, axis=0, keepdims=True)
        dy = diff / d
        dy_ref[...] = dy
        dff, dg = _norm_bwd(ffv, g, dy)
        dff_ref[...] = dff.astype(BF16)
        dg_ref[...] += dg

    return _row_call(kern, "loss_head",
                     [(h1, True), (ff, True), (g_fpost, False), (target, True)],
                     [((s, d), F32, True), ((s, d), BF16, True), ((1, d), F32, False), ((1, 1), F32, False)], s, d)


def mid_bwd(dy, du2, h1, mix, g_fpre, g_post):
    s, d = dy.shape

    def kern(dy_ref, du2_ref, h1_ref, mix_ref, gf_ref, gp_ref, dh1_ref, dmix_ref, dgf_ref, dgp_ref):
        @pl.when(pl.program_id(0) == 0)
        def _():
            dgf_ref[...] = jnp.zeros_like(dgf_ref)
            dgp_ref[...] = jnp.zeros_like(dgp_ref)

        dh, dgf = _norm_bwd(h1_ref[...], gf_ref[...], du2_ref[...])
        dh1 = dy_ref[...] + dh
        dh1_ref[...] = dh1
        dmix, dgp = _norm_bwd(mix_ref[...], gp_ref[...], dh1)
        dmix_ref[...] = dmix.astype(BF16)
        dgf_ref[...] += dgf
        dgp_ref[...] += dgp

    return _row_call(kern, "mid_bwd",
                     [(dy, True), (du2, True), (h1, True), (mix, True), (g_fpre, False), (g_post, False)],
                     [((s, d), F32, True), ((s, d), BF16, True), ((1, d), F32, False), ((1, d), F32, False)], s, d)


def in_bwd(dh1, du, x, g_pre):
    s, d = x.shape

    def kern(dh1_ref, du_ref, x_ref, g_ref, dx_ref, dg_ref):
        @pl.when(pl.program_id(0) == 0)
        def _():
            dg_ref[...] = jnp.zeros_like(dg_ref)

        dxn, dg = _norm_bwd(x_ref[...], g_ref[...], du_ref[...])
        dx_ref[...] = dh1_ref[...] + dxn
        dg_ref[...] += dg

    return _row_call(kern, "in_bwd", [(dh1, True), (du, True), (x, True), (g_pre, False)],
                     [((s, d), F32, True), ((1, d), F32, False)], s, d)


def _sigmoid(v):
    return 1.0 / (1.0 + jnp.exp(-v))


def gate_fwd(bsb, bfx, gf):
    s, d = bsb.shape
    tr, tc = _tile(s, 256, 16), _tile(d, 512, LANES)
    nc = d // tc

    def kern(bsb_ref, bfx_ref, gs_ref, gx_ref, o_ref):
        o_ref[...] = (_sigmoid(gs_ref[...]) * bsb_ref[...] + _sigmoid(gx_ref[...]) * bfx_ref[...]).astype(BF16)

    blk = pl.BlockSpec((tr, tc), lambda i, j: (i, j))
    return pl.pallas_call(
        kern, name="gate_fwd", grid=(s // tr, nc),
        in_specs=[blk, blk, blk, pl.BlockSpec((tr, tc), lambda i, j: (i, j + nc))],
        out_specs=blk, out_shape=jax.ShapeDtypeStruct((s, d), BF16),
        compiler_params=_params("parallel", "parallel"),
    )(bsb, bfx, gf, gf)


def gate_bwd(dmerged, bsb, bfx, gf):
    s, d = bsb.shape
    tr, tc = _tile(s, 256, 16), _tile(d, 512, LANES)
    nc = d // tc

    def kern(dm_ref, bsb_ref, bfx_ref, gs_ref, gx_ref, dbs_ref, dbx_ref, dgs_ref, dgx_ref):
        dm = dm_ref[...]
        ss = _sigmoid(gs_ref[...])
        sx = _sigmoid(gx_ref[...])
        dbs_ref[...] = (dm * ss).astype(BF16)
        dbx_ref[...] = (dm * sx).astype(BF16)
        dgs_ref[...] = (dm * bsb_ref[...] * ss * (1.0 - ss)).astype(BF16)
        dgx_ref[...] = (dm * bfx_ref[...] * sx * (1.0 - sx)).astype(BF16)

    blk = pl.BlockSpec((tr, tc), lambda i, j: (i, j))
    out = jax.ShapeDtypeStruct((s, d), BF16)
    return pl.pallas_call(
        kern, name="gate_bwd", grid=(s // tr, nc),
        in_specs=[blk, blk, blk, blk, pl.BlockSpec((tr, tc), lambda i, j: (i, j + nc))],
        out_specs=[blk, blk, blk, blk], out_shape=[out, out, out, out],
        compiler_params=_params("parallel", "parallel"),
    )(dmerged, bsb, bfx, gf, gf)


FFN_ROWS = 1024


def ffn_up_fused(u2, w_gu, cw):
    s, d = u2.shape
    nc = w_gu.shape[0]
    tm, tk = _tile(s, FFN_ROWS, LANES), _tile(d, 512, LANES)
    nk = d // tk

    def kern(a_ref, b_ref, gu_ref, act_ref):
        kk = pl.program_id(2)

        @pl.when(kk == 0)
        def _():
            gu_ref[...] = jnp.zeros_like(gu_ref)

        gu_ref[...] += jnp.dot(a_ref[...], b_ref[...], preferred_element_type=F32)

        @pl.when(kk == nk - 1)
        def _():
            g = gu_ref[:, :cw]
            act_ref[...] = (g * _sigmoid(g) * gu_ref[:, cw:]).astype(BF16)

    return pl.pallas_call(
        kern, name="ffn_gate_up", grid=(s // tm, nc, nk),
        in_specs=[pl.BlockSpec((tm, tk), lambda i, j, kk: (i, kk)),
                  pl.BlockSpec((None, tk, 2 * cw), lambda i, j, kk: (j, kk, 0))],
        out_specs=[pl.BlockSpec((tm, 2 * cw), lambda i, j, kk: (i, j)), pl.BlockSpec((tm, cw), lambda i, j, kk: (i, j))],
        out_shape=[jax.ShapeDtypeStruct((s, nc * 2 * cw), F32), jax.ShapeDtypeStruct((s, nc * cw), BF16)],
        compiler_params=_params("parallel", "parallel", "arbitrary"),
    )(u2, w_gu)


def ffn_down_bwd_fused(d_ff, w_dn, gu, cw):
    s, d = d_ff.shape
    nc = gu.shape[1] // (2 * cw)
    tm, tk = _tile(s, FFN_ROWS, LANES), _tile(d, 512, LANES)
    nk = d // tk

    def kern(a_ref, b_ref, gu_ref, o_ref, acc_ref):
        kk = pl.program_id(2)

        @pl.when(kk == 0)
        def _():
            acc_ref[...] = jnp.zeros_like(acc_ref)

        acc_ref[...] += lax.dot_general(a_ref[...], b_ref[...], NT, preferred_element_type=F32)

        @pl.when(kk == nk - 1)
        def _():
            da = acc_ref[...]
            g = gu_ref[:, :cw]
            sg = _sigmoid(g)
            o_ref[:, :cw] = (da * gu_ref[:, cw:] * (sg * (1.0 + g * (1.0 - sg)))).astype(BF16)
            o_ref[:, cw:] = (da * (g * sg)).astype(BF16)

    return pl.pallas_call(
        kern, name="d_act_swiglu", grid=(s // tm, nc, nk),
        in_specs=[pl.BlockSpec((tm, tk), lambda i, j, kk: (i, kk)),
                  pl.BlockSpec((cw, tk), lambda i, j, kk: (j, kk)),
                  pl.BlockSpec((tm, 2 * cw), lambda i, j, kk: (i, j))],
        out_specs=pl.BlockSpec((tm, 2 * cw), lambda i, j, kk: (i, j)),
        out_shape=jax.ShapeDtypeStruct(gu.shape, BF16),
        scratch_shapes=[pltpu.VMEM((tm, cw), F32)],
        compiler_params=_params("parallel", "parallel", "arbitrary"),
    )(d_ff, w_dn, gu)


def _split3(v):
    hi = v.astype(BF16)
    r = v - hi.astype(F32)
    mid = r.astype(BF16)
    lo = (r - mid.astype(F32)).astype(BF16)
    return hi, mid, lo


def _dot3_right(v, ones):
    hi, mid, lo = _split3(v)
    d = lambda p: jnp.dot(p, ones, preferred_element_type=F32)
    return (d(lo) + d(mid)) + d(hi)


def _dot3_left(ones, v):
    hi, mid, lo = _split3(v)
    d = lambda p: jnp.dot(ones, p, preferred_element_type=F32)
    return (d(lo) + d(mid)) + d(hi)


def _split2(v):
    hi = v.astype(BF16)
    return hi, (v - hi.astype(F32)).astype(BF16)


def _dot2_right(v, ones):
    hi, lo = _split2(v)
    return jnp.dot(lo, ones, preferred_element_type=F32) + jnp.dot(hi, ones, preferred_element_type=F32)


def _log1p_exp_neg_abs(v):
    return jnp.log(1.0 + jnp.exp(-jnp.abs(v)))


def _mask01(cond):
    return jnp.where(cond, 1.0, 0.0).astype(BF16)


def _iota2(t):
    return (lax.broadcasted_iota(jnp.int32, (t, t), 0), lax.broadcasted_iota(jnp.int32, (t, t), 1))


def cum_fwd(gf, b_pad, f_col0):
    s = gf.shape[0]
    t = _tile(s, ATT_TILE, LANES)
    fb = f_col0 // LANES

    def kern(f_ref, b_ref, cum_ref, carry_ref):
        @pl.when(pl.program_id(0) == 0)
        def _():
            carry_ref[...] = jnp.zeros_like(carry_ref)

        v = f_ref[...] + b_ref[...]
        lf = jnp.minimum(v, 0.0) - _log1p_exp_neg_abs(v)
        row, col = _iota2(t)
        cum = _dot3_left(_mask01(col <= row), lf) + carry_ref[...]
        cum_ref[...] = cum
        carry_ref[...] = cum[t - 1:t, :]

    return pl.pallas_call(
        kern, name="cum_fwd", grid=(s // t,),
        in_specs=[pl.BlockSpec((t, LANES), lambda i: (i, fb)), pl.BlockSpec((1, LANES), lambda i: (0, 0))],
        out_specs=pl.BlockSpec((t, LANES), lambda i: (i, 0)),
        out_shape=jax.ShapeDtypeStruct((s, LANES), F32),
        scratch_shapes=[pltpu.VMEM((1, LANES), F32)],
        compiler_params=_params("arbitrary"),
    )(gf, b_pad)


def cum_bwd(dcum, gf, b_pad, f_col0, n_heads):
    s = gf.shape[0]
    t = _tile(s, ATT_TILE, LANES)
    nb = s // t
    fb = f_col0 // LANES

    def kern(dc_ref, f_ref, b_ref, df_ref, db_ref, carry_ref):
        @pl.when(pl.program_id(0) == 0)
        def _():
            carry_ref[...] = jnp.zeros_like(carry_ref)
            db_ref[...] = jnp.zeros_like(db_ref)

        row, col = _iota2(t)
        dlf = _dot3_left(_mask01(col >= row), dc_ref[...]) + carry_ref[...]
        carry_ref[...] = dlf[0:1, :]
        v = f_ref[...] + b_ref[...]
        sig_neg = jnp.exp(-jnp.maximum(v, 0.0) - _log1p_exp_neg_abs(v))
        lane = lax.broadcasted_iota(jnp.int32, (t, LANES), 1)
        df = jnp.where(lane < n_heads, dlf * sig_neg, 0.0)
        df_ref[...] = df.astype(BF16)
        db_ref[...] += jnp.sum(df, axis=0, keepdims=True)

    return pl.pallas_call(
        kern, name="cum_bwd", grid=(nb,),
        in_specs=[pl.BlockSpec((t, LANES), lambda i: (nb - 1 - i, 0)),
                  pl.BlockSpec((t, LANES), lambda i: (nb - 1 - i, fb)),
                  pl.BlockSpec((1, LANES), lambda i: (0, 0))],
        out_specs=[pl.BlockSpec((t, LANES), lambda i: (nb - 1 - i, 0)), pl.BlockSpec((1, LANES), lambda i: (0, 0))],
        out_shape=[jax.ShapeDtypeStruct((s, LANES), BF16), jax.ShapeDtypeStruct((1, LANES), F32)],
        scratch_shapes=[pltpu.VMEM((1, LANES), F32)],
        compiler_params=_params("arbitrary"),
    )(dcum, gf, b_pad)


def _qkv_specs(s, t, n_heads, base):
    return [pl.BlockSpec((t, HEAD_DIM), lambda h, i: (i, base + h)),
            pl.BlockSpec((s, HEAD_DIM), lambda h, i: (0, base + n_heads + h)),
            pl.BlockSpec((s, HEAD_DIM), lambda h, i: (0, base + 2 * n_heads + h))]


def _strips(t):
    sr = _tile(t, ATT_STRIP, 8)
    return sr, t // sr, [slice(si * sr, (si + 1) * sr) for si in range(t // sr)]


def _key_minus_row(sr, t):
    return lax.broadcasted_iota(jnp.int32, (sr, t), 1) - lax.broadcasted_iota(jnp.int32, (sr, t), 0)


def _keep(valid, v):
    return v if valid is None else jnp.where(valid, v, 0.0)


def _sb_scores(q, k, diff, lim):
    z = lax.dot_general(q, k, NT, preferred_element_type=F32) * (HEAD_DIM ** -0.5)
    valid = None if lim is None else diff < lim
    l1p = _log1p_exp_neg_abs(z)
    return z, valid, l1p, _keep(valid, -jnp.maximum(z, 0.0) - l1p)


def sb_fwd(qkv, n_heads, base):
    s = qkv.shape[0]
    t = _tile(s, ATT_TILE, LANES)
    sr, ns, strips = _strips(t)

    def kern(q_ref, k_ref, v_ref, o_ref):
        i = pl.program_id(1)
        row, col = _iota2(t)
        after = _mask01(row > col)
        diff = _key_minus_row(sr, t)
        qs = [q_ref[sl, :] for sl in strips]

        def tile(j, carry, diagonal):
            runs, accs = carry
            off = pl.multiple_of(j * t, t)
            k = k_ref[pl.ds(off, t), :]
            v = v_ref[pl.ds(off, t), :]
            new_runs, new_accs = [], []
            for si in range(ns):
                z, valid, l1p, log_keep = _sb_scores(qs[si], k, diff, si * sr if diagonal else None)
                between = _dot2_right(log_keep, after) + runs[si]
                w = _keep(valid, jnp.exp(jnp.minimum(z, 0.0) - l1p + between))
                new_accs.append(accs[si] + jnp.dot(w.astype(BF16), v, preferred_element_type=F32))
                new_runs.append(runs[si] + jnp.sum(log_keep, axis=1, keepdims=True))
            return tuple(new_runs), tuple(new_accs)

        init = (tuple(jnp.zeros((sr, 1), F32) for _ in strips), tuple(jnp.zeros((sr, HEAD_DIM), F32) for _ in strips))
        _, accs = lax.fori_loop(0, i, lambda jj, c: tile(i - 1 - jj, c, False), tile(i, init, True))
        for sl, acc in zip(strips, accs):
            o_ref[sl, :] = acc.astype(o_ref.dtype)

    return pl.pallas_call(
        kern, name="sb_fwd", grid=(n_heads, s // t),
        in_specs=_qkv_specs(s, t, n_heads, base),
        out_specs=pl.BlockSpec((t, HEAD_DIM), lambda h, i: (i, h)),
        out_shape=jax.ShapeDtypeStruct((s, n_heads * HEAD_DIM), BF16),
        compiler_params=_params("parallel", "arbitrary"),
    )(qkv, qkv, qkv)


def sb_bwd(qkv, d_o, n_heads, base):
    s = qkv.shape[0]
    t = _tile(s, ATT_TILE, LANES)
    nq = s // t
    sr, ns, strips = _strips(t)
    scale = HEAD_DIM ** -0.5

    def kern(q_ref, k_ref, v_ref, do_ref, dq_ref, dk_ref, dv_ref, dk_acc, dv_acc, run_ref):
        i = pl.program_id(1)

        @pl.when(i == 0)
        def _():
            dk_acc[...] = jnp.zeros_like(dk_acc)
            dv_acc[...] = jnp.zeros_like(dv_acc)

        row, col = _iota2(t)
        after = _mask01(row > col)
        before = _mask01(row < col)
        diff = _key_minus_row(sr, t)
        qs = [q_ref[sl, :] for sl in strips]
        dos = [do_ref[sl, :] for sl in strips]

        def sweep1(j, runs, diagonal):
            k = k_ref[pl.ds(pl.multiple_of(j * t, t), t), :]
            new_runs = []
            for si, sl in enumerate(strips):
                _, _, _, log_keep = _sb_scores(qs[si], k, diff, si * sr if diagonal else None)
                run_ref[j, sl, :] = runs[si]
                new_runs.append(runs[si] + jnp.sum(log_keep, axis=1, keepdims=True))
            return tuple(new_runs)

        lax.fori_loop(0, i, lambda jj, c: sweep1(i - 1 - jj, c, False),
                      sweep1(i, tuple(jnp.zeros((sr, 1), F32) for _ in strips), True))

        def sweep2(j, carry, diagonal):
            run_es, dqs = carry
            off = pl.multiple_of(j * t, t)
            k = k_ref[pl.ds(off, t), :]
            v = v_ref[pl.ds(off, t), :]
            new_es, new_dqs = [], []
            dk_t = jnp.zeros((t, HEAD_DIM), F32)
            dv_t = jnp.zeros((t, HEAD_DIM), F32)
            for si, sl in enumerate(strips):
                z, valid, l1p, log_keep = _sb_scores(qs[si], k, diff, si * sr if diagonal else None)
                between = _dot2_right(log_keep, after) + run_ref[j, sl, :]
                w = _keep(valid, jnp.exp(jnp.minimum(z, 0.0) - l1p + between))
                dw = lax.dot_general(dos[si], v, NT, preferred_element_type=F32)
                e = dw * w
                e_before = _dot2_right(e, before) + run_es[si]
                keep = jnp.exp(log_keep)
                dz = _keep(valid, e * keep - e_before * (1.0 - keep)) * scale
                dzb = dz.astype(BF16)
                new_dqs.append(dqs[si] + jnp.dot(dzb, k, preferred_element_type=F32))
                dk_t = dk_t + lax.dot_general(dzb, qs[si], TN, preferred_element_type=F32)
                dv_t = dv_t + lax.dot_general(w.astype(BF16), dos[si], TN, preferred_element_type=F32)
                new_es.append(run_es[si] + jnp.sum(e, axis=1, keepdims=True))
            dk_acc[pl.ds(off, t), :] += dk_t
            dv_acc[pl.ds(off, t), :] += dv_t
            return tuple(new_es), tuple(new_dqs)

        init = (tuple(jnp.zeros((sr, 1), F32) for _ in strips), tuple(jnp.zeros((sr, HEAD_DIM), F32) for _ in strips))
        _, dqs = sweep2(i, lax.fori_loop(0, i, lambda j, c: sweep2(j, c, False), init), True)
        for sl, dq in zip(strips, dqs):
            dq_ref[sl, :] = dq.astype(BF16)

        @pl.when(i == nq - 1)
        def _():
            dk_ref[...] = dk_acc[...].astype(BF16)
            dv_ref[...] = dv_acc[...].astype(BF16)

    out = jax.ShapeDtypeStruct((s, n_heads * HEAD_DIM), BF16)
    head_blk = pl.BlockSpec((s, HEAD_DIM), lambda h, i: (0, h))
    tile_blk = pl.BlockSpec((t, HEAD_DIM), lambda h, i: (i, h))
    return pl.pallas_call(
        kern, name="sb_bwd", grid=(n_heads, nq),
        in_specs=_qkv_specs(s, t, n_heads, base) + [tile_blk],
        out_specs=[tile_blk, head_blk, head_blk],
        out_shape=[out, out, out],
        scratch_shapes=[pltpu.VMEM((s, HEAD_DIM), F32), pltpu.VMEM((s, HEAD_DIM), F32), pltpu.VMEM((nq, t, 1), F32)],
        compiler_params=_params("parallel", "arbitrary"),
    )(qkv, qkv, qkv, d_o)


def _fox_scores(q, k, cq, ck, diff, lim):
    sc = lax.dot_general(q, k, NT, preferred_element_type=F32) * (HEAD_DIM ** -0.5)
    sc = sc + cq - ck
    if lim is None:
        return sc, None
    valid = diff < lim
    return jnp.where(valid, sc, NEG_BIG), valid


def fox_fwd(qkv, cum_col, cum_row, n_heads, base):
    s = qkv.shape[0]
    t = _tile(s, ATT_TILE, LANES)
    sr, ns, strips = _strips(t)

    def kern(q_ref, k_ref, v_ref, cq_ref, ck_ref, o_ref, lse_ref):
        i = pl.program_id(1)
        diff = _key_minus_row(sr, t)
        qs = [q_ref[sl, :] for sl in strips]
        cqs = [cq_ref[0, sl, :] for sl in strips]

        def tile(j, carry, diagonal):
            off = pl.multiple_of(j * t, t)
            k = k_ref[pl.ds(off, t), :]
            v = v_ref[pl.ds(off, t), :]
            ck = ck_ref[0, :, pl.ds(off, t)]
            out = []
            for si in range(ns):
                m, l, acc = carry[si]
                sc, _ = _fox_scores(qs[si], k, cqs[si], ck, diff, si * sr + 1 if diagonal else None)
                m_new = jnp.maximum(m, jnp.max(sc, axis=1, keepdims=True))
                p = jnp.exp(sc - m_new)
                alpha = jnp.exp(m - m_new)
                l = alpha * l + jnp.sum(p, axis=1, keepdims=True)
                acc = alpha * acc + jnp.dot(p.astype(BF16), v, preferred_element_type=F32)
                out.append((m_new, l, acc))
            return tuple(out)

        init = tuple((jnp.full((sr, 1), NEG_BIG, F32), jnp.zeros((sr, 1), F32), jnp.zeros((sr, HEAD_DIM), F32))
                     for _ in strips)
        res = tile(i, lax.fori_loop(0, i, lambda j, c: tile(j, c, False), init), True)
        for sl, (m, l, acc) in zip(strips, res):
            o_ref[sl, :] = acc / l
            lse_ref[0, sl, :] = m + jnp.log(l)

    col_blk = pl.BlockSpec((1, t, 1), lambda h, i: (h, i, 0))
    return pl.pallas_call(
        kern, name="fox_fwd", grid=(n_heads, s // t),
        in_specs=_qkv_specs(s, t, n_heads, base) + [col_blk, pl.BlockSpec((1, 1, s), lambda h, i: (h, 0, 0))],
        out_specs=[pl.BlockSpec((t, HEAD_DIM), lambda h, i: (i, h)), col_blk],
        out_shape=[jax.ShapeDtypeStruct((s, n_heads * HEAD_DIM), F32), jax.ShapeDtypeStruct((n_heads, s, 1), F32)],
        compiler_params=_params("parallel", "arbitrary"),
    )(qkv, qkv, qkv, cum_col, cum_row)


def fox_bwd(qkv, cum_col, cum_row, o, d_o, lse, n_heads, base):
    s = qkv.shape[0]
    t = _tile(s, ATT_TILE, LANES)
    nq = s // t
    sr, ns, strips = _strips(t)
    scale = HEAD_DIM ** -0.5

    def kern(q_ref, k_ref, v_ref, cq_ref, ck_ref, o_ref, do_ref, lse_ref,
             dq_ref, dk_ref, dv_ref, dcq_ref, dck_ref, dk_acc, dv_acc, dck_acc):
        i = pl.program_id(1)

        @pl.when(i == 0)
        def _():
            dk_acc[...] = jnp.zeros_like(dk_acc)
            dv_acc[...] = jnp.zeros_like(dv_acc)
            dck_acc[...] = jnp.zeros_like(dck_acc)

        diff = _key_minus_row(sr, t)
        qs = [q_ref[sl, :] for sl in strips]
        dos = [do_ref[sl, :] for sl in strips]
        cqs = [cq_ref[0, sl, :] for sl in strips]
        lses = [lse_ref[0, sl, :] for sl in strips]
        deltas = [jnp.sum(dos[si].astype(F32) * o_ref[sl, :], axis=1, keepdims=True) for si, sl in enumerate(strips)]

        def tile(j, carry, diagonal):
            off = pl.multiple_of(j * t, t)
            k = k_ref[pl.ds(off, t), :]
            v = v_ref[pl.ds(off, t), :]
            ck = ck_ref[0, :, pl.ds(off, t)]
            out = []
            dk_t = jnp.zeros((t, HEAD_DIM), F32)
            dv_t = jnp.zeros((t, HEAD_DIM), F32)
            dck_t = jnp.zeros((1, t), F32)
            for si in range(ns):
                dq, dcq = carry[si]
                sc, valid = _fox_scores(qs[si], k, cqs[si], ck, diff, si * sr + 1 if diagonal else None)
                p = _keep(valid, jnp.exp(sc - lses[si]))
                dp = lax.dot_general(dos[si], v, NT, preferred_element_type=F32)
                ds = p * (dp - deltas[si])
                dsb = (ds * scale).astype(BF16)
                dq = dq + jnp.dot(dsb, k, preferred_element_type=F32)
                dk_t = dk_t + lax.dot_general(dsb, qs[si], TN, preferred_element_type=F32)
                dv_t = dv_t + lax.dot_general(p.astype(BF16), dos[si], TN, preferred_element_type=F32)
                dck_t = dck_t + jnp.sum(ds, axis=0, keepdims=True)
                out.append((dq, dcq + jnp.sum(ds, axis=1, keepdims=True)))
            dk_acc[pl.ds(off, t), :] += dk_t
            dv_acc[pl.ds(off, t), :] += dv_t
            dck_acc[:, pl.ds(off, t)] -= dck_t
            return tuple(out)

        init = tuple((jnp.zeros((sr, HEAD_DIM), F32), jnp.zeros((sr, 1), F32)) for _ in strips)
        res = tile(i, lax.fori_loop(0, i, lambda j, c: tile(j, c, False), init), True)
        for sl, (dq, dcq) in zip(strips, res):
            dq_ref[sl, :] = dq.astype(BF16)
            dcq_ref[0, sl, :] = dcq

        @pl.when(i == nq - 1)
        def _():
            dk_ref[...] = dk_acc[...].astype(BF16)
            dv_ref[...] = dv_acc[...].astype(BF16)
            dck_ref[0] = dck_acc[...]

    out = jax.ShapeDtypeStruct((s, n_heads * HEAD_DIM), BF16)
    head_blk = pl.BlockSpec((s, HEAD_DIM), lambda h, i: (0, h))
    tile_blk = pl.BlockSpec((t, HEAD_DIM), lambda h, i: (i, h))
    col_blk = pl.BlockSpec((1, t, 1), lambda h, i: (h, i, 0))
    row_blk = pl.BlockSpec((1, 1, s), lambda h, i: (h, 0, 0))
    return pl.pallas_call(
        kern, name="fox_bwd", grid=(n_heads, nq),
        in_specs=_qkv_specs(s, t, n_heads, base) + [col_blk, row_blk, tile_blk, tile_blk, col_blk],
        out_specs=[tile_blk, head_blk, head_blk, col_blk, row_blk],
        out_shape=[out, out, out, jax.ShapeDtypeStruct((n_heads, s, 1), F32),
                   jax.ShapeDtypeStruct((n_heads, 1, s), F32)],
        scratch_shapes=[pltpu.VMEM((s, HEAD_DIM), F32), pltpu.VMEM((s, HEAD_DIM), F32), pltpu.VMEM((1, s), F32)],
        compiler_params=_params("parallel", "arbitrary"),
    )(qkv, qkv, qkv, cum_col, cum_row, o, d_o, lse)


def _place():
    x, y, c = lax.axis_index("x"), lax.axis_index("y"), lax.axis_index("c")
    other_chips = [(1 - x, y), (x, 1 - y), (1 - x, 1 - y)]
    return x, y, c, other_chips


ANY = pl.BlockSpec(memory_space=pl.ANY)


def _remote(src, dst, send_sem, recv_sem, dev):
    return pltpu.make_async_remote_copy(src_ref=src, dst_ref=dst, send_sem=send_sem, recv_sem=recv_sem,
                                        device_id=dev, device_id_type=MESH)


def cast_place(name, ws, chip):
    r = ws[0].shape[1]
    cs = [w.shape[2] for w in ws]
    tr = _tile(r, 256, 16)

    def kern(chip_ref, *refs):
        o_ref = refs[-1]
        off = 0
        for w_ref, c in zip(refs[:-1], cs):
            o_ref[:, off:off + c] = w_ref[...].astype(BF16)
            off += c

    return pl.pallas_call(
        kern, name=name,
        grid_spec=pltpu.PrefetchScalarGridSpec(
            num_scalar_prefetch=1, grid=(r // tr,),
            in_specs=[pl.BlockSpec((None, tr, c), lambda i, chip_ref: (0, i, 0)) for c in cs],
            out_specs=pl.BlockSpec((None, tr, sum(cs)), lambda i, chip_ref: (chip_ref[0], i, 0))),
        out_shape=jax.ShapeDtypeStruct((N_CHIPS, r, sum(cs)), BF16),
        compiler_params=_params("parallel"),
    )(chip, *ws)


HBM = pl.BlockSpec(memory_space=pltpu.HBM)
SEM = pl.BlockSpec(memory_space=pltpu.SEMAPHORE)
SPLIT = pltpu.CompilerParams(has_side_effects=pltpu.SideEffectType.DATAFLOW_SIDE_EFFECTING)


def _in_hbm(a):
    return pltpu.with_memory_space_constraint(a, pltpu.HBM)


def _slab_rows(ref, k, core):
    half = ref.shape[1] // 2
    return ref.at[k, pl.ds(pl.multiple_of(core * half, 16), half)]


def gather_start(name, bufs):
    n = len(bufs)

    def body(*refs):
        ins, send, recv, token = refs[:n], refs[n], refs[n + 1], refs[-1]
        x, y, c, chips = _place()
        me = 2 * x + y
        for a in range(n):
            for j in range(3):
                rows = _slab_rows(ins[a], me, c)
                _remote(rows, rows, send.at[3 * a + j], recv.at[3 * a + j], (chips[j][0], chips[j][1], c)).start()
        token[...] = jnp.zeros_like(token)

    sem = pltpu.SemaphoreType.DMA((3 * n,))
    res = pl.pallas_call(
        body, name=name, in_specs=[HBM] * n, out_specs=[SEM, SEM] + [HBM] * n + [pl.BlockSpec(memory_space=pltpu.VMEM)],
        out_shape=[sem, sem] + [pltpu.HBM(b.shape, b.dtype) for b in bufs] + [jax.ShapeDtypeStruct((8, LANES), F32)],
        input_output_aliases={a: 2 + a for a in range(n)}, compiler_params=SPLIT,
    )(*[_in_hbm(b) for b in bufs])
    return res[0], res[1], res[2:2 + n], res[-1]


def gather_wait(name, bufs, send_sems, recv_sems, after):
    n = len(bufs)

    def body(*refs):
        ins, send, recv = refs[:n], refs[n], refs[n + 1]
        x, y, c, chips = _place()
        me = 2 * x + y
        for a in range(n):
            for j in range(3):
                dev = (chips[j][0], chips[j][1], c)
                mine = _slab_rows(ins[a], me, c)
                _remote(mine, mine, send.at[3 * a + j], recv.at[3 * a + j], dev).wait_send()
                land = _slab_rows(ins[a], 2 * chips[j][0] + chips[j][1], c)
                _remote(land, land, send.at[3 * a + j], recv.at[3 * a + j], dev).wait_recv()

    return pl.pallas_call(
        body, name=name, in_specs=[HBM] * n + [SEM, SEM] + [ANY] * len(after), out_specs=[HBM] * n,
        out_shape=[pltpu.HBM(b.shape, b.dtype) for b in bufs],
        input_output_aliases={a: a for a in range(n)}, compiler_params=SPLIT,
    )(*bufs, send_sems, recv_sems, *after)


def gather_forward(name, bufs):
    n = len(bufs)

    def body(*refs):
        outs = refs[n:2 * n]
        send_sems, recv_sems = refs[2 * n:]
        x, y, c, chips = _place()
        sibling = (x, y, 1 - c)

        def d2d(a, j, core):
            rows = _slab_rows(outs[a], 2 * chips[j][0] + chips[j][1], core)
            return _remote(rows, rows, send_sems.at[3 * a + j], recv_sems.at[3 * a + j], sibling)

        pairs = [(a, j) for a in range(n) for j in range(3)]
        for a, j in pairs:
            d2d(a, j, c).start()
        for a, j in pairs:
            d2d(a, j, 1 - c).wait_recv()
        for a, j in pairs:
            d2d(a, j, c).wait_send()

    return pl.pallas_call(
        body, name=name, in_specs=[ANY] * n, out_specs=[ANY] * n,
        out_shape=[jax.ShapeDtypeStruct(b.shape, b.dtype) for b in bufs],
        input_output_aliases={a: a for a in range(n)},
        scratch_shapes=[pltpu.SemaphoreType.DMA((3 * n,)), pltpu.SemaphoreType.DMA((3 * n,))],
    )(*bufs)


def _forward_plan(refs):
    x, y, c, chips = _place()
    out = []
    for ref in refs:
        for j in range(3):
            k = 2 * chips[j][0] + chips[j][1]
            out.append((_slab_rows(ref, k, c), _slab_rows(ref, k, c), _slab_rows(ref, k, 1 - c)))
    return out


def _swap_plan(refs):
    x, y, c, _ = _place()
    n = len(refs) // 2
    out = []
    for a in range(n):
        half = refs[a].shape[1] // 2
        src = refs[a].at[:, pl.ds(pl.multiple_of((1 - c) * half, 16), half), :]
        out.append((src, refs[n + a], refs[n + a]))
    return out


def sibling_start(name, arrays, plan, n_copies, after=()):
    n = len(arrays)

    n_in = n + len(after)

    def body(*refs):
        send, recv, token = refs[n_in], refs[n_in + 1], refs[-1]
        x, y, c, _ = _place()
        for idx, (src, dst, _) in enumerate(plan(refs[:n])):
            _remote(src, dst, send.at[idx], recv.at[idx], (x, y, 1 - c)).start()
        token[...] = jnp.zeros_like(token)

    sem = pltpu.SemaphoreType.DMA((n_copies,))
    res = pl.pallas_call(
        body, name=name, in_specs=[HBM] * n + [ANY] * len(after),
        out_specs=[SEM, SEM] + [HBM] * n + [pl.BlockSpec(memory_space=pltpu.VMEM)],
        out_shape=[sem, sem] + [pltpu.HBM(b.shape, b.dtype) for b in arrays] + [jax.ShapeDtypeStruct((8, LANES), F32)],
        input_output_aliases={a: 2 + a for a in range(n)}, compiler_params=SPLIT,
    )(*[_in_hbm(b) for b in arrays], *after)
    return res[0], res[1], res[2:2 + n], res[-1]


def sibling_wait(name, arrays, send_sems, recv_sems, plan, after):
    n = len(arrays)

    def body(*refs):
        send, recv = refs[n], refs[n + 1]
        x, y, c, _ = _place()
        for idx, (src, dst, filled) in enumerate(plan(refs[:n])):
            _remote(src, dst, send.at[idx], recv.at[idx], (x, y, 1 - c)).wait_send()
            _remote(filled, filled, send.at[idx], recv.at[idx], (x, y, 1 - c)).wait_recv()

    return pl.pallas_call(
        body, name=name, in_specs=[HBM] * n + [SEM, SEM] + [ANY] * len(after), out_specs=[HBM] * n,
        out_shape=[pltpu.HBM(b.shape, b.dtype) for b in arrays],
        input_output_aliases={a: a for a in range(n)}, compiler_params=SPLIT,
    )(*arrays, send_sems, recv_sems, *after)


def swap_halves(name, pieces):
    n = len(pieces)
    halves = [p.shape[1] // 2 for p in pieces]

    def body(*refs):
        ins, outs = refs[:n], refs[n:2 * n]
        send_sems, recv_sems = refs[2 * n:]
        x, y, c, _ = _place()
        cps = [_remote(ins[a].at[:, pl.ds(pl.multiple_of((1 - c) * halves[a], 16), halves[a]), :], outs[a],
                       send_sems.at[a], recv_sems.at[a], (x, y, 1 - c)) for a in range(n)]
        for cp in cps:
            cp.start()
        for cp in cps:
            cp.wait()

    return pl.pallas_call(
        body, name=name, in_specs=[ANY] * n, out_specs=[ANY] * n,
        out_shape=[jax.ShapeDtypeStruct((N_CHIPS, h, p.shape[2]), p.dtype) for p, h in zip(pieces, halves)],
        scratch_shapes=[pltpu.SemaphoreType.DMA((n,)), pltpu.SemaphoreType.DMA((n,))],
    )(*pieces)


def pair_sum(name, pieces, got, core):
    _, r, w = pieces.shape
    half = r // 2
    tr = _tile(half, 256, 16)

    def kern(core_ref, p_ref, g_ref, o_ref):
        o_ref[...] = (p_ref[...].astype(F32) + g_ref[...].astype(F32)).astype(o_ref.dtype)

    return pl.pallas_call(
        kern, name=name,
        grid_spec=pltpu.PrefetchScalarGridSpec(
            num_scalar_prefetch=1, grid=(N_CHIPS, half // tr),
            in_specs=[pl.BlockSpec((None, None, tr, w), lambda k, i, core_ref: (k, core_ref[0], i, 0)),
                      pl.BlockSpec((None, tr, w), lambda k, i, core_ref: (k, i, 0))],
            out_specs=pl.BlockSpec((None, tr, w), lambda k, i, core_ref: (k, i, 0))),
        out_shape=jax.ShapeDtypeStruct((N_CHIPS, half, w), pieces.dtype),
        compiler_params=_params("parallel", "parallel"),
    )(core, pieces.reshape(N_CHIPS, 2, half, w), got)


def _scatter_copies(sums, lands, send, recv):
    x, y, c, chips = _place()
    return [_remote(sums[a].at[2 * chips[j][0] + chips[j][1]], lands[a].at[j], send.at[3 * a + j], recv.at[3 * a + j],
                    (chips[j][0], chips[j][1], c)) for a in range(len(sums)) for j in range(3)]


def scatter_start(name, sums):
    n = len(sums)
    lands = [lax.empty((3,) + t.shape[1:], t.dtype) for t in sums]

    def body(*refs):
        ins, land_in, send, recv, token = refs[:n], refs[n:2 * n], refs[2 * n], refs[2 * n + 1], refs[-1]
        for cp in _scatter_copies(ins, land_in, send, recv):
            cp.start()
        token[...] = jnp.zeros_like(token)

    sem = pltpu.SemaphoreType.DMA((3 * n,))
    res = pl.pallas_call(
        body, name=name, in_specs=[HBM] * (2 * n),
        out_specs=[SEM, SEM] + [HBM] * (2 * n) + [pl.BlockSpec(memory_space=pltpu.VMEM)],
        out_shape=[sem, sem] + [pltpu.HBM(t.shape, t.dtype) for t in sums + lands] + [jax.ShapeDtypeStruct((8, LANES), F32)],
        input_output_aliases={a: 2 + a for a in range(2 * n)}, compiler_params=SPLIT,
    )(*[_in_hbm(t) for t in sums + lands])
    return res[0], res[1], res[2:2 + n], res[2 + n:2 + 2 * n], res[-1]


def scatter_wait(name, sums, lands, send_sems, recv_sems, after):
    n = len(sums)

    def body(*refs):
        ins, land_in, send, recv = refs[:n], refs[n:2 * n], refs[2 * n], refs[2 * n + 1]
        for cp in _scatter_copies(ins, land_in, send, recv):
            cp.wait_send()
            cp.wait_recv()

    res = pl.pallas_call(
        body, name=name, in_specs=[HBM] * (2 * n) + [SEM, SEM, ANY], out_specs=[HBM] * (2 * n),
        out_shape=[pltpu.HBM(t.shape, t.dtype) for t in sums + lands],
        input_output_aliases={a: a for a in range(2 * n)}, compiler_params=SPLIT,
    )(*sums, *lands, send_sems, recv_sems, after)
    return res[:n], res[n:]


def chip_sum(name, sums, got, chip, core):
    _, half, w = sums.shape
    tr = _tile(half, 256, 16)
    nb = half // tr

    def kern(ids_ref, s_ref, g0_ref, g1_ref, g2_ref, o_ref):
        o_ref[...] = ((s_ref[...].astype(F32) + g0_ref[...].astype(F32)) + g1_ref[...].astype(F32)) \
            + g2_ref[...].astype(F32)

    def got_spec(j):
        return pl.BlockSpec((None, tr, w), lambda i, ids_ref: (j, i, 0))

    return pl.pallas_call(
        kern, name=name,
        grid_spec=pltpu.PrefetchScalarGridSpec(
            num_scalar_prefetch=1, grid=(nb,),
            in_specs=[pl.BlockSpec((None, tr, w), lambda i, ids_ref: (ids_ref[0], i, 0)),
                      got_spec(0), got_spec(1), got_spec(2)],
            out_specs=pl.BlockSpec((tr, w), lambda i, ids_ref: (ids_ref[1] * nb + i, 0))),
        out_shape=jax.ShapeDtypeStruct((2 * half, w), F32),
        compiler_params=_params("parallel"),
    )(jnp.concatenate([chip, core]), sums, got, got, got)


def join_halves(name, shards):
    n = len(shards)
    halves = [g.shape[0] // 2 for g in shards]

    def body(*refs):
        outs = refs[n:2 * n]
        send_sems, recv_sems = refs[2 * n:]
        x, y, c, _ = _place()
        cps = []
        for a in range(n):
            rows = outs[a].at[pl.ds(pl.multiple_of(c * halves[a], 8), halves[a])]
            cps.append(_remote(rows, rows, send_sems.at[a], recv_sems.at[a], (x, y, 1 - c)))
        for cp in cps:
            cp.start()
        for cp in cps:
            cp.wait()

    return pl.pallas_call(
        body, name=name, in_specs=[ANY] * n, out_specs=[ANY] * n,
        out_shape=[jax.ShapeDtypeStruct(g.shape, g.dtype) for g in shards],
        input_output_aliases={a: a for a in range(n)},
        scratch_shapes=[pltpu.SemaphoreType.DMA((n,)), pltpu.SemaphoreType.DMA((n,))],
    )(*shards)


def _adam(w, g, m, v):
    m = ADAM_B1 * m + (1.0 - ADAM_B1) * g
    v = ADAM_B2 * v + (1.0 - ADAM_B2) * (g * g)
    m_hat = m / (1.0 - ADAM_B1 ** ADAM_STEP)
    v_hat = v / (1.0 - ADAM_B2 ** ADAM_STEP)
    delta = -ADAM_LR * (m_hat / (jnp.sqrt(v_hat) + ADAM_EPS) + ADAM_WD * w)
    return delta, m, v


def small_allreduce_adam(g_part, w, m, v):
    n_dev = 8
    r, d = g_part.shape

    def body(g_ref, w_ref, m_ref, v_ref, gs_ref, dl_ref, nm_ref, nv_ref, all_ref, send_sems, recv_sems):
        x, y, c, _ = _place()
        me = 4 * x + 2 * y + c
        all_ref[me] = g_ref[...]
        cps = []
        for rel in range(1, n_dev):
            px = 1 - x if rel & 4 else x
            py = 1 - y if rel & 2 else y
            pc = 1 - c if rel & 1 else c
            cps.append(_remote(g_ref, all_ref.at[me], send_sems.at[rel - 1], recv_sems.at[rel - 1], (px, py, pc)))
        for cp in cps:
            cp.start()
        for cp in cps:
            cp.wait()
        total = all_ref[0]
        for dev in range(1, n_dev):
            total = total + all_ref[dev]
        gs_ref[...] = total
        delta, nm, nv = _adam(w_ref[...], total, m_ref[...], v_ref[...])
        dl_ref[...] = delta
        nm_ref[...] = nm
        nv_ref[...] = nv

    vm = pl.BlockSpec(memory_space=pltpu.VMEM)
    out = jax.ShapeDtypeStruct((r, d), F32)
    return pl.pallas_call(
        body, name="small_allreduce_adam", in_specs=[vm, vm, vm, vm], out_specs=[vm, vm, vm, vm],
        out_shape=[out, out, out, out],
        scratch_shapes=[pltpu.VMEM((n_dev, r, d), F32), pltpu.SemaphoreType.DMA((n_dev - 1,)),
                        pltpu.SemaphoreType.DMA((n_dev - 1,))],
    )(g_part, w, m, v)


def adam_update(name, w, m, v, g_buf, col_blk, after=None):
    w, m, v = w[0], m[0], v[0]
    r, c = w.shape
    tr = _tile(r, 128, 8)
    extra = [] if after is None else [after]

    def kern(w_ref, m_ref, v_ref, g_ref, *rest):
        go_ref, dl_ref, nm_ref, nv_ref = rest[len(extra):]
        g = g_ref[...]
        delta, nm, nv = _adam(w_ref[...], g, m_ref[...], v_ref[...])
        go_ref[...] = g
        dl_ref[...] = delta
        nm_ref[...] = nm
        nv_ref[...] = nv

    blk = pl.BlockSpec((tr, c), lambda i: (i, 0))
    out = jax.ShapeDtypeStruct((r, c), F32)
    res = pl.pallas_call(
        kern, name=name, grid=(r // tr,),
        in_specs=[blk, blk, blk, pl.BlockSpec((tr, c), lambda i: (i, col_blk))] + [ANY] * len(extra),
        out_specs=[blk] * 4, out_shape=[out] * 4, compiler_params=_params("parallel"),
    )(w, m, v, g_buf, *extra)
    return [a[None] for a in res]


def _w_in_segments(cw, n_qkv, n_heads, d):
    out = []

    def add(lo, hi, main):
        while lo < hi:
            k, a = divmod(lo, cw)
            w = min(cw - a, hi - lo)
            out.append((k, a, main, w))
            lo, main = lo + w, main + w

    add(0, n_qkv, 0)
    add(n_qkv + n_heads, N_CHIPS * cw, n_qkv)
    add(n_qkv, n_qkv + n_heads, n_qkv + 2 * d)
    return out


def regroup_w_in(g_in, segments, n_main):
    _, d, cw = g_in.shape
    tr = _tile(d, 128, 16)
    n_real = max(m + w for _, _, m, w in segments)

    def kern(s_ref, o_ref):
        for k, a, m, w in segments:
            o_ref[:, m:m + w] = s_ref[k, :, a:a + w]
        o_ref[:, n_real:] = jnp.zeros((tr, n_main - n_real), o_ref.dtype)

    return pl.pallas_call(
        kern, name="regroup_w_in", grid=(d // tr,),
        in_specs=[pl.BlockSpec((N_CHIPS, tr, cw), lambda i: (0, i, 0))],
        out_specs=pl.BlockSpec((tr, n_main), lambda i: (i, 0)),
        out_shape=jax.ShapeDtypeStruct((d, n_main), g_in.dtype), compiler_params=_params("parallel"),
    )(g_in)


def regroup_dw_in(dw_main, segments, cw):
    d, n_main = dw_main.shape
    tr = _tile(d, 128, 16)

    def kern(s_ref, o_ref):
        for k, a, m, w in segments:
            o_ref[k, :, a:a + w] = s_ref[:, m:m + w]

    return pl.pallas_call(
        kern, name="regroup_dw_in", grid=(d // tr,),
        in_specs=[pl.BlockSpec((tr, n_main), lambda i: (i, 0))],
        out_specs=pl.BlockSpec((N_CHIPS, tr, cw), lambda i: (0, i, 0)),
        out_shape=jax.ShapeDtypeStruct((N_CHIPS, d, cw), dw_main.dtype), compiler_params=_params("parallel"),
    )(dw_main)


def kernel(x, norm_mix_pre, norm_mix_post, w_in, b_forget, w_branch_sb, w_branch_fox, w_out, norm_ffn_pre, norm_ffn_post, w_ffn_gate, w_ffn_up, w_ffn_down, loss_target, m_norm_mix_pre, m_norm_mix_post, m_w_in, m_b_forget, m_w_branch_sb, m_w_branch_fox, m_w_out, m_norm_ffn_pre, m_norm_ffn_post, m_w_ffn_gate, m_w_ffn_up, m_w_ffn_down, v_norm_mix_pre, v_norm_mix_post, v_w_in, v_b_forget, v_w_branch_sb, v_w_branch_fox, v_w_out, v_norm_ffn_pre, v_norm_ffn_post, v_w_ffn_gate, v_w_ffn_up, v_w_ffn_down):
    s, d = x.shape[1], x.shape[2]
    n_heads = b_forget.shape[1]
    d_att = n_heads * HEAD_DIM
    c_in = w_in.shape[2]
    c_br = w_branch_sb.shape[2]
    c_gu = w_ffn_gate.shape[2]
    d_ff = c_gu * N_CHIPS
    d_in = c_in * N_CHIPS
    f_pad = 512
    n_qkv = 6 * d_att
    n_gf = 2 * d + f_pad
    core = lax.axis_index("c").astype(jnp.int32).reshape(1)
    chip = (2 * lax.axis_index("x") + lax.axis_index("y")).astype(jnp.int32).reshape(1)

    in_send, in_recv, in_bufs, in_token = gather_start("gather_start_w_in", [cast_place("place_w_in", [w_in], chip)])
    ag_send, ag_recv, ag_bufs, ag_token = gather_start("gather_start_rest", [
        cast_place("place_branch", [w_branch_sb, w_branch_fox], chip),
        cast_place("place_out", [w_out + in_token[0, 0]], chip),
        cast_place("place_gate_up", [w_ffn_gate, w_ffn_up], chip),
        cast_place("place_down", [w_ffn_down], chip)])
    g_in, = gather_forward("forward_w_in", gather_wait("gather_wait_w_in", in_bufs, in_send, in_recv,
                                                       [ag_token, m_w_in[0], v_w_in[0]]))
    segments = _w_in_segments(c_in, n_qkv, n_heads, d)
    w_main = regroup_w_in(g_in, segments, n_qkv + n_gf)
    x2 = x[0]
    tgt = loss_target[0]
    b_pad = jnp.pad(b_forget, ((0, 0), (0, LANES - n_heads)))

    u = norm_in(x2, norm_mix_pre)
    qkv = mm(u, w_main, "nn", BF16, "proj_qkv", b_win=(0, n_qkv))
    gf = mm(u, w_main, "nn", F32, "proj_gates", b_win=(n_qkv, n_gf))
    cum = cum_fwd(gf, b_pad, 2 * d)
    cum_heads = cum[:, :n_heads].T
    cum_col, cum_row = cum_heads[:, :, None], cum_heads[:, None, :]
    o_sb = sb_fwd(qkv, n_heads, 0)
    o_fx, lse = fox_fwd(qkv, cum_col, cum_row, n_heads, 3 * n_heads)
    rest = gather_wait("gather_wait_rest", ag_bufs, ag_send, ag_recv, [o_sb, o_fx])
    g_br, g_out = gather_forward("forward_small", rest[:2])
    fb_send, fb_recv, fb_bufs, fb_token = sibling_start("forward_big_start", rest[2:], _forward_plan, 6, after=[g_br])
    w_o = g_out.reshape(d, d)
    bsb = mm(o_sb, g_br, "nn", F32, "branch_sb", tn=c_br, chunks=(1, 0), after=fb_token)
    bfx = mm(o_fx, g_br, "nn", F32, "branch_fox", tn=c_br, chunks=(1, 1), after=fb_token)
    merged = gate_fwd(bsb, bfx, gf)
    mix = mm(merged, w_o, "nn", F32, "out_proj")
    h1, u2 = mid_fwd(x2, mix, norm_mix_post, norm_ffn_pre)
    g_gu, g_dn = sibling_wait("forward_big_wait", fb_bufs, fb_send, fb_recv, _forward_plan, [u2])
    w_dn = g_dn.reshape(d_ff, d)
    gu, act = ffn_up_fused(u2, g_gu, c_gu)
    ff = mm(act, w_dn, "nn", F32, "ffn_down")
    dy, d_ff_out, dg_fpost, loss_part = loss_head(h1, ff, norm_ffn_post, tgt)

    p_dn = mm(act, d_ff_out, "tn", BF16, "dw_ffn_down").reshape(N_CHIPS, d_ff // N_CHIPS, d)
    d_gu = ffn_down_bwd_fused(d_ff_out, w_dn, gu, c_gu)
    p_gu = mm(u2, d_gu, "tn", BF16, "dw_ffn_gate_up", tn=c_gu, chunks=(2, 0),
              out_into=lax.empty((N_CHIPS, d, 2 * c_gu), BF16))
    sw_send, sw_recv, sw_arrs, sw_token = sibling_start(
        "swap_big_start", [p_gu, p_dn, lax.empty((N_CHIPS, d // 2, 2 * c_gu), BF16),
                           lax.empty((N_CHIPS, d_ff // N_CHIPS // 2, d), BF16)], _swap_plan, 2)
    du2 = mm(d_gu, g_gu, "nt", F32, "d_u2", tk=c_gu, chunks=(2, 0), after=sw_token)
    dh1, d_mix, dg_fpre, dg_post = mid_bwd(dy, du2, h1, mix, norm_ffn_pre, norm_mix_post)
    p_out = mm(merged, d_mix, "tn", BF16, "dw_out").reshape(N_CHIPS, d // N_CHIPS, d)
    d_merged = mm(d_mix, w_o, "nt", F32, "d_merged")
    d_bsb, d_bfx, d_gs, d_gx = gate_bwd(d_merged, bsb, bfx, gf)
    p_br = mm(o_sb, d_bsb, "tn", BF16, "dw_branch_sb", tn=c_br, chunks=(1, 0),
              out_into=lax.empty((N_CHIPS, d_att, 2 * c_br), BF16))
    p_br = mm(o_fx, d_bfx, "tn", BF16, "dw_branch_fox", tn=c_br, chunks=(1, 1), out_into=p_br)
    d_osb = mm(d_bsb, g_br, "nt", BF16, "d_o_sb", tk=c_br, chunks=(1, 0))
    d_ofx = mm(d_bfx, g_br, "nt", BF16, "d_o_fox", tk=c_br, chunks=(1, 1))

    def reduce_start(tag, pieces, names):
        from_sibling = swap_halves("swap_halves_" + tag, pieces)
        sums = [pair_sum("pair_sum_" + t, p, q, core) for t, p, q in zip(names, pieces, from_sibling)]
        return scatter_start("scatter_start_" + tag, sums)

    def reduce_end(tag, started, names, after):
        send, recv, sums, lands, _ = started
        sums, lands = scatter_wait("scatter_wait_" + tag, sums, lands, send, recv, after)
        return join_halves("join_halves_" + tag, [chip_sum("chip_sum_" + t, sm, got, chip, core)
                                                  for t, sm, got in zip(names, sums, lands)])

    rest_names = ["branch", "out", "gate_up", "down"]
    p_gu, p_dn, q_gu, q_dn = sibling_wait("swap_big_wait", sw_arrs, sw_send, sw_recv, _swap_plan, [p_br])
    q_br, q_out = swap_halves("swap_halves_small", [p_br, p_out])
    rest_started = scatter_start("scatter_start_rest", [
        pair_sum("pair_sum_" + t, p, q, core)
        for t, p, q in zip(rest_names, [p_br, p_out, p_gu, p_dn], [q_br, q_out, q_gu, q_dn])])
    d_osb = d_osb + rest_started[4][0, 0].astype(BF16)
    dq_s, dk_s, dv_s = sb_bwd(qkv, d_osb, n_heads, 0)
    dq_f, dk_f, dv_f, dcq, dck = fox_bwd(qkv, cum_col, cum_row, o_fx, d_ofx, lse, n_heads, 3 * n_heads)
    d_cum = jnp.pad((dcq[:, :, 0] + dck[:, 0, :]).T, ((0, 0), (0, LANES - n_heads)))
    d_f, db_pad = cum_bwd(d_cum, gf, b_pad, 2 * d, n_heads)
    d_main = jnp.concatenate(
        [dq_s, dk_s, dv_s, dq_f, dk_f, dv_f, d_gs, d_gx, d_f, jnp.zeros((s, f_pad - LANES), BF16)], axis=1)
    p_in = regroup_dw_in(mm(u, d_main, "tn", BF16, "dw_in"), segments, c_in)

    in_started = reduce_start("w_in", [p_in], ["in"])
    du = mm(d_main, w_main, "nt", F32, "d_u", after=in_started[4])
    dx, dg_pre = in_bwd(dh1, du, x2, norm_mix_pre + in_started[4][0:1, 0:1])
    gr_br, gr_out, gr_gu, gr_dn = reduce_end("rest", rest_started, rest_names, dx)

    upd_bs = adam_update("adam_branch_sb", w_branch_sb, m_w_branch_sb, v_w_branch_sb, gr_br, 0)
    upd_bf = adam_update("adam_branch_fox", w_branch_fox, m_w_branch_fox, v_w_branch_fox, gr_br, 1)
    upd_o = adam_update("adam_out", w_out, m_w_out, v_w_out, gr_out, 0)

    def pack(rows):
        rows = [jnp.pad(r_, ((0, 0), (0, d - r_.shape[1]))) for r_ in rows]
        return jnp.concatenate(rows + [jnp.zeros((8 - len(rows), d), F32)], axis=0)

    sm_g, sm_d, sm_m, sm_v = small_allreduce_adam(
        pack([dg_pre, dg_post, dg_fpre, dg_fpost, db_pad]),
        pack([norm_mix_pre, norm_mix_post, norm_ffn_pre, norm_ffn_post, b_forget]),
        pack([m_norm_mix_pre, m_norm_mix_post, m_norm_ffn_pre, m_norm_ffn_post, m_b_forget]),
        pack([v_norm_mix_pre, v_norm_mix_post, v_norm_ffn_pre, v_norm_ffn_post, v_b_forget]))

    done = sm_d[0:1, 0:1] + sum(u_[1][0, 0:1, 0:1] for u_ in (upd_bs, upd_bf, upd_o))
    gr_in, = reduce_end("w_in", in_started, ["in"], done)
    upd_in = adam_update("adam_w_in", w_in, m_w_in, v_w_in, gr_in, 0)
    upd_ga = adam_update("adam_gate", w_ffn_gate, m_w_ffn_gate, v_w_ffn_gate, gr_gu, 0, after=upd_in[1])
    upd_up = adam_update("adam_up", w_ffn_up, m_w_ffn_up, v_w_ffn_up, gr_gu, 1, after=upd_in[1])
    upd_dn = adam_update("adam_down", w_ffn_down, m_w_ffn_down, v_w_ffn_down, gr_dn, 0, after=upd_in[1])
    grads, deltas, new_ms, new_vs = zip(upd_in, upd_bs, upd_bf, upd_o, upd_ga, upd_up, upd_dn)

    def small(a):
        return [a[0:1], a[1:2], a[2:3], a[3:4], a[4:5, :n_heads]]

    def ordered(sm, bg):
        return [sm[0], sm[1], bg[0], sm[4], bg[1], bg[2], bg[3], sm[2], sm[3], bg[4], bg[5], bg[6]]

    loss = lax.psum(loss_part[0, 0], ("x", "y", "c"))
    return (loss, dx[None], *ordered(small(sm_g), grads), *ordered(small(sm_d), deltas),
            *ordered(small(sm_m), new_ms), *ordered(small(sm_v), new_vs))
```

```python
import functools

import jax
import jax.numpy as jnp
from jax import lax
from jax.experimental import pallas as pl
from jax.experimental.pallas import tpu as pltpu

F32 = jnp.float32
BF16 = jnp.bfloat16
MESH = pl.DeviceIdType.MESH

HEAD_DIM = 128
LANES = 128
ATT_TILE = 512
ROW_TILE = 256
N_CHIPS = 4
RMS_EPS = 1e-6
ADAM_LR = 0.001
ADAM_B1 = 0.9
ADAM_B2 = 0.999
ADAM_EPS = 1e-08
ADAM_WD = 0.01
ADAM_STEP = 10
NEG_BIG = -1e30
VMEM_LIMIT = 56 * 1024 * 1024
MM_VMEM_BUDGET = 40 * 1024 * 1024
ATT_STRIP = 512

NN = (((1,), (0,)), ((), ()))
NT = (((1,), (1,)), ((), ()))
TN = (((0,), (0,)), ((), ()))


def _tile(n, pref, align):
    best = None
    t = align
    while t <= min(n, pref):
        if n % t == 0:
            best = t
        t += align
    return n if best is None else best


def _params(*sem):
    return pltpu.CompilerParams(dimension_semantics=sem, vmem_limit_bytes=VMEM_LIMIT)


def _mm_tiles(m, n, k, a_bytes, b_bytes, out_bytes, tn, tk):
    tm = _tile(m, 2048, LANES)
    tk = tk or _tile(k, 512, LANES)

    def vmem(t):
        acc = 0 if out_bytes == 4 else tm * t * 4
        return acc + 2 * tm * t * out_bytes + 2 * (tm * tk * a_bytes + tk * t * b_bytes)

    if tn is None:
        fits = [t for t in range(LANES, min(n, 2048) + 1, LANES) if n % t == 0 and vmem(t) <= MM_VMEM_BUDGET]
        tn = max(fits) if fits else _tile(n, LANES, LANES)
    return tm, tn, tk


def mm(a, b, mode, out_dtype, name, *, tn=None, tk=None, b_win=None, chunks=None, out_into=None, after=None):
    n_per, blk0 = chunks if chunks else (1, 0)
    if mode == "nn":
        m, k = a.shape
        n = b.shape[0] * n_per * tn if chunks else (b_win[1] if b_win else b.shape[1])
    elif mode == "nt":
        m = a.shape[0]
        k = b.shape[0] * n_per * tk if chunks else a.shape[1]
        n = b.shape[-2]
    else:
        k, m = a.shape
        n = b.shape[1]
    in_place = jnp.dtype(out_dtype) == jnp.dtype(F32)
    tm, tn, tk = _mm_tiles(m, n, k, a.dtype.itemsize, b.dtype.itemsize, jnp.dtype(out_dtype).itemsize, tn, tk)
    assert m % tm == 0 and n % tn == 0 and k % tk == 0, (name, m, n, k, tm, tn, tk)
    j0 = 0
    if b_win:
        assert b_win[0] % tn == 0
        j0 = b_win[0] // tn
    nk = k // tk
    dims = {"nn": NN, "nt": NT, "tn": TN}[mode]

    def kern(a_ref, b_ref, *rest):
        o_ref, acc_ref = (rest[-1], rest[-1]) if in_place else (rest[-2], rest[-1])
        kk = pl.program_id(2)

        @pl.when(kk == 0)
        def _():
            acc_ref[...] = jnp.zeros_like(acc_ref)

        acc_ref[...] += lax.dot_general(a_ref[...].astype(BF16), b_ref[...].astype(BF16), dims,
                                        preferred_element_type=F32)

        if not in_place:
            @pl.when(kk == nk - 1)
            def _():
                o_ref[...] = acc_ref[...].astype(o_ref.dtype)

    out_spec = pl.BlockSpec((tm, tn), lambda i, j, kk: (i, j))
    out_shape = jax.ShapeDtypeStruct((m, n), out_dtype)
    if mode == "nn":
        a_spec = pl.BlockSpec((tm, tk), lambda i, j, kk: (i, kk))
        if chunks:
            b_spec = pl.BlockSpec((None, tk, tn), lambda i, j, kk: (j // n_per, kk, blk0 + j % n_per))
        else:
            b_spec = pl.BlockSpec((tk, tn), lambda i, j, kk: (kk, j + j0))
    elif mode == "nt":
        a_spec = pl.BlockSpec((tm, tk), lambda i, j, kk: (i, kk))
        if chunks:
            b_spec = pl.BlockSpec((None, tn, tk), lambda i, j, kk: (kk // n_per, j, blk0 + kk % n_per))
        else:
            b_spec = pl.BlockSpec((tn, tk), lambda i, j, kk: (j, kk))
    else:
        a_spec = pl.BlockSpec((tk, tm), lambda i, j, kk: (kk, i))
        b_spec = pl.BlockSpec((tk, tn), lambda i, j, kk: (kk, j))
        if chunks:
            out_spec = pl.BlockSpec((None, tm, tn), lambda i, j, kk: (j // n_per, i, blk0 + j % n_per))
    in_specs, operands, aliases = [a_spec, b_spec], [a, b], {}
    if chunks and mode == "tn":
        assert out_into is not None
        out_shape = jax.ShapeDtypeStruct(out_into.shape, out_dtype)
        in_specs.append(pl.BlockSpec(memory_space=pl.ANY))
        operands.append(out_into)
        aliases = {2: 0}
    if after is not None:
        in_specs.append(pl.BlockSpec(memory_space=pl.ANY))
        operands.append(after)
    return pl.pallas_call(
        kern, name=name, grid=(m // tm, n // tn, nk),
        in_specs=in_specs, out_specs=out_spec, out_shape=out_shape,
        scratch_shapes=[] if in_place else [pltpu.VMEM((tm, tn), F32)], input_output_aliases=aliases,
        compiler_params=_params("parallel", "parallel", "arbitrary"),
    )(*operands)


def _rstd(v):
    return lax.rsqrt(jnp.mean(v * v, axis=-1, keepdims=True) + RMS_EPS)


def _norm_bwd(v, g, dy):
    r = _rstd(v)
    vh = v * r
    dyg = dy * g
    dv = r * (dyg - vh * jnp.mean(dyg * vh, axis=-1, keepdims=True))
    return dv, jnp.sum(dy * vh, axis=0, keepdims=True)


def _row_call(kern, name, ins, outs, s, d):
    tr = _tile(s, ROW_TILE, 16)

    def spec(shape, is_row):
        if is_row:
            return pl.BlockSpec((tr, shape[1]), lambda i: (i, 0))
        return pl.BlockSpec(shape, lambda i: (0, 0))

    return pl.pallas_call(
        kern, name=name, grid=(s // tr,),
        in_specs=[spec(a.shape, r) for a, r in ins],
        out_specs=[spec(sh, r) for sh, _, r in outs],
        out_shape=[jax.ShapeDtypeStruct(sh, dt) for sh, dt, _ in outs],
        compiler_params=_params("arbitrary"),
    )(*[a for a, _ in ins])


def norm_in(x, g):
    s, d = x.shape

    def kern(x_ref, g_ref, u_ref):
        v = x_ref[...]
        u_ref[...] = (v * _rstd(v) * g_ref[...]).astype(BF16)

    return _row_call(kern, "norm_in", [(x, True), (g, False)], [((s, d), BF16, True)], s, d)[0]


def mid_fwd(x, mix, g_post, g_fpre):
    s, d = x.shape

    def kern(x_ref, mix_ref, gp_ref, gf_ref, h1_ref, u2_ref):
        mixv = mix_ref[...]
        h1 = x_ref[...] + mixv * _rstd(mixv) * gp_ref[...]
        h1_ref[...] = h1
        u2_ref[...] = (h1 * _rstd(h1) * gf_ref[...]).astype(BF16)

    return _row_call(kern, "mid_fwd", [(x, True), (mix, True), (g_post, False), (g_fpre, False)],
                     [((s, d), F32, True), ((s, d), BF16, True)], s, d)


def loss_head(h1, ff, g_fpost, target):
    s, d = h1.shape

    def kern(h1_ref, ff_ref, g_ref, t_ref, dy_ref, dff_ref, dg_ref, loss_ref):
        @pl.when(pl.program_id(0) == 0)
        def _():
            dg_ref[...] = jnp.zeros_like(dg_ref)
            loss_ref[...] = jnp.zeros_like(loss_ref)

        ffv = ff_ref[...]
        g = g_ref[...]
        y = h1_ref[...] + ffv * _rstd(ffv) * g
        diff = y - t_ref[...]
        row_loss = jnp.mean(diff * diff, axis=-1, keepdims=True)
        loss_ref[...] += 0.5 * jnp.sum(row_loss, axis=0, keepdims=True)
        dy = diff / d
        dy_ref[...] = dy
        dff, dg = _norm_bwd(ffv, g, dy)
        dff_ref[...] = dff.astype(BF16)
        dg_ref[...] += dg

    return _row_call(kern, "loss_head",
                     [(h1, True), (ff, True), (g_fpost, False), (target, True)],
                     [((s, d), F32, True), ((s, d), BF16, True), ((1, d), F32, False), ((1, 1), F32, False)], s, d)


def mid_bwd(dy, du2, h1, mix, g_fpre, g_post):
    s, d = dy.shape

    def kern(dy_ref, du2_ref, h1_ref, mix_ref, gf_ref, gp_ref, dh1_ref, dmix_ref, dgf_ref, dgp_ref):
        @pl.when(pl.program_id(0) == 0)
        def _():
            dgf_ref[...] = jnp.zeros_like(dgf_ref)
            dgp_ref[...] = jnp.zeros_like(dgp_ref)

        dh, dgf = _norm_bwd(h1_ref[...], gf_ref[...], du2_ref[...])
        dh1 = dy_ref[...] + dh
        dh1_ref[...] = dh1
        dmix, dgp = _norm_bwd(mix_ref[...], gp_ref[...], dh1)
        dmix_ref[...] = dmix.astype(BF16)
        dgf_ref[...] += dgf
        dgp_ref[...] += dgp

    return _row_call(kern, "mid_bwd",
                     [(dy, True), (du2, True), (h1, True), (mix, True), (g_fpre, False), (g_post, False)],
                     [((s, d), F32, True), ((s, d), BF16, True), ((1, d), F32, False), ((1, d), F32, False)], s, d)


def in_bwd(dh1, du, x, g_pre):
    s, d = x.shape

    def kern(dh1_ref, du_ref, x_ref, g_ref, dx_ref, dg_ref):
        @pl.when(pl.program_id(0) == 0)
        def _():
            dg_ref[...] = jnp.zeros_like(dg_ref)

        dxn, dg = _norm_bwd(x_ref[...], g_ref[...], du_ref[...])
        dx_ref[...] = dh1_ref[...] + dxn
        dg_ref[...] += dg

    return _row_call(kern, "in_bwd", [(dh1, True), (du, True), (x, True), (g_pre, False)],
                     [((s, d), F32, True), ((1, d), F32, False)], s, d)


def _sigmoid(v):
    return 1.0 / (1.0 + jnp.exp(-v))


def gate_fwd(bsb, bfx, gf):
    s, d = bsb.shape
    tr, tc = _tile(s, 256, 16), _tile(d, 512, LANES)
    nc = d // tc

    def kern(bsb_ref, bfx_ref, gs_ref, gx_ref, o_ref):
        o_ref[...] = (_sigmoid(gs_ref[...]) * bsb_ref[...] + _sigmoid(gx_ref[...]) * bfx_ref[...]).astype(BF16)

    blk = pl.BlockSpec((tr, tc), lambda i, j: (i, j))
    return pl.pallas_call(
        kern, name="gate_fwd", grid=(s // tr, nc),
        in_specs=[blk, blk, blk, pl.BlockSpec((tr, tc), lambda i, j: (i, j + nc))],
        out_specs=blk, out_shape=jax.ShapeDtypeStruct((s, d), BF16),
        compiler_params=_params("parallel", "parallel"),
    )(bsb, bfx, gf, gf)


def gate_bwd(dmerged, bsb, bfx, gf):
    s, d = bsb.shape
    tr, tc = _tile(s, 256, 16), _tile(d, 512, LANES)
    nc = d // tc

    def kern(dm_ref, bsb_ref, bfx_ref, gs_ref, gx_ref, dbs_ref, dbx_ref, dgs_ref, dgx_ref):
        dm = dm_ref[...]
        ss = _sigmoid(gs_ref[...])
        sx = _sigmoid(gx_ref[...])
        dbs_ref[...] = (dm * ss).astype(BF16)
        dbx_ref[...] = (dm * sx).astype(BF16)
        dgs_ref[...] = (dm * bsb_ref[...] * ss * (1.0 - ss)).astype(BF16)
        dgx_ref[...] = (dm * bfx_ref[...] * sx * (1.0 - sx)).astype(BF16)

    blk = pl.BlockSpec((tr, tc), lambda i, j: (i, j))
    out = jax.ShapeDtypeStruct((s, d), BF16)
    return pl.pallas_call(
        kern, name="gate_bwd", grid=(s // tr, nc),
        in_specs=[blk, blk, blk, blk, pl.BlockSpec((tr, tc), lambda i, j: (i, j + nc))],
        out_specs=[blk, blk, blk, blk], out_shape=[out, out, out, out],
        compiler_params=_params("parallel", "parallel"),
    )(dmerged, bsb, bfx, gf, gf)


FFN_ROWS = 1024


def ffn_up_fused(u2, w_gu, cw):
    s, d = u2.shape
    nc = w_gu.shape[0]
    tm, tk = _tile(s, FFN_ROWS, LANES), _tile(d, 512, LANES)
    nk = d // tk

    def kern(a_ref, b_ref, gu_ref, act_ref):
        kk = pl.program_id(2)

        @pl.when(kk == 0)
        def _():
            gu_ref[...] = jnp.zeros_like(gu_ref)

        gu_ref[...] += jnp.dot(a_ref[...], b_ref[...], preferred_element_type=F32)

        @pl.when(kk == nk - 1)
        def _():
            g = gu_ref[:, :cw]
            act_ref[...] = (g * _sigmoid(g) * gu_ref[:, cw:]).astype(BF16)

    return pl.pallas_call(
        kern, name="ffn_gate_up", grid=(s // tm, nc, nk),
        in_specs=[pl.BlockSpec((tm, tk), lambda i, j, kk: (i, kk)),
                  pl.BlockSpec((None, tk, 2 * cw), lambda i, j, kk: (j, kk, 0))],
        out_specs=[pl.BlockSpec((tm, 2 * cw), lambda i, j, kk: (i, j)), pl.BlockSpec((tm, cw), lambda i, j, kk: (i, j))],
        out_shape=[jax.ShapeDtypeStruct((s, nc * 2 * cw), F32), jax.ShapeDtypeStruct((s, nc * cw), BF16)],
        compiler_params=_params("parallel", "parallel", "arbitrary"),
    )(u2, w_gu)


def ffn_down_bwd_fused(d_ff, w_dn, gu, cw):
    s, d = d_ff.shape
    nc = gu.shape[1] // (2 * cw)
    tm, tk = _tile(s, FFN_ROWS, LANES), _tile(d, 512, LANES)
    nk = d // tk

    def kern(a_ref, b_ref, gu_ref, o_ref, acc_ref):
        kk = pl.program_id(2)

        @pl.when(kk == 0)
        def _():
            acc_ref[...] = jnp.zeros_like(acc_ref)

        acc_ref[...] += lax.dot_general(a_ref[...], b_ref[...], NT, preferred_element_type=F32)

        @pl.when(kk == nk - 1)
        def _():
            da = acc_ref[...]
            g = gu_ref[:, :cw]
            sg = _sigmoid(g)
            o_ref[:, :cw] = (da * gu_ref[:, cw:] * (sg * (1.0 + g * (1.0 - sg)))).astype(BF16)
            o_ref[:, cw:] = (da * (g * sg)).astype(BF16)

    return pl.pallas_call(
        kern, name="d_act_swiglu", grid=(s // tm, nc, nk),
        in_specs=[pl.BlockSpec((tm, tk), lambda i, j, kk: (i, kk)),
                  pl.BlockSpec((cw, tk), lambda i, j, kk: (j, kk)),
                  pl.BlockSpec((tm, 2 * cw), lambda i, j, kk: (i, j))],
        out_specs=pl.BlockSpec((tm, 2 * cw), lambda i, j, kk: (i, j)),
        out_shape=jax.ShapeDtypeStruct(gu.shape, BF16),
        scratch_shapes=[pltpu.VMEM((tm, cw), F32)],
        compiler_params=_params("parallel", "parallel", "arbitrary"),
    )(d_ff, w_dn, gu)


def _split3(v):
    hi = v.astype(BF16)
    r = v - hi.astype(F32)
    mid = r.astype(BF16)
    lo = (r - mid.astype(F32)).astype(BF16)
    return hi, mid, lo


def _dot3_right(v, ones):
    hi, mid, lo = _split3(v)
    d = lambda p: jnp.dot(p, ones, preferred_element_type=F32)
    return (d(lo) + d(mid)) + d(hi)


def _dot3_left(ones, v):
    hi, mid, lo = _split3(v)
    d = lambda p: jnp.dot(ones, p, preferred_element_type=F32)
    return (d(lo) + d(mid)) + d(hi)


def _split2(v):
    hi = v.astype(BF16)
    return hi, (v - hi.astype(F32)).astype(BF16)


def _dot2_right(v, ones):
    hi, lo = _split2(v)
    return jnp.dot(lo, ones, preferred_element_type=F32) + jnp.dot(hi, ones, preferred_element_type=F32)


def _log1p_exp_neg_abs(v):
    return jnp.log(1.0 + jnp.exp(-jnp.abs(v)))


def _mask01(cond):
    return jnp.where(cond, 1.0, 0.0).astype(BF16)


def _iota2(t):
    return (lax.broadcasted_iota(jnp.int32, (t, t), 0), lax.broadcasted_iota(jnp.int32, (t, t), 1))


def cum_fwd(gf, b_pad, f_col0):
    s = gf.shape[0]
    t = _tile(s, ATT_TILE, LANES)
    fb = f_col0 // LANES

    def kern(f_ref, b_ref, cum_ref, carry_ref):
        @pl.when(pl.program_id(0) == 0)
        def _():
            carry_ref[...] = jnp.zeros_like(carry_ref)

        v = f_ref[...] + b_ref[...]
        lf = jnp.minimum(v, 0.0) - _log1p_exp_neg_abs(v)
        row, col = _iota2(t)
        cum = _dot3_left(_mask01(col <= row), lf) + carry_ref[...]
        cum_ref[...] = cum
        carry_ref[...] = cum[t - 1:t, :]

    return pl.pallas_call(
        kern, name="cum_fwd", grid=(s // t,),
        in_specs=[pl.BlockSpec((t, LANES), lambda i: (i, fb)), pl.BlockSpec((1, LANES), lambda i: (0, 0))],
        out_specs=pl.BlockSpec((t, LANES), lambda i: (i, 0)),
        out_shape=jax.ShapeDtypeStruct((s, LANES), F32),
        scratch_shapes=[pltpu.VMEM((1, LANES), F32)],
        compiler_params=_params("arbitrary"),
    )(gf, b_pad)


def cum_bwd(dcum, gf, b_pad, f_col0, n_heads):
    s = gf.shape[0]
    t = _tile(s, ATT_TILE, LANES)
    nb = s // t
    fb = f_col0 // LANES

    def kern(dc_ref, f_ref, b_ref, df_ref, db_ref, carry_ref):
        @pl.when(pl.program_id(0) == 0)
        def _():
            carry_ref[...] = jnp.zeros_like(carry_ref)
            db_ref[...] = jnp.zeros_like(db_ref)

        row, col = _iota2(t)
        dlf = _dot3_left(_mask01(col >= row), dc_ref[...]) + carry_ref[...]
        carry_ref[...] = dlf[0:1, :]
        v = f_ref[...] + b_ref[...]
        sig_neg = jnp.exp(-jnp.maximum(v, 0.0) - _log1p_exp_neg_abs(v))
        lane = lax.broadcasted_iota(jnp.int32, (t, LANES), 1)
        df = jnp.where(lane < n_heads, dlf * sig_neg, 0.0)
        df_ref[...] = df.astype(BF16)
        db_ref[...] += jnp.sum(df, axis=0, keepdims=True)

    return pl.pallas_call(
        kern, name="cum_bwd", grid=(nb,),
        in_specs=[pl.BlockSpec((t, LANES), lambda i: (nb - 1 - i, 0)),
                  pl.BlockSpec((t, LANES), lambda i: (nb - 1 - i, fb)),
                  pl.BlockSpec((1, LANES), lambda i: (0, 0))],
        out_specs=[pl.BlockSpec((t, LANES), lambda i: (nb - 1 - i, 0)), pl.BlockSpec((1, LANES), lambda i: (0, 0))],
        out_shape=[jax.ShapeDtypeStruct((s, LANES), BF16), jax.ShapeDtypeStruct((1, LANES), F32)],
        scratch_shapes=[pltpu.VMEM((1, LANES), F32)],
        compiler_params=_params("arbitrary"),
    )(dcum, gf, b_pad)


def _qkv_specs(s, t, n_heads, base):
    return [pl.BlockSpec((t, HEAD_DIM), lambda h, i: (i, base + h)),
            pl.BlockSpec((s, HEAD_DIM), lambda h, i: (0, base + n_heads + h)),
            pl.BlockSpec((s, HEAD_DIM), lambda h, i: (0, base + 2 * n_heads + h))]


def _strips(t):
    sr = _tile(t, ATT_STRIP, 8)
    return sr, t // sr, [slice(si * sr, (si + 1) * sr) for si in range(t // sr)]


def _key_minus_row(sr, t):
    return lax.broadcasted_iota(jnp.int32, (sr, t), 1) - lax.broadcasted_iota(jnp.int32, (sr, t), 0)


def _keep(valid, v):
    return v if valid is None else jnp.where(valid, v, 0.0)


def _sb_scores(q, k, diff, lim):
    z = lax.dot_general(q, k, NT, preferred_element_type=F32) * (HEAD_DIM ** -0.5)
    valid = None if lim is None else diff < lim
    l1p = _log1p_exp_neg_abs(z)
    return z, valid, l1p, _keep(valid, -jnp.maximum(z, 0.0) - l1p)


def sb_fwd(qkv, n_heads, base):
    s = qkv.shape[0]
    t = _tile(s, ATT_TILE, LANES)
    sr, ns, strips = _strips(t)

    def kern(q_ref, k_ref, v_ref, o_ref):
        i = pl.program_id(1)
        row, col = _iota2(t)
        after = _mask01(row > col)
        diff = _key_minus_row(sr, t)
        qs = [q_ref[sl, :] for sl in strips]

        def tile(j, carry, diagonal):
            runs, accs = carry
            off = pl.multiple_of(j * t, t)
            k = k_ref[pl.ds(off, t), :]
            v = v_ref[pl.ds(off, t), :]
            new_runs, new_accs = [], []
            for si in range(ns):
                z, valid, l1p, log_keep = _sb_scores(qs[si], k, diff, si * sr if diagonal else None)
                between = _dot2_right(log_keep, after) + runs[si]
                w = _keep(valid, jnp.exp(jnp.minimum(z, 0.0) - l1p + between))
                new_accs.append(accs[si] + jnp.dot(w.astype(BF16), v, preferred_element_type=F32))
                new_runs.append(runs[si] + jnp.sum(log_keep, axis=1, keepdims=True))
            return tuple(new_runs), tuple(new_accs)

        init = (tuple(jnp.zeros((sr, 1), F32) for _ in strips), tuple(jnp.zeros((sr, HEAD_DIM), F32) for _ in strips))
        _, accs = lax.fori_loop(0, i, lambda jj, c: tile(i - 1 - jj, c, False), tile(i, init, True))
        for sl, acc in zip(strips, accs):
            o_ref[sl, :] = acc.astype(o_ref.dtype)

    return pl.pallas_call(
        kern, name="sb_fwd", grid=(n_heads, s // t),
        in_specs=_qkv_specs(s, t, n_heads, base),
        out_specs=pl.BlockSpec((t, HEAD_DIM), lambda h, i: (i, h)),
        out_shape=jax.ShapeDtypeStruct((s, n_heads * HEAD_DIM), BF16),
        compiler_params=_params("parallel", "arbitrary"),
    )(qkv, qkv, qkv)


def sb_bwd(qkv, d_o, n_heads, base):
    s = qkv.shape[0]
    t = _tile(s, ATT_TILE, LANES)
    nq = s // t
    sr, ns, strips = _strips(t)
    scale = HEAD_DIM ** -0.5

    def kern(q_ref, k_ref, v_ref, do_ref, dq_ref, dk_ref, dv_ref, dk_acc, dv_acc, run_ref):
        i = pl.program_id(1)

        @pl.when(i == 0)
        def _():
            dk_acc[...] = jnp.zeros_like(dk_acc)
            dv_acc[...] = jnp.zeros_like(dv_acc)

        row, col = _iota2(t)
        after = _mask01(row > col)
        before = _mask01(row < col)
        diff = _key_minus_row(sr, t)
        qs = [q_ref[sl, :] for sl in strips]
        dos = [do_ref[sl, :] for sl in strips]

        def sweep1(j, runs, diagonal):
            k = k_ref[pl.ds(pl.multiple_of(j * t, t), t), :]
            new_runs = []
            for si, sl in enumerate(strips):
                _, _, _, log_keep = _sb_scores(qs[si], k, diff, si * sr if diagonal else None)
                run_ref[j, sl, :] = runs[si]
                new_runs.append(runs[si] + jnp.sum(log_keep, axis=1, keepdims=True))
            return tuple(new_runs)

        lax.fori_loop(0, i, lambda jj, c: sweep1(i - 1 - jj, c, False),
                      sweep1(i, tuple(jnp.zeros((sr, 1), F32) for _ in strips), True))

        def sweep2(j, carry, diagonal):
            run_es, dqs = carry
            off = pl.multiple_of(j * t, t)
            k = k_ref[pl.ds(off, t), :]
            v = v_ref[pl.ds(off, t), :]
            new_es, new_dqs = [], []
            dk_t = jnp.zeros((t, HEAD_DIM), F32)
            dv_t = jnp.zeros((t, HEAD_DIM), F32)
            for si, sl in enumerate(strips):
                z, valid, l1p, log_keep = _sb_scores(qs[si], k, diff, si * sr if diagonal else None)
                between = _dot2_right(log_keep, after) + run_ref[j, sl, :]
                w = _keep(valid, jnp.exp(jnp.minimum(z, 0.0) - l1p + between))
                dw = lax.dot_general(dos[si], v, NT, preferred_element_type=F32)
                e = dw * w
                e_before = _dot2_right(e, before) + run_es[si]
                keep = jnp.exp(log_keep)
                dz = _keep(valid, e * keep - e_before * (1.0 - keep)) * scale
                dzb = dz.astype(BF16)
                new_dqs.append(dqs[si] + jnp.dot(dzb, k, preferred_element_type=F32))
                dk_t = dk_t + lax.dot_general(dzb, qs[si], TN, preferred_element_type=F32)
                dv_t = dv_t + lax.dot_general(w.astype(BF16), dos[si], TN, preferred_element_type=F32)
                new_es.append(run_es[si] + jnp.sum(e, axis=1, keepdims=True))
            dk_acc[pl.ds(off, t), :] += dk_t
            dv_acc[pl.ds(off, t), :] += dv_t
            return tuple(new_es), tuple(new_dqs)

        init = (tuple(jnp.zeros((sr, 1), F32) for _ in strips), tuple(jnp.zeros((sr, HEAD_DIM), F32) for _ in strips))
        _, dqs = sweep2(i, lax.fori_loop(0, i, lambda j, c: sweep2(j, c, False), init), True)
        for sl, dq in zip(strips, dqs):
            dq_ref[sl, :] = dq.astype(BF16)

        @pl.when(i == nq - 1)
        def _():
            dk_ref[...] = dk_acc[...].astype(BF16)
            dv_ref[...] = dv_acc[...].astype(BF16)

    out = jax.ShapeDtypeStruct((s, n_heads * HEAD_DIM), BF16)
    head_blk = pl.BlockSpec((s, HEAD_DIM), lambda h, i: (0, h))
    tile_blk = pl.BlockSpec((t, HEAD_DIM), lambda h, i: (i, h))
    return pl.pallas_call(
        kern, name="sb_bwd", grid=(n_heads, nq),
        in_specs=_qkv_specs(s, t, n_heads, base) + [tile_blk],
        out_specs=[tile_blk, head_blk, head_blk],
        out_shape=[out, out, out],
        scratch_shapes=[pltpu.VMEM((s, HEAD_DIM), F32), pltpu.VMEM((s, HEAD_DIM), F32), pltpu.VMEM((nq, t, 1), F32)],
        compiler_params=_params("parallel", "arbitrary"),
    )(qkv, qkv, qkv, d_o)


def _fox_scores(q, k, cq, ck, diff, lim):
    sc = lax.dot_general(q, k, NT, preferred_element_type=F32) * (HEAD_DIM ** -0.5)
    sc = sc + cq - ck
    if lim is None:
        return sc, None
    valid = diff < lim
    return jnp.where(valid, sc, NEG_BIG), valid


def fox_fwd(qkv, cum_col, cum_row, n_heads, base):
    s = qkv.shape[0]
    t = _tile(s, ATT_TILE, LANES)
    sr, ns, strips = _strips(t)

    def kern(q_ref, k_ref, v_ref, cq_ref, ck_ref, o_ref, lse_ref):
        i = pl.program_id(1)
        diff = _key_minus_row(sr, t)
        qs = [q_ref[sl, :] for sl in strips]
        cqs = [cq_ref[0, sl, :] for sl in strips]

        def tile(j, carry, diagonal):
            off = pl.multiple_of(j * t, t)
            k = k_ref[pl.ds(off, t), :]
            v = v_ref[pl.ds(off, t), :]
            ck = ck_ref[0, :, pl.ds(off, t)]
            out = []
            for si in range(ns):
                m, l, acc = carry[si]
                sc, _ = _fox_scores(qs[si], k, cqs[si], ck, diff, si * sr + 1 if diagonal else None)
                m_new = jnp.maximum(m, jnp.max(sc, axis=1, keepdims=True))
                p = jnp.exp(sc - m_new)
                alpha = jnp.exp(m - m_new)
                l = alpha * l + jnp.sum(p, axis=1, keepdims=True)
                acc = alpha * acc + jnp.dot(p.astype(BF16), v, preferred_element_type=F32)
                out.append((m_new, l, acc))
            return tuple(out)

        init = tuple((jnp.full((sr, 1), NEG_BIG, F32), jnp.zeros((sr, 1), F32), jnp.zeros((sr, HEAD_DIM), F32))
                     for _ in strips)
        res = tile(i, lax.fori_loop(0, i, lambda j, c: tile(j, c, False), init), True)
        for sl, (m, l, acc) in zip(strips, res):
            o_ref[sl, :] = acc / l
            lse_ref[0, sl, :] = m + jnp.log(l)

    col_blk = pl.BlockSpec((1, t, 1), lambda h, i: (h, i, 0))
    return pl.pallas_call(
        kern, name="fox_fwd", grid=(n_heads, s // t),
        in_specs=_qkv_specs(s, t, n_heads, base) + [col_blk, pl.BlockSpec((1, 1, s), lambda h, i: (h, 0, 0))],
        out_specs=[pl.BlockSpec((t, HEAD_DIM), lambda h, i: (i, h)), col_blk],
        out_shape=[jax.ShapeDtypeStruct((s, n_heads * HEAD_DIM), F32), jax.ShapeDtypeStruct((n_heads, s, 1), F32)],
        compiler_params=_params("parallel", "arbitrary"),
    )(qkv, qkv, qkv, cum_col, cum_row)


def fox_bwd(qkv, cum_col, cum_row, o, d_o, lse, n_heads, base):
    s = qkv.shape[0]
    t = _tile(s, ATT_TILE, LANES)
    nq = s // t
    sr, ns, strips = _strips(t)
    scale = HEAD_DIM ** -0.5

    def kern(q_ref, k_ref, v_ref, cq_ref, ck_ref, o_ref, do_ref, lse_ref,
             dq_ref, dk_ref, dv_ref, dcq_ref, dck_ref, dk_acc, dv_acc, dck_acc):
        i = pl.program_id(1)

        @pl.when(i == 0)
        def _():
            dk_acc[...] = jnp.zeros_like(dk_acc)
            dv_acc[...] = jnp.zeros_like(dv_acc)
            dck_acc[...] = jnp.zeros_like(dck_acc)

        diff = _key_minus_row(sr, t)
        qs = [q_ref[sl, :] for sl in strips]
        dos = [do_ref[sl, :] for sl in strips]
        cqs = [cq_ref[0, sl, :] for sl in strips]
        lses = [lse_ref[0, sl, :] for sl in strips]
        deltas = [jnp.sum(dos[si].astype(F32) * o_ref[sl, :], axis=1, keepdims=True) for si, sl in enumerate(strips)]

        def tile(j, carry, diagonal):
            off = pl.multiple_of(j * t, t)
            k = k_ref[pl.ds(off, t), :]
            v = v_ref[pl.ds(off, t), :]
            ck = ck_ref[0, :, pl.ds(off, t)]
            out = []
            dk_t = jnp.zeros((t, HEAD_DIM), F32)
            dv_t = jnp.zeros((t, HEAD_DIM), F32)
            dck_t = jnp.zeros((1, t), F32)
            for si in range(ns):
                dq, dcq = carry[si]
                sc, valid = _fox_scores(qs[si], k, cqs[si], ck, diff, si * sr + 1 if diagonal else None)
                p = _keep(valid, jnp.exp(sc - lses[si]))
                dp = lax.dot_general(dos[si], v, NT, preferred_element_type=F32)
                ds = p * (dp - deltas[si])
                dsb = (ds * scale).astype(BF16)
                dq = dq + jnp.dot(dsb, k, preferred_element_type=F32)
                dk_t = dk_t + lax.dot_general(dsb, qs[si], TN, preferred_element_type=F32)
                dv_t = dv_t + lax.dot_general(p.astype(BF16), dos[si], TN, preferred_element_type=F32)
                dck_t = dck_t + jnp.sum(ds, axis=0, keepdims=True)
                out.append((dq, dcq + jnp.sum(ds, axis=1, keepdims=True)))
            dk_acc[pl.ds(off, t), :] += dk_t
            dv_acc[pl.ds(off, t), :] += dv_t
            dck_acc[:, pl.ds(off, t)] -= dck_t
            return tuple(out)

        init = tuple((jnp.zeros((sr, HEAD_DIM), F32), jnp.zeros((sr, 1), F32)) for _ in strips)
        res = tile(i, lax.fori_loop(0, i, lambda j, c: tile(j, c, False), init), True)
        for sl, (dq, dcq) in zip(strips, res):
            dq_ref[sl, :] = dq.astype(BF16)
            dcq_ref[0, sl, :] = dcq

        @pl.when(i == nq - 1)
        def _():
            dk_ref[...] = dk_acc[...].astype(BF16)
            dv_ref[...] = dv_acc[...].astype(BF16)
            dck_ref[0] = dck_acc[...]

    out = jax.ShapeDtypeStruct((s, n_heads * HEAD_DIM), BF16)
    head_blk = pl.BlockSpec((s, HEAD_DIM), lambda h, i: (0, h))
    tile_blk = pl.BlockSpec((t, HEAD_DIM), lambda h, i: (i, h))
    col_blk = pl.BlockSpec((1, t, 1), lambda h, i: (h, i, 0))
    row_blk = pl.BlockSpec((1, 1, s), lambda h, i: (h, 0, 0))
    return pl.pallas_call(
        kern, name="fox_bwd", grid=(n_heads, nq),
        in_specs=_qkv_specs(s, t, n_heads, base) + [col_blk, row_blk, tile_blk, tile_blk, col_blk],
        out_specs=[tile_blk, head_blk, head_blk, col_blk, row_blk],
        out_shape=[out, out, out, jax.ShapeDtypeStruct((n_heads, s, 1), F32),
                   jax.ShapeDtypeStruct((n_heads, 1, s), F32)],
        scratch_shapes=[pltpu.VMEM((s, HEAD_DIM), F32), pltpu.VMEM((s, HEAD_DIM), F32), pltpu.VMEM((1, s), F32)],
        compiler_params=_params("parallel", "arbitrary"),
    )(qkv, qkv, qkv, cum_col, cum_row, o, d_o, lse)


def _place():
    x, y, c = lax.axis_index("x"), lax.axis_index("y"), lax.axis_index("c")
    other_chips = [(1 - x, y), (x, 1 - y), (1 - x, 1 - y)]
    return x, y, c, other_chips


ANY = pl.BlockSpec(memory_space=pl.ANY)


def _remote(src, dst, send_sem, recv_sem, dev):
    return pltpu.make_async_remote_copy(src_ref=src, dst_ref=dst, send_sem=send_sem, recv_sem=recv_sem,
                                        device_id=dev, device_id_type=MESH)


def place_transposed(name, w_t, chip):
    c, _, r = w_t.shape
    tc = LANES

    def kern(chip_ref, w_ref, o_ref):
        o_ref[...] = w_ref[:, 0, :].T.astype(BF16)

    return pl.pallas_call(
        kern, name=name,
        grid_spec=pltpu.PrefetchScalarGridSpec(
            num_scalar_prefetch=1, grid=(pl.cdiv(c, tc),),
            in_specs=[pl.BlockSpec((tc, 1, r), lambda j, chip_ref: (j, 0, 0))],
            out_specs=pl.BlockSpec((None, r, tc), lambda j, chip_ref: (chip_ref[0], 0, j))),
        out_shape=jax.ShapeDtypeStruct((N_CHIPS, r, c), BF16),
        compiler_params=_params("parallel"),
    )(chip, w_t)


def adam_update_transposed(name, w_t, m_t, v_t, g_buf):
    c, _, r = w_t.shape
    tc = LANES

    def kern(w_ref, m_ref, v_ref, g_ref, go_ref, dl_ref, nm_ref, nv_ref):
        g = g_ref[...].T
        delta, nm, nv = _adam(w_ref[:, 0, :], g, m_ref[:, 0, :], v_ref[:, 0, :])
        go_ref[:, 0, :] = g
        dl_ref[:, 0, :] = delta
        nm_ref[:, 0, :] = nm
        nv_ref[:, 0, :] = nv

    blk = pl.BlockSpec((tc, 1, r), lambda j: (j, 0, 0))
    out = jax.ShapeDtypeStruct((c, 1, r), F32)
    return pl.pallas_call(
        kern, name=name, grid=(pl.cdiv(c, tc),),
        in_specs=[blk, blk, blk, pl.BlockSpec((r, tc), lambda j: (0, j))],
        out_specs=[blk] * 4, out_shape=[out] * 4, compiler_params=_params("parallel"),
    )(w_t, m_t, v_t, g_buf)


def cast_place(name, ws, chip):
    r = ws[0].shape[1]
    cs = [w.shape[2] for w in ws]
    tr = _tile(r, 256, 16)

    def kern(chip_ref, *refs):
        o_ref = refs[-1]
        off = 0
        for w_ref, c in zip(refs[:-1], cs):
            o_ref[:, off:off + c] = w_ref[...].astype(BF16)
            off += c

    return pl.pallas_call(
        kern, name=name,
        grid_spec=pltpu.PrefetchScalarGridSpec(
            num_scalar_prefetch=1, grid=(r // tr,),
            in_specs=[pl.BlockSpec((None, tr, c), lambda i, chip_ref: (0, i, 0)) for c in cs],
            out_specs=pl.BlockSpec((None, tr, sum(cs)), lambda i, chip_ref: (chip_ref[0], i, 0))),
        out_shape=jax.ShapeDtypeStruct((N_CHIPS, r, sum(cs)), BF16),
        compiler_params=_params("parallel"),
    )(chip, *ws)


HBM = pl.BlockSpec(memory_space=pltpu.HBM)
SEM = pl.BlockSpec(memory_space=pltpu.SEMAPHORE)
SPLIT = pltpu.CompilerParams(has_side_effects=pltpu.SideEffectType.DATAFLOW_SIDE_EFFECTING)


def _in_hbm(a):
    return pltpu.with_memory_space_constraint(a, pltpu.HBM)


def _slab_rows(ref, k, core):
    half = ref.shape[1] // 2
    return ref.at[k, pl.ds(pl.multiple_of(core * half, 16), half)]


def gather_start(name, bufs):
    n = len(bufs)

    def body(*refs):
        ins, send, recv, token = refs[:n], refs[n], refs[n + 1], refs[-1]
        x, y, c, chips = _place()
        me = 2 * x + y
        for a in range(n):
            for j in range(3):
                rows = _slab_rows(ins[a], me, c)
                _remote(rows, rows, send.at[3 * a + j], recv.at[3 * a + j], (chips[j][0], chips[j][1], c)).start()
        token[...] = jnp.zeros_like(token)

    sem = pltpu.SemaphoreType.DMA((3 * n,))
    res = pl.pallas_call(
        body, name=name, in_specs=[HBM] * n, out_specs=[SEM, SEM] + [HBM] * n + [pl.BlockSpec(memory_space=pltpu.VMEM)],
        out_shape=[sem, sem] + [pltpu.HBM(b.shape, b.dtype) for b in bufs] + [jax.ShapeDtypeStruct((8, LANES), F32)],
        input_output_aliases={a: 2 + a for a in range(n)}, compiler_params=SPLIT,
    )(*[_in_hbm(b) for b in bufs])
    return res[0], res[1], res[2:2 + n], res[-1]


def gather_wait(name, bufs, send_sems, recv_sems, after):
    n = len(bufs)

    def body(*refs):
        ins, send, recv = refs[:n], refs[n], refs[n + 1]
        x, y, c, chips = _place()
        me = 2 * x + y
        for a in range(n):
            for j in range(3):
                dev = (chips[j][0], chips[j][1], c)
                mine = _slab_rows(ins[a], me, c)
                _remote(mine, mine, send.at[3 * a + j], recv.at[3 * a + j], dev).wait_send()
                land = _slab_rows(ins[a], 2 * chips[j][0] + chips[j][1], c)
                _remote(land, land, send.at[3 * a + j], recv.at[3 * a + j], dev).wait_recv()

    return pl.pallas_call(
        body, name=name, in_specs=[HBM] * n + [SEM, SEM] + [ANY] * len(after), out_specs=[HBM] * n,
        out_shape=[pltpu.HBM(b.shape, b.dtype) for b in bufs],
        input_output_aliases={a: a for a in range(n)}, compiler_params=SPLIT,
    )(*bufs, send_sems, recv_sems, *after)


def gather_forward(name, bufs):
    n = len(bufs)

    def body(*refs):
        outs = refs[n:2 * n]
        send_sems, recv_sems = refs[2 * n:]
        x, y, c, chips = _place()
        sibling = (x, y, 1 - c)

        def d2d(a, j, core):
            rows = _slab_rows(outs[a], 2 * chips[j][0] + chips[j][1], core)
            return _remote(rows, rows, send_sems.at[3 * a + j], recv_sems.at[3 * a + j], sibling)

        pairs = [(a, j) for a in range(n) for j in range(3)]
        for a, j in pairs:
            d2d(a, j, c).start()
        for a, j in pairs:
            d2d(a, j, 1 - c).wait_recv()
        for a, j in pairs:
            d2d(a, j, c).wait_send()

    return pl.pallas_call(
        body, name=name, in_specs=[ANY] * n, out_specs=[ANY] * n,
        out_shape=[jax.ShapeDtypeStruct(b.shape, b.dtype) for b in bufs],
        input_output_aliases={a: a for a in range(n)},
        scratch_shapes=[pltpu.SemaphoreType.DMA((3 * n,)), pltpu.SemaphoreType.DMA((3 * n,))],
    )(*bufs)


def _forward_plan(refs):
    x, y, c, chips = _place()
    out = []
    for ref in refs:
        for j in range(3):
            k = 2 * chips[j][0] + chips[j][1]
            out.append((_slab_rows(ref, k, c), _slab_rows(ref, k, c), _slab_rows(ref, k, 1 - c)))
    return out


def _swap_plan(refs):
    x, y, c, _ = _place()
    n = len(refs) // 2
    out = []
    for a in range(n):
        half = refs[a].shape[1] // 2
        src = refs[a].at[:, pl.ds(pl.multiple_of((1 - c) * half, 16), half), :]
        out.append((src, refs[n + a], refs[n + a]))
    return out


def sibling_start(name, arrays, plan, n_copies, after=()):
    n = len(arrays)

    n_in = n + len(after)

    def body(*refs):
        send, recv, token = refs[n_in], refs[n_in + 1], refs[-1]
        x, y, c, _ = _place()
        for idx, (src, dst, _) in enumerate(plan(refs[:n])):
            _remote(src, dst, send.at[idx], recv.at[idx], (x, y, 1 - c)).start()
        token[...] = jnp.zeros_like(token)

    sem = pltpu.SemaphoreType.DMA((n_copies,))
    res = pl.pallas_call(
        body, name=name, in_specs=[HBM] * n + [ANY] * len(after),
        out_specs=[SEM, SEM] + [HBM] * n + [pl.BlockSpec(memory_space=pltpu.VMEM)],
        out_shape=[sem, sem] + [pltpu.HBM(b.shape, b.dtype) for b in arrays] + [jax.ShapeDtypeStruct((8, LANES), F32)],
        input_output_aliases={a: 2 + a for a in range(n)}, compiler_params=SPLIT,
    )(*[_in_hbm(b) for b in arrays], *after)
    return res[0], res[1], res[2:2 + n], res[-1]


def sibling_wait(name, arrays, send_sems, recv_sems, plan, after):
    n = len(arrays)

    def body(*refs):
        send, recv = refs[n], refs[n + 1]
        x, y, c, _ = _place()
        for idx, (src, dst, filled) in enumerate(plan(refs[:n])):
            _remote(src, dst, send.at[idx], recv.at[idx], (x, y, 1 - c)).wait_send()
            _remote(filled, filled, send.at[idx], recv.at[idx], (x, y, 1 - c)).wait_recv()

    return pl.pallas_call(
        body, name=name, in_specs=[HBM] * n + [SEM, SEM] + [ANY] * len(after), out_specs=[HBM] * n,
        out_shape=[pltpu.HBM(b.shape, b.dtype) for b in arrays],
        input_output_aliases={a: a for a in range(n)}, compiler_params=SPLIT,
    )(*arrays, send_sems, recv_sems, *after)


def swap_halves(name, pieces):
    n = len(pieces)
    halves = [p.shape[1] // 2 for p in pieces]

    def body(*refs):
        ins, outs = refs[:n], refs[n:2 * n]
        send_sems, recv_sems = refs[2 * n:]
        x, y, c, _ = _place()
        cps = [_remote(ins[a].at[:, pl.ds(pl.multiple_of((1 - c) * halves[a], 16), halves[a]), :], outs[a],
                       send_sems.at[a], recv_sems.at[a], (x, y, 1 - c)) for a in range(n)]
        for cp in cps:
            cp.start()
        for cp in cps:
            cp.wait()

    return pl.pallas_call(
        body, name=name, in_specs=[ANY] * n, out_specs=[ANY] * n,
        out_shape=[jax.ShapeDtypeStruct((N_CHIPS, h, p.shape[2]), p.dtype) for p, h in zip(pieces, halves)],
        scratch_shapes=[pltpu.SemaphoreType.DMA((n,)), pltpu.SemaphoreType.DMA((n,))],
    )(*pieces)


def pair_sum(name, pieces, got, core):
    _, r, w = pieces.shape
    half = r // 2
    tr = _tile(half, 256, 16)

    def kern(core_ref, p_ref, g_ref, o_ref):
        o_ref[...] = (p_ref[...].astype(F32) + g_ref[...].astype(F32)).astype(o_ref.dtype)

    return pl.pallas_call(
        kern, name=name,
        grid_spec=pltpu.PrefetchScalarGridSpec(
            num_scalar_prefetch=1, grid=(N_CHIPS, half // tr),
            in_specs=[pl.BlockSpec((None, None, tr, w), lambda k, i, core_ref: (k, core_ref[0], i, 0)),
                      pl.BlockSpec((None, tr, w), lambda k, i, core_ref: (k, i, 0))],
            out_specs=pl.BlockSpec((None, tr, w), lambda k, i, core_ref: (k, i, 0))),
        out_shape=jax.ShapeDtypeStruct((N_CHIPS, half, w), pieces.dtype),
        compiler_params=_params("parallel", "parallel"),
    )(core, pieces.reshape(N_CHIPS, 2, half, w), got)


def _scatter_copies(sums, lands, send, recv):
    x, y, c, chips = _place()
    return [_remote(sums[a].at[2 * chips[j][0] + chips[j][1]], lands[a].at[j], send.at[3 * a + j], recv.at[3 * a + j],
                    (chips[j][0], chips[j][1], c)) for a in range(len(sums)) for j in range(3)]


def scatter_start(name, sums):
    n = len(sums)
    lands = [lax.empty((3,) + t.shape[1:], t.dtype) for t in sums]

    def body(*refs):
        ins, land_in, send, recv, token = refs[:n], refs[n:2 * n], refs[2 * n], refs[2 * n + 1], refs[-1]
        for cp in _scatter_copies(ins, land_in, send, recv):
            cp.start()
        token[...] = jnp.zeros_like(token)

    sem = pltpu.SemaphoreType.DMA((3 * n,))
    res = pl.pallas_call(
        body, name=name, in_specs=[HBM] * (2 * n),
        out_specs=[SEM, SEM] + [HBM] * (2 * n) + [pl.BlockSpec(memory_space=pltpu.VMEM)],
        out_shape=[sem, sem] + [pltpu.HBM(t.shape, t.dtype) for t in sums + lands] + [jax.ShapeDtypeStruct((8, LANES), F32)],
        input_output_aliases={a: 2 + a for a in range(2 * n)}, compiler_params=SPLIT,
    )(*[_in_hbm(t) for t in sums + lands])
    return res[0], res[1], res[2:2 + n], res[2 + n:2 + 2 * n], res[-1]


def scatter_wait(name, sums, lands, send_sems, recv_sems, after):
    n = len(sums)

    def body(*refs):
        ins, land_in, send, recv = refs[:n], refs[n:2 * n], refs[2 * n], refs[2 * n + 1]
        for cp in _scatter_copies(ins, land_in, send, recv):
            cp.wait_send()
            cp.wait_recv()

    res = pl.pallas_call(
        body, name=name, in_specs=[HBM] * (2 * n) + [SEM, SEM, ANY], out_specs=[HBM] * (2 * n),
        out_shape=[pltpu.HBM(t.shape, t.dtype) for t in sums + lands],
        input_output_aliases={a: a for a in range(2 * n)}, compiler_params=SPLIT,
    )(*sums, *lands, send_sems, recv_sems, after)
    return res[:n], res[n:]


def chip_sum(name, sums, got, chip, core):
    _, half, w = sums.shape
    tr = _tile(half, 256, 16)
    nb = half // tr

    def kern(ids_ref, s_ref, g0_ref, g1_ref, g2_ref, o_ref):
        o_ref[...] = ((s_ref[...].astype(F32) + g0_ref[...].astype(F32)) + g1_ref[...].astype(F32)) \
            + g2_ref[...].astype(F32)

    def got_spec(j):
        return pl.BlockSpec((None, tr, w), lambda i, ids_ref: (j, i, 0))

    return pl.pallas_call(
        kern, name=name,
        grid_spec=pltpu.PrefetchScalarGridSpec(
            num_scalar_prefetch=1, grid=(nb,),
            in_specs=[pl.BlockSpec((None, tr, w), lambda i, ids_ref: (ids_ref[0], i, 0)),
                      got_spec(0), got_spec(1), got_spec(2)],
            out_specs=pl.BlockSpec((tr, w), lambda i, ids_ref: (ids_ref[1] * nb + i, 0))),
        out_shape=jax.ShapeDtypeStruct((2 * half, w), F32),
        compiler_params=_params("parallel"),
    )(jnp.concatenate([chip, core]), sums, got, got, got)


def join_halves(name, shards):
    n = len(shards)
    halves = [g.shape[0] // 2 for g in shards]

    def body(*refs):
        outs = refs[n:2 * n]
        send_sems, recv_sems = refs[2 * n:]
        x, y, c, _ = _place()
        cps = []
        for a in range(n):
            rows = outs[a].at[pl.ds(pl.multiple_of(c * halves[a], 8), halves[a])]
            cps.append(_remote(rows, rows, send_sems.at[a], recv_sems.at[a], (x, y, 1 - c)))
        for cp in cps:
            cp.start()
        for cp in cps:
            cp.wait()

    return pl.pallas_call(
        body, name=name, in_specs=[ANY] * n, out_specs=[ANY] * n,
        out_shape=[jax.ShapeDtypeStruct(g.shape, g.dtype) for g in shards],
        input_output_aliases={a: a for a in range(n)},
        scratch_shapes=[pltpu.SemaphoreType.DMA((n,)), pltpu.SemaphoreType.DMA((n,))],
    )(*shards)


def _adam(w, g, m, v):
    m = ADAM_B1 * m + (1.0 - ADAM_B1) * g
    v = ADAM_B2 * v + (1.0 - ADAM_B2) * (g * g)
    m_hat = m / (1.0 - ADAM_B1 ** ADAM_STEP)
    v_hat = v / (1.0 - ADAM_B2 ** ADAM_STEP)
    delta = -ADAM_LR * (m_hat / (jnp.sqrt(v_hat) + ADAM_EPS) + ADAM_WD * w)
    return delta, m, v


def small_allreduce_adam(g_part, w, m, v):
    n_dev = 8
    r, d = g_part.shape

    def body(g_ref, w_ref, m_ref, v_ref, gs_ref, dl_ref, nm_ref, nv_ref, all_ref, send_sems, recv_sems):
        x, y, c, _ = _place()
        me = 4 * x + 2 * y + c
        all_ref[me] = g_ref[...]
        cps = []
        for rel in range(1, n_dev):
            px = 1 - x if rel & 4 else x
            py = 1 - y if rel & 2 else y
            pc = 1 - c if rel & 1 else c
            cps.append(_remote(g_ref, all_ref.at[me], send_sems.at[rel - 1], recv_sems.at[rel - 1], (px, py, pc)))
        for cp in cps:
            cp.start()
        for cp in cps:
            cp.wait()
        total = all_ref[0]
        for dev in range(1, n_dev):
            total = total + all_ref[dev]
        gs_ref[...] = total
        delta, nm, nv = _adam(w_ref[...], total, m_ref[...], v_ref[...])
        dl_ref[...] = delta
        nm_ref[...] = nm
        nv_ref[...] = nv

    vm = pl.BlockSpec(memory_space=pltpu.VMEM)
    out = jax.ShapeDtypeStruct((r, d), F32)
    return pl.pallas_call(
        body, name="small_allreduce_adam", in_specs=[vm, vm, vm, vm], out_specs=[vm, vm, vm, vm],
        out_shape=[out, out, out, out],
        scratch_shapes=[pltpu.VMEM((n_dev, r, d), F32), pltpu.SemaphoreType.DMA((n_dev - 1,)),
                        pltpu.SemaphoreType.DMA((n_dev - 1,))],
    )(g_part, w, m, v)


def adam_update(name, w, m, v, g_buf, col_blk, after=None):
    w, m, v = w[0], m[0], v[0]
    r, c = w.shape
    tr = _tile(r, 128, 8)
    extra = [] if after is None else [after]

    def kern(w_ref, m_ref, v_ref, g_ref, *rest):
        go_ref, dl_ref, nm_ref, nv_ref = rest[len(extra):]
        g = g_ref[...]
        delta, nm, nv = _adam(w_ref[...], g, m_ref[...], v_ref[...])
        go_ref[...] = g
        dl_ref[...] = delta
        nm_ref[...] = nm
        nv_ref[...] = nv

    blk = pl.BlockSpec((tr, c), lambda i: (i, 0))
    out = jax.ShapeDtypeStruct((r, c), F32)
    res = pl.pallas_call(
        kern, name=name, grid=(r // tr,),
        in_specs=[blk, blk, blk, pl.BlockSpec((tr, c), lambda i: (i, col_blk))] + [ANY] * len(extra),
        out_specs=[blk] * 4, out_shape=[out] * 4, compiler_params=_params("parallel"),
    )(w, m, v, g_buf, *extra)
    return [a[None] for a in res]


def _w_in_segments(cw, n_qkv, n_heads, d):
    out = []

    def add(lo, hi, main):
        while lo < hi:
            k, a = divmod(lo, cw)
            w = min(cw - a, hi - lo)
            out.append((k, a, main, w))
            lo, main = lo + w, main + w

    add(0, n_qkv, 0)
    add(n_qkv + n_heads, N_CHIPS * cw, n_qkv)
    add(n_qkv, n_qkv + n_heads, n_qkv + 2 * d)
    return out


def regroup_w_in(g_in, segments, n_main):
    _, d, cw = g_in.shape
    tr = _tile(d, 128, 16)
    n_real = max(m + w for _, _, m, w in segments)

    def kern(s_ref, o_ref):
        for k, a, m, w in segments:
            o_ref[:, m:m + w] = s_ref[k, :, a:a + w]
        o_ref[:, n_real:] = jnp.zeros((tr, n_main - n_real), o_ref.dtype)

    return pl.pallas_call(
        kern, name="regroup_w_in", grid=(d // tr,),
        in_specs=[pl.BlockSpec((N_CHIPS, tr, cw), lambda i: (0, i, 0))],
        out_specs=pl.BlockSpec((tr, n_main), lambda i: (i, 0)),
        out_shape=jax.ShapeDtypeStruct((d, n_main), g_in.dtype), compiler_params=_params("parallel"),
    )(g_in)


def regroup_dw_in(dw_main, segments, cw):
    d, n_main = dw_main.shape
    tr = _tile(d, 128, 16)

    def kern(s_ref, o_ref):
        for k, a, m, w in segments:
            o_ref[k, :, a:a + w] = s_ref[:, m:m + w]

    return pl.pallas_call(
        kern, name="regroup_dw_in", grid=(d // tr,),
        in_specs=[pl.BlockSpec((tr, n_main), lambda i: (i, 0))],
        out_specs=pl.BlockSpec((N_CHIPS, tr, cw), lambda i: (0, i, 0)),
        out_shape=jax.ShapeDtypeStruct((N_CHIPS, d, cw), dw_main.dtype), compiler_params=_params("parallel"),
    )(dw_main)


def kernel(x, norm_mix_pre, norm_mix_post, w_in, b_forget, w_branch_sb, w_branch_fox, w_out, norm_ffn_pre, norm_ffn_post, w_ffn_gate, w_ffn_up, w_ffn_down, loss_target, m_norm_mix_pre, m_norm_mix_post, m_w_in, m_b_forget, m_w_branch_sb, m_w_branch_fox, m_w_out, m_norm_ffn_pre, m_norm_ffn_post, m_w_ffn_gate, m_w_ffn_up, m_w_ffn_down, v_norm_mix_pre, v_norm_mix_post, v_w_in, v_b_forget, v_w_branch_sb, v_w_branch_fox, v_w_out, v_norm_ffn_pre, v_norm_ffn_post, v_w_ffn_gate, v_w_ffn_up, v_w_ffn_down):
    s, d = x.shape[1], x.shape[2]
    n_heads = b_forget.shape[1]
    d_att = n_heads * HEAD_DIM
    c_in = w_in.shape[2]
    c_br = w_branch_sb.shape[2]
    c_gu = w_ffn_gate.shape[2]
    d_ff = c_gu * N_CHIPS
    d_in = c_in * N_CHIPS
    f_pad = 512
    n_qkv = 6 * d_att
    n_gf = 2 * d + f_pad
    core = lax.axis_index("c").astype(jnp.int32).reshape(1)
    chip = (2 * lax.axis_index("x") + lax.axis_index("y")).astype(jnp.int32).reshape(1)

    as_t = lambda a: jnp.transpose(a, (2, 0, 1))
    w_in_t, m_in_t, v_in_t = as_t(w_in), as_t(m_w_in), as_t(v_w_in)
    in_send, in_recv, in_bufs, in_token = gather_start(
        "gather_start_w_in", [place_transposed("place_w_in", w_in_t, chip)])
    ag_send, ag_recv, ag_bufs, ag_token = gather_start("gather_start_rest", [
        cast_place("place_branch", [w_branch_sb, w_branch_fox], chip),
        cast_place("place_out", [w_out + in_token[0, 0]], chip),
        cast_place("place_gate_up", [w_ffn_gate, w_ffn_up], chip),
        cast_place("place_down", [w_ffn_down], chip)])
    g_in, = gather_forward("forward_w_in", gather_wait("gather_wait_w_in", in_bufs, in_send, in_recv,
                                                       [ag_token]))
    segments = _w_in_segments(c_in, n_qkv, n_heads, d)
    w_main = regroup_w_in(g_in, segments, n_qkv + n_gf)
    x2 = x[0]
    tgt = loss_target[0]
    b_pad = jnp.pad(b_forget, ((0, 0), (0, LANES - n_heads)))

    u = norm_in(x2, norm_mix_pre)
    qkv = mm(u, w_main, "nn", BF16, "proj_qkv", b_win=(0, n_qkv))
    gf = mm(u, w_main, "nn", F32, "proj_gates", b_win=(n_qkv, n_gf))
    cum = cum_fwd(gf, b_pad, 2 * d)
    cum_heads = cum[:, :n_heads].T
    cum_col, cum_row = cum_heads[:, :, None], cum_heads[:, None, :]
    o_sb = sb_fwd(qkv, n_heads, 0)
    o_fx, lse = fox_fwd(qkv, cum_col, cum_row, n_heads, 3 * n_heads)
    rest = gather_wait("gather_wait_rest", ag_bufs, ag_send, ag_recv, [o_sb, o_fx])
    g_br, g_out = gather_forward("forward_small", rest[:2])
    fb_send, fb_recv, fb_bufs, fb_token = sibling_start("forward_big_start", rest[2:], _forward_plan, 6, after=[g_br])
    w_o = g_out.reshape(d, d)
    bsb = mm(o_sb, g_br, "nn", F32, "branch_sb", tn=c_br, chunks=(1, 0), after=fb_token)
    bfx = mm(o_fx, g_br, "nn", F32, "branch_fox", tn=c_br, chunks=(1, 1), after=fb_token)
    merged = gate_fwd(bsb, bfx, gf)
    mix = mm(merged, w_o, "nn", F32, "out_proj")
    h1, u2 = mid_fwd(x2, mix, norm_mix_post, norm_ffn_pre)
    g_gu, g_dn = sibling_wait("forward_big_wait", fb_bufs, fb_send, fb_recv, _forward_plan, [u2])
    w_dn = g_dn.reshape(d_ff, d)
    gu, act = ffn_up_fused(u2, g_gu, c_gu)
    ff = mm(act, w_dn, "nn", F32, "ffn_down")
    dy, d_ff_out, dg_fpost, loss_part = loss_head(h1, ff, norm_ffn_post, tgt)

    p_dn = mm(act, d_ff_out, "tn", BF16, "dw_ffn_down").reshape(N_CHIPS, d_ff // N_CHIPS, d)
    d_gu = ffn_down_bwd_fused(d_ff_out, w_dn, gu, c_gu)
    p_gu = mm(u2, d_gu, "tn", BF16, "dw_ffn_gate_up", tn=c_gu, chunks=(2, 0),
              out_into=lax.empty((N_CHIPS, d, 2 * c_gu), BF16))
    sw_send, sw_recv, sw_arrs, sw_token = sibling_start(
        "swap_big_start", [p_gu, p_dn, lax.empty((N_CHIPS, d // 2, 2 * c_gu), BF16),
                           lax.empty((N_CHIPS, d_ff // N_CHIPS // 2, d), BF16)], _swap_plan, 2)
    du2 = mm(d_gu, g_gu, "nt", F32, "d_u2", tk=c_gu, chunks=(2, 0), after=sw_token)
    dh1, d_mix, dg_fpre, dg_post = mid_bwd(dy, du2, h1, mix, norm_ffn_pre, norm_mix_post)
    p_out = mm(merged, d_mix, "tn", BF16, "dw_out").reshape(N_CHIPS, d // N_CHIPS, d)
    d_merged = mm(d_mix, w_o, "nt", F32, "d_merged")
    d_bsb, d_bfx, d_gs, d_gx = gate_bwd(d_merged, bsb, bfx, gf)
    p_br = mm(o_sb, d_bsb, "tn", BF16, "dw_branch_sb", tn=c_br, chunks=(1, 0),
              out_into=lax.empty((N_CHIPS, d_att, 2 * c_br), BF16))
    p_br = mm(o_fx, d_bfx, "tn", BF16, "dw_branch_fox", tn=c_br, chunks=(1, 1), out_into=p_br)
    d_osb = mm(d_bsb, g_br, "nt", BF16, "d_o_sb", tk=c_br, chunks=(1, 0))
    d_ofx = mm(d_bfx, g_br, "nt", BF16, "d_o_fox", tk=c_br, chunks=(1, 1))

    def reduce_start(tag, pieces, names):
        from_sibling = swap_halves("swap_halves_" + tag, pieces)
        sums = [pair_sum("pair_sum_" + t, p, q, core) for t, p, q in zip(names, pieces, from_sibling)]
        return scatter_start("scatter_start_" + tag, sums)

    def reduce_end(tag, started, names, after):
        send, recv, sums, lands, _ = started
        sums, lands = scatter_wait("scatter_wait_" + tag, sums, lands, send, recv, after)
        return join_halves("join_halves_" + tag, [chip_sum("chip_sum_" + t, sm, got, chip, core)
                                                  for t, sm, got in zip(names, sums, lands)])

    rest_names = ["branch", "out", "gate_up", "down"]
    p_gu, p_dn, q_gu, q_dn = sibling_wait("swap_big_wait", sw_arrs, sw_send, sw_recv, _swap_plan, [p_br])
    q_br, q_out = swap_halves("swap_halves_small", [p_br, p_out])
    rest_started = scatter_start("scatter_start_rest", [
        pair_sum("pair_sum_" + t, p, q, core)
        for t, p, q in zip(rest_names, [p_br, p_out, p_gu, p_dn], [q_br, q_out, q_gu, q_dn])])
    d_osb = d_osb + rest_started[4][0, 0].astype(BF16)
    dq_s, dk_s, dv_s = sb_bwd(qkv, d_osb, n_heads, 0)
    dq_f, dk_f, dv_f, dcq, dck = fox_bwd(qkv, cum_col, cum_row, o_fx, d_ofx, lse, n_heads, 3 * n_heads)
    d_cum = jnp.pad((dcq[:, :, 0] + dck[:, 0, :]).T, ((0, 0), (0, LANES - n_heads)))
    d_f, db_pad = cum_bwd(d_cum, gf, b_pad, 2 * d, n_heads)
    d_main = jnp.concatenate(
        [dq_s, dk_s, dv_s, dq_f, dk_f, dv_f, d_gs, d_gx, d_f, jnp.zeros((s, f_pad - LANES), BF16)], axis=1)
    p_in = regroup_dw_in(mm(u, d_main, "tn", BF16, "dw_in"), segments, c_in)

    in_started = reduce_start("w_in", [p_in], ["in"])
    du = mm(d_main, w_main, "nt", F32, "d_u", after=in_started[4])
    dx, dg_pre = in_bwd(dh1, du, x2, norm_mix_pre + in_started[4][0:1, 0:1])
    gr_br, gr_out, gr_gu, gr_dn = reduce_end("rest", rest_started, rest_names, dx)

    upd_bs = adam_update("adam_branch_sb", w_branch_sb, m_w_branch_sb, v_w_branch_sb, gr_br, 0)
    upd_bf = adam_update("adam_branch_fox", w_branch_fox, m_w_branch_fox, v_w_branch_fox, gr_br, 1)
    upd_o = adam_update("adam_out", w_out, m_w_out, v_w_out, gr_out, 0)

    def pack(rows):
        rows = [jnp.pad(r_, ((0, 0), (0, d - r_.shape[1]))) for r_ in rows]
        return jnp.concatenate(rows + [jnp.zeros((8 - len(rows), d), F32)], axis=0)

    sm_g, sm_d, sm_m, sm_v = small_allreduce_adam(
        pack([dg_pre, dg_post, dg_fpre, dg_fpost, db_pad]),
        pack([norm_mix_pre, norm_mix_post, norm_ffn_pre, norm_ffn_post, b_forget]),
        pack([m_norm_mix_pre, m_norm_mix_post, m_norm_ffn_pre, m_norm_ffn_post, m_b_forget]),
        pack([v_norm_mix_pre, v_norm_mix_post, v_norm_ffn_pre, v_norm_ffn_post, v_b_forget]))

    done = sm_d[0:1, 0:1] + sum(u_[1][0, 0:1, 0:1] for u_ in (upd_bs, upd_bf, upd_o))
    gr_in, = reduce_end("w_in", in_started, ["in"], done)
    upd_in_t = adam_update_transposed("adam_w_in", w_in_t, m_in_t, v_in_t, gr_in)
    upd_in = [jnp.transpose(a, (1, 2, 0)) for a in upd_in_t]
    upd_ga = adam_update("adam_gate", w_ffn_gate, m_w_ffn_gate, v_w_ffn_gate, gr_gu, 0, after=upd_in_t[1])
    upd_up = adam_update("adam_up", w_ffn_up, m_w_ffn_up, v_w_ffn_up, gr_gu, 1, after=upd_in_t[1])
    upd_dn = adam_update("adam_down", w_ffn_down, m_w_ffn_down, v_w_ffn_down, gr_dn, 0, after=upd_in_t[1])
    grads, deltas, new_ms, new_vs = zip(upd_in, upd_bs, upd_bf, upd_o, upd_ga, upd_up, upd_dn)

    def small(a):
        return [a[0:1], a[1:2], a[2:3], a[3:4], a[4:5, :n_heads]]

    def ordered(sm, bg):
        return [sm[0], sm[1], bg[0], sm[4], bg[1], bg[2], bg[3], sm[2], sm[3], bg[4], bg[5], bg[6]]

    loss = lax.psum(loss_part[0, 0], ("x", "y", "c"))
    return (loss, dx[None], *ordered(small(sm_g), grads), *ordered(small(sm_d), deltas),
            *ordered(small(sm_m), new_ms), *ordered(small(sm_v), new_vs))
```

```python
import functools

import jax
import jax.numpy as jnp
from jax import lax
from jax.experimental import pallas as pl
from jax.experimental.pallas import tpu as pltpu

F32 = jnp.float32
BF16 = jnp.bfloat16
MESH = pl.DeviceIdType.MESH

HEAD_DIM = 128
LANES = 128
ATT_TILE = 512
ROW_TILE = 256
N_CHIPS = 4
RMS_EPS = 1e-6
ADAM_LR = 0.001
ADAM_B1 = 0.9
ADAM_B2 = 0.999
ADAM_EPS = 1e-08
ADAM_WD = 0.01
ADAM_STEP = 10
NEG_BIG = -1e30
VMEM_LIMIT = 56 * 1024 * 1024
MM_VMEM_BUDGET = 40 * 1024 * 1024
ATT_STRIP = 512

NN = (((1,), (0,)), ((), ()))
NT = (((1,), (1,)), ((), ()))
TN = (((0,), (0,)), ((), ()))


def _tile(n, pref, align):
    best = None
    t = align
    while t <= min(n, pref):
        if n % t == 0:
            best = t
        t += align
    return n if best is None else best


def _params(*sem):
    return pltpu.CompilerParams(dimension_semantics=sem, vmem_limit_bytes=VMEM_LIMIT)


def _mm_tiles(m, n, k, a_bytes, b_bytes, out_bytes, tn, tk):
    tm = _tile(m, 2048, LANES)
    tk = tk or _tile(k, 512, LANES)

    def vmem(t):
        acc = 0 if out_bytes == 4 else tm * t * 4
        return acc + 2 * tm * t * out_bytes + 2 * (tm * tk * a_bytes + tk * t * b_bytes)

    if tn is None:
        fits = [t for t in range(LANES, min(n, 2048) + 1, LANES) if n % t == 0 and vmem(t) <= MM_VMEM_BUDGET]
        tn = max(fits) if fits else _tile(n, LANES, LANES)
    return tm, tn, tk


def mm(a, b, mode, out_dtype, name, *, tn=None, tk=None, b_win=None, chunks=None, out_into=None, after=None):
    n_per, blk0 = chunks if chunks else (1, 0)
    if mode == "nn":
        m, k = a.shape
        n = b.shape[0] * n_per * tn if chunks else (b_win[1] if b_win else b.shape[1])
    elif mode == "nt":
        m = a.shape[0]
        k = b.shape[0] * n_per * tk if chunks else a.shape[1]
        n = b.shape[-2]
    else:
        k, m = a.shape
        n = b.shape[1]
    in_place = jnp.dtype(out_dtype) == jnp.dtype(F32)
    tm, tn, tk = _mm_tiles(m, n, k, a.dtype.itemsize, b.dtype.itemsize, jnp.dtype(out_dtype).itemsize, tn, tk)
    assert m % tm == 0 and n % tn == 0 and k % tk == 0, (name, m, n, k, tm, tn, tk)
    j0 = 0
    if b_win:
        assert b_win[0] % tn == 0
        j0 = b_win[0] // tn
    nk = k // tk
    dims = {"nn": NN, "nt": NT, "tn": TN}[mode]

    def kern(a_ref, b_ref, *rest):
        o_ref, acc_ref = (rest[-1], rest[-1]) if in_place else (rest[-2], rest[-1])
        kk = pl.program_id(2)

        @pl.when(kk == 0)
        def _():
            acc_ref[...] = jnp.zeros_like(acc_ref)

        acc_ref[...] += lax.dot_general(a_ref[...].astype(BF16), b_ref[...].astype(BF16), dims,
                                        preferred_element_type=F32)

        if not in_place:
            @pl.when(kk == nk - 1)
            def _():
                o_ref[...] = acc_ref[...].astype(o_ref.dtype)

    out_spec = pl.BlockSpec((tm, tn), lambda i, j, kk: (i, j))
    out_shape = jax.ShapeDtypeStruct((m, n), out_dtype)
    if mode == "nn":
        a_spec = pl.BlockSpec((tm, tk), lambda i, j, kk: (i, kk))
        if chunks:
            b_spec = pl.BlockSpec((None, tk, tn), lambda i, j, kk: (j // n_per, kk, blk0 + j % n_per))
        else:
            b_spec = pl.BlockSpec((tk, tn), lambda i, j, kk: (kk, j + j0))
    elif mode == "nt":
        a_spec = pl.BlockSpec((tm, tk), lambda i, j, kk: (i, kk))
        if chunks:
            b_spec = pl.BlockSpec((None, tn, tk), lambda i, j, kk: (kk // n_per, j, blk0 + kk % n_per))
        else:
            b_spec = pl.BlockSpec((tn, tk), lambda i, j, kk: (j, kk))
    else:
        a_spec = pl.BlockSpec((tk, tm), lambda i, j, kk: (kk, i))
        b_spec = pl.BlockSpec((tk, tn), lambda i, j, kk: (kk, j))
        if chunks:
            out_spec = pl.BlockSpec((None, tm, tn), lambda i, j, kk: (j // n_per, i, blk0 + j % n_per))
    in_specs, operands, aliases = [a_spec, b_spec], [a, b], {}
    if chunks and mode == "tn":
        assert out_into is not None
        out_shape = jax.ShapeDtypeStruct(out_into.shape, out_dtype)
        in_specs.append(pl.BlockSpec(memory_space=pl.ANY))
        operands.append(out_into)
        aliases = {2: 0}
    if after is not None:
        in_specs.append(pl.BlockSpec(memory_space=pl.ANY))
        operands.append(after)
    return pl.pallas_call(
        kern, name=name, grid=(m // tm, n // tn, nk),
        in_specs=in_specs, out_specs=out_spec, out_shape=out_shape,
        scratch_shapes=[] if in_place else [pltpu.VMEM((tm, tn), F32)], input_output_aliases=aliases,
        compiler_params=_params("parallel", "parallel", "arbitrary"),
    )(*operands)


def _rstd(v):
    return lax.rsqrt(jnp.mean(v * v, axis=-1, keepdims=True) + RMS_EPS)


def _norm_bwd(v, g, dy):
    r = _rstd(v)
    vh = v * r
    dyg = dy * g
    dv = r * (dyg - vh * jnp.mean(dyg * vh, axis=-1, keepdims=True))
    return dv, jnp.sum(dy * vh, axis=0, keepdims=True)


def _row_call(kern, name, ins, outs, s, d):
    tr = _tile(s, ROW_TILE, 16)

    def spec(shape, is_row):
        if is_row:
            return pl.BlockSpec((tr, shape[1]), lambda i: (i, 0))
        return pl.BlockSpec(shape, lambda i: (0, 0))

    return pl.pallas_call(
        kern, name=name, grid=(s // tr,),
        in_specs=[spec(a.shape, r) for a, r in ins],
        out_specs=[spec(sh, r) for sh, _, r in outs],
        out_shape=[jax.ShapeDtypeStruct(sh, dt) for sh, dt, _ in outs],
        compiler_params=_params("arbitrary"),
    )(*[a for a, _ in ins])


def norm_in(x, g):
    s, d = x.shape

    def kern(x_ref, g_ref, u_ref):
        v = x_ref[...]
        u_ref[...] = (v * _rstd(v) * g_ref[...]).astype(BF16)

    return _row_call(kern, "norm_in", [(x, True), (g, False)], [((s, d), BF16, True)], s, d)[0]


def mid_fwd(x, mix, g_post, g_fpre):
    s, d = x.shape

    def kern(x_ref, mix_ref, gp_ref, gf_ref, h1_ref, u2_ref):
        mixv = mix_ref[...]
        h1 = x_ref[...] + mixv * _rstd(mixv) * gp_ref[...]
        h1_ref[...] = h1
        u2_ref[...] = (h1 * _rstd(h1) * gf_ref[...]).astype(BF16)

    return _row_call(kern, "mid_fwd", [(x, True), (mix, True), (g_post, False), (g_fpre, False)],
                     [((s, d), F32, True), ((s, d), BF16, True)], s, d)


def loss_head(h1, ff, g_fpost, target):
    s, d = h1.shape

    def kern(h1_ref, ff_ref, g_ref, t_ref, dy_ref, dff_ref, dg_ref, loss_ref):
        @pl.when(pl.program_id(0) == 0)
        def _():
            dg_ref[...] = jnp.zeros_like(dg_ref)
            loss_ref[...] = jnp.zeros_like(loss_ref)

        ffv = ff_ref[...]
        g = g_ref[...]
        y = h1_ref[...] + ffv * _rstd(ffv) * g
        diff = y - t_ref[...]
        row_loss = jnp.mean(diff * diff, axis=-1, keepdims=True)
        loss_ref[...] += 0.5 * jnp.sum(row_loss, axis=0, keepdims=True)
        dy = diff / d
        dy_ref[...] = dy
        dff, dg = _norm_bwd(ffv, g, dy)
        dff_ref[...] = dff.astype(BF16)
        dg_ref[...] += dg

    return _row_call(kern, "loss_head",
                     [(h1, True), (ff, True), (g_fpost, False), (target, True)],
                     [((s, d), F32, True), ((s, d), BF16, True), ((1, d), F32, False), ((1, 1), F32, False)], s, d)


def mid_bwd(dy, du2, h1, mix, g_fpre, g_post):
    s, d = dy.shape

    def kern(dy_ref, du2_ref, h1_ref, mix_ref, gf_ref, gp_ref, dh1_ref, dmix_ref, dgf_ref, dgp_ref):
        @pl.when(pl.program_id(0) == 0)
        def _():
            dgf_ref[...] = jnp.zeros_like(dgf_ref)
            dgp_ref[...] = jnp.zeros_like(dgp_ref)

        dh, dgf = _norm_bwd(h1_ref[...], gf_ref[...], du2_ref[...])
        dh1 = dy_ref[...] + dh
        dh1_ref[...] = dh1
        dmix, dgp = _norm_bwd(mix_ref[...], gp_ref[...], dh1)
        dmix_ref[...] = dmix.astype(BF16)
        dgf_ref[...] += dgf
        dgp_ref[...] += dgp

    return _row_call(kern, "mid_bwd",
                     [(dy, True), (du2, True), (h1, True), (mix, True), (g_fpre, False), (g_post, False)],
                     [((s, d), F32, True), ((s, d), BF16, True), ((1, d), F32, False), ((1, d), F32, False)], s, d)


def in_bwd(dh1, du, x, g_pre):
    s, d = x.shape

    def kern(dh1_ref, du_ref, x_ref, g_ref, dx_ref, dg_ref):
        @pl.when(pl.program_id(0) == 0)
        def _():
            dg_ref[...] = jnp.zeros_like(dg_ref)

        dxn, dg = _norm_bwd(x_ref[...], g_ref[...], du_ref[...])
        dx_ref[...] = dh1_ref[...] + dxn
        dg_ref[...] += dg

    return _row_call(kern, "in_bwd", [(dh1, True), (du, True), (x, True), (g_pre, False)],
                     [((s, d), F32, True), ((1, d), F32, False)], s, d)


def _sigmoid(v):
    return 1.0 / (1.0 + jnp.exp(-v))


def gate_fwd(bsb, bfx, gf):
    s, d = bsb.shape
    tr, tc = _tile(s, 256, 16), _tile(d, 512, LANES)
    nc = d // tc

    def kern(bsb_ref, bfx_ref, gs_ref, gx_ref, o_ref):
        o_ref[...] = (_sigmoid(gs_ref[...]) * bsb_ref[...] + _sigmoid(gx_ref[...]) * bfx_ref[...]).astype(BF16)

    blk = pl.BlockSpec((tr, tc), lambda i, j: (i, j))
    return pl.pallas_call(
        kern, name="gate_fwd", grid=(s // tr, nc),
        in_specs=[blk, blk, blk, pl.BlockSpec((tr, tc), lambda i, j: (i, j + nc))],
        out_specs=blk, out_shape=jax.ShapeDtypeStruct((s, d), BF16),
        compiler_params=_params("parallel", "parallel"),
    )(bsb, bfx, gf, gf)


def gate_bwd(dmerged, bsb, bfx, gf):
    s, d = bsb.shape
    tr, tc = _tile(s, 256, 16), _tile(d, 512, LANES)
    nc = d // tc

    def kern(dm_ref, bsb_ref, bfx_ref, gs_ref, gx_ref, dbs_ref, dbx_ref, dgs_ref, dgx_ref):
        dm = dm_ref[...]
        ss = _sigmoid(gs_ref[...])
        sx = _sigmoid(gx_ref[...])
        dbs_ref[...] = (dm * ss).astype(BF16)
        dbx_ref[...] = (dm * sx).astype(BF16)
        dgs_ref[...] = (dm * bsb_ref[...] * ss * (1.0 - ss)).astype(BF16)
        dgx_ref[...] = (dm * bfx_ref[...] * sx * (1.0 - sx)).astype(BF16)

    blk = pl.BlockSpec((tr, tc), lambda i, j: (i, j))
    out = jax.ShapeDtypeStruct((s, d), BF16)
    return pl.pallas_call(
        kern, name="gate_bwd", grid=(s // tr, nc),
        in_specs=[blk, blk, blk, blk, pl.BlockSpec((tr, tc), lambda i, j: (i, j + nc))],
        out_specs=[blk, blk, blk, blk], out_shape=[out, out, out, out],
        compiler_params=_params("parallel", "parallel"),
    )(dmerged, bsb, bfx, gf, gf)


FFN_ROWS = 1024


def ffn_up_fused(u2, w_gu, cw):
    s, d = u2.shape
    nc = w_gu.shape[0]
    tm, tk = _tile(s, FFN_ROWS, LANES), _tile(d, 512, LANES)
    nk = d // tk

    def kern(a_ref, b_ref, gu_ref, act_ref):
        kk = pl.program_id(2)

        @pl.when(kk == 0)
        def _():
            gu_ref[...] = jnp.zeros_like(gu_ref)

        gu_ref[...] += jnp.dot(a_ref[...], b_ref[...], preferred_element_type=F32)

        @pl.when(kk == nk - 1)
        def _():
            g = gu_ref[:, :cw]
            act_ref[...] = (g * _sigmoid(g) * gu_ref[:, cw:]).astype(BF16)

    return pl.pallas_call(
        kern, name="ffn_gate_up", grid=(s // tm, nc, nk),
        in_specs=[pl.BlockSpec((tm, tk), lambda i, j, kk: (i, kk)),
                  pl.BlockSpec((None, tk, 2 * cw), lambda i, j, kk: (j, kk, 0))],
        out_specs=[pl.BlockSpec((tm, 2 * cw), lambda i, j, kk: (i, j)), pl.BlockSpec((tm, cw), lambda i, j, kk: (i, j))],
        out_shape=[jax.ShapeDtypeStruct((s, nc * 2 * cw), F32), jax.ShapeDtypeStruct((s, nc * cw), BF16)],
        compiler_params=_params("parallel", "parallel", "arbitrary"),
    )(u2, w_gu)


def ffn_down_bwd_fused(d_ff, w_dn, gu, cw):
    s, d = d_ff.shape
    nc = gu.shape[1] // (2 * cw)
    tm, tk = _tile(s, FFN_ROWS, LANES), _tile(d, 512, LANES)
    nk = d // tk

    def kern(a_ref, b_ref, gu_ref, o_ref, acc_ref):
        kk = pl.program_id(2)

        @pl.when(kk == 0)
        def _():
            acc_ref[...] = jnp.zeros_like(acc_ref)

        acc_ref[...] += lax.dot_general(a_ref[...], b_ref[...], NT, preferred_element_type=F32)

        @pl.when(kk == nk - 1)
        def _():
            da = acc_ref[...]
            g = gu_ref[:, :cw]
            sg = _sigmoid(g)
            o_ref[:, :cw] = (da * gu_ref[:, cw:] * (sg * (1.0 + g * (1.0 - sg)))).astype(BF16)
            o_ref[:, cw:] = (da * (g * sg)).astype(BF16)

    return pl.pallas_call(
        kern, name="d_act_swiglu", grid=(s // tm, nc, nk),
        in_specs=[pl.BlockSpec((tm, tk), lambda i, j, kk: (i, kk)),
                  pl.BlockSpec((cw, tk), lambda i, j, kk: (j, kk)),
                  pl.BlockSpec((tm, 2 * cw), lambda i, j, kk: (i, j))],
        out_specs=pl.BlockSpec((tm, 2 * cw), lambda i, j, kk: (i, j)),
        out_shape=jax.ShapeDtypeStruct(gu.shape, BF16),
        scratch_shapes=[pltpu.VMEM((tm, cw), F32)],
        compiler_params=_params("parallel", "parallel", "arbitrary"),
    )(d_ff, w_dn, gu)


def _split3(v):
    hi = v.astype(BF16)
    r = v - hi.astype(F32)
    mid = r.astype(BF16)
    lo = (r - mid.astype(F32)).astype(BF16)
    return hi, mid, lo


def _dot3_right(v, ones):
    hi, mid, lo = _split3(v)
    d = lambda p: jnp.dot(p, ones, preferred_element_type=F32)
    return (d(lo) + d(mid)) + d(hi)


def _dot3_left(ones, v):
    hi, mid, lo = _split3(v)
    d = lambda p: jnp.dot(ones, p, preferred_element_type=F32)
    return (d(lo) + d(mid)) + d(hi)


def _split2(v):
    hi = v.astype(BF16)
    return hi, (v - hi.astype(F32)).astype(BF16)


def _dot2_right(v, ones):
    hi, lo = _split2(v)
    return jnp.dot(lo, ones, preferred_element_type=F32) + jnp.dot(hi, ones, preferred_element_type=F32)


def _log1p_exp_neg_abs(v):
    return jnp.log(1.0 + jnp.exp(-jnp.abs(v)))


def _mask01(cond):
    return jnp.where(cond, 1.0, 0.0).astype(BF16)


def _iota2(t):
    return (lax.broadcasted_iota(jnp.int32, (t, t), 0), lax.broadcasted_iota(jnp.int32, (t, t), 1))


def cum_fwd(gf, b_pad, f_col0):
    s = gf.shape[0]
    t = _tile(s, ATT_TILE, LANES)
    fb = f_col0 // LANES

    def kern(f_ref, b_ref, cum_ref, carry_ref):
        @pl.when(pl.program_id(0) == 0)
        def _():
            carry_ref[...] = jnp.zeros_like(carry_ref)

        v = f_ref[...] + b_ref[...]
        lf = jnp.minimum(v, 0.0) - _log1p_exp_neg_abs(v)
        row, col = _iota2(t)
        cum = _dot3_left(_mask01(col <= row), lf) + carry_ref[...]
        cum_ref[...] = cum
        carry_ref[...] = cum[t - 1:t, :]

    return pl.pallas_call(
        kern, name="cum_fwd", grid=(s // t,),
        in_specs=[pl.BlockSpec((t, LANES), lambda i: (i, fb)), pl.BlockSpec((1, LANES), lambda i: (0, 0))],
        out_specs=pl.BlockSpec((t, LANES), lambda i: (i, 0)),
        out_shape=jax.ShapeDtypeStruct((s, LANES), F32),
        scratch_shapes=[pltpu.VMEM((1, LANES), F32)],
        compiler_params=_params("arbitrary"),
    )(gf, b_pad)


def cum_bwd(dcum, gf, b_pad, f_col0, n_heads):
    s = gf.shape[0]
    t = _tile(s, ATT_TILE, LANES)
    nb = s // t
    fb = f_col0 // LANES

    def kern(dc_ref, f_ref, b_ref, df_ref, db_ref, carry_ref):
        @pl.when(pl.program_id(0) == 0)
        def _():
            carry_ref[...] = jnp.zeros_like(carry_ref)
            db_ref[...] = jnp.zeros_like(db_ref)

        row, col = _iota2(t)
        dlf = _dot3_left(_mask01(col >= row), dc_ref[...]) + carry_ref[...]
        carry_ref[...] = dlf[0:1, :]
        v = f_ref[...] + b_ref[...]
        sig_neg = jnp.exp(-jnp.maximum(v, 0.0) - _log1p_exp_neg_abs(v))
        lane = lax.broadcasted_iota(jnp.int32, (t, LANES), 1)
        df = jnp.where(lane < n_heads, dlf * sig_neg, 0.0)
        df_ref[...] = df.astype(BF16)
        db_ref[...] += jnp.sum(df, axis=0, keepdims=True)

    return pl.pallas_call(
        kern, name="cum_bwd", grid=(nb,),
        in_specs=[pl.BlockSpec((t, LANES), lambda i: (nb - 1 - i, 0)),
                  pl.BlockSpec((t, LANES), lambda i: (nb - 1 - i, fb)),
                  pl.BlockSpec((1, LANES), lambda i: (0, 0))],
        out_specs=[pl.BlockSpec((t, LANES), lambda i: (nb - 1 - i, 0)), pl.BlockSpec((1, LANES), lambda i: (0, 0))],
        out_shape=[jax.ShapeDtypeStruct((s, LANES), BF16), jax.ShapeDtypeStruct((1, LANES), F32)],
        scratch_shapes=[pltpu.VMEM((1, LANES), F32)],
        compiler_params=_params("arbitrary"),
    )(dcum, gf, b_pad)


def _qkv_specs(s, t, n_heads, base):
    return [pl.BlockSpec((t, HEAD_DIM), lambda h, i: (i, base + h)),
            pl.BlockSpec((s, HEAD_DIM), lambda h, i: (0, base + n_heads + h)),
            pl.BlockSpec((s, HEAD_DIM), lambda h, i: (0, base + 2 * n_heads + h))]


def _strips(t):
    sr = _tile(t, ATT_STRIP, 8)
    return sr, t // sr, [slice(si * sr, (si + 1) * sr) for si in range(t // sr)]


def _key_minus_row(sr, t):
    return lax.broadcasted_iota(jnp.int32, (sr, t), 1) - lax.broadcasted_iota(jnp.int32, (sr, t), 0)


def _keep(valid, v):
    return v if valid is None else jnp.where(valid, v, 0.0)


def _sb_scores(q, k, diff, lim):
    z = lax.dot_general(q, k, NT, preferred_element_type=F32) * (HEAD_DIM ** -0.5)
    valid = None if lim is None else diff < lim
    l1p = _log1p_exp_neg_abs(z)
    return z, valid, l1p, _keep(valid, -jnp.maximum(z, 0.0) - l1p)


def sb_fwd(qkv, n_heads, base):
    s = qkv.shape[0]
    t = _tile(s, ATT_TILE, LANES)
    sr, ns, strips = _strips(t)

    def kern(q_ref, k_ref, v_ref, o_ref):
        i = pl.program_id(1)
        row, col = _iota2(t)
        after = _mask01(row > col)
        diff = _key_minus_row(sr, t)
        qs = [q_ref[sl, :] for sl in strips]

        def tile(j, carry, diagonal):
            runs, accs = carry
            off = pl.multiple_of(j * t, t)
            k = k_ref[pl.ds(off, t), :]
            v = v_ref[pl.ds(off, t), :]
            new_runs, new_accs = [], []
            for si in range(ns):
                z, valid, l1p, log_keep = _sb_scores(qs[si], k, diff, si * sr if diagonal else None)
                between = _dot2_right(log_keep, after) + runs[si]
                w = _keep(valid, jnp.exp(jnp.minimum(z, 0.0) - l1p + between))
                new_accs.append(accs[si] + jnp.dot(w.astype(BF16), v, preferred_element_type=F32))
                new_runs.append(runs[si] + jnp.sum(log_keep, axis=1, keepdims=True))
            return tuple(new_runs), tuple(new_accs)

        init = (tuple(jnp.zeros((sr, 1), F32) for _ in strips), tuple(jnp.zeros((sr, HEAD_DIM), F32) for _ in strips))
        _, accs = lax.fori_loop(0, i, lambda jj, c: tile(i - 1 - jj, c, False), tile(i, init, True))
        for sl, acc in zip(strips, accs):
            o_ref[sl, :] = acc.astype(o_ref.dtype)

    return pl.pallas_call(
        kern, name="sb_fwd", grid=(n_heads, s // t),
        in_specs=_qkv_specs(s, t, n_heads, base),
        out_specs=pl.BlockSpec((t, HEAD_DIM), lambda h, i: (i, h)),
        out_shape=jax.ShapeDtypeStruct((s, n_heads * HEAD_DIM), BF16),
        compiler_params=_params("parallel", "arbitrary"),
    )(qkv, qkv, qkv)


def sb_bwd(qkv, d_o, n_heads, base):
    s = qkv.shape[0]
    t = _tile(s, ATT_TILE, LANES)
    nq = s // t
    sr, ns, strips = _strips(t)
    scale = HEAD_DIM ** -0.5

    def kern(q_ref, k_ref, v_ref, do_ref, dq_ref, dk_ref, dv_ref, dk_acc, dv_acc, run_ref):
        i = pl.program_id(1)

        @pl.when(i == 0)
        def _():
            dk_acc[...] = jnp.zeros_like(dk_acc)
            dv_acc[...] = jnp.zeros_like(dv_acc)

        row, col = _iota2(t)
        after = _mask01(row > col)
        before = _mask01(row < col)
        diff = _key_minus_row(sr, t)
        qs = [q_ref[sl, :] for sl in strips]
        dos = [do_ref[sl, :] for sl in strips]

        def sweep1(j, runs, diagonal):
            k = k_ref[pl.ds(pl.multiple_of(j * t, t), t), :]
            new_runs = []
            for si, sl in enumerate(strips):
                _, _, _, log_keep = _sb_scores(qs[si], k, diff, si * sr if diagonal else None)
                run_ref[j, sl, :] = runs[si]
                new_runs.append(runs[si] + jnp.sum(log_keep, axis=1, keepdims=True))
            return tuple(new_runs)

        lax.fori_loop(0, i, lambda jj, c: sweep1(i - 1 - jj, c, False),
                      sweep1(i, tuple(jnp.zeros((sr, 1), F32) for _ in strips), True))

        def sweep2(j, carry, diagonal):
            run_es, dqs = carry
            off = pl.multiple_of(j * t, t)
            k = k_ref[pl.ds(off, t), :]
            v = v_ref[pl.ds(off, t), :]
            new_es, new_dqs = [], []
            dk_t = jnp.zeros((t, HEAD_DIM), F32)
            dv_t = jnp.zeros((t, HEAD_DIM), F32)
            for si, sl in enumerate(strips):
                z, valid, l1p, log_keep = _sb_scores(qs[si], k, diff, si * sr if diagonal else None)
                between = _dot2_right(log_keep, after) + run_ref[j, sl, :]
                w = _keep(valid, jnp.exp(jnp.minimum(z, 0.0) - l1p + between))
                dw = lax.dot_general(dos[si], v, NT, preferred_element_type=F32)
                e = dw * w
                e_before = _dot2_right(e, before) + run_es[si]
                keep = jnp.exp(log_keep)
                dz = _keep(valid, e * keep - e_before * (1.0 - keep)) * scale
                dzb = dz.astype(BF16)
                new_dqs.append(dqs[si] + jnp.dot(dzb, k, preferred_element_type=F32))
                dk_t = dk_t + lax.dot_general(dzb, qs[si], TN, preferred_element_type=F32)
                dv_t = dv_t + lax.dot_general(w.astype(BF16), dos[si], TN, preferred_element_type=F32)
                new_es.append(run_es[si] + jnp.sum(e, axis=1, keepdims=True))
            dk_acc[pl.ds(off, t), :] += dk_t
            dv_acc[pl.ds(off, t), :] += dv_t
            return tuple(new_es), tuple(new_dqs)

        init = (tuple(jnp.zeros((sr, 1), F32) for _ in strips), tuple(jnp.zeros((sr, HEAD_DIM), F32) for _ in strips))
        _, dqs = sweep2(i, lax.fori_loop(0, i, lambda j, c: sweep2(j, c, False), init), True)
        for sl, dq in zip(strips, dqs):
            dq_ref[sl, :] = dq.astype(BF16)

        @pl.when(i == nq - 1)
        def _():
            dk_ref[...] = dk_acc[...].astype(BF16)
            dv_ref[...] = dv_acc[...].astype(BF16)

    out = jax.ShapeDtypeStruct((s, n_heads * HEAD_DIM), BF16)
    head_blk = pl.BlockSpec((s, HEAD_DIM), lambda h, i: (0, h))
    tile_blk = pl.BlockSpec((t, HEAD_DIM), lambda h, i: (i, h))
    return pl.pallas_call(
        kern, name="sb_bwd", grid=(n_heads, nq),
        in_specs=_qkv_specs(s, t, n_heads, base) + [tile_blk],
        out_specs=[tile_blk, head_blk, head_blk],
        out_shape=[out, out, out],
        scratch_shapes=[pltpu.VMEM((s, HEAD_DIM), F32), pltpu.VMEM((s, HEAD_DIM), F32), pltpu.VMEM((nq, t, 1), F32)],
        compiler_params=_params("parallel", "arbitrary"),
    )(qkv, qkv, qkv, d_o)


def _fox_scores(q, k, cq, ck, diff, lim):
    sc = lax.dot_general(q, k, NT, preferred_element_type=F32) * (HEAD_DIM ** -0.5)
    sc = sc + cq - ck
    if lim is None:
        return sc, None
    valid = diff < lim
    return jnp.where(valid, sc, NEG_BIG), valid


def fox_fwd(qkv, cum_col, cum_row, n_heads, base):
    s = qkv.shape[0]
    t = _tile(s, ATT_TILE, LANES)
    sr, ns, strips = _strips(t)

    def kern(q_ref, k_ref, v_ref, cq_ref, ck_ref, o_ref, lse_ref):
        i = pl.program_id(1)
        diff = _key_minus_row(sr, t)
        qs = [q_ref[sl, :] for sl in strips]
        cqs = [cq_ref[0, sl, :] for sl in strips]

        def tile(j, carry, diagonal):
            off = pl.multiple_of(j * t, t)
            k = k_ref[pl.ds(off, t), :]
            v = v_ref[pl.ds(off, t), :]
            ck = ck_ref[0, :, pl.ds(off, t)]
            out = []
            for si in range(ns):
                m, l, acc = carry[si]
                sc, _ = _fox_scores(qs[si], k, cqs[si], ck, diff, si * sr + 1 if diagonal else None)
                m_new = jnp.maximum(m, jnp.max(sc, axis=1, keepdims=True))
                p = jnp.exp(sc - m_new)
                alpha = jnp.exp(m - m_new)
                l = alpha * l + jnp.sum(p, axis=1, keepdims=True)
                acc = alpha * acc + jnp.dot(p.astype(BF16), v, preferred_element_type=F32)
                out.append((m_new, l, acc))
            return tuple(out)

        init = tuple((jnp.full((sr, 1), NEG_BIG, F32), jnp.zeros((sr, 1), F32), jnp.zeros((sr, HEAD_DIM), F32))
                     for _ in strips)
        res = tile(i, lax.fori_loop(0, i, lambda j, c: tile(j, c, False), init), True)
        for sl, (m, l, acc) in zip(strips, res):
            o_ref[sl, :] = acc / l
            lse_ref[0, sl, :] = m + jnp.log(l)

    col_blk = pl.BlockSpec((1, t, 1), lambda h, i: (h, i, 0))
    return pl.pallas_call(
        kern, name="fox_fwd", grid=(n_heads, s // t),
        in_specs=_qkv_specs(s, t, n_heads, base) + [col_blk, pl.BlockSpec((1, 1, s), lambda h, i: (h, 0, 0))],
        out_specs=[pl.BlockSpec((t, HEAD_DIM), lambda h, i: (i, h)), col_blk],
        out_shape=[jax.ShapeDtypeStruct((s, n_heads * HEAD_DIM), F32), jax.ShapeDtypeStruct((n_heads, s, 1), F32)],
        compiler_params=_params("parallel", "arbitrary"),
    )(qkv, qkv, qkv, cum_col, cum_row)


def fox_bwd(qkv, cum_col, cum_row, o, d_o, lse, n_heads, base):
    s = qkv.shape[0]
    t = _tile(s, ATT_TILE, LANES)
    nq = s // t
    sr, ns, strips = _strips(t)
    scale = HEAD_DIM ** -0.5

    def kern(q_ref, k_ref, v_ref, cq_ref, ck_ref, o_ref, do_ref, lse_ref,
             dq_ref, dk_ref, dv_ref, dcq_ref, dck_ref, dk_acc, dv_acc, dck_acc):
        i = pl.program_id(1)

        @pl.when(i == 0)
        def _():
            dk_acc[...] = jnp.zeros_like(dk_acc)
            dv_acc[...] = jnp.zeros_like(dv_acc)
            dck_acc[...] = jnp.zeros_like(dck_acc)

        diff = _key_minus_row(sr, t)
        qs = [q_ref[sl, :] for sl in strips]
        dos = [do_ref[sl, :] for sl in strips]
        cqs = [cq_ref[0, sl, :] for sl in strips]
        lses = [lse_ref[0, sl, :] for sl in strips]
        deltas = [jnp.sum(dos[si].astype(F32) * o_ref[sl, :], axis=1, keepdims=True) for si, sl in enumerate(strips)]

        def tile(j, carry, diagonal):
            off = pl.multiple_of(j * t, t)
            k = k_ref[pl.ds(off, t), :]
            v = v_ref[pl.ds(off, t), :]
            ck = ck_ref[0, :, pl.ds(off, t)]
            out = []
            dk_t = jnp.zeros((t, HEAD_DIM), F32)
            dv_t = jnp.zeros((t, HEAD_DIM), F32)
            dck_t = jnp.zeros((1, t), F32)
            for si in range(ns):
                dq, dcq = carry[si]
                sc, valid = _fox_scores(qs[si], k, cqs[si], ck, diff, si * sr + 1 if diagonal else None)
                p = _keep(valid, jnp.exp(sc - lses[si]))
                dp = lax.dot_general(dos[si], v, NT, preferred_element_type=F32)
                ds = p * (dp - deltas[si])
                dsb = (ds * scale).astype(BF16)
                dq = dq + jnp.dot(dsb, k, preferred_element_type=F32)
                dk_t = dk_t + lax.dot_general(dsb, qs[si], TN, preferred_element_type=F32)
                dv_t = dv_t + lax.dot_general(p.astype(BF16), dos[si], TN, preferred_element_type=F32)
                dck_t = dck_t + jnp.sum(ds, axis=0, keepdims=True)
                out.append((dq, dcq + jnp.sum(ds, axis=1, keepdims=True)))
            dk_acc[pl.ds(off, t), :] += dk_t
            dv_acc[pl.ds(off, t), :] += dv_t
            dck_acc[:, pl.ds(off, t)] -= dck_t
            return tuple(out)

        init = tuple((jnp.zeros((sr, HEAD_DIM), F32), jnp.zeros((sr, 1), F32)) for _ in strips)
        res = tile(i, lax.fori_loop(0, i, lambda j, c: tile(j, c, False), init), True)
        for sl, (dq, dcq) in zip(strips, res):
            dq_ref[sl, :] = dq.astype(BF16)
            dcq_ref[0, sl, :] = dcq

        @pl.when(i == nq - 1)
        def _():
            dk_ref[...] = dk_acc[...].astype(BF16)
            dv_ref[...] = dv_acc[...].astype(BF16)
            dck_ref[0] = dck_acc[...]

    out = jax.ShapeDtypeStruct((s, n_heads * HEAD_DIM), BF16)
    head_blk = pl.BlockSpec((s, HEAD_DIM), lambda h, i: (0, h))
    tile_blk = pl.BlockSpec((t, HEAD_DIM), lambda h, i: (i, h))
    col_blk = pl.BlockSpec((1, t, 1), lambda h, i: (h, i, 0))
    row_blk = pl.BlockSpec((1, 1, s), lambda h, i: (h, 0, 0))
    return pl.pallas_call(
        kern, name="fox_bwd", grid=(n_heads, nq),
        in_specs=_qkv_specs(s, t, n_heads, base) + [col_blk, row_blk, tile_blk, tile_blk, col_blk],
        out_specs=[tile_blk, head_blk, head_blk, col_blk, row_blk],
        out_shape=[out, out, out, jax.ShapeDtypeStruct((n_heads, s, 1), F32),
                   jax.ShapeDtypeStruct((n_heads, 1, s), F32)],
        scratch_shapes=[pltpu.VMEM((s, HEAD_DIM), F32), pltpu.VMEM((s, HEAD_DIM), F32), pltpu.VMEM((1, s), F32)],
        compiler_params=_params("parallel", "arbitrary"),
    )(qkv, qkv, qkv, cum_col, cum_row, o, d_o, lse)


def _place():
    x, y, c = lax.axis_index("x"), lax.axis_index("y"), lax.axis_index("c")
    other_chips = [(1 - x, y), (x, 1 - y), (1 - x, 1 - y)]
    return x, y, c, other_chips


ANY = pl.BlockSpec(memory_space=pl.ANY)


def _remote(src, dst, send_sem, recv_sem, dev):
    return pltpu.make_async_remote_copy(src_ref=src, dst_ref=dst, send_sem=send_sem, recv_sem=recv_sem,
                                        device_id=dev, device_id_type=MESH)


def place_transposed(name, w_t, chip):
    c, _, r = w_t.shape
    tc = LANES

    def kern(chip_ref, w_ref, o_ref):
        o_ref[...] = w_ref[:, 0, :].T.astype(BF16)

    return pl.pallas_call(
        kern, name=name,
        grid_spec=pltpu.PrefetchScalarGridSpec(
            num_scalar_prefetch=1, grid=(pl.cdiv(c, tc),),
            in_specs=[pl.BlockSpec((tc, 1, r), lambda j, chip_ref: (j, 0, 0))],
            out_specs=pl.BlockSpec((None, r, tc), lambda j, chip_ref: (chip_ref[0], 0, j))),
        out_shape=jax.ShapeDtypeStruct((N_CHIPS, r, c), BF16),
        compiler_params=_params("parallel"),
    )(chip, w_t)


def adam_update_transposed(name, w_t, m_t, v_t, g_buf):
    c, _, r = w_t.shape
    tc = LANES

    def kern(w_ref, m_ref, v_ref, g_ref, go_ref, dl_ref, nm_ref, nv_ref):
        g = g_ref[...].T
        delta, nm, nv = _adam(w_ref[:, 0, :], g, m_ref[:, 0, :], v_ref[:, 0, :])
        go_ref[:, 0, :] = g
        dl_ref[:, 0, :] = delta
        nm_ref[:, 0, :] = nm
        nv_ref[:, 0, :] = nv

    blk = pl.BlockSpec((tc, 1, r), lambda j: (j, 0, 0))
    out = jax.ShapeDtypeStruct((c, 1, r), F32)
    return pl.pallas_call(
        kern, name=name, grid=(pl.cdiv(c, tc),),
        in_specs=[blk, blk, blk, pl.BlockSpec((r, tc), lambda j: (0, j))],
        out_specs=[blk] * 4, out_shape=[out] * 4, compiler_params=_params("parallel"),
    )(w_t, m_t, v_t, g_buf)


def cast_place(name, ws, chip):
    r = ws[0].shape[1]
    cs = [w.shape[2] for w in ws]
    tr = _tile(r, 256, 16)

    def kern(chip_ref, *refs):
        o_ref = refs[-1]
        off = 0
        for w_ref, c in zip(refs[:-1], cs):
            o_ref[:, off:off + c] = w_ref[...].astype(BF16)
            off += c

    return pl.pallas_call(
        kern, name=name,
        grid_spec=pltpu.PrefetchScalarGridSpec(
            num_scalar_prefetch=1, grid=(r // tr,),
            in_specs=[pl.BlockSpec((None, tr, c), lambda i, chip_ref: (0, i, 0)) for c in cs],
            out_specs=pl.BlockSpec((None, tr, sum(cs)), lambda i, chip_ref: (chip_ref[0], i, 0))),
        out_shape=jax.ShapeDtypeStruct((N_CHIPS, r, sum(cs)), BF16),
        compiler_params=_params("parallel"),
    )(chip, *ws)


HBM = pl.BlockSpec(memory_space=pltpu.HBM)
SEM = pl.BlockSpec(memory_space=pltpu.SEMAPHORE)
SPLIT = pltpu.CompilerParams(has_side_effects=pltpu.SideEffectType.DATAFLOW_SIDE_EFFECTING)


def _in_hbm(a):
    return pltpu.with_memory_space_constraint(a, pltpu.HBM)


def _slab_rows(ref, k, core):
    half = ref.shape[1] // 2
    return ref.at[k, pl.ds(pl.multiple_of(core * half, 16), half)]


def gather_start(name, bufs):
    n = len(bufs)

    def body(*refs):
        ins, send, recv, token = refs[:n], refs[n], refs[n + 1], refs[-1]
        x, y, c, chips = _place()
        me = 2 * x + y
        for a in range(n):
            for j in range(3):
                rows = _slab_rows(ins[a], me, c)
                _remote(rows, rows, send.at[3 * a + j], recv.at[3 * a + j], (chips[j][0], chips[j][1], c)).start()
        token[...] = jnp.zeros_like(token)

    sem = pltpu.SemaphoreType.DMA((3 * n,))
    res = pl.pallas_call(
        body, name=name, in_specs=[HBM] * n, out_specs=[SEM, SEM] + [HBM] * n + [pl.BlockSpec(memory_space=pltpu.VMEM)],
        out_shape=[sem, sem] + [pltpu.HBM(b.shape, b.dtype) for b in bufs] + [jax.ShapeDtypeStruct((8, LANES), F32)],
        input_output_aliases={a: 2 + a for a in range(n)}, compiler_params=SPLIT,
    )(*[_in_hbm(b) for b in bufs])
    return res[0], res[1], res[2:2 + n], res[-1]


def gather_wait(name, bufs, send_sems, recv_sems, after):
    n = len(bufs)

    def body(*refs):
        ins, send, recv = refs[:n], refs[n], refs[n + 1]
        x, y, c, chips = _place()
        me = 2 * x + y
        for a in range(n):
            for j in range(3):
                dev = (chips[j][0], chips[j][1], c)
                mine = _slab_rows(ins[a], me, c)
                _remote(mine, mine, send.at[3 * a + j], recv.at[3 * a + j], dev).wait_send()
                land = _slab_rows(ins[a], 2 * chips[j][0] + chips[j][1], c)
                _remote(land, land, send.at[3 * a + j], recv.at[3 * a + j], dev).wait_recv()

    return pl.pallas_call(
        body, name=name, in_specs=[HBM] * n + [SEM, SEM] + [ANY] * len(after), out_specs=[HBM] * n,
        out_shape=[pltpu.HBM(b.shape, b.dtype) for b in bufs],
        input_output_aliases={a: a for a in range(n)}, compiler_params=SPLIT,
    )(*bufs, send_sems, recv_sems, *after)


def gather_forward(name, bufs):
    n = len(bufs)

    def body(*refs):
        outs = refs[n:2 * n]
        send_sems, recv_sems = refs[2 * n:]
        x, y, c, chips = _place()
        sibling = (x, y, 1 - c)

        def d2d(a, j, core):
            rows = _slab_rows(outs[a], 2 * chips[j][0] + chips[j][1], core)
            return _remote(rows, rows, send_sems.at[3 * a + j], recv_sems.at[3 * a + j], sibling)

        pairs = [(a, j) for a in range(n) for j in range(3)]
        for a, j in pairs:
            d2d(a, j, c).start()
        for a, j in pairs:
            d2d(a, j, 1 - c).wait_recv()
        for a, j in pairs:
            d2d(a, j, c).wait_send()

    return pl.pallas_call(
        body, name=name, in_specs=[ANY] * n, out_specs=[ANY] * n,
        out_shape=[jax.ShapeDtypeStruct(b.shape, b.dtype) for b in bufs],
        input_output_aliases={a: a for a in range(n)},
        scratch_shapes=[pltpu.SemaphoreType.DMA((3 * n,)), pltpu.SemaphoreType.DMA((3 * n,))],
    )(*bufs)


def _forward_plan(refs):
    x, y, c, chips = _place()
    out = []
    for ref in refs:
        for j in range(3):
            k = 2 * chips[j][0] + chips[j][1]
            out.append((_slab_rows(ref, k, c), _slab_rows(ref, k, c), _slab_rows(ref, k, 1 - c)))
    return out


def _swap_plan(refs):
    x, y, c, _ = _place()
    n = len(refs) // 2
    out = []
    for a in range(n):
        half = refs[a].shape[1] // 2
        src = refs[a].at[:, pl.ds(pl.multiple_of((1 - c) * half, 16), half), :]
        out.append((src, refs[n + a], refs[n + a]))
    return out


def sibling_start(name, arrays, plan, n_copies, after=()):
    n = len(arrays)

    n_in = n + len(after)

    def body(*refs):
        send, recv, token = refs[n_in], refs[n_in + 1], refs[-1]
        x, y, c, _ = _place()
        for idx, (src, dst, _) in enumerate(plan(refs[:n])):
            _remote(src, dst, send.at[idx], recv.at[idx], (x, y, 1 - c)).start()
        token[...] = jnp.zeros_like(token)

    sem = pltpu.SemaphoreType.DMA((n_copies,))
    res = pl.pallas_call(
        body, name=name, in_specs=[HBM] * n + [ANY] * len(after),
        out_specs=[SEM, SEM] + [HBM] * n + [pl.BlockSpec(memory_space=pltpu.VMEM)],
        out_shape=[sem, sem] + [pltpu.HBM(b.shape, b.dtype) for b in arrays] + [jax.ShapeDtypeStruct((8, LANES), F32)],
        input_output_aliases={a: 2 + a for a in range(n)}, compiler_params=SPLIT,
    )(*[_in_hbm(b) for b in arrays], *after)
    return res[0], res[1], res[2:2 + n], res[-1]


def sibling_wait(name, arrays, send_sems, recv_sems, plan, after):
    n = len(arrays)

    def body(*refs):
        send, recv = refs[n], refs[n + 1]
        x, y, c, _ = _place()
        for idx, (src, dst, filled) in enumerate(plan(refs[:n])):
            _remote(src, dst, send.at[idx], recv.at[idx], (x, y, 1 - c)).wait_send()
            _remote(filled, filled, send.at[idx], recv.at[idx], (x, y, 1 - c)).wait_recv()

    return pl.pallas_call(
        body, name=name, in_specs=[HBM] * n + [SEM, SEM] + [ANY] * len(after), out_specs=[HBM] * n,
        out_shape=[pltpu.HBM(b.shape, b.dtype) for b in arrays],
        input_output_aliases={a: a for a in range(n)}, compiler_params=SPLIT,
    )(*arrays, send_sems, recv_sems, *after)


def swap_halves(name, pieces):
    n = len(pieces)
    halves = [p.shape[1] // 2 for p in pieces]

    def body(*refs):
        ins, outs = refs[:n], refs[n:2 * n]
        send_sems, recv_sems = refs[2 * n:]
        x, y, c, _ = _place()
        cps = [_remote(ins[a].at[:, pl.ds(pl.multiple_of((1 - c) * halves[a], 16), halves[a]), :], outs[a],
                       send_sems.at[a], recv_sems.at[a], (x, y, 1 - c)) for a in range(n)]
        for cp in cps:
            cp.start()
        for cp in cps:
            cp.wait()

    return pl.pallas_call(
        body, name=name, in_specs=[ANY] * n, out_specs=[ANY] * n,
        out_shape=[jax.ShapeDtypeStruct((N_CHIPS, h, p.shape[2]), p.dtype) for p, h in zip(pieces, halves)],
        scratch_shapes=[pltpu.SemaphoreType.DMA((n,)), pltpu.SemaphoreType.DMA((n,))],
    )(*pieces)


def pair_sum(name, pieces, got, core):
    _, r, w = pieces.shape
    half = r // 2
    tr = _tile(half, 256, 16)

    def kern(core_ref, p_ref, g_ref, o_ref):
        o_ref[...] = (p_ref[...].astype(F32) + g_ref[...].astype(F32)).astype(o_ref.dtype)

    return pl.pallas_call(
        kern, name=name,
        grid_spec=pltpu.PrefetchScalarGridSpec(
            num_scalar_prefetch=1, grid=(N_CHIPS, half // tr),
            in_specs=[pl.BlockSpec((None, None, tr, w), lambda k, i, core_ref: (k, core_ref[0], i, 0)),
                      pl.BlockSpec((None, tr, w), lambda k, i, core_ref: (k, i, 0))],
            out_specs=pl.BlockSpec((None, tr, w), lambda k, i, core_ref: (k, i, 0))),
        out_shape=jax.ShapeDtypeStruct((N_CHIPS, half, w), pieces.dtype),
        compiler_params=_params("parallel", "parallel"),
    )(core, pieces.reshape(N_CHIPS, 2, half, w), got)


def _scatter_copies(sums, lands, send, recv):
    x, y, c, chips = _place()
    return [_remote(sums[a].at[2 * chips[j][0] + chips[j][1]], lands[a].at[j], send.at[3 * a + j], recv.at[3 * a + j],
                    (chips[j][0], chips[j][1], c)) for a in range(len(sums)) for j in range(3)]


def scatter_start(name, sums):
    n = len(sums)
    lands = [lax.empty((3,) + t.shape[1:], t.dtype) for t in sums]

    def body(*refs):
        ins, land_in, send, recv, token = refs[:n], refs[n:2 * n], refs[2 * n], refs[2 * n + 1], refs[-1]
        for cp in _scatter_copies(ins, land_in, send, recv):
            cp.start()
        token[...] = jnp.zeros_like(token)

    sem = pltpu.SemaphoreType.DMA((3 * n,))
    res = pl.pallas_call(
        body, name=name, in_specs=[HBM] * (2 * n),
        out_specs=[SEM, SEM] + [HBM] * (2 * n) + [pl.BlockSpec(memory_space=pltpu.VMEM)],
        out_shape=[sem, sem] + [pltpu.HBM(t.shape, t.dtype) for t in sums + lands] + [jax.ShapeDtypeStruct((8, LANES), F32)],
        input_output_aliases={a: 2 + a for a in range(2 * n)}, compiler_params=SPLIT,
    )(*[_in_hbm(t) for t in sums + lands])
    return res[0], res[1], res[2:2 + n], res[2 + n:2 + 2 * n], res[-1]


def scatter_wait(name, sums, lands, send_sems, recv_sems, after):
    n = len(sums)

    def body(*refs):
        ins, land_in, send, recv = refs[:n], refs[n:2 * n], refs[2 * n], refs[2 * n + 1]
        for cp in _scatter_copies(ins, land_in, send, recv):
            cp.wait_send()
            cp.wait_recv()

    res = pl.pallas_call(
        body, name=name, in_specs=[HBM] * (2 * n) + [SEM, SEM, ANY], out_specs=[HBM] * (2 * n),
        out_shape=[pltpu.HBM(t.shape, t.dtype) for t in sums + lands],
        input_output_aliases={a: a for a in range(2 * n)}, compiler_params=SPLIT,
    )(*sums, *lands, send_sems, recv_sems, after)
    return res[:n], res[n:]


def chip_sum(name, sums, got, chip, core):
    _, half, w = sums.shape
    tr = _tile(half, 256, 16)
    nb = half // tr

    def kern(ids_ref, s_ref, g0_ref, g1_ref, g2_ref, o_ref):
        o_ref[...] = ((s_ref[...].astype(F32) + g0_ref[...].astype(F32)) + g1_ref[...].astype(F32)) \
            + g2_ref[...].astype(F32)

    def got_spec(j):
        return pl.BlockSpec((None, tr, w), lambda i, ids_ref: (j, i, 0))

    return pl.pallas_call(
        kern, name=name,
        grid_spec=pltpu.PrefetchScalarGridSpec(
            num_scalar_prefetch=1, grid=(nb,),
            in_specs=[pl.BlockSpec((None, tr, w), lambda i, ids_ref: (ids_ref[0], i, 0)),
                      got_spec(0), got_spec(1), got_spec(2)],
            out_specs=pl.BlockSpec((tr, w), lambda i, ids_ref: (ids_ref[1] * nb + i, 0))),
        out_shape=jax.ShapeDtypeStruct((2 * half, w), F32),
        compiler_params=_params("parallel"),
    )(jnp.concatenate([chip, core]), sums, got, got, got)


def join_halves(name, shards):
    n = len(shards)
    halves = [g.shape[0] // 2 for g in shards]

    def body(*refs):
        outs = refs[n:2 * n]
        send_sems, recv_sems = refs[2 * n:]
        x, y, c, _ = _place()
        cps = []
        for a in range(n):
            rows = outs[a].at[pl.ds(pl.multiple_of(c * halves[a], 8), halves[a])]
            cps.append(_remote(rows, rows, send_sems.at[a], recv_sems.at[a], (x, y, 1 - c)))
        for cp in cps:
            cp.start()
        for cp in cps:
            cp.wait()

    return pl.pallas_call(
        body, name=name, in_specs=[ANY] * n, out_specs=[ANY] * n,
        out_shape=[jax.ShapeDtypeStruct(g.shape, g.dtype) for g in shards],
        input_output_aliases={a: a for a in range(n)},
        scratch_shapes=[pltpu.SemaphoreType.DMA((n,)), pltpu.SemaphoreType.DMA((n,))],
    )(*shards)


def _adam(w, g, m, v):
    m = ADAM_B1 * m + (1.0 - ADAM_B1) * g
    v = ADAM_B2 * v + (1.0 - ADAM_B2) * (g * g)
    m_hat = m / (1.0 - ADAM_B1 ** ADAM_STEP)
    v_hat = v / (1.0 - ADAM_B2 ** ADAM_STEP)
    delta = -ADAM_LR * (m_hat / (jnp.sqrt(v_hat) + ADAM_EPS) + ADAM_WD * w)
    return delta, m, v


def small_allreduce_adam(g_part, w, m, v, after):
    n_dev = 8
    r, d = g_part.shape

    def body(g_ref, w_ref, m_ref, v_ref, after_ref, gs_ref, dl_ref, nm_ref, nv_ref, all_ref, send_sems, recv_sems):
        x, y, c, _ = _place()
        me = 4 * x + 2 * y + c
        all_ref[me] = g_ref[...]
        cps = []
        for rel in range(1, n_dev):
            px = 1 - x if rel & 4 else x
            py = 1 - y if rel & 2 else y
            pc = 1 - c if rel & 1 else c
            cps.append(_remote(g_ref, all_ref.at[me], send_sems.at[rel - 1], recv_sems.at[rel - 1], (px, py, pc)))
        for cp in cps:
            cp.start()
        for cp in cps:
            cp.wait()
        total = all_ref[0]
        for dev in range(1, n_dev):
            total = total + all_ref[dev]
        gs_ref[...] = total
        delta, nm, nv = _adam(w_ref[...], total, m_ref[...], v_ref[...])
        dl_ref[...] = delta
        nm_ref[...] = nm
        nv_ref[...] = nv

    vm = pl.BlockSpec(memory_space=pltpu.VMEM)
    out = jax.ShapeDtypeStruct((r, d), F32)
    return pl.pallas_call(
        body, name="small_allreduce_adam", in_specs=[vm, vm, vm, vm, ANY], out_specs=[vm, vm, vm, vm],
        out_shape=[out, out, out, out],
        scratch_shapes=[pltpu.VMEM((n_dev, r, d), F32), pltpu.SemaphoreType.DMA((n_dev - 1,)),
                        pltpu.SemaphoreType.DMA((n_dev - 1,))],
    )(g_part, w, m, v, after)


def adam_update(name, w, m, v, g_buf, col_blk, after=None):
    w, m, v = w[0], m[0], v[0]
    r, c = w.shape
    tr = _tile(r, 128, 8)
    extra = [] if after is None else [after]

    def kern(w_ref, m_ref, v_ref, g_ref, *rest):
        go_ref, dl_ref, nm_ref, nv_ref = rest[len(extra):]
        g = g_ref[...]
        delta, nm, nv = _adam(w_ref[...], g, m_ref[...], v_ref[...])
        go_ref[...] = g
        dl_ref[...] = delta
        nm_ref[...] = nm
        nv_ref[...] = nv

    blk = pl.BlockSpec((tr, c), lambda i: (i, 0))
    out = jax.ShapeDtypeStruct((r, c), F32)
    res = pl.pallas_call(
        kern, name=name, grid=(r // tr,),
        in_specs=[blk, blk, blk, pl.BlockSpec((tr, c), lambda i: (i, col_blk))] + [ANY] * len(extra),
        out_specs=[blk] * 4, out_shape=[out] * 4, compiler_params=_params("parallel"),
    )(w, m, v, g_buf, *extra)
    return [a[None] for a in res]


def _w_in_segments(cw, n_qkv, n_heads, d):
    out = []

    def add(lo, hi, main):
        while lo < hi:
            k, a = divmod(lo, cw)
            w = min(cw - a, hi - lo)
            out.append((k, a, main, w))
            lo, main = lo + w, main + w

    add(0, n_qkv, 0)
    add(n_qkv + n_heads, N_CHIPS * cw, n_qkv)
    add(n_qkv, n_qkv + n_heads, n_qkv + 2 * d)
    return out


def regroup_w_in(g_in, segments, n_main):
    _, d, cw = g_in.shape
    tr = _tile(d, 128, 16)
    n_real = max(m + w for _, _, m, w in segments)

    def kern(s_ref, o_ref):
        for k, a, m, w in segments:
            o_ref[:, m:m + w] = s_ref[k, :, a:a + w]
        o_ref[:, n_real:] = jnp.zeros((tr, n_main - n_real), o_ref.dtype)

    return pl.pallas_call(
        kern, name="regroup_w_in", grid=(d // tr,),
        in_specs=[pl.BlockSpec((N_CHIPS, tr, cw), lambda i: (0, i, 0))],
        out_specs=pl.BlockSpec((tr, n_main), lambda i: (i, 0)),
        out_shape=jax.ShapeDtypeStruct((d, n_main), g_in.dtype), compiler_params=_params("parallel"),
    )(g_in)


def regroup_dw_in(dw_main, segments, cw):
    d, n_main = dw_main.shape
    tr = _tile(d, 128, 16)

    def kern(s_ref, o_ref):
        for k, a, m, w in segments:
            o_ref[k, :, a:a + w] = s_ref[:, m:m + w]

    return pl.pallas_call(
        kern, name="regroup_dw_in", grid=(d // tr,),
        in_specs=[pl.BlockSpec((tr, n_main), lambda i: (i, 0))],
        out_specs=pl.BlockSpec((N_CHIPS, tr, cw), lambda i: (0, i, 0)),
        out_shape=jax.ShapeDtypeStruct((N_CHIPS, d, cw), dw_main.dtype), compiler_params=_params("parallel"),
    )(dw_main)


def kernel(x, norm_mix_pre, norm_mix_post, w_in, b_forget, w_branch_sb, w_branch_fox, w_out, norm_ffn_pre, norm_ffn_post, w_ffn_gate, w_ffn_up, w_ffn_down, loss_target, m_norm_mix_pre, m_norm_mix_post, m_w_in, m_b_forget, m_w_branch_sb, m_w_branch_fox, m_w_out, m_norm_ffn_pre, m_norm_ffn_post, m_w_ffn_gate, m_w_ffn_up, m_w_ffn_down, v_norm_mix_pre, v_norm_mix_post, v_w_in, v_b_forget, v_w_branch_sb, v_w_branch_fox, v_w_out, v_norm_ffn_pre, v_norm_ffn_post, v_w_ffn_gate, v_w_ffn_up, v_w_ffn_down):
    s, d = x.shape[1], x.shape[2]
    n_heads = b_forget.shape[1]
    d_att = n_heads * HEAD_DIM
    c_in = w_in.shape[2]
    c_br = w_branch_sb.shape[2]
    c_gu = w_ffn_gate.shape[2]
    d_ff = c_gu * N_CHIPS
    d_in = c_in * N_CHIPS
    f_pad = 512
    n_qkv = 6 * d_att
    n_gf = 2 * d + f_pad
    core = lax.axis_index("c").astype(jnp.int32).reshape(1)
    chip = (2 * lax.axis_index("x") + lax.axis_index("y")).astype(jnp.int32).reshape(1)

    as_t = lambda a: jnp.transpose(a, (2, 0, 1))
    w_in_t, m_in_t, v_in_t = as_t(w_in), as_t(m_w_in), as_t(v_w_in)
    in_send, in_recv, in_bufs, in_token = gather_start(
        "gather_start_w_in", [place_transposed("place_w_in", w_in_t, chip)])
    ag_send, ag_recv, ag_bufs, ag_token = gather_start("gather_start_rest", [
        cast_place("place_branch", [w_branch_sb, w_branch_fox], chip),
        cast_place("place_out", [w_out + in_token[0, 0]], chip),
        cast_place("place_gate_up", [w_ffn_gate, w_ffn_up], chip),
        cast_place("place_down", [w_ffn_down], chip)])
    g_in, = gather_forward("forward_w_in", gather_wait("gather_wait_w_in", in_bufs, in_send, in_recv,
                                                       [ag_token]))
    segments = _w_in_segments(c_in, n_qkv, n_heads, d)
    w_main = regroup_w_in(g_in, segments, n_qkv + n_gf)
    x2 = x[0]
    tgt = loss_target[0]
    b_pad = jnp.pad(b_forget, ((0, 0), (0, LANES - n_heads)))

    u = norm_in(x2, norm_mix_pre)
    qkv = mm(u, w_main, "nn", BF16, "proj_qkv", b_win=(0, n_qkv))
    gf = mm(u, w_main, "nn", F32, "proj_gates", b_win=(n_qkv, n_gf))
    cum = cum_fwd(gf, b_pad, 2 * d)
    cum_heads = cum[:, :n_heads].T
    cum_col, cum_row = cum_heads[:, :, None], cum_heads[:, None, :]
    o_sb = sb_fwd(qkv, n_heads, 0)
    o_fx, lse = fox_fwd(qkv, cum_col, cum_row, n_heads, 3 * n_heads)
    rest = gather_wait("gather_wait_rest", ag_bufs, ag_send, ag_recv, [o_sb, o_fx])
    g_br, g_out = gather_forward("forward_small", rest[:2])
    fb_send, fb_recv, fb_bufs, fb_token = sibling_start("forward_big_start", rest[2:], _forward_plan, 6, after=[g_br])
    w_o = g_out.reshape(d, d)
    bsb = mm(o_sb, g_br, "nn", F32, "branch_sb", tn=c_br, chunks=(1, 0), after=fb_token)
    bfx = mm(o_fx, g_br, "nn", F32, "branch_fox", tn=c_br, chunks=(1, 1), after=fb_token)
    merged = gate_fwd(bsb, bfx, gf)
    mix = mm(merged, w_o, "nn", F32, "out_proj")
    h1, u2 = mid_fwd(x2, mix, norm_mix_post, norm_ffn_pre)
    g_gu, g_dn = sibling_wait("forward_big_wait", fb_bufs, fb_send, fb_recv, _forward_plan, [u2])
    w_dn = g_dn.reshape(d_ff, d)
    gu, act = ffn_up_fused(u2, g_gu, c_gu)
    ff = mm(act, w_dn, "nn", F32, "ffn_down")
    dy, d_ff_out, dg_fpost, loss_part = loss_head(h1, ff, norm_ffn_post, tgt)

    p_dn = mm(act, d_ff_out, "tn", BF16, "dw_ffn_down").reshape(N_CHIPS, d_ff // N_CHIPS, d)
    d_gu = ffn_down_bwd_fused(d_ff_out, w_dn, gu, c_gu)
    p_gu = mm(u2, d_gu, "tn", BF16, "dw_ffn_gate_up", tn=c_gu, chunks=(2, 0),
              out_into=lax.empty((N_CHIPS, d, 2 * c_gu), BF16))
    sw_send, sw_recv, sw_arrs, sw_token = sibling_start(
        "swap_big_start", [p_gu, p_dn, lax.empty((N_CHIPS, d // 2, 2 * c_gu), BF16),
                           lax.empty((N_CHIPS, d_ff // N_CHIPS // 2, d), BF16)], _swap_plan, 2)
    du2 = mm(d_gu, g_gu, "nt", F32, "d_u2", tk=c_gu, chunks=(2, 0), after=sw_token)
    dh1, d_mix, dg_fpre, dg_post = mid_bwd(dy, du2, h1, mix, norm_ffn_pre, norm_mix_post)
    p_out = mm(merged, d_mix, "tn", BF16, "dw_out").reshape(N_CHIPS, d // N_CHIPS, d)
    d_merged = mm(d_mix, w_o, "nt", F32, "d_merged")
    d_bsb, d_bfx, d_gs, d_gx = gate_bwd(d_merged, bsb, bfx, gf)
    p_br = mm(o_sb, d_bsb, "tn", BF16, "dw_branch_sb", tn=c_br, chunks=(1, 0),
              out_into=lax.empty((N_CHIPS, d_att, 2 * c_br), BF16))
    p_br = mm(o_fx, d_bfx, "tn", BF16, "dw_branch_fox", tn=c_br, chunks=(1, 1), out_into=p_br)
    d_osb = mm(d_bsb, g_br, "nt", BF16, "d_o_sb", tk=c_br, chunks=(1, 0))
    d_ofx = mm(d_bfx, g_br, "nt", BF16, "d_o_fox", tk=c_br, chunks=(1, 1))

    def reduce_start(tag, pieces, names):
        from_sibling = swap_halves("swap_halves_" + tag, pieces)
        sums = [pair_sum("pair_sum_" + t, p, q, core) for t, p, q in zip(names, pieces, from_sibling)]
        return scatter_start("scatter_start_" + tag, sums)

    def reduce_end(tag, started, names, after):
        send, recv, sums, lands, _ = started
        sums, lands = scatter_wait("scatter_wait_" + tag, sums, lands, send, recv, after)
        return join_halves("join_halves_" + tag, [chip_sum("chip_sum_" + t, sm, got, chip, core)
                                                  for t, sm, got in zip(names, sums, lands)])

    rest_names = ["branch", "out", "gate_up", "down"]
    p_gu, p_dn, q_gu, q_dn = sibling_wait("swap_big_wait", sw_arrs, sw_send, sw_recv, _swap_plan, [p_br])
    q_br, q_out = swap_halves("swap_halves_small", [p_br, p_out])
    rest_started = scatter_start("scatter_start_rest", [
        pair_sum("pair_sum_" + t, p, q, core)
        for t, p, q in zip(rest_names, [p_br, p_out, p_gu, p_dn], [q_br, q_out, q_gu, q_dn])])
    d_osb = d_osb + rest_started[4][0, 0].astype(BF16)
    dq_s, dk_s, dv_s = sb_bwd(qkv, d_osb, n_heads, 0)
    dq_f, dk_f, dv_f, dcq, dck = fox_bwd(qkv, cum_col, cum_row, o_fx, d_ofx, lse, n_heads, 3 * n_heads)
    d_cum = jnp.pad((dcq[:, :, 0] + dck[:, 0, :]).T, ((0, 0), (0, LANES - n_heads)))
    d_f, db_pad = cum_bwd(d_cum, gf, b_pad, 2 * d, n_heads)
    d_main = jnp.concatenate(
        [dq_s, dk_s, dv_s, dq_f, dk_f, dv_f, d_gs, d_gx, d_f, jnp.zeros((s, f_pad - LANES), BF16)], axis=1)
    p_in = regroup_dw_in(mm(u, d_main, "tn", BF16, "dw_in"), segments, c_in)

    in_started = reduce_start("w_in", [p_in], ["in"])
    du = mm(d_main, w_main, "nt", F32, "d_u", after=in_started[4])
    dx, dg_pre = in_bwd(dh1, du, x2, norm_mix_pre + in_started[4][0:1, 0:1])
    gr_br, gr_out, gr_gu, gr_dn = reduce_end("rest", rest_started, rest_names, dx)

    upd_bs = adam_update("adam_branch_sb", w_branch_sb, m_w_branch_sb, v_w_branch_sb, gr_br, 0)
    upd_bf = adam_update("adam_branch_fox", w_branch_fox, m_w_branch_fox, v_w_branch_fox, gr_br, 1)
    upd_o = adam_update("adam_out", w_out, m_w_out, v_w_out, gr_out, 0)
    upd_ga = adam_update("adam_gate", w_ffn_gate, m_w_ffn_gate, v_w_ffn_gate, gr_gu, 0)
    upd_up = adam_update("adam_up", w_ffn_up, m_w_ffn_up, v_w_ffn_up, gr_gu, 1)
    upd_dn = adam_update("adam_down", w_ffn_down, m_w_ffn_down, v_w_ffn_down, gr_dn, 0)

    done = sum(u_[1][0, 0:1, 0:1] for u_ in (upd_bs, upd_bf, upd_o, upd_ga, upd_up, upd_dn))
    gr_in, = reduce_end("w_in", in_started, ["in"], done)
    upd_in_t = adam_update_transposed("adam_w_in", w_in_t, m_in_t, v_in_t, gr_in)
    upd_in = [jnp.transpose(a, (1, 2, 0)) for a in upd_in_t]
    grads, deltas, new_ms, new_vs = zip(upd_in, upd_bs, upd_bf, upd_o, upd_ga, upd_up, upd_dn)

    def pack(rows):
        rows = [jnp.pad(r_, ((0, 0), (0, d - r_.shape[1]))) for r_ in rows]
        return jnp.concatenate(rows + [jnp.zeros((8 - len(rows), d), F32)], axis=0)

    sm_g, sm_d, sm_m, sm_v = small_allreduce_adam(
        pack([dg_pre, dg_post, dg_fpre, dg_fpost, db_pad]),
        pack([norm_mix_pre, norm_mix_post, norm_ffn_pre, norm_ffn_post, b_forget]),
        pack([m_norm_mix_pre, m_norm_mix_post, m_norm_ffn_pre, m_norm_ffn_post, m_b_forget]),
        pack([v_norm_mix_pre, v_norm_mix_post, v_norm_ffn_pre, v_norm_ffn_post, v_b_forget]), after=upd_in_t[1])

    def small(a):
        return [a[0:1], a[1:2], a[2:3], a[3:4], a[4:5, :n_heads]]

    def ordered(sm, bg):
        return [sm[0], sm[1], bg[0], sm[4], bg[1], bg[2], bg[3], sm[2], sm[3], bg[4], bg[5], bg[6]]

    loss = lax.psum(loss_part[0, 0], ("x", "y", "c"))
    return (loss, dx[None], *ordered(small(sm_g), grads), *ordered(small(sm_d), deltas),
            *ordered(small(sm_m), new_ms), *ordered(small(sm_v), new_vs))
```

```python
import functools

import jax
import jax.numpy as jnp
from jax import lax
from jax.experimental import pallas as pl
from jax.experimental.pallas import tpu as pltpu

F32 = jnp.float32
BF16 = jnp.bfloat16
MESH = pl.DeviceIdType.MESH

HEAD_DIM = 128
LANES = 128
ATT_TILE = 512
ROW_TILE = 256
N_CHIPS = 4
RMS_EPS = 1e-6
ADAM_LR = 0.001
ADAM_B1 = 0.9
ADAM_B2 = 0.999
ADAM_EPS = 1e-08
ADAM_WD = 0.01
ADAM_STEP = 10
NEG_BIG = -1e30
VMEM_LIMIT = 56 * 1024 * 1024
MM_VMEM_BUDGET = 40 * 1024 * 1024
ATT_STRIP = 512

NN = (((1,), (0,)), ((), ()))
NT = (((1,), (1,)), ((), ()))
TN = (((0,), (0,)), ((), ()))


def _tile(n, pref, align):
    best = None
    t = align
    while t <= min(n, pref):
        if n % t == 0:
            best = t
        t += align
    return n if best is None else best


def _params(*sem):
    return pltpu.CompilerParams(dimension_semantics=sem, vmem_limit_bytes=VMEM_LIMIT)


def _mm_tiles(m, n, k, a_bytes, b_bytes, out_bytes, tn, tk):
    tm = _tile(m, 2048, LANES)
    tk = tk or _tile(k, 512, LANES)

    def vmem(t):
        acc = 0 if out_bytes == 4 else tm * t * 4
        return acc + 2 * tm * t * out_bytes + 2 * (tm * tk * a_bytes + tk * t * b_bytes)

    if tn is None:
        fits = [t for t in range(LANES, min(n, 2048) + 1, LANES) if n % t == 0 and vmem(t) <= MM_VMEM_BUDGET]
        tn = max(fits) if fits else _tile(n, LANES, LANES)
    return tm, tn, tk


def mm(a, b, mode, out_dtype, name, *, tn=None, tk=None, b_win=None, chunks=None, out_into=None, after=None):
    n_per, blk0 = chunks if chunks else (1, 0)
    if mode == "nn":
        m, k = a.shape
        n = b.shape[0] * n_per * tn if chunks else (b_win[1] if b_win else b.shape[1])
    elif mode == "nt":
        m = a.shape[0]
        k = b.shape[0] * n_per * tk if chunks else a.shape[1]
        n = b.shape[-2]
    else:
        k, m = a.shape
        n = b.shape[1]
    in_place = jnp.dtype(out_dtype) == jnp.dtype(F32)
    tm, tn, tk = _mm_tiles(m, n, k, a.dtype.itemsize, b.dtype.itemsize, jnp.dtype(out_dtype).itemsize, tn, tk)
    assert m % tm == 0 and n % tn == 0 and k % tk == 0, (name, m, n, k, tm, tn, tk)
    j0 = 0
    if b_win:
        assert b_win[0] % tn == 0
        j0 = b_win[0] // tn
    nk = k // tk
    dims = {"nn": NN, "nt": NT, "tn": TN}[mode]

    def kern(a_ref, b_ref, *rest):
        o_ref, acc_ref = (rest[-1], rest[-1]) if in_place else (rest[-2], rest[-1])
        kk = pl.program_id(2)

        @pl.when(kk == 0)
        def _():
            acc_ref[...] = jnp.zeros_like(acc_ref)

        acc_ref[...] += lax.dot_general(a_ref[...].astype(BF16), b_ref[...].astype(BF16), dims,
                                        preferred_element_type=F32)

        if not in_place:
            @pl.when(kk == nk - 1)
            def _():
                o_ref[...] = acc_ref[...].astype(o_ref.dtype)

    out_spec = pl.BlockSpec((tm, tn), lambda i, j, kk: (i, j))
    out_shape = jax.ShapeDtypeStruct((m, n), out_dtype)
    if mode == "nn":
        a_spec = pl.BlockSpec((tm, tk), lambda i, j, kk: (i, kk))
        if chunks:
            b_spec = pl.BlockSpec((None, tk, tn), lambda i, j, kk: (j // n_per, kk, blk0 + j % n_per))
        else:
            b_spec = pl.BlockSpec((tk, tn), lambda i, j, kk: (kk, j + j0))
    elif mode == "nt":
        a_spec = pl.BlockSpec((tm, tk), lambda i, j, kk: (i, kk))
        if chunks:
            b_spec = pl.BlockSpec((None, tn, tk), lambda i, j, kk: (kk // n_per, j, blk0 + kk % n_per))
        else:
            b_spec = pl.BlockSpec((tn, tk), lambda i, j, kk: (j, kk))
    else:
        a_spec = pl.BlockSpec((tk, tm), lambda i, j, kk: (kk, i))
        b_spec = pl.BlockSpec((tk, tn), lambda i, j, kk: (kk, j))
        if chunks:
            out_spec = pl.BlockSpec((None, tm, tn), lambda i, j, kk: (j // n_per, i, blk0 + j % n_per))
    in_specs, operands, aliases = [a_spec, b_spec], [a, b], {}
    if chunks and mode == "tn":
        assert out_into is not None
        out_shape = jax.ShapeDtypeStruct(out_into.shape, out_dtype)
        in_specs.append(pl.BlockSpec(memory_space=pl.ANY))
        operands.append(out_into)
        aliases = {2: 0}
    if after is not None:
        in_specs.append(pl.BlockSpec(memory_space=pl.ANY))
        operands.append(after)
    return pl.pallas_call(
        kern, name=name, grid=(m // tm, n // tn, nk),
        in_specs=in_specs, out_specs=out_spec, out_shape=out_shape,
        scratch_shapes=[] if in_place else [pltpu.VMEM((tm, tn), F32)], input_output_aliases=aliases,
        compiler_params=_params("parallel", "parallel", "arbitrary"),
    )(*operands)


def _rstd(v):
    return lax.rsqrt(jnp.mean(v * v, axis=-1, keepdims=True) + RMS_EPS)


def _norm_bwd(v, g, dy):
    r = _rstd(v)
    vh = v * r
    dyg = dy * g
    dv = r * (dyg - vh * jnp.mean(dyg * vh, axis=-1, keepdims=True))
    return dv, jnp.sum(dy * vh, axis=0, keepdims=True)


def _row_call(kern, name, ins, outs, s, d):
    tr = _tile(s, ROW_TILE, 16)

    def spec(shape, is_row):
        if is_row:
            return pl.BlockSpec((tr, shape[1]), lambda i: (i, 0))
        return pl.BlockSpec(shape, lambda i: (0, 0))

    return pl.pallas_call(
        kern, name=name, grid=(s // tr,),
        in_specs=[spec(a.shape, r) for a, r in ins],
        out_specs=[spec(sh, r) for sh, _, r in outs],
        out_shape=[jax.ShapeDtypeStruct(sh, dt) for sh, dt, _ in outs],
        compiler_params=_params("arbitrary"),
    )(*[a for a, _ in ins])


def norm_in(x, g):
    s, d = x.shape

    def kern(x_ref, g_ref, u_ref):
        v = x_ref[...]
        u_ref[...] = (v * _rstd(v) * g_ref[...]).astype(BF16)

    return _row_call(kern, "norm_in", [(x, True), (g, False)], [((s, d), BF16, True)], s, d)[0]


def mid_fwd(x, mix, g_post, g_fpre):
    s, d = x.shape

    def kern(x_ref, mix_ref, gp_ref, gf_ref, h1_ref, u2_ref):
        mixv = mix_ref[...]
        h1 = x_ref[...] + mixv * _rstd(mixv) * gp_ref[...]
        h1_ref[...] = h1
        u2_ref[...] = (h1 * _rstd(h1) * gf_ref[...]).astype(BF16)

    return _row_call(kern, "mid_fwd", [(x, True), (mix, True), (g_post, False), (g_fpre, False)],
                     [((s, d), F32, True), ((s, d), BF16, True)], s, d)


def loss_head(h1, ff, g_fpost, target):
    s, d = h1.shape

    def kern(h1_ref, ff_ref, g_ref, t_ref, dy_ref, dff_ref, dg_ref, loss_ref):
        @pl.when(pl.program_id(0) == 0)
        def _():
            dg_ref[...] = jnp.zeros_like(dg_ref)
            loss_ref[...] = jnp.zeros_like(loss_ref)

        ffv = ff_ref[...]
        g = g_ref[...]
        y = h1_ref[...] + ffv * _rstd(ffv) * g
        diff = y - t_ref[...]
        row_loss = jnp.mean(diff * diff, axis=-1, keepdims=True)
        loss_ref[...] += 0.5 * jnp.sum(row_loss, axis=0, keepdims=True)
        dy = diff / d
        dy_ref[...] = dy
        dff, dg = _norm_bwd(ffv, g, dy)
        dff_ref[...] = dff.astype(BF16)
        dg_ref[...] += dg

    return _row_call(kern, "loss_head",
                     [(h1, True), (ff, True), (g_fpost, False), (target, True)],
                     [((s, d), F32, True), ((s, d), BF16, True), ((1, d), F32, False), ((1, 1), F32, False)], s, d)


def mid_bwd(dy, du2, h1, mix, g_fpre, g_post):
    s, d = dy.shape

    def kern(dy_ref, du2_ref, h1_ref, mix_ref, gf_ref, gp_ref, dh1_ref, dmix_ref, dgf_ref, dgp_ref):
        @pl.when(pl.program_id(0) == 0)
        def _():
            dgf_ref[...] = jnp.zeros_like(dgf_ref)
            dgp_ref[...] = jnp.zeros_like(dgp_ref)

        dh, dgf = _norm_bwd(h1_ref[...], gf_ref[...], du2_ref[...])
        dh1 = dy_ref[...] + dh
        dh1_ref[...] = dh1
        dmix, dgp = _norm_bwd(mix_ref[...], gp_ref[...], dh1)
        dmix_ref[...] = dmix.astype(BF16)
        dgf_ref[...] += dgf
        dgp_ref[...] += dgp

    return _row_call(kern, "mid_bwd",
                     [(dy, True), (du2, True), (h1, True), (mix, True), (g_fpre, False), (g_post, False)],
                     [((s, d), F32, True), ((s, d), BF16, True), ((1, d), F32, False), ((1, d), F32, False)], s, d)


def in_bwd(dh1, du, x, g_pre):
    s, d = x.shape

    def kern(dh1_ref, du_ref, x_ref, g_ref, dx_ref, dg_ref):
        @pl.when(pl.program_id(0) == 0)
        def _():
            dg_ref[...] = jnp.zeros_like(dg_ref)

        dxn, dg = _norm_bwd(x_ref[...], g_ref[...], du_ref[...])
        dx_ref[...] = dh1_ref[...] + dxn
        dg_ref[...] += dg

    return _row_call(kern, "in_bwd", [(dh1, True), (du, True), (x, True), (g_pre, False)],
                     [((s, d), F32, True), ((1, d), F32, False)], s, d)


def _sigmoid(v):
    return 1.0 / (1.0 + jnp.exp(-v))


def gate_fwd(bsb, bfx, gf):
    s, d = bsb.shape
    tr, tc = _tile(s, 256, 16), _tile(d, 512, LANES)
    nc = d // tc

    def kern(bsb_ref, bfx_ref, gs_ref, gx_ref, o_ref):
        o_ref[...] = (_sigmoid(gs_ref[...]) * bsb_ref[...] + _sigmoid(gx_ref[...]) * bfx_ref[...]).astype(BF16)

    blk = pl.BlockSpec((tr, tc), lambda i, j: (i, j))
    return pl.pallas_call(
        kern, name="gate_fwd", grid=(s // tr, nc),
        in_specs=[blk, blk, blk, pl.BlockSpec((tr, tc), lambda i, j: (i, j + nc))],
        out_specs=blk, out_shape=jax.ShapeDtypeStruct((s, d), BF16),
        compiler_params=_params("parallel", "parallel"),
    )(bsb, bfx, gf, gf)


def gate_bwd(dmerged, bsb, bfx, gf):
    s, d = bsb.shape
    tr, tc = _tile(s, 256, 16), _tile(d, 512, LANES)
    nc = d // tc

    def kern(dm_ref, bsb_ref, bfx_ref, gs_ref, gx_ref, dbs_ref, dbx_ref, dgs_ref, dgx_ref):
        dm = dm_ref[...]
        ss = _sigmoid(gs_ref[...])
        sx = _sigmoid(gx_ref[...])
        dbs_ref[...] = (dm * ss).astype(BF16)
        dbx_ref[...] = (dm * sx).astype(BF16)
        dgs_ref[...] = (dm * bsb_ref[...] * ss * (1.0 - ss)).astype(BF16)
        dgx_ref[...] = (dm * bfx_ref[...] * sx * (1.0 - sx)).astype(BF16)

    blk = pl.BlockSpec((tr, tc), lambda i, j: (i, j))
    out = jax.ShapeDtypeStruct((s, d), BF16)
    return pl.pallas_call(
        kern, name="gate_bwd", grid=(s // tr, nc),
        in_specs=[blk, blk, blk, blk, pl.BlockSpec((tr, tc), lambda i, j: (i, j + nc))],
        out_specs=[blk, blk, blk, blk], out_shape=[out, out, out, out],
        compiler_params=_params("parallel", "parallel"),
    )(dmerged, bsb, bfx, gf, gf)


FFN_ROWS = 1024


def ffn_up_fused(u2, w_gu, cw):
    s, d = u2.shape
    nc = w_gu.shape[0]
    tm, tk = _tile(s, FFN_ROWS, LANES), _tile(d, 512, LANES)
    nk = d // tk

    def kern(a_ref, b_ref, gu_ref, act_ref):
        kk = pl.program_id(2)

        @pl.when(kk == 0)
        def _():
            gu_ref[...] = jnp.zeros_like(gu_ref)

        gu_ref[...] += jnp.dot(a_ref[...], b_ref[...], preferred_element_type=F32)

        @pl.when(kk == nk - 1)
        def _():
            g = gu_ref[:, :cw]
            act_ref[...] = (g * _sigmoid(g) * gu_ref[:, cw:]).astype(BF16)

    return pl.pallas_call(
        kern, name="ffn_gate_up", grid=(s // tm, nc, nk),
        in_specs=[pl.BlockSpec((tm, tk), lambda i, j, kk: (i, kk)),
                  pl.BlockSpec((None, tk, 2 * cw), lambda i, j, kk: (j, kk, 0))],
        out_specs=[pl.BlockSpec((tm, 2 * cw), lambda i, j, kk: (i, j)), pl.BlockSpec((tm, cw), lambda i, j, kk: (i, j))],
        out_shape=[jax.ShapeDtypeStruct((s, nc * 2 * cw), F32), jax.ShapeDtypeStruct((s, nc * cw), BF16)],
        compiler_params=_params("parallel", "parallel", "arbitrary"),
    )(u2, w_gu)


def ffn_down_bwd_fused(d_ff, w_dn, gu, cw):
    s, d = d_ff.shape
    nc = gu.shape[1] // (2 * cw)
    tm, tk = _tile(s, FFN_ROWS, LANES), _tile(d, 512, LANES)
    nk = d // tk

    def kern(a_ref, b_ref, gu_ref, o_ref, acc_ref):
        kk = pl.program_id(2)

        @pl.when(kk == 0)
        def _():
            acc_ref[...] = jnp.zeros_like(acc_ref)

        acc_ref[...] += lax.dot_general(a_ref[...], b_ref[...], NT, preferred_element_type=F32)

        @pl.when(kk == nk - 1)
        def _():
            da = acc_ref[...]
            g = gu_ref[:, :cw]
            sg = _sigmoid(g)
            o_ref[:, :cw] = (da * gu_ref[:, cw:] * (sg * (1.0 + g * (1.0 - sg)))).astype(BF16)
            o_ref[:, cw:] = (da * (g * sg)).astype(BF16)

    return pl.pallas_call(
        kern, name="d_act_swiglu", grid=(s // tm, nc, nk),
        in_specs=[pl.BlockSpec((tm, tk), lambda i, j, kk: (i, kk)),
                  pl.BlockSpec((cw, tk), lambda i, j, kk: (j, kk)),
                  pl.BlockSpec((tm, 2 * cw), lambda i, j, kk: (i, j))],
        out_specs=pl.BlockSpec((tm, 2 * cw), lambda i, j, kk: (i, j)),
        out_shape=jax.ShapeDtypeStruct(gu.shape, BF16),
        scratch_shapes=[pltpu.VMEM((tm, cw), F32)],
        compiler_params=_params("parallel", "parallel", "arbitrary"),
    )(d_ff, w_dn, gu)


def _split3(v):
    hi = v.astype(BF16)
    r = v - hi.astype(F32)
    mid = r.astype(BF16)
    lo = (r - mid.astype(F32)).astype(BF16)
    return hi, mid, lo


def _dot3_right(v, ones):
    hi, mid, lo = _split3(v)
    d = lambda p: jnp.dot(p, ones, preferred_element_type=F32)
    return (d(lo) + d(mid)) + d(hi)


def _dot3_left(ones, v):
    hi, mid, lo = _split3(v)
    d = lambda p: jnp.dot(ones, p, preferred_element_type=F32)
    return (d(lo) + d(mid)) + d(hi)


def _split2(v):
    hi = v.astype(BF16)
    return hi, (v - hi.astype(F32)).astype(BF16)


def _dot2_right(v, ones):
    hi, lo = _split2(v)
    return jnp.dot(lo, ones, preferred_element_type=F32) + jnp.dot(hi, ones, preferred_element_type=F32)


def _log1p_exp_neg_abs(v):
    return jnp.log(1.0 + jnp.exp(-jnp.abs(v)))


def _mask01(cond):
    return jnp.where(cond, 1.0, 0.0).astype(BF16)


def _iota2(t):
    return (lax.broadcasted_iota(jnp.int32, (t, t), 0), lax.broadcasted_iota(jnp.int32, (t, t), 1))


def cum_fwd(gf, b_pad, f_col0):
    s = gf.shape[0]
    t = _tile(s, ATT_TILE, LANES)
    fb = f_col0 // LANES

    def kern(f_ref, b_ref, cum_ref, carry_ref):
        @pl.when(pl.program_id(0) == 0)
        def _():
            carry_ref[...] = jnp.zeros_like(carry_ref)

        v = f_ref[...] + b_ref[...]
        lf = jnp.minimum(v, 0.0) - _log1p_exp_neg_abs(v)
        row, col = _iota2(t)
        cum = _dot3_left(_mask01(col <= row), lf) + carry_ref[...]
        cum_ref[...] = cum
        carry_ref[...] = cum[t - 1:t, :]

    return pl.pallas_call(
        kern, name="cum_fwd", grid=(s // t,),
        in_specs=[pl.BlockSpec((t, LANES), lambda i: (i, fb)), pl.BlockSpec((1, LANES), lambda i: (0, 0))],
        out_specs=pl.BlockSpec((t, LANES), lambda i: (i, 0)),
        out_shape=jax.ShapeDtypeStruct((s, LANES), F32),
        scratch_shapes=[pltpu.VMEM((1, LANES), F32)],
        compiler_params=_params("arbitrary"),
    )(gf, b_pad)


def cum_bwd(dcum, gf, b_pad, f_col0, n_heads):
    s = gf.shape[0]
    t = _tile(s, ATT_TILE, LANES)
    nb = s // t
    fb = f_col0 // LANES

    def kern(dc_ref, f_ref, b_ref, df_ref, db_ref, carry_ref):
        @pl.when(pl.program_id(0) == 0)
        def _():
            carry_ref[...] = jnp.zeros_like(carry_ref)
            db_ref[...] = jnp.zeros_like(db_ref)

        row, col = _iota2(t)
        dlf = _dot3_left(_mask01(col >= row), dc_ref[...]) + carry_ref[...]
        carry_ref[...] = dlf[0:1, :]
        v = f_ref[...] + b_ref[...]
        sig_neg = jnp.exp(-jnp.maximum(v, 0.0) - _log1p_exp_neg_abs(v))
        lane = lax.broadcasted_iota(jnp.int32, (t, LANES), 1)
        df = jnp.where(lane < n_heads, dlf * sig_neg, 0.0)
        df_ref[...] = df.astype(BF16)
        db_ref[...] += jnp.sum(df, axis=0, keepdims=True)

    return pl.pallas_call(
        kern, name="cum_bwd", grid=(nb,),
        in_specs=[pl.BlockSpec((t, LANES), lambda i: (nb - 1 - i, 0)),
                  pl.BlockSpec((t, LANES), lambda i: (nb - 1 - i, fb)),
                  pl.BlockSpec((1, LANES), lambda i: (0, 0))],
        out_specs=[pl.BlockSpec((t, LANES), lambda i: (nb - 1 - i, 0)), pl.BlockSpec((1, LANES), lambda i: (0, 0))],
        out_shape=[jax.ShapeDtypeStruct((s, LANES), BF16), jax.ShapeDtypeStruct((1, LANES), F32)],
        scratch_shapes=[pltpu.VMEM((1, LANES), F32)],
        compiler_params=_params("arbitrary"),
    )(dcum, gf, b_pad)


def _qkv_specs(s, t, n_heads, base):
    return [pl.BlockSpec((t, HEAD_DIM), lambda h, i: (i, base + h)),
            pl.BlockSpec((s, HEAD_DIM), lambda h, i: (0, base + n_heads + h)),
            pl.BlockSpec((s, HEAD_DIM), lambda h, i: (0, base + 2 * n_heads + h))]


def _strips(t):
    sr = _tile(t, ATT_STRIP, 8)
    return sr, t // sr, [slice(si * sr, (si + 1) * sr) for si in range(t // sr)]


def _key_minus_row(sr, t):
    return lax.broadcasted_iota(jnp.int32, (sr, t), 1) - lax.broadcasted_iota(jnp.int32, (sr, t), 0)


def _keep(valid, v):
    return v if valid is None else jnp.where(valid, v, 0.0)


def _sb_scores(q, k, diff, lim):
    z = lax.dot_general(q, k, NT, preferred_element_type=F32) * (HEAD_DIM ** -0.5)
    valid = None if lim is None else diff < lim
    l1p = _log1p_exp_neg_abs(z)
    return z, valid, l1p, _keep(valid, -jnp.maximum(z, 0.0) - l1p)


def sb_fwd(qkv, n_heads, base):
    s = qkv.shape[0]
    t = _tile(s, ATT_TILE, LANES)
    sr, ns, strips = _strips(t)

    def kern(q_ref, k_ref, v_ref, o_ref):
        i = pl.program_id(1)
        row, col = _iota2(t)
        after = _mask01(row > col)
        diff = _key_minus_row(sr, t)
        qs = [q_ref[sl, :] for sl in strips]

        def tile(j, carry, diagonal):
            runs, accs = carry
            off = pl.multiple_of(j * t, t)
            k = k_ref[pl.ds(off, t), :]
            v = v_ref[pl.ds(off, t), :]
            new_runs, new_accs = [], []
            for si in range(ns):
                z, valid, l1p, log_keep = _sb_scores(qs[si], k, diff, si * sr if diagonal else None)
                between = _dot2_right(log_keep, after) + runs[si]
                w = _keep(valid, jnp.exp(jnp.minimum(z, 0.0) - l1p + between))
                new_accs.append(accs[si] + jnp.dot(w.astype(BF16), v, preferred_element_type=F32))
                new_runs.append(runs[si] + jnp.sum(log_keep, axis=1, keepdims=True))
            return tuple(new_runs), tuple(new_accs)

        init = (tuple(jnp.zeros((sr, 1), F32) for _ in strips), tuple(jnp.zeros((sr, HEAD_DIM), F32) for _ in strips))
        _, accs = lax.fori_loop(0, i, lambda jj, c: tile(i - 1 - jj, c, False), tile(i, init, True))
        for sl, acc in zip(strips, accs):
            o_ref[sl, :] = acc.astype(o_ref.dtype)

    return pl.pallas_call(
        kern, name="sb_fwd", grid=(n_heads, s // t),
        in_specs=_qkv_specs(s, t, n_heads, base),
        out_specs=pl.BlockSpec((t, HEAD_DIM), lambda h, i: (i, h)),
        out_shape=jax.ShapeDtypeStruct((s, n_heads * HEAD_DIM), BF16),
        compiler_params=_params("parallel", "arbitrary"),
    )(qkv, qkv, qkv)


def sb_bwd(qkv, d_o, n_heads, base):
    s = qkv.shape[0]
    t = _tile(s, ATT_TILE, LANES)
    nq = s // t
    sr, ns, strips = _strips(t)
    scale = HEAD_DIM ** -0.5

    def kern(q_ref, k_ref, v_ref, do_ref, dq_ref, dk_ref, dv_ref, dk_acc, dv_acc, run_ref):
        i = pl.program_id(1)

        @pl.when(i == 0)
        def _():
            dk_acc[...] = jnp.zeros_like(dk_acc)
            dv_acc[...] = jnp.zeros_like(dv_acc)

        row, col = _iota2(t)
        after = _mask01(row > col)
        before = _mask01(row < col)
        diff = _key_minus_row(sr, t)
        qs = [q_ref[sl, :] for sl in strips]
        dos = [do_ref[sl, :] for sl in strips]

        def sweep1(j, runs, diagonal):
            k = k_ref[pl.ds(pl.multiple_of(j * t, t), t), :]
            new_runs = []
            for si, sl in enumerate(strips):
                _, _, _, log_keep = _sb_scores(qs[si], k, diff, si * sr if diagonal else None)
                run_ref[j, sl, :] = runs[si]
                new_runs.append(runs[si] + jnp.sum(log_keep, axis=1, keepdims=True))
            return tuple(new_runs)

        lax.fori_loop(0, i, lambda jj, c: sweep1(i - 1 - jj, c, False),
                      sweep1(i, tuple(jnp.zeros((sr, 1), F32) for _ in strips), True))

        def sweep2(j, carry, diagonal):
            run_es, dqs = carry
            off = pl.multiple_of(j * t, t)
            k = k_ref[pl.ds(off, t), :]
            v = v_ref[pl.ds(off, t), :]
            new_es, new_dqs = [], []
            dk_t = jnp.zeros((t, HEAD_DIM), F32)
            dv_t = jnp.zeros((t, HEAD_DIM), F32)
            for si, sl in enumerate(strips):
                z, valid, l1p, log_keep = _sb_scores(qs[si], k, diff, si * sr if diagonal else None)
                between = _dot2_right(log_keep, after) + run_ref[j, sl, :]
                w = _keep(valid, jnp.exp(jnp.minimum(z, 0.0) - l1p + between))
                dw = lax.dot_general(dos[si], v, NT, preferred_element_type=F32)
                e = dw * w
                e_before = _dot2_right(e, before) + run_es[si]
                keep = jnp.exp(log_keep)
                dz = _keep(valid, e * keep - e_before * (1.0 - keep)) * scale
                dzb = dz.astype(BF16)
                new_dqs.append(dqs[si] + jnp.dot(dzb, k, preferred_element_type=F32))
                dk_t = dk_t + lax.dot_general(dzb, qs[si], TN, preferred_element_type=F32)
                dv_t = dv_t + lax.dot_general(w.astype(BF16), dos[si], TN, preferred_element_type=F32)
                new_es.append(run_es[si] + jnp.sum(e, axis=1, keepdims=True))
            dk_acc[pl.ds(off, t), :] += dk_t
            dv_acc[pl.ds(off, t), :] += dv_t
            return tuple(new_es), tuple(new_dqs)

        init = (tuple(jnp.zeros((sr, 1), F32) for _ in strips), tuple(jnp.zeros((sr, HEAD_DIM), F32) for _ in strips))
        _, dqs = sweep2(i, lax.fori_loop(0, i, lambda j, c: sweep2(j, c, False), init), True)
        for sl, dq in zip(strips, dqs):
            dq_ref[sl, :] = dq.astype(BF16)

        @pl.when(i == nq - 1)
        def _():
            dk_ref[...] = dk_acc[...].astype(BF16)
            dv_ref[...] = dv_acc[...].astype(BF16)

    out = jax.ShapeDtypeStruct((s, n_heads * HEAD_DIM), BF16)
    head_blk = pl.BlockSpec((s, HEAD_DIM), lambda h, i: (0, h))
    tile_blk = pl.BlockSpec((t, HEAD_DIM), lambda h, i: (i, h))
    return pl.pallas_call(
        kern, name="sb_bwd", grid=(n_heads, nq),
        in_specs=_qkv_specs(s, t, n_heads, base) + [tile_blk],
        out_specs=[tile_blk, head_blk, head_blk],
        out_shape=[out, out, out],
        scratch_shapes=[pltpu.VMEM((s, HEAD_DIM), F32), pltpu.VMEM((s, HEAD_DIM), F32), pltpu.VMEM((nq, t, 1), F32)],
        compiler_params=_params("parallel", "arbitrary"),
    )(qkv, qkv, qkv, d_o)


def _fox_scores(q, k, cq, ck, diff, lim):
    sc = lax.dot_general(q, k, NT, preferred_element_type=F32) * (HEAD_DIM ** -0.5)
    sc = sc + cq - ck
    if lim is None:
        return sc, None
    valid = diff < lim
    return jnp.where(valid, sc, NEG_BIG), valid


def fox_fwd(qkv, cum_col, cum_row, n_heads, base):
    s = qkv.shape[0]
    t = _tile(s, ATT_TILE, LANES)
    sr, ns, strips = _strips(t)

    def kern(q_ref, k_ref, v_ref, cq_ref, ck_ref, o_ref, lse_ref):
        i = pl.program_id(1)
        diff = _key_minus_row(sr, t)
        qs = [q_ref[sl, :] for sl in strips]
        cqs = [cq_ref[0, sl, :] for sl in strips]

        def tile(j, carry, diagonal):
            off = pl.multiple_of(j * t, t)
            k = k_ref[pl.ds(off, t), :]
            v = v_ref[pl.ds(off, t), :]
            ck = ck_ref[0, :, pl.ds(off, t)]
            out = []
            for si in range(ns):
                m, l, acc = carry[si]
                sc, _ = _fox_scores(qs[si], k, cqs[si], ck, diff, si * sr + 1 if diagonal else None)
                m_new = jnp.maximum(m, jnp.max(sc, axis=1, keepdims=True))
                p = jnp.exp(sc - m_new)
                alpha = jnp.exp(m - m_new)
                l = alpha * l + jnp.sum(p, axis=1, keepdims=True)
                acc = alpha * acc + jnp.dot(p.astype(BF16), v, preferred_element_type=F32)
                out.append((m_new, l, acc))
            return tuple(out)

        init = tuple((jnp.full((sr, 1), NEG_BIG, F32), jnp.zeros((sr, 1), F32), jnp.zeros((sr, HEAD_DIM), F32))
                     for _ in strips)
        res = tile(i, lax.fori_loop(0, i, lambda j, c: tile(j, c, False), init), True)
        for sl, (m, l, acc) in zip(strips, res):
            o_ref[sl, :] = acc / l
            lse_ref[0, sl, :] = m + jnp.log(l)

    col_blk = pl.BlockSpec((1, t, 1), lambda h, i: (h, i, 0))
    return pl.pallas_call(
        kern, name="fox_fwd", grid=(n_heads, s // t),
        in_specs=_qkv_specs(s, t, n_heads, base) + [col_blk, pl.BlockSpec((1, 1, s), lambda h, i: (h, 0, 0))],
        out_specs=[pl.BlockSpec((t, HEAD_DIM), lambda h, i: (i, h)), col_blk],
        out_shape=[jax.ShapeDtypeStruct((s, n_heads * HEAD_DIM), F32), jax.ShapeDtypeStruct((n_heads, s, 1), F32)],
        compiler_params=_params("parallel", "arbitrary"),
    )(qkv, qkv, qkv, cum_col, cum_row)


def fox_bwd(qkv, cum_col, cum_row, o, d_o, lse, n_heads, base):
    s = qkv.shape[0]
    t = _tile(s, ATT_TILE, LANES)
    nq = s // t
    sr, ns, strips = _strips(t)
    scale = HEAD_DIM ** -0.5

    def kern(q_ref, k_ref, v_ref, cq_ref, ck_ref, o_ref, do_ref, lse_ref,
             dq_ref, dk_ref, dv_ref, dcq_ref, dck_ref, dk_acc, dv_acc, dck_acc):
        i = pl.program_id(1)

        @pl.when(i == 0)
        def _():
            dk_acc[...] = jnp.zeros_like(dk_acc)
            dv_acc[...] = jnp.zeros_like(dv_acc)
            dck_acc[...] = jnp.zeros_like(dck_acc)

        diff = _key_minus_row(sr, t)
        qs = [q_ref[sl, :] for sl in strips]
        dos = [do_ref[sl, :] for sl in strips]
        cqs = [cq_ref[0, sl, :] for sl in strips]
        lses = [lse_ref[0, sl, :] for sl in strips]
        deltas = [jnp.sum(dos[si].astype(F32) * o_ref[sl, :], axis=1, keepdims=True) for si, sl in enumerate(strips)]

        def tile(j, carry, diagonal):
            off = pl.multiple_of(j * t, t)
            k = k_ref[pl.ds(off, t), :]
            v = v_ref[pl.ds(off, t), :]
            ck = ck_ref[0, :, pl.ds(off, t)]
            out = []
            dk_t = jnp.zeros((t, HEAD_DIM), F32)
            dv_t = jnp.zeros((t, HEAD_DIM), F32)
            dck_t = jnp.zeros((1, t), F32)
            for si in range(ns):
                dq, dcq = carry[si]
                sc, valid = _fox_scores(qs[si], k, cqs[si], ck, diff, si * sr + 1 if diagonal else None)
                p = _keep(valid, jnp.exp(sc - lses[si]))
                dp = lax.dot_general(dos[si], v, NT, preferred_element_type=F32)
                ds = p * (dp - deltas[si])
                dsb = (ds * scale).astype(BF16)
                dq = dq + jnp.dot(dsb, k, preferred_element_type=F32)
                dk_t = dk_t + lax.dot_general(dsb, qs[si], TN, preferred_element_type=F32)
                dv_t = dv_t + lax.dot_general(p.astype(BF16), dos[si], TN, preferred_element_type=F32)
                dck_t = dck_t + jnp.sum(ds, axis=0, keepdims=True)
                out.append((dq, dcq + jnp.sum(ds, axis=1, keepdims=True)))
            dk_acc[pl.ds(off, t), :] += dk_t
            dv_acc[pl.ds(off, t), :] += dv_t
            dck_acc[:, pl.ds(off, t)] -= dck_t
            return tuple(out)

        init = tuple((jnp.zeros((sr, HEAD_DIM), F32), jnp.zeros((sr, 1), F32)) for _ in strips)
        res = tile(i, lax.fori_loop(0, i, lambda j, c: tile(j, c, False), init), True)
        for sl, (dq, dcq) in zip(strips, res):
            dq_ref[sl, :] = dq.astype(BF16)
            dcq_ref[0, sl, :] = dcq

        @pl.when(i == nq - 1)
        def _():
            dk_ref[...] = dk_acc[...].astype(BF16)
            dv_ref[...] = dv_acc[...].astype(BF16)
            dck_ref[0] = dck_acc[...]

    out = jax.ShapeDtypeStruct((s, n_heads * HEAD_DIM), BF16)
    head_blk = pl.BlockSpec((s, HEAD_DIM), lambda h, i: (0, h))
    tile_blk = pl.BlockSpec((t, HEAD_DIM), lambda h, i: (i, h))
    col_blk = pl.BlockSpec((1, t, 1), lambda h, i: (h, i, 0))
    row_blk = pl.BlockSpec((1, 1, s), lambda h, i: (h, 0, 0))
    return pl.pallas_call(
        kern, name="fox_bwd", grid=(n_heads, nq),
        in_specs=_qkv_specs(s, t, n_heads, base) + [col_blk, row_blk, tile_blk, tile_blk, col_blk],
        out_specs=[tile_blk, head_blk, head_blk, col_blk, row_blk],
        out_shape=[out, out, out, jax.ShapeDtypeStruct((n_heads, s, 1), F32),
                   jax.ShapeDtypeStruct((n_heads, 1, s), F32)],
        scratch_shapes=[pltpu.VMEM((s, HEAD_DIM), F32), pltpu.VMEM((s, HEAD_DIM), F32), pltpu.VMEM((1, s), F32)],
        compiler_params=_params("parallel", "arbitrary"),
    )(qkv, qkv, qkv, cum_col, cum_row, o, d_o, lse)


def _place():
    x, y, c = lax.axis_index("x"), lax.axis_index("y"), lax.axis_index("c")
    other_chips = [(1 - x, y), (x, 1 - y), (1 - x, 1 - y)]
    return x, y, c, other_chips


ANY = pl.BlockSpec(memory_space=pl.ANY)


def _remote(src, dst, send_sem, recv_sem, dev):
    return pltpu.make_async_remote_copy(src_ref=src, dst_ref=dst, send_sem=send_sem, recv_sem=recv_sem,
                                        device_id=dev, device_id_type=MESH)


def place_transposed(name, w_t, chip):
    c, _, r = w_t.shape
    tc = LANES

    def kern(chip_ref, w_ref, o_ref):
        o_ref[...] = w_ref[:, 0, :].T.astype(BF16)

    return pl.pallas_call(
        kern, name=name,
        grid_spec=pltpu.PrefetchScalarGridSpec(
            num_scalar_prefetch=1, grid=(pl.cdiv(c, tc),),
            in_specs=[pl.BlockSpec((tc, 1, r), lambda j, chip_ref: (j, 0, 0))],
            out_specs=pl.BlockSpec((None, r, tc), lambda j, chip_ref: (chip_ref[0], 0, j))),
        out_shape=jax.ShapeDtypeStruct((N_CHIPS, r, c), BF16),
        compiler_params=_params("parallel"),
    )(chip, w_t)


def adam_update_transposed(name, w_t, m_t, v_t, g_buf):
    c, _, r = w_t.shape
    tc = LANES

    def kern(w_ref, m_ref, v_ref, g_ref, go_ref, dl_ref, nm_ref, nv_ref):
        g = g_ref[...].T
        delta, nm, nv = _adam(w_ref[:, 0, :], g, m_ref[:, 0, :], v_ref[:, 0, :])
        go_ref[:, 0, :] = g
        dl_ref[:, 0, :] = delta
        nm_ref[:, 0, :] = nm
        nv_ref[:, 0, :] = nv

    blk = pl.BlockSpec((tc, 1, r), lambda j: (j, 0, 0))
    out = jax.ShapeDtypeStruct((c, 1, r), F32)
    return pl.pallas_call(
        kern, name=name, grid=(pl.cdiv(c, tc),),
        in_specs=[blk, blk, blk, pl.BlockSpec((r, tc), lambda j: (0, j))],
        out_specs=[blk] * 4, out_shape=[out] * 4, compiler_params=_params("parallel"),
    )(w_t, m_t, v_t, g_buf)


def cast_place(name, ws, chip):
    r = ws[0].shape[1]
    cs = [w.shape[2] for w in ws]
    tr = _tile(r, 256, 16)

    def kern(chip_ref, *refs):
        o_ref = refs[-1]
        off = 0
        for w_ref, c in zip(refs[:-1], cs):
            o_ref[:, off:off + c] = w_ref[...].astype(BF16)
            off += c

    return pl.pallas_call(
        kern, name=name,
        grid_spec=pltpu.PrefetchScalarGridSpec(
            num_scalar_prefetch=1, grid=(r // tr,),
            in_specs=[pl.BlockSpec((None, tr, c), lambda i, chip_ref: (0, i, 0)) for c in cs],
            out_specs=pl.BlockSpec((None, tr, sum(cs)), lambda i, chip_ref: (chip_ref[0], i, 0))),
        out_shape=jax.ShapeDtypeStruct((N_CHIPS, r, sum(cs)), BF16),
        compiler_params=_params("parallel"),
    )(chip, *ws)


HBM = pl.BlockSpec(memory_space=pltpu.HBM)
SEM = pl.BlockSpec(memory_space=pltpu.SEMAPHORE)
SPLIT = pltpu.CompilerParams(has_side_effects=pltpu.SideEffectType.DATAFLOW_SIDE_EFFECTING)


def _in_hbm(a):
    return pltpu.with_memory_space_constraint(a, pltpu.HBM)


def _slab_rows(ref, k, core):
    half = ref.shape[1] // 2
    return ref.at[k, pl.ds(pl.multiple_of(core * half, 16), half)]


def gather_start(name, bufs):
    n = len(bufs)

    def body(*refs):
        ins, send, recv, token = refs[:n], refs[n], refs[n + 1], refs[-1]
        x, y, c, chips = _place()
        me = 2 * x + y
        for a in range(n):
            for j in range(3):
                rows = _slab_rows(ins[a], me, c)
                _remote(rows, rows, send.at[3 * a + j], recv.at[3 * a + j], (chips[j][0], chips[j][1], c)).start()
        token[...] = jnp.zeros_like(token)

    sem = pltpu.SemaphoreType.DMA((3 * n,))
    res = pl.pallas_call(
        body, name=name, in_specs=[HBM] * n, out_specs=[SEM, SEM] + [HBM] * n + [pl.BlockSpec(memory_space=pltpu.VMEM)],
        out_shape=[sem, sem] + [pltpu.HBM(b.shape, b.dtype) for b in bufs] + [jax.ShapeDtypeStruct((8, LANES), F32)],
        input_output_aliases={a: 2 + a for a in range(n)}, compiler_params=SPLIT,
    )(*[_in_hbm(b) for b in bufs])
    return res[0], res[1], res[2:2 + n], res[-1]


def gather_wait(name, bufs, send_sems, recv_sems, after):
    n = len(bufs)

    def body(*refs):
        ins, send, recv = refs[:n], refs[n], refs[n + 1]
        x, y, c, chips = _place()
        me = 2 * x + y
        for a in range(n):
            for j in range(3):
                dev = (chips[j][0], chips[j][1], c)
                mine = _slab_rows(ins[a], me, c)
                _remote(mine, mine, send.at[3 * a + j], recv.at[3 * a + j], dev).wait_send()
                land = _slab_rows(ins[a], 2 * chips[j][0] + chips[j][1], c)
                _remote(land, land, send.at[3 * a + j], recv.at[3 * a + j], dev).wait_recv()

    return pl.pallas_call(
        body, name=name, in_specs=[HBM] * n + [SEM, SEM] + [ANY] * len(after), out_specs=[HBM] * n,
        out_shape=[pltpu.HBM(b.shape, b.dtype) for b in bufs],
        input_output_aliases={a: a for a in range(n)}, compiler_params=SPLIT,
    )(*bufs, send_sems, recv_sems, *after)


def gather_forward(name, bufs):
    n = len(bufs)

    def body(*refs):
        outs = refs[n:2 * n]
        send_sems, recv_sems = refs[2 * n:]
        x, y, c, chips = _place()
        sibling = (x, y, 1 - c)

        def d2d(a, j, core):
            rows = _slab_rows(outs[a], 2 * chips[j][0] + chips[j][1], core)
            return _remote(rows, rows, send_sems.at[3 * a + j], recv_sems.at[3 * a + j], sibling)

        pairs = [(a, j) for a in range(n) for j in range(3)]
        for a, j in pairs:
            d2d(a, j, c).start()
        for a, j in pairs:
            d2d(a, j, 1 - c).wait_recv()
        for a, j in pairs:
            d2d(a, j, c).wait_send()

    return pl.pallas_call(
        body, name=name, in_specs=[ANY] * n, out_specs=[ANY] * n,
        out_shape=[jax.ShapeDtypeStruct(b.shape, b.dtype) for b in bufs],
        input_output_aliases={a: a for a in range(n)},
        scratch_shapes=[pltpu.SemaphoreType.DMA((3 * n,)), pltpu.SemaphoreType.DMA((3 * n,))],
    )(*bufs)


def _forward_plan(refs):
    x, y, c, chips = _place()
    out = []
    for ref in refs:
        for j in range(3):
            k = 2 * chips[j][0] + chips[j][1]
            out.append((_slab_rows(ref, k, c), _slab_rows(ref, k, c), _slab_rows(ref, k, 1 - c)))
    return out


def _swap_plan(refs):
    x, y, c, _ = _place()
    n = len(refs) // 2
    out = []
    for a in range(n):
        half = refs[a].shape[1] // 2
        src = refs[a].at[:, pl.ds(pl.multiple_of((1 - c) * half, 16), half), :]
        out.append((src, refs[n + a], refs[n + a]))
    return out


def sibling_start(name, arrays, plan, n_copies, after=()):
    n = len(arrays)

    n_in = n + len(after)

    def body(*refs):
        send, recv, token = refs[n_in], refs[n_in + 1], refs[-1]
        x, y, c, _ = _place()
        for idx, (src, dst, _) in enumerate(plan(refs[:n])):
            _remote(src, dst, send.at[idx], recv.at[idx], (x, y, 1 - c)).start()
        token[...] = jnp.zeros_like(token)

    sem = pltpu.SemaphoreType.DMA((n_copies,))
    res = pl.pallas_call(
        body, name=name, in_specs=[HBM] * n + [ANY] * len(after),
        out_specs=[SEM, SEM] + [HBM] * n + [pl.BlockSpec(memory_space=pltpu.VMEM)],
        out_shape=[sem, sem] + [pltpu.HBM(b.shape, b.dtype) for b in arrays] + [jax.ShapeDtypeStruct((8, LANES), F32)],
        input_output_aliases={a: 2 + a for a in range(n)}, compiler_params=SPLIT,
    )(*[_in_hbm(b) for b in arrays], *after)
    return res[0], res[1], res[2:2 + n], res[-1]


def sibling_wait(name, arrays, send_sems, recv_sems, plan, after):
    n = len(arrays)

    def body(*refs):
        send, recv = refs[n], refs[n + 1]
        x, y, c, _ = _place()
        for idx, (src, dst, filled) in enumerate(plan(refs[:n])):
            _remote(src, dst, send.at[idx], recv.at[idx], (x, y, 1 - c)).wait_send()
            _remote(filled, filled, send.at[idx], recv.at[idx], (x, y, 1 - c)).wait_recv()

    return pl.pallas_call(
        body, name=name, in_specs=[HBM] * n + [SEM, SEM] + [ANY] * len(after), out_specs=[HBM] * n,
        out_shape=[pltpu.HBM(b.shape, b.dtype) for b in arrays],
        input_output_aliases={a: a for a in range(n)}, compiler_params=SPLIT,
    )(*arrays, send_sems, recv_sems, *after)


def swap_halves(name, pieces):
    n = len(pieces)
    halves = [p.shape[1] // 2 for p in pieces]

    def body(*refs):
        ins, outs = refs[:n], refs[n:2 * n]
        send_sems, recv_sems = refs[2 * n:]
        x, y, c, _ = _place()
        cps = [_remote(ins[a].at[:, pl.ds(pl.multiple_of((1 - c) * halves[a], 16), halves[a]), :], outs[a],
                       send_sems.at[a], recv_sems.at[a], (x, y, 1 - c)) for a in range(n)]
        for cp in cps:
            cp.start()
        for cp in cps:
            cp.wait()

    return pl.pallas_call(
        body, name=name, in_specs=[ANY] * n, out_specs=[ANY] * n,
        out_shape=[jax.ShapeDtypeStruct((N_CHIPS, h, p.shape[2]), p.dtype) for p, h in zip(pieces, halves)],
        scratch_shapes=[pltpu.SemaphoreType.DMA((n,)), pltpu.SemaphoreType.DMA((n,))],
    )(*pieces)


def pair_sum(name, pieces, got, core):
    _, r, w = pieces.shape
    half = r // 2
    tr = _tile(half, 256, 16)

    def kern(core_ref, p_ref, g_ref, o_ref):
        o_ref[...] = (p_ref[...].astype(F32) + g_ref[...].astype(F32)).astype(o_ref.dtype)

    return pl.pallas_call(
        kern, name=name,
        grid_spec=pltpu.PrefetchScalarGridSpec(
            num_scalar_prefetch=1, grid=(N_CHIPS, half // tr),
            in_specs=[pl.BlockSpec((None, None, tr, w), lambda k, i, core_ref: (k, core_ref[0], i, 0)),
                      pl.BlockSpec((None, tr, w), lambda k, i, core_ref: (k, i, 0))],
            out_specs=pl.BlockSpec((None, tr, w), lambda k, i, core_ref: (k, i, 0))),
        out_shape=jax.ShapeDtypeStruct((N_CHIPS, half, w), pieces.dtype),
        compiler_params=_params("parallel", "parallel"),
    )(core, pieces.reshape(N_CHIPS, 2, half, w), got)


def _scatter_copies(sums, lands, send, recv):
    x, y, c, chips = _place()
    return [_remote(sums[a].at[2 * chips[j][0] + chips[j][1]], lands[a].at[j], send.at[3 * a + j], recv.at[3 * a + j],
                    (chips[j][0], chips[j][1], c)) for a in range(len(sums)) for j in range(3)]


def scatter_start(name, sums):
    n = len(sums)
    lands = [lax.empty((3,) + t.shape[1:], t.dtype) for t in sums]

    def body(*refs):
        ins, land_in, send, recv, token = refs[:n], refs[n:2 * n], refs[2 * n], refs[2 * n + 1], refs[-1]
        for cp in _scatter_copies(ins, land_in, send, recv):
            cp.start()
        token[...] = jnp.zeros_like(token)

    sem = pltpu.SemaphoreType.DMA((3 * n,))
    res = pl.pallas_call(
        body, name=name, in_specs=[HBM] * (2 * n),
        out_specs=[SEM, SEM] + [HBM] * (2 * n) + [pl.BlockSpec(memory_space=pltpu.VMEM)],
        out_shape=[sem, sem] + [pltpu.HBM(t.shape, t.dtype) for t in sums + lands] + [jax.ShapeDtypeStruct((8, LANES), F32)],
        input_output_aliases={a: 2 + a for a in range(2 * n)}, compiler_params=SPLIT,
    )(*[_in_hbm(t) for t in sums + lands])
    return res[0], res[1], res[2:2 + n], res[2 + n:2 + 2 * n], res[-1]


def scatter_wait(name, sums, lands, send_sems, recv_sems, after):
    n = len(sums)

    def body(*refs):
        ins, land_in, send, recv = refs[:n], refs[n:2 * n], refs[2 * n], refs[2 * n + 1]
        for cp in _scatter_copies(ins, land_in, send, recv):
            cp.wait_send()
            cp.wait_recv()

    res = pl.pallas_call(
        body, name=name, in_specs=[HBM] * (2 * n) + [SEM, SEM, ANY], out_specs=[HBM] * (2 * n),
        out_shape=[pltpu.HBM(t.shape, t.dtype) for t in sums + lands],
        input_output_aliases={a: a for a in range(2 * n)}, compiler_params=SPLIT,
    )(*sums, *lands, send_sems, recv_sems, after)
    return res[:n], res[n:]


def chip_sum(name, sums, got, chip, core):
    _, half, w = sums.shape
    tr = _tile(half, 256, 16)
    nb = half // tr

    def kern(ids_ref, s_ref, g0_ref, g1_ref, g2_ref, o_ref):
        o_ref[...] = ((s_ref[...].astype(F32) + g0_ref[...].astype(F32)) + g1_ref[...].astype(F32)) \
            + g2_ref[...].astype(F32)

    def got_spec(j):
        return pl.BlockSpec((None, tr, w), lambda i, ids_ref: (j, i, 0))

    return pl.pallas_call(
        kern, name=name,
        grid_spec=pltpu.PrefetchScalarGridSpec(
            num_scalar_prefetch=1, grid=(nb,),
            in_specs=[pl.BlockSpec((None, tr, w), lambda i, ids_ref: (ids_ref[0], i, 0)),
                      got_spec(0), got_spec(1), got_spec(2)],
            out_specs=pl.BlockSpec((tr, w), lambda i, ids_ref: (ids_ref[1] * nb + i, 0))),
        out_shape=jax.ShapeDtypeStruct((2 * half, w), F32),
        compiler_params=_params("parallel"),
    )(jnp.concatenate([chip, core]), sums, got, got, got)


def join_halves(name, shards):
    n = len(shards)
    halves = [g.shape[0] // 2 for g in shards]

    def body(*refs):
        outs = refs[n:2 * n]
        send_sems, recv_sems = refs[2 * n:]
        x, y, c, _ = _place()
        cps = []
        for a in range(n):
            rows = outs[a].at[pl.ds(pl.multiple_of(c * halves[a], 8), halves[a])]
            cps.append(_remote(rows, rows, send_sems.at[a], recv_sems.at[a], (x, y, 1 - c)))
        for cp in cps:
            cp.start()
        for cp in cps:
            cp.wait()

    return pl.pallas_call(
        body, name=name, in_specs=[ANY] * n, out_specs=[ANY] * n,
        out_shape=[jax.ShapeDtypeStruct(g.shape, g.dtype) for g in shards],
        input_output_aliases={a: a for a in range(n)},
        scratch_shapes=[pltpu.SemaphoreType.DMA((n,)), pltpu.SemaphoreType.DMA((n,))],
    )(*shards)


def _adam(w, g, m, v):
    m = ADAM_B1 * m + (1.0 - ADAM_B1) * g
    v = ADAM_B2 * v + (1.0 - ADAM_B2) * (g * g)
    m_hat = m / (1.0 - ADAM_B1 ** ADAM_STEP)
    v_hat = v / (1.0 - ADAM_B2 ** ADAM_STEP)
    delta = -ADAM_LR * (m_hat / (jnp.sqrt(v_hat) + ADAM_EPS) + ADAM_WD * w)
    return delta, m, v


def small_allreduce_adam(g_part, w, m, v, after):
    n_dev = 8
    r, d = g_part.shape

    def body(g_ref, w_ref, m_ref, v_ref, after_ref, gs_ref, dl_ref, nm_ref, nv_ref, all_ref, send_sems, recv_sems):
        x, y, c, _ = _place()
        me = 4 * x + 2 * y + c
        all_ref[me] = g_ref[...]
        cps = []
        for rel in range(1, n_dev):
            px = 1 - x if rel & 4 else x
            py = 1 - y if rel & 2 else y
            pc = 1 - c if rel & 1 else c
            cps.append(_remote(g_ref, all_ref.at[me], send_sems.at[rel - 1], recv_sems.at[rel - 1], (px, py, pc)))
        for cp in cps:
            cp.start()
        for cp in cps:
            cp.wait()
        total = all_ref[0]
        for dev in range(1, n_dev):
            total = total + all_ref[dev]
        gs_ref[...] = total
        delta, nm, nv = _adam(w_ref[...], total, m_ref[...], v_ref[...])
        dl_ref[...] = delta
        nm_ref[...] = nm
        nv_ref[...] = nv

    vm = pl.BlockSpec(memory_space=pltpu.VMEM)
    out = jax.ShapeDtypeStruct((r, d), F32)
    return pl.pallas_call(
        body, name="small_allreduce_adam", in_specs=[vm, vm, vm, vm, ANY], out_specs=[vm, vm, vm, vm],
        out_shape=[out, out, out, out],
        scratch_shapes=[pltpu.VMEM((n_dev, r, d), F32), pltpu.SemaphoreType.DMA((n_dev - 1,)),
                        pltpu.SemaphoreType.DMA((n_dev - 1,))],
    )(g_part, w, m, v, after)


def adam_update(name, w, m, v, g_buf, col_blk, after=None):
    w, m, v = w[0], m[0], v[0]
    r, c = w.shape
    tr = _tile(r, 128, 8)
    extra = [] if after is None else [after]

    def kern(w_ref, m_ref, v_ref, g_ref, *rest):
        go_ref, dl_ref, nm_ref, nv_ref = rest[len(extra):]
        g = g_ref[...]
        delta, nm, nv = _adam(w_ref[...], g, m_ref[...], v_ref[...])
        go_ref[...] = g
        dl_ref[...] = delta
        nm_ref[...] = nm
        nv_ref[...] = nv

    blk = pl.BlockSpec((tr, c), lambda i: (i, 0))
    out = jax.ShapeDtypeStruct((r, c), F32)
    res = pl.pallas_call(
        kern, name=name, grid=(r // tr,),
        in_specs=[blk, blk, blk, pl.BlockSpec((tr, c), lambda i: (i, col_blk))] + [ANY] * len(extra),
        out_specs=[blk] * 4, out_shape=[out] * 4, compiler_params=_params("parallel"),
    )(w, m, v, g_buf, *extra)
    return [a[None] for a in res]


def _w_in_segments(cw, n_qkv, n_heads, d):
    out = []

    def add(lo, hi, main):
        while lo < hi:
            k, a = divmod(lo, cw)
            w = min(cw - a, hi - lo)
            out.append((k, a, main, w))
            lo, main = lo + w, main + w

    add(0, n_qkv, 0)
    add(n_qkv + n_heads, N_CHIPS * cw, n_qkv)
    add(n_qkv, n_qkv + n_heads, n_qkv + 2 * d)
    return out


def regroup_w_in(g_in, segments, n_main):
    _, d, cw = g_in.shape
    tr = _tile(d, 128, 16)
    n_real = max(m + w for _, _, m, w in segments)

    def kern(s_ref, o_ref):
        for k, a, m, w in segments:
            o_ref[:, m:m + w] = s_ref[k, :, a:a + w]
        o_ref[:, n_real:] = jnp.zeros((tr, n_main - n_real), o_ref.dtype)

    return pl.pallas_call(
        kern, name="regroup_w_in", grid=(d // tr,),
        in_specs=[pl.BlockSpec((N_CHIPS, tr, cw), lambda i: (0, i, 0))],
        out_specs=pl.BlockSpec((tr, n_main), lambda i: (i, 0)),
        out_shape=jax.ShapeDtypeStruct((d, n_main), g_in.dtype), compiler_params=_params("parallel"),
    )(g_in)


def regroup_dw_in(dw_main, segments, cw):
    d, n_main = dw_main.shape
    tr = _tile(d, 128, 16)

    def kern(s_ref, o_ref):
        for k, a, m, w in segments:
            o_ref[k, :, a:a + w] = s_ref[:, m:m + w]

    return pl.pallas_call(
        kern, name="regroup_dw_in", grid=(d // tr,),
        in_specs=[pl.BlockSpec((tr, n_main), lambda i: (i, 0))],
        out_specs=pl.BlockSpec((N_CHIPS, tr, cw), lambda i: (0, i, 0)),
        out_shape=jax.ShapeDtypeStruct((N_CHIPS, d, cw), dw_main.dtype), compiler_params=_params("parallel"),
    )(dw_main)


def kernel(x, norm_mix_pre, norm_mix_post, w_in, b_forget, w_branch_sb, w_branch_fox, w_out, norm_ffn_pre, norm_ffn_post, w_ffn_gate, w_ffn_up, w_ffn_down, loss_target, m_norm_mix_pre, m_norm_mix_post, m_w_in, m_b_forget, m_w_branch_sb, m_w_branch_fox, m_w_out, m_norm_ffn_pre, m_norm_ffn_post, m_w_ffn_gate, m_w_ffn_up, m_w_ffn_down, v_norm_mix_pre, v_norm_mix_post, v_w_in, v_b_forget, v_w_branch_sb, v_w_branch_fox, v_w_out, v_norm_ffn_pre, v_norm_ffn_post, v_w_ffn_gate, v_w_ffn_up, v_w_ffn_down):
    s, d = x.shape[1], x.shape[2]
    n_heads = b_forget.shape[1]
    d_att = n_heads * HEAD_DIM
    c_in = w_in.shape[2]
    c_br = w_branch_sb.shape[2]
    c_gu = w_ffn_gate.shape[2]
    d_ff = c_gu * N_CHIPS
    d_in = c_in * N_CHIPS
    f_pad = 512
    n_qkv = 6 * d_att
    n_gf = 2 * d + f_pad
    core = lax.axis_index("c").astype(jnp.int32).reshape(1)
    chip = (2 * lax.axis_index("x") + lax.axis_index("y")).astype(jnp.int32).reshape(1)

    as_t = lambda a: jnp.transpose(a, (2, 0, 1))
    w_in_t, m_in_t, v_in_t = as_t(w_in), as_t(m_w_in), as_t(v_w_in)
    in_send, in_recv, in_bufs, in_token = gather_start(
        "gather_start_w_in", [place_transposed("place_w_in", w_in_t, chip)])
    sm_send, sm_recv, sm_bufs, sm_token = gather_start("gather_start_small", [
        cast_place("place_branch", [w_branch_sb, w_branch_fox], chip),
        cast_place("place_out", [w_out + in_token[0, 0]], chip)])
    ag_send, ag_recv, ag_bufs, ag_token = gather_start("gather_start_ffn", [
        cast_place("place_gate_up", [w_ffn_gate, w_ffn_up], chip),
        cast_place("place_down", [w_ffn_down + sm_token[0, 0]], chip)])
    g_in, = gather_forward("forward_w_in", gather_wait("gather_wait_w_in", in_bufs, in_send, in_recv,
                                                       [ag_token]))
    segments = _w_in_segments(c_in, n_qkv, n_heads, d)
    w_main = regroup_w_in(g_in, segments, n_qkv + n_gf)
    x2 = x[0]
    tgt = loss_target[0]
    b_pad = jnp.pad(b_forget, ((0, 0), (0, LANES - n_heads)))

    u = norm_in(x2, norm_mix_pre)
    qkv = mm(u, w_main, "nn", BF16, "proj_qkv", b_win=(0, n_qkv))
    gf = mm(u, w_main, "nn", F32, "proj_gates", b_win=(n_qkv, n_gf))
    cum = cum_fwd(gf, b_pad, 2 * d)
    cum_heads = cum[:, :n_heads].T
    cum_col, cum_row = cum_heads[:, :, None], cum_heads[:, None, :]
    o_sb = sb_fwd(qkv, n_heads, 0)
    o_fx, lse = fox_fwd(qkv, cum_col, cum_row, n_heads, 3 * n_heads)
    g_br, g_out = gather_forward("forward_small",
                                 gather_wait("gather_wait_small", sm_bufs, sm_send, sm_recv, [o_sb, o_fx]))
    w_o = g_out.reshape(d, d)
    bsb = mm(o_sb, g_br, "nn", F32, "branch_sb", tn=c_br, chunks=(1, 0))
    bfx = mm(o_fx, g_br, "nn", F32, "branch_fox", tn=c_br, chunks=(1, 1))
    merged = gate_fwd(bsb, bfx, gf)
    ffn_bufs = gather_wait("gather_wait_ffn", ag_bufs, ag_send, ag_recv, [merged])
    fb_send, fb_recv, fb_bufs, fb_token = sibling_start("forward_big_start", ffn_bufs, _forward_plan, 6)
    mix = mm(merged, w_o, "nn", F32, "out_proj", after=fb_token)
    h1, u2 = mid_fwd(x2, mix, norm_mix_post, norm_ffn_pre)
    g_gu, g_dn = sibling_wait("forward_big_wait", fb_bufs, fb_send, fb_recv, _forward_plan, [u2])
    w_dn = g_dn.reshape(d_ff, d)
    gu, act = ffn_up_fused(u2, g_gu, c_gu)
    ff = mm(act, w_dn, "nn", F32, "ffn_down")
    dy, d_ff_out, dg_fpost, loss_part = loss_head(h1, ff, norm_ffn_post, tgt)

    p_dn = mm(act, d_ff_out, "tn", BF16, "dw_ffn_down").reshape(N_CHIPS, d_ff // N_CHIPS, d)
    d_gu = ffn_down_bwd_fused(d_ff_out, w_dn, gu, c_gu)
    p_gu = mm(u2, d_gu, "tn", BF16, "dw_ffn_gate_up", tn=c_gu, chunks=(2, 0),
              out_into=lax.empty((N_CHIPS, d, 2 * c_gu), BF16))
    sw_send, sw_recv, sw_arrs, sw_token = sibling_start(
        "swap_big_start", [p_gu, p_dn, lax.empty((N_CHIPS, d // 2, 2 * c_gu), BF16),
                           lax.empty((N_CHIPS, d_ff // N_CHIPS // 2, d), BF16)], _swap_plan, 2)
    du2 = mm(d_gu, g_gu, "nt", F32, "d_u2", tk=c_gu, chunks=(2, 0), after=sw_token)
    dh1, d_mix, dg_fpre, dg_post = mid_bwd(dy, du2, h1, mix, norm_ffn_pre, norm_mix_post)
    p_out = mm(merged, d_mix, "tn", BF16, "dw_out").reshape(N_CHIPS, d // N_CHIPS, d)
    d_merged = mm(d_mix, w_o, "nt", F32, "d_merged")
    d_bsb, d_bfx, d_gs, d_gx = gate_bwd(d_merged, bsb, bfx, gf)
    p_br = mm(o_sb, d_bsb, "tn", BF16, "dw_branch_sb", tn=c_br, chunks=(1, 0),
              out_into=lax.empty((N_CHIPS, d_att, 2 * c_br), BF16))
    p_br = mm(o_fx, d_bfx, "tn", BF16, "dw_branch_fox", tn=c_br, chunks=(1, 1), out_into=p_br)
    d_osb = mm(d_bsb, g_br, "nt", BF16, "d_o_sb", tk=c_br, chunks=(1, 0))
    d_ofx = mm(d_bfx, g_br, "nt", BF16, "d_o_fox", tk=c_br, chunks=(1, 1))

    def reduce_start(tag, pieces, names):
        from_sibling = swap_halves("swap_halves_" + tag, pieces)
        sums = [pair_sum("pair_sum_" + t, p, q, core) for t, p, q in zip(names, pieces, from_sibling)]
        return scatter_start("scatter_start_" + tag, sums)

    def reduce_end(tag, started, names, after):
        send, recv, sums, lands, _ = started
        sums, lands = scatter_wait("scatter_wait_" + tag, sums, lands, send, recv, after)
        return join_halves("join_halves_" + tag, [chip_sum("chip_sum_" + t, sm, got, chip, core)
                                                  for t, sm, got in zip(names, sums, lands)])

    rest_names = ["branch", "out", "gate_up", "down"]
    p_gu, p_dn, q_gu, q_dn = sibling_wait("swap_big_wait", sw_arrs, sw_send, sw_recv, _swap_plan, [p_br])
    q_br, q_out = swap_halves("swap_halves_small", [p_br, p_out])
    rest_started = scatter_start("scatter_start_rest", [
        pair_sum("pair_sum_" + t, p, q, core)
        for t, p, q in zip(rest_names, [p_br, p_out, p_gu, p_dn], [q_br, q_out, q_gu, q_dn])])
    d_osb = d_osb + rest_started[4][0, 0].astype(BF16)
    dq_s, dk_s, dv_s = sb_bwd(qkv, d_osb, n_heads, 0)
    dq_f, dk_f, dv_f, dcq, dck = fox_bwd(qkv, cum_col, cum_row, o_fx, d_ofx, lse, n_heads, 3 * n_heads)
    d_cum = jnp.pad((dcq[:, :, 0] + dck[:, 0, :]).T, ((0, 0), (0, LANES - n_heads)))
    d_f, db_pad = cum_bwd(d_cum, gf, b_pad, 2 * d, n_heads)
    d_main = jnp.concatenate(
        [dq_s, dk_s, dv_s, dq_f, dk_f, dv_f, d_gs, d_gx, d_f, jnp.zeros((s, f_pad - LANES), BF16)], axis=1)
    p_in = regroup_dw_in(mm(u, d_main, "tn", BF16, "dw_in"), segments, c_in)

    in_started = reduce_start("w_in", [p_in], ["in"])
    du = mm(d_main, w_main, "nt", F32, "d_u", after=in_started[4])
    dx, dg_pre = in_bwd(dh1, du, x2, norm_mix_pre + in_started[4][0:1, 0:1])
    gr_br, gr_out, gr_gu, gr_dn = reduce_end("rest", rest_started, rest_names, dx)

    upd_bs = adam_update("adam_branch_sb", w_branch_sb, m_w_branch_sb, v_w_branch_sb, gr_br, 0)
    upd_bf = adam_update("adam_branch_fox", w_branch_fox, m_w_branch_fox, v_w_branch_fox, gr_br, 1)
    upd_o = adam_update("adam_out", w_out, m_w_out, v_w_out, gr_out, 0)
    upd_ga = adam_update("adam_gate", w_ffn_gate, m_w_ffn_gate, v_w_ffn_gate, gr_gu, 0)
    upd_up = adam_update("adam_up", w_ffn_up, m_w_ffn_up, v_w_ffn_up, gr_gu, 1)
    upd_dn = adam_update("adam_down", w_ffn_down, m_w_ffn_down, v_w_ffn_down, gr_dn, 0)

    done = sum(u_[1][0, 0:1, 0:1] for u_ in (upd_bs, upd_bf, upd_o, upd_ga, upd_up, upd_dn))
    gr_in, = reduce_end("w_in", in_started, ["in"], done)
    upd_in_t = adam_update_transposed("adam_w_in", w_in_t, m_in_t, v_in_t, gr_in)
    upd_in = [jnp.transpose(a, (1, 2, 0)) for a in upd_in_t]
    grads, deltas, new_ms, new_vs = zip(upd_in, upd_bs, upd_bf, upd_o, upd_ga, upd_up, upd_dn)

    def pack(rows):
        rows = [jnp.pad(r_, ((0, 0), (0, d - r_.shape[1]))) for r_ in rows]
        return jnp.concatenate(rows + [jnp.zeros((8 - len(rows), d), F32)], axis=0)

    sm_g, sm_d, sm_m, sm_v = small_allreduce_adam(
        pack([dg_pre, dg_post, dg_fpre, dg_fpost, db_pad]),
        pack([norm_mix_pre, norm_mix_post, norm_ffn_pre, norm_ffn_post, b_forget]),
        pack([m_norm_mix_pre, m_norm_mix_post, m_norm_ffn_pre, m_norm_ffn_post, m_b_forget]),
        pack([v_norm_mix_pre, v_norm_mix_post, v_norm_ffn_pre, v_norm_ffn_post, v_b_forget]), after=upd_in_t[1])

    def small(a):
        return [a[0:1], a[1:2], a[2:3], a[3:4], a[4:5, :n_heads]]

    def ordered(sm, bg):
        return [sm[0], sm[1], bg[0], sm[4], bg[1], bg[2], bg[3], sm[2], sm[3], bg[4], bg[5], bg[6]]

    loss = lax.psum(loss_part[0, 0], ("x", "y", "c"))
    return (loss, dx[None], *ordered(small(sm_g), grads), *ordered(small(sm_d), deltas),
            *ordered(small(sm_m), new_ms), *ordered(small(sm_v), new_vs))
```

```python
import functools

import jax
import jax.numpy as jnp
from jax import lax
from jax.experimental import pallas as pl
from jax.experimental.pallas import tpu as pltpu

F32 = jnp.float32
BF16 = jnp.bfloat16
MESH = pl.DeviceIdType.MESH

HEAD_DIM = 128
LANES = 128
ATT_TILE = 512
ROW_TILE = 256
N_CHIPS = 4
RMS_EPS = 1e-6
ADAM_LR = 0.001
ADAM_B1 = 0.9
ADAM_B2 = 0.999
ADAM_EPS = 1e-08
ADAM_WD = 0.01
ADAM_STEP = 10
NEG_BIG = -1e30
VMEM_LIMIT = 56 * 1024 * 1024
MM_VMEM_BUDGET = 40 * 1024 * 1024
ATT_STRIP = 512

NN = (((1,), (0,)), ((), ()))
NT = (((1,), (1,)), ((), ()))
TN = (((0,), (0,)), ((), ()))


def _tile(n, pref, align):
    best = None
    t = align
    while t <= min(n, pref):
        if n % t == 0:
            best = t
        t += align
    return n if best is None else best


def _params(*sem):
    return pltpu.CompilerParams(dimension_semantics=sem, vmem_limit_bytes=VMEM_LIMIT)


def _mm_tiles(m, n, k, a_bytes, b_bytes, out_bytes, tn, tk):
    tm = _tile(m, 2048, LANES)
    tk = tk or _tile(k, 512, LANES)

    def vmem(t):
        acc = 0 if out_bytes == 4 else tm * t * 4
        return acc + 2 * tm * t * out_bytes + 2 * (tm * tk * a_bytes + tk * t * b_bytes)

    if tn is None:
        fits = [t for t in range(LANES, min(n, 2048) + 1, LANES) if n % t == 0 and vmem(t) <= MM_VMEM_BUDGET]
        tn = max(fits) if fits else _tile(n, LANES, LANES)
    return tm, tn, tk


def mm(a, b, mode, out_dtype, name, *, tn=None, tk=None, b_win=None, chunks=None, out_into=None, after=None):
    n_per, blk0 = chunks if chunks else (1, 0)
    if mode == "nn":
        m, k = a.shape
        n = b.shape[0] * n_per * tn if chunks else (b_win[1] if b_win else b.shape[1])
    elif mode == "nt":
        m = a.shape[0]
        k = b.shape[0] * n_per * tk if chunks else a.shape[1]
        n = b.shape[-2]
    else:
        k, m = a.shape
        n = b.shape[1]
    in_place = jnp.dtype(out_dtype) == jnp.dtype(F32)
    tm, tn, tk = _mm_tiles(m, n, k, a.dtype.itemsize, b.dtype.itemsize, jnp.dtype(out_dtype).itemsize, tn, tk)
    assert m % tm == 0 and n % tn == 0 and k % tk == 0, (name, m, n, k, tm, tn, tk)
    j0 = 0
    if b_win:
        assert b_win[0] % tn == 0
        j0 = b_win[0] // tn
    nk = k // tk
    dims = {"nn": NN, "nt": NT, "tn": TN}[mode]

    def kern(a_ref, b_ref, *rest):
        o_ref, acc_ref = (rest[-1], rest[-1]) if in_place else (rest[-2], rest[-1])
        kk = pl.program_id(2)

        @pl.when(kk == 0)
        def _():
            acc_ref[...] = jnp.zeros_like(acc_ref)

        acc_ref[...] += lax.dot_general(a_ref[...].astype(BF16), b_ref[...].astype(BF16), dims,
                                        preferred_element_type=F32)

        if not in_place:
            @pl.when(kk == nk - 1)
            def _():
                o_ref[...] = acc_ref[...].astype(o_ref.dtype)

    out_spec = pl.BlockSpec((tm, tn), lambda i, j, kk: (i, j))
    out_shape = jax.ShapeDtypeStruct((m, n), out_dtype)
    if mode == "nn":
        a_spec = pl.BlockSpec((tm, tk), lambda i, j, kk: (i, kk))
        if chunks:
            b_spec = pl.BlockSpec((None, tk, tn), lambda i, j, kk: (j // n_per, kk, blk0 + j % n_per))
        else:
            b_spec = pl.BlockSpec((tk, tn), lambda i, j, kk: (kk, j + j0))
    elif mode == "nt":
        a_spec = pl.BlockSpec((tm, tk), lambda i, j, kk: (i, kk))
        if chunks:
            b_spec = pl.BlockSpec((None, tn, tk), lambda i, j, kk: (kk // n_per, j, blk0 + kk % n_per))
        else:
            b_spec = pl.BlockSpec((tn, tk), lambda i, j, kk: (j, kk))
    else:
        a_spec = pl.BlockSpec((tk, tm), lambda i, j, kk: (kk, i))
        b_spec = pl.BlockSpec((tk, tn), lambda i, j, kk: (kk, j))
        if chunks:
            out_spec = pl.BlockSpec((None, tm, tn), lambda i, j, kk: (j // n_per, i, blk0 + j % n_per))
    in_specs, operands, aliases = [a_spec, b_spec], [a, b], {}
    if chunks and mode == "tn":
        assert out_into is not None
        out_shape = jax.ShapeDtypeStruct(out_into.shape, out_dtype)
        in_specs.append(pl.BlockSpec(memory_space=pl.ANY))
        operands.append(out_into)
        aliases = {2: 0}
    if after is not None:
        in_specs.append(pl.BlockSpec(memory_space=pl.ANY))
        operands.append(after)
    return pl.pallas_call(
        kern, name=name, grid=(m // tm, n // tn, nk),
        in_specs=in_specs, out_specs=out_spec, out_shape=out_shape,
        scratch_shapes=[] if in_place else [pltpu.VMEM((tm, tn), F32)], input_output_aliases=aliases,
        compiler_params=_params("parallel", "parallel", "arbitrary"),
    )(*operands)


def _rstd(v):
    return lax.rsqrt(jnp.mean(v * v, axis=-1, keepdims=True) + RMS_EPS)


def _norm_bwd(v, g, dy):
    r = _rstd(v)
    vh = v * r
    dyg = dy * g
    dv = r * (dyg - vh * jnp.mean(dyg * vh, axis=-1, keepdims=True))
    return dv, jnp.sum(dy * vh, axis=0, keepdims=True)


def _row_call(kern, name, ins, outs, s, d):
    tr = _tile(s, ROW_TILE, 16)

    def spec(shape, is_row):
        if is_row:
            return pl.BlockSpec((tr, shape[1]), lambda i: (i, 0))
        return pl.BlockSpec(shape, lambda i: (0, 0))

    return pl.pallas_call(
        kern, name=name, grid=(s // tr,),
        in_specs=[spec(a.shape, r) for a, r in ins],
        out_specs=[spec(sh, r) for sh, _, r in outs],
        out_shape=[jax.ShapeDtypeStruct(sh, dt) for sh, dt, _ in outs],
        compiler_params=_params("arbitrary"),
    )(*[a for a, _ in ins])


def norm_in(x, g):
    s, d = x.shape

    def kern(x_ref, g_ref, u_ref):
        v = x_ref[...]
        u_ref[...] = (v * _rstd(v) * g_ref[...]).astype(BF16)

    return _row_call(kern, "norm_in", [(x, True), (g, False)], [((s, d), BF16, True)], s, d)[0]


def mid_fwd(x, mix, g_post, g_fpre):
    s, d = x.shape

    def kern(x_ref, mix_ref, gp_ref, gf_ref, h1_ref, u2_ref):
        mixv = mix_ref[...]
        h1 = x_ref[...] + mixv * _rstd(mixv) * gp_ref[...]
        h1_ref[...] = h1
        u2_ref[...] = (h1 * _rstd(h1) * gf_ref[...]).astype(BF16)

    return _row_call(kern, "mid_fwd", [(x, True), (mix, True), (g_post, False), (g_fpre, False)],
                     [((s, d), F32, True), ((s, d), BF16, True)], s, d)


def loss_head(h1, ff, g_fpost, target):
    s, d = h1.shape

    def kern(h1_ref, ff_ref, g_ref, t_ref, dy_ref, dff_ref, dg_ref, loss_ref):
        @pl.when(pl.program_id(0) == 0)
        def _():
            dg_ref[...] = jnp.zeros_like(dg_ref)
            loss_ref[...] = jnp.zeros_like(loss_ref)

        ffv = ff_ref[...]
        g = g_ref[...]
        y = h1_ref[...] + ffv * _rstd(ffv) * g
        diff = y - t_ref[...]
        row_loss = jnp.mean(diff * diff, axis=-1, keepdims=True)
        loss_ref[...] += 0.5 * jnp.sum(row_loss, axis=0, keepdims=True)
        dy = diff / d
        dy_ref[...] = dy
        dff, dg = _norm_bwd(ffv, g, dy)
        dff_ref[...] = dff.astype(BF16)
        dg_ref[...] += dg

    return _row_call(kern, "loss_head",
                     [(h1, True), (ff, True), (g_fpost, False), (target, True)],
                     [((s, d), F32, True), ((s, d), BF16, True), ((1, d), F32, False), ((1, 1), F32, False)], s, d)


def mid_bwd(dy, du2, h1, mix, g_fpre, g_post):
    s, d = dy.shape

    def kern(dy_ref, du2_ref, h1_ref, mix_ref, gf_ref, gp_ref, dh1_ref, dmix_ref, dgf_ref, dgp_ref):
        @pl.when(pl.program_id(0) == 0)
        def _():
            dgf_ref[...] = jnp.zeros_like(dgf_ref)
            dgp_ref[...] = jnp.zeros_like(dgp_ref)

        dh, dgf = _norm_bwd(h1_ref[...], gf_ref[...], du2_ref[...])
        dh1 = dy_ref[...] + dh
        dh1_ref[...] = dh1
        dmix, dgp = _norm_bwd(mix_ref[...], gp_ref[...], dh1)
        dmix_ref[...] = dmix.astype(BF16)
        dgf_ref[...] += dgf
        dgp_ref[...] += dgp

    return _row_call(kern, "mid_bwd",
                     [(dy, True), (du2, True), (h1, True), (mix, True), (g_fpre, False), (g_post, False)],
                     [((s, d), F32, True), ((s, d), BF16, True), ((1, d), F32, False), ((1, d), F32, False)], s, d)


def in_bwd(dh1, du, x, g_pre):
    s, d = x.shape

    def kern(dh1_ref, du_ref, x_ref, g_ref, dx_ref, dg_ref):
        @pl.when(pl.program_id(0) == 0)
        def _():
            dg_ref[...] = jnp.zeros_like(dg_ref)

        dxn, dg = _norm_bwd(x_ref[...], g_ref[...], du_ref[...])
        dx_ref[...] = dh1_ref[...] + dxn
        dg_ref[...] += dg

    return _row_call(kern, "in_bwd", [(dh1, True), (du, True), (x, True), (g_pre, False)],
                     [((s, d), F32, True), ((1, d), F32, False)], s, d)


def _sigmoid(v):
    return 1.0 / (1.0 + jnp.exp(-v))


def gate_fwd(bsb, bfx, gf):
    s, d = bsb.shape
    tr, tc = _tile(s, 256, 16), _tile(d, 512, LANES)
    nc = d // tc

    def kern(bsb_ref, bfx_ref, gs_ref, gx_ref, o_ref):
        o_ref[...] = (_sigmoid(gs_ref[...]) * bsb_ref[...] + _sigmoid(gx_ref[...]) * bfx_ref[...]).astype(BF16)

    blk = pl.BlockSpec((tr, tc), lambda i, j: (i, j))
    return pl.pallas_call(
        kern, name="gate_fwd", grid=(s // tr, nc),
        in_specs=[blk, blk, blk, pl.BlockSpec((tr, tc), lambda i, j: (i, j + nc))],
        out_specs=blk, out_shape=jax.ShapeDtypeStruct((s, d), BF16),
        compiler_params=_params("parallel", "parallel"),
    )(bsb, bfx, gf, gf)


def gate_bwd(dmerged, bsb, bfx, gf):
    s, d = bsb.shape
    tr, tc = _tile(s, 256, 16), _tile(d, 512, LANES)
    nc = d // tc

    def kern(dm_ref, bsb_ref, bfx_ref, gs_ref, gx_ref, dbs_ref, dbx_ref, dgs_ref, dgx_ref):
        dm = dm_ref[...]
        ss = _sigmoid(gs_ref[...])
        sx = _sigmoid(gx_ref[...])
        dbs_ref[...] = (dm * ss).astype(BF16)
        dbx_ref[...] = (dm * sx).astype(BF16)
        dgs_ref[...] = (dm * bsb_ref[...] * ss * (1.0 - ss)).astype(BF16)
        dgx_ref[...] = (dm * bfx_ref[...] * sx * (1.0 - sx)).astype(BF16)

    blk = pl.BlockSpec((tr, tc), lambda i, j: (i, j))
    out = jax.ShapeDtypeStruct((s, d), BF16)
    return pl.pallas_call(
        kern, name="gate_bwd", grid=(s // tr, nc),
        in_specs=[blk, blk, blk, blk, pl.BlockSpec((tr, tc), lambda i, j: (i, j + nc))],
        out_specs=[blk, blk, blk, blk], out_shape=[out, out, out, out],
        compiler_params=_params("parallel", "parallel"),
    )(dmerged, bsb, bfx, gf, gf)


FFN_ROWS = 1024


def ffn_up_fused(u2, w_gu, cw):
    s, d = u2.shape
    nc = w_gu.shape[0]
    tm, tk = _tile(s, FFN_ROWS, LANES), _tile(d, 512, LANES)
    nk = d // tk

    def kern(a_ref, b_ref, gu_ref, act_ref):
        kk = pl.program_id(2)

        @pl.when(kk == 0)
        def _():
            gu_ref[...] = jnp.zeros_like(gu_ref)

        gu_ref[...] += jnp.dot(a_ref[...], b_ref[...], preferred_element_type=F32)

        @pl.when(kk == nk - 1)
        def _():
            g = gu_ref[:, :cw]
            act_ref[...] = (g * _sigmoid(g) * gu_ref[:, cw:]).astype(BF16)

    return pl.pallas_call(
        kern, name="ffn_gate_up", grid=(s // tm, nc, nk),
        in_specs=[pl.BlockSpec((tm, tk), lambda i, j, kk: (i, kk)),
                  pl.BlockSpec((None, tk, 2 * cw), lambda i, j, kk: (j, kk, 0))],
        out_specs=[pl.BlockSpec((tm, 2 * cw), lambda i, j, kk: (i, j)), pl.BlockSpec((tm, cw), lambda i, j, kk: (i, j))],
        out_shape=[jax.ShapeDtypeStruct((s, nc * 2 * cw), F32), jax.ShapeDtypeStruct((s, nc * cw), BF16)],
        compiler_params=_params("parallel", "parallel", "arbitrary"),
    )(u2, w_gu)


def ffn_down_bwd_fused(d_ff, w_dn, gu, cw):
    s, d = d_ff.shape
    nc = gu.shape[1] // (2 * cw)
    tm, tk = _tile(s, FFN_ROWS, LANES), _tile(d, 512, LANES)
    nk = d // tk

    def kern(a_ref, b_ref, gu_ref, o_ref, acc_ref):
        kk = pl.program_id(2)

        @pl.when(kk == 0)
        def _():
            acc_ref[...] = jnp.zeros_like(acc_ref)

        acc_ref[...] += lax.dot_general(a_ref[...], b_ref[...], NT, preferred_element_type=F32)

        @pl.when(kk == nk - 1)
        def _():
            da = acc_ref[...]
            g = gu_ref[:, :cw]
            sg = _sigmoid(g)
            o_ref[:, :cw] = (da * gu_ref[:, cw:] * (sg * (1.0 + g * (1.0 - sg)))).astype(BF16)
            o_ref[:, cw:] = (da * (g * sg)).astype(BF16)

    return pl.pallas_call(
        kern, name="d_act_swiglu", grid=(s // tm, nc, nk),
        in_specs=[pl.BlockSpec((tm, tk), lambda i, j, kk: (i, kk)),
                  pl.BlockSpec((cw, tk), lambda i, j, kk: (j, kk)),
                  pl.BlockSpec((tm, 2 * cw), lambda i, j, kk: (i, j))],
        out_specs=pl.BlockSpec((tm, 2 * cw), lambda i, j, kk: (i, j)),
        out_shape=jax.ShapeDtypeStruct(gu.shape, BF16),
        scratch_shapes=[pltpu.VMEM((tm, cw), F32)],
        compiler_params=_params("parallel", "parallel", "arbitrary"),
    )(d_ff, w_dn, gu)


def _split3(v):
    hi = v.astype(BF16)
    r = v - hi.astype(F32)
    mid = r.astype(BF16)
    lo = (r - mid.astype(F32)).astype(BF16)
    return hi, mid, lo


def _dot3_right(v, ones):
    hi, mid, lo = _split3(v)
    d = lambda p: jnp.dot(p, ones, preferred_element_type=F32)
    return (d(lo) + d(mid)) + d(hi)


def _dot3_left(ones, v):
    hi, mid, lo = _split3(v)
    d = lambda p: jnp.dot(ones, p, preferred_element_type=F32)
    return (d(lo) + d(mid)) + d(hi)


def _split2(v):
    hi = v.astype(BF16)
    return hi, (v - hi.astype(F32)).astype(BF16)


def _dot2_right(v, ones):
    hi, lo = _split2(v)
    return jnp.dot(lo, ones, preferred_element_type=F32) + jnp.dot(hi, ones, preferred_element_type=F32)


def _log1p_exp_neg_abs(v):
    return jnp.log(1.0 + jnp.exp(-jnp.abs(v)))


def _mask01(cond):
    return jnp.where(cond, 1.0, 0.0).astype(BF16)


def _iota2(t):
    return (lax.broadcasted_iota(jnp.int32, (t, t), 0), lax.broadcasted_iota(jnp.int32, (t, t), 1))


def cum_fwd(gf, b_pad, f_col0):
    s = gf.shape[0]
    t = _tile(s, ATT_TILE, LANES)
    fb = f_col0 // LANES

    def kern(f_ref, b_ref, cum_ref, carry_ref):
        @pl.when(pl.program_id(0) == 0)
        def _():
            carry_ref[...] = jnp.zeros_like(carry_ref)

        v = f_ref[...] + b_ref[...]
        lf = jnp.minimum(v, 0.0) - _log1p_exp_neg_abs(v)
        row, col = _iota2(t)
        cum = _dot3_left(_mask01(col <= row), lf) + carry_ref[...]
        cum_ref[...] = cum
        carry_ref[...] = cum[t - 1:t, :]

    return pl.pallas_call(
        kern, name="cum_fwd", grid=(s // t,),
        in_specs=[pl.BlockSpec((t, LANES), lambda i: (i, fb)), pl.BlockSpec((1, LANES), lambda i: (0, 0))],
        out_specs=pl.BlockSpec((t, LANES), lambda i: (i, 0)),
        out_shape=jax.ShapeDtypeStruct((s, LANES), F32),
        scratch_shapes=[pltpu.VMEM((1, LANES), F32)],
        compiler_params=_params("arbitrary"),
    )(gf, b_pad)


def cum_bwd(dcum, gf, b_pad, f_col0, n_heads):
    s = gf.shape[0]
    t = _tile(s, ATT_TILE, LANES)
    nb = s // t
    fb = f_col0 // LANES

    def kern(dc_ref, f_ref, b_ref, df_ref, db_ref, carry_ref):
        @pl.when(pl.program_id(0) == 0)
        def _():
            carry_ref[...] = jnp.zeros_like(carry_ref)
            db_ref[...] = jnp.zeros_like(db_ref)

        row, col = _iota2(t)
        dlf = _dot3_left(_mask01(col >= row), dc_ref[...]) + carry_ref[...]
        carry_ref[...] = dlf[0:1, :]
        v = f_ref[...] + b_ref[...]
        sig_neg = jnp.exp(-jnp.maximum(v, 0.0) - _log1p_exp_neg_abs(v))
        lane = lax.broadcasted_iota(jnp.int32, (t, LANES), 1)
        df = jnp.where(lane < n_heads, dlf * sig_neg, 0.0)
        df_ref[...] = df.astype(BF16)
        db_ref[...] += jnp.sum(df, axis=0, keepdims=True)

    return pl.pallas_call(
        kern, name="cum_bwd", grid=(nb,),
        in_specs=[pl.BlockSpec((t, LANES), lambda i: (nb - 1 - i, 0)),
                  pl.BlockSpec((t, LANES), lambda i: (nb - 1 - i, fb)),
                  pl.BlockSpec((1, LANES), lambda i: (0, 0))],
        out_specs=[pl.BlockSpec((t, LANES), lambda i: (nb - 1 - i, 0)), pl.BlockSpec((1, LANES), lambda i: (0, 0))],
        out_shape=[jax.ShapeDtypeStruct((s, LANES), BF16), jax.ShapeDtypeStruct((1, LANES), F32)],
        scratch_shapes=[pltpu.VMEM((1, LANES), F32)],
        compiler_params=_params("arbitrary"),
    )(dcum, gf, b_pad)


def _qkv_specs(s, t, n_heads, base):
    return [pl.BlockSpec((t, HEAD_DIM), lambda h, i: (i, base + h)),
            pl.BlockSpec((s, HEAD_DIM), lambda h, i: (0, base + n_heads + h)),
            pl.BlockSpec((s, HEAD_DIM), lambda h, i: (0, base + 2 * n_heads + h))]


def _strips(t):
    sr = _tile(t, ATT_STRIP, 8)
    return sr, t // sr, [slice(si * sr, (si + 1) * sr) for si in range(t // sr)]


def _key_minus_row(sr, t):
    return lax.broadcasted_iota(jnp.int32, (sr, t), 1) - lax.broadcasted_iota(jnp.int32, (sr, t), 0)


def _keep(valid, v):
    return v if valid is None else jnp.where(valid, v, 0.0)


def _sb_scores(q, k, diff, lim):
    z = lax.dot_general(q, k, NT, preferred_element_type=F32) * (HEAD_DIM ** -0.5)
    valid = None if lim is None else diff < lim
    l1p = _log1p_exp_neg_abs(z)
    return z, valid, l1p, _keep(valid, -jnp.maximum(z, 0.0) - l1p)


def sb_fwd(qkv, n_heads, base):
    s = qkv.shape[0]
    t = _tile(s, ATT_TILE, LANES)
    sr, ns, strips = _strips(t)

    def kern(q_ref, k_ref, v_ref, o_ref):
        i = pl.program_id(1)
        row, col = _iota2(t)
        after = _mask01(row > col)
        diff = _key_minus_row(sr, t)
        qs = [q_ref[sl, :] for sl in strips]

        def tile(j, carry, diagonal):
            runs, accs = carry
            off = pl.multiple_of(j * t, t)
            k = k_ref[pl.ds(off, t), :]
            v = v_ref[pl.ds(off, t), :]
            new_runs, new_accs = [], []
            for si in range(ns):
                z, valid, l1p, log_keep = _sb_scores(qs[si], k, diff, si * sr if diagonal else None)
                between = _dot2_right(log_keep, after) + runs[si]
                w = _keep(valid, jnp.exp(jnp.minimum(z, 0.0) - l1p + between))
                new_accs.append(accs[si] + jnp.dot(w.astype(BF16), v, preferred_element_type=F32))
                new_runs.append(runs[si] + jnp.sum(log_keep, axis=1, keepdims=True))
            return tuple(new_runs), tuple(new_accs)

        init = (tuple(jnp.zeros((sr, 1), F32) for _ in strips), tuple(jnp.zeros((sr, HEAD_DIM), F32) for _ in strips))
        _, accs = lax.fori_loop(0, i, lambda jj, c: tile(i - 1 - jj, c, False), tile(i, init, True))
        for sl, acc in zip(strips, accs):
            o_ref[sl, :] = acc.astype(o_ref.dtype)

    return pl.pallas_call(
        kern, name="sb_fwd", grid=(n_heads, s // t),
        in_specs=_qkv_specs(s, t, n_heads, base),
        out_specs=pl.BlockSpec((t, HEAD_DIM), lambda h, i: (i, h)),
        out_shape=jax.ShapeDtypeStruct((s, n_heads * HEAD_DIM), BF16),
        compiler_params=_params("parallel", "arbitrary"),
    )(qkv, qkv, qkv)


def sb_bwd(qkv, d_o, n_heads, base):
    s = qkv.shape[0]
    t = _tile(s, ATT_TILE, LANES)
    nq = s // t
    sr, ns, strips = _strips(t)
    scale = HEAD_DIM ** -0.5

    def kern(q_ref, k_ref, v_ref, do_ref, dq_ref, dk_ref, dv_ref, dk_acc, dv_acc, run_ref):
        i = pl.program_id(1)

        @pl.when(i == 0)
        def _():
            dk_acc[...] = jnp.zeros_like(dk_acc)
            dv_acc[...] = jnp.zeros_like(dv_acc)

        row, col = _iota2(t)
        after = _mask01(row > col)
        before = _mask01(row < col)
        diff = _key_minus_row(sr, t)
        qs = [q_ref[sl, :] for sl in strips]
        dos = [do_ref[sl, :] for sl in strips]

        def sweep1(j, runs, diagonal):
            k = k_ref[pl.ds(pl.multiple_of(j * t, t), t), :]
            new_runs = []
            for si, sl in enumerate(strips):
                _, _, _, log_keep = _sb_scores(qs[si], k, diff, si * sr if diagonal else None)
                run_ref[j, sl, :] = runs[si]
                new_runs.append(runs[si] + jnp.sum(log_keep, axis=1, keepdims=True))
            return tuple(new_runs)

        lax.fori_loop(0, i, lambda jj, c: sweep1(i - 1 - jj, c, False),
                      sweep1(i, tuple(jnp.zeros((sr, 1), F32) for _ in strips), True))

        def sweep2(j, carry, diagonal):
            run_es, dqs = carry
            off = pl.multiple_of(j * t, t)
            k = k_ref[pl.ds(off, t), :]
            v = v_ref[pl.ds(off, t), :]
            new_es, new_dqs = [], []
            dk_t = jnp.zeros((t, HEAD_DIM), F32)
            dv_t = jnp.zeros((t, HEAD_DIM), F32)
            for si, sl in enumerate(strips):
                z, valid, l1p, log_keep = _sb_scores(qs[si], k, diff, si * sr if diagonal else None)
                between = _dot2_right(log_keep, after) + run_ref[j, sl, :]
                w = _keep(valid, jnp.exp(jnp.minimum(z, 0.0) - l1p + between))
                dw = lax.dot_general(dos[si], v, NT, preferred_element_type=F32)
                e = dw * w
                e_before = _dot2_right(e, before) + run_es[si]
                keep = jnp.exp(log_keep)
                dz = _keep(valid, e * keep - e_before * (1.0 - keep)) * scale
                dzb = dz.astype(BF16)
                new_dqs.append(dqs[si] + jnp.dot(dzb, k, preferred_element_type=F32))
                dk_t = dk_t + lax.dot_general(dzb, qs[si], TN, preferred_element_type=F32)
                dv_t = dv_t + lax.dot_general(w.astype(BF16), dos[si], TN, preferred_element_type=F32)
                new_es.append(run_es[si] + jnp.sum(e, axis=1, keepdims=True))
            dk_acc[pl.ds(off, t), :] += dk_t
            dv_acc[pl.ds(off, t), :] += dv_t
            return tuple(new_es), tuple(new_dqs)

        init = (tuple(jnp.zeros((sr, 1), F32) for _ in strips), tuple(jnp.zeros((sr, HEAD_DIM), F32) for _ in strips))
        _, dqs = sweep2(i, lax.fori_loop(0, i, lambda j, c: sweep2(j, c, False), init), True)
        for sl, dq in zip(strips, dqs):
            dq_ref[sl, :] = dq.astype(BF16)

        @pl.when(i == nq - 1)
        def _():
            dk_ref[...] = dk_acc[...].astype(BF16)
            dv_ref[...] = dv_acc[...].astype(BF16)

    out = jax.ShapeDtypeStruct((s, n_heads * HEAD_DIM), BF16)
    head_blk = pl.BlockSpec((s, HEAD_DIM), lambda h, i: (0, h))
    tile_blk = pl.BlockSpec((t, HEAD_DIM), lambda h, i: (i, h))
    return pl.pallas_call(
        kern, name="sb_bwd", grid=(n_heads, nq),
        in_specs=_qkv_specs(s, t, n_heads, base) + [tile_blk],
        out_specs=[tile_blk, head_blk, head_blk],
        out_shape=[out, out, out],
        scratch_shapes=[pltpu.VMEM((s, HEAD_DIM), F32), pltpu.VMEM((s, HEAD_DIM), F32), pltpu.VMEM((nq, t, 1), F32)],
        compiler_params=_params("parallel", "arbitrary"),
    )(qkv, qkv, qkv, d_o)


def _fox_scores(q, k, cq, ck, diff, lim):
    sc = lax.dot_general(q, k, NT, preferred_element_type=F32) * (HEAD_DIM ** -0.5)
    sc = sc + cq - ck
    if lim is None:
        return sc, None
    valid = diff < lim
    return jnp.where(valid, sc, NEG_BIG), valid


def fox_fwd(qkv, cum_col, cum_row, n_heads, base):
    s = qkv.shape[0]
    t = _tile(s, ATT_TILE, LANES)
    sr, ns, strips = _strips(t)

    def kern(q_ref, k_ref, v_ref, cq_ref, ck_ref, o_ref, lse_ref):
        i = pl.program_id(1)
        diff = _key_minus_row(sr, t)
        qs = [q_ref[sl, :] for sl in strips]
        cqs = [cq_ref[0, sl, :] for sl in strips]

        def tile(j, carry, diagonal):
            off = pl.multiple_of(j * t, t)
            k = k_ref[pl.ds(off, t), :]
            v = v_ref[pl.ds(off, t), :]
            ck = ck_ref[0, :, pl.ds(off, t)]
            out = []
            for si in range(ns):
                m, l, acc = carry[si]
                sc, _ = _fox_scores(qs[si], k, cqs[si], ck, diff, si * sr + 1 if diagonal else None)
                m_new = jnp.maximum(m, jnp.max(sc, axis=1, keepdims=True))
                p = jnp.exp(sc - m_new)
                alpha = jnp.exp(m - m_new)
                l = alpha * l + jnp.sum(p, axis=1, keepdims=True)
                acc = alpha * acc + jnp.dot(p.astype(BF16), v, preferred_element_type=F32)
                out.append((m_new, l, acc))
            return tuple(out)

        init = tuple((jnp.full((sr, 1), NEG_BIG, F32), jnp.zeros((sr, 1), F32), jnp.zeros((sr, HEAD_DIM), F32))
                     for _ in strips)
        res = tile(i, lax.fori_loop(0, i, lambda j, c: tile(j, c, False), init), True)
        for sl, (m, l, acc) in zip(strips, res):
            o_ref[sl, :] = acc / l
            lse_ref[0, sl, :] = m + jnp.log(l)

    col_blk = pl.BlockSpec((1, t, 1), lambda h, i: (h, i, 0))
    return pl.pallas_call(
        kern, name="fox_fwd", grid=(n_heads, s // t),
        in_specs=_qkv_specs(s, t, n_heads, base) + [col_blk, pl.BlockSpec((1, 1, s), lambda h, i: (h, 0, 0))],
        out_specs=[pl.BlockSpec((t, HEAD_DIM), lambda h, i: (i, h)), col_blk],
        out_shape=[jax.ShapeDtypeStruct((s, n_heads * HEAD_DIM), F32), jax.ShapeDtypeStruct((n_heads, s, 1), F32)],
        compiler_params=_params("parallel", "arbitrary"),
    )(qkv, qkv, qkv, cum_col, cum_row)


def fox_bwd(qkv, cum_col, cum_row, o, d_o, lse, n_heads, base):
    s = qkv.shape[0]
    t = _tile(s, ATT_TILE, LANES)
    nq = s // t
    sr, ns, strips = _strips(t)
    scale = HEAD_DIM ** -0.5

    def kern(q_ref, k_ref, v_ref, cq_ref, ck_ref, o_ref, do_ref, lse_ref,
             dq_ref, dk_ref, dv_ref, dcq_ref, dck_ref, dk_acc, dv_acc, dck_acc):
        i = pl.program_id(1)

        @pl.when(i == 0)
        def _():
            dk_acc[...] = jnp.zeros_like(dk_acc)
            dv_acc[...] = jnp.zeros_like(dv_acc)
            dck_acc[...] = jnp.zeros_like(dck_acc)

        diff = _key_minus_row(sr, t)
        qs = [q_ref[sl, :] for sl in strips]
        dos = [do_ref[sl, :] for sl in strips]
        cqs = [cq_ref[0, sl, :] for sl in strips]
        lses = [lse_ref[0, sl, :] for sl in strips]
        deltas = [jnp.sum(dos[si].astype(F32) * o_ref[sl, :], axis=1, keepdims=True) for si, sl in enumerate(strips)]

        def tile(j, carry, diagonal):
            off = pl.multiple_of(j * t, t)
            k = k_ref[pl.ds(off, t), :]
            v = v_ref[pl.ds(off, t), :]
            ck = ck_ref[0, :, pl.ds(off, t)]
            out = []
            dk_t = jnp.zeros((t, HEAD_DIM), F32)
            dv_t = jnp.zeros((t, HEAD_DIM), F32)
            dck_t = jnp.zeros((1, t), F32)
            for si in range(ns):
                dq, dcq = carry[si]
                sc, valid = _fox_scores(qs[si], k, cqs[si], ck, diff, si * sr + 1 if diagonal else None)
                p = _keep(valid, jnp.exp(sc - lses[si]))
                dp = lax.dot_general(dos[si], v, NT, preferred_element_type=F32)
                ds = p * (dp - deltas[si])
                dsb = (ds * scale).astype(BF16)
                dq = dq + jnp.dot(dsb, k, preferred_element_type=F32)
                dk_t = dk_t + lax.dot_general(dsb, qs[si], TN, preferred_element_type=F32)
                dv_t = dv_t + lax.dot_general(p.astype(BF16), dos[si], TN, preferred_element_type=F32)
                dck_t = dck_t + jnp.sum(ds, axis=0, keepdims=True)
                out.append((dq, dcq + jnp.sum(ds, axis=1, keepdims=True)))
            dk_acc[pl.ds(off, t), :] += dk_t
            dv_acc[pl.ds(off, t), :] += dv_t
            dck_acc[:, pl.ds(off, t)] -= dck_t
            return tuple(out)

        init = tuple((jnp.zeros((sr, HEAD_DIM), F32), jnp.zeros((sr, 1), F32)) for _ in strips)
        res = tile(i, lax.fori_loop(0, i, lambda j, c: tile(j, c, False), init), True)
        for sl, (dq, dcq) in zip(strips, res):
            dq_ref[sl, :] = dq.astype(BF16)
            dcq_ref[0, sl, :] = dcq

        @pl.when(i == nq - 1)
        def _():
            dk_ref[...] = dk_acc[...].astype(BF16)
            dv_ref[...] = dv_acc[...].astype(BF16)
            dck_ref[0] = dck_acc[...]

    out = jax.ShapeDtypeStruct((s, n_heads * HEAD_DIM), BF16)
    head_blk = pl.BlockSpec((s, HEAD_DIM), lambda h, i: (0, h))
    tile_blk = pl.BlockSpec((t, HEAD_DIM), lambda h, i: (i, h))
    col_blk = pl.BlockSpec((1, t, 1), lambda h, i: (h, i, 0))
    row_blk = pl.BlockSpec((1, 1, s), lambda h, i: (h, 0, 0))
    return pl.pallas_call(
        kern, name="fox_bwd", grid=(n_heads, nq),
        in_specs=_qkv_specs(s, t, n_heads, base) + [col_blk, row_blk, tile_blk, tile_blk, col_blk],
        out_specs=[tile_blk, head_blk, head_blk, col_blk, row_blk],
        out_shape=[out, out, out, jax.ShapeDtypeStruct((n_heads, s, 1), F32),
                   jax.ShapeDtypeStruct((n_heads, 1, s), F32)],
        scratch_shapes=[pltpu.VMEM((s, HEAD_DIM), F32), pltpu.VMEM((s, HEAD_DIM), F32), pltpu.VMEM((1, s), F32)],
        compiler_params=_params("parallel", "arbitrary"),
    )(qkv, qkv, qkv, cum_col, cum_row, o, d_o, lse)


def _place():
    x, y, c = lax.axis_index("x"), lax.axis_index("y"), lax.axis_index("c")
    other_chips = [(1 - x, y), (x, 1 - y), (1 - x, 1 - y)]
    return x, y, c, other_chips


ANY = pl.BlockSpec(memory_space=pl.ANY)


def _remote(src, dst, send_sem, recv_sem, dev):
    return pltpu.make_async_remote_copy(src_ref=src, dst_ref=dst, send_sem=send_sem, recv_sem=recv_sem,
                                        device_id=dev, device_id_type=MESH)


def place_transposed(name, w_t, chip):
    c, _, r = w_t.shape
    tc = LANES

    def kern(chip_ref, w_ref, o_ref):
        o_ref[...] = w_ref[:, 0, :].T.astype(BF16)

    return pl.pallas_call(
        kern, name=name,
        grid_spec=pltpu.PrefetchScalarGridSpec(
            num_scalar_prefetch=1, grid=(pl.cdiv(c, tc),),
            in_specs=[pl.BlockSpec((tc, 1, r), lambda j, chip_ref: (j, 0, 0))],
            out_specs=pl.BlockSpec((None, r, tc), lambda j, chip_ref: (chip_ref[0], 0, j))),
        out_shape=jax.ShapeDtypeStruct((N_CHIPS, r, c), BF16),
        compiler_params=_params("parallel"),
    )(chip, w_t)


def adam_update_transposed(name, w_t, m_t, v_t, g_buf):
    c, _, r = w_t.shape
    tc = LANES

    def kern(w_ref, m_ref, v_ref, g_ref, go_ref, dl_ref, nm_ref, nv_ref):
        g = g_ref[...].T
        delta, nm, nv = _adam(w_ref[:, 0, :], g, m_ref[:, 0, :], v_ref[:, 0, :])
        go_ref[:, 0, :] = g
        dl_ref[:, 0, :] = delta
        nm_ref[:, 0, :] = nm
        nv_ref[:, 0, :] = nv

    blk = pl.BlockSpec((tc, 1, r), lambda j: (j, 0, 0))
    out = jax.ShapeDtypeStruct((c, 1, r), F32)
    return pl.pallas_call(
        kern, name=name, grid=(pl.cdiv(c, tc),),
        in_specs=[blk, blk, blk, pl.BlockSpec((r, tc), lambda j: (0, j))],
        out_specs=[blk] * 4, out_shape=[out] * 4, compiler_params=_params("parallel"),
    )(w_t, m_t, v_t, g_buf)


def cast_place(name, ws, chip):
    r = ws[0].shape[1]
    cs = [w.shape[2] for w in ws]
    tr = _tile(r, 256, 16)

    def kern(chip_ref, *refs):
        o_ref = refs[-1]
        off = 0
        for w_ref, c in zip(refs[:-1], cs):
            o_ref[:, off:off + c] = w_ref[...].astype(BF16)
            off += c

    return pl.pallas_call(
        kern, name=name,
        grid_spec=pltpu.PrefetchScalarGridSpec(
            num_scalar_prefetch=1, grid=(r // tr,),
            in_specs=[pl.BlockSpec((None, tr, c), lambda i, chip_ref: (0, i, 0)) for c in cs],
            out_specs=pl.BlockSpec((None, tr, sum(cs)), lambda i, chip_ref: (chip_ref[0], i, 0))),
        out_shape=jax.ShapeDtypeStruct((N_CHIPS, r, sum(cs)), BF16),
        compiler_params=_params("parallel"),
    )(chip, *ws)


HBM = pl.BlockSpec(memory_space=pltpu.HBM)
SEM = pl.BlockSpec(memory_space=pltpu.SEMAPHORE)
SPLIT = pltpu.CompilerParams(has_side_effects=pltpu.SideEffectType.DATAFLOW_SIDE_EFFECTING)


def _in_hbm(a):
    return pltpu.with_memory_space_constraint(a, pltpu.HBM)


def _slab_rows(ref, k, core):
    half = ref.shape[1] // 2
    return ref.at[k, pl.ds(pl.multiple_of(core * half, 16), half)]


def gather_start(name, bufs):
    n = len(bufs)

    def body(*refs):
        ins, send, recv, token = refs[:n], refs[n], refs[n + 1], refs[-1]
        x, y, c, chips = _place()
        me = 2 * x + y
        for a in range(n):
            for j in range(3):
                rows = _slab_rows(ins[a], me, c)
                _remote(rows, rows, send.at[3 * a + j], recv.at[3 * a + j], (chips[j][0], chips[j][1], c)).start()
        token[...] = jnp.zeros_like(token)

    sem = pltpu.SemaphoreType.DMA((3 * n,))
    res = pl.pallas_call(
        body, name=name, in_specs=[HBM] * n, out_specs=[SEM, SEM] + [HBM] * n + [pl.BlockSpec(memory_space=pltpu.VMEM)],
        out_shape=[sem, sem] + [pltpu.HBM(b.shape, b.dtype) for b in bufs] + [jax.ShapeDtypeStruct((8, LANES), F32)],
        input_output_aliases={a: 2 + a for a in range(n)}, compiler_params=SPLIT,
    )(*[_in_hbm(b) for b in bufs])
    return res[0], res[1], res[2:2 + n], res[-1]


def gather_wait(name, bufs, send_sems, recv_sems, after):
    n = len(bufs)

    def body(*refs):
        ins, send, recv = refs[:n], refs[n], refs[n + 1]
        x, y, c, chips = _place()
        me = 2 * x + y
        for a in range(n):
            for j in range(3):
                dev = (chips[j][0], chips[j][1], c)
                mine = _slab_rows(ins[a], me, c)
                _remote(mine, mine, send.at[3 * a + j], recv.at[3 * a + j], dev).wait_send()
                land = _slab_rows(ins[a], 2 * chips[j][0] + chips[j][1], c)
                _remote(land, land, send.at[3 * a + j], recv.at[3 * a + j], dev).wait_recv()

    return pl.pallas_call(
        body, name=name, in_specs=[HBM] * n + [SEM, SEM] + [ANY] * len(after), out_specs=[HBM] * n,
        out_shape=[pltpu.HBM(b.shape, b.dtype) for b in bufs],
        input_output_aliases={a: a for a in range(n)}, compiler_params=SPLIT,
    )(*bufs, send_sems, recv_sems, *after)


def gather_forward(name, bufs):
    n = len(bufs)

    def body(*refs):
        outs = refs[n:2 * n]
        send_sems, recv_sems = refs[2 * n:]
        x, y, c, chips = _place()
        sibling = (x, y, 1 - c)

        def d2d(a, j, core):
            rows = _slab_rows(outs[a], 2 * chips[j][0] + chips[j][1], core)
            return _remote(rows, rows, send_sems.at[3 * a + j], recv_sems.at[3 * a + j], sibling)

        pairs = [(a, j) for a in range(n) for j in range(3)]
        for a, j in pairs:
            d2d(a, j, c).start()
        for a, j in pairs:
            d2d(a, j, 1 - c).wait_recv()
        for a, j in pairs:
            d2d(a, j, c).wait_send()

    return pl.pallas_call(
        body, name=name, in_specs=[ANY] * n, out_specs=[ANY] * n,
        out_shape=[jax.ShapeDtypeStruct(b.shape, b.dtype) for b in bufs],
        input_output_aliases={a: a for a in range(n)},
        scratch_shapes=[pltpu.SemaphoreType.DMA((3 * n,)), pltpu.SemaphoreType.DMA((3 * n,))],
    )(*bufs)


def _forward_plan(refs):
    x, y, c, chips = _place()
    out = []
    for ref in refs:
        for j in range(3):
            k = 2 * chips[j][0] + chips[j][1]
            out.append((_slab_rows(ref, k, c), _slab_rows(ref, k, c), _slab_rows(ref, k, 1 - c)))
    return out


def _join_plan(refs):
    x, y, c, _ = _place()
    out = []
    for ref in refs:
        half = ref.shape[0] // 2
        mine = ref.at[pl.ds(pl.multiple_of(c * half, 8), half)]
        out.append((mine, mine, ref.at[pl.ds(pl.multiple_of((1 - c) * half, 8), half)]))
    return out


def _swap_plan(refs):
    x, y, c, _ = _place()
    n = len(refs) // 2
    out = []
    for a in range(n):
        half = refs[a].shape[1] // 2
        src = refs[a].at[:, pl.ds(pl.multiple_of((1 - c) * half, 16), half), :]
        out.append((src, refs[n + a], refs[n + a]))
    return out


def sibling_start(name, arrays, plan, n_copies, after=()):
    n = len(arrays)

    n_in = n + len(after)

    def body(*refs):
        send, recv, token = refs[n_in], refs[n_in + 1], refs[-1]
        x, y, c, _ = _place()
        for idx, (src, dst, _) in enumerate(plan(refs[:n])):
            _remote(src, dst, send.at[idx], recv.at[idx], (x, y, 1 - c)).start()
        token[...] = jnp.zeros_like(token)

    sem = pltpu.SemaphoreType.DMA((n_copies,))
    res = pl.pallas_call(
        body, name=name, in_specs=[HBM] * n + [ANY] * len(after),
        out_specs=[SEM, SEM] + [HBM] * n + [pl.BlockSpec(memory_space=pltpu.VMEM)],
        out_shape=[sem, sem] + [pltpu.HBM(b.shape, b.dtype) for b in arrays] + [jax.ShapeDtypeStruct((8, LANES), F32)],
        input_output_aliases={a: 2 + a for a in range(n)}, compiler_params=SPLIT,
    )(*[_in_hbm(b) for b in arrays], *after)
    return res[0], res[1], res[2:2 + n], res[-1]


def sibling_wait(name, arrays, send_sems, recv_sems, plan, after):
    n = len(arrays)

    def body(*refs):
        send, recv = refs[n], refs[n + 1]
        x, y, c, _ = _place()
        for idx, (src, dst, filled) in enumerate(plan(refs[:n])):
            _remote(src, dst, send.at[idx], recv.at[idx], (x, y, 1 - c)).wait_send()
            _remote(filled, filled, send.at[idx], recv.at[idx], (x, y, 1 - c)).wait_recv()

    return pl.pallas_call(
        body, name=name, in_specs=[HBM] * n + [SEM, SEM] + [ANY] * len(after), out_specs=[HBM] * n,
        out_shape=[pltpu.HBM(b.shape, b.dtype) for b in arrays],
        input_output_aliases={a: a for a in range(n)}, compiler_params=SPLIT,
    )(*arrays, send_sems, recv_sems, *after)


def swap_halves(name, pieces):
    n = len(pieces)
    halves = [p.shape[1] // 2 for p in pieces]

    def body(*refs):
        ins, outs = refs[:n], refs[n:2 * n]
        send_sems, recv_sems = refs[2 * n:]
        x, y, c, _ = _place()
        cps = [_remote(ins[a].at[:, pl.ds(pl.multiple_of((1 - c) * halves[a], 16), halves[a]), :], outs[a],
                       send_sems.at[a], recv_sems.at[a], (x, y, 1 - c)) for a in range(n)]
        for cp in cps:
            cp.start()
        for cp in cps:
            cp.wait()

    return pl.pallas_call(
        body, name=name, in_specs=[ANY] * n, out_specs=[ANY] * n,
        out_shape=[jax.ShapeDtypeStruct((N_CHIPS, h, p.shape[2]), p.dtype) for p, h in zip(pieces, halves)],
        scratch_shapes=[pltpu.SemaphoreType.DMA((n,)), pltpu.SemaphoreType.DMA((n,))],
    )(*pieces)


def pair_sum(name, pieces, got, core):
    _, r, w = pieces.shape
    half = r // 2
    tr = _tile(half, 256, 16)

    def kern(core_ref, p_ref, g_ref, o_ref):
        o_ref[...] = (p_ref[...].astype(F32) + g_ref[...].astype(F32)).astype(o_ref.dtype)

    return pl.pallas_call(
        kern, name=name,
        grid_spec=pltpu.PrefetchScalarGridSpec(
            num_scalar_prefetch=1, grid=(N_CHIPS, half // tr),
            in_specs=[pl.BlockSpec((None, None, tr, w), lambda k, i, core_ref: (k, core_ref[0], i, 0)),
                      pl.BlockSpec((None, tr, w), lambda k, i, core_ref: (k, i, 0))],
            out_specs=pl.BlockSpec((None, tr, w), lambda k, i, core_ref: (k, i, 0))),
        out_shape=jax.ShapeDtypeStruct((N_CHIPS, half, w), pieces.dtype),
        compiler_params=_params("parallel", "parallel"),
    )(core, pieces.reshape(N_CHIPS, 2, half, w), got)


def _scatter_copies(sums, lands, send, recv):
    x, y, c, chips = _place()
    return [_remote(sums[a].at[2 * chips[j][0] + chips[j][1]], lands[a].at[j], send.at[3 * a + j], recv.at[3 * a + j],
                    (chips[j][0], chips[j][1], c)) for a in range(len(sums)) for j in range(3)]


def scatter_start(name, sums):
    n = len(sums)
    lands = [lax.empty((3,) + t.shape[1:], t.dtype) for t in sums]

    def body(*refs):
        ins, land_in, send, recv, token = refs[:n], refs[n:2 * n], refs[2 * n], refs[2 * n + 1], refs[-1]
        for cp in _scatter_copies(ins, land_in, send, recv):
            cp.start()
        token[...] = jnp.zeros_like(token)

    sem = pltpu.SemaphoreType.DMA((3 * n,))
    res = pl.pallas_call(
        body, name=name, in_specs=[HBM] * (2 * n),
        out_specs=[SEM, SEM] + [HBM] * (2 * n) + [pl.BlockSpec(memory_space=pltpu.VMEM)],
        out_shape=[sem, sem] + [pltpu.HBM(t.shape, t.dtype) for t in sums + lands] + [jax.ShapeDtypeStruct((8, LANES), F32)],
        input_output_aliases={a: 2 + a for a in range(2 * n)}, compiler_params=SPLIT,
    )(*[_in_hbm(t) for t in sums + lands])
    return res[0], res[1], res[2:2 + n], res[2 + n:2 + 2 * n], res[-1]


def scatter_wait(name, sums, lands, send_sems, recv_sems, after):
    n = len(sums)

    def body(*refs):
        ins, land_in, send, recv = refs[:n], refs[n:2 * n], refs[2 * n], refs[2 * n + 1]
        for cp in _scatter_copies(ins, land_in, send, recv):
            cp.wait_send()
            cp.wait_recv()

    res = pl.pallas_call(
        body, name=name, in_specs=[HBM] * (2 * n) + [SEM, SEM, ANY], out_specs=[HBM] * (2 * n),
        out_shape=[pltpu.HBM(t.shape, t.dtype) for t in sums + lands],
        input_output_aliases={a: a for a in range(2 * n)}, compiler_params=SPLIT,
    )(*sums, *lands, send_sems, recv_sems, after)
    return res[:n], res[n:]


def chip_sum(name, sums, got, chip, core):
    _, half, w = sums.shape
    tr = _tile(half, 256, 16)
    nb = half // tr

    def kern(ids_ref, s_ref, g0_ref, g1_ref, g2_ref, o_ref):
        o_ref[...] = ((s_ref[...].astype(F32) + g0_ref[...].astype(F32)) + g1_ref[...].astype(F32)) \
            + g2_ref[...].astype(F32)

    def got_spec(j):
        return pl.BlockSpec((None, tr, w), lambda i, ids_ref: (j, i, 0))

    return pl.pallas_call(
        kern, name=name,
        grid_spec=pltpu.PrefetchScalarGridSpec(
            num_scalar_prefetch=1, grid=(nb,),
            in_specs=[pl.BlockSpec((None, tr, w), lambda i, ids_ref: (ids_ref[0], i, 0)),
                      got_spec(0), got_spec(1), got_spec(2)],
            out_specs=pl.BlockSpec((tr, w), lambda i, ids_ref: (ids_ref[1] * nb + i, 0))),
        out_shape=jax.ShapeDtypeStruct((2 * half, w), F32),
        compiler_params=_params("parallel"),
    )(jnp.concatenate([chip, core]), sums, got, got, got)


def join_halves(name, shards):
    n = len(shards)
    halves = [g.shape[0] // 2 for g in shards]

    def body(*refs):
        outs = refs[n:2 * n]
        send_sems, recv_sems = refs[2 * n:]
        x, y, c, _ = _place()
        cps = []
        for a in range(n):
            rows = outs[a].at[pl.ds(pl.multiple_of(c * halves[a], 8), halves[a])]
            cps.append(_remote(rows, rows, send_sems.at[a], recv_sems.at[a], (x, y, 1 - c)))
        for cp in cps:
            cp.start()
        for cp in cps:
            cp.wait()

    return pl.pallas_call(
        body, name=name, in_specs=[ANY] * n, out_specs=[ANY] * n,
        out_shape=[jax.ShapeDtypeStruct(g.shape, g.dtype) for g in shards],
        input_output_aliases={a: a for a in range(n)},
        scratch_shapes=[pltpu.SemaphoreType.DMA((n,)), pltpu.SemaphoreType.DMA((n,))],
    )(*shards)


def _adam(w, g, m, v):
    m = ADAM_B1 * m + (1.0 - ADAM_B1) * g
    v = ADAM_B2 * v + (1.0 - ADAM_B2) * (g * g)
    m_hat = m / (1.0 - ADAM_B1 ** ADAM_STEP)
    v_hat = v / (1.0 - ADAM_B2 ** ADAM_STEP)
    delta = -ADAM_LR * (m_hat / (jnp.sqrt(v_hat) + ADAM_EPS) + ADAM_WD * w)
    return delta, m, v


def small_allreduce_adam(g_part, w, m, v, after):
    n_dev = 8
    r, d = g_part.shape

    def body(g_ref, w_ref, m_ref, v_ref, after_ref, gs_ref, dl_ref, nm_ref, nv_ref, all_ref, send_sems, recv_sems):
        x, y, c, _ = _place()
        me = 4 * x + 2 * y + c
        all_ref[me] = g_ref[...]
        cps = []
        for rel in range(1, n_dev):
            px = 1 - x if rel & 4 else x
            py = 1 - y if rel & 2 else y
            pc = 1 - c if rel & 1 else c
            cps.append(_remote(g_ref, all_ref.at[me], send_sems.at[rel - 1], recv_sems.at[rel - 1], (px, py, pc)))
        for cp in cps:
            cp.start()
        for cp in cps:
            cp.wait()
        total = all_ref[0]
        for dev in range(1, n_dev):
            total = total + all_ref[dev]
        gs_ref[...] = total
        delta, nm, nv = _adam(w_ref[...], total, m_ref[...], v_ref[...])
        dl_ref[...] = delta
        nm_ref[...] = nm
        nv_ref[...] = nv

    vm = pl.BlockSpec(memory_space=pltpu.VMEM)
    out = jax.ShapeDtypeStruct((r, d), F32)
    return pl.pallas_call(
        body, name="small_allreduce_adam", in_specs=[vm, vm, vm, vm, ANY], out_specs=[vm, vm, vm, vm],
        out_shape=[out, out, out, out],
        scratch_shapes=[pltpu.VMEM((n_dev, r, d), F32), pltpu.SemaphoreType.DMA((n_dev - 1,)),
                        pltpu.SemaphoreType.DMA((n_dev - 1,))],
    )(g_part, w, m, v, after)


def adam_update(name, w, m, v, g_buf, col_blk, after=None):
    w, m, v = w[0], m[0], v[0]
    r, c = w.shape
    tr = _tile(r, 128, 8)
    extra = [] if after is None else [after]

    def kern(w_ref, m_ref, v_ref, g_ref, *rest):
        go_ref, dl_ref, nm_ref, nv_ref = rest[len(extra):]
        g = g_ref[...]
        delta, nm, nv = _adam(w_ref[...], g, m_ref[...], v_ref[...])
        go_ref[...] = g
        dl_ref[...] = delta
        nm_ref[...] = nm
        nv_ref[...] = nv

    blk = pl.BlockSpec((tr, c), lambda i: (i, 0))
    out = jax.ShapeDtypeStruct((r, c), F32)
    res = pl.pallas_call(
        kern, name=name, grid=(r // tr,),
        in_specs=[blk, blk, blk, pl.BlockSpec((tr, c), lambda i: (i, col_blk))] + [ANY] * len(extra),
        out_specs=[blk] * 4, out_shape=[out] * 4, compiler_params=_params("parallel"),
    )(w, m, v, g_buf, *extra)
    return [a[None] for a in res]


def _w_in_segments(cw, n_qkv, n_heads, d):
    out = []

    def add(lo, hi, main):
        while lo < hi:
            k, a = divmod(lo, cw)
            w = min(cw - a, hi - lo)
            out.append((k, a, main, w))
            lo, main = lo + w, main + w

    add(0, n_qkv, 0)
    add(n_qkv + n_heads, N_CHIPS * cw, n_qkv)
    add(n_qkv, n_qkv + n_heads, n_qkv + 2 * d)
    return out


def regroup_w_in(g_in, segments, n_main):
    _, d, cw = g_in.shape
    tr = _tile(d, 128, 16)
    n_real = max(m + w for _, _, m, w in segments)

    def kern(s_ref, o_ref):
        for k, a, m, w in segments:
            o_ref[:, m:m + w] = s_ref[k, :, a:a + w]
        o_ref[:, n_real:] = jnp.zeros((tr, n_main - n_real), o_ref.dtype)

    return pl.pallas_call(
        kern, name="regroup_w_in", grid=(d // tr,),
        in_specs=[pl.BlockSpec((N_CHIPS, tr, cw), lambda i: (0, i, 0))],
        out_specs=pl.BlockSpec((tr, n_main), lambda i: (i, 0)),
        out_shape=jax.ShapeDtypeStruct((d, n_main), g_in.dtype), compiler_params=_params("parallel"),
    )(g_in)


def regroup_dw_in(dw_main, segments, cw):
    d, n_main = dw_main.shape
    tr = _tile(d, 128, 16)

    def kern(s_ref, o_ref):
        for k, a, m, w in segments:
            o_ref[k, :, a:a + w] = s_ref[:, m:m + w]

    return pl.pallas_call(
        kern, name="regroup_dw_in", grid=(d // tr,),
        in_specs=[pl.BlockSpec((tr, n_main), lambda i: (i, 0))],
        out_specs=pl.BlockSpec((N_CHIPS, tr, cw), lambda i: (0, i, 0)),
        out_shape=jax.ShapeDtypeStruct((N_CHIPS, d, cw), dw_main.dtype), compiler_params=_params("parallel"),
    )(dw_main)


def kernel(x, norm_mix_pre, norm_mix_post, w_in, b_forget, w_branch_sb, w_branch_fox, w_out, norm_ffn_pre, norm_ffn_post, w_ffn_gate, w_ffn_up, w_ffn_down, loss_target, m_norm_mix_pre, m_norm_mix_post, m_w_in, m_b_forget, m_w_branch_sb, m_w_branch_fox, m_w_out, m_norm_ffn_pre, m_norm_ffn_post, m_w_ffn_gate, m_w_ffn_up, m_w_ffn_down, v_norm_mix_pre, v_norm_mix_post, v_w_in, v_b_forget, v_w_branch_sb, v_w_branch_fox, v_w_out, v_norm_ffn_pre, v_norm_ffn_post, v_w_ffn_gate, v_w_ffn_up, v_w_ffn_down):
    s, d = x.shape[1], x.shape[2]
    n_heads = b_forget.shape[1]
    d_att = n_heads * HEAD_DIM
    c_in = w_in.shape[2]
    c_br = w_branch_sb.shape[2]
    c_gu = w_ffn_gate.shape[2]
    d_ff = c_gu * N_CHIPS
    d_in = c_in * N_CHIPS
    f_pad = 512
    n_qkv = 6 * d_att
    n_gf = 2 * d + f_pad
    core = lax.axis_index("c").astype(jnp.int32).reshape(1)
    chip = (2 * lax.axis_index("x") + lax.axis_index("y")).astype(jnp.int32).reshape(1)

    as_t = lambda a: jnp.transpose(a, (2, 0, 1))
    w_in_t, m_in_t, v_in_t = as_t(w_in), as_t(m_w_in), as_t(v_w_in)
    in_send, in_recv, in_bufs, in_token = gather_start(
        "gather_start_w_in", [place_transposed("place_w_in", w_in_t, chip)])
    sm_send, sm_recv, sm_bufs, sm_token = gather_start("gather_start_small", [
        cast_place("place_branch", [w_branch_sb, w_branch_fox], chip),
        cast_place("place_out", [w_out + in_token[0, 0]], chip)])
    ag_send, ag_recv, ag_bufs, ag_token = gather_start("gather_start_ffn", [
        cast_place("place_gate_up", [w_ffn_gate, w_ffn_up], chip),
        cast_place("place_down", [w_ffn_down + sm_token[0, 0]], chip)])
    g_in, = gather_forward("forward_w_in", gather_wait("gather_wait_w_in", in_bufs, in_send, in_recv,
                                                       [ag_token]))
    segments = _w_in_segments(c_in, n_qkv, n_heads, d)
    w_main = regroup_w_in(g_in, segments, n_qkv + n_gf)
    x2 = x[0]
    tgt = loss_target[0]
    b_pad = jnp.pad(b_forget, ((0, 0), (0, LANES - n_heads)))

    u = norm_in(x2, norm_mix_pre)
    qkv = mm(u, w_main, "nn", BF16, "proj_qkv", b_win=(0, n_qkv))
    gf = mm(u, w_main, "nn", F32, "proj_gates", b_win=(n_qkv, n_gf))
    cum = cum_fwd(gf, b_pad, 2 * d)
    cum_heads = cum[:, :n_heads].T
    cum_col, cum_row = cum_heads[:, :, None], cum_heads[:, None, :]
    o_sb = sb_fwd(qkv, n_heads, 0)
    o_fx, lse = fox_fwd(qkv, cum_col, cum_row, n_heads, 3 * n_heads)
    g_br, g_out = gather_forward("forward_small",
                                 gather_wait("gather_wait_small", sm_bufs, sm_send, sm_recv, [o_sb, o_fx]))
    w_o = g_out.reshape(d, d)
    bsb = mm(o_sb, g_br, "nn", F32, "branch_sb", tn=c_br, chunks=(1, 0))
    bfx = mm(o_fx, g_br, "nn", F32, "branch_fox", tn=c_br, chunks=(1, 1))
    merged = gate_fwd(bsb, bfx, gf)
    ffn_bufs = gather_wait("gather_wait_ffn", ag_bufs, ag_send, ag_recv, [merged])
    fb_send, fb_recv, fb_bufs, fb_token = sibling_start("forward_big_start", ffn_bufs, _forward_plan, 6)
    mix = mm(merged, w_o, "nn", F32, "out_proj", after=fb_token)
    h1, u2 = mid_fwd(x2, mix, norm_mix_post, norm_ffn_pre)
    g_gu, g_dn = sibling_wait("forward_big_wait", fb_bufs, fb_send, fb_recv, _forward_plan, [u2])
    w_dn = g_dn.reshape(d_ff, d)
    gu, act = ffn_up_fused(u2, g_gu, c_gu)
    ff = mm(act, w_dn, "nn", F32, "ffn_down")
    dy, d_ff_out, dg_fpost, loss_part = loss_head(h1, ff, norm_ffn_post, tgt)

    p_dn = mm(act, d_ff_out, "tn", BF16, "dw_ffn_down").reshape(N_CHIPS, d_ff // N_CHIPS, d)
    d_gu = ffn_down_bwd_fused(d_ff_out, w_dn, gu, c_gu)
    p_gu = mm(u2, d_gu, "tn", BF16, "dw_ffn_gate_up", tn=c_gu, chunks=(2, 0),
              out_into=lax.empty((N_CHIPS, d, 2 * c_gu), BF16))
    sw_send, sw_recv, sw_arrs, sw_token = sibling_start(
        "swap_big_start", [p_gu, p_dn, lax.empty((N_CHIPS, d // 2, 2 * c_gu), BF16),
                           lax.empty((N_CHIPS, d_ff // N_CHIPS // 2, d), BF16)], _swap_plan, 2)
    du2 = mm(d_gu, g_gu, "nt", F32, "d_u2", tk=c_gu, chunks=(2, 0), after=sw_token)
    dh1, d_mix, dg_fpre, dg_post = mid_bwd(dy, du2, h1, mix, norm_ffn_pre, norm_mix_post)
    p_out = mm(merged, d_mix, "tn", BF16, "dw_out").reshape(N_CHIPS, d // N_CHIPS, d)
    d_merged = mm(d_mix, w_o, "nt", F32, "d_merged")
    d_bsb, d_bfx, d_gs, d_gx = gate_bwd(d_merged, bsb, bfx, gf)
    p_br = mm(o_sb, d_bsb, "tn", BF16, "dw_branch_sb", tn=c_br, chunks=(1, 0),
              out_into=lax.empty((N_CHIPS, d_att, 2 * c_br), BF16))
    p_br = mm(o_fx, d_bfx, "tn", BF16, "dw_branch_fox", tn=c_br, chunks=(1, 1), out_into=p_br)
    d_osb = mm(d_bsb, g_br, "nt", BF16, "d_o_sb", tk=c_br, chunks=(1, 0))
    d_ofx = mm(d_bfx, g_br, "nt", BF16, "d_o_fox", tk=c_br, chunks=(1, 1))

    def reduce_start(tag, pieces, names):
        from_sibling = swap_halves("swap_halves_" + tag, pieces)
        sums = [pair_sum("pair_sum_" + t, p, q, core) for t, p, q in zip(names, pieces, from_sibling)]
        return scatter_start("scatter_start_" + tag, sums)

    def reduce_end(tag, started, names, after):
        send, recv, sums, lands, _ = started
        sums, lands = scatter_wait("scatter_wait_" + tag, sums, lands, send, recv, after)
        return join_halves("join_halves_" + tag, [chip_sum("chip_sum_" + t, sm, got, chip, core)
                                                  for t, sm, got in zip(names, sums, lands)])

    rest_names = ["branch", "out", "gate_up", "down"]
    p_gu, p_dn, q_gu, q_dn = sibling_wait("swap_big_wait", sw_arrs, sw_send, sw_recv, _swap_plan, [p_br])
    q_br, q_out = swap_halves("swap_halves_small", [p_br, p_out])
    rest_started = scatter_start("scatter_start_rest", [
        pair_sum("pair_sum_" + t, p, q, core)
        for t, p, q in zip(rest_names, [p_br, p_out, p_gu, p_dn], [q_br, q_out, q_gu, q_dn])])
    d_osb = d_osb + rest_started[4][0, 0].astype(BF16)
    dq_s, dk_s, dv_s = sb_bwd(qkv, d_osb, n_heads, 0)
    dq_f, dk_f, dv_f, dcq, dck = fox_bwd(qkv, cum_col, cum_row, o_fx, d_ofx, lse, n_heads, 3 * n_heads)
    d_cum = jnp.pad((dcq[:, :, 0] + dck[:, 0, :]).T, ((0, 0), (0, LANES - n_heads)))
    d_f, db_pad = cum_bwd(d_cum, gf, b_pad, 2 * d, n_heads)
    d_main = jnp.concatenate(
        [dq_s, dk_s, dv_s, dq_f, dk_f, dv_f, d_gs, d_gx, d_f, jnp.zeros((s, f_pad - LANES), BF16)], axis=1)
    p_in = regroup_dw_in(mm(u, d_main, "tn", BF16, "dw_in"), segments, c_in)

    in_started = reduce_start("w_in", [p_in], ["in"])
    rs_send, rs_recv, rs_sums, rs_lands, _ = rest_started
    rs_sums, rs_lands = scatter_wait("scatter_wait_rest", rs_sums, rs_lands, rs_send, rs_recv, in_started[4])
    j_send, j_recv, j_shards, j_token = sibling_start(
        "join_rest_start", [chip_sum("chip_sum_" + t, sm, got, chip, core)
                            for t, sm, got in zip(rest_names, rs_sums, rs_lands)], _join_plan, 4)
    du = mm(d_main, w_main, "nt", F32, "d_u", after=j_token)
    dx, dg_pre = in_bwd(dh1, du, x2, norm_mix_pre + in_started[4][0:1, 0:1])
    gr_br, gr_out, gr_gu, gr_dn = sibling_wait("join_rest_wait", j_shards, j_send, j_recv, _join_plan, [dx])

    upd_bs = adam_update("adam_branch_sb", w_branch_sb, m_w_branch_sb, v_w_branch_sb, gr_br, 0)
    upd_bf = adam_update("adam_branch_fox", w_branch_fox, m_w_branch_fox, v_w_branch_fox, gr_br, 1)
    upd_o = adam_update("adam_out", w_out, m_w_out, v_w_out, gr_out, 0)
    upd_ga = adam_update("adam_gate", w_ffn_gate, m_w_ffn_gate, v_w_ffn_gate, gr_gu, 0)
    upd_up = adam_update("adam_up", w_ffn_up, m_w_ffn_up, v_w_ffn_up, gr_gu, 1)
    upd_dn = adam_update("adam_down", w_ffn_down, m_w_ffn_down, v_w_ffn_down, gr_dn, 0)

    done = sum(u_[1][0, 0:1, 0:1] for u_ in (upd_bs, upd_bf, upd_o, upd_ga, upd_up, upd_dn))
    gr_in, = reduce_end("w_in", in_started, ["in"], done)
    upd_in_t = adam_update_transposed("adam_w_in", w_in_t, m_in_t, v_in_t, gr_in)
    upd_in = [jnp.transpose(a, (1, 2, 0)) for a in upd_in_t]
    grads, deltas, new_ms, new_vs = zip(upd_in, upd_bs, upd_bf, upd_o, upd_ga, upd_up, upd_dn)

    def pack(rows):
        rows = [jnp.pad(r_, ((0, 0), (0, d - r_.shape[1]))) for r_ in rows]
        return jnp.concatenate(rows + [jnp.zeros((8 - len(rows), d), F32)], axis=0)

    sm_g, sm_d, sm_m, sm_v = small_allreduce_adam(
        pack([dg_pre, dg_post, dg_fpre, dg_fpost, db_pad]),
        pack([norm_mix_pre, norm_mix_post, norm_ffn_pre, norm_ffn_post, b_forget]),
        pack([m_norm_mix_pre, m_norm_mix_post, m_norm_ffn_pre, m_norm_ffn_post, m_b_forget]),
        pack([v_norm_mix_pre, v_norm_mix_post, v_norm_ffn_pre, v_norm_ffn_post, v_b_forget]), after=upd_in_t[1])

    def small(a):
        return [a[0:1], a[1:2], a[2:3], a[3:4], a[4:5, :n_heads]]

    def ordered(sm, bg):
        return [sm[0], sm[1], bg[0], sm[4], bg[1], bg[2], bg[3], sm[2], sm[3], bg[4], bg[5], bg[6]]

    loss = lax.psum(loss_part[0, 0], ("x", "y", "c"))
    return (loss, dx[None], *ordered(small(sm_g), grads), *ordered(small(sm_d), deltas),
            *ordered(small(sm_m), new_ms), *ordered(small(sm_v), new_vs))
```

```python
import functools

import jax
import jax.numpy as jnp
from jax import lax
from jax.experimental import pallas as pl
from jax.experimental.pallas import tpu as pltpu

F32 = jnp.float32
BF16 = jnp.bfloat16
MESH = pl.DeviceIdType.MESH

HEAD_DIM = 128
LANES = 128
ATT_TILE = 512
ROW_TILE = 256
N_CHIPS = 4
RMS_EPS = 1e-6
ADAM_LR = 0.001
ADAM_B1 = 0.9
ADAM_B2 = 0.999
ADAM_EPS = 1e-08
ADAM_WD = 0.01
ADAM_STEP = 10
NEG_BIG = -1e30
VMEM_LIMIT = 56 * 1024 * 1024
MM_VMEM_BUDGET = 40 * 1024 * 1024
ATT_STRIP = 512

NN = (((1,), (0,)), ((), ()))
NT = (((1,), (1,)), ((), ()))
TN = (((0,), (0,)), ((), ()))


def _tile(n, pref, align):
    best = None
    t = align
    while t <= min(n, pref):
        if n % t == 0:
            best = t
        t += align
    return n if best is None else best


def _params(*sem):
    return pltpu.CompilerParams(dimension_semantics=sem, vmem_limit_bytes=VMEM_LIMIT)


def _mm_tiles(m, n, k, a_bytes, b_bytes, out_bytes, tn, tk):
    tm = _tile(m, 2048, LANES)
    tk = tk or _tile(k, 512, LANES)

    def vmem(t):
        acc = 0 if out_bytes == 4 else tm * t * 4
        return acc + 2 * tm * t * out_bytes + 2 * (tm * tk * a_bytes + tk * t * b_bytes)

    if tn is None:
        fits = [t for t in range(LANES, min(n, 2048) + 1, LANES) if n % t == 0 and vmem(t) <= MM_VMEM_BUDGET]
        tn = max(fits) if fits else _tile(n, LANES, LANES)
    return tm, tn, tk


def mm(a, b, mode, out_dtype, name, *, tn=None, tk=None, b_win=None, chunks=None, out_into=None, after=None):
    n_per, blk0 = chunks if chunks else (1, 0)
    if mode == "nn":
        m, k = a.shape
        n = b.shape[0] * n_per * tn if chunks else (b_win[1] if b_win else b.shape[1])
    elif mode == "nt":
        m = a.shape[0]
        k = b.shape[0] * n_per * tk if chunks else a.shape[1]
        n = b.shape[-2]
    else:
        k, m = a.shape
        n = b.shape[1]
    in_place = jnp.dtype(out_dtype) == jnp.dtype(F32)
    tm, tn, tk = _mm_tiles(m, n, k, a.dtype.itemsize, b.dtype.itemsize, jnp.dtype(out_dtype).itemsize, tn, tk)
    assert m % tm == 0 and n % tn == 0 and k % tk == 0, (name, m, n, k, tm, tn, tk)
    j0 = 0
    if b_win:
        assert b_win[0] % tn == 0
        j0 = b_win[0] // tn
    nk = k // tk
    dims = {"nn": NN, "nt": NT, "tn": TN}[mode]

    def kern(a_ref, b_ref, *rest):
        o_ref, acc_ref = (rest[-1], rest[-1]) if in_place else (rest[-2], rest[-1])
        kk = pl.program_id(2)

        @pl.when(kk == 0)
        def _():
            acc_ref[...] = jnp.zeros_like(acc_ref)

        acc_ref[...] += lax.dot_general(a_ref[...].astype(BF16), b_ref[...].astype(BF16), dims,
                                        preferred_element_type=F32)

        if not in_place:
            @pl.when(kk == nk - 1)
            def _():
                o_ref[...] = acc_ref[...].astype(o_ref.dtype)

    out_spec = pl.BlockSpec((tm, tn), lambda i, j, kk: (i, j))
    out_shape = jax.ShapeDtypeStruct((m, n), out_dtype)
    if mode == "nn":
        a_spec = pl.BlockSpec((tm, tk), lambda i, j, kk: (i, kk))
        if chunks:
            b_spec = pl.BlockSpec((None, tk, tn), lambda i, j, kk: (j // n_per, kk, blk0 + j % n_per))
        else:
            b_spec = pl.BlockSpec((tk, tn), lambda i, j, kk: (kk, j + j0))
    elif mode == "nt":
        a_spec = pl.BlockSpec((tm, tk), lambda i, j, kk: (i, kk))
        if chunks:
            b_spec = pl.BlockSpec((None, tn, tk), lambda i, j, kk: (kk // n_per, j, blk0 + kk % n_per))
        else:
            b_spec = pl.BlockSpec((tn, tk), lambda i, j, kk: (j, kk))
    else:
        a_spec = pl.BlockSpec((tk, tm), lambda i, j, kk: (kk, i))
        b_spec = pl.BlockSpec((tk, tn), lambda i, j, kk: (kk, j))
        if chunks:
            out_spec = pl.BlockSpec((None, tm, tn), lambda i, j, kk: (j // n_per, i, blk0 + j % n_per))
    in_specs, operands, aliases = [a_spec, b_spec], [a, b], {}
    if chunks and mode == "tn":
        assert out_into is not None
        out_shape = jax.ShapeDtypeStruct(out_into.shape, out_dtype)
        in_specs.append(pl.BlockSpec(memory_space=pl.ANY))
        operands.append(out_into)
        aliases = {2: 0}
    if after is not None:
        in_specs.append(pl.BlockSpec(memory_space=pl.ANY))
        operands.append(after)
    return pl.pallas_call(
        kern, name=name, grid=(m // tm, n // tn, nk),
        in_specs=in_specs, out_specs=out_spec, out_shape=out_shape,
        scratch_shapes=[] if in_place else [pltpu.VMEM((tm, tn), F32)], input_output_aliases=aliases,
        compiler_params=_params("parallel", "parallel", "arbitrary"),
    )(*operands)


def _rstd(v):
    return lax.rsqrt(jnp.mean(v * v, axis=-1, keepdims=True) + RMS_EPS)


def _norm_bwd(v, g, dy):
    r = _rstd(v)
    vh = v * r
    dyg = dy * g
    dv = r * (dyg - vh * jnp.mean(dyg * vh, axis=-1, keepdims=True))
    return dv, jnp.sum(dy * vh, axis=0, keepdims=True)


def _row_call(kern, name, ins, outs, s, d):
    tr = _tile(s, ROW_TILE, 16)

    def spec(shape, is_row):
        if is_row:
            return pl.BlockSpec((tr, shape[1]), lambda i: (i, 0))
        return pl.BlockSpec(shape, lambda i: (0, 0))

    return pl.pallas_call(
        kern, name=name, grid=(s // tr,),
        in_specs=[spec(a.shape, r) for a, r in ins],
        out_specs=[spec(sh, r) for sh, _, r in outs],
        out_shape=[jax.ShapeDtypeStruct(sh, dt) for sh, dt, _ in outs],
        compiler_params=_params("arbitrary"),
    )(*[a for a, _ in ins])


def norm_in(x, g):
    s, d = x.shape

    def kern(x_ref, g_ref, u_ref):
        v = x_ref[...]
        u_ref[...] = (v * _rstd(v) * g_ref[...]).astype(BF16)

    return _row_call(kern, "norm_in", [(x, True), (g, False)], [((s, d), BF16, True)], s, d)[0]


def mid_fwd(x, mix, g_post, g_fpre):
    s, d = x.shape

    def kern(x_ref, mix_ref, gp_ref, gf_ref, h1_ref, u2_ref):
        mixv = mix_ref[...]
        h1 = x_ref[...] + mixv * _rstd(mixv) * gp_ref[...]
        h1_ref[...] = h1
        u2_ref[...] = (h1 * _rstd(h1) * gf_ref[...]).astype(BF16)

    return _row_call(kern, "mid_fwd", [(x, True), (mix, True), (g_post, False), (g_fpre, False)],
                     [((s, d), F32, True), ((s, d), BF16, True)], s, d)


def loss_head(h1, ff, g_fpost, target):
    s, d = h1.shape

    def kern(h1_ref, ff_ref, g_ref, t_ref, dy_ref, dff_ref, dg_ref, loss_ref):
        @pl.when(pl.program_id(0) == 0)
        def _():
            dg_ref[...] = jnp.zeros_like(dg_ref)
            loss_ref[...] = jnp.zeros_like(loss_ref)

        ffv = ff_ref[...]
        g = g_ref[...]
        y = h1_ref[...] + ffv * _rstd(ffv) * g
        diff = y - t_ref[...]
        row_loss = jnp.mean(diff * diff, axis=-1, keepdims=True)
        loss_ref[...] += 0.5 * jnp.sum(row_loss, axis=0, keepdims=True)
        dy = diff / d
        dy_ref[...] = dy
        dff, dg = _norm_bwd(ffv, g, dy)
        dff_ref[...] = dff.astype(BF16)
        dg_ref[...] += dg

    return _row_call(kern, "loss_head",
                     [(h1, True), (ff, True), (g_fpost, False), (target, True)],
                     [((s, d), F32, True), ((s, d), BF16, True), ((1, d), F32, False), ((1, 1), F32, False)], s, d)


def mid_bwd(dy, du2, h1, mix, g_fpre, g_post):
    s, d = dy.shape

    def kern(dy_ref, du2_ref, h1_ref, mix_ref, gf_ref, gp_ref, dh1_ref, dmix_ref, dgf_ref, dgp_ref):
        @pl.when(pl.program_id(0) == 0)
        def _():
            dgf_ref[...] = jnp.zeros_like(dgf_ref)
            dgp_ref[...] = jnp.zeros_like(dgp_ref)

        dh, dgf = _norm_bwd(h1_ref[...], gf_ref[...], du2_ref[...])
        dh1 = dy_ref[...] + dh
        dh1_ref[...] = dh1
        dmix, dgp = _norm_bwd(mix_ref[...], gp_ref[...], dh1)
        dmix_ref[...] = dmix.astype(BF16)
        dgf_ref[...] += dgf
        dgp_ref[...] += dgp

    return _row_call(kern, "mid_bwd",
                     [(dy, True), (du2, True), (h1, True), (mix, True), (g_fpre, False), (g_post, False)],
                     [((s, d), F32, True), ((s, d), BF16, True), ((1, d), F32, False), ((1, d), F32, False)], s, d)


def in_bwd(dh1, du, x, g_pre):
    s, d = x.shape

    def kern(dh1_ref, du_ref, x_ref, g_ref, dx_ref, dg_ref):
        @pl.when(pl.program_id(0) == 0)
        def _():
            dg_ref[...] = jnp.zeros_like(dg_ref)

        dxn, dg = _norm_bwd(x_ref[...], g_ref[...], du_ref[...])
        dx_ref[...] = dh1_ref[...] + dxn
        dg_ref[...] += dg

    return _row_call(kern, "in_bwd", [(dh1, True), (du, True), (x, True), (g_pre, False)],
                     [((s, d), F32, True), ((1, d), F32, False)], s, d)


def _sigmoid(v):
    return 1.0 / (1.0 + jnp.exp(-v))


def gate_fwd(bsb, bfx, gf):
    s, d = bsb.shape
    tr, tc = _tile(s, 256, 16), _tile(d, 512, LANES)
    nc = d // tc

    def kern(bsb_ref, bfx_ref, gs_ref, gx_ref, o_ref):
        o_ref[...] = (_sigmoid(gs_ref[...]) * bsb_ref[...] + _sigmoid(gx_ref[...]) * bfx_ref[...]).astype(BF16)

    blk = pl.BlockSpec((tr, tc), lambda i, j: (i, j))
    return pl.pallas_call(
        kern, name="gate_fwd", grid=(s // tr, nc),
        in_specs=[blk, blk, blk, pl.BlockSpec((tr, tc), lambda i, j: (i, j + nc))],
        out_specs=blk, out_shape=jax.ShapeDtypeStruct((s, d), BF16),
        compiler_params=_params("parallel", "parallel"),
    )(bsb, bfx, gf, gf)


def gate_bwd(dmerged, bsb, bfx, gf):
    s, d = bsb.shape
    tr, tc = _tile(s, 256, 16), _tile(d, 512, LANES)
    nc = d // tc

    def kern(dm_ref, bsb_ref, bfx_ref, gs_ref, gx_ref, dbs_ref, dbx_ref, dgs_ref, dgx_ref):
        dm = dm_ref[...]
        ss = _sigmoid(gs_ref[...])
        sx = _sigmoid(gx_ref[...])
        dbs_ref[...] = (dm * ss).astype(BF16)
        dbx_ref[...] = (dm * sx).astype(BF16)
        dgs_ref[...] = (dm * bsb_ref[...] * ss * (1.0 - ss)).astype(BF16)
        dgx_ref[...] = (dm * bfx_ref[...] * sx * (1.0 - sx)).astype(BF16)

    blk = pl.BlockSpec((tr, tc), lambda i, j: (i, j))
    out = jax.ShapeDtypeStruct((s, d), BF16)
    return pl.pallas_call(
        kern, name="gate_bwd", grid=(s // tr, nc),
        in_specs=[blk, blk, blk, blk, pl.BlockSpec((tr, tc), lambda i, j: (i, j + nc))],
        out_specs=[blk, blk, blk, blk], out_shape=[out, out, out, out],
        compiler_params=_params("parallel", "parallel"),
    )(dmerged, bsb, bfx, gf, gf)


FFN_ROWS = 1024


def ffn_up_fused(u2, w_gu, cw):
    s, d = u2.shape
    nc = w_gu.shape[0]
    tm, tk = _tile(s, FFN_ROWS, LANES), _tile(d, 512, LANES)
    nk = d // tk

    def kern(a_ref, b_ref, gu_ref, act_ref):
        kk = pl.program_id(2)

        @pl.when(kk == 0)
        def _():
            gu_ref[...] = jnp.zeros_like(gu_ref)

        gu_ref[...] += jnp.dot(a_ref[...], b_ref[...], preferred_element_type=F32)

        @pl.when(kk == nk - 1)
        def _():
            g = gu_ref[:, :cw]
            act_ref[...] = (g * _sigmoid(g) * gu_ref[:, cw:]).astype(BF16)

    return pl.pallas_call(
        kern, name="ffn_gate_up", grid=(s // tm, nc, nk),
        in_specs=[pl.BlockSpec((tm, tk), lambda i, j, kk: (i, kk)),
                  pl.BlockSpec((None, tk, 2 * cw), lambda i, j, kk: (j, kk, 0))],
        out_specs=[pl.BlockSpec((tm, 2 * cw), lambda i, j, kk: (i, j)), pl.BlockSpec((tm, cw), lambda i, j, kk: (i, j))],
        out_shape=[jax.ShapeDtypeStruct((s, nc * 2 * cw), F32), jax.ShapeDtypeStruct((s, nc * cw), BF16)],
        compiler_params=_params("parallel", "parallel", "arbitrary"),
    )(u2, w_gu)


def ffn_down_bwd_fused(d_ff, w_dn, gu, cw):
    s, d = d_ff.shape
    nc = gu.shape[1] // (2 * cw)
    tm, tk = _tile(s, FFN_ROWS, LANES), _tile(d, 512, LANES)
    nk = d // tk

    def kern(a_ref, b_ref, gu_ref, o_ref, acc_ref):
        kk = pl.program_id(2)

        @pl.when(kk == 0)
        def _():
            acc_ref[...] = jnp.zeros_like(acc_ref)

        acc_ref[...] += lax.dot_general(a_ref[...], b_ref[...], NT, preferred_element_type=F32)

        @pl.when(kk == nk - 1)
        def _():
            da = acc_ref[...]
            g = gu_ref[:, :cw]
            sg = _sigmoid(g)
            o_ref[:, :cw] = (da * gu_ref[:, cw:] * (sg * (1.0 + g * (1.0 - sg)))).astype(BF16)
            o_ref[:, cw:] = (da * (g * sg)).astype(BF16)

    return pl.pallas_call(
        kern, name="d_act_swiglu", grid=(s // tm, nc, nk),
        in_specs=[pl.BlockSpec((tm, tk), lambda i, j, kk: (i, kk)),
                  pl.BlockSpec((cw, tk), lambda i, j, kk: (j, kk)),
                  pl.BlockSpec((tm, 2 * cw), lambda i, j, kk: (i, j))],
        out_specs=pl.BlockSpec((tm, 2 * cw), lambda i, j, kk: (i, j)),
        out_shape=jax.ShapeDtypeStruct(gu.shape, BF16),
        scratch_shapes=[pltpu.VMEM((tm, cw), F32)],
        compiler_params=_params("parallel", "parallel", "arbitrary"),
    )(d_ff, w_dn, gu)


def _split3(v):
    hi = v.astype(BF16)
    r = v - hi.astype(F32)
    mid = r.astype(BF16)
    lo = (r - mid.astype(F32)).astype(BF16)
    return hi, mid, lo


def _dot3_right(v, ones):
    hi, mid, lo = _split3(v)
    d = lambda p: jnp.dot(p, ones, preferred_element_type=F32)
    return (d(lo) + d(mid)) + d(hi)


def _dot3_left(ones, v):
    hi, mid, lo = _split3(v)
    d = lambda p: jnp.dot(ones, p, preferred_element_type=F32)
    return (d(lo) + d(mid)) + d(hi)


def _split2(v):
    hi = v.astype(BF16)
    return hi, (v - hi.astype(F32)).astype(BF16)


def _dot2_right(v, ones):
    hi, lo = _split2(v)
    return jnp.dot(lo, ones, preferred_element_type=F32) + jnp.dot(hi, ones, preferred_element_type=F32)


def _log1p_exp_neg_abs(v):
    return jnp.log(1.0 + jnp.exp(-jnp.abs(v)))


def _mask01(cond):
    return jnp.where(cond, 1.0, 0.0).astype(BF16)


def _iota2(t):
    return (lax.broadcasted_iota(jnp.int32, (t, t), 0), lax.broadcasted_iota(jnp.int32, (t, t), 1))


def cum_fwd(gf, b_pad, f_col0):
    s = gf.shape[0]
    t = _tile(s, ATT_TILE, LANES)
    fb = f_col0 // LANES

    def kern(f_ref, b_ref, cum_ref, carry_ref):
        @pl.when(pl.program_id(0) == 0)
        def _():
            carry_ref[...] = jnp.zeros_like(carry_ref)

        v = f_ref[...] + b_ref[...]
        lf = jnp.minimum(v, 0.0) - _log1p_exp_neg_abs(v)
        row, col = _iota2(t)
        cum = _dot3_left(_mask01(col <= row), lf) + carry_ref[...]
        cum_ref[...] = cum
        carry_ref[...] = cum[t - 1:t, :]

    return pl.pallas_call(
        kern, name="cum_fwd", grid=(s // t,),
        in_specs=[pl.BlockSpec((t, LANES), lambda i: (i, fb)), pl.BlockSpec((1, LANES), lambda i: (0, 0))],
        out_specs=pl.BlockSpec((t, LANES), lambda i: (i, 0)),
        out_shape=jax.ShapeDtypeStruct((s, LANES), F32),
        scratch_shapes=[pltpu.VMEM((1, LANES), F32)],
        compiler_params=_params("arbitrary"),
    )(gf, b_pad)


def cum_bwd(dcum, gf, b_pad, f_col0, n_heads):
    s = gf.shape[0]
    t = _tile(s, ATT_TILE, LANES)
    nb = s // t
    fb = f_col0 // LANES

    def kern(dc_ref, f_ref, b_ref, df_ref, db_ref, carry_ref):
        @pl.when(pl.program_id(0) == 0)
        def _():
            carry_ref[...] = jnp.zeros_like(carry_ref)
            db_ref[...] = jnp.zeros_like(db_ref)

        row, col = _iota2(t)
        dlf = _dot3_left(_mask01(col >= row), dc_ref[...]) + carry_ref[...]
        carry_ref[...] = dlf[0:1, :]
        v = f_ref[...] + b_ref[...]
        sig_neg = jnp.exp(-jnp.maximum(v, 0.0) - _log1p_exp_neg_abs(v))
        lane = lax.broadcasted_iota(jnp.int32, (t, LANES), 1)
        df = jnp.where(lane < n_heads, dlf * sig_neg, 0.0)
        df_ref[...] = df.astype(BF16)
        db_ref[...] += jnp.sum(df, axis=0, keepdims=True)

    return pl.pallas_call(
        kern, name="cum_bwd", grid=(nb,),
        in_specs=[pl.BlockSpec((t, LANES), lambda i: (nb - 1 - i, 0)),
                  pl.BlockSpec((t, LANES), lambda i: (nb - 1 - i, fb)),
                  pl.BlockSpec((1, LANES), lambda i: (0, 0))],
        out_specs=[pl.BlockSpec((t, LANES), lambda i: (nb - 1 - i, 0)), pl.BlockSpec((1, LANES), lambda i: (0, 0))],
        out_shape=[jax.ShapeDtypeStruct((s, LANES), BF16), jax.ShapeDtypeStruct((1, LANES), F32)],
        scratch_shapes=[pltpu.VMEM((1, LANES), F32)],
        compiler_params=_params("arbitrary"),
    )(dcum, gf, b_pad)


def _qkv_specs(s, t, n_heads, base):
    return [pl.BlockSpec((t, HEAD_DIM), lambda h, i: (i, base + h)),
            pl.BlockSpec((s, HEAD_DIM), lambda h, i: (0, base + n_heads + h)),
            pl.BlockSpec((s, HEAD_DIM), lambda h, i: (0, base + 2 * n_heads + h))]


def _strips(t):
    sr = _tile(t, ATT_STRIP, 8)
    return sr, t // sr, [slice(si * sr, (si + 1) * sr) for si in range(t // sr)]


def _key_minus_row(sr, t):
    return lax.broadcasted_iota(jnp.int32, (sr, t), 1) - lax.broadcasted_iota(jnp.int32, (sr, t), 0)


def _keep(valid, v):
    return v if valid is None else jnp.where(valid, v, 0.0)


def _sb_scores(q, k, diff, lim):
    z = lax.dot_general(q, k, NT, preferred_element_type=F32) * (HEAD_DIM ** -0.5)
    valid = None if lim is None else diff < lim
    l1p = _log1p_exp_neg_abs(z)
    return z, valid, l1p, _keep(valid, -jnp.maximum(z, 0.0) - l1p)


def sb_fwd(qkv, n_heads, base):
    s = qkv.shape[0]
    t = _tile(s, ATT_TILE, LANES)
    sr, ns, strips = _strips(t)

    def kern(q_ref, k_ref, v_ref, o_ref, run_ref):
        i = pl.program_id(1)
        row, col = _iota2(t)
        after = _mask01(row > col)
        diff = _key_minus_row(sr, t)
        qs = [q_ref[sl, :] for sl in strips]

        def tile(j, carry, diagonal):
            runs, accs = carry
            off = pl.multiple_of(j * t, t)
            k = k_ref[pl.ds(off, t), :]
            v = v_ref[pl.ds(off, t), :]
            new_runs, new_accs = [], []
            for si, sl in enumerate(strips):
                run_ref[0, j, sl, :] = jnp.broadcast_to(runs[si], (sr, LANES))
                z, valid, l1p, log_keep = _sb_scores(qs[si], k, diff, si * sr if diagonal else None)
                between = _dot2_right(log_keep, after) + runs[si]
                w = _keep(valid, jnp.exp(jnp.minimum(z, 0.0) - l1p + between))
                new_accs.append(accs[si] + jnp.dot(w.astype(BF16), v, preferred_element_type=F32))
                new_runs.append(runs[si] + jnp.sum(log_keep, axis=1, keepdims=True))
            return tuple(new_runs), tuple(new_accs)

        init = (tuple(jnp.zeros((sr, 1), F32) for _ in strips), tuple(jnp.zeros((sr, HEAD_DIM), F32) for _ in strips))
        _, accs = lax.fori_loop(0, i, lambda jj, c: tile(i - 1 - jj, c, False), tile(i, init, True))
        for sl, acc in zip(strips, accs):
            o_ref[sl, :] = acc.astype(o_ref.dtype)

    nq = s // t
    return pl.pallas_call(
        kern, name="sb_fwd", grid=(n_heads, nq),
        in_specs=_qkv_specs(s, t, n_heads, base),
        out_specs=[pl.BlockSpec((t, HEAD_DIM), lambda h, i: (i, h)),
                   pl.BlockSpec((1, nq, t, LANES), lambda h, i: (h, 0, i, 0))],
        out_shape=[jax.ShapeDtypeStruct((s, n_heads * HEAD_DIM), BF16),
                   jax.ShapeDtypeStruct((n_heads, nq, s, LANES), F32)],
        compiler_params=_params("parallel", "arbitrary"),
    )(qkv, qkv, qkv)


def sb_bwd(qkv, d_o, runs, n_heads, base):
    s = qkv.shape[0]
    t = _tile(s, ATT_TILE, LANES)
    nq = s // t
    sr, ns, strips = _strips(t)
    scale = HEAD_DIM ** -0.5

    def kern(q_ref, k_ref, v_ref, do_ref, run_ref, dq_ref, dk_ref, dv_ref, dk_acc, dv_acc):
        i = pl.program_id(1)

        @pl.when(i == 0)
        def _():
            dk_acc[...] = jnp.zeros_like(dk_acc)
            dv_acc[...] = jnp.zeros_like(dv_acc)

        row, col = _iota2(t)
        after = _mask01(row > col)
        before = _mask01(row < col)
        diff = _key_minus_row(sr, t)
        qs = [q_ref[sl, :] for sl in strips]
        dos = [do_ref[sl, :] for sl in strips]

        def sweep2(j, carry, diagonal):
            run_es, dqs = carry
            off = pl.multiple_of(j * t, t)
            k = k_ref[pl.ds(off, t), :]
            v = v_ref[pl.ds(off, t), :]
            new_es, new_dqs = [], []
            dk_t = jnp.zeros((t, HEAD_DIM), F32)
            dv_t = jnp.zeros((t, HEAD_DIM), F32)
            for si, sl in enumerate(strips):
                z, valid, l1p, log_keep = _sb_scores(qs[si], k, diff, si * sr if diagonal else None)
                between = _dot2_right(log_keep, after) + run_ref[0, j, sl, 0:1]
                w = _keep(valid, jnp.exp(jnp.minimum(z, 0.0) - l1p + between))
                dw = lax.dot_general(dos[si], v, NT, preferred_element_type=F32)
                e = dw * w
                e_before = _dot2_right(e, before) + run_es[si]
                keep = jnp.exp(log_keep)
                dz = _keep(valid, e * keep - e_before * (1.0 - keep)) * scale
                dzb = dz.astype(BF16)
                new_dqs.append(dqs[si] + jnp.dot(dzb, k, preferred_element_type=F32))
                dk_t = dk_t + lax.dot_general(dzb, qs[si], TN, preferred_element_type=F32)
                dv_t = dv_t + lax.dot_general(w.astype(BF16), dos[si], TN, preferred_element_type=F32)
                new_es.append(run_es[si] + jnp.sum(e, axis=1, keepdims=True))
            dk_acc[pl.ds(off, t), :] += dk_t
            dv_acc[pl.ds(off, t), :] += dv_t
            return tuple(new_es), tuple(new_dqs)

        init = (tuple(jnp.zeros((sr, 1), F32) for _ in strips), tuple(jnp.zeros((sr, HEAD_DIM), F32) for _ in strips))
        _, dqs = sweep2(i, lax.fori_loop(0, i, lambda j, c: sweep2(j, c, False), init), True)
        for sl, dq in zip(strips, dqs):
            dq_ref[sl, :] = dq.astype(BF16)

        @pl.when(i == nq - 1)
        def _():
            dk_ref[...] = dk_acc[...].astype(BF16)
            dv_ref[...] = dv_acc[...].astype(BF16)

    out = jax.ShapeDtypeStruct((s, n_heads * HEAD_DIM), BF16)
    head_blk = pl.BlockSpec((s, HEAD_DIM), lambda h, i: (0, h))
    tile_blk = pl.BlockSpec((t, HEAD_DIM), lambda h, i: (i, h))
    return pl.pallas_call(
        kern, name="sb_bwd", grid=(n_heads, nq),
        in_specs=_qkv_specs(s, t, n_heads, base) + [tile_blk, pl.BlockSpec((1, nq, t, LANES), lambda h, i: (h, 0, i, 0))],
        out_specs=[tile_blk, head_blk, head_blk],
        out_shape=[out, out, out],
        scratch_shapes=[pltpu.VMEM((s, HEAD_DIM), F32), pltpu.VMEM((s, HEAD_DIM), F32)],
        compiler_params=_params("parallel", "arbitrary"),
    )(qkv, qkv, qkv, d_o, runs)


def _fox_scores(q, k, cq, ck, diff, lim):
    sc = lax.dot_general(q, k, NT, preferred_element_type=F32) * (HEAD_DIM ** -0.5)
    sc = sc + cq - ck
    if lim is None:
        return sc, None
    valid = diff < lim
    return jnp.where(valid, sc, NEG_BIG), valid


def fox_fwd(qkv, cum_col, cum_row, n_heads, base):
    s = qkv.shape[0]
    t = _tile(s, ATT_TILE, LANES)
    sr, ns, strips = _strips(t)

    def kern(q_ref, k_ref, v_ref, cq_ref, ck_ref, o_ref, lse_ref):
        i = pl.program_id(1)
        diff = _key_minus_row(sr, t)
        qs = [q_ref[sl, :] for sl in strips]
        cqs = [cq_ref[0, sl, :] for sl in strips]

        def tile(j, carry, diagonal):
            off = pl.multiple_of(j * t, t)
            k = k_ref[pl.ds(off, t), :]
            v = v_ref[pl.ds(off, t), :]
            ck = ck_ref[0, :, pl.ds(off, t)]
            out = []
            for si in range(ns):
                m, l, acc = carry[si]
                sc, _ = _fox_scores(qs[si], k, cqs[si], ck, diff, si * sr + 1 if diagonal else None)
                m_new = jnp.maximum(m, jnp.max(sc, axis=1, keepdims=True))
                p = jnp.exp(sc - m_new)
                alpha = jnp.exp(m - m_new)
                l = alpha * l + jnp.sum(p, axis=1, keepdims=True)
                acc = alpha * acc + jnp.dot(p.astype(BF16), v, preferred_element_type=F32)
                out.append((m_new, l, acc))
            return tuple(out)

        init = tuple((jnp.full((sr, 1), NEG_BIG, F32), jnp.zeros((sr, 1), F32), jnp.zeros((sr, HEAD_DIM), F32))
                     for _ in strips)
        res = tile(i, lax.fori_loop(0, i, lambda j, c: tile(j, c, False), init), True)
        for sl, (m, l, acc) in zip(strips, res):
            o_ref[sl, :] = acc / l
            lse_ref[0, sl, :] = m + jnp.log(l)

    col_blk = pl.BlockSpec((1, t, 1), lambda h, i: (h, i, 0))
    return pl.pallas_call(
        kern, name="fox_fwd", grid=(n_heads, s // t),
        in_specs=_qkv_specs(s, t, n_heads, base) + [col_blk, pl.BlockSpec((1, 1, s), lambda h, i: (h, 0, 0))],
        out_specs=[pl.BlockSpec((t, HEAD_DIM), lambda h, i: (i, h)), col_blk],
        out_shape=[jax.ShapeDtypeStruct((s, n_heads * HEAD_DIM), F32), jax.ShapeDtypeStruct((n_heads, s, 1), F32)],
        compiler_params=_params("parallel", "arbitrary"),
    )(qkv, qkv, qkv, cum_col, cum_row)


def fox_bwd(qkv, cum_col, cum_row, o, d_o, lse, n_heads, base):
    s = qkv.shape[0]
    t = _tile(s, ATT_TILE, LANES)
    nq = s // t
    sr, ns, strips = _strips(t)
    scale = HEAD_DIM ** -0.5

    def kern(q_ref, k_ref, v_ref, cq_ref, ck_ref, o_ref, do_ref, lse_ref,
             dq_ref, dk_ref, dv_ref, dcq_ref, dck_ref, dk_acc, dv_acc, dck_acc):
        i = pl.program_id(1)

        @pl.when(i == 0)
        def _():
            dk_acc[...] = jnp.zeros_like(dk_acc)
            dv_acc[...] = jnp.zeros_like(dv_acc)
            dck_acc[...] = jnp.zeros_like(dck_acc)

        diff = _key_minus_row(sr, t)
        qs = [q_ref[sl, :] for sl in strips]
        dos = [do_ref[sl, :] for sl in strips]
        cqs = [cq_ref[0, sl, :] for sl in strips]
        lses = [lse_ref[0, sl, :] for sl in strips]
        deltas = [jnp.sum(dos[si].astype(F32) * o_ref[sl, :], axis=1, keepdims=True) for si, sl in enumerate(strips)]

        def tile(j, carry, diagonal):
            off = pl.multiple_of(j * t, t)
            k = k_ref[pl.ds(off, t), :]
            v = v_ref[pl.ds(off, t), :]
            ck = ck_ref[0, :, pl.ds(off, t)]
            out = []
            dk_t = jnp.zeros((t, HEAD_DIM), F32)
            dv_t = jnp.zeros((t, HEAD_DIM), F32)
            dck_t = jnp.zeros((1, t), F32)
            for si in range(ns):
                dq, dcq = carry[si]
                sc, valid = _fox_scores(qs[si], k, cqs[si], ck, diff, si * sr + 1 if diagonal else None)
                p = _keep(valid, jnp.exp(sc - lses[si]))
                dp = lax.dot_general(dos[si], v, NT, preferred_element_type=F32)
                ds = p * (dp - deltas[si])
                dsb = (ds * scale).astype(BF16)
                dq = dq + jnp.dot(dsb, k, preferred_element_type=F32)
                dk_t = dk_t + lax.dot_general(dsb, qs[si], TN, preferred_element_type=F32)
                dv_t = dv_t + lax.dot_general(p.astype(BF16), dos[si], TN, preferred_element_type=F32)
                dck_t = dck_t + jnp.sum(ds, axis=0, keepdims=True)
                out.append((dq, dcq + jnp.sum(ds, axis=1, keepdims=True)))
            dk_acc[pl.ds(off, t), :] += dk_t
            dv_acc[pl.ds(off, t), :] += dv_t
            dck_acc[:, pl.ds(off, t)] -= dck_t
            return tuple(out)

        init = tuple((jnp.zeros((sr, HEAD_DIM), F32), jnp.zeros((sr, 1), F32)) for _ in strips)
        res = tile(i, lax.fori_loop(0, i, lambda j, c: tile(j, c, False), init), True)
        for sl, (dq, dcq) in zip(strips, res):
            dq_ref[sl, :] = dq.astype(BF16)
            dcq_ref[0, sl, :] = dcq

        @pl.when(i == nq - 1)
        def _():
            dk_ref[...] = dk_acc[...].astype(BF16)
            dv_ref[...] = dv_acc[...].astype(BF16)
            dck_ref[0] = dck_acc[...]

    out = jax.ShapeDtypeStruct((s, n_heads * HEAD_DIM), BF16)
    head_blk = pl.BlockSpec((s, HEAD_DIM), lambda h, i: (0, h))
    tile_blk = pl.BlockSpec((t, HEAD_DIM), lambda h, i: (i, h))
    col_blk = pl.BlockSpec((1, t, 1), lambda h, i: (h, i, 0))
    row_blk = pl.BlockSpec((1, 1, s), lambda h, i: (h, 0, 0))
    return pl.pallas_call(
        kern, name="fox_bwd", grid=(n_heads, nq),
        in_specs=_qkv_specs(s, t, n_heads, base) + [col_blk, row_blk, tile_blk, tile_blk, col_blk],
        out_specs=[tile_blk, head_blk, head_blk, col_blk, row_blk],
        out_shape=[out, out, out, jax.ShapeDtypeStruct((n_heads, s, 1), F32),
                   jax.ShapeDtypeStruct((n_heads, 1, s), F32)],
        scratch_shapes=[pltpu.VMEM((s, HEAD_DIM), F32), pltpu.VMEM((s, HEAD_DIM), F32), pltpu.VMEM((1, s), F32)],
        compiler_params=_params("parallel", "arbitrary"),
    )(qkv, qkv, qkv, cum_col, cum_row, o, d_o, lse)


def _place():
    x, y, c = lax.axis_index("x"), lax.axis_index("y"), lax.axis_index("c")
    other_chips = [(1 - x, y), (x, 1 - y), (1 - x, 1 - y)]
    return x, y, c, other_chips


ANY = pl.BlockSpec(memory_space=pl.ANY)


def _remote(src, dst, send_sem, recv_sem, dev):
    return pltpu.make_async_remote_copy(src_ref=src, dst_ref=dst, send_sem=send_sem, recv_sem=recv_sem,
                                        device_id=dev, device_id_type=MESH)


def place_transposed(name, w_t, chip):
    c, _, r = w_t.shape
    tc = LANES

    def kern(chip_ref, w_ref, o_ref):
        o_ref[...] = w_ref[:, 0, :].T.astype(BF16)

    return pl.pallas_call(
        kern, name=name,
        grid_spec=pltpu.PrefetchScalarGridSpec(
            num_scalar_prefetch=1, grid=(pl.cdiv(c, tc),),
            in_specs=[pl.BlockSpec((tc, 1, r), lambda j, chip_ref: (j, 0, 0))],
            out_specs=pl.BlockSpec((None, r, tc), lambda j, chip_ref: (chip_ref[0], 0, j))),
        out_shape=jax.ShapeDtypeStruct((N_CHIPS, r, c), BF16),
        compiler_params=_params("parallel"),
    )(chip, w_t)


def adam_update_transposed(name, w_t, m_t, v_t, g_buf):
    c, _, r = w_t.shape
    tc = LANES

    def kern(w_ref, m_ref, v_ref, g_ref, go_ref, dl_ref, nm_ref, nv_ref):
        g = g_ref[...].T
        delta, nm, nv = _adam(w_ref[:, 0, :], g, m_ref[:, 0, :], v_ref[:, 0, :])
        go_ref[:, 0, :] = g
        dl_ref[:, 0, :] = delta
        nm_ref[:, 0, :] = nm
        nv_ref[:, 0, :] = nv

    blk = pl.BlockSpec((tc, 1, r), lambda j: (j, 0, 0))
    out = jax.ShapeDtypeStruct((c, 1, r), F32)
    return pl.pallas_call(
        kern, name=name, grid=(pl.cdiv(c, tc),),
        in_specs=[blk, blk, blk, pl.BlockSpec((r, tc), lambda j: (0, j))],
        out_specs=[blk] * 4, out_shape=[out] * 4, compiler_params=_params("parallel"),
    )(w_t, m_t, v_t, g_buf)


def cast_place(name, ws, chip):
    r = ws[0].shape[1]
    cs = [w.shape[2] for w in ws]
    tr = _tile(r, 256, 16)

    def kern(chip_ref, *refs):
        o_ref = refs[-1]
        off = 0
        for w_ref, c in zip(refs[:-1], cs):
            o_ref[:, off:off + c] = w_ref[...].astype(BF16)
            off += c

    return pl.pallas_call(
        kern, name=name,
        grid_spec=pltpu.PrefetchScalarGridSpec(
            num_scalar_prefetch=1, grid=(r // tr,),
            in_specs=[pl.BlockSpec((None, tr, c), lambda i, chip_ref: (0, i, 0)) for c in cs],
            out_specs=pl.BlockSpec((None, tr, sum(cs)), lambda i, chip_ref: (chip_ref[0], i, 0))),
        out_shape=jax.ShapeDtypeStruct((N_CHIPS, r, sum(cs)), BF16),
        compiler_params=_params("parallel"),
    )(chip, *ws)


HBM = pl.BlockSpec(memory_space=pltpu.HBM)
SEM = pl.BlockSpec(memory_space=pltpu.SEMAPHORE)
SPLIT = pltpu.CompilerParams(has_side_effects=pltpu.SideEffectType.DATAFLOW_SIDE_EFFECTING)


def _in_hbm(a):
    return pltpu.with_memory_space_constraint(a, pltpu.HBM)


def _slab_rows(ref, k, core):
    half = ref.shape[1] // 2
    return ref.at[k, pl.ds(pl.multiple_of(core * half, 16), half)]


def gather_start(name, bufs):
    n = len(bufs)

    def body(*refs):
        ins, send, recv, token = refs[:n], refs[n], refs[n + 1], refs[-1]
        x, y, c, chips = _place()
        me = 2 * x + y
        for a in range(n):
            for j in range(3):
                rows = _slab_rows(ins[a], me, c)
                _remote(rows, rows, send.at[3 * a + j], recv.at[3 * a + j], (chips[j][0], chips[j][1], c)).start()
        token[...] = jnp.zeros_like(token)

    sem = pltpu.SemaphoreType.DMA((3 * n,))
    res = pl.pallas_call(
        body, name=name, in_specs=[HBM] * n, out_specs=[SEM, SEM] + [HBM] * n + [pl.BlockSpec(memory_space=pltpu.VMEM)],
        out_shape=[sem, sem] + [pltpu.HBM(b.shape, b.dtype) for b in bufs] + [jax.ShapeDtypeStruct((8, LANES), F32)],
        input_output_aliases={a: 2 + a for a in range(n)}, compiler_params=SPLIT,
    )(*[_in_hbm(b) for b in bufs])
    return res[0], res[1], res[2:2 + n], res[-1]


def gather_wait(name, bufs, send_sems, recv_sems, after):
    n = len(bufs)

    def body(*refs):
        ins, send, recv = refs[:n], refs[n], refs[n + 1]
        x, y, c, chips = _place()
        me = 2 * x + y
        for a in range(n):
            for j in range(3):
                dev = (chips[j][0], chips[j][1], c)
                mine = _slab_rows(ins[a], me, c)
                _remote(mine, mine, send.at[3 * a + j], recv.at[3 * a + j], dev).wait_send()
                land = _slab_rows(ins[a], 2 * chips[j][0] + chips[j][1], c)
                _remote(land, land, send.at[3 * a + j], recv.at[3 * a + j], dev).wait_recv()

    return pl.pallas_call(
        body, name=name, in_specs=[HBM] * n + [SEM, SEM] + [ANY] * len(after), out_specs=[HBM] * n,
        out_shape=[pltpu.HBM(b.shape, b.dtype) for b in bufs],
        input_output_aliases={a: a for a in range(n)}, compiler_params=SPLIT,
    )(*bufs, send_sems, recv_sems, *after)


def gather_forward(name, bufs):
    n = len(bufs)

    def body(*refs):
        outs = refs[n:2 * n]
        send_sems, recv_sems = refs[2 * n:]
        x, y, c, chips = _place()
        sibling = (x, y, 1 - c)

        def d2d(a, j, core):
            rows = _slab_rows(outs[a], 2 * chips[j][0] + chips[j][1], core)
            return _remote(rows, rows, send_sems.at[3 * a + j], recv_sems.at[3 * a + j], sibling)

        pairs = [(a, j) for a in range(n) for j in range(3)]
        for a, j in pairs:
            d2d(a, j, c).start()
        for a, j in pairs:
            d2d(a, j, 1 - c).wait_recv()
        for a, j in pairs:
            d2d(a, j, c).wait_send()

    return pl.pallas_call(
        body, name=name, in_specs=[ANY] * n, out_specs=[ANY] * n,
        out_shape=[jax.ShapeDtypeStruct(b.shape, b.dtype) for b in bufs],
        input_output_aliases={a: a for a in range(n)},
        scratch_shapes=[pltpu.SemaphoreType.DMA((3 * n,)), pltpu.SemaphoreType.DMA((3 * n,))],
    )(*bufs)


def _forward_plan(refs):
    x, y, c, chips = _place()
    out = []
    for ref in refs:
        for j in range(3):
            k = 2 * chips[j][0] + chips[j][1]
            out.append((_slab_rows(ref, k, c), _slab_rows(ref, k, c), _slab_rows(ref, k, 1 - c)))
    return out


def _join_plan(refs):
    x, y, c, _ = _place()
    out = []
    for ref in refs:
        half = ref.shape[0] // 2
        mine = ref.at[pl.ds(pl.multiple_of(c * half, 8), half)]
        out.append((mine, mine, ref.at[pl.ds(pl.multiple_of((1 - c) * half, 8), half)]))
    return out


def _swap_plan(refs):
    x, y, c, _ = _place()
    n = len(refs) // 2
    out = []
    for a in range(n):
        half = refs[a].shape[1] // 2
        src = refs[a].at[:, pl.ds(pl.multiple_of((1 - c) * half, 16), half), :]
        out.append((src, refs[n + a], refs[n + a]))
    return out


def sibling_start(name, arrays, plan, n_copies, after=()):
    n = len(arrays)

    n_in = n + len(after)

    def body(*refs):
        send, recv, token = refs[n_in], refs[n_in + 1], refs[-1]
        x, y, c, _ = _place()
        for idx, (src, dst, _) in enumerate(plan(refs[:n])):
            _remote(src, dst, send.at[idx], recv.at[idx], (x, y, 1 - c)).start()
        token[...] = jnp.zeros_like(token)

    sem = pltpu.SemaphoreType.DMA((n_copies,))
    res = pl.pallas_call(
        body, name=name, in_specs=[HBM] * n + [ANY] * len(after),
        out_specs=[SEM, SEM] + [HBM] * n + [pl.BlockSpec(memory_space=pltpu.VMEM)],
        out_shape=[sem, sem] + [pltpu.HBM(b.shape, b.dtype) for b in arrays] + [jax.ShapeDtypeStruct((8, LANES), F32)],
        input_output_aliases={a: 2 + a for a in range(n)}, compiler_params=SPLIT,
    )(*[_in_hbm(b) for b in arrays], *after)
    return res[0], res[1], res[2:2 + n], res[-1]


def sibling_wait(name, arrays, send_sems, recv_sems, plan, after):
    n = len(arrays)

    def body(*refs):
        send, recv = refs[n], refs[n + 1]
        x, y, c, _ = _place()
        for idx, (src, dst, filled) in enumerate(plan(refs[:n])):
            _remote(src, dst, send.at[idx], recv.at[idx], (x, y, 1 - c)).wait_send()
            _remote(filled, filled, send.at[idx], recv.at[idx], (x, y, 1 - c)).wait_recv()

    return pl.pallas_call(
        body, name=name, in_specs=[HBM] * n + [SEM, SEM] + [ANY] * len(after), out_specs=[HBM] * n,
        out_shape=[pltpu.HBM(b.shape, b.dtype) for b in arrays],
        input_output_aliases={a: a for a in range(n)}, compiler_params=SPLIT,
    )(*arrays, send_sems, recv_sems, *after)


def swap_halves(name, pieces):
    n = len(pieces)
    halves = [p.shape[1] // 2 for p in pieces]

    def body(*refs):
        ins, outs = refs[:n], refs[n:2 * n]
        send_sems, recv_sems = refs[2 * n:]
        x, y, c, _ = _place()
        cps = [_remote(ins[a].at[:, pl.ds(pl.multiple_of((1 - c) * halves[a], 16), halves[a]), :], outs[a],
                       send_sems.at[a], recv_sems.at[a], (x, y, 1 - c)) for a in range(n)]
        for cp in cps:
            cp.start()
        for cp in cps:
            cp.wait()

    return pl.pallas_call(
        body, name=name, in_specs=[ANY] * n, out_specs=[ANY] * n,
        out_shape=[jax.ShapeDtypeStruct((N_CHIPS, h, p.shape[2]), p.dtype) for p, h in zip(pieces, halves)],
        scratch_shapes=[pltpu.SemaphoreType.DMA((n,)), pltpu.SemaphoreType.DMA((n,))],
    )(*pieces)


def pair_sum(name, pieces, got, core):
    _, r, w = pieces.shape
    half = r // 2
    tr = _tile(half, 256, 16)

    def kern(core_ref, p_ref, g_ref, o_ref):
        o_ref[...] = (p_ref[...].astype(F32) + g_ref[...].astype(F32)).astype(o_ref.dtype)

    return pl.pallas_call(
        kern, name=name,
        grid_spec=pltpu.PrefetchScalarGridSpec(
            num_scalar_prefetch=1, grid=(N_CHIPS, half // tr),
            in_specs=[pl.BlockSpec((None, None, tr, w), lambda k, i, core_ref: (k, core_ref[0], i, 0)),
                      pl.BlockSpec((None, tr, w), lambda k, i, core_ref: (k, i, 0))],
            out_specs=pl.BlockSpec((None, tr, w), lambda k, i, core_ref: (k, i, 0))),
        out_shape=jax.ShapeDtypeStruct((N_CHIPS, half, w), pieces.dtype),
        compiler_params=_params("parallel", "parallel"),
    )(core, pieces.reshape(N_CHIPS, 2, half, w), got)


def _scatter_copies(sums, lands, send, recv):
    x, y, c, chips = _place()
    return [_remote(sums[a].at[2 * chips[j][0] + chips[j][1]], lands[a].at[j], send.at[3 * a + j], recv.at[3 * a + j],
                    (chips[j][0], chips[j][1], c)) for a in range(len(sums)) for j in range(3)]


def scatter_start(name, sums):
    n = len(sums)
    lands = [lax.empty((3,) + t.shape[1:], t.dtype) for t in sums]

    def body(*refs):
        ins, land_in, send, recv, token = refs[:n], refs[n:2 * n], refs[2 * n], refs[2 * n + 1], refs[-1]
        for cp in _scatter_copies(ins, land_in, send, recv):
            cp.start()
        token[...] = jnp.zeros_like(token)

    sem = pltpu.SemaphoreType.DMA((3 * n,))
    res = pl.pallas_call(
        body, name=name, in_specs=[HBM] * (2 * n),
        out_specs=[SEM, SEM] + [HBM] * (2 * n) + [pl.BlockSpec(memory_space=pltpu.VMEM)],
        out_shape=[sem, sem] + [pltpu.HBM(t.shape, t.dtype) for t in sums + lands] + [jax.ShapeDtypeStruct((8, LANES), F32)],
        input_output_aliases={a: 2 + a for a in range(2 * n)}, compiler_params=SPLIT,
    )(*[_in_hbm(t) for t in sums + lands])
    return res[0], res[1], res[2:2 + n], res[2 + n:2 + 2 * n], res[-1]


def scatter_wait(name, sums, lands, send_sems, recv_sems, after):
    n = len(sums)

    def body(*refs):
        ins, land_in, send, recv = refs[:n], refs[n:2 * n], refs[2 * n], refs[2 * n + 1]
        for cp in _scatter_copies(ins, land_in, send, recv):
            cp.wait_send()
            cp.wait_recv()

    res = pl.pallas_call(
        body, name=name, in_specs=[HBM] * (2 * n) + [SEM, SEM, ANY], out_specs=[HBM] * (2 * n),
        out_shape=[pltpu.HBM(t.shape, t.dtype) for t in sums + lands],
        input_output_aliases={a: a for a in range(2 * n)}, compiler_params=SPLIT,
    )(*sums, *lands, send_sems, recv_sems, after)
    return res[:n], res[n:]


def chip_sum(name, sums, got, chip, core):
    _, half, w = sums.shape
    tr = _tile(half, 256, 16)
    nb = half // tr

    def kern(ids_ref, s_ref, g0_ref, g1_ref, g2_ref, o_ref):
        o_ref[...] = ((s_ref[...].astype(F32) + g0_ref[...].astype(F32)) + g1_ref[...].astype(F32)) \
            + g2_ref[...].astype(F32)

    def got_spec(j):
        return pl.BlockSpec((None, tr, w), lambda i, ids_ref: (j, i, 0))

    return pl.pallas_call(
        kern, name=name,
        grid_spec=pltpu.PrefetchScalarGridSpec(
            num_scalar_prefetch=1, grid=(nb,),
            in_specs=[pl.BlockSpec((None, tr, w), lambda i, ids_ref: (ids_ref[0], i, 0)),
                      got_spec(0), got_spec(1), got_spec(2)],
            out_specs=pl.BlockSpec((tr, w), lambda i, ids_ref: (ids_ref[1] * nb + i, 0))),
        out_shape=jax.ShapeDtypeStruct((2 * half, w), F32),
        compiler_params=_params("parallel"),
    )(jnp.concatenate([chip, core]), sums, got, got, got)


def join_halves(name, shards):
    n = len(shards)
    halves = [g.shape[0] // 2 for g in shards]

    def body(*refs):
        outs = refs[n:2 * n]
        send_sems, recv_sems = refs[2 * n:]
        x, y, c, _ = _place()
        cps = []
        for a in range(n):
            rows = outs[a].at[pl.ds(pl.multiple_of(c * halves[a], 8), halves[a])]
            cps.append(_remote(rows, rows, send_sems.at[a], recv_sems.at[a], (x, y, 1 - c)))
        for cp in cps:
            cp.start()
        for cp in cps:
            cp.wait()

    return pl.pallas_call(
        body, name=name, in_specs=[ANY] * n, out_specs=[ANY] * n,
        out_shape=[jax.ShapeDtypeStruct(g.shape, g.dtype) for g in shards],
        input_output_aliases={a: a for a in range(n)},
        scratch_shapes=[pltpu.SemaphoreType.DMA((n,)), pltpu.SemaphoreType.DMA((n,))],
    )(*shards)


def _adam(w, g, m, v):
    m = ADAM_B1 * m + (1.0 - ADAM_B1) * g
    v = ADAM_B2 * v + (1.0 - ADAM_B2) * (g * g)
    m_hat = m / (1.0 - ADAM_B1 ** ADAM_STEP)
    v_hat = v / (1.0 - ADAM_B2 ** ADAM_STEP)
    delta = -ADAM_LR * (m_hat / (jnp.sqrt(v_hat) + ADAM_EPS) + ADAM_WD * w)
    return delta, m, v


def small_allreduce_adam(g_part, w, m, v, after):
    n_dev = 8
    r, d = g_part.shape

    def body(g_ref, w_ref, m_ref, v_ref, after_ref, gs_ref, dl_ref, nm_ref, nv_ref, all_ref, send_sems, recv_sems):
        x, y, c, _ = _place()
        me = 4 * x + 2 * y + c
        all_ref[me] = g_ref[...]
        cps = []
        for rel in range(1, n_dev):
            px = 1 - x if rel & 4 else x
            py = 1 - y if rel & 2 else y
            pc = 1 - c if rel & 1 else c
            cps.append(_remote(g_ref, all_ref.at[me], send_sems.at[rel - 1], recv_sems.at[rel - 1], (px, py, pc)))
        for cp in cps:
            cp.start()
        for cp in cps:
            cp.wait()
        total = all_ref[0]
        for dev in range(1, n_dev):
            total = total + all_ref[dev]
        gs_ref[...] = total
        delta, nm, nv = _adam(w_ref[...], total, m_ref[...], v_ref[...])
        dl_ref[...] = delta
        nm_ref[...] = nm
        nv_ref[...] = nv

    vm = pl.BlockSpec(memory_space=pltpu.VMEM)
    out = jax.ShapeDtypeStruct((r, d), F32)
    return pl.pallas_call(
        body, name="small_allreduce_adam", in_specs=[vm, vm, vm, vm, ANY], out_specs=[vm, vm, vm, vm],
        out_shape=[out, out, out, out],
        scratch_shapes=[pltpu.VMEM((n_dev, r, d), F32), pltpu.SemaphoreType.DMA((n_dev - 1,)),
                        pltpu.SemaphoreType.DMA((n_dev - 1,))],
    )(g_part, w, m, v, after)


def adam_update(name, w, m, v, g_buf, col_blk, after=None):
    w, m, v = w[0], m[0], v[0]
    r, c = w.shape
    tr = _tile(r, 128, 8)
    extra = [] if after is None else [after]

    def kern(w_ref, m_ref, v_ref, g_ref, *rest):
        go_ref, dl_ref, nm_ref, nv_ref = rest[len(extra):]
        g = g_ref[...]
        delta, nm, nv = _adam(w_ref[...], g, m_ref[...], v_ref[...])
        go_ref[...] = g
        dl_ref[...] = delta
        nm_ref[...] = nm
        nv_ref[...] = nv

    blk = pl.BlockSpec((tr, c), lambda i: (i, 0))
    out = jax.ShapeDtypeStruct((r, c), F32)
    res = pl.pallas_call(
        kern, name=name, grid=(r // tr,),
        in_specs=[blk, blk, blk, pl.BlockSpec((tr, c), lambda i: (i, col_blk))] + [ANY] * len(extra),
        out_specs=[blk] * 4, out_shape=[out] * 4, compiler_params=_params("parallel"),
    )(w, m, v, g_buf, *extra)
    return [a[None] for a in res]


def _w_in_segments(cw, n_qkv, n_heads, d):
    out = []

    def add(lo, hi, main):
        while lo < hi:
            k, a = divmod(lo, cw)
            w = min(cw - a, hi - lo)
            out.append((k, a, main, w))
            lo, main = lo + w, main + w

    add(0, n_qkv, 0)
    add(n_qkv + n_heads, N_CHIPS * cw, n_qkv)
    add(n_qkv, n_qkv + n_heads, n_qkv + 2 * d)
    return out


def regroup_w_in(g_in, segments, n_main):
    _, d, cw = g_in.shape
    tr = _tile(d, 128, 16)
    n_real = max(m + w for _, _, m, w in segments)

    def kern(s_ref, o_ref):
        for k, a, m, w in segments:
            o_ref[:, m:m + w] = s_ref[k, :, a:a + w]
        o_ref[:, n_real:] = jnp.zeros((tr, n_main - n_real), o_ref.dtype)

    return pl.pallas_call(
        kern, name="regroup_w_in", grid=(d // tr,),
        in_specs=[pl.BlockSpec((N_CHIPS, tr, cw), lambda i: (0, i, 0))],
        out_specs=pl.BlockSpec((tr, n_main), lambda i: (i, 0)),
        out_shape=jax.ShapeDtypeStruct((d, n_main), g_in.dtype), compiler_params=_params("parallel"),
    )(g_in)


def regroup_dw_in(dw_main, segments, cw):
    d, n_main = dw_main.shape
    tr = _tile(d, 128, 16)

    def kern(s_ref, o_ref):
        for k, a, m, w in segments:
            o_ref[k, :, a:a + w] = s_ref[:, m:m + w]

    return pl.pallas_call(
        kern, name="regroup_dw_in", grid=(d // tr,),
        in_specs=[pl.BlockSpec((tr, n_main), lambda i: (i, 0))],
        out_specs=pl.BlockSpec((N_CHIPS, tr, cw), lambda i: (0, i, 0)),
        out_shape=jax.ShapeDtypeStruct((N_CHIPS, d, cw), dw_main.dtype), compiler_params=_params("parallel"),
    )(dw_main)


def kernel(x, norm_mix_pre, norm_mix_post, w_in, b_forget, w_branch_sb, w_branch_fox, w_out, norm_ffn_pre, norm_ffn_post, w_ffn_gate, w_ffn_up, w_ffn_down, loss_target, m_norm_mix_pre, m_norm_mix_post, m_w_in, m_b_forget, m_w_branch_sb, m_w_branch_fox, m_w_out, m_norm_ffn_pre, m_norm_ffn_post, m_w_ffn_gate, m_w_ffn_up, m_w_ffn_down, v_norm_mix_pre, v_norm_mix_post, v_w_in, v_b_forget, v_w_branch_sb, v_w_branch_fox, v_w_out, v_norm_ffn_pre, v_norm_ffn_post, v_w_ffn_gate, v_w_ffn_up, v_w_ffn_down):
    s, d = x.shape[1], x.shape[2]
    n_heads = b_forget.shape[1]
    d_att = n_heads * HEAD_DIM
    c_in = w_in.shape[2]
    c_br = w_branch_sb.shape[2]
    c_gu = w_ffn_gate.shape[2]
    d_ff = c_gu * N_CHIPS
    d_in = c_in * N_CHIPS
    f_pad = 512
    n_qkv = 6 * d_att
    n_gf = 2 * d + f_pad
    core = lax.axis_index("c").astype(jnp.int32).reshape(1)
    chip = (2 * lax.axis_index("x") + lax.axis_index("y")).astype(jnp.int32).reshape(1)

    as_t = lambda a: jnp.transpose(a, (2, 0, 1))
    w_in_t, m_in_t, v_in_t = as_t(w_in), as_t(m_w_in), as_t(v_w_in)
    in_send, in_recv, in_bufs, in_token = gather_start(
        "gather_start_w_in", [place_transposed("place_w_in", w_in_t, chip)])
    sm_send, sm_recv, sm_bufs, sm_token = gather_start("gather_start_small", [
        cast_place("place_branch", [w_branch_sb, w_branch_fox], chip),
        cast_place("place_out", [w_out + in_token[0, 0]], chip)])
    ag_send, ag_recv, ag_bufs, ag_token = gather_start("gather_start_ffn", [
        cast_place("place_gate_up", [w_ffn_gate, w_ffn_up], chip),
        cast_place("place_down", [w_ffn_down + sm_token[0, 0]], chip)])
    g_in, = gather_forward("forward_w_in", gather_wait("gather_wait_w_in", in_bufs, in_send, in_recv,
                                                       [ag_token]))
    segments = _w_in_segments(c_in, n_qkv, n_heads, d)
    w_main = regroup_w_in(g_in, segments, n_qkv + n_gf)
    x2 = x[0]
    tgt = loss_target[0]
    b_pad = jnp.pad(b_forget, ((0, 0), (0, LANES - n_heads)))

    u = norm_in(x2, norm_mix_pre)
    qkv = mm(u, w_main, "nn", BF16, "proj_qkv", b_win=(0, n_qkv))
    gf = mm(u, w_main, "nn", F32, "proj_gates", b_win=(n_qkv, n_gf))
    cum = cum_fwd(gf, b_pad, 2 * d)
    cum_heads = cum[:, :n_heads].T
    cum_col, cum_row = cum_heads[:, :, None], cum_heads[:, None, :]
    o_sb, sb_runs = sb_fwd(qkv, n_heads, 0)
    o_fx, lse = fox_fwd(qkv, cum_col, cum_row, n_heads, 3 * n_heads)
    g_br, g_out = gather_forward("forward_small",
                                 gather_wait("gather_wait_small", sm_bufs, sm_send, sm_recv, [o_sb, o_fx]))
    w_o = g_out.reshape(d, d)
    bsb = mm(o_sb, g_br, "nn", F32, "branch_sb", tn=c_br, chunks=(1, 0))
    bfx = mm(o_fx, g_br, "nn", F32, "branch_fox", tn=c_br, chunks=(1, 1))
    merged = gate_fwd(bsb, bfx, gf)
    ffn_bufs = gather_wait("gather_wait_ffn", ag_bufs, ag_send, ag_recv, [merged])
    fb_send, fb_recv, fb_bufs, fb_token = sibling_start("forward_big_start", ffn_bufs, _forward_plan, 6)
    mix = mm(merged, w_o, "nn", F32, "out_proj", after=fb_token)
    h1, u2 = mid_fwd(x2, mix, norm_mix_post, norm_ffn_pre)
    g_gu, g_dn = sibling_wait("forward_big_wait", fb_bufs, fb_send, fb_recv, _forward_plan, [u2])
    w_dn = g_dn.reshape(d_ff, d)
    gu, act = ffn_up_fused(u2, g_gu, c_gu)
    ff = mm(act, w_dn, "nn", F32, "ffn_down")
    dy, d_ff_out, dg_fpost, loss_part = loss_head(h1, ff, norm_ffn_post, tgt)

    p_dn = mm(act, d_ff_out, "tn", BF16, "dw_ffn_down").reshape(N_CHIPS, d_ff // N_CHIPS, d)
    d_gu = ffn_down_bwd_fused(d_ff_out, w_dn, gu, c_gu)
    p_gu = mm(u2, d_gu, "tn", BF16, "dw_ffn_gate_up", tn=c_gu, chunks=(2, 0),
              out_into=lax.empty((N_CHIPS, d, 2 * c_gu), BF16))
    sw_send, sw_recv, sw_arrs, sw_token = sibling_start(
        "swap_big_start", [p_gu, p_dn, lax.empty((N_CHIPS, d // 2, 2 * c_gu), BF16),
                           lax.empty((N_CHIPS, d_ff // N_CHIPS // 2, d), BF16)], _swap_plan, 2)
    du2 = mm(d_gu, g_gu, "nt", F32, "d_u2", tk=c_gu, chunks=(2, 0), after=sw_token)
    dh1, d_mix, dg_fpre, dg_post = mid_bwd(dy, du2, h1, mix, norm_ffn_pre, norm_mix_post)
    p_out = mm(merged, d_mix, "tn", BF16, "dw_out").reshape(N_CHIPS, d // N_CHIPS, d)
    d_merged = mm(d_mix, w_o, "nt", F32, "d_merged")
    d_bsb, d_bfx, d_gs, d_gx = gate_bwd(d_merged, bsb, bfx, gf)
    p_br = mm(o_sb, d_bsb, "tn", BF16, "dw_branch_sb", tn=c_br, chunks=(1, 0),
              out_into=lax.empty((N_CHIPS, d_att, 2 * c_br), BF16))
    p_br = mm(o_fx, d_bfx, "tn", BF16, "dw_branch_fox", tn=c_br, chunks=(1, 1), out_into=p_br)
    d_osb = mm(d_bsb, g_br, "nt", BF16, "d_o_sb", tk=c_br, chunks=(1, 0))
    d_ofx = mm(d_bfx, g_br, "nt", BF16, "d_o_fox", tk=c_br, chunks=(1, 1))

    def reduce_start(tag, pieces, names):
        from_sibling = swap_halves("swap_halves_" + tag, pieces)
        sums = [pair_sum("pair_sum_" + t, p, q, core) for t, p, q in zip(names, pieces, from_sibling)]
        return scatter_start("scatter_start_" + tag, sums)

    def reduce_end(tag, started, names, after):
        send, recv, sums, lands, _ = started
        sums, lands = scatter_wait("scatter_wait_" + tag, sums, lands, send, recv, after)
        return join_halves("join_halves_" + tag, [chip_sum("chip_sum_" + t, sm, got, chip, core)
                                                  for t, sm, got in zip(names, sums, lands)])

    rest_names = ["branch", "out", "gate_up", "down"]
    p_gu, p_dn, q_gu, q_dn = sibling_wait("swap_big_wait", sw_arrs, sw_send, sw_recv, _swap_plan, [p_br])
    q_br, q_out = swap_halves("swap_halves_small", [p_br, p_out])
    rest_started = scatter_start("scatter_start_rest", [
        pair_sum("pair_sum_" + t, p, q, core)
        for t, p, q in zip(rest_names, [p_br, p_out, p_gu, p_dn], [q_br, q_out, q_gu, q_dn])])
    d_osb = d_osb + rest_started[4][0, 0].astype(BF16)
    dq_s, dk_s, dv_s = sb_bwd(qkv, d_osb, sb_runs, n_heads, 0)
    dq_f, dk_f, dv_f, dcq, dck = fox_bwd(qkv, cum_col, cum_row, o_fx, d_ofx, lse, n_heads, 3 * n_heads)
    d_cum = jnp.pad((dcq[:, :, 0] + dck[:, 0, :]).T, ((0, 0), (0, LANES - n_heads)))
    d_f, db_pad = cum_bwd(d_cum, gf, b_pad, 2 * d, n_heads)
    d_main = jnp.concatenate(
        [dq_s, dk_s, dv_s, dq_f, dk_f, dv_f, d_gs, d_gx, d_f, jnp.zeros((s, f_pad - LANES), BF16)], axis=1)
    p_in = regroup_dw_in(mm(u, d_main, "tn", BF16, "dw_in"), segments, c_in)

    in_started = reduce_start("w_in", [p_in], ["in"])
    rs_send, rs_recv, rs_sums, rs_lands, _ = rest_started
    rs_sums, rs_lands = scatter_wait("scatter_wait_rest", rs_sums, rs_lands, rs_send, rs_recv, in_started[4])
    j_send, j_recv, j_shards, j_token = sibling_start(
        "join_rest_start", [chip_sum("chip_sum_" + t, sm, got, chip, core)
                            for t, sm, got in zip(rest_names, rs_sums, rs_lands)], _join_plan, 4)
    du = mm(d_main, w_main, "nt", F32, "d_u", after=j_token)
    dx, dg_pre = in_bwd(dh1, du, x2, norm_mix_pre + in_started[4][0:1, 0:1])
    gr_br, gr_out, gr_gu, gr_dn = sibling_wait("join_rest_wait", j_shards, j_send, j_recv, _join_plan, [dx])

    upd_bs = adam_update("adam_branch_sb", w_branch_sb, m_w_branch_sb, v_w_branch_sb, gr_br, 0)
    upd_bf = adam_update("adam_branch_fox", w_branch_fox, m_w_branch_fox, v_w_branch_fox, gr_br, 1)
    upd_o = adam_update("adam_out", w_out, m_w_out, v_w_out, gr_out, 0)
    upd_ga = adam_update("adam_gate", w_ffn_gate, m_w_ffn_gate, v_w_ffn_gate, gr_gu, 0)
    upd_up = adam_update("adam_up", w_ffn_up, m_w_ffn_up, v_w_ffn_up, gr_gu, 1)
    upd_dn = adam_update("adam_down", w_ffn_down, m_w_ffn_down, v_w_ffn_down, gr_dn, 0)

    done = sum(u_[1][0, 0:1, 0:1] for u_ in (upd_bs, upd_bf, upd_o, upd_ga, upd_up, upd_dn))
    gr_in, = reduce_end("w_in", in_started, ["in"], done)
    upd_in_t = adam_update_transposed("adam_w_in", w_in_t, m_in_t, v_in_t, gr_in)
    upd_in = [jnp.transpose(a, (1, 2, 0)) for a in upd_in_t]
    grads, deltas, new_ms, new_vs = zip(upd_in, upd_bs, upd_bf, upd_o, upd_ga, upd_up, upd_dn)

    def pack(rows):
        rows = [jnp.pad(r_, ((0, 0), (0, d - r_.shape[1]))) for r_ in rows]
        return jnp.concatenate(rows + [jnp.zeros((8 - len(rows), d), F32)], axis=0)

    sm_g, sm_d, sm_m, sm_v = small_allreduce_adam(
        pack([dg_pre, dg_post, dg_fpre, dg_fpost, db_pad]),
        pack([norm_mix_pre, norm_mix_post, norm_ffn_pre, norm_ffn_post, b_forget]),
        pack([m_norm_mix_pre, m_norm_mix_post, m_norm_ffn_pre, m_norm_ffn_post, m_b_forget]),
        pack([v_norm_mix_pre, v_norm_mix_post, v_norm_ffn_pre, v_norm_ffn_post, v_b_forget]), after=upd_in_t[1])

    def small(a):
        return [a[0:1], a[1:2], a[2:3], a[3:4], a[4:5, :n_heads]]

    def ordered(sm, bg):
        return [sm[0], sm[1], bg[0], sm[4], bg[1], bg[2], bg[3], sm[2], sm[3], bg[4], bg[5], bg[6]]

    loss = lax.psum(loss_part[0, 0], ("x", "y", "c"))
    return (loss, dx[None], *ordered(small(sm_g), grads), *ordered(small(sm_d), deltas),
            *ordered(small(sm_m), new_ms), *ordered(small(sm_v), new_vs))
```

```python
import functools

import jax
import jax.numpy as jnp
from jax import lax
from jax.experimental import pallas as pl
from jax.experimental.pallas import tpu as pltpu

F32 = jnp.float32
BF16 = jnp.bfloat16
MESH = pl.DeviceIdType.MESH

HEAD_DIM = 128
LANES = 128
ATT_TILE = 512
ROW_TILE = 256
N_CHIPS = 4
RMS_EPS = 1e-6
ADAM_LR = 0.001
ADAM_B1 = 0.9
ADAM_B2 = 0.999
ADAM_EPS = 1e-08
ADAM_WD = 0.01
ADAM_STEP = 10
NEG_BIG = -1e30
VMEM_LIMIT = 56 * 1024 * 1024
MM_VMEM_BUDGET = 40 * 1024 * 1024
ATT_STRIP = 512

NN = (((1,), (0,)), ((), ()))
NT = (((1,), (1,)), ((), ()))
TN = (((0,), (0,)), ((), ()))


def _tile(n, pref, align):
    best = None
    t = align
    while t <= min(n, pref):
        if n % t == 0:
            best = t
        t += align
    return n if best is None else best


def _params(*sem):
    return pltpu.CompilerParams(dimension_semantics=sem, vmem_limit_bytes=VMEM_LIMIT)


def _mm_tiles(m, n, k, a_bytes, b_bytes, out_bytes, tn, tk):
    tm = _tile(m, 2048, LANES)
    tk = tk or _tile(k, 512, LANES)

    def vmem(t):
        acc = 0 if out_bytes == 4 else tm * t * 4
        return acc + 2 * tm * t * out_bytes + 2 * (tm * tk * a_bytes + tk * t * b_bytes)

    if tn is None:
        fits = [t for t in range(LANES, min(n, 2048) + 1, LANES) if n % t == 0 and vmem(t) <= MM_VMEM_BUDGET]
        tn = max(fits) if fits else _tile(n, LANES, LANES)
    return tm, tn, tk


def mm(a, b, mode, out_dtype, name, *, tn=None, tk=None, b_win=None, chunks=None, out_into=None, after=None):
    n_per, blk0 = chunks if chunks else (1, 0)
    if mode == "nn":
        m, k = a.shape
        n = b.shape[0] * n_per * tn if chunks else (b_win[1] if b_win else b.shape[1])
    elif mode == "nt":
        m = a.shape[0]
        k = b.shape[0] * n_per * tk if chunks else a.shape[1]
        n = b.shape[-2]
    else:
        k, m = a.shape
        n = b.shape[1]
    in_place = jnp.dtype(out_dtype) == jnp.dtype(F32)
    tm, tn, tk = _mm_tiles(m, n, k, a.dtype.itemsize, b.dtype.itemsize, jnp.dtype(out_dtype).itemsize, tn, tk)
    assert m % tm == 0 and n % tn == 0 and k % tk == 0, (name, m, n, k, tm, tn, tk)
    j0 = 0
    if b_win:
        assert b_win[0] % tn == 0
        j0 = b_win[0] // tn
    nk = k // tk
    dims = {"nn": NN, "nt": NT, "tn": TN}[mode]

    def kern(a_ref, b_ref, *rest):
        o_ref, acc_ref = (rest[-1], rest[-1]) if in_place else (rest[-2], rest[-1])
        kk = pl.program_id(2)

        @pl.when(kk == 0)
        def _():
            acc_ref[...] = jnp.zeros_like(acc_ref)

        acc_ref[...] += lax.dot_general(a_ref[...].astype(BF16), b_ref[...].astype(BF16), dims,
                                        preferred_element_type=F32)

        if not in_place:
            @pl.when(kk == nk - 1)
            def _():
                o_ref[...] = acc_ref[...].astype(o_ref.dtype)

    out_spec = pl.BlockSpec((tm, tn), lambda i, j, kk: (i, j))
    out_shape = jax.ShapeDtypeStruct((m, n), out_dtype)
    if mode == "nn":
        a_spec = pl.BlockSpec((tm, tk), lambda i, j, kk: (i, kk))
        if chunks:
            b_spec = pl.BlockSpec((None, tk, tn), lambda i, j, kk: (j // n_per, kk, blk0 + j % n_per))
        else:
            b_spec = pl.BlockSpec((tk, tn), lambda i, j, kk: (kk, j + j0))
    elif mode == "nt":
        a_spec = pl.BlockSpec((tm, tk), lambda i, j, kk: (i, kk))
        if chunks:
            b_spec = pl.BlockSpec((None, tn, tk), lambda i, j, kk: (kk // n_per, j, blk0 + kk % n_per))
        else:
            b_spec = pl.BlockSpec((tn, tk), lambda i, j, kk: (j, kk))
    else:
        a_spec = pl.BlockSpec((tk, tm), lambda i, j, kk: (kk, i))
        b_spec = pl.BlockSpec((tk, tn), lambda i, j, kk: (kk, j))
        if chunks:
            out_spec = pl.BlockSpec((None, tm, tn), lambda i, j, kk: (j // n_per, i, blk0 + j % n_per))
    in_specs, operands, aliases = [a_spec, b_spec], [a, b], {}
    if chunks and mode == "tn":
        assert out_into is not None
        out_shape = jax.ShapeDtypeStruct(out_into.shape, out_dtype)
        in_specs.append(pl.BlockSpec(memory_space=pl.ANY))
        operands.append(out_into)
        aliases = {2: 0}
    if after is not None:
        in_specs.append(pl.BlockSpec(memory_space=pl.ANY))
        operands.append(after)
    return pl.pallas_call(
        kern, name=name, grid=(m // tm, n // tn, nk),
        in_specs=in_specs, out_specs=out_spec, out_shape=out_shape,
        scratch_shapes=[] if in_place else [pltpu.VMEM((tm, tn), F32)], input_output_aliases=aliases,
        compiler_params=_params("parallel", "parallel", "arbitrary"),
    )(*operands)


def _rstd(v):
    return lax.rsqrt(jnp.mean(v * v, axis=-1, keepdims=True) + RMS_EPS)


def _norm_bwd(v, g, dy):
    r = _rstd(v)
    vh = v * r
    dyg = dy * g
    dv = r * (dyg - vh * jnp.mean(dyg * vh, axis=-1, keepdims=True))
    return dv, jnp.sum(dy * vh, axis=0, keepdims=True)


def _row_call(kern, name, ins, outs, s, d):
    tr = _tile(s, ROW_TILE, 16)

    def spec(shape, is_row):
        if is_row:
            return pl.BlockSpec((tr, shape[1]), lambda i: (i, 0))
        return pl.BlockSpec(shape, lambda i: (0, 0))

    return pl.pallas_call(
        kern, name=name, grid=(s // tr,),
        in_specs=[spec(a.shape, r) for a, r in ins],
        out_specs=[spec(sh, r) for sh, _, r in outs],
        out_shape=[jax.ShapeDtypeStruct(sh, dt) for sh, dt, _ in outs],
        compiler_params=_params("arbitrary"),
    )(*[a for a, _ in ins])


def norm_in(x, g):
    s, d = x.shape

    def kern(x_ref, g_ref, u_ref):
        v = x_ref[...]
        u_ref[...] = (v * _rstd(v) * g_ref[...]).astype(BF16)

    return _row_call(kern, "norm_in", [(x, True), (g, False)], [((s, d), BF16, True)], s, d)[0]


def mid_fwd(x, mix, g_post, g_fpre):
    s, d = x.shape

    def kern(x_ref, mix_ref, gp_ref, gf_ref, h1_ref, u2_ref):
        mixv = mix_ref[...]
        h1 = x_ref[...] + mixv * _rstd(mixv) * gp_ref[...]
        h1_ref[...] = h1
        u2_ref[...] = (h1 * _rstd(h1) * gf_ref[...]).astype(BF16)

    return _row_call(kern, "mid_fwd", [(x, True), (mix, True), (g_post, False), (g_fpre, False)],
                     [((s, d), F32, True), ((s, d), BF16, True)], s, d)


def loss_head(h1, ff, g_fpost, target):
    s, d = h1.shape

    def kern(h1_ref, ff_ref, g_ref, t_ref, dy_ref, dff_ref, dg_ref, loss_ref):
        @pl.when(pl.program_id(0) == 0)
        def _():
            dg_ref[...] = jnp.zeros_like(dg_ref)
            loss_ref[...] = jnp.zeros_like(loss_ref)

        ffv = ff_ref[...]
        g = g_ref[...]
        y = h1_ref[...] + ffv * _rstd(ffv) * g
        diff = y - t_ref[...]
        row_loss = jnp.mean(diff * diff, axis=-1, keepdims=True)
        loss_ref[...] += 0.5 * jnp.sum(row_loss, axis=0, keepdims=True)
        dy = diff / d
        dy_ref[...] = dy
        dff, dg = _norm_bwd(ffv, g, dy)
        dff_ref[...] = dff.astype(BF16)
        dg_ref[...] += dg

    return _row_call(kern, "loss_head",
                     [(h1, True), (ff, True), (g_fpost, False), (target, True)],
                     [((s, d), F32, True), ((s, d), BF16, True), ((1, d), F32, False), ((1, 1), F32, False)], s, d)


def mid_bwd(dy, du2, h1, mix, g_fpre, g_post):
    s, d = dy.shape

    def kern(dy_ref, du2_ref, h1_ref, mix_ref, gf_ref, gp_ref, dh1_ref, dmix_ref, dgf_ref, dgp_ref):
        @pl.when(pl.program_id(0) == 0)
        def _():
            dgf_ref[...] = jnp.zeros_like(dgf_ref)
            dgp_ref[...] = jnp.zeros_like(dgp_ref)

        dh, dgf = _norm_bwd(h1_ref[...], gf_ref[...], du2_ref[...])
        dh1 = dy_ref[...] + dh
        dh1_ref[...] = dh1
        dmix, dgp = _norm_bwd(mix_ref[...], gp_ref[...], dh1)
        dmix_ref[...] = dmix.astype(BF16)
        dgf_ref[...] += dgf
        dgp_ref[...] += dgp

    return _row_call(kern, "mid_bwd",
                     [(dy, True), (du2, True), (h1, True), (mix, True), (g_fpre, False), (g_post, False)],
                     [((s, d), F32, True), ((s, d), BF16, True), ((1, d), F32, False), ((1, d), F32, False)], s, d)


def in_bwd(dh1, du, x, g_pre):
    s, d = x.shape

    def kern(dh1_ref, du_ref, x_ref, g_ref, dx_ref, dg_ref):
        @pl.when(pl.program_id(0) == 0)
        def _():
            dg_ref[...] = jnp.zeros_like(dg_ref)

        dxn, dg = _norm_bwd(x_ref[...], g_ref[...], du_ref[...])
        dx_ref[...] = dh1_ref[...] + dxn
        dg_ref[...] += dg

    return _row_call(kern, "in_bwd", [(dh1, True), (du, True), (x, True), (g_pre, False)],
                     [((s, d), F32, True), ((1, d), F32, False)], s, d)


def _sigmoid(v):
    return 1.0 / (1.0 + jnp.exp(-v))


def gate_fwd(bsb, bfx, gf):
    s, d = bsb.shape
    tr, tc = _tile(s, 256, 16), _tile(d, 512, LANES)
    nc = d // tc

    def kern(bsb_ref, bfx_ref, gs_ref, gx_ref, o_ref):
        o_ref[...] = (_sigmoid(gs_ref[...]) * bsb_ref[...] + _sigmoid(gx_ref[...]) * bfx_ref[...]).astype(BF16)

    blk = pl.BlockSpec((tr, tc), lambda i, j: (i, j))
    return pl.pallas_call(
        kern, name="gate_fwd", grid=(s // tr, nc),
        in_specs=[blk, blk, blk, pl.BlockSpec((tr, tc), lambda i, j: (i, j + nc))],
        out_specs=blk, out_shape=jax.ShapeDtypeStruct((s, d), BF16),
        compiler_params=_params("parallel", "parallel"),
    )(bsb, bfx, gf, gf)


def gate_bwd(dmerged, bsb, bfx, gf):
    s, d = bsb.shape
    tr, tc = _tile(s, 256, 16), _tile(d, 512, LANES)
    nc = d // tc

    def kern(dm_ref, bsb_ref, bfx_ref, gs_ref, gx_ref, dbs_ref, dbx_ref, dgs_ref, dgx_ref):
        dm = dm_ref[...]
        ss = _sigmoid(gs_ref[...])
        sx = _sigmoid(gx_ref[...])
        dbs_ref[...] = (dm * ss).astype(BF16)
        dbx_ref[...] = (dm * sx).astype(BF16)
        dgs_ref[...] = (dm * bsb_ref[...] * ss * (1.0 - ss)).astype(BF16)
        dgx_ref[...] = (dm * bfx_ref[...] * sx * (1.0 - sx)).astype(BF16)

    blk = pl.BlockSpec((tr, tc), lambda i, j: (i, j))
    out = jax.ShapeDtypeStruct((s, d), BF16)
    return pl.pallas_call(
        kern, name="gate_bwd", grid=(s // tr, nc),
        in_specs=[blk, blk, blk, blk, pl.BlockSpec((tr, tc), lambda i, j: (i, j + nc))],
        out_specs=[blk, blk, blk, blk], out_shape=[out, out, out, out],
        compiler_params=_params("parallel", "parallel"),
    )(dmerged, bsb, bfx, gf, gf)


FFN_ROWS = 1024


def ffn_up_fused(u2, w_gu, cw):
    s, d = u2.shape
    nc = w_gu.shape[0]
    tm, tk = _tile(s, FFN_ROWS, LANES), _tile(d, 512, LANES)
    nk = d // tk

    def kern(a_ref, b_ref, gu_ref, act_ref):
        kk = pl.program_id(2)

        @pl.when(kk == 0)
        def _():
            gu_ref[...] = jnp.zeros_like(gu_ref)

        gu_ref[...] += jnp.dot(a_ref[...], b_ref[...], preferred_element_type=F32)

        @pl.when(kk == nk - 1)
        def _():
            g = gu_ref[:, :cw]
            act_ref[...] = (g * _sigmoid(g) * gu_ref[:, cw:]).astype(BF16)

    return pl.pallas_call(
        kern, name="ffn_gate_up", grid=(s // tm, nc, nk),
        in_specs=[pl.BlockSpec((tm, tk), lambda i, j, kk: (i, kk)),
                  pl.BlockSpec((None, tk, 2 * cw), lambda i, j, kk: (j, kk, 0))],
        out_specs=[pl.BlockSpec((tm, 2 * cw), lambda i, j, kk: (i, j)), pl.BlockSpec((tm, cw), lambda i, j, kk: (i, j))],
        out_shape=[jax.ShapeDtypeStruct((s, nc * 2 * cw), F32), jax.ShapeDtypeStruct((s, nc * cw), BF16)],
        compiler_params=_params("parallel", "parallel", "arbitrary"),
    )(u2, w_gu)


def ffn_down_bwd_fused(d_ff, w_dn, gu, cw):
    s, d = d_ff.shape
    nc = gu.shape[1] // (2 * cw)
    tm, tk = _tile(s, FFN_ROWS, LANES), _tile(d, 512, LANES)
    nk = d // tk

    def kern(a_ref, b_ref, gu_ref, o_ref, acc_ref):
        kk = pl.program_id(2)

        @pl.when(kk == 0)
        def _():
            acc_ref[...] = jnp.zeros_like(acc_ref)

        acc_ref[...] += lax.dot_general(a_ref[...], b_ref[...], NT, preferred_element_type=F32)

        @pl.when(kk == nk - 1)
        def _():
            da = acc_ref[...]
            g = gu_ref[:, :cw]
            sg = _sigmoid(g)
            o_ref[:, :cw] = (da * gu_ref[:, cw:] * (sg * (1.0 + g * (1.0 - sg)))).astype(BF16)
            o_ref[:, cw:] = (da * (g * sg)).astype(BF16)

    return pl.pallas_call(
        kern, name="d_act_swiglu", grid=(s // tm, nc, nk),
        in_specs=[pl.BlockSpec((tm, tk), lambda i, j, kk: (i, kk)),
                  pl.BlockSpec((cw, tk), lambda i, j, kk: (j, kk)),
                  pl.BlockSpec((tm, 2 * cw), lambda i, j, kk: (i, j))],
        out_specs=pl.BlockSpec((tm, 2 * cw), lambda i, j, kk: (i, j)),
        out_shape=jax.ShapeDtypeStruct(gu.shape, BF16),
        scratch_shapes=[pltpu.VMEM((tm, cw), F32)],
        compiler_params=_params("parallel", "parallel", "arbitrary"),
    )(d_ff, w_dn, gu)


def _split3(v):
    hi = v.astype(BF16)
    r = v - hi.astype(F32)
    mid = r.astype(BF16)
    lo = (r - mid.astype(F32)).astype(BF16)
    return hi, mid, lo


def _dot3_right(v, ones):
    hi, mid, lo = _split3(v)
    d = lambda p: jnp.dot(p, ones, preferred_element_type=F32)
    return (d(lo) + d(mid)) + d(hi)


def _dot3_left(ones, v):
    hi, mid, lo = _split3(v)
    d = lambda p: jnp.dot(ones, p, preferred_element_type=F32)
    return (d(lo) + d(mid)) + d(hi)


def _split2(v):
    hi = v.astype(BF16)
    return hi, (v - hi.astype(F32)).astype(BF16)


def _dot2_right(v, ones):
    hi, lo = _split2(v)
    return jnp.dot(lo, ones, preferred_element_type=F32) + jnp.dot(hi, ones, preferred_element_type=F32)


def _log1p_exp_neg_abs(v):
    return jnp.log(1.0 + jnp.exp(-jnp.abs(v)))


def _mask01(cond):
    return jnp.where(cond, 1.0, 0.0).astype(BF16)


def _iota2(t):
    return (lax.broadcasted_iota(jnp.int32, (t, t), 0), lax.broadcasted_iota(jnp.int32, (t, t), 1))


def cum_fwd(gf, b_pad, f_col0):
    s = gf.shape[0]
    t = _tile(s, ATT_TILE, LANES)
    fb = f_col0 // LANES

    def kern(f_ref, b_ref, cum_ref, carry_ref):
        @pl.when(pl.program_id(0) == 0)
        def _():
            carry_ref[...] = jnp.zeros_like(carry_ref)

        v = f_ref[...] + b_ref[...]
        lf = jnp.minimum(v, 0.0) - _log1p_exp_neg_abs(v)
        row, col = _iota2(t)
        cum = _dot3_left(_mask01(col <= row), lf) + carry_ref[...]
        cum_ref[...] = cum
        carry_ref[...] = cum[t - 1:t, :]

    return pl.pallas_call(
        kern, name="cum_fwd", grid=(s // t,),
        in_specs=[pl.BlockSpec((t, LANES), lambda i: (i, fb)), pl.BlockSpec((1, LANES), lambda i: (0, 0))],
        out_specs=pl.BlockSpec((t, LANES), lambda i: (i, 0)),
        out_shape=jax.ShapeDtypeStruct((s, LANES), F32),
        scratch_shapes=[pltpu.VMEM((1, LANES), F32)],
        compiler_params=_params("arbitrary"),
    )(gf, b_pad)


def cum_bwd(dcum, gf, b_pad, f_col0, n_heads):
    s = gf.shape[0]
    t = _tile(s, ATT_TILE, LANES)
    nb = s // t
    fb = f_col0 // LANES

    def kern(dc_ref, f_ref, b_ref, df_ref, db_ref, carry_ref):
        @pl.when(pl.program_id(0) == 0)
        def _():
            carry_ref[...] = jnp.zeros_like(carry_ref)
            db_ref[...] = jnp.zeros_like(db_ref)

        row, col = _iota2(t)
        dlf = _dot3_left(_mask01(col >= row), dc_ref[...]) + carry_ref[...]
        carry_ref[...] = dlf[0:1, :]
        v = f_ref[...] + b_ref[...]
        sig_neg = jnp.exp(-jnp.maximum(v, 0.0) - _log1p_exp_neg_abs(v))
        lane = lax.broadcasted_iota(jnp.int32, (t, LANES), 1)
        df = jnp.where(lane < n_heads, dlf * sig_neg, 0.0)
        df_ref[...] = df.astype(BF16)
        db_ref[...] += jnp.sum(df, axis=0, keepdims=True)

    return pl.pallas_call(
        kern, name="cum_bwd", grid=(nb,),
        in_specs=[pl.BlockSpec((t, LANES), lambda i: (nb - 1 - i, 0)),
                  pl.BlockSpec((t, LANES), lambda i: (nb - 1 - i, fb)),
                  pl.BlockSpec((1, LANES), lambda i: (0, 0))],
        out_specs=[pl.BlockSpec((t, LANES), lambda i: (nb - 1 - i, 0)), pl.BlockSpec((1, LANES), lambda i: (0, 0))],
        out_shape=[jax.ShapeDtypeStruct((s, LANES), BF16), jax.ShapeDtypeStruct((1, LANES), F32)],
        scratch_shapes=[pltpu.VMEM((1, LANES), F32)],
        compiler_params=_params("arbitrary"),
    )(dcum, gf, b_pad)


def _qkv_specs(s, t, n_heads, base):
    return [pl.BlockSpec((t, HEAD_DIM), lambda h, i: (i, base + h)),
            pl.BlockSpec((s, HEAD_DIM), lambda h, i: (0, base + n_heads + h)),
            pl.BlockSpec((s, HEAD_DIM), lambda h, i: (0, base + 2 * n_heads + h))]


def _strips(t):
    sr = _tile(t, ATT_STRIP, 8)
    return sr, t // sr, [slice(si * sr, (si + 1) * sr) for si in range(t // sr)]


def _key_minus_row(sr, t):
    return lax.broadcasted_iota(jnp.int32, (sr, t), 1) - lax.broadcasted_iota(jnp.int32, (sr, t), 0)


def _keep(valid, v):
    return v if valid is None else jnp.where(valid, v, 0.0)


def _sb_scores(q, k, diff, lim):
    z = lax.dot_general(q, k, NT, preferred_element_type=F32) * (HEAD_DIM ** -0.5)
    valid = None if lim is None else diff < lim
    l1p = _log1p_exp_neg_abs(z)
    return z, valid, l1p, _keep(valid, -jnp.maximum(z, 0.0) - l1p)


def sb_fwd(qkv, n_heads, base):
    s = qkv.shape[0]
    t = _tile(s, ATT_TILE, LANES)
    sr, ns, strips = _strips(t)

    def kern(q_ref, k_ref, v_ref, o_ref, run_ref):
        i = pl.program_id(1)
        row, col = _iota2(t)
        after = _mask01(row > col)
        diff = _key_minus_row(sr, t)
        qs = [q_ref[sl, :] for sl in strips]

        def tile(j, carry, diagonal):
            runs, accs = carry
            off = pl.multiple_of(j * t, t)
            k = k_ref[pl.ds(off, t), :]
            v = v_ref[pl.ds(off, t), :]
            new_runs, new_accs = [], []
            for si, sl in enumerate(strips):
                run_ref[0, j, sl, :] = jnp.broadcast_to(runs[si], (sr, LANES))
                z, valid, l1p, log_keep = _sb_scores(qs[si], k, diff, si * sr if diagonal else None)
                between = _dot2_right(log_keep, after) + runs[si]
                w = _keep(valid, jnp.exp(jnp.minimum(z, 0.0) - l1p + between))
                new_accs.append(accs[si] + jnp.dot(w.astype(BF16), v, preferred_element_type=F32))
                new_runs.append(runs[si] + jnp.sum(log_keep, axis=1, keepdims=True))
            return tuple(new_runs), tuple(new_accs)

        init = (tuple(jnp.zeros((sr, 1), F32) for _ in strips), tuple(jnp.zeros((sr, HEAD_DIM), F32) for _ in strips))
        _, accs = lax.fori_loop(0, i, lambda jj, c: tile(i - 1 - jj, c, False), tile(i, init, True))
        for sl, acc in zip(strips, accs):
            o_ref[sl, :] = acc.astype(o_ref.dtype)

    nq = s // t
    return pl.pallas_call(
        kern, name="sb_fwd", grid=(n_heads, nq),
        in_specs=_qkv_specs(s, t, n_heads, base),
        out_specs=[pl.BlockSpec((t, HEAD_DIM), lambda h, i: (i, h)),
                   pl.BlockSpec((1, nq, t, LANES), lambda h, i: (h, 0, i, 0))],
        out_shape=[jax.ShapeDtypeStruct((s, n_heads * HEAD_DIM), BF16),
                   jax.ShapeDtypeStruct((n_heads, nq, s, LANES), F32)],
        compiler_params=_params("parallel", "arbitrary"),
    )(qkv, qkv, qkv)


def sb_bwd(qkv, d_o, runs, n_heads, base):
    s = qkv.shape[0]
    t = _tile(s, ATT_TILE, LANES)
    nq = s // t
    sr, ns, strips = _strips(t)
    scale = HEAD_DIM ** -0.5

    def kern(q_ref, k_ref, v_ref, do_ref, run_ref, dq_ref, dk_ref, dv_ref, dk_acc, dv_acc):
        i = pl.program_id(1)

        @pl.when(i == 0)
        def _():
            dk_acc[...] = jnp.zeros_like(dk_acc)
            dv_acc[...] = jnp.zeros_like(dv_acc)

        row, col = _iota2(t)
        after = _mask01(row > col)
        before = _mask01(row < col)
        diff = _key_minus_row(sr, t)
        qs = [q_ref[sl, :] for sl in strips]
        dos = [do_ref[sl, :] for sl in strips]

        def sweep2(j, carry, diagonal):
            run_es, dqs = carry
            off = pl.multiple_of(j * t, t)
            k = k_ref[pl.ds(off, t), :]
            v = v_ref[pl.ds(off, t), :]
            new_es, new_dqs = [], []
            dk_t = jnp.zeros((t, HEAD_DIM), F32)
            dv_t = jnp.zeros((t, HEAD_DIM), F32)
            for si, sl in enumerate(strips):
                z, valid, l1p, log_keep = _sb_scores(qs[si], k, diff, si * sr if diagonal else None)
                between = _dot2_right(log_keep, after) + run_ref[0, j, sl, 0:1]
                w = _keep(valid, jnp.exp(jnp.minimum(z, 0.0) - l1p + between))
                dw = lax.dot_general(dos[si], v, NT, preferred_element_type=F32)
                e = dw * w
                e_before = _dot2_right(e, before) + run_es[si]
                keep = jnp.exp(log_keep)
                dz = _keep(valid, e * keep - e_before * (1.0 - keep)) * scale
                dzb = dz.astype(BF16)
                new_dqs.append(dqs[si] + jnp.dot(dzb, k, preferred_element_type=F32))
                dk_t = dk_t + lax.dot_general(dzb, qs[si], TN, preferred_element_type=F32)
                dv_t = dv_t + lax.dot_general(w.astype(BF16), dos[si], TN, preferred_element_type=F32)
                new_es.append(run_es[si] + jnp.sum(e, axis=1, keepdims=True))
            dk_acc[pl.ds(off, t), :] += dk_t
            dv_acc[pl.ds(off, t), :] += dv_t
            return tuple(new_es), tuple(new_dqs)

        init = (tuple(jnp.zeros((sr, 1), F32) for _ in strips), tuple(jnp.zeros((sr, HEAD_DIM), F32) for _ in strips))
        _, dqs = sweep2(i, lax.fori_loop(0, i, lambda j, c: sweep2(j, c, False), init), True)
        for sl, dq in zip(strips, dqs):
            dq_ref[sl, :] = dq.astype(BF16)

        @pl.when(i == nq - 1)
        def _():
            dk_ref[...] = dk_acc[...].astype(BF16)
            dv_ref[...] = dv_acc[...].astype(BF16)

    out = jax.ShapeDtypeStruct((s, n_heads * HEAD_DIM), BF16)
    head_blk = pl.BlockSpec((s, HEAD_DIM), lambda h, i: (0, h))
    tile_blk = pl.BlockSpec((t, HEAD_DIM), lambda h, i: (i, h))
    return pl.pallas_call(
        kern, name="sb_bwd", grid=(n_heads, nq),
        in_specs=_qkv_specs(s, t, n_heads, base) + [tile_blk, pl.BlockSpec((1, nq, t, LANES), lambda h, i: (h, 0, i, 0))],
        out_specs=[tile_blk, head_blk, head_blk],
        out_shape=[out, out, out],
        scratch_shapes=[pltpu.VMEM((s, HEAD_DIM), F32), pltpu.VMEM((s, HEAD_DIM), F32)],
        compiler_params=_params("parallel", "arbitrary"),
    )(qkv, qkv, qkv, d_o, runs)


def _fox_scores(q, k, cq, ck, diff, lim):
    sc = lax.dot_general(q, k, NT, preferred_element_type=F32) * (HEAD_DIM ** -0.5)
    sc = sc + cq - ck
    if lim is None:
        return sc, None
    valid = diff < lim
    return jnp.where(valid, sc, NEG_BIG), valid


def fox_fwd(qkv, cum_col, cum_row, n_heads, base):
    s = qkv.shape[0]
    t = _tile(s, ATT_TILE, LANES)
    sr, ns, strips = _strips(t)

    def kern(q_ref, k_ref, v_ref, cq_ref, ck_ref, o_ref, lse_ref):
        i = pl.program_id(1)
        diff = _key_minus_row(sr, t)
        qs = [q_ref[sl, :] for sl in strips]
        cqs = [cq_ref[0, sl, :] for sl in strips]

        def tile(j, carry, diagonal):
            off = pl.multiple_of(j * t, t)
            k = k_ref[pl.ds(off, t), :]
            v = v_ref[pl.ds(off, t), :]
            ck = ck_ref[0, :, pl.ds(off, t)]
            out = []
            for si in range(ns):
                m, l, acc = carry[si]
                sc, _ = _fox_scores(qs[si], k, cqs[si], ck, diff, si * sr + 1 if diagonal else None)
                m_new = jnp.maximum(m, jnp.max(sc, axis=1, keepdims=True))
                p = jnp.exp(sc - m_new)
                alpha = jnp.exp(m - m_new)
                l = alpha * l + jnp.sum(p, axis=1, keepdims=True)
                acc = alpha * acc + jnp.dot(p.astype(BF16), v, preferred_element_type=F32)
                out.append((m_new, l, acc))
            return tuple(out)

        init = tuple((jnp.full((sr, 1), NEG_BIG, F32), jnp.zeros((sr, 1), F32), jnp.zeros((sr, HEAD_DIM), F32))
                     for _ in strips)
        res = tile(i, lax.fori_loop(0, i, lambda j, c: tile(j, c, False), init), True)
        for sl, (m, l, acc) in zip(strips, res):
            o_ref[sl, :] = acc / l
            lse_ref[0, sl, :] = m + jnp.log(l)

    col_blk = pl.BlockSpec((1, t, 1), lambda h, i: (h, i, 0))
    return pl.pallas_call(
        kern, name="fox_fwd", grid=(n_heads, s // t),
        in_specs=_qkv_specs(s, t, n_heads, base) + [col_blk, pl.BlockSpec((1, 1, s), lambda h, i: (h, 0, 0))],
        out_specs=[pl.BlockSpec((t, HEAD_DIM), lambda h, i: (i, h)), col_blk],
        out_shape=[jax.ShapeDtypeStruct((s, n_heads * HEAD_DIM), F32), jax.ShapeDtypeStruct((n_heads, s, 1), F32)],
        compiler_params=_params("parallel", "arbitrary"),
    )(qkv, qkv, qkv, cum_col, cum_row)


def fox_bwd(qkv, cum_col, cum_row, o, d_o, lse, n_heads, base):
    s = qkv.shape[0]
    t = _tile(s, ATT_TILE, LANES)
    nq = s // t
    sr, ns, strips = _strips(t)
    scale = HEAD_DIM ** -0.5

    def kern(q_ref, k_ref, v_ref, cq_ref, ck_ref, o_ref, do_ref, lse_ref,
             dq_ref, dk_ref, dv_ref, dcq_ref, dck_ref, dk_acc, dv_acc, dck_acc):
        i = pl.program_id(1)

        @pl.when(i == 0)
        def _():
            dk_acc[...] = jnp.zeros_like(dk_acc)
            dv_acc[...] = jnp.zeros_like(dv_acc)
            dck_acc[...] = jnp.zeros_like(dck_acc)

        diff = _key_minus_row(sr, t)
        qs = [q_ref[sl, :] for sl in strips]
        dos = [do_ref[sl, :] for sl in strips]
        cqs = [cq_ref[0, sl, :] for sl in strips]
        lses = [lse_ref[0, sl, :] for sl in strips]
        deltas = [jnp.sum(dos[si].astype(F32) * o_ref[sl, :], axis=1, keepdims=True) for si, sl in enumerate(strips)]

        def tile(j, carry, diagonal):
            off = pl.multiple_of(j * t, t)
            k = k_ref[pl.ds(off, t), :]
            v = v_ref[pl.ds(off, t), :]
            ck = ck_ref[0, :, pl.ds(off, t)]
            out = []
            dk_t = jnp.zeros((t, HEAD_DIM), F32)
            dv_t = jnp.zeros((t, HEAD_DIM), F32)
            dck_t = jnp.zeros((1, t), F32)
            for si in range(ns):
                dq, dcq = carry[si]
                sc, valid = _fox_scores(qs[si], k, cqs[si], ck, diff, si * sr + 1 if diagonal else None)
                p = _keep(valid, jnp.exp(sc - lses[si]))
                dp = lax.dot_general(dos[si], v, NT, preferred_element_type=F32)
                ds = p * (dp - deltas[si])
                dsb = (ds * scale).astype(BF16)
                dq = dq + jnp.dot(dsb, k, preferred_element_type=F32)
                dk_t = dk_t + lax.dot_general(dsb, qs[si], TN, preferred_element_type=F32)
                dv_t = dv_t + lax.dot_general(p.astype(BF16), dos[si], TN, preferred_element_type=F32)
                dck_t = dck_t + jnp.sum(ds, axis=0, keepdims=True)
                out.append((dq, dcq + jnp.sum(ds, axis=1, keepdims=True)))
            dk_acc[pl.ds(off, t), :] += dk_t
            dv_acc[pl.ds(off, t), :] += dv_t
            dck_acc[:, pl.ds(off, t)] -= dck_t
            return tuple(out)

        init = tuple((jnp.zeros((sr, HEAD_DIM), F32), jnp.zeros((sr, 1), F32)) for _ in strips)
        res = tile(i, lax.fori_loop(0, i, lambda j, c: tile(j, c, False), init), True)
        for sl, (dq, dcq) in zip(strips, res):
            dq_ref[sl, :] = dq.astype(BF16)
            dcq_ref[0, sl, :] = dcq

        @pl.when(i == nq - 1)
        def _():
            dk_ref[...] = dk_acc[...].astype(BF16)
            dv_ref[...] = dv_acc[...].astype(BF16)
            dck_ref[0] = dck_acc[...]

    out = jax.ShapeDtypeStruct((s, n_heads * HEAD_DIM), BF16)
    head_blk = pl.BlockSpec((s, HEAD_DIM), lambda h, i: (0, h))
    tile_blk = pl.BlockSpec((t, HEAD_DIM), lambda h, i: (i, h))
    col_blk = pl.BlockSpec((1, t, 1), lambda h, i: (h, i, 0))
    row_blk = pl.BlockSpec((1, 1, s), lambda h, i: (h, 0, 0))
    return pl.pallas_call(
        kern, name="fox_bwd", grid=(n_heads, nq),
        in_specs=_qkv_specs(s, t, n_heads, base) + [col_blk, row_blk, tile_blk, tile_blk, col_blk],
        out_specs=[tile_blk, head_blk, head_blk, col_blk, row_blk],
        out_shape=[out, out, out, jax.ShapeDtypeStruct((n_heads, s, 1), F32),
                   jax.ShapeDtypeStruct((n_heads, 1, s), F32)],
        scratch_shapes=[pltpu.VMEM((s, HEAD_DIM), F32), pltpu.VMEM((s, HEAD_DIM), F32), pltpu.VMEM((1, s), F32)],
        compiler_params=_params("parallel", "arbitrary"),
    )(qkv, qkv, qkv, cum_col, cum_row, o, d_o, lse)


def _place():
    x, y, c = lax.axis_index("x"), lax.axis_index("y"), lax.axis_index("c")
    other_chips = [(1 - x, y), (x, 1 - y), (1 - x, 1 - y)]
    return x, y, c, other_chips


ANY = pl.BlockSpec(memory_space=pl.ANY)


def _remote(src, dst, send_sem, recv_sem, dev):
    return pltpu.make_async_remote_copy(src_ref=src, dst_ref=dst, send_sem=send_sem, recv_sem=recv_sem,
                                        device_id=dev, device_id_type=MESH)


def place_transposed(name, w_t, chip):
    c, _, r = w_t.shape
    tc = LANES

    def kern(chip_ref, w_ref, o_ref):
        o_ref[...] = w_ref[:, 0, :].T.astype(BF16)

    return pl.pallas_call(
        kern, name=name,
        grid_spec=pltpu.PrefetchScalarGridSpec(
            num_scalar_prefetch=1, grid=(pl.cdiv(c, tc),),
            in_specs=[pl.BlockSpec((tc, 1, r), lambda j, chip_ref: (j, 0, 0))],
            out_specs=pl.BlockSpec((None, r, tc), lambda j, chip_ref: (chip_ref[0], 0, j))),
        out_shape=jax.ShapeDtypeStruct((N_CHIPS, r, c), BF16),
        compiler_params=_params("parallel"),
    )(chip, w_t)


def adam_update_transposed(name, w_t, m_t, v_t, g_buf):
    c, _, r = w_t.shape
    tc = LANES

    def kern(w_ref, m_ref, v_ref, g_ref, go_ref, dl_ref, nm_ref, nv_ref):
        g = g_ref[...].T
        delta, nm, nv = _adam(w_ref[:, 0, :], g, m_ref[:, 0, :], v_ref[:, 0, :])
        go_ref[:, 0, :] = g
        dl_ref[:, 0, :] = delta
        nm_ref[:, 0, :] = nm
        nv_ref[:, 0, :] = nv

    blk = pl.BlockSpec((tc, 1, r), lambda j: (j, 0, 0))
    out = jax.ShapeDtypeStruct((c, 1, r), F32)
    return pl.pallas_call(
        kern, name=name, grid=(pl.cdiv(c, tc),),
        in_specs=[blk, blk, blk, pl.BlockSpec((r, tc), lambda j: (0, j))],
        out_specs=[blk] * 4, out_shape=[out] * 4, compiler_params=_params("parallel"),
    )(w_t, m_t, v_t, g_buf)


def cast_place(name, ws, chip):
    r = ws[0].shape[1]
    cs = [w.shape[2] for w in ws]
    tr = _tile(r, 256, 16)

    def kern(chip_ref, *refs):
        o_ref = refs[-1]
        off = 0
        for w_ref, c in zip(refs[:-1], cs):
            o_ref[:, off:off + c] = w_ref[...].astype(BF16)
            off += c

    return pl.pallas_call(
        kern, name=name,
        grid_spec=pltpu.PrefetchScalarGridSpec(
            num_scalar_prefetch=1, grid=(r // tr,),
            in_specs=[pl.BlockSpec((None, tr, c), lambda i, chip_ref: (0, i, 0)) for c in cs],
            out_specs=pl.BlockSpec((None, tr, sum(cs)), lambda i, chip_ref: (chip_ref[0], i, 0))),
        out_shape=jax.ShapeDtypeStruct((N_CHIPS, r, sum(cs)), BF16),
        compiler_params=_params("parallel"),
    )(chip, *ws)


HBM = pl.BlockSpec(memory_space=pltpu.HBM)
SEM = pl.BlockSpec(memory_space=pltpu.SEMAPHORE)
SPLIT = pltpu.CompilerParams(has_side_effects=pltpu.SideEffectType.DATAFLOW_SIDE_EFFECTING)


def _in_hbm(a):
    return pltpu.with_memory_space_constraint(a, pltpu.HBM)


def _slab_rows(ref, k, core):
    half = ref.shape[1] // 2
    return ref.at[k, pl.ds(pl.multiple_of(core * half, 16), half)]


def gather_start(name, bufs):
    n = len(bufs)

    def body(*refs):
        ins, send, recv, token = refs[:n], refs[n], refs[n + 1], refs[-1]
        x, y, c, chips = _place()
        me = 2 * x + y
        for a in range(n):
            for j in range(3):
                rows = _slab_rows(ins[a], me, c)
                _remote(rows, rows, send.at[3 * a + j], recv.at[3 * a + j], (chips[j][0], chips[j][1], c)).start()
        token[...] = jnp.zeros_like(token)

    sem = pltpu.SemaphoreType.DMA((3 * n,))
    res = pl.pallas_call(
        body, name=name, in_specs=[HBM] * n, out_specs=[SEM, SEM] + [HBM] * n + [pl.BlockSpec(memory_space=pltpu.VMEM)],
        out_shape=[sem, sem] + [pltpu.HBM(b.shape, b.dtype) for b in bufs] + [jax.ShapeDtypeStruct((8, LANES), F32)],
        input_output_aliases={a: 2 + a for a in range(n)}, compiler_params=SPLIT,
    )(*[_in_hbm(b) for b in bufs])
    return res[0], res[1], res[2:2 + n], res[-1]


def gather_wait(name, bufs, send_sems, recv_sems, after):
    n = len(bufs)

    def body(*refs):
        ins, send, recv = refs[:n], refs[n], refs[n + 1]
        x, y, c, chips = _place()
        me = 2 * x + y
        for a in range(n):
            for j in range(3):
                dev = (chips[j][0], chips[j][1], c)
                mine = _slab_rows(ins[a], me, c)
                _remote(mine, mine, send.at[3 * a + j], recv.at[3 * a + j], dev).wait_send()
                land = _slab_rows(ins[a], 2 * chips[j][0] + chips[j][1], c)
                _remote(land, land, send.at[3 * a + j], recv.at[3 * a + j], dev).wait_recv()

    return pl.pallas_call(
        body, name=name, in_specs=[HBM] * n + [SEM, SEM] + [ANY] * len(after), out_specs=[HBM] * n,
        out_shape=[pltpu.HBM(b.shape, b.dtype) for b in bufs],
        input_output_aliases={a: a for a in range(n)}, compiler_params=SPLIT,
    )(*bufs, send_sems, recv_sems, *after)


def gather_forward(name, bufs):
    n = len(bufs)

    def body(*refs):
        outs = refs[n:2 * n]
        send_sems, recv_sems = refs[2 * n:]
        x, y, c, chips = _place()
        sibling = (x, y, 1 - c)

        def d2d(a, j, core):
            rows = _slab_rows(outs[a], 2 * chips[j][0] + chips[j][1], core)
            return _remote(rows, rows, send_sems.at[3 * a + j], recv_sems.at[3 * a + j], sibling)

        pairs = [(a, j) for a in range(n) for j in range(3)]
        for a, j in pairs:
            d2d(a, j, c).start()
        for a, j in pairs:
            d2d(a, j, 1 - c).wait_recv()
        for a, j in pairs:
            d2d(a, j, c).wait_send()

    return pl.pallas_call(
        body, name=name, in_specs=[ANY] * n, out_specs=[ANY] * n,
        out_shape=[jax.ShapeDtypeStruct(b.shape, b.dtype) for b in bufs],
        input_output_aliases={a: a for a in range(n)},
        scratch_shapes=[pltpu.SemaphoreType.DMA((3 * n,)), pltpu.SemaphoreType.DMA((3 * n,))],
    )(*bufs)


def _forward_plan(refs):
    x, y, c, chips = _place()
    out = []
    for ref in refs:
        for j in range(3):
            k = 2 * chips[j][0] + chips[j][1]
            out.append((_slab_rows(ref, k, c), _slab_rows(ref, k, c), _slab_rows(ref, k, 1 - c)))
    return out


def _join_plan(refs):
    x, y, c, _ = _place()
    out = []
    for ref in refs:
        half = ref.shape[0] // 2
        mine = ref.at[pl.ds(pl.multiple_of(c * half, 8), half)]
        out.append((mine, mine, ref.at[pl.ds(pl.multiple_of((1 - c) * half, 8), half)]))
    return out


def _swap_plan(refs):
    x, y, c, _ = _place()
    n = len(refs) // 2
    out = []
    for a in range(n):
        half = refs[a].shape[1] // 2
        src = refs[a].at[:, pl.ds(pl.multiple_of((1 - c) * half, 16), half), :]
        out.append((src, refs[n + a], refs[n + a]))
    return out


def sibling_start(name, arrays, plan, n_copies, after=()):
    n = len(arrays)

    n_in = n + len(after)

    def body(*refs):
        send, recv, token = refs[n_in], refs[n_in + 1], refs[-1]
        x, y, c, _ = _place()
        for idx, (src, dst, _) in enumerate(plan(refs[:n])):
            _remote(src, dst, send.at[idx], recv.at[idx], (x, y, 1 - c)).start()
        token[...] = jnp.zeros_like(token)

    sem = pltpu.SemaphoreType.DMA((n_copies,))
    res = pl.pallas_call(
        body, name=name, in_specs=[HBM] * n + [ANY] * len(after),
        out_specs=[SEM, SEM] + [HBM] * n + [pl.BlockSpec(memory_space=pltpu.VMEM)],
        out_shape=[sem, sem] + [pltpu.HBM(b.shape, b.dtype) for b in arrays] + [jax.ShapeDtypeStruct((8, LANES), F32)],
        input_output_aliases={a: 2 + a for a in range(n)}, compiler_params=SPLIT,
    )(*[_in_hbm(b) for b in arrays], *after)
    return res[0], res[1], res[2:2 + n], res[-1]


def sibling_wait(name, arrays, send_sems, recv_sems, plan, after):
    n = len(arrays)

    def body(*refs):
        send, recv = refs[n], refs[n + 1]
        x, y, c, _ = _place()
        for idx, (src, dst, filled) in enumerate(plan(refs[:n])):
            _remote(src, dst, send.at[idx], recv.at[idx], (x, y, 1 - c)).wait_send()
            _remote(filled, filled, send.at[idx], recv.at[idx], (x, y, 1 - c)).wait_recv()

    return pl.pallas_call(
        body, name=name, in_specs=[HBM] * n + [SEM, SEM] + [ANY] * len(after), out_specs=[HBM] * n,
        out_shape=[pltpu.HBM(b.shape, b.dtype) for b in arrays],
        input_output_aliases={a: a for a in range(n)}, compiler_params=SPLIT,
    )(*arrays, send_sems, recv_sems, *after)


def swap_halves(name, pieces):
    n = len(pieces)
    halves = [p.shape[1] // 2 for p in pieces]

    def body(*refs):
        ins, outs = refs[:n], refs[n:2 * n]
        send_sems, recv_sems = refs[2 * n:]
        x, y, c, _ = _place()
        cps = [_remote(ins[a].at[:, pl.ds(pl.multiple_of((1 - c) * halves[a], 16), halves[a]), :], outs[a],
                       send_sems.at[a], recv_sems.at[a], (x, y, 1 - c)) for a in range(n)]
        for cp in cps:
            cp.start()
        for cp in cps:
            cp.wait()

    return pl.pallas_call(
        body, name=name, in_specs=[ANY] * n, out_specs=[ANY] * n,
        out_shape=[jax.ShapeDtypeStruct((N_CHIPS, h, p.shape[2]), p.dtype) for p, h in zip(pieces, halves)],
        scratch_shapes=[pltpu.SemaphoreType.DMA((n,)), pltpu.SemaphoreType.DMA((n,))],
    )(*pieces)


def pair_sum(name, pieces, got, core):
    _, r, w = pieces.shape
    half = r // 2
    tr = _tile(half, 256, 16)

    def kern(core_ref, p_ref, g_ref, o_ref):
        o_ref[...] = (p_ref[...].astype(F32) + g_ref[...].astype(F32)).astype(o_ref.dtype)

    return pl.pallas_call(
        kern, name=name,
        grid_spec=pltpu.PrefetchScalarGridSpec(
            num_scalar_prefetch=1, grid=(N_CHIPS, half // tr),
            in_specs=[pl.BlockSpec((None, None, tr, w), lambda k, i, core_ref: (k, core_ref[0], i, 0)),
                      pl.BlockSpec((None, tr, w), lambda k, i, core_ref: (k, i, 0))],
            out_specs=pl.BlockSpec((None, tr, w), lambda k, i, core_ref: (k, i, 0))),
        out_shape=jax.ShapeDtypeStruct((N_CHIPS, half, w), pieces.dtype),
        compiler_params=_params("parallel", "parallel"),
    )(core, pieces.reshape(N_CHIPS, 2, half, w), got)


def _scatter_copies(sums, lands, send, recv):
    x, y, c, chips = _place()
    return [_remote(sums[a].at[2 * chips[j][0] + chips[j][1]], lands[a].at[j], send.at[3 * a + j], recv.at[3 * a + j],
                    (chips[j][0], chips[j][1], c)) for a in range(len(sums)) for j in range(3)]


def scatter_start(name, sums):
    n = len(sums)
    lands = [lax.empty((3,) + t.shape[1:], t.dtype) for t in sums]

    def body(*refs):
        ins, land_in, send, recv, token = refs[:n], refs[n:2 * n], refs[2 * n], refs[2 * n + 1], refs[-1]
        for cp in _scatter_copies(ins, land_in, send, recv):
            cp.start()
        token[...] = jnp.zeros_like(token)

    sem = pltpu.SemaphoreType.DMA((3 * n,))
    res = pl.pallas_call(
        body, name=name, in_specs=[HBM] * (2 * n),
        out_specs=[SEM, SEM] + [HBM] * (2 * n) + [pl.BlockSpec(memory_space=pltpu.VMEM)],
        out_shape=[sem, sem] + [pltpu.HBM(t.shape, t.dtype) for t in sums + lands] + [jax.ShapeDtypeStruct((8, LANES), F32)],
        input_output_aliases={a: 2 + a for a in range(2 * n)}, compiler_params=SPLIT,
    )(*[_in_hbm(t) for t in sums + lands])
    return res[0], res[1], res[2:2 + n], res[2 + n:2 + 2 * n], res[-1]


def scatter_wait(name, sums, lands, send_sems, recv_sems, after):
    n = len(sums)

    def body(*refs):
        ins, land_in, send, recv = refs[:n], refs[n:2 * n], refs[2 * n], refs[2 * n + 1]
        for cp in _scatter_copies(ins, land_in, send, recv):
            cp.wait_send()
            cp.wait_recv()

    res = pl.pallas_call(
        body, name=name, in_specs=[HBM] * (2 * n) + [SEM, SEM, ANY], out_specs=[HBM] * (2 * n),
        out_shape=[pltpu.HBM(t.shape, t.dtype) for t in sums + lands],
        input_output_aliases={a: a for a in range(2 * n)}, compiler_params=SPLIT,
    )(*sums, *lands, send_sems, recv_sems, after)
    return res[:n], res[n:]


def chip_sum(name, sums, got, chip, core):
    _, half, w = sums.shape
    tr = _tile(half, 256, 16)
    nb = half // tr

    def kern(ids_ref, s_ref, g0_ref, g1_ref, g2_ref, o_ref):
        o_ref[...] = ((s_ref[...].astype(F32) + g0_ref[...].astype(F32)) + g1_ref[...].astype(F32)) \
            + g2_ref[...].astype(F32)

    def got_spec(j):
        return pl.BlockSpec((None, tr, w), lambda i, ids_ref: (j, i, 0))

    return pl.pallas_call(
        kern, name=name,
        grid_spec=pltpu.PrefetchScalarGridSpec(
            num_scalar_prefetch=1, grid=(nb,),
            in_specs=[pl.BlockSpec((None, tr, w), lambda i, ids_ref: (ids_ref[0], i, 0)),
                      got_spec(0), got_spec(1), got_spec(2)],
            out_specs=pl.BlockSpec((tr, w), lambda i, ids_ref: (ids_ref[1] * nb + i, 0))),
        out_shape=jax.ShapeDtypeStruct((2 * half, w), F32),
        compiler_params=_params("parallel"),
    )(jnp.concatenate([chip, core]), sums, got, got, got)


def join_halves(name, shards):
    n = len(shards)
    halves = [g.shape[0] // 2 for g in shards]

    def body(*refs):
        outs = refs[n:2 * n]
        send_sems, recv_sems = refs[2 * n:]
        x, y, c, _ = _place()
        cps = []
        for a in range(n):
            rows = outs[a].at[pl.ds(pl.multiple_of(c * halves[a], 8), halves[a])]
            cps.append(_remote(rows, rows, send_sems.at[a], recv_sems.at[a], (x, y, 1 - c)))
        for cp in cps:
            cp.start()
        for cp in cps:
            cp.wait()

    return pl.pallas_call(
        body, name=name, in_specs=[ANY] * n, out_specs=[ANY] * n,
        out_shape=[jax.ShapeDtypeStruct(g.shape, g.dtype) for g in shards],
        input_output_aliases={a: a for a in range(n)},
        scratch_shapes=[pltpu.SemaphoreType.DMA((n,)), pltpu.SemaphoreType.DMA((n,))],
    )(*shards)


def _adam(w, g, m, v):
    m = ADAM_B1 * m + (1.0 - ADAM_B1) * g
    v = ADAM_B2 * v + (1.0 - ADAM_B2) * (g * g)
    m_hat = m / (1.0 - ADAM_B1 ** ADAM_STEP)
    v_hat = v / (1.0 - ADAM_B2 ** ADAM_STEP)
    delta = -ADAM_LR * (m_hat / (jnp.sqrt(v_hat) + ADAM_EPS) + ADAM_WD * w)
    return delta, m, v


def small_allreduce_adam(g_part, w, m, v, after):
    n_dev = 8
    r, d = g_part.shape

    def body(g_ref, w_ref, m_ref, v_ref, after_ref, gs_ref, dl_ref, nm_ref, nv_ref, all_ref, send_sems, recv_sems):
        x, y, c, _ = _place()
        me = 4 * x + 2 * y + c
        all_ref[me] = g_ref[...]
        cps = []
        for rel in range(1, n_dev):
            px = 1 - x if rel & 4 else x
            py = 1 - y if rel & 2 else y
            pc = 1 - c if rel & 1 else c
            cps.append(_remote(g_ref, all_ref.at[me], send_sems.at[rel - 1], recv_sems.at[rel - 1], (px, py, pc)))
        for cp in cps:
            cp.start()
        for cp in cps:
            cp.wait()
        total = all_ref[0]
        for dev in range(1, n_dev):
            total = total + all_ref[dev]
        gs_ref[...] = total
        delta, nm, nv = _adam(w_ref[...], total, m_ref[...], v_ref[...])
        dl_ref[...] = delta
        nm_ref[...] = nm
        nv_ref[...] = nv

    vm = pl.BlockSpec(memory_space=pltpu.VMEM)
    out = jax.ShapeDtypeStruct((r, d), F32)
    return pl.pallas_call(
        body, name="small_allreduce_adam", in_specs=[vm, vm, vm, vm, ANY], out_specs=[vm, vm, vm, vm],
        out_shape=[out, out, out, out],
        scratch_shapes=[pltpu.VMEM((n_dev, r, d), F32), pltpu.SemaphoreType.DMA((n_dev - 1,)),
                        pltpu.SemaphoreType.DMA((n_dev - 1,))],
    )(g_part, w, m, v, after)


def adam_update(name, w, m, v, g_buf, col_blk, after=None):
    w, m, v = w[0], m[0], v[0]
    r, c = w.shape
    tr = _tile(r, 128, 8)
    extra = [] if after is None else [after]

    def kern(w_ref, m_ref, v_ref, g_ref, *rest):
        go_ref, dl_ref, nm_ref, nv_ref = rest[len(extra):]
        g = g_ref[...]
        delta, nm, nv = _adam(w_ref[...], g, m_ref[...], v_ref[...])
        go_ref[...] = g
        dl_ref[...] = delta
        nm_ref[...] = nm
        nv_ref[...] = nv

    blk = pl.BlockSpec((tr, c), lambda i: (i, 0))
    out = jax.ShapeDtypeStruct((r, c), F32)
    res = pl.pallas_call(
        kern, name=name, grid=(r // tr,),
        in_specs=[blk, blk, blk, pl.BlockSpec((tr, c), lambda i: (i, col_blk))] + [ANY] * len(extra),
        out_specs=[blk] * 4, out_shape=[out] * 4, compiler_params=_params("parallel"),
    )(w, m, v, g_buf, *extra)
    return [a[None] for a in res]


def _w_in_segments(cw, n_qkv, n_heads, d):
    out = []

    def add(lo, hi, main):
        while lo < hi:
            k, a = divmod(lo, cw)
            w = min(cw - a, hi - lo)
            out.append((k, a, main, w))
            lo, main = lo + w, main + w

    add(0, n_qkv, 0)
    add(n_qkv + n_heads, N_CHIPS * cw, n_qkv)
    add(n_qkv, n_qkv + n_heads, n_qkv + 2 * d)
    return out


def regroup_w_in(g_in, segments, n_main):
    _, d, cw = g_in.shape
    tr = _tile(d, 128, 16)
    n_real = max(m + w for _, _, m, w in segments)

    def kern(s_ref, o_ref):
        for k, a, m, w in segments:
            o_ref[:, m:m + w] = s_ref[k, :, a:a + w]
        o_ref[:, n_real:] = jnp.zeros((tr, n_main - n_real), o_ref.dtype)

    return pl.pallas_call(
        kern, name="regroup_w_in", grid=(d // tr,),
        in_specs=[pl.BlockSpec((N_CHIPS, tr, cw), lambda i: (0, i, 0))],
        out_specs=pl.BlockSpec((tr, n_main), lambda i: (i, 0)),
        out_shape=jax.ShapeDtypeStruct((d, n_main), g_in.dtype), compiler_params=_params("parallel"),
    )(g_in)


def regroup_dw_in(dw_main, segments, cw):
    d, n_main = dw_main.shape
    tr = _tile(d, 128, 16)

    def kern(s_ref, o_ref):
        for k, a, m, w in segments:
            o_ref[k, :, a:a + w] = s_ref[:, m:m + w]

    return pl.pallas_call(
        kern, name="regroup_dw_in", grid=(d // tr,),
        in_specs=[pl.BlockSpec((tr, n_main), lambda i: (i, 0))],
        out_specs=pl.BlockSpec((N_CHIPS, tr, cw), lambda i: (0, i, 0)),
        out_shape=jax.ShapeDtypeStruct((N_CHIPS, d, cw), dw_main.dtype), compiler_params=_params("parallel"),
    )(dw_main)


def kernel(x, norm_mix_pre, norm_mix_post, w_in, b_forget, w_branch_sb, w_branch_fox, w_out, norm_ffn_pre, norm_ffn_post, w_ffn_gate, w_ffn_up, w_ffn_down, loss_target, m_norm_mix_pre, m_norm_mix_post, m_w_in, m_b_forget, m_w_branch_sb, m_w_branch_fox, m_w_out, m_norm_ffn_pre, m_norm_ffn_post, m_w_ffn_gate, m_w_ffn_up, m_w_ffn_down, v_norm_mix_pre, v_norm_mix_post, v_w_in, v_b_forget, v_w_branch_sb, v_w_branch_fox, v_w_out, v_norm_ffn_pre, v_norm_ffn_post, v_w_ffn_gate, v_w_ffn_up, v_w_ffn_down):
    s, d = x.shape[1], x.shape[2]
    n_heads = b_forget.shape[1]
    d_att = n_heads * HEAD_DIM
    c_in = w_in.shape[2]
    c_br = w_branch_sb.shape[2]
    c_gu = w_ffn_gate.shape[2]
    d_ff = c_gu * N_CHIPS
    d_in = c_in * N_CHIPS
    f_pad = 512
    n_qkv = 6 * d_att
    n_gf = 2 * d + f_pad
    core = lax.axis_index("c").astype(jnp.int32).reshape(1)
    chip = (2 * lax.axis_index("x") + lax.axis_index("y")).astype(jnp.int32).reshape(1)

    as_t = lambda a: jnp.transpose(a, (2, 0, 1))
    w_in_t, m_in_t, v_in_t = as_t(w_in), as_t(m_w_in), as_t(v_w_in)
    in_send, in_recv, in_bufs, in_token = gather_start(
        "gather_start_w_in", [place_transposed("place_w_in", w_in_t, chip)])
    sm_send, sm_recv, sm_bufs, sm_token = gather_start("gather_start_small", [
        cast_place("place_branch", [w_branch_sb, w_branch_fox], chip),
        cast_place("place_out", [w_out + in_token[0, 0]], chip)])
    ag_send, ag_recv, ag_bufs, ag_token = gather_start("gather_start_ffn", [
        cast_place("place_gate_up", [w_ffn_gate, w_ffn_up], chip),
        cast_place("place_down", [w_ffn_down + sm_token[0, 0]], chip)])
    x2 = x[0]
    tgt = loss_target[0]
    b_pad = jnp.pad(b_forget, ((0, 0), (0, LANES - n_heads)))
    u = norm_in(x2, norm_mix_pre)
    g_in, = gather_forward("forward_w_in", gather_wait("gather_wait_w_in", in_bufs, in_send, in_recv,
                                                       [ag_token, u]))
    segments = _w_in_segments(c_in, n_qkv, n_heads, d)
    w_main = regroup_w_in(g_in, segments, n_qkv + n_gf)

    qkv = mm(u, w_main, "nn", BF16, "proj_qkv", b_win=(0, n_qkv))
    gf = mm(u, w_main, "nn", F32, "proj_gates", b_win=(n_qkv, n_gf))
    cum = cum_fwd(gf, b_pad, 2 * d)
    cum_heads = cum[:, :n_heads].T
    cum_col, cum_row = cum_heads[:, :, None], cum_heads[:, None, :]
    o_sb, sb_runs = sb_fwd(qkv, n_heads, 0)
    o_fx, lse = fox_fwd(qkv, cum_col, cum_row, n_heads, 3 * n_heads)
    g_br, g_out = gather_forward("forward_small",
                                 gather_wait("gather_wait_small", sm_bufs, sm_send, sm_recv, [o_sb, o_fx]))
    w_o = g_out.reshape(d, d)
    bsb = mm(o_sb, g_br, "nn", F32, "branch_sb", tn=c_br, chunks=(1, 0))
    bfx = mm(o_fx, g_br, "nn", F32, "branch_fox", tn=c_br, chunks=(1, 1))
    merged = gate_fwd(bsb, bfx, gf)
    ffn_bufs = gather_wait("gather_wait_ffn", ag_bufs, ag_send, ag_recv, [merged])
    fb_send, fb_recv, fb_bufs, fb_token = sibling_start("forward_big_start", ffn_bufs, _forward_plan, 6)
    mix = mm(merged, w_o, "nn", F32, "out_proj", after=fb_token)
    h1, u2 = mid_fwd(x2, mix, norm_mix_post, norm_ffn_pre)
    g_gu, g_dn = sibling_wait("forward_big_wait", fb_bufs, fb_send, fb_recv, _forward_plan, [u2])
    w_dn = g_dn.reshape(d_ff, d)
    gu, act = ffn_up_fused(u2, g_gu, c_gu)
    ff = mm(act, w_dn, "nn", F32, "ffn_down")
    dy, d_ff_out, dg_fpost, loss_part = loss_head(h1, ff, norm_ffn_post, tgt)

    p_dn = mm(act, d_ff_out, "tn", BF16, "dw_ffn_down").reshape(N_CHIPS, d_ff // N_CHIPS, d)
    d_gu = ffn_down_bwd_fused(d_ff_out, w_dn, gu, c_gu)
    p_gu = mm(u2, d_gu, "tn", BF16, "dw_ffn_gate_up", tn=c_gu, chunks=(2, 0),
              out_into=lax.empty((N_CHIPS, d, 2 * c_gu), BF16))
    sw_send, sw_recv, sw_arrs, sw_token = sibling_start(
        "swap_big_start", [p_gu, p_dn, lax.empty((N_CHIPS, d // 2, 2 * c_gu), BF16),
                           lax.empty((N_CHIPS, d_ff // N_CHIPS // 2, d), BF16)], _swap_plan, 2)
    du2 = mm(d_gu, g_gu, "nt", F32, "d_u2", tk=c_gu, chunks=(2, 0), after=sw_token)
    dh1, d_mix, dg_fpre, dg_post = mid_bwd(dy, du2, h1, mix, norm_ffn_pre, norm_mix_post)
    p_out = mm(merged, d_mix, "tn", BF16, "dw_out").reshape(N_CHIPS, d // N_CHIPS, d)
    d_merged = mm(d_mix, w_o, "nt", F32, "d_merged")
    d_bsb, d_bfx, d_gs, d_gx = gate_bwd(d_merged, bsb, bfx, gf)
    p_br = mm(o_sb, d_bsb, "tn", BF16, "dw_branch_sb", tn=c_br, chunks=(1, 0),
              out_into=lax.empty((N_CHIPS, d_att, 2 * c_br), BF16))
    p_br = mm(o_fx, d_bfx, "tn", BF16, "dw_branch_fox", tn=c_br, chunks=(1, 1), out_into=p_br)
    d_osb = mm(d_bsb, g_br, "nt", BF16, "d_o_sb", tk=c_br, chunks=(1, 0))
    d_ofx = mm(d_bfx, g_br, "nt", BF16, "d_o_fox", tk=c_br, chunks=(1, 1))

    def reduce_start(tag, pieces, names):
        from_sibling = swap_halves("swap_halves_" + tag, pieces)
        sums = [pair_sum("pair_sum_" + t, p, q, core) for t, p, q in zip(names, pieces, from_sibling)]
        return scatter_start("scatter_start_" + tag, sums)

    def reduce_end(tag, started, names, after):
        send, recv, sums, lands, _ = started
        sums, lands = scatter_wait("scatter_wait_" + tag, sums, lands, send, recv, after)
        return join_halves("join_halves_" + tag, [chip_sum("chip_sum_" + t, sm, got, chip, core)
                                                  for t, sm, got in zip(names, sums, lands)])

    rest_names = ["branch", "out", "gate_up", "down"]
    p_gu, p_dn, q_gu, q_dn = sibling_wait("swap_big_wait", sw_arrs, sw_send, sw_recv, _swap_plan, [p_br])
    q_br, q_out = swap_halves("swap_halves_small", [p_br, p_out])
    rest_started = scatter_start("scatter_start_rest", [
        pair_sum("pair_sum_" + t, p, q, core)
        for t, p, q in zip(rest_names, [p_br, p_out, p_gu, p_dn], [q_br, q_out, q_gu, q_dn])])
    d_osb = d_osb + rest_started[4][0, 0].astype(BF16)
    dq_s, dk_s, dv_s = sb_bwd(qkv, d_osb, sb_runs, n_heads, 0)
    dq_f, dk_f, dv_f, dcq, dck = fox_bwd(qkv, cum_col, cum_row, o_fx, d_ofx, lse, n_heads, 3 * n_heads)
    d_cum = jnp.pad((dcq[:, :, 0] + dck[:, 0, :]).T, ((0, 0), (0, LANES - n_heads)))
    d_f, db_pad = cum_bwd(d_cum, gf, b_pad, 2 * d, n_heads)
    d_main = jnp.concatenate(
        [dq_s, dk_s, dv_s, dq_f, dk_f, dv_f, d_gs, d_gx, d_f, jnp.zeros((s, f_pad - LANES), BF16)], axis=1)
    p_in = regroup_dw_in(mm(u, d_main, "tn", BF16, "dw_in"), segments, c_in)

    in_started = reduce_start("w_in", [p_in], ["in"])
    rs_send, rs_recv, rs_sums, rs_lands, _ = rest_started
    rs_sums, rs_lands = scatter_wait("scatter_wait_rest", rs_sums, rs_lands, rs_send, rs_recv, in_started[4])
    j_send, j_recv, j_shards, j_token = sibling_start(
        "join_rest_start", [chip_sum("chip_sum_" + t, sm, got, chip, core)
                            for t, sm, got in zip(rest_names, rs_sums, rs_lands)], _join_plan, 4)
    du = mm(d_main, w_main, "nt", F32, "d_u", after=j_token)
    dx, dg_pre = in_bwd(dh1, du, x2, norm_mix_pre + in_started[4][0:1, 0:1])
    gr_br, gr_out, gr_gu, gr_dn = sibling_wait("join_rest_wait", j_shards, j_send, j_recv, _join_plan, [dx])

    upd_bs = adam_update("adam_branch_sb", w_branch_sb, m_w_branch_sb, v_w_branch_sb, gr_br, 0)
    upd_bf = adam_update("adam_branch_fox", w_branch_fox, m_w_branch_fox, v_w_branch_fox, gr_br, 1)
    upd_o = adam_update("adam_out", w_out, m_w_out, v_w_out, gr_out, 0)
    upd_ga = adam_update("adam_gate", w_ffn_gate, m_w_ffn_gate, v_w_ffn_gate, gr_gu, 0)
    upd_up = adam_update("adam_up", w_ffn_up, m_w_ffn_up, v_w_ffn_up, gr_gu, 1)
    upd_dn = adam_update("adam_down", w_ffn_down, m_w_ffn_down, v_w_ffn_down, gr_dn, 0)

    done = sum(u_[1][0, 0:1, 0:1] for u_ in (upd_bs, upd_bf, upd_o, upd_ga, upd_up, upd_dn))
    gr_in, = reduce_end("w_in", in_started, ["in"], done)
    upd_in_t = adam_update_transposed("adam_w_in", w_in_t, m_in_t, v_in_t, gr_in)
    upd_in = [jnp.transpose(a, (1, 2, 0)) for a in upd_in_t]
    grads, deltas, new_ms, new_vs = zip(upd_in, upd_bs, upd_bf, upd_o, upd_ga, upd_up, upd_dn)

    def pack(rows):
        rows = [jnp.pad(r_, ((0, 0), (0, d - r_.shape[1]))) for r_ in rows]
        return jnp.concatenate(rows + [jnp.zeros((8 - len(rows), d), F32)], axis=0)

    sm_g, sm_d, sm_m, sm_v = small_allreduce_adam(
        pack([dg_pre, dg_post, dg_fpre, dg_fpost, db_pad, loss_part]),
        pack([norm_mix_pre, norm_mix_post, norm_ffn_pre, norm_ffn_post, b_forget]),
        pack([m_norm_mix_pre, m_norm_mix_post, m_norm_ffn_pre, m_norm_ffn_post, m_b_forget]),
        pack([v_norm_mix_pre, v_norm_mix_post, v_norm_ffn_pre, v_norm_ffn_post, v_b_forget]), after=upd_in_t[1])

    def small(a):
        return [a[0:1], a[1:2], a[2:3], a[3:4], a[4:5, :n_heads]]

    def ordered(sm, bg):
        return [sm[0], sm[1], bg[0], sm[4], bg[1], bg[2], bg[3], sm[2], sm[3], bg[4], bg[5], bg[6]]

    return (sm_g[5, 0], dx[None], *ordered(small(sm_g), grads), *ordered(small(sm_d), deltas),
            *ordered(small(sm_m), new_ms), *ordered(small(sm_v), new_vs))
```
